```python
import math
import jax, jax.numpy as jnp
from jax import lax
import numpy as np

D_MODEL = 1024
BATCH = 16
SEQ = 2048
DEPTH = 2

D_MIX = D_MODEL
D_ATTN = D_MIX // 2
D_CONV = D_MIX - D_ATTN
HEAD_DIM = 64
N_HEADS = D_ATTN // HEAD_DIM
CONV_GROUP = 64
N_CONV_GROUPS = D_CONV // CONV_GROUP
DILATED_CONFIGS = ((128, 1), (512, 4), (2048, 16))
BAND_BLOCK = 128
CONV_WIDTH = 3
FFN_CONV_WIDTH = 3
D_FF = 2816
EPS = 1e-6

kernel_name = "hybrid_dilated_attn_shortconv_convffn"


def rmsnorm(x, g):
    xf = x.astype(jnp.float32)
    y = xf * lax.rsqrt(jnp.mean(xf * xf, axis=-1, keepdims=True) + EPS)
    return (y * g.astype(jnp.float32)).astype(x.dtype)


def group_rmsnorm(x, g, n_groups):
    shp = x.shape
    xg = x.reshape(*shp[:-1], n_groups, shp[-1] // n_groups)
    return rmsnorm(xg, g.reshape(n_groups, -1)).reshape(shp)


def causal_dwconv(u, w):
    K = w.shape[0]
    S = u.shape[1]
    up = jnp.pad(u, ((0, 0), (K - 1, 0), (0, 0)))
    return sum(up[:, k:k + S] * w[k].astype(u.dtype) for k in range(K))


def alibi_slopes(n):
    return 2.0 ** (-8.0 * jnp.arange(1, n + 1, dtype=jnp.float32) / n)


def dilated_branch(q, k, v, slopes, window, dilation):
    B, H, S, hd = q.shape
    span = window // dilation
    L = S // dilation
    nb = -(-L // BAND_BLOCK)
    Lp = nb * BAND_BLOCK

    def to_blocks(t):
        t = t.reshape(B, H, L, dilation, hd).transpose(0, 1, 3, 2, 4)
        t = jnp.pad(t, ((0, 0), (0, 0), (0, 0), (0, Lp - L), (0, 0)))
        return t.reshape(B, H, dilation, nb, BAND_BLOCK, hd)

    def with_prev(t):
        prev = jnp.pad(t, ((0, 0), (0, 0), (0, 0), (1, 0), (0, 0), (0, 0)))[:, :, :, :-1]
        return jnp.concatenate([prev, t], axis=-2)

    qb, kb, vb = to_blocks(q), to_blocks(k), to_blocks(v)
    kk, vv = with_prev(kb), with_prev(vb)
    s = jnp.einsum('bhrnqd,bhrnkd->bhrnqk', qb, kk)
    i = jnp.arange(BAND_BLOCK)[:, None]
    j = jnp.arange(2 * BAND_BLOCK)[None, :]
    dist = BAND_BLOCK + i - j
    blk = jnp.arange(nb)[:, None, None]
    valid = (dist >= 0) & (dist <= span) & ((blk > 0) | (j >= BAND_BLOCK))
    bias = -(slopes * dilation).reshape(1, H, 1, 1, 1, 1) * dist.astype(jnp.float32)
    s = jnp.where(valid, s + bias, -jnp.inf)
    m = jnp.max(s, axis=-1)
    p = jnp.exp(s - m[..., None])
    l = jnp.sum(p, axis=-1)
    o = jnp.einsum('bhrnqk,bhrnkd->bhrnqd', p, vv)

    def from_blocks(t):
        rest = t.shape[5:]
        t = t.reshape(B, H, dilation, Lp, *rest)[:, :, :, :L]
        t = jnp.moveaxis(t, 2, 3)
        return t.reshape(B, H, S, *rest)

    return from_blocks(o), from_blocks(m), from_blocks(l)


def dilated_attention(q, k, v):
    slopes = alibi_slopes(q.shape[1])
    outs = [dilated_branch(q, k, v, slopes, w, d) for (w, d) in DILATED_CONFIGS]
    m_max = jnp.max(jnp.stack([m for (_, m, _) in outs]), axis=0)
    num = sum(o * jnp.exp(m - m_max)[..., None] for (o, m, _) in outs)
    den = sum(l * jnp.exp(m - m_max) for (_, m, l) in outs)
    return num / den[..., None]


def hybrid_mixer(h, w_in, mix_conv_w, attn_out_g, conv_out_g, w_out):
    B, S, _ = h.shape
    proj = h @ w_in
    q, k, v, gate_b, gate_c, u = jnp.split(
        proj, [D_ATTN, 2 * D_ATTN, 3 * D_ATTN, 3 * D_ATTN + D_CONV, 3 * D_ATTN + 2 * D_CONV], axis=-1)

    def heads(t):
        return t.reshape(B, S, N_HEADS, HEAD_DIM).transpose(0, 2, 1, 3).astype(jnp.float32)

    attn = dilated_attention(heads(q) * (HEAD_DIM ** -0.5), heads(k), heads(v))
    attn = attn.transpose(0, 2, 1, 3).reshape(B, S, D_ATTN).astype(h.dtype)
    attn = group_rmsnorm(attn, attn_out_g, N_HEADS)

    y = gate_b * causal_dwconv(gate_c * u, mix_conv_w)
    y = group_rmsnorm(y, conv_out_g, N_CONV_GROUPS)

    return jnp.concatenate([attn, y], axis=-1) @ w_out


def conv_glu_ffn(h, ffn_up, ffn_conv_w, ffn_down):
    up = causal_dwconv(h @ ffn_up, ffn_conv_w)
    gate, val = jnp.split(up, 2, axis=-1)
    return (jax.nn.silu(gate) * val) @ ffn_down


def _fwd_setup_inputs(seed: int = 0) -> dict:
    key = jax.random.key(seed)
    ks = jax.random.split(key, 12)
    f32 = jnp.float32
    n = jax.random.normal
    d_in = 3 * D_ATTN + 3 * D_CONV
    return {
        "x": n(ks[0], (BATCH, SEQ, D_MODEL), f32),
        "norm1_g": 1.0 + 0.02 * n(ks[1], (DEPTH, D_MODEL), f32),
        "w_in": n(ks[2], (DEPTH, D_MODEL, d_in), f32) * D_MODEL ** -0.5,
        "mix_conv_w": n(ks[3], (DEPTH, CONV_WIDTH, D_CONV), f32) * CONV_WIDTH ** -0.5,
        "attn_out_g": 1.0 + 0.02 * n(ks[4], (DEPTH, D_ATTN), f32),
        "conv_out_g": 1.0 + 0.02 * n(ks[5], (DEPTH, D_CONV), f32),
        "w_out": n(ks[6], (DEPTH, D_MIX, D_MODEL), f32) * D_MIX ** -0.5,
        "norm2_g": 1.0 + 0.02 * n(ks[7], (DEPTH, D_MODEL), f32),
        "ffn_up": n(ks[8], (DEPTH, D_MODEL, 2 * D_FF), f32) * D_MODEL ** -0.5,
        "ffn_conv_w": n(ks[9], (DEPTH, FFN_CONV_WIDTH, 2 * D_FF), f32) * FFN_CONV_WIDTH ** -0.5,
        "ffn_down": n(ks[10], (DEPTH, D_FF, D_MODEL), f32) * D_FF ** -0.5,
        "final_norm_g": 1.0 + 0.02 * n(ks[11], (D_MODEL,), f32),
    }


def _fwd_reference(x, norm1_g, w_in, mix_conv_w, attn_out_g, conv_out_g, w_out,
              norm2_g, ffn_up, ffn_conv_w, ffn_down, final_norm_g):
    for layer in range(DEPTH):
        h = rmsnorm(x, norm1_g[layer])
        x = x + hybrid_mixer(h, w_in[layer], mix_conv_w[layer], attn_out_g[layer],
                             conv_out_g[layer], w_out[layer])
        h = rmsnorm(x, norm2_g[layer])
        x = x + conv_glu_ffn(h, ffn_up[layer], ffn_conv_w[layer], ffn_down[layer])
    return rmsnorm(x, final_norm_g)


import jax as _jax
import jax.numpy as _jnp

TWIN_FORMAT = 'train_step'
FWD_PARAMS = ['x', 'norm1_g', 'w_in', 'mix_conv_w', 'attn_out_g', 'conv_out_g', 'w_out', 'norm2_g', 'ffn_up', 'ffn_conv_w', 'ffn_down', 'final_norm_g']
TWIN_WEIGHTS = ['norm1_g', 'w_in', 'mix_conv_w', 'attn_out_g', 'conv_out_g', 'w_out', 'norm2_g', 'ffn_up', 'ffn_conv_w', 'ffn_down', 'final_norm_g']
TWIN_DIFF_INPUT = 'x'
TWIN_INPUTS = ['x', 'norm1_g', 'w_in', 'mix_conv_w', 'attn_out_g', 'conv_out_g', 'w_out', 'norm2_g', 'ffn_up', 'ffn_conv_w', 'ffn_down', 'final_norm_g', 'loss_target', 'm_norm1_g', 'm_w_in', 'm_mix_conv_w', 'm_attn_out_g', 'm_conv_out_g', 'm_w_out', 'm_norm2_g', 'm_ffn_up', 'm_ffn_conv_w', 'm_ffn_down', 'm_final_norm_g', 'v_norm1_g', 'v_w_in', 'v_mix_conv_w', 'v_attn_out_g', 'v_conv_out_g', 'v_w_out', 'v_norm2_g', 'v_ffn_up', 'v_ffn_conv_w', 'v_ffn_down', 'v_final_norm_g']
TWIN_OUTPUTS = ['loss', 'grad_x', 'grad_norm1_g', 'grad_w_in', 'grad_mix_conv_w', 'grad_attn_out_g', 'grad_conv_out_g', 'grad_w_out', 'grad_norm2_g', 'grad_ffn_up', 'grad_ffn_conv_w', 'grad_ffn_down', 'grad_final_norm_g', 'delta_norm1_g', 'delta_w_in', 'delta_mix_conv_w', 'delta_attn_out_g', 'delta_conv_out_g', 'delta_w_out', 'delta_norm2_g', 'delta_ffn_up', 'delta_ffn_conv_w', 'delta_ffn_down', 'delta_final_norm_g', 'new_m_norm1_g', 'new_m_w_in', 'new_m_mix_conv_w', 'new_m_attn_out_g', 'new_m_conv_out_g', 'new_m_w_out', 'new_m_norm2_g', 'new_m_ffn_up', 'new_m_ffn_conv_w', 'new_m_ffn_down', 'new_m_final_norm_g', 'new_v_norm1_g', 'new_v_w_in', 'new_v_mix_conv_w', 'new_v_attn_out_g', 'new_v_conv_out_g', 'new_v_w_out', 'new_v_norm2_g', 'new_v_ffn_up', 'new_v_ffn_conv_w', 'new_v_ffn_down', 'new_v_final_norm_g']
TWIN_LEAF_KINDS = {'loss': 'loss', 'grad_x': 'grad_x', 'grad_norm1_g': 'grad_w', 'grad_w_in': 'grad_w', 'grad_mix_conv_w': 'grad_w', 'grad_attn_out_g': 'grad_w', 'grad_conv_out_g': 'grad_w', 'grad_w_out': 'grad_w', 'grad_norm2_g': 'grad_w', 'grad_ffn_up': 'grad_w', 'grad_ffn_conv_w': 'grad_w', 'grad_ffn_down': 'grad_w', 'grad_final_norm_g': 'grad_w', 'delta_norm1_g': 'delta_w', 'delta_w_in': 'delta_w', 'delta_mix_conv_w': 'delta_w', 'delta_attn_out_g': 'delta_w', 'delta_conv_out_g': 'delta_w', 'delta_w_out': 'delta_w', 'delta_norm2_g': 'delta_w', 'delta_ffn_up': 'delta_w', 'delta_ffn_conv_w': 'delta_w', 'delta_ffn_down': 'delta_w', 'delta_final_norm_g': 'delta_w', 'new_m_norm1_g': 'new_m', 'new_m_w_in': 'new_m', 'new_m_mix_conv_w': 'new_m', 'new_m_attn_out_g': 'new_m', 'new_m_conv_out_g': 'new_m', 'new_m_w_out': 'new_m', 'new_m_norm2_g': 'new_m', 'new_m_ffn_up': 'new_m', 'new_m_ffn_conv_w': 'new_m', 'new_m_ffn_down': 'new_m', 'new_m_final_norm_g': 'new_m', 'new_v_norm1_g': 'new_v', 'new_v_w_in': 'new_v', 'new_v_mix_conv_w': 'new_v', 'new_v_attn_out_g': 'new_v', 'new_v_conv_out_g': 'new_v', 'new_v_w_out': 'new_v', 'new_v_norm2_g': 'new_v', 'new_v_ffn_up': 'new_v', 'new_v_ffn_conv_w': 'new_v', 'new_v_ffn_down': 'new_v', 'new_v_final_norm_g': 'new_v'}


def _forward(args):
    return _fwd_reference(*[args[k] for k in FWD_PARAMS])


def _output_shape():
    out = _jax.eval_shape(lambda: _forward(_fwd_setup_inputs(0)))
    return out.shape, out.dtype

N_MICROBATCH = 1
ADAM_LR = 0.001
ADAM_B1 = 0.9
ADAM_B2 = 0.999
ADAM_EPS = 1e-08
ADAM_WD = 0.01
ADAM_STEP = 10
PER_EXAMPLE_BATCH_AXIS = {'x': 0, 'loss_target': 0}
SHARED_INPUTS = []
_WEIGHT_DTYPES = {'norm1_g': _jnp.float32, 'w_in': _jnp.float32, 'mix_conv_w': _jnp.float32, 'attn_out_g': _jnp.float32, 'conv_out_g': _jnp.float32, 'w_out': _jnp.float32, 'norm2_g': _jnp.float32, 'ffn_up': _jnp.float32, 'ffn_conv_w': _jnp.float32, 'ffn_down': _jnp.float32, 'final_norm_g': _jnp.float32}
MOMENT_SCALE = {'norm1_g': 2.332389e-01, 'w_in': 1.349766e-01, 'mix_conv_w': 1.528080e-01, 'attn_out_g': 1.481559e-01, 'conv_out_g': 1.424055e-01, 'w_out': 1.424379e-01, 'norm2_g': 1.061733e-01, 'ffn_up': 4.523846e-02, 'ffn_conv_w': 4.532266e-02, 'ffn_down': 7.412599e-02, 'final_norm_g': 3.198735e+01}


def _to_microbatches(a, axis):
    t = _jnp.moveaxis(a, axis, 0)
    t = t.reshape((N_MICROBATCH, t.shape[0] // N_MICROBATCH) + t.shape[1:])
    return _jnp.moveaxis(t, 1, axis + 1)


def setup_inputs(seed: int = 0) -> dict:
    inp = _fwd_setup_inputs(seed)
    key = _jax.random.fold_in(_jax.random.key(seed), 7919)
    shape, _ = _output_shape()
    out = dict(inp)
    out["loss_target"] = _jax.random.normal(_jax.random.fold_in(key, 0), shape, _jnp.float32)
    for i, name in enumerate(TWIN_WEIGHTS):
        w = inp[name].astype(_jnp.float32)
        if MOMENT_SCALE is None:
            s = _jnp.sqrt(_jnp.mean(_jnp.square(w)) + 1e-30)
        else:
            s = MOMENT_SCALE[name]
        km, kv = _jax.random.split(_jax.random.fold_in(key, i + 1))
        out[name] = w
        out["m_" + name] = s * _jax.random.normal(km, w.shape, _jnp.float32)
        out["v_" + name] = (s * s) * _jax.random.uniform(kv, w.shape, _jnp.float32, 0.5, 1.5)
    if N_MICROBATCH > 1:
        for name, axis in PER_EXAMPLE_BATCH_AXIS.items():
            out[name] = _to_microbatches(out[name], axis)
    return {'x': out['x'], 'norm1_g': out['norm1_g'], 'w_in': out['w_in'], 'mix_conv_w': out['mix_conv_w'], 'attn_out_g': out['attn_out_g'], 'conv_out_g': out['conv_out_g'], 'w_out': out['w_out'], 'norm2_g': out['norm2_g'], 'ffn_up': out['ffn_up'], 'ffn_conv_w': out['ffn_conv_w'], 'ffn_down': out['ffn_down'], 'final_norm_g': out['final_norm_g'], 'loss_target': out['loss_target'], 'm_norm1_g': out['m_norm1_g'], 'm_w_in': out['m_w_in'], 'm_mix_conv_w': out['m_mix_conv_w'], 'm_attn_out_g': out['m_attn_out_g'], 'm_conv_out_g': out['m_conv_out_g'], 'm_w_out': out['m_w_out'], 'm_norm2_g': out['m_norm2_g'], 'm_ffn_up': out['m_ffn_up'], 'm_ffn_conv_w': out['m_ffn_conv_w'], 'm_ffn_down': out['m_ffn_down'], 'm_final_norm_g': out['m_final_norm_g'], 'v_norm1_g': out['v_norm1_g'], 'v_w_in': out['v_w_in'], 'v_mix_conv_w': out['v_mix_conv_w'], 'v_attn_out_g': out['v_attn_out_g'], 'v_conv_out_g': out['v_conv_out_g'], 'v_w_out': out['v_w_out'], 'v_norm2_g': out['v_norm2_g'], 'v_ffn_up': out['v_ffn_up'], 'v_ffn_conv_w': out['v_ffn_conv_w'], 'v_ffn_down': out['v_ffn_down'], 'v_final_norm_g': out['v_final_norm_g']}


def _loss(weights, diff, rest, loss_target):
    with _jax.named_scope("forward"):
        args = {**rest, TWIN_DIFF_INPUT: diff, **{k: w.astype(_WEIGHT_DTYPES[k]) for k, w in weights.items()}}
        y = _forward(args)
    with _jax.named_scope("loss_head"):
        err = _jnp.square(y.astype(_jnp.float32) - loss_target)
        return 0.5 * _jnp.sum(_jnp.mean(err, axis=-1)) if err.ndim else 0.5 * err


def _adamw(w, g, m, v):
    m = ADAM_B1 * m + (1.0 - ADAM_B1) * g
    v = ADAM_B2 * v + (1.0 - ADAM_B2) * _jnp.square(g)
    m_hat = m / (1.0 - ADAM_B1 ** ADAM_STEP)
    v_hat = v / (1.0 - ADAM_B2 ** ADAM_STEP)
    delta = -ADAM_LR * (m_hat / (_jnp.sqrt(v_hat) + ADAM_EPS) + ADAM_WD * w)
    return delta, m, v


def reference(x, norm1_g, w_in, mix_conv_w, attn_out_g, conv_out_g, w_out, norm2_g, ffn_up, ffn_conv_w, ffn_down, final_norm_g, loss_target, m_norm1_g, m_w_in, m_mix_conv_w, m_attn_out_g, m_conv_out_g, m_w_out, m_norm2_g, m_ffn_up, m_ffn_conv_w, m_ffn_down, m_final_norm_g, v_norm1_g, v_w_in, v_mix_conv_w, v_attn_out_g, v_conv_out_g, v_w_out, v_norm2_g, v_ffn_up, v_ffn_conv_w, v_ffn_down, v_final_norm_g):
    given = dict(x=x, norm1_g=norm1_g, w_in=w_in, mix_conv_w=mix_conv_w, attn_out_g=attn_out_g, conv_out_g=conv_out_g, w_out=w_out, norm2_g=norm2_g, ffn_up=ffn_up, ffn_conv_w=ffn_conv_w, ffn_down=ffn_down, final_norm_g=final_norm_g, loss_target=loss_target, m_norm1_g=m_norm1_g, m_w_in=m_w_in, m_mix_conv_w=m_mix_conv_w, m_attn_out_g=m_attn_out_g, m_conv_out_g=m_conv_out_g, m_w_out=m_w_out, m_norm2_g=m_norm2_g, m_ffn_up=m_ffn_up, m_ffn_conv_w=m_ffn_conv_w, m_ffn_down=m_ffn_down, m_final_norm_g=m_final_norm_g, v_norm1_g=v_norm1_g, v_w_in=v_w_in, v_mix_conv_w=v_mix_conv_w, v_attn_out_g=v_attn_out_g, v_conv_out_g=v_conv_out_g, v_w_out=v_w_out, v_norm2_g=v_norm2_g, v_ffn_up=v_ffn_up, v_ffn_conv_w=v_ffn_conv_w, v_ffn_down=v_ffn_down, v_final_norm_g=v_final_norm_g)
    weights = {n: given[n] for n in TWIN_WEIGHTS}
    shared = {n: given[n] for n in SHARED_INPUTS}
    per_example = {n: given[n] for n in ['x']}
    grad_fn = _jax.value_and_grad(_loss, argnums=(0, 1))

    def one_microbatch(ex, loss_target):
        ex = dict(ex)
        diff = ex.pop(TWIN_DIFF_INPUT)
        return grad_fn(weights, diff, {**shared, **ex}, loss_target)

    if N_MICROBATCH == 1:
        loss, (grad_w, grad_x) = one_microbatch(per_example, given["loss_target"])
    else:
        def body(carry, xs):
            loss_sum, grad_sum = carry
            l_k, (gw_k, gx_k) = one_microbatch(xs[0], xs[1])
            with _jax.named_scope("update"):
                return (loss_sum + l_k, _jax.tree.map(_jnp.add, grad_sum, gw_k)), gx_k

        init = (_jnp.zeros((), _jnp.float32), _jax.tree.map(_jnp.zeros_like, weights))
        (loss, grad_w), grad_x = _jax.lax.scan(body, init, (per_example, given["loss_target"]))
    with _jax.named_scope("update"):
        delta_w, new_m, new_v = {}, {}, {}
        for n in TWIN_WEIGHTS:
            delta_w[n], new_m[n], new_v[n] = _adamw(weights[n], grad_w[n], given["m_" + n], given["v_" + n])
    return (loss, grad_x, *[grad_w[n] for n in TWIN_WEIGHTS], *[delta_w[n] for n in TWIN_WEIGHTS],
            *[new_m[n] for n in TWIN_WEIGHTS], *[new_v[n] for n in TWIN_WEIGHTS])
```

```python
import math

import jax
import jax.numpy as jnp
from jax import lax
from jax.experimental import pallas as pl
from jax.experimental.pallas import tpu as pltpu

F32 = jnp.float32
BF16 = jnp.bfloat16

D_MODEL = 1024
D_ATTN = 512
D_CONV = 512
HEAD_DIM = 64
N_HEADS = 8
D_FF = 2816
DEPTH = 2
D_IN = 3 * D_ATTN + 3 * D_CONV
EPS = 1e-6
DILATIONS = (1, 4, 16)
BAND = 128
N_DEV = 8
IN_CHUNK = D_IN // N_DEV
UP_CHUNK = 2 * D_FF // N_DEV
N_UP_PAIRS = N_DEV // 2
CW_PACK = UP_CHUNK + D_CONV // N_DEV
ADAM_LR = 0.001
ADAM_B1 = 0.9
ADAM_B2 = 0.999
ADAM_EPS = 1e-08
ADAM_WD = 0.01
ADAM_STEP = 10
LANES = 128
SUBLANES = 8
VMEM_LIMIT = 56 * 1024 * 1024

NEG = -1e30
MESH = pl.DeviceIdType.MESH


def _params(sem=None, vmem=VMEM_LIMIT):
    return pltpu.CompilerParams(dimension_semantics=sem, vmem_limit_bytes=vmem)


NN = (((1,), (0,)), ((), ()))
NT = (((1,), (1,)), ((), ()))
TN = (((0,), (0,)), ((), ()))


def _matmul(a, b, *, grid, a_spec, b_spec, o_spec, o_shape, o_dtype, dims, name, res=None, res_spec=None):
    nk = grid[2]
    o_block = tuple(s for s in o_spec.block_shape if s is not None)

    def body(*refs):
        if res is None:
            a_ref, b_ref, o_ref, *scr = refs
            r_ref = None
        else:
            a_ref, b_ref, r_ref, o_ref, *scr = refs
        part = lax.dot_general(a_ref[...].astype(BF16), b_ref[...].astype(BF16), dims, preferred_element_type=F32)

        def finish(total):
            if r_ref is not None:
                total = total + r_ref[...]
            o_ref[...] = total.astype(o_dtype)

        if nk == 1:
            finish(part)
        else:
            acc = scr[0]
            k = pl.program_id(2)

            @pl.when(k == 0)
            def _():
                acc[...] = part

            @pl.when(k > 0)
            def _():
                acc[...] += part

            @pl.when(k == nk - 1)
            def _():
                finish(acc[...])

    in_specs = [a_spec, b_spec] + ([res_spec] if res is not None else [])
    args = (a, b) + ((res,) if res is not None else ())
    return pl.pallas_call(
        body, name=name, grid=grid, in_specs=in_specs, out_specs=o_spec,
        out_shape=jax.ShapeDtypeStruct(o_shape, o_dtype),
        scratch_shapes=[pltpu.VMEM(o_block, F32)] if nk > 1 else [],
        compiler_params=_params(("parallel", "parallel", "arbitrary")),
    )(*args)


ROW_TILE = 512


def _rms_fwd(x, g, name):
    t, d = x.shape

    def body(x_ref, g_ref, h_ref):
        xv = x_ref[...]
        r = lax.rsqrt(jnp.mean(xv * xv, axis=-1, keepdims=True) + EPS)
        h_ref[...] = (xv * r * g_ref[...]).astype(BF16)

    return pl.pallas_call(
        body, name=name, grid=(t // ROW_TILE,),
        in_specs=[pl.BlockSpec((ROW_TILE, d), lambda i: (i, 0)), pl.BlockSpec((1, d), lambda i: (0, 0))],
        out_specs=pl.BlockSpec((ROW_TILE, d), lambda i: (i, 0)),
        out_shape=jax.ShapeDtypeStruct((t, d), BF16),
        compiler_params=_params(("parallel",)),
    )(x, g)


def _rms_bwd(x, g, dh, dres, name):
    t, d = x.shape

    def body(x_ref, g_ref, dh_ref, dres_ref, dx_ref, dxb_ref, dg_ref):
        xv = x_ref[...]
        r = lax.rsqrt(jnp.mean(xv * xv, axis=-1, keepdims=True) + EPS)
        xh = xv * r
        dhv = dh_ref[...]
        gd = dhv * g_ref[...]
        dx = r * (gd - xh * jnp.mean(gd * xh, axis=-1, keepdims=True)) + dres_ref[...]
        dx_ref[...] = dx
        dxb_ref[...] = dx.astype(BF16)
        part = jnp.sum(dhv * xh, axis=0, keepdims=True)

        @pl.when(pl.program_id(0) == 0)
        def _():
            dg_ref[...] = part

        @pl.when(pl.program_id(0) > 0)
        def _():
            dg_ref[...] += part

    row = pl.BlockSpec((ROW_TILE, d), lambda i: (i, 0))
    vec = pl.BlockSpec((1, d), lambda i: (0, 0))
    return pl.pallas_call(
        body, name=name, grid=(t // ROW_TILE,),
        in_specs=[row, vec, row, row], out_specs=[row, row, vec],
        out_shape=[jax.ShapeDtypeStruct((t, d), F32), jax.ShapeDtypeStruct((t, d), BF16),
                   jax.ShapeDtypeStruct((1, d), F32)],
        compiler_params=_params(("arbitrary",)),
    )(x, g, dh, dres)


def _loss_head(x, g, target, name):
    t, d = x.shape

    def body(x_ref, g_ref, t_ref, loss_ref, dx_ref, dxb_ref, dg_ref):
        xv = x_ref[...]
        r = lax.rsqrt(jnp.mean(xv * xv, axis=-1, keepdims=True) + EPS)
        xh = xv * r
        gv = g_ref[...]
        err = xh * gv - t_ref[...]
        loss = jnp.full((1, LANES), 0.5 / d, F32) * jnp.sum(err * err)
        dy = err * (1.0 / d)
        gd = dy * gv
        dx = r * (gd - xh * jnp.mean(gd * xh, axis=-1, keepdims=True))
        dx_ref[...] = dx
        dxb_ref[...] = dx.astype(BF16)
        part = jnp.sum(dy * xh, axis=0, keepdims=True)

        @pl.when(pl.program_id(0) == 0)
        def _():
            dg_ref[...] = part
            loss_ref[...] = loss

        @pl.when(pl.program_id(0) > 0)
        def _():
            dg_ref[...] += part
            loss_ref[...] += loss

    row = pl.BlockSpec((ROW_TILE, d), lambda i: (i, 0))
    vec = pl.BlockSpec((1, d), lambda i: (0, 0))
    return pl.pallas_call(
        body, name=name, grid=(t // ROW_TILE,),
        in_specs=[row, vec, row],
        out_specs=[pl.BlockSpec((1, LANES), lambda i: (0, 0)), row, row, vec],
        out_shape=[jax.ShapeDtypeStruct((1, LANES), F32), jax.ShapeDtypeStruct((t, d), F32),
                   jax.ShapeDtypeStruct((t, d), BF16), jax.ShapeDtypeStruct((1, d), F32)],
        compiler_params=_params(("arbitrary",)),
    )(x, g, target)


def _group_matrix(n):
    shift = int(math.log2(HEAD_DIM))
    r = lax.broadcasted_iota(jnp.int32, (n, n), 0) >> shift
    c = lax.broadcasted_iota(jnp.int32, (n, n), 1) >> shift
    return (r == c).astype(BF16)


def _group_sum(v, gmat):
    hi = v.astype(BF16)
    rest = v - hi.astype(F32)
    mid = rest.astype(BF16)
    lo = (rest - mid.astype(F32)).astype(BF16)

    def dot(p):
        return jnp.dot(p, gmat, preferred_element_type=F32)

    return dot(hi) + dot(mid) + dot(lo)


def _shift_rows(ext, k):
    return pltpu.roll(ext, k % ext.shape[0], 0)


def _store_columns(stage, out_hbm, sems, row0, nrows, col_blocks):
    rows = pl.ds(pl.multiple_of(row0, SUBLANES * 2), nrows)
    copies = [
        pltpu.make_async_copy(stage.at[i], out_hbm.at[rows, pl.ds(pl.multiple_of(cb * LANES, LANES), LANES)], sems.at[i])
        for i, cb in enumerate(col_blocks)
    ]
    for cp in copies:
        cp.start()
    for cp in copies:
        cp.wait()


def _attn_consts(width):
    i = lax.broadcasted_iota(jnp.int32, (BAND, width), 0)
    j = lax.broadcasted_iota(jnp.int32, (BAND, width), 1)
    dist = (width - BAND) + i - j
    inwin = (dist >= 0) & (dist <= BAND)
    return dist.astype(F32), inwin, j


def _head_masks():
    lane = lax.broadcasted_iota(jnp.int32, (1, LANES), 1)
    return [(lane < HEAD_DIM).astype(F32), (lane >= HEAD_DIM).astype(F32)]


def _permute_in(src_ref, dst_ref, dil, seq):
    length = seq // dil
    for r in range(dil):
        dst_ref[pl.ds(r * length, length), :] = src_ref[pl.ds(r, length, stride=dil), :].astype(dst_ref.dtype)


def _slopes_table():
    slopes = 2.0 ** (-8.0 * jnp.arange(1, N_HEADS + 1, dtype=F32) / N_HEADS)
    return jnp.broadcast_to(slopes[:, None], (N_HEADS, 2 * BAND))


def _attn_fwd(proj, attn_g, nbatch, seq):
    t = nbatch * seq
    nblk = seq // BAND
    scale = HEAD_DIM ** -0.5

    def body(q_ref, k_ref, v_ref, g_ref, sl_ref, o_ref, lse_ref, cat_ref, pq, pk, pv, po, pm, pll, ao, am, al):
        hp = pl.program_id(1)
        hmask = _head_masks()
        slope = [sl_ref[pl.ds(2 * hp + hh, 1), :] for hh in range(2)]

        def run_branch(dil, qs, ks, vs, osink, msink, lsink):
            nb = seq // dil // BAND
            width = 2 * BAND if nb > 1 else BAND
            distf, inwin, jcol = _attn_consts(width)
            bias = [distf * (slope[hh][:, :width] * (-float(dil))) for hh in range(2)]

            def blk(m, carry):
                row0 = pl.multiple_of(m * BAND, BAND)
                q = qs[pl.ds(row0, BAND), :] * scale
                if nb > 1:
                    prow = pl.multiple_of(jnp.maximum(m - 1, 0) * BAND, BAND)
                    kk = jnp.concatenate([ks[pl.ds(prow, BAND), :], ks[pl.ds(row0, BAND), :]], axis=0)
                    vv = jnp.concatenate([vs[pl.ds(prow, BAND), :], vs[pl.ds(row0, BAND), :]], axis=0)
                    valid = inwin & (jcol >= jnp.where((m % nb) == 0, BAND, 0))
                else:
                    kk = ks[pl.ds(row0, BAND), :]
                    vv = vs[pl.ds(row0, BAND), :]
                    valid = inwin
                kb = kk.astype(BF16)
                o = jnp.zeros((BAND, LANES), F32)
                mfull = jnp.zeros((BAND, LANES), F32)
                lfull = jnp.zeros((BAND, LANES), F32)
                for hh in range(2):
                    qh = (q * hmask[hh]).astype(BF16)
                    s = lax.dot_general(qh, kb, NT, preferred_element_type=F32)
                    s = jnp.where(valid, s + bias[hh], NEG)
                    mh = jnp.max(s, axis=1, keepdims=True)
                    p = jnp.exp(s - mh)
                    lh = jnp.sum(p, axis=1, keepdims=True)
                    o = o + jnp.dot(p.astype(BF16), (vv * hmask[hh]).astype(BF16), preferred_element_type=F32)
                    mfull = mfull + mh * hmask[hh]
                    lfull = lfull + lh * hmask[hh]
                osink[pl.ds(row0, BAND), :] = o
                msink[pl.ds(row0, BAND), :] = mfull
                lsink[pl.ds(row0, BAND), :] = lfull
                return carry

            lax.fori_loop(0, nblk, blk, 0)

        run_branch(1, q_ref, k_ref, v_ref, ao, am, al)
        for dil in DILATIONS[1:]:
            length = seq // dil
            _permute_in(q_ref, pq, dil, seq)
            _permute_in(k_ref, pk, dil, seq)
            _permute_in(v_ref, pv, dil, seq)
            run_branch(dil, pq, pk, pv, po, pm, pll)
            for r in range(dil):
                nat = pl.ds(r, length, stride=dil)
                per = pl.ds(r * length, length)
                m0 = am[nat, :]
                mb = pm[per, :]
                mn = jnp.maximum(m0, mb)
                e0 = jnp.exp(m0 - mn)
                eb = jnp.exp(mb - mn)
                ao[nat, :] = ao[nat, :] * e0 + po[per, :] * eb
                al[nat, :] = al[nat, :] * e0 + pll[per, :] * eb
                am[nat, :] = mn

        gmat = _group_matrix(LANES)
        gv = g_ref[...]

        def fin(c, carry):
            rows = pl.ds(pl.multiple_of(c * 256, 256), 256)
            lv = al[rows, :]
            o = ao[rows, :] / lv
            o_ref[rows, :] = o
            lse_ref[rows, :] = am[rows, :] + jnp.log(lv)
            ms = _group_sum(o * o, gmat) * (1.0 / HEAD_DIM)
            cat_ref[rows, :] = (o * lax.rsqrt(ms + EPS) * gv).astype(BF16)
            return carry

        lax.fori_loop(0, seq // 256, fin, 0)

    nq = D_ATTN // LANES
    blk = lambda off: pl.BlockSpec((seq, LANES), lambda b, h: (b, h + off))
    scratch = [pltpu.VMEM((seq, LANES), F32) for _ in range(9)]
    return pl.pallas_call(
        body, name="attn_fwd", grid=(nbatch, nq),
        in_specs=[blk(0), blk(nq), blk(2 * nq), pl.BlockSpec((1, LANES), lambda b, h: (0, h)),
                  pl.BlockSpec((N_HEADS, 2 * BAND), lambda b, h: (0, 0))],
        out_specs=[blk(0), blk(0), blk(0)],
        out_shape=[jax.ShapeDtypeStruct((t, D_ATTN), F32), jax.ShapeDtypeStruct((t, D_ATTN), F32),
                   jax.ShapeDtypeStruct((t, D_MODEL), BF16)],
        scratch_shapes=scratch,
        compiler_params=_params(("parallel", "parallel")),
    )(proj, proj, proj, attn_g, _slopes_table())


def _attn_bwd(proj, o, lse, d_cat, attn_g, nbatch, seq):
    t = nbatch * seq
    nblk = seq // BAND
    scale = HEAD_DIM ** -0.5

    def body(q_ref, k_ref, v_ref, o_ref, lse_ref, dy_ref, g_ref, sl_ref, dproj_ref, dg_ref,
             do_n, dl_n, dq_n, dk_n, dv_n, pq, pk, pv, pdo, plse, pdl, pdq, pdk, pdv, stage, sems):
        hp = pl.program_id(0)
        hmask = _head_masks()
        slope = [sl_ref[pl.ds(2 * hp + hh, 1), :] for hh in range(2)]
        gmat = _group_matrix(LANES)
        gv = g_ref[...]

        def prep(c, dg):
            rows = pl.ds(pl.multiple_of(c * 256, 256), 256)
            ov = o_ref[rows, :]
            dyn = dy_ref[rows, :].astype(F32)
            r = lax.rsqrt(_group_sum(ov * ov, gmat) * (1.0 / HEAD_DIM) + EPS)
            gd = dyn * gv
            oh = ov * r
            do = r * (gd - oh * (_group_sum(gd * oh, gmat) * (1.0 / HEAD_DIM)))
            do_n[rows, :] = do
            dl_n[rows, :] = _group_sum(do * ov, gmat)
            return dg + jnp.sum(dyn * oh, axis=0, keepdims=True)

        dg = lax.fori_loop(0, seq // 256, prep, jnp.zeros((1, LANES), F32))

        @pl.when(pl.program_id(1) == 0)
        def _():
            dg_ref[...] = dg

        @pl.when(pl.program_id(1) > 0)
        def _():
            dg_ref[...] += dg

        def clear(*refs):
            def step(c, carry):
                rows = pl.ds(pl.multiple_of(c * 256, 256), 256)
                for ref in refs:
                    ref[rows, :] = jnp.zeros((256, LANES), F32)
                return carry

            lax.fori_loop(0, seq // 256, step, 0)

        clear(dq_n, dk_n, dv_n)

        def run_branch(dil, qs, ks, vs, dos, lses, dls, dqs, dks, dvs):
            nb = seq // dil // BAND
            width = 2 * BAND if nb > 1 else BAND
            distf, inwin, jcol = _attn_consts(width)
            bias = [distf * (slope[hh][:, :width] * (-float(dil))) for hh in range(2)]

            def blk(m, carry):
                row0 = pl.multiple_of(m * BAND, BAND)
                cur = pl.ds(row0, BAND)
                q = qs[cur, :] * scale
                dov = dos[cur, :]
                lsev = lses[cur, :]
                dlv = dls[cur, :]
                if nb > 1:
                    prow = pl.multiple_of(jnp.maximum(m - 1, 0) * BAND, BAND)
                    prev = pl.ds(prow, BAND)
                    kk = jnp.concatenate([ks[prev, :], ks[cur, :]], axis=0)
                    vv = jnp.concatenate([vs[prev, :], vs[cur, :]], axis=0)
                    valid = inwin & (jcol >= jnp.where((m % nb) == 0, BAND, 0))
                else:
                    kk = ks[cur, :]
                    vv = vs[cur, :]
                    valid = inwin
                kb = kk.astype(BF16)
                vb = vv.astype(BF16)
                dq = jnp.zeros((BAND, LANES), F32)
                dkk = jnp.zeros((width, LANES), F32)
                dvv = jnp.zeros((width, LANES), F32)
                for hh in range(2):
                    c0 = hh * HEAD_DIM
                    qh = (q * hmask[hh]).astype(BF16)
                    doh = (dov * hmask[hh]).astype(BF16)
                    s = lax.dot_general(qh, kb, NT, preferred_element_type=F32) + bias[hh]
                    p = jnp.where(valid, jnp.exp(s - lsev[:, c0:c0 + 1]), 0.0)
                    dp = lax.dot_general(doh, vb, NT, preferred_element_type=F32)
                    ds = (p * (dp - dlv[:, c0:c0 + 1])).astype(BF16)
                    dq = dq + jnp.dot(ds, kb, preferred_element_type=F32) * hmask[hh]
                    dkk = dkk + lax.dot_general(ds, qh, TN, preferred_element_type=F32)
                    dvv = dvv + lax.dot_general(p.astype(BF16), doh, TN, preferred_element_type=F32)
                dqs[cur, :] += dq
                if nb > 1:
                    dks[prev, :] += dkk[:BAND]
                    dvs[prev, :] += dvv[:BAND]
                    dks[cur, :] += dkk[BAND:]
                    dvs[cur, :] += dvv[BAND:]
                else:
                    dks[cur, :] += dkk
                    dvs[cur, :] += dvv
                return carry

            lax.fori_loop(0, nblk, blk, 0)

        run_branch(1, q_ref, k_ref, v_ref, do_n, lse_ref, dl_n, dq_n, dk_n, dv_n)
        for dil in DILATIONS[1:]:
            length = seq // dil
            for src, dst in ((q_ref, pq), (k_ref, pk), (v_ref, pv), (do_n, pdo), (lse_ref, plse), (dl_n, pdl)):
                _permute_in(src, dst, dil, seq)
            clear(pdq, pdk, pdv)
            run_branch(dil, pq, pk, pv, pdo, plse, pdl, pdq, pdk, pdv)
            for r in range(dil):
                nat = pl.ds(r, length, stride=dil)
                per = pl.ds(r * length, length)
                dq_n[nat, :] += pdq[per, :]
                dk_n[nat, :] += pdk[per, :]
                dv_n[nat, :] += pdv[per, :]

        def emit(c, carry):
            rows = pl.ds(pl.multiple_of(c * 256, 256), 256)
            stage[0, rows, :] = (dq_n[rows, :] * scale).astype(BF16)
            stage[1, rows, :] = dk_n[rows, :].astype(BF16)
            stage[2, rows, :] = dv_n[rows, :].astype(BF16)
            return carry

        lax.fori_loop(0, seq // 256, emit, 0)
        _store_columns(stage, dproj_ref, sems, pl.program_id(1) * seq, seq, [hp, nq + hp, 2 * nq + hp])

    nq = D_ATTN // LANES
    blk = lambda off: pl.BlockSpec((seq, LANES), lambda h, b: (b, h + off))
    vec = pl.BlockSpec((1, LANES), lambda h, b: (0, h))
    scratch = [pltpu.VMEM((seq, LANES), F32) for _ in range(14)]
    scratch += [pltpu.VMEM((3, seq, LANES), BF16), pltpu.SemaphoreType.DMA((3,))]
    d_proj, dg = pl.pallas_call(
        body, name="attn_bwd", grid=(nq, nbatch),
        in_specs=[blk(0), blk(nq), blk(2 * nq), blk(0), blk(0), blk(0), vec,
                  pl.BlockSpec((N_HEADS, 2 * BAND), lambda h, b: (0, 0))],
        out_specs=[pl.BlockSpec(memory_space=pl.ANY), vec],
        out_shape=[jax.ShapeDtypeStruct((t, D_IN), BF16), jax.ShapeDtypeStruct((1, D_ATTN), F32)],
        scratch_shapes=scratch,
        compiler_params=_params(("arbitrary", "arbitrary")),
    )(proj, proj, proj, o, lse, d_cat, attn_g, _slopes_table())
    return d_proj, dg


HALO = 2 * SUBLANES


def _window(ref, c, rows, nchunks, after):
    row0 = pl.multiple_of(c * rows, rows)
    prev0 = pl.multiple_of(jnp.maximum(row0 - HALO, 0), HALO)
    parts = [ref[pl.ds(prev0, HALO), :].astype(F32) * (c > 0).astype(F32), ref[pl.ds(row0, rows), :].astype(F32)]
    if after:
        next0 = pl.multiple_of(jnp.minimum(row0 + rows, (nchunks - 1) * rows), HALO)
        parts.append(ref[pl.ds(next0, HALO), :].astype(F32) * (c < nchunks - 1).astype(F32))
    return jnp.concatenate(parts, axis=0)


def _conv(z, w):
    return w[0:1] * _shift_rows(z, 2) + w[1:2] * _shift_rows(z, 1) + w[2:3] * z


def _conv_t(dy, w):
    return w[2:3] * dy + w[1:2] * _shift_rows(dy, -1) + w[0:1] * _shift_rows(dy, -2)


def _conv_wgrad(dy, z, cur):
    return [jnp.sum((dy * _shift_rows(z, 2 - k))[cur], axis=0, keepdims=True) for k in range(3)]


MIX_ROWS = 256
GATE_B_BLOCK = 3 * D_ATTN // LANES
GATE_C_BLOCK = GATE_B_BLOCK + D_CONV // LANES
U_BLOCK = GATE_C_BLOCK + D_CONV // LANES


def _convmix_fwd(proj, cat, mcw, conv_g, nbatch, seq):
    nchunks = seq // MIX_ROWS

    def body(gb_ref, gc_ref, u_ref, w_ref, g_ref, cat_in, cat_ref):
        del cat_in
        gmat = _group_matrix(LANES)
        w = w_ref[...]
        gv = g_ref[...]

        def step(c, carry):
            cur = pl.ds(pl.multiple_of(c * MIX_ROWS, MIX_ROWS), MIX_ROWS)
            z = _window(gc_ref, c, MIX_ROWS, nchunks, False) * _window(u_ref, c, MIX_ROWS, nchunks, False)
            y = gb_ref[cur, :] * _conv(z, w)[HALO:]
            ms = _group_sum(y * y, gmat) * (1.0 / HEAD_DIM)
            cat_ref[cur, :] = (y * lax.rsqrt(ms + EPS) * gv).astype(BF16)
            return carry

        lax.fori_loop(0, nchunks, step, 0)

    nc = D_CONV // LANES
    blk = lambda off: pl.BlockSpec((seq, LANES), lambda b, j: (b, j + off))
    return pl.pallas_call(
        body, name="convmix_fwd", grid=(nbatch, nc),
        in_specs=[blk(GATE_B_BLOCK), blk(GATE_C_BLOCK), blk(U_BLOCK),
                  pl.BlockSpec((3, LANES), lambda b, j: (0, j)), pl.BlockSpec((1, LANES), lambda b, j: (0, j)),
                  pl.BlockSpec(memory_space=pl.ANY)],
        out_specs=blk(D_ATTN // LANES),
        out_shape=jax.ShapeDtypeStruct(cat.shape, cat.dtype),
        input_output_aliases={5: 0},
        compiler_params=_params(("parallel", "parallel")),
    )(proj, proj, proj, mcw, conv_g, cat)


def _convmix_bwd(proj, d_cat, d_proj, mcw, conv_g, nbatch, seq):
    nchunks = seq // MIX_ROWS

    def body(gb_ref, gc_ref, u_ref, dy_ref, w_ref, g_ref, dproj_in, dproj_ref, dw_ref, dg_ref, stage, sems):
        del dproj_in
        cb = pl.program_id(0)
        b = pl.program_id(1)
        gmat = _group_matrix(LANES)
        w = w_ref[...]
        gv = g_ref[...]
        cur = slice(HALO, HALO + MIX_ROWS)

        def step(c, carry):
            rows = pl.ds(pl.multiple_of(c * MIX_ROWS, MIX_ROWS), MIX_ROWS)
            gb = _window(gb_ref, c, MIX_ROWS, nchunks, True)
            gc = _window(gc_ref, c, MIX_ROWS, nchunks, True)
            u = _window(u_ref, c, MIX_ROWS, nchunks, True)
            dyn = _window(dy_ref, c, MIX_ROWS, nchunks, True)
            z = gc * u
            conv = _conv(z, w)
            y = gb * conv
            r = lax.rsqrt(_group_sum(y * y, gmat) * (1.0 / HEAD_DIM) + EPS)
            yh = y * r
            gd = dyn * gv
            dy = r * (gd - yh * (_group_sum(gd * yh, gmat) * (1.0 / HEAD_DIM)))
            dc = dy * gb
            dz = _conv_t(dc, w)
            stage[0, rows, :] = (dy * conv)[cur].astype(BF16)
            stage[1, rows, :] = (dz * u)[cur].astype(BF16)
            stage[2, rows, :] = (dz * gc)[cur].astype(BF16)
            dws = _conv_wgrad(dc, z, cur)
            dg = jnp.sum((dyn * yh)[cur], axis=0, keepdims=True)
            return tuple(a + d for a, d in zip(carry, dws + [dg]))

        zero = jnp.zeros((1, LANES), F32)
        dw0, dw1, dw2, dg = lax.fori_loop(0, nchunks, step, (zero, zero, zero, zero))

        @pl.when(b == 0)
        def _():
            dw_ref[0:1, :] = dw0
            dw_ref[1:2, :] = dw1
            dw_ref[2:3, :] = dw2
            dg_ref[...] = dg

        @pl.when(b > 0)
        def _():
            dw_ref[0:1, :] += dw0
            dw_ref[1:2, :] += dw1
            dw_ref[2:3, :] += dw2
            dg_ref[...] += dg

        _store_columns(stage, dproj_ref, sems, b * seq, seq, [GATE_B_BLOCK + cb, GATE_C_BLOCK + cb, U_BLOCK + cb])

    nc = D_CONV // LANES
    blk = lambda off: pl.BlockSpec((seq, LANES), lambda j, b: (b, j + off))
    return pl.pallas_call(
        body, name="convmix_bwd", grid=(nc, nbatch),
        in_specs=[blk(GATE_B_BLOCK), blk(GATE_C_BLOCK), blk(U_BLOCK), blk(D_ATTN // LANES),
                  pl.BlockSpec((3, LANES), lambda j, b: (0, j)), pl.BlockSpec((1, LANES), lambda j, b: (0, j)),
                  pl.BlockSpec(memory_space=pl.ANY)],
        out_specs=[pl.BlockSpec(memory_space=pl.ANY), pl.BlockSpec((3, LANES), lambda j, b: (0, j)),
                   pl.BlockSpec((1, LANES), lambda j, b: (0, j))],
        out_shape=[jax.ShapeDtypeStruct(d_proj.shape, d_proj.dtype), jax.ShapeDtypeStruct((3, D_CONV), F32),
                   jax.ShapeDtypeStruct((1, D_CONV), F32)],
        scratch_shapes=[pltpu.VMEM((3, seq, LANES), BF16), pltpu.SemaphoreType.DMA((3,))],
        input_output_aliases={6: 0},
        compiler_params=_params(("arbitrary", "arbitrary")),
    )(proj, proj, proj, d_cat, mcw, conv_g, d_proj)


FFN_ROWS = 128


def _ffn_act_fwd(pre, fcw, nbatch, seq):
    t = nbatch * seq
    nchunks = seq // FFN_ROWS

    def body(pre_ref, w_ref, act_ref):
        wa = w_ref[0]
        wc = w_ref[1]

        def step(c, carry):
            cur = pl.ds(pl.multiple_of(c * FFN_ROWS, FFN_ROWS), FFN_ROWS)
            a = _conv(_window(pre_ref.at[0], c, FFN_ROWS, nchunks, False), wa)[HALO:]
            v = _conv(_window(pre_ref.at[1], c, FFN_ROWS, nchunks, False), wc)[HALO:]
            act_ref[cur, :] = (a * jax.nn.sigmoid(a) * v).astype(BF16)
            return carry

        lax.fori_loop(0, nchunks, step, 0)

    return pl.pallas_call(
        body, name="ffn_act_fwd", grid=(N_UP_PAIRS, nbatch),
        in_specs=[pl.BlockSpec((2, None, seq, UP_CHUNK), lambda i, b: (0, i, b, 0)),
                  pl.BlockSpec((2, None, 3, UP_CHUNK), lambda i, b: (0, i, 0, 0))],
        out_specs=pl.BlockSpec((None, seq, UP_CHUNK), lambda i, b: (i, b, 0)),
        out_shape=jax.ShapeDtypeStruct((N_UP_PAIRS, t, UP_CHUNK), BF16),
        compiler_params=_params(("parallel", "parallel")),
    )(pre, fcw)


def _ffn_act_bwd(pre, d_act, fcw, nbatch, seq):
    nchunks = seq // FFN_ROWS

    def body(pre_ref, da_ref, w_ref, dpre_ref, dw_ref):
        b = pl.program_id(1)
        wa = w_ref[0]
        wc = w_ref[1]
        cur = slice(HALO, HALO + FFN_ROWS)

        def step(c, carry):
            rows = pl.ds(pl.multiple_of(c * FFN_ROWS, FFN_ROWS), FFN_ROWS)
            pg = _window(pre_ref.at[0], c, FFN_ROWS, nchunks, True)
            pv = _window(pre_ref.at[1], c, FFN_ROWS, nchunks, True)
            dact = _window(da_ref, c, FFN_ROWS, nchunks, True)
            a = _conv(pg, wa)
            v = _conv(pv, wc)
            sg = jax.nn.sigmoid(a)
            da = dact * v * (sg * (1.0 + a * (1.0 - sg)))
            dv = dact * (a * sg)
            dpre_ref[0, rows, :] = _conv_t(da, wa)[cur].astype(BF16)
            dpre_ref[1, rows, :] = _conv_t(dv, wc)[cur].astype(BF16)
            return tuple(acc + d for acc, d in zip(carry, _conv_wgrad(da, pg, cur) + _conv_wgrad(dv, pv, cur)))

        zero = jnp.zeros((1, UP_CHUNK), F32)
        sums = lax.fori_loop(0, nchunks, step, (zero,) * 6)

        @pl.when(b == 0)
        def _():
            for i in range(6):
                dw_ref[i // 3, pl.ds(i % 3, 1), :] = sums[i]

        @pl.when(b > 0)
        def _():
            for i in range(6):
                dw_ref[i // 3, pl.ds(i % 3, 1), :] += sums[i]

    pair = pl.BlockSpec((2, None, seq, UP_CHUNK), lambda i, b: (0, i, b, 0))
    wspec = pl.BlockSpec((2, None, 3, UP_CHUNK), lambda i, b: (0, i, 0, 0))
    return pl.pallas_call(
        body, name="ffn_act_bwd", grid=(N_UP_PAIRS, nbatch),
        in_specs=[pair, pl.BlockSpec((None, seq, UP_CHUNK), lambda i, b: (i, b, 0)), wspec],
        out_specs=[pair, wspec],
        out_shape=[jax.ShapeDtypeStruct(pre.shape, BF16), jax.ShapeDtypeStruct(fcw.shape, F32)],
        compiler_params=_params(("parallel", "arbitrary")),
    )(pre, d_act, fcw)


def _adamw(land, w, m, v, row_tile, name):
    nl, _, nr, ncol = land.shape
    c1 = 1.0 - ADAM_B1 ** ADAM_STEP
    c2 = 1.0 - ADAM_B2 ** ADAM_STEP

    def body(land_ref, w_ref, m_ref, v_ref, g_ref, d_ref, mo_ref, vo_ref):
        g = land_ref[0].astype(F32)
        for j in range(1, N_DEV):
            g = g + land_ref[j].astype(F32)
        m2 = ADAM_B1 * m_ref[...] + (1.0 - ADAM_B1) * g
        v2 = ADAM_B2 * v_ref[...] + (1.0 - ADAM_B2) * (g * g)
        g_ref[...] = g
        mo_ref[...] = m2
        vo_ref[...] = v2
        d_ref[...] = -ADAM_LR * ((m2 / c1) / (jnp.sqrt(v2 / c2) + ADAM_EPS) + ADAM_WD * w_ref[...])

    tile = pl.BlockSpec((None, row_tile, ncol), lambda l, i: (l, i, 0))
    return pl.pallas_call(
        body, name=name, grid=(nl, nr // row_tile),
        in_specs=[pl.BlockSpec((None, N_DEV, row_tile, ncol), lambda l, i: (l, 0, i, 0)), tile, tile, tile],
        out_specs=[tile] * 4,
        out_shape=[jax.ShapeDtypeStruct(w.shape, F32)] * 4,
        compiler_params=_params(("parallel", "parallel")),
    )(land, w, m, v)


class _Item:
    def __init__(self, src, chunked, land_shape, prefix=(), land_in=None):
        self.src, self.chunked, self.land_shape, self.prefix, self.land_in = src, chunked, land_shape, prefix, land_in


def _exchange(items, name):
    n = len(items)
    aliased = [i for i, it in enumerate(items) if it.land_in is not None]

    def body(*refs):
        srcs = refs[:n]
        lands = refs[n + len(aliased):2 * n + len(aliased)]
        send, recv, local = refs[2 * n + len(aliased):]
        x, y, c = lax.axis_index("x"), lax.axis_index("y"), lax.axis_index("c")
        me = 4 * x + 2 * y + c

        def flipped(k):
            px = 1 - x if k & 4 else x
            py = 1 - y if k & 2 else y
            pc = 1 - c if k & 1 else c
            return (px, py, pc), 4 * px + 2 * py + pc

        def copy(i, k, chunk, slot, dev):
            it = items[i]
            src = srcs[i].at[chunk] if it.chunked else srcs[i]
            return pltpu.make_async_remote_copy(
                src_ref=src, dst_ref=lands[i].at[it.prefix + (slot,)],
                send_sem=send.at[i, k - 1], recv_sem=recv.at[i, k - 1], device_id=dev, device_id_type=MESH)

        own = [pltpu.make_async_copy(srcs[i].at[me] if items[i].chunked else srcs[i],
                                     lands[i].at[items[i].prefix + (me,)], local.at[i]) for i in range(n)]
        order = (2, 4, 6, 3, 5, 7, 1)
        for k in order:
            dev, idx = flipped(k)
            for i in range(n):
                copy(i, k, idx, me, dev).start()
        for cp in own:
            cp.start()
        for k in order:
            dev, idx = flipped(k)
            for i in range(n):
                copy(i, k, me, idx, dev).wait_recv()
        for k in order:
            dev, idx = flipped(k)
            for i in range(n):
                copy(i, k, idx, me, dev).wait_send()
        for cp in own:
            cp.wait()

    hbm = pl.BlockSpec(memory_space=pl.ANY)
    outs = pl.pallas_call(
        body, name=name,
        in_specs=[hbm] * (n + len(aliased)), out_specs=[hbm] * n,
        out_shape=[jax.ShapeDtypeStruct(it.land_shape, it.src.dtype) for it in items],
        scratch_shapes=[pltpu.SemaphoreType.DMA((n, N_DEV - 1)), pltpu.SemaphoreType.DMA((n, N_DEV - 1)),
                        pltpu.SemaphoreType.DMA((n,))],
        input_output_aliases={n + j: i for j, i in enumerate(aliased)},
        compiler_params=pltpu.CompilerParams(has_side_effects=True),
    )(*[it.src for it in items], *[items[i].land_in for i in aliased])
    return outs


TM = 1024
TM_ACC = 512


def kernel(x, norm1_g, w_in, mix_conv_w, attn_out_g, conv_out_g, w_out, norm2_g, ffn_up, ffn_conv_w, ffn_down, final_norm_g, loss_target, m_norm1_g, m_w_in, m_mix_conv_w, m_attn_out_g, m_conv_out_g, m_w_out, m_norm2_g, m_ffn_up, m_ffn_conv_w, m_ffn_down, m_final_norm_g, v_norm1_g, v_w_in, v_mix_conv_w, v_attn_out_g, v_conv_out_g, v_w_out, v_norm2_g, v_ffn_up, v_ffn_conv_w, v_ffn_down, v_final_norm_g):
    nbatch, seq, d = x.shape
    t = nbatch * seq
    nt, nta = t // TM, t // TM_ACC
    out_rows = D_MODEL // N_DEV
    down_rows = D_FF // N_DEV
    xf = x.reshape(t, d)
    target = loss_target.reshape(t, d)

    cw_local = jnp.concatenate([ffn_conv_w, mix_conv_w], axis=-1)
    items = []
    for l in range(DEPTH):
        items += [
            _Item(w_in[l].astype(BF16), False, (N_DEV, D_MODEL, IN_CHUNK)),
            _Item(w_out[l].astype(BF16), False, (N_DEV, out_rows, D_MODEL)),
            _Item(ffn_up[l].astype(BF16), False, (N_DEV, D_MODEL, UP_CHUNK)),
            _Item(ffn_down[l].astype(BF16), False, (N_DEV, down_rows, D_MODEL)),
        ]
    items.append(_Item(cw_local, False, (N_DEV, DEPTH, 3, CW_PACK)))
    gathered = _exchange(items, "gather_weights")
    win = [gathered[4 * l] for l in range(DEPTH)]
    wout = [gathered[4 * l + 1].reshape(D_MODEL, D_MODEL) for l in range(DEPTH)]
    wup = [gathered[4 * l + 2] for l in range(DEPTH)]
    wdown = [gathered[4 * l + 3].reshape(N_UP_PAIRS, UP_CHUNK, D_MODEL) for l in range(DEPTH)]
    cw_all = gathered[-1]
    fcw = [cw_all[:, l, :, :UP_CHUNK].reshape(2, N_UP_PAIRS, 3, UP_CHUNK) for l in range(DEPTH)]
    mcw = [cw_all[:, l, :, UP_CHUNK:].transpose(1, 0, 2).reshape(3, D_CONV) for l in range(DEPTH)]

    full = lambda i, j, k: (0, 0)

    saved = []
    xin = xf
    for l in range(DEPTH):
        h1 = _rms_fwd(xin, norm1_g[l][None], f"rms1_fwd_{l}")
        proj = _matmul(
            h1, win[l], grid=(nt, N_DEV, 1), dims=NN, name=f"proj_{l}",
            a_spec=pl.BlockSpec((TM, D_MODEL), lambda i, j, k: (i, 0)),
            b_spec=pl.BlockSpec((None, D_MODEL, IN_CHUNK), lambda i, j, k: (j, 0, 0)),
            o_spec=pl.BlockSpec((TM, IN_CHUNK), lambda i, j, k: (i, j)), o_shape=(t, D_IN), o_dtype=F32)
        o, lse, cat = _attn_fwd(proj, attn_out_g[l][None], nbatch, seq)
        cat = _convmix_fwd(proj, cat, mcw[l], conv_out_g[l][None], nbatch, seq)
        xmid = _matmul(
            cat, wout[l], grid=(nta, 1, 1), dims=NN, name=f"mix_out_{l}",
            a_spec=pl.BlockSpec((TM_ACC, D_MODEL), lambda i, j, k: (i, 0)),
            b_spec=pl.BlockSpec((D_MODEL, D_MODEL), full),
            o_spec=pl.BlockSpec((TM_ACC, D_MODEL), lambda i, j, k: (i, 0)), o_shape=(t, D_MODEL), o_dtype=F32,
            res=xin, res_spec=pl.BlockSpec((TM_ACC, D_MODEL), lambda i, j, k: (i, 0)))
        h2 = _rms_fwd(xmid, norm2_g[l][None], f"rms2_fwd_{l}")
        pre = _matmul(
            h2, wup[l], grid=(nt, N_DEV, 1), dims=NN, name=f"ffn_up_{l}",
            a_spec=pl.BlockSpec((TM, D_MODEL), lambda i, j, k: (i, 0)),
            b_spec=pl.BlockSpec((None, D_MODEL, UP_CHUNK), lambda i, j, k: (j, 0, 0)),
            o_spec=pl.BlockSpec((None, TM, UP_CHUNK), lambda i, j, k: (j, i, 0)),
            o_shape=(N_DEV, t, UP_CHUNK), o_dtype=BF16).reshape(2, N_UP_PAIRS, t, UP_CHUNK)
        act = _ffn_act_fwd(pre, fcw[l], nbatch, seq)
        xout = _matmul(
            act, wdown[l], grid=(nta, 1, N_UP_PAIRS), dims=NN, name=f"ffn_down_{l}",
            a_spec=pl.BlockSpec((None, TM_ACC, UP_CHUNK), lambda i, j, k: (k, i, 0)),
            b_spec=pl.BlockSpec((None, UP_CHUNK, D_MODEL), lambda i, j, k: (k, 0, 0)),
            o_spec=pl.BlockSpec((TM_ACC, D_MODEL), lambda i, j, k: (i, 0)), o_shape=(t, D_MODEL), o_dtype=F32,
            res=xmid, res_spec=pl.BlockSpec((TM_ACC, D_MODEL), lambda i, j, k: (i, 0)))
        saved.append((xin, h1, proj, o, lse, cat, xmid, h2, pre, act))
        xin = xout

    loss_part, dx, dxb, dgf = _loss_head(xin, final_norm_g[None], target, "loss_head")

    lands = [None] * 5
    dg1, dg2, dga, dgc = [None] * DEPTH, [None] * DEPTH, [None] * DEPTH, [None] * DEPTH
    for l in reversed(range(DEPTH)):
        xin, h1, proj, o, lse, cat, xmid, h2, pre, act = saved[l]
        d_act = _matmul(
            dxb, wdown[l], grid=(nt, N_UP_PAIRS, 1), dims=NT, name=f"d_act_{l}",
            a_spec=pl.BlockSpec((TM, D_MODEL), lambda i, j, k: (i, 0)),
            b_spec=pl.BlockSpec((None, UP_CHUNK, D_MODEL), lambda i, j, k: (j, 0, 0)),
            o_spec=pl.BlockSpec((None, TM, UP_CHUNK), lambda i, j, k: (j, i, 0)),
            o_shape=(N_UP_PAIRS, t, UP_CHUNK), o_dtype=BF16)
        g_down = _matmul(
            act, dxb, grid=(N_UP_PAIRS, 1, nt), dims=TN, name=f"g_down_{l}",
            a_spec=pl.BlockSpec((None, TM, UP_CHUNK), lambda i, j, k: (i, k, 0)),
            b_spec=pl.BlockSpec((TM, D_MODEL), lambda i, j, k: (k, 0)),
            o_spec=pl.BlockSpec((None, UP_CHUNK, D_MODEL), lambda i, j, k: (i, 0, 0)),
            o_shape=(N_UP_PAIRS, UP_CHUNK, D_MODEL), o_dtype=BF16).reshape(N_DEV, down_rows, D_MODEL)
        d_pre, d_fcw = _ffn_act_bwd(pre, d_act, fcw[l], nbatch, seq)
        d_pre = d_pre.reshape(N_DEV, t, UP_CHUNK)
        dh2 = _matmul(
            d_pre, wup[l], grid=(nta, 1, N_DEV), dims=NT, name=f"d_h2_{l}",
            a_spec=pl.BlockSpec((None, TM_ACC, UP_CHUNK), lambda i, j, k: (k, i, 0)),
            b_spec=pl.BlockSpec((None, D_MODEL, UP_CHUNK), lambda i, j, k: (k, 0, 0)),
            o_spec=pl.BlockSpec((TM_ACC, D_MODEL), lambda i, j, k: (i, 0)), o_shape=(t, D_MODEL), o_dtype=F32)
        g_up = _matmul(
            h2, d_pre, grid=(1, N_DEV, nt), dims=TN, name=f"g_up_{l}",
            a_spec=pl.BlockSpec((TM, D_MODEL), lambda i, j, k: (k, 0)),
            b_spec=pl.BlockSpec((None, TM, UP_CHUNK), lambda i, j, k: (j, k, 0)),
            o_spec=pl.BlockSpec((None, D_MODEL, UP_CHUNK), lambda i, j, k: (j, 0, 0)),
            o_shape=(N_DEV, D_MODEL, UP_CHUNK), o_dtype=BF16)
        dxm, dxmb, dg2[l] = _rms_bwd(xmid, norm2_g[l][None], dh2, dx, f"rms2_bwd_{l}")
        d_cat = _matmul(
            dxmb, wout[l], grid=(nta, 1, 1), dims=NT, name=f"d_cat_{l}",
            a_spec=pl.BlockSpec((TM_ACC, D_MODEL), lambda i, j, k: (i, 0)),
            b_spec=pl.BlockSpec((D_MODEL, D_MODEL), full),
            o_spec=pl.BlockSpec((TM_ACC, D_MODEL), lambda i, j, k: (i, 0)), o_shape=(t, D_MODEL), o_dtype=BF16)
        g_out = _matmul(
            cat, dxmb, grid=(1, 1, nt), dims=TN, name=f"g_out_{l}",
            a_spec=pl.BlockSpec((TM, D_MODEL), lambda i, j, k: (k, 0)),
            b_spec=pl.BlockSpec((TM, D_MODEL), lambda i, j, k: (k, 0)),
            o_spec=pl.BlockSpec((D_MODEL, D_MODEL), full),
            o_shape=(D_MODEL, D_MODEL), o_dtype=BF16).reshape(N_DEV, out_rows, D_MODEL)
        d_proj, dga[l] = _attn_bwd(proj, o, lse, d_cat, attn_out_g[l][None], nbatch, seq)
        d_proj, d_mcw, dgc[l] = _convmix_bwd(proj, d_cat, d_proj, mcw[l], conv_out_g[l][None], nbatch, seq)
        dh1 = _matmul(
            d_proj, win[l], grid=(nta, 1, N_DEV), dims=NT, name=f"d_h1_{l}",
            a_spec=pl.BlockSpec((TM_ACC, IN_CHUNK), lambda i, j, k: (i, k)),
            b_spec=pl.BlockSpec((None, D_MODEL, IN_CHUNK), lambda i, j, k: (k, 0, 0)),
            o_spec=pl.BlockSpec((TM_ACC, D_MODEL), lambda i, j, k: (i, 0)), o_shape=(t, D_MODEL), o_dtype=F32)
        g_in = _matmul(
            h1, d_proj, grid=(1, N_DEV, nt), dims=TN, name=f"g_in_{l}",
            a_spec=pl.BlockSpec((TM, D_MODEL), lambda i, j, k: (k, 0)),
            b_spec=pl.BlockSpec((TM, IN_CHUNK), lambda i, j, k: (k, j)),
            o_spec=pl.BlockSpec((None, D_MODEL, IN_CHUNK), lambda i, j, k: (j, 0, 0)),
            o_shape=(N_DEV, D_MODEL, IN_CHUNK), o_dtype=BF16)
        dx, dxb, dg1[l] = _rms_bwd(xin, norm1_g[l][None], dh1, dxm, f"rms1_bwd_{l}")

        g_cw = jnp.concatenate(
            [d_fcw.reshape(N_DEV, 3, UP_CHUNK), d_mcw.reshape(3, N_DEV, D_CONV // N_DEV).transpose(1, 0, 2)], axis=-1)
        srcs = [g_in, g_out, g_up, g_down, g_cw]
        shapes = [(DEPTH, N_DEV) + s.shape[1:] for s in srcs]
        items = [_Item(s, True, shp, (l,), land) for s, shp, land in zip(srcs, shapes, lands)]
        if l == 0:
            small = jnp.concatenate(
                [dg1[0], dg1[1], dg2[0], dg2[1], dgf,
                 jnp.concatenate([dga[0], dgc[0]], axis=-1), jnp.concatenate([dga[1], dgc[1]], axis=-1),
                 jnp.zeros((1, D_MODEL), F32)], axis=0)
            items.append(_Item(small, False, (1, N_DEV, SUBLANES, D_MODEL), (0,)))
        landed = _exchange(items, f"scatter_grads_{l}")
        lands = landed[:5]

    def pack_small(n1, a, c, n2, f):
        return jnp.concatenate(
            [n1, n2, f[None], jnp.concatenate([a, c], axis=-1), jnp.zeros((1, D_MODEL), F32)], axis=0)[None]

    res_in = _adamw(lands[0], w_in, m_w_in, v_w_in, 256, "adamw_w_in")
    res_out = _adamw(lands[1], w_out, m_w_out, v_w_out, out_rows, "adamw_w_out")
    res_up = _adamw(lands[2], ffn_up, m_ffn_up, v_ffn_up, 256, "adamw_ffn_up")
    res_down = _adamw(lands[3], ffn_down, m_ffn_down, v_ffn_down, down_rows, "adamw_ffn_down")
    res_cw = _adamw(
        lands[4], cw_local, jnp.concatenate([m_ffn_conv_w, m_mix_conv_w], axis=-1),
        jnp.concatenate([v_ffn_conv_w, v_mix_conv_w], axis=-1), 3, "adamw_conv_w")
    res_small = _adamw(
        landed[5], pack_small(norm1_g, attn_out_g, conv_out_g, norm2_g, final_norm_g),
        pack_small(m_norm1_g, m_attn_out_g, m_conv_out_g, m_norm2_g, m_final_norm_g),
        pack_small(v_norm1_g, v_attn_out_g, v_conv_out_g, v_norm2_g, v_final_norm_g), SUBLANES, "adamw_gains")

    loss = lax.psum(loss_part[0, 0], ("x", "y", "c"))

    def unpack(kind):
        s = res_small[kind][0]
        cwr = res_cw[kind]
        return (s[0:2], res_in[kind], cwr[..., UP_CHUNK:], s[5:7, :D_ATTN], s[5:7, D_ATTN:], res_out[kind],
                s[2:4], res_up[kind], cwr[..., :UP_CHUNK], res_down[kind], s[4])

    return (loss, dx.reshape(nbatch, seq, d), *unpack(0), *unpack(1), *unpack(2), *unpack(3))
```

```python
import math

import jax
import jax.numpy as jnp
from jax import lax
from jax.experimental import pallas as pl
from jax.experimental.pallas import tpu as pltpu

F32 = jnp.float32
BF16 = jnp.bfloat16

D_MODEL = 1024
D_ATTN = 512
D_CONV = 512
HEAD_DIM = 64
N_HEADS = 8
D_FF = 2816
DEPTH = 2
D_IN = 3 * D_ATTN + 3 * D_CONV
EPS = 1e-6
DILATIONS = (1, 4, 16)
BAND = 128
N_DEV = 8
IN_CHUNK = D_IN // N_DEV
UP_CHUNK = 2 * D_FF // N_DEV
N_UP_PAIRS = N_DEV // 2
CW_PACK = UP_CHUNK + D_CONV // N_DEV
ADAM_LR = 0.001
ADAM_B1 = 0.9
ADAM_B2 = 0.999
ADAM_EPS = 1e-08
ADAM_WD = 0.01
ADAM_STEP = 10
LANES = 128
SUBLANES = 8
VMEM_LIMIT = 56 * 1024 * 1024

NEG = -1e30
MESH = pl.DeviceIdType.MESH


def _params(sem=None, vmem=VMEM_LIMIT):
    return pltpu.CompilerParams(dimension_semantics=sem, vmem_limit_bytes=vmem)


NN = (((1,), (0,)), ((), ()))
NT = (((1,), (1,)), ((), ()))
TN = (((0,), (0,)), ((), ()))


def _matmul(a, b, *, grid, a_spec, b_spec, o_spec, o_shape, o_dtype, dims, name, res=None, res_spec=None):
    nk = grid[2]
    o_block = tuple(s for s in o_spec.block_shape if s is not None)

    def body(*refs):
        if res is None:
            a_ref, b_ref, o_ref, *scr = refs
            r_ref = None
        else:
            a_ref, b_ref, r_ref, o_ref, *scr = refs
        part = lax.dot_general(a_ref[...].astype(BF16), b_ref[...].astype(BF16), dims, preferred_element_type=F32)

        def finish(total):
            if r_ref is not None:
                total = total + r_ref[...]
            o_ref[...] = total.astype(o_dtype)

        if nk == 1:
            finish(part)
        else:
            acc = scr[0]
            k = pl.program_id(2)

            @pl.when(k == 0)
            def _():
                acc[...] = part

            @pl.when(k > 0)
            def _():
                acc[...] += part

            @pl.when(k == nk - 1)
            def _():
                finish(acc[...])

    in_specs = [a_spec, b_spec] + ([res_spec] if res is not None else [])
    args = (a, b) + ((res,) if res is not None else ())
    return pl.pallas_call(
        body, name=name, grid=grid, in_specs=in_specs, out_specs=o_spec,
        out_shape=jax.ShapeDtypeStruct(o_shape, o_dtype),
        scratch_shapes=[pltpu.VMEM(o_block, F32)] if nk > 1 else [],
        compiler_params=_params(("parallel", "parallel", "arbitrary")),
    )(*args)


ROW_TILE = 512


def _rms_fwd(x, g, name):
    t, d = x.shape

    def body(x_ref, g_ref, h_ref):
        xv = x_ref[...]
        r = lax.rsqrt(jnp.mean(xv * xv, axis=-1, keepdims=True) + EPS)
        h_ref[...] = (xv * r * g_ref[...]).astype(BF16)

    return pl.pallas_call(
        body, name=name, grid=(t // ROW_TILE,),
        in_specs=[pl.BlockSpec((ROW_TILE, d), lambda i: (i, 0)), pl.BlockSpec((1, d), lambda i: (0, 0))],
        out_specs=pl.BlockSpec((ROW_TILE, d), lambda i: (i, 0)),
        out_shape=jax.ShapeDtypeStruct((t, d), BF16),
        compiler_params=_params(("parallel",)),
    )(x, g)


def _rms_bwd(x, g, dh, dres, name):
    t, d = x.shape

    def body(x_ref, g_ref, dh_ref, dres_ref, dx_ref, dxb_ref, dg_ref):
        xv = x_ref[...]
        r = lax.rsqrt(jnp.mean(xv * xv, axis=-1, keepdims=True) + EPS)
        xh = xv * r
        dhv = dh_ref[...]
        gd = dhv * g_ref[...]
        dx = r * (gd - xh * jnp.mean(gd * xh, axis=-1, keepdims=True)) + dres_ref[...]
        dx_ref[...] = dx
        dxb_ref[...] = dx.astype(BF16)
        part = jnp.sum(dhv * xh, axis=0, keepdims=True)

        @pl.when(pl.program_id(0) == 0)
        def _():
            dg_ref[...] = part

        @pl.when(pl.program_id(0) > 0)
        def _():
            dg_ref[...] += part

    row = pl.BlockSpec((ROW_TILE, d), lambda i: (i, 0))
    vec = pl.BlockSpec((1, d), lambda i: (0, 0))
    return pl.pallas_call(
        body, name=name, grid=(t // ROW_TILE,),
        in_specs=[row, vec, row, row], out_specs=[row, row, vec],
        out_shape=[jax.ShapeDtypeStruct((t, d), F32), jax.ShapeDtypeStruct((t, d), BF16),
                   jax.ShapeDtypeStruct((1, d), F32)],
        compiler_params=_params(("arbitrary",)),
    )(x, g, dh, dres)


def _loss_head(x, g, target, name):
    t, d = x.shape

    def body(x_ref, g_ref, t_ref, loss_ref, dx_ref, dxb_ref, dg_ref):
        xv = x_ref[...]
        r = lax.rsqrt(jnp.mean(xv * xv, axis=-1, keepdims=True) + EPS)
        xh = xv * r
        gv = g_ref[...]
        err = xh * gv - t_ref[...]
        loss = jnp.full((1, LANES), 0.5 / d, F32) * jnp.sum(err * err)
        dy = err * (1.0 / d)
        gd = dy * gv
        dx = r * (gd - xh * jnp.mean(gd * xh, axis=-1, keepdims=True))
        dx_ref[...] = dx
        dxb_ref[...] = dx.astype(BF16)
        part = jnp.sum(dy * xh, axis=0, keepdims=True)

        @pl.when(pl.program_id(0) == 0)
        def _():
            dg_ref[...] = part
            loss_ref[...] = loss

        @pl.when(pl.program_id(0) > 0)
        def _():
            dg_ref[...] += part
            loss_ref[...] += loss

    row = pl.BlockSpec((ROW_TILE, d), lambda i: (i, 0))
    vec = pl.BlockSpec((1, d), lambda i: (0, 0))
    return pl.pallas_call(
        body, name=name, grid=(t // ROW_TILE,),
        in_specs=[row, vec, row],
        out_specs=[pl.BlockSpec((1, LANES), lambda i: (0, 0)), row, row, vec],
        out_shape=[jax.ShapeDtypeStruct((1, LANES), F32), jax.ShapeDtypeStruct((t, d), F32),
                   jax.ShapeDtypeStruct((t, d), BF16), jax.ShapeDtypeStruct((1, d), F32)],
        compiler_params=_params(("arbitrary",)),
    )(x, g, target)


def _group_matrix(n):
    shift = int(math.log2(HEAD_DIM))
    r = lax.broadcasted_iota(jnp.int32, (n, n), 0) >> shift
    c = lax.broadcasted_iota(jnp.int32, (n, n), 1) >> shift
    return (r == c).astype(BF16)


def _group_sum(v, gmat):
    hi = v.astype(BF16)
    rest = v - hi.astype(F32)
    mid = rest.astype(BF16)
    lo = (rest - mid.astype(F32)).astype(BF16)

    def dot(p):
        return jnp.dot(p, gmat, preferred_element_type=F32)

    return dot(hi) + dot(mid) + dot(lo)


def _shift_rows(ext, k):
    return pltpu.roll(ext, k % ext.shape[0], 0)


def _store_columns(stage, out_hbm, sems, row0, nrows, col_blocks):
    rows = pl.ds(pl.multiple_of(row0, SUBLANES * 2), nrows)
    copies = [
        pltpu.make_async_copy(stage.at[i], out_hbm.at[rows, pl.ds(pl.multiple_of(cb * LANES, LANES), LANES)], sems.at[i])
        for i, cb in enumerate(col_blocks)
    ]
    for cp in copies:
        cp.start()
    for cp in copies:
        cp.wait()


def _attn_consts(width):
    i = lax.broadcasted_iota(jnp.int32, (BAND, width), 0)
    j = lax.broadcasted_iota(jnp.int32, (BAND, width), 1)
    dist = (width - BAND) + i - j
    inwin = (dist >= 0) & (dist <= BAND)
    return dist.astype(F32), inwin, j


def _head_masks():
    lane = lax.broadcasted_iota(jnp.int32, (1, LANES), 1)
    return [(lane < HEAD_DIM).astype(F32), (lane >= HEAD_DIM).astype(F32)]


def _permute_in(src_ref, dst_ref, dil, seq):
    length = seq // dil
    for r in range(dil):
        dst_ref[pl.ds(r * length, length), :] = src_ref[pl.ds(r, length, stride=dil), :].astype(dst_ref.dtype)


def _slopes_table():
    slopes = 2.0 ** (-8.0 * jnp.arange(1, N_HEADS + 1, dtype=F32) / N_HEADS)
    return jnp.broadcast_to(slopes[:, None], (N_HEADS, 2 * BAND))


def _attn_fwd(proj, attn_g, nbatch, seq):
    t = nbatch * seq
    nblk = seq // BAND
    scale = HEAD_DIM ** -0.5

    def body(q_ref, k_ref, v_ref, g_ref, sl_ref, o_ref, lse_ref, cat_ref, pq, pk, pv, po, pm, pll, ao, am, al):
        hp = pl.program_id(1)
        hmask = _head_masks()
        slope = [sl_ref[pl.ds(2 * hp + hh, 1), :] for hh in range(2)]

        def run_branch(dil, qs, ks, vs, osink, msink, lsink):
            nb = seq // dil // BAND
            width = 2 * BAND if nb > 1 else BAND
            distf, inwin, jcol = _attn_consts(width)
            bias = [distf * (slope[hh][:, :width] * (-float(dil))) for hh in range(2)]

            def blk(m, carry):
                row0 = pl.multiple_of(m * BAND, BAND)
                q = qs[pl.ds(row0, BAND), :] * scale
                if nb > 1:
                    prow = pl.multiple_of(jnp.maximum(m - 1, 0) * BAND, BAND)
                    kk = jnp.concatenate([ks[pl.ds(prow, BAND), :], ks[pl.ds(row0, BAND), :]], axis=0)
                    vv = jnp.concatenate([vs[pl.ds(prow, BAND), :], vs[pl.ds(row0, BAND), :]], axis=0)
                    valid = inwin & (jcol >= jnp.where((m % nb) == 0, BAND, 0))
                else:
                    kk = ks[pl.ds(row0, BAND), :]
                    vv = vs[pl.ds(row0, BAND), :]
                    valid = inwin
                kb = kk.astype(BF16)
                o = jnp.zeros((BAND, LANES), F32)
                mfull = jnp.zeros((BAND, LANES), F32)
                lfull = jnp.zeros((BAND, LANES), F32)
                for hh in range(2):
                    qh = (q * hmask[hh]).astype(BF16)
                    s = lax.dot_general(qh, kb, NT, preferred_element_type=F32)
                    s = jnp.where(valid, s + bias[hh], NEG)
                    mh = jnp.max(s, axis=1, keepdims=True)
                    p = jnp.exp(s - mh)
                    lh = jnp.sum(p, axis=1, keepdims=True)
                    o = o + jnp.dot(p.astype(BF16), (vv * hmask[hh]).astype(BF16), preferred_element_type=F32)
                    mfull = mfull + mh * hmask[hh]
                    lfull = lfull + lh * hmask[hh]
                osink[pl.ds(row0, BAND), :] = o
                msink[pl.ds(row0, BAND), :] = mfull
                lsink[pl.ds(row0, BAND), :] = lfull
                return carry

            lax.fori_loop(0, nblk, blk, 0)

        run_branch(1, q_ref, k_ref, v_ref, ao, am, al)
        for dil in DILATIONS[1:]:
            length = seq // dil
            _permute_in(q_ref, pq, dil, seq)
            _permute_in(k_ref, pk, dil, seq)
            _permute_in(v_ref, pv, dil, seq)
            run_branch(dil, pq, pk, pv, po, pm, pll)
            for r in range(dil):
                nat = pl.ds(r, length, stride=dil)
                per = pl.ds(r * length, length)
                m0 = am[nat, :]
                mb = pm[per, :]
                mn = jnp.maximum(m0, mb)
                e0 = jnp.exp(m0 - mn)
                eb = jnp.exp(mb - mn)
                ao[nat, :] = ao[nat, :] * e0 + po[per, :] * eb
                al[nat, :] = al[nat, :] * e0 + pll[per, :] * eb
                am[nat, :] = mn

        gmat = _group_matrix(LANES)
        gv = g_ref[...]

        def fin(c, carry):
            rows = pl.ds(pl.multiple_of(c * 256, 256), 256)
            lv = al[rows, :]
            o = ao[rows, :] / lv
            o_ref[rows, :] = o
            lse_ref[rows, :] = am[rows, :] + jnp.log(lv)
            ms = _group_sum(o * o, gmat) * (1.0 / HEAD_DIM)
            cat_ref[rows, :] = (o * lax.rsqrt(ms + EPS) * gv).astype(BF16)
            return carry

        lax.fori_loop(0, seq // 256, fin, 0)

    nq = D_ATTN // LANES
    blk = lambda off: pl.BlockSpec((seq, LANES), lambda b, h: (b, h + off))
    scratch = [pltpu.VMEM((seq, LANES), F32) for _ in range(9)]
    return pl.pallas_call(
        body, name="attn_fwd", grid=(nbatch, nq),
        in_specs=[blk(0), blk(nq), blk(2 * nq), pl.BlockSpec((1, LANES), lambda b, h: (0, h)),
                  pl.BlockSpec((N_HEADS, 2 * BAND), lambda b, h: (0, 0))],
        out_specs=[blk(0), blk(0), blk(0)],
        out_shape=[jax.ShapeDtypeStruct((t, D_ATTN), F32), jax.ShapeDtypeStruct((t, D_ATTN), F32),
                   jax.ShapeDtypeStruct((t, D_MODEL), BF16)],
        scratch_shapes=scratch,
        compiler_params=_params(("parallel", "parallel")),
    )(proj, proj, proj, attn_g, _slopes_table())


def _attn_bwd(proj, o, lse, d_cat, attn_g, nbatch, seq):
    t = nbatch * seq
    nblk = seq // BAND
    scale = HEAD_DIM ** -0.5

    def body(q_ref, k_ref, v_ref, o_ref, lse_ref, dy_ref, g_ref, sl_ref, dproj_ref, dg_ref,
             do_n, dl_n, dq_n, dk_n, dv_n, pq, pk, pv, pdo, plse, pdl, pdq, pdk, pdv, stage, sems):
        hp = pl.program_id(0)
        hmask = _head_masks()
        slope = [sl_ref[pl.ds(2 * hp + hh, 1), :] for hh in range(2)]
        gmat = _group_matrix(LANES)
        gv = g_ref[...]

        def prep(c, dg):
            rows = pl.ds(pl.multiple_of(c * 256, 256), 256)
            ov = o_ref[rows, :]
            dyn = dy_ref[rows, :].astype(F32)
            r = lax.rsqrt(_group_sum(ov * ov, gmat) * (1.0 / HEAD_DIM) + EPS)
            gd = dyn * gv
            oh = ov * r
            do = r * (gd - oh * (_group_sum(gd * oh, gmat) * (1.0 / HEAD_DIM)))
            do_n[rows, :] = do
            dl_n[rows, :] = _group_sum(do * ov, gmat)
            return dg + jnp.sum(dyn * oh, axis=0, keepdims=True)

        dg = lax.fori_loop(0, seq // 256, prep, jnp.zeros((1, LANES), F32))

        @pl.when(pl.program_id(1) == 0)
        def _():
            dg_ref[...] = dg

        @pl.when(pl.program_id(1) > 0)
        def _():
            dg_ref[...] += dg

        def clear(*refs):
            def step(c, carry):
                rows = pl.ds(pl.multiple_of(c * 256, 256), 256)
                for ref in refs:
                    ref[rows, :] = jnp.zeros((256, LANES), F32)
                return carry

            lax.fori_loop(0, seq // 256, step, 0)

        clear(dq_n, dk_n, dv_n)

        def run_branch(dil, qs, ks, vs, dos, lses, dls, dqs, dks, dvs):
            nb = seq // dil // BAND
            width = 2 * BAND if nb > 1 else BAND
            distf, inwin, jcol = _attn_consts(width)
            bias = [distf * (slope[hh][:, :width] * (-float(dil))) for hh in range(2)]

            def blk(m, carry):
                row0 = pl.multiple_of(m * BAND, BAND)
                cur = pl.ds(row0, BAND)
                q = qs[cur, :] * scale
                dov = dos[cur, :]
                lsev = lses[cur, :]
                dlv = dls[cur, :]
                if nb > 1:
                    prow = pl.multiple_of(jnp.maximum(m - 1, 0) * BAND, BAND)
                    prev = pl.ds(prow, BAND)
                    kk = jnp.concatenate([ks[prev, :], ks[cur, :]], axis=0)
                    vv = jnp.concatenate([vs[prev, :], vs[cur, :]], axis=0)
                    valid = inwin & (jcol >= jnp.where((m % nb) == 0, BAND, 0))
                else:
                    kk = ks[cur, :]
                    vv = vs[cur, :]
                    valid = inwin
                kb = kk.astype(BF16)
                vb = vv.astype(BF16)
                dq = jnp.zeros((BAND, LANES), F32)
                dkk = jnp.zeros((width, LANES), F32)
                dvv = jnp.zeros((width, LANES), F32)
                for hh in range(2):
                    c0 = hh * HEAD_DIM
                    qh = (q * hmask[hh]).astype(BF16)
                    doh = (dov * hmask[hh]).astype(BF16)
                    s = lax.dot_general(qh, kb, NT, preferred_element_type=F32) + bias[hh]
                    p = jnp.where(valid, jnp.exp(s - lsev[:, c0:c0 + 1]), 0.0)
                    dp = lax.dot_general(doh, vb, NT, preferred_element_type=F32)
                    ds = (p * (dp - dlv[:, c0:c0 + 1])).astype(BF16)
                    dq = dq + jnp.dot(ds, kb, preferred_element_type=F32) * hmask[hh]
                    dkk = dkk + lax.dot_general(ds, qh, TN, preferred_element_type=F32)
                    dvv = dvv + lax.dot_general(p.astype(BF16), doh, TN, preferred_element_type=F32)
                dqs[cur, :] += dq
                if nb > 1:
                    dks[prev, :] += dkk[:BAND]
                    dvs[prev, :] += dvv[:BAND]
                    dks[cur, :] += dkk[BAND:]
                    dvs[cur, :] += dvv[BAND:]
                else:
                    dks[cur, :] += dkk
                    dvs[cur, :] += dvv
                return carry

            lax.fori_loop(0, nblk, blk, 0)

        run_branch(1, q_ref, k_ref, v_ref, do_n, lse_ref, dl_n, dq_n, dk_n, dv_n)
        for dil in DILATIONS[1:]:
            length = seq // dil
            for src, dst in ((q_ref, pq), (k_ref, pk), (v_ref, pv), (do_n, pdo), (lse_ref, plse), (dl_n, pdl)):
                _permute_in(src, dst, dil, seq)
            clear(pdq, pdk, pdv)
            run_branch(dil, pq, pk, pv, pdo, plse, pdl, pdq, pdk, pdv)
            for r in range(dil):
                nat = pl.ds(r, length, stride=dil)
                per = pl.ds(r * length, length)
                dq_n[nat, :] += pdq[per, :]
                dk_n[nat, :] += pdk[per, :]
                dv_n[nat, :] += pdv[per, :]

        def emit(c, carry):
            rows = pl.ds(pl.multiple_of(c * 256, 256), 256)
            stage[0, rows, :] = (dq_n[rows, :] * scale).astype(BF16)
            stage[1, rows, :] = dk_n[rows, :].astype(BF16)
            stage[2, rows, :] = dv_n[rows, :].astype(BF16)
            return carry

        lax.fori_loop(0, seq // 256, emit, 0)
        _store_columns(stage, dproj_ref, sems, pl.program_id(1) * seq, seq, [hp, nq + hp, 2 * nq + hp])

    nq = D_ATTN // LANES
    blk = lambda off: pl.BlockSpec((seq, LANES), lambda h, b: (b, h + off))
    vec = pl.BlockSpec((1, LANES), lambda h, b: (0, h))
    scratch = [pltpu.VMEM((seq, LANES), F32) for _ in range(14)]
    scratch += [pltpu.VMEM((3, seq, LANES), BF16), pltpu.SemaphoreType.DMA((3,))]
    d_proj, dg = pl.pallas_call(
        body, name="attn_bwd", grid=(nq, nbatch),
        in_specs=[blk(0), blk(nq), blk(2 * nq), blk(0), blk(0), blk(0), vec,
                  pl.BlockSpec((N_HEADS, 2 * BAND), lambda h, b: (0, 0))],
        out_specs=[pl.BlockSpec(memory_space=pl.ANY), vec],
        out_shape=[jax.ShapeDtypeStruct((t, D_IN), BF16), jax.ShapeDtypeStruct((1, D_ATTN), F32)],
        scratch_shapes=scratch,
        compiler_params=_params(("arbitrary", "arbitrary")),
    )(proj, proj, proj, o, lse, d_cat, attn_g, _slopes_table())
    return d_proj, dg


HALO = 2 * SUBLANES


def _window(ref, c, rows, nchunks, after):
    row0 = pl.multiple_of(c * rows, rows)
    prev0 = pl.multiple_of(jnp.maximum(row0 - HALO, 0), HALO)
    parts = [ref[pl.ds(prev0, HALO), :].astype(F32) * (c > 0).astype(F32), ref[pl.ds(row0, rows), :].astype(F32)]
    if after:
        next0 = pl.multiple_of(jnp.minimum(row0 + rows, (nchunks - 1) * rows), HALO)
        parts.append(ref[pl.ds(next0, HALO), :].astype(F32) * (c < nchunks - 1).astype(F32))
    return jnp.concatenate(parts, axis=0)


def _conv(z, w):
    return w[0:1] * _shift_rows(z, 2) + w[1:2] * _shift_rows(z, 1) + w[2:3] * z


def _conv_t(dy, w):
    return w[2:3] * dy + w[1:2] * _shift_rows(dy, -1) + w[0:1] * _shift_rows(dy, -2)


def _conv_wgrad(dy, z, cur):
    return [jnp.sum((dy * _shift_rows(z, 2 - k))[cur], axis=0, keepdims=True) for k in range(3)]


MIX_ROWS = 256
GATE_B_BLOCK = 3 * D_ATTN // LANES
GATE_C_BLOCK = GATE_B_BLOCK + D_CONV // LANES
U_BLOCK = GATE_C_BLOCK + D_CONV // LANES


def _convmix_fwd(proj, cat, mcw, conv_g, nbatch, seq):
    nchunks = seq // MIX_ROWS

    def body(gb_ref, gc_ref, u_ref, w_ref, g_ref, cat_in, cat_ref):
        del cat_in
        gmat = _group_matrix(LANES)
        w = w_ref[...]
        gv = g_ref[...]

        def step(c, carry):
            cur = pl.ds(pl.multiple_of(c * MIX_ROWS, MIX_ROWS), MIX_ROWS)
            z = _window(gc_ref, c, MIX_ROWS, nchunks, False) * _window(u_ref, c, MIX_ROWS, nchunks, False)
            y = gb_ref[cur, :] * _conv(z, w)[HALO:]
            ms = _group_sum(y * y, gmat) * (1.0 / HEAD_DIM)
            cat_ref[cur, :] = (y * lax.rsqrt(ms + EPS) * gv).astype(BF16)
            return carry

        lax.fori_loop(0, nchunks, step, 0)

    nc = D_CONV // LANES
    blk = lambda off: pl.BlockSpec((seq, LANES), lambda b, j: (b, j + off))
    return pl.pallas_call(
        body, name="convmix_fwd", grid=(nbatch, nc),
        in_specs=[blk(GATE_B_BLOCK), blk(GATE_C_BLOCK), blk(U_BLOCK),
                  pl.BlockSpec((3, LANES), lambda b, j: (0, j)), pl.BlockSpec((1, LANES), lambda b, j: (0, j)),
                  pl.BlockSpec(memory_space=pl.ANY)],
        out_specs=blk(D_ATTN // LANES),
        out_shape=jax.ShapeDtypeStruct(cat.shape, cat.dtype),
        input_output_aliases={5: 0},
        compiler_params=_params(("parallel", "parallel")),
    )(proj, proj, proj, mcw, conv_g, cat)


def _convmix_bwd(proj, d_cat, d_proj, mcw, conv_g, nbatch, seq):
    nchunks = seq // MIX_ROWS

    def body(gb_ref, gc_ref, u_ref, dy_ref, w_ref, g_ref, dproj_in, dproj_ref, dw_ref, dg_ref, stage, sems):
        del dproj_in
        cb = pl.program_id(0)
        b = pl.program_id(1)
        gmat = _group_matrix(LANES)
        w = w_ref[...]
        gv = g_ref[...]
        cur = slice(HALO, HALO + MIX_ROWS)

        def step(c, carry):
            rows = pl.ds(pl.multiple_of(c * MIX_ROWS, MIX_ROWS), MIX_ROWS)
            gb = _window(gb_ref, c, MIX_ROWS, nchunks, True)
            gc = _window(gc_ref, c, MIX_ROWS, nchunks, True)
            u = _window(u_ref, c, MIX_ROWS, nchunks, True)
            dyn = _window(dy_ref, c, MIX_ROWS, nchunks, True)
            z = gc * u
            conv = _conv(z, w)
            y = gb * conv
            r = lax.rsqrt(_group_sum(y * y, gmat) * (1.0 / HEAD_DIM) + EPS)
            yh = y * r
            gd = dyn * gv
            dy = r * (gd - yh * (_group_sum(gd * yh, gmat) * (1.0 / HEAD_DIM)))
            dc = dy * gb
            dz = _conv_t(dc, w)
            stage[0, rows, :] = (dy * conv)[cur].astype(BF16)
            stage[1, rows, :] = (dz * u)[cur].astype(BF16)
            stage[2, rows, :] = (dz * gc)[cur].astype(BF16)
            dws = _conv_wgrad(dc, z, cur)
            dg = jnp.sum((dyn * yh)[cur], axis=0, keepdims=True)
            return tuple(a + d for a, d in zip(carry, dws + [dg]))

        zero = jnp.zeros((1, LANES), F32)
        dw0, dw1, dw2, dg = lax.fori_loop(0, nchunks, step, (zero, zero, zero, zero))

        @pl.when(b == 0)
        def _():
            dw_ref[0:1, :] = dw0
            dw_ref[1:2, :] = dw1
            dw_ref[2:3, :] = dw2
            dg_ref[...] = dg

        @pl.when(b > 0)
        def _():
            dw_ref[0:1, :] += dw0
            dw_ref[1:2, :] += dw1
            dw_ref[2:3, :] += dw2
            dg_ref[...] += dg

        _store_columns(stage, dproj_ref, sems, b * seq, seq, [GATE_B_BLOCK + cb, GATE_C_BLOCK + cb, U_BLOCK + cb])

    nc = D_CONV // LANES
    blk = lambda off: pl.BlockSpec((seq, LANES), lambda j, b: (b, j + off))
    return pl.pallas_call(
        body, name="convmix_bwd", grid=(nc, nbatch),
        in_specs=[blk(GATE_B_BLOCK), blk(GATE_C_BLOCK), blk(U_BLOCK), blk(D_ATTN // LANES),
                  pl.BlockSpec((3, LANES), lambda j, b: (0, j)), pl.BlockSpec((1, LANES), lambda j, b: (0, j)),
                  pl.BlockSpec(memory_space=pl.ANY)],
        out_specs=[pl.BlockSpec(memory_space=pl.ANY), pl.BlockSpec((3, LANES), lambda j, b: (0, j)),
                   pl.BlockSpec((1, LANES), lambda j, b: (0, j))],
        out_shape=[jax.ShapeDtypeStruct(d_proj.shape, d_proj.dtype), jax.ShapeDtypeStruct((3, D_CONV), F32),
                   jax.ShapeDtypeStruct((1, D_CONV), F32)],
        scratch_shapes=[pltpu.VMEM((3, seq, LANES), BF16), pltpu.SemaphoreType.DMA((3,))],
        input_output_aliases={6: 0},
        compiler_params=_params(("arbitrary", "arbitrary")),
    )(proj, proj, proj, d_cat, mcw, conv_g, d_proj)


FFN_ROWS = 128


def _ffn_act_fwd(pre, fcw, nbatch, seq):
    t = nbatch * seq
    nchunks = seq // FFN_ROWS

    def body(pre_ref, w_ref, act_ref):
        wa = w_ref[0]
        wc = w_ref[1]

        def step(c, carry):
            cur = pl.ds(pl.multiple_of(c * FFN_ROWS, FFN_ROWS), FFN_ROWS)
            a = _conv(_window(pre_ref.at[0], c, FFN_ROWS, nchunks, False), wa)[HALO:]
            v = _conv(_window(pre_ref.at[1], c, FFN_ROWS, nchunks, False), wc)[HALO:]
            act_ref[cur, :] = (a * jax.nn.sigmoid(a) * v).astype(BF16)
            return carry

        lax.fori_loop(0, nchunks, step, 0)

    return pl.pallas_call(
        body, name="ffn_act_fwd", grid=(N_UP_PAIRS, nbatch),
        in_specs=[pl.BlockSpec((2, None, seq, UP_CHUNK), lambda i, b: (0, i, b, 0)),
                  pl.BlockSpec((2, None, 3, UP_CHUNK), lambda i, b: (0, i, 0, 0))],
        out_specs=pl.BlockSpec((None, seq, UP_CHUNK), lambda i, b: (i, b, 0)),
        out_shape=jax.ShapeDtypeStruct((N_UP_PAIRS, t, UP_CHUNK), BF16),
        compiler_params=_params(("parallel", "parallel")),
    )(pre, fcw)


def _ffn_act_bwd(pre, d_act, fcw, nbatch, seq):
    nchunks = seq // FFN_ROWS

    def body(pre_ref, da_ref, w_ref, dpre_ref, dw_ref):
        b = pl.program_id(1)
        wa = w_ref[0]
        wc = w_ref[1]
        cur = slice(HALO, HALO + FFN_ROWS)

        def step(c, carry):
            rows = pl.ds(pl.multiple_of(c * FFN_ROWS, FFN_ROWS), FFN_ROWS)
            pg = _window(pre_ref.at[0], c, FFN_ROWS, nchunks, True)
            pv = _window(pre_ref.at[1], c, FFN_ROWS, nchunks, True)
            dact = _window(da_ref, c, FFN_ROWS, nchunks, True)
            a = _conv(pg, wa)
            v = _conv(pv, wc)
            sg = jax.nn.sigmoid(a)
            da = dact * v * (sg * (1.0 + a * (1.0 - sg)))
            dv = dact * (a * sg)
            dpre_ref[0, rows, :] = _conv_t(da, wa)[cur].astype(BF16)
            dpre_ref[1, rows, :] = _conv_t(dv, wc)[cur].astype(BF16)
            return tuple(acc + d for acc, d in zip(carry, _conv_wgrad(da, pg, cur) + _conv_wgrad(dv, pv, cur)))

        zero = jnp.zeros((1, UP_CHUNK), F32)
        sums = lax.fori_loop(0, nchunks, step, (zero,) * 6)

        @pl.when(b == 0)
        def _():
            for i in range(6):
                dw_ref[i // 3, pl.ds(i % 3, 1), :] = sums[i]

        @pl.when(b > 0)
        def _():
            for i in range(6):
                dw_ref[i // 3, pl.ds(i % 3, 1), :] += sums[i]

    pair = pl.BlockSpec((2, None, seq, UP_CHUNK), lambda i, b: (0, i, b, 0))
    wspec = pl.BlockSpec((2, None, 3, UP_CHUNK), lambda i, b: (0, i, 0, 0))
    return pl.pallas_call(
        body, name="ffn_act_bwd", grid=(N_UP_PAIRS, nbatch),
        in_specs=[pair, pl.BlockSpec((None, seq, UP_CHUNK), lambda i, b: (i, b, 0)), wspec],
        out_specs=[pair, wspec],
        out_shape=[jax.ShapeDtypeStruct(pre.shape, BF16), jax.ShapeDtypeStruct(fcw.shape, F32)],
        compiler_params=_params(("parallel", "arbitrary")),
    )(pre, d_act, fcw)


def _adamw(lands, w, m, v, row_tile, name):
    nl = len(lands)
    _, nr, ncol = lands[0].shape
    c1 = 1.0 - ADAM_B1 ** ADAM_STEP
    c2 = 1.0 - ADAM_B2 ** ADAM_STEP

    def body(*refs):
        land_refs = refs[:nl]
        w_ref, m_ref, v_ref, g_ref, d_ref, mo_ref, vo_ref = refs[nl:]
        for l in range(nl):
            @pl.when(pl.program_id(0) == l)
            def _(l=l):
                g = land_refs[l][0].astype(F32)
                for j in range(1, N_DEV):
                    g = g + land_refs[l][j].astype(F32)
                g_ref[...] = g

        g = g_ref[...]
        m2 = ADAM_B1 * m_ref[...] + (1.0 - ADAM_B1) * g
        v2 = ADAM_B2 * v_ref[...] + (1.0 - ADAM_B2) * (g * g)
        mo_ref[...] = m2
        vo_ref[...] = v2
        d_ref[...] = -ADAM_LR * ((m2 / c1) / (jnp.sqrt(v2 / c2) + ADAM_EPS) + ADAM_WD * w_ref[...])

    def land_spec(l):
        return pl.BlockSpec((N_DEV, row_tile, ncol), lambda k, i: (0, jnp.where(k == l, i, 0), 0))

    tile = pl.BlockSpec((None, row_tile, ncol), lambda k, i: (k, i, 0))
    return pl.pallas_call(
        body, name=name, grid=(nl, nr // row_tile),
        in_specs=[land_spec(l) for l in range(nl)] + [tile, tile, tile],
        out_specs=[tile] * 4,
        out_shape=[jax.ShapeDtypeStruct(w.shape, F32)] * 4,
        compiler_params=_params(("arbitrary", "arbitrary")),
    )(*lands, w, m, v)


class _Item:
    def __init__(self, src, chunked):
        self.src, self.chunked = src, chunked
        self.land_shape = (N_DEV,) + (src.shape[1:] if chunked else src.shape)


def _mesh_place():
    x, y, c = lax.axis_index("x"), lax.axis_index("y"), lax.axis_index("c")
    return x, y, c, 4 * x + 2 * y + c


def _flipped(x, y, c, k):
    px = 1 - x if k & 4 else x
    py = 1 - y if k & 2 else y
    pc = 1 - c if k & 1 else c
    return (px, py, pc), 4 * px + 2 * py + pc


PEER_ORDER = (2, 4, 6, 3, 5, 7, 1)


def _exchange(items, name):
    n = len(items)

    def body(*refs):
        srcs, lands = refs[:n], refs[n:2 * n]
        send, recv, local = refs[2 * n:]
        x, y, c, me = _mesh_place()

        def copy(i, k, chunk, slot, dev):
            src = srcs[i].at[chunk] if items[i].chunked else srcs[i]
            return pltpu.make_async_remote_copy(
                src_ref=src, dst_ref=lands[i].at[slot],
                send_sem=send.at[i, k - 1], recv_sem=recv.at[i, k - 1], device_id=dev, device_id_type=MESH)

        own = [pltpu.make_async_copy(srcs[i].at[me] if items[i].chunked else srcs[i], lands[i].at[me], local.at[i])
               for i in range(n)]
        for k in PEER_ORDER:
            dev, idx = _flipped(x, y, c, k)
            for i in range(n):
                copy(i, k, idx, me, dev).start()
        for cp in own:
            cp.start()
        for k in PEER_ORDER:
            dev, idx = _flipped(x, y, c, k)
            for i in range(n):
                copy(i, k, me, idx, dev).wait_recv()
        for k in PEER_ORDER:
            dev, idx = _flipped(x, y, c, k)
            for i in range(n):
                copy(i, k, idx, me, dev).wait_send()
        for cp in own:
            cp.wait()

    hbm = pl.BlockSpec(memory_space=pl.ANY)
    return pl.pallas_call(
        body, name=name,
        in_specs=[hbm] * n, out_specs=[hbm] * n,
        out_shape=[jax.ShapeDtypeStruct(it.land_shape, it.src.dtype) for it in items],
        scratch_shapes=[pltpu.SemaphoreType.DMA((n, N_DEV - 1)), pltpu.SemaphoreType.DMA((n, N_DEV - 1)),
                        pltpu.SemaphoreType.DMA((n,))],
        compiler_params=pltpu.CompilerParams(has_side_effects=True),
    )(*[it.src for it in items])


HBM_SPEC = pl.BlockSpec(memory_space=pltpu.HBM)
SEM_SPEC = pl.BlockSpec(memory_space=pltpu.SEMAPHORE)
DATAFLOW = pltpu.SideEffectType.DATAFLOW_SIDE_EFFECTING


def _exchange_start(items, name):
    n = len(items)

    def body(*refs):
        srcs, land_ins = refs[:n], refs[n:2 * n]
        outs = refs[2 * n:6 * n]
        token, local = refs[6 * n:]
        del land_ins
        x, y, c, me = _mesh_place()
        for k in PEER_ORDER:
            dev, idx = _flipped(x, y, c, k)
            for i in range(n):
                send, recv, _, land = outs[4 * i:4 * i + 4]
                src = srcs[i].at[idx] if items[i].chunked else srcs[i]
                pltpu.make_async_remote_copy(src_ref=src, dst_ref=land.at[me], send_sem=send, recv_sem=recv,
                                             device_id=dev, device_id_type=MESH).start()
        own = [pltpu.make_async_copy(srcs[i].at[me] if items[i].chunked else srcs[i], outs[4 * i + 3].at[me],
                                     local.at[i]) for i in range(n)]
        for cp in own:
            cp.start()
        for cp in own:
            cp.wait()
        token[...] = jnp.zeros_like(token)

    out_shape, out_specs, args, lands = [], [], [], []
    for it in items:
        out_shape += [pltpu.SemaphoreType.DMA(()), pltpu.SemaphoreType.DMA(()),
                      pltpu.HBM(it.src.shape, it.src.dtype), pltpu.HBM(it.land_shape, it.src.dtype)]
        out_specs += [SEM_SPEC, SEM_SPEC, HBM_SPEC, HBM_SPEC]
        args.append(pltpu.with_memory_space_constraint(it.src, pltpu.HBM))
        lands.append(pltpu.with_memory_space_constraint(lax.empty(it.land_shape, it.src.dtype), pltpu.HBM))
    outs = pl.pallas_call(
        body, name=name,
        in_specs=[HBM_SPEC] * (2 * n), out_specs=out_specs + [pl.BlockSpec(memory_space=pltpu.VMEM)],
        out_shape=out_shape + [jax.ShapeDtypeStruct((SUBLANES, LANES), F32)],
        scratch_shapes=[pltpu.SemaphoreType.DMA((n,))],
        input_output_aliases={**{i: 4 * i + 2 for i in range(n)}, **{n + i: 4 * i + 3 for i in range(n)}},
        compiler_params=pltpu.CompilerParams(has_side_effects=DATAFLOW),
    )(*args, *lands)
    return [tuple(outs[4 * i:4 * i + 4]) for i in range(n)], outs[-1]


def _exchange_wait(handles, after, name):
    n = len(handles)

    def body(*refs):
        x, y, c, _ = _mesh_place()
        for i in range(n):
            src, land, send, recv = refs[4 * i:4 * i + 4]
            del src
            seven = land.at[pl.ds(0, N_DEV - 1)]
            cp = pltpu.make_async_remote_copy(src_ref=seven, dst_ref=seven, send_sem=send, recv_sem=recv,
                                              device_id=(x, y, 1 - c), device_id_type=MESH)
            cp.wait_send()
            cp.wait_recv()

    args, in_specs, out_shape = [], [], []
    for send, recv, src, land in handles:
        args += [src, land, send, recv]
        in_specs += [HBM_SPEC, HBM_SPEC, SEM_SPEC, SEM_SPEC]
        out_shape += [pltpu.HBM(src.shape, src.dtype), pltpu.HBM(land.shape, land.dtype)]
    outs = pl.pallas_call(
        body, name=name,
        in_specs=in_specs + [pl.BlockSpec(memory_space=pl.ANY)] * len(after), out_specs=[HBM_SPEC] * (2 * n),
        out_shape=out_shape,
        input_output_aliases={**{4 * i: 2 * i for i in range(n)}, **{4 * i + 1: 2 * i + 1 for i in range(n)}},
        compiler_params=pltpu.CompilerParams(has_side_effects=DATAFLOW),
    )(*args, *after)
    return [outs[2 * i + 1] for i in range(n)]


TM = 1024
TM_ACC = 512


def kernel(x, norm1_g, w_in, mix_conv_w, attn_out_g, conv_out_g, w_out, norm2_g, ffn_up, ffn_conv_w, ffn_down, final_norm_g, loss_target, m_norm1_g, m_w_in, m_mix_conv_w, m_attn_out_g, m_conv_out_g, m_w_out, m_norm2_g, m_ffn_up, m_ffn_conv_w, m_ffn_down, m_final_norm_g, v_norm1_g, v_w_in, v_mix_conv_w, v_attn_out_g, v_conv_out_g, v_w_out, v_norm2_g, v_ffn_up, v_ffn_conv_w, v_ffn_down, v_final_norm_g):
    nbatch, seq, d = x.shape
    t = nbatch * seq
    nt, nta = t // TM, t // TM_ACC
    out_rows = D_MODEL // N_DEV
    down_rows = D_FF // N_DEV
    xf = x.reshape(t, d)
    target = loss_target.reshape(t, d)

    cw_local = jnp.concatenate([ffn_conv_w, mix_conv_w], axis=-1)
    items = [_Item(cw_local, False)]
    for l in range(DEPTH):
        items += [_Item(w_in[l].astype(BF16), False), _Item(w_out[l].astype(BF16), False),
                  _Item(ffn_up[l].astype(BF16), False), _Item(ffn_down[l].astype(BF16), False)]
    handles, _ = _exchange_start(items, "gather_start")
    win, wout, wup, wdown = [None] * DEPTH, [None] * DEPTH, [None] * DEPTH, [None] * DEPTH

    full = lambda i, j, k: (0, 0)

    saved = []
    xin = xf
    for l in range(DEPTH):
        h1 = _rms_fwd(xin, norm1_g[l][None], f"rms1_fwd_{l}")
        if l == 0:
            cw_all, win[0] = _exchange_wait([handles[0], handles[1]], [h1], "gather_wait_in_0")
            fcw = [cw_all[:, k, :, :UP_CHUNK].reshape(2, N_UP_PAIRS, 3, UP_CHUNK) for k in range(DEPTH)]
            mcw = [cw_all[:, k, :, UP_CHUNK:].transpose(1, 0, 2).reshape(3, D_CONV) for k in range(DEPTH)]
        proj = _matmul(
            h1, win[l], grid=(nt, N_DEV, 1), dims=NN, name=f"proj_{l}",
            a_spec=pl.BlockSpec((TM, D_MODEL), lambda i, j, k: (i, 0)),
            b_spec=pl.BlockSpec((None, D_MODEL, IN_CHUNK), lambda i, j, k: (j, 0, 0)),
            o_spec=pl.BlockSpec((TM, IN_CHUNK), lambda i, j, k: (i, j)), o_shape=(t, D_IN), o_dtype=F32)
        o, lse, cat = _attn_fwd(proj, attn_out_g[l][None], nbatch, seq)
        cat = _convmix_fwd(proj, cat, mcw[l], conv_out_g[l][None], nbatch, seq)
        if l == 0:
            (got,) = _exchange_wait([handles[2]], [cat], "gather_wait_out_0")
            wout[0] = got.reshape(D_MODEL, D_MODEL)
        xmid = _matmul(
            cat, wout[l], grid=(nta, 1, 1), dims=NN, name=f"mix_out_{l}",
            a_spec=pl.BlockSpec((TM_ACC, D_MODEL), lambda i, j, k: (i, 0)),
            b_spec=pl.BlockSpec((D_MODEL, D_MODEL), full),
            o_spec=pl.BlockSpec((TM_ACC, D_MODEL), lambda i, j, k: (i, 0)), o_shape=(t, D_MODEL), o_dtype=F32,
            res=xin, res_spec=pl.BlockSpec((TM_ACC, D_MODEL), lambda i, j, k: (i, 0)))
        h2 = _rms_fwd(xmid, norm2_g[l][None], f"rms2_fwd_{l}")
        if l == 0:
            wup[0], got = _exchange_wait([handles[3], handles[4]], [h2], "gather_wait_ffn_0")
            wdown[0] = got.reshape(N_UP_PAIRS, UP_CHUNK, D_MODEL)
        pre = _matmul(
            h2, wup[l], grid=(nt, N_DEV, 1), dims=NN, name=f"ffn_up_{l}",
            a_spec=pl.BlockSpec((TM, D_MODEL), lambda i, j, k: (i, 0)),
            b_spec=pl.BlockSpec((None, D_MODEL, UP_CHUNK), lambda i, j, k: (j, 0, 0)),
            o_spec=pl.BlockSpec((None, TM, UP_CHUNK), lambda i, j, k: (j, i, 0)),
            o_shape=(N_DEV, t, UP_CHUNK), o_dtype=BF16).reshape(2, N_UP_PAIRS, t, UP_CHUNK)
        act = _ffn_act_fwd(pre, fcw[l], nbatch, seq)
        if l == 0:
            win[1], got_out, wup[1], got_down = _exchange_wait(handles[5:9], [act], "gather_wait_1")
            wout[1] = got_out.reshape(D_MODEL, D_MODEL)
            wdown[1] = got_down.reshape(N_UP_PAIRS, UP_CHUNK, D_MODEL)
        xout = _matmul(
            act, wdown[l], grid=(nta, 1, N_UP_PAIRS), dims=NN, name=f"ffn_down_{l}",
            a_spec=pl.BlockSpec((None, TM_ACC, UP_CHUNK), lambda i, j, k: (k, i, 0)),
            b_spec=pl.BlockSpec((None, UP_CHUNK, D_MODEL), lambda i, j, k: (k, 0, 0)),
            o_spec=pl.BlockSpec((TM_ACC, D_MODEL), lambda i, j, k: (i, 0)), o_shape=(t, D_MODEL), o_dtype=F32,
            res=xmid, res_spec=pl.BlockSpec((TM_ACC, D_MODEL), lambda i, j, k: (i, 0)))
        saved.append((xin, h1, proj, o, lse, cat, xmid, h2, pre, act))
        xin = xout

    loss_part, dx, dxb, dgf = _loss_head(xin, final_norm_g[None], target, "loss_head")

    dg1, dg2, dga, dgc = [None] * DEPTH, [None] * DEPTH, [None] * DEPTH, [None] * DEPTH
    for l in reversed(range(DEPTH)):
        xin, h1, proj, o, lse, cat, xmid, h2, pre, act = saved[l]
        d_act = _matmul(
            dxb, wdown[l], grid=(nt, N_UP_PAIRS, 1), dims=NT, name=f"d_act_{l}",
            a_spec=pl.BlockSpec((TM, D_MODEL), lambda i, j, k: (i, 0)),
            b_spec=pl.BlockSpec((None, UP_CHUNK, D_MODEL), lambda i, j, k: (j, 0, 0)),
            o_spec=pl.BlockSpec((None, TM, UP_CHUNK), lambda i, j, k: (j, i, 0)),
            o_shape=(N_UP_PAIRS, t, UP_CHUNK), o_dtype=BF16)
        g_down = _matmul(
            act, dxb, grid=(N_UP_PAIRS, 1, nt), dims=TN, name=f"g_down_{l}",
            a_spec=pl.BlockSpec((None, TM, UP_CHUNK), lambda i, j, k: (i, k, 0)),
            b_spec=pl.BlockSpec((TM, D_MODEL), lambda i, j, k: (k, 0)),
            o_spec=pl.BlockSpec((None, UP_CHUNK, D_MODEL), lambda i, j, k: (i, 0, 0)),
            o_shape=(N_UP_PAIRS, UP_CHUNK, D_MODEL), o_dtype=BF16).reshape(N_DEV, down_rows, D_MODEL)
        d_pre, d_fcw = _ffn_act_bwd(pre, d_act, fcw[l], nbatch, seq)
        d_pre = d_pre.reshape(N_DEV, t, UP_CHUNK)
        dh2 = _matmul(
            d_pre, wup[l], grid=(nta, 1, N_DEV), dims=NT, name=f"d_h2_{l}",
            a_spec=pl.BlockSpec((None, TM_ACC, UP_CHUNK), lambda i, j, k: (k, i, 0)),
            b_spec=pl.BlockSpec((None, D_MODEL, UP_CHUNK), lambda i, j, k: (k, 0, 0)),
            o_spec=pl.BlockSpec((TM_ACC, D_MODEL), lambda i, j, k: (i, 0)), o_shape=(t, D_MODEL), o_dtype=F32)
        g_up = _matmul(
            h2, d_pre, grid=(1, N_DEV, nt), dims=TN, name=f"g_up_{l}",
            a_spec=pl.BlockSpec((TM, D_MODEL), lambda i, j, k: (k, 0)),
            b_spec=pl.BlockSpec((None, TM, UP_CHUNK), lambda i, j, k: (j, k, 0)),
            o_spec=pl.BlockSpec((None, D_MODEL, UP_CHUNK), lambda i, j, k: (j, 0, 0)),
            o_shape=(N_DEV, D_MODEL, UP_CHUNK), o_dtype=BF16)
        dxm, dxmb, dg2[l] = _rms_bwd(xmid, norm2_g[l][None], dh2, dx, f"rms2_bwd_{l}")
        d_cat = _matmul(
            dxmb, wout[l], grid=(nta, 1, 1), dims=NT, name=f"d_cat_{l}",
            a_spec=pl.BlockSpec((TM_ACC, D_MODEL), lambda i, j, k: (i, 0)),
            b_spec=pl.BlockSpec((D_MODEL, D_MODEL), full),
            o_spec=pl.BlockSpec((TM_ACC, D_MODEL), lambda i, j, k: (i, 0)), o_shape=(t, D_MODEL), o_dtype=BF16)
        g_out = _matmul(
            cat, dxmb, grid=(1, 1, nt), dims=TN, name=f"g_out_{l}",
            a_spec=pl.BlockSpec((TM, D_MODEL), lambda i, j, k: (k, 0)),
            b_spec=pl.BlockSpec((TM, D_MODEL), lambda i, j, k: (k, 0)),
            o_spec=pl.BlockSpec((D_MODEL, D_MODEL), full),
            o_shape=(D_MODEL, D_MODEL), o_dtype=BF16).reshape(N_DEV, out_rows, D_MODEL)
        attn_g = attn_out_g[l][None]
        if l == 0:
            early, tok = _exchange_start([_Item(g_out, True), _Item(g_up, True), _Item(g_down, True)], "scatter_start_0a")
            attn_g = attn_g + tok[0, 0]
        d_proj, dga[l] = _attn_bwd(proj, o, lse, d_cat, attn_g, nbatch, seq)
        d_proj, d_mcw, dgc[l] = _convmix_bwd(proj, d_cat, d_proj, mcw[l], conv_out_g[l][None], nbatch, seq)
        g_in = _matmul(
            h1, d_proj, grid=(1, N_DEV, nt), dims=TN, name=f"g_in_{l}",
            a_spec=pl.BlockSpec((TM, D_MODEL), lambda i, j, k: (k, 0)),
            b_spec=pl.BlockSpec((TM, IN_CHUNK), lambda i, j, k: (k, j)),
            o_spec=pl.BlockSpec((None, D_MODEL, IN_CHUNK), lambda i, j, k: (j, 0, 0)),
            o_shape=(N_DEV, D_MODEL, IN_CHUNK), o_dtype=BF16)
        g_cw = jnp.concatenate(
            [d_fcw.reshape(N_DEV, 3, UP_CHUNK), d_mcw.reshape(3, N_DEV, D_CONV // N_DEV).transpose(1, 0, 2)], axis=-1)
        if l == 0:
            late, tok = _exchange_start([_Item(g_in, True), _Item(g_cw, True)], "scatter_start_0b")
        else:
            upper, tok = _exchange_start(
                [_Item(g_in, True), _Item(g_out, True), _Item(g_up, True), _Item(g_down, True), _Item(g_cw, True)],
                "scatter_start_1")
        dh1 = _matmul(
            d_proj, win[l], grid=(nta, 1, N_DEV), dims=NT, name=f"d_h1_{l}",
            a_spec=pl.BlockSpec((TM_ACC, IN_CHUNK), lambda i, j, k: (i, k)),
            b_spec=pl.BlockSpec((None, D_MODEL, IN_CHUNK), lambda i, j, k: (k, 0, 0)),
            o_spec=pl.BlockSpec((TM_ACC, D_MODEL), lambda i, j, k: (i, 0)), o_shape=(t, D_MODEL), o_dtype=F32)
        dx, dxb, dg1[l] = _rms_bwd(xin, norm1_g[l][None] + tok[0, 0], dh1, dxm, f"rms1_bwd_{l}")

    def pack_small(n1, a, c, n2, f):
        return jnp.concatenate(
            [n1, n2, f[None], jnp.concatenate([a, c], axis=-1), jnp.zeros((1, D_MODEL), F32)], axis=0)[None]

    land_in1, land_out1, land_up1, land_down1, land_cw1 = _exchange_wait(upper, [dxb], "scatter_wait_1")
    land_out0, land_up0, land_down0 = _exchange_wait(early, [dxb], "scatter_wait_0a")
    res_out = _adamw([land_out0, land_out1], w_out, m_w_out, v_w_out, out_rows, "adamw_w_out")
    res_up = _adamw([land_up0, land_up1], ffn_up, m_ffn_up, v_ffn_up, 256, "adamw_ffn_up")
    res_down = _adamw([land_down0, land_down1], ffn_down, m_ffn_down, v_ffn_down, down_rows, "adamw_ffn_down")
    small = jnp.concatenate(
        [dg1[0], dg1[1], dg2[0], dg2[1], dgf,
         jnp.concatenate([dga[0], dgc[0]], axis=-1), jnp.concatenate([dga[1], dgc[1]], axis=-1),
         jnp.zeros((1, D_MODEL), F32)], axis=0)
    (land_small,) = _exchange([_Item(small, False)], "gather_gain_grads")
    res_small = _adamw(
        [land_small], pack_small(norm1_g, attn_out_g, conv_out_g, norm2_g, final_norm_g),
        pack_small(m_norm1_g, m_attn_out_g, m_conv_out_g, m_norm2_g, m_final_norm_g),
        pack_small(v_norm1_g, v_attn_out_g, v_conv_out_g, v_norm2_g, v_final_norm_g), SUBLANES, "adamw_gains")
    land_in0, land_cw0 = _exchange_wait(late, [res_small[0], res_down[0], res_up[0], res_out[0]], "scatter_wait_0b")
    res_in = _adamw([land_in0, land_in1], w_in, m_w_in, v_w_in, 256, "adamw_w_in")
    res_cw = _adamw(
        [land_cw0, land_cw1], cw_local, jnp.concatenate([m_ffn_conv_w, m_mix_conv_w], axis=-1),
        jnp.concatenate([v_ffn_conv_w, v_mix_conv_w], axis=-1), 3, "adamw_conv_w")

    loss = lax.psum(loss_part[0, 0], ("x", "y", "c"))

    def unpack(kind):
        s = res_small[kind][0]
        cwr = res_cw[kind]
        return (s[0:2], res_in[kind], cwr[..., UP_CHUNK:], s[5:7, :D_ATTN], s[5:7, D_ATTN:], res_out[kind],
                s[2:4], res_up[kind], cwr[..., :UP_CHUNK], res_down[kind], s[4])

    return (loss, dx.reshape(nbatch, seq, d), *unpack(0), *unpack(1), *unpack(2), *unpack(3))
```

```python
import math

import jax
import jax.numpy as jnp
from jax import lax
from jax.experimental import pallas as pl
from jax.experimental.pallas import tpu as pltpu

F32 = jnp.float32
BF16 = jnp.bfloat16

D_MODEL = 1024
D_ATTN = 512
D_CONV = 512
HEAD_DIM = 64
N_HEADS = 8
D_FF = 2816
DEPTH = 2
D_IN = 3 * D_ATTN + 3 * D_CONV
EPS = 1e-6
DILATIONS = (1, 4, 16)
BAND = 128
N_DEV = 8
IN_CHUNK = D_IN // N_DEV
UP_CHUNK = 2 * D_FF // N_DEV
N_UP_PAIRS = N_DEV // 2
CW_PACK = UP_CHUNK + D_CONV // N_DEV
ADAM_LR = 0.001
ADAM_B1 = 0.9
ADAM_B2 = 0.999
ADAM_EPS = 1e-08
ADAM_WD = 0.01
ADAM_STEP = 10
LANES = 128
SUBLANES = 8
VMEM_LIMIT = 56 * 1024 * 1024

NEG = -1e30
MESH = pl.DeviceIdType.MESH


def _params(sem=None, vmem=VMEM_LIMIT):
    return pltpu.CompilerParams(dimension_semantics=sem, vmem_limit_bytes=vmem)


NN = (((1,), (0,)), ((), ()))
NT = (((1,), (1,)), ((), ()))
TN = (((0,), (0,)), ((), ()))


def _matmul(a, b, *, grid, a_spec, b_spec, o_spec, o_shape, o_dtype, dims, name, res=None, res_spec=None):
    nk = grid[2]
    o_block = tuple(s for s in o_spec.block_shape if s is not None)

    def body(*refs):
        if res is None:
            a_ref, b_ref, o_ref, *scr = refs
            r_ref = None
        else:
            a_ref, b_ref, r_ref, o_ref, *scr = refs
        part = lax.dot_general(a_ref[...].astype(BF16), b_ref[...].astype(BF16), dims, preferred_element_type=F32)

        def finish(total):
            if r_ref is not None:
                total = total + r_ref[...]
            o_ref[...] = total.astype(o_dtype)

        if nk == 1:
            finish(part)
        else:
            acc = scr[0]
            k = pl.program_id(2)

            @pl.when(k == 0)
            def _():
                acc[...] = part

            @pl.when(k > 0)
            def _():
                acc[...] += part

            @pl.when(k == nk - 1)
            def _():
                finish(acc[...])

    in_specs = [a_spec, b_spec] + ([res_spec] if res is not None else [])
    args = (a, b) + ((res,) if res is not None else ())
    return pl.pallas_call(
        body, name=name, grid=grid, in_specs=in_specs, out_specs=o_spec,
        out_shape=jax.ShapeDtypeStruct(o_shape, o_dtype),
        scratch_shapes=[pltpu.VMEM(o_block, F32)] if nk > 1 else [],
        compiler_params=_params(("parallel", "parallel", "arbitrary")),
    )(*args)


ROW_TILE = 512


def _rms_fwd(x, g, name):
    t, d = x.shape

    def body(x_ref, g_ref, h_ref):
        xv = x_ref[...]
        r = lax.rsqrt(jnp.mean(xv * xv, axis=-1, keepdims=True) + EPS)
        h_ref[...] = (xv * r * g_ref[...]).astype(BF16)

    return pl.pallas_call(
        body, name=name, grid=(t // ROW_TILE,),
        in_specs=[pl.BlockSpec((ROW_TILE, d), lambda i: (i, 0)), pl.BlockSpec((1, d), lambda i: (0, 0))],
        out_specs=pl.BlockSpec((ROW_TILE, d), lambda i: (i, 0)),
        out_shape=jax.ShapeDtypeStruct((t, d), BF16),
        compiler_params=_params(("parallel",)),
    )(x, g)


def _rms_bwd(x, g, dh, dres, name):
    t, d = x.shape

    def body(x_ref, g_ref, dh_ref, dres_ref, dx_ref, dxb_ref, dg_ref):
        xv = x_ref[...]
        r = lax.rsqrt(jnp.mean(xv * xv, axis=-1, keepdims=True) + EPS)
        xh = xv * r
        dhv = dh_ref[...]
        gd = dhv * g_ref[...]
        dx = r * (gd - xh * jnp.mean(gd * xh, axis=-1, keepdims=True)) + dres_ref[...]
        dx_ref[...] = dx
        dxb_ref[...] = dx.astype(BF16)
        part = jnp.sum(dhv * xh, axis=0, keepdims=True)

        @pl.when(pl.program_id(0) == 0)
        def _():
            dg_ref[...] = part

        @pl.when(pl.program_id(0) > 0)
        def _():
            dg_ref[...] += part

    row = pl.BlockSpec((ROW_TILE, d), lambda i: (i, 0))
    vec = pl.BlockSpec((1, d), lambda i: (0, 0))
    return pl.pallas_call(
        body, name=name, grid=(t // ROW_TILE,),
        in_specs=[row, vec, row, row], out_specs=[row, row, vec],
        out_shape=[jax.ShapeDtypeStruct((t, d), F32), jax.ShapeDtypeStruct((t, d), BF16),
                   jax.ShapeDtypeStruct((1, d), F32)],
        compiler_params=_params(("arbitrary",)),
    )(x, g, dh, dres)


def _loss_head(x, g, target, name):
    t, d = x.shape

    def body(x_ref, g_ref, t_ref, loss_ref, dx_ref, dxb_ref, dg_ref):
        xv = x_ref[...]
        r = lax.rsqrt(jnp.mean(xv * xv, axis=-1, keepdims=True) + EPS)
        xh = xv * r
        gv = g_ref[...]
        err = xh * gv - t_ref[...]
        loss = jnp.full((1, LANES), 0.5 / d, F32) * jnp.sum(err * err)
        dy = err * (1.0 / d)
        gd = dy * gv
        dx = r * (gd - xh * jnp.mean(gd * xh, axis=-1, keepdims=True))
        dx_ref[...] = dx
        dxb_ref[...] = dx.astype(BF16)
        part = jnp.sum(dy * xh, axis=0, keepdims=True)

        @pl.when(pl.program_id(0) == 0)
        def _():
            dg_ref[...] = part
            loss_ref[...] = loss

        @pl.when(pl.program_id(0) > 0)
        def _():
            dg_ref[...] += part
            loss_ref[...] += loss

    row = pl.BlockSpec((ROW_TILE, d), lambda i: (i, 0))
    vec = pl.BlockSpec((1, d), lambda i: (0, 0))
    return pl.pallas_call(
        body, name=name, grid=(t // ROW_TILE,),
        in_specs=[row, vec, row],
        out_specs=[pl.BlockSpec((1, LANES), lambda i: (0, 0)), row, row, vec],
        out_shape=[jax.ShapeDtypeStruct((1, LANES), F32), jax.ShapeDtypeStruct((t, d), F32),
                   jax.ShapeDtypeStruct((t, d), BF16), jax.ShapeDtypeStruct((1, d), F32)],
        compiler_params=_params(("arbitrary",)),
    )(x, g, target)


def _group_matrix(n):
    shift = int(math.log2(HEAD_DIM))
    r = lax.broadcasted_iota(jnp.int32, (n, n), 0) >> shift
    c = lax.broadcasted_iota(jnp.int32, (n, n), 1) >> shift
    return (r == c).astype(BF16)


def _group_sum(v, gmat):
    hi = v.astype(BF16)
    rest = v - hi.astype(F32)
    mid = rest.astype(BF16)
    lo = (rest - mid.astype(F32)).astype(BF16)

    def dot(p):
        return jnp.dot(p, gmat, preferred_element_type=F32)

    return dot(hi) + dot(mid) + dot(lo)


def _shift_rows(ext, k):
    return pltpu.roll(ext, k % ext.shape[0], 0)


def _store_columns(stage, out_hbm, sems, row0, nrows, col_blocks):
    rows = pl.ds(pl.multiple_of(row0, SUBLANES * 2), nrows)
    copies = [
        pltpu.make_async_copy(stage.at[i], out_hbm.at[rows, pl.ds(pl.multiple_of(cb * LANES, LANES), LANES)], sems.at[i])
        for i, cb in enumerate(col_blocks)
    ]
    for cp in copies:
        cp.start()
    for cp in copies:
        cp.wait()


def _attn_consts(width):
    i = lax.broadcasted_iota(jnp.int32, (BAND, width), 0)
    j = lax.broadcasted_iota(jnp.int32, (BAND, width), 1)
    dist = (width - BAND) + i - j
    inwin = (dist >= 0) & (dist <= BAND)
    return dist.astype(F32), inwin, j


def _head_masks():
    lane = lax.broadcasted_iota(jnp.int32, (1, LANES), 1)
    return [(lane < HEAD_DIM).astype(F32), (lane >= HEAD_DIM).astype(F32)]


def _permute_in(src_ref, dst_ref, dil, seq):
    length = seq // dil
    for r in range(dil):
        dst_ref[pl.ds(r * length, length), :] = src_ref[pl.ds(r, length, stride=dil), :].astype(dst_ref.dtype)


def _slopes_table():
    slopes = 2.0 ** (-8.0 * jnp.arange(1, N_HEADS + 1, dtype=F32) / N_HEADS)
    return jnp.broadcast_to(slopes[:, None], (N_HEADS, 2 * BAND))


def _attn_fwd(proj, attn_g, nbatch, seq):
    t = nbatch * seq
    nblk = seq // BAND
    scale = HEAD_DIM ** -0.5

    def body(q_ref, k_ref, v_ref, g_ref, sl_ref, o_ref, lse_ref, cat_ref, pq, pk, pv, po, pm, pll, ao, am, al):
        hp = pl.program_id(1)
        hmask = _head_masks()
        slope = [sl_ref[pl.ds(2 * hp + hh, 1), :] for hh in range(2)]

        def run_branch(dil, qs, ks, vs, osink, msink, lsink):
            nb = seq // dil // BAND
            width = 2 * BAND if nb > 1 else BAND
            distf, inwin, jcol = _attn_consts(width)
            bias = [distf * (slope[hh][:, :width] * (-float(dil))) for hh in range(2)]

            def blk(m, carry):
                row0 = pl.multiple_of(m * BAND, BAND)
                q = qs[pl.ds(row0, BAND), :] * scale
                if nb > 1:
                    prow = pl.multiple_of(jnp.maximum(m - 1, 0) * BAND, BAND)
                    kk = jnp.concatenate([ks[pl.ds(prow, BAND), :], ks[pl.ds(row0, BAND), :]], axis=0)
                    vv = jnp.concatenate([vs[pl.ds(prow, BAND), :], vs[pl.ds(row0, BAND), :]], axis=0)
                    valid = inwin & (jcol >= jnp.where((m % nb) == 0, BAND, 0))
                else:
                    kk = ks[pl.ds(row0, BAND), :]
                    vv = vs[pl.ds(row0, BAND), :]
                    valid = inwin
                kb = kk.astype(BF16)
                o = jnp.zeros((BAND, LANES), F32)
                mfull = jnp.zeros((BAND, LANES), F32)
                lfull = jnp.zeros((BAND, LANES), F32)
                for hh in range(2):
                    qh = (q * hmask[hh]).astype(BF16)
                    s = lax.dot_general(qh, kb, NT, preferred_element_type=F32)
                    s = jnp.where(valid, s + bias[hh], NEG)
                    mh = jnp.max(s, axis=1, keepdims=True)
                    p = jnp.exp(s - mh)
                    lh = jnp.sum(p, axis=1, keepdims=True)
                    o = o + jnp.dot(p.astype(BF16), (vv * hmask[hh]).astype(BF16), preferred_element_type=F32)
                    mfull = mfull + mh * hmask[hh]
                    lfull = lfull + lh * hmask[hh]
                osink[pl.ds(row0, BAND), :] = o
                msink[pl.ds(row0, BAND), :] = mfull
                lsink[pl.ds(row0, BAND), :] = lfull
                return carry

            lax.fori_loop(0, nblk, blk, 0)

        run_branch(1, q_ref, k_ref, v_ref, ao, am, al)
        for dil in DILATIONS[1:]:
            length = seq // dil
            _permute_in(q_ref, pq, dil, seq)
            _permute_in(k_ref, pk, dil, seq)
            _permute_in(v_ref, pv, dil, seq)
            run_branch(dil, pq, pk, pv, po, pm, pll)
            for r in range(dil):
                nat = pl.ds(r, length, stride=dil)
                per = pl.ds(r * length, length)
                m0 = am[nat, :]
                mb = pm[per, :]
                mn = jnp.maximum(m0, mb)
                e0 = jnp.exp(m0 - mn)
                eb = jnp.exp(mb - mn)
                ao[nat, :] = ao[nat, :] * e0 + po[per, :] * eb
                al[nat, :] = al[nat, :] * e0 + pll[per, :] * eb
                am[nat, :] = mn

        gmat = _group_matrix(LANES)
        gv = g_ref[...]

        def fin(c, carry):
            rows = pl.ds(pl.multiple_of(c * 256, 256), 256)
            lv = al[rows, :]
            o = ao[rows, :] / lv
            o_ref[rows, :] = o
            lse_ref[rows, :] = am[rows, :] + jnp.log(lv)
            ms = _group_sum(o * o, gmat) * (1.0 / HEAD_DIM)
            cat_ref[rows, :] = (o * lax.rsqrt(ms + EPS) * gv).astype(BF16)
            return carry

        lax.fori_loop(0, seq // 256, fin, 0)

    nq = D_ATTN // LANES
    blk = lambda off: pl.BlockSpec((seq, LANES), lambda b, h: (b, h + off))
    scratch = [pltpu.VMEM((seq, LANES), F32) for _ in range(9)]
    return pl.pallas_call(
        body, name="attn_fwd", grid=(nbatch, nq),
        in_specs=[blk(0), blk(nq), blk(2 * nq), pl.BlockSpec((1, LANES), lambda b, h: (0, h)),
                  pl.BlockSpec((N_HEADS, 2 * BAND), lambda b, h: (0, 0))],
        out_specs=[blk(0), blk(0), blk(0)],
        out_shape=[jax.ShapeDtypeStruct((t, D_ATTN), F32), jax.ShapeDtypeStruct((t, D_ATTN), F32),
                   jax.ShapeDtypeStruct((t, D_MODEL), BF16)],
        scratch_shapes=scratch,
        compiler_params=_params(("parallel", "parallel")),
    )(proj, proj, proj, attn_g, _slopes_table())


def _attn_bwd(proj, o, lse, d_cat, attn_g, nbatch, seq):
    t = nbatch * seq
    nblk = seq // BAND
    scale = HEAD_DIM ** -0.5

    def body(q_ref, k_ref, v_ref, o_ref, lse_ref, dy_ref, g_ref, sl_ref, dproj_ref, dg_ref,
             do_n, dl_n, dq_n, dk_n, dv_n, pq, pk, pv, pdo, plse, pdl, pdq, pdk, pdv, stage, sems):
        hp = pl.program_id(0)
        hmask = _head_masks()
        slope = [sl_ref[pl.ds(2 * hp + hh, 1), :] for hh in range(2)]
        gmat = _group_matrix(LANES)
        gv = g_ref[...]

        def prep(c, dg):
            rows = pl.ds(pl.multiple_of(c * 256, 256), 256)
            ov = o_ref[rows, :]
            dyn = dy_ref[rows, :].astype(F32)
            r = lax.rsqrt(_group_sum(ov * ov, gmat) * (1.0 / HEAD_DIM) + EPS)
            gd = dyn * gv
            oh = ov * r
            do = r * (gd - oh * (_group_sum(gd * oh, gmat) * (1.0 / HEAD_DIM)))
            do_n[rows, :] = do
            dl_n[rows, :] = _group_sum(do * ov, gmat)
            return dg + jnp.sum(dyn * oh, axis=0, keepdims=True)

        dg = lax.fori_loop(0, seq // 256, prep, jnp.zeros((1, LANES), F32))

        @pl.when(pl.program_id(1) == 0)
        def _():
            dg_ref[...] = dg

        @pl.when(pl.program_id(1) > 0)
        def _():
            dg_ref[...] += dg

        def clear(*refs):
            def step(c, carry):
                rows = pl.ds(pl.multiple_of(c * 256, 256), 256)
                for ref in refs:
                    ref[rows, :] = jnp.zeros((256, LANES), F32)
                return carry

            lax.fori_loop(0, seq // 256, step, 0)

        clear(dq_n, dk_n, dv_n)

        def run_branch(dil, qs, ks, vs, dos, lses, dls, dqs, dks, dvs):
            nb = seq // dil // BAND
            width = 2 * BAND if nb > 1 else BAND
            distf, inwin, jcol = _attn_consts(width)
            bias = [distf * (slope[hh][:, :width] * (-float(dil))) for hh in range(2)]

            def blk(m, carry):
                row0 = pl.multiple_of(m * BAND, BAND)
                cur = pl.ds(row0, BAND)
                q = qs[cur, :] * scale
                dov = dos[cur, :]
                lsev = lses[cur, :]
                dlv = dls[cur, :]
                if nb > 1:
                    prow = pl.multiple_of(jnp.maximum(m - 1, 0) * BAND, BAND)
                    prev = pl.ds(prow, BAND)
                    kk = jnp.concatenate([ks[prev, :], ks[cur, :]], axis=0)
                    vv = jnp.concatenate([vs[prev, :], vs[cur, :]], axis=0)
                    valid = inwin & (jcol >= jnp.where((m % nb) == 0, BAND, 0))
                else:
                    kk = ks[cur, :]
                    vv = vs[cur, :]
                    valid = inwin
                kb = kk.astype(BF16)
                vb = vv.astype(BF16)
                dq = jnp.zeros((BAND, LANES), F32)
                dkk = jnp.zeros((width, LANES), F32)
                dvv = jnp.zeros((width, LANES), F32)
                for hh in range(2):
                    c0 = hh * HEAD_DIM
                    qh = (q * hmask[hh]).astype(BF16)
                    doh = (dov * hmask[hh]).astype(BF16)
                    s = lax.dot_general(qh, kb, NT, preferred_element_type=F32) + bias[hh]
                    p = jnp.where(valid, jnp.exp(s - lsev[:, c0:c0 + 1]), 0.0)
                    dp = lax.dot_general(doh, vb, NT, preferred_element_type=F32)
                    ds = (p * (dp - dlv[:, c0:c0 + 1])).astype(BF16)
                    dq = dq + jnp.dot(ds, kb, preferred_element_type=F32) * hmask[hh]
                    dkk = dkk + lax.dot_general(ds, qh, TN, preferred_element_type=F32)
                    dvv = dvv + lax.dot_general(p.astype(BF16), doh, TN, preferred_element_type=F32)
                dqs[cur, :] += dq
                if nb > 1:
                    dks[prev, :] += dkk[:BAND]
                    dvs[prev, :] += dvv[:BAND]
                    dks[cur, :] += dkk[BAND:]
                    dvs[cur, :] += dvv[BAND:]
                else:
                    dks[cur, :] += dkk
                    dvs[cur, :] += dvv
                return carry

            lax.fori_loop(0, nblk, blk, 0)

        run_branch(1, q_ref, k_ref, v_ref, do_n, lse_ref, dl_n, dq_n, dk_n, dv_n)
        for dil in DILATIONS[1:]:
            length = seq // dil
            for src, dst in ((q_ref, pq), (k_ref, pk), (v_ref, pv), (do_n, pdo), (lse_ref, plse), (dl_n, pdl)):
                _permute_in(src, dst, dil, seq)
            clear(pdq, pdk, pdv)
            run_branch(dil, pq, pk, pv, pdo, plse, pdl, pdq, pdk, pdv)
            for r in range(dil):
                nat = pl.ds(r, length, stride=dil)
                per = pl.ds(r * length, length)
                dq_n[nat, :] += pdq[per, :]
                dk_n[nat, :] += pdk[per, :]
                dv_n[nat, :] += pdv[per, :]

        def emit(c, carry):
            rows = pl.ds(pl.multiple_of(c * 256, 256), 256)
            stage[0, rows, :] = (dq_n[rows, :] * scale).astype(BF16)
            stage[1, rows, :] = dk_n[rows, :].astype(BF16)
            stage[2, rows, :] = dv_n[rows, :].astype(BF16)
            return carry

        lax.fori_loop(0, seq // 256, emit, 0)
        _store_columns(stage, dproj_ref, sems, pl.program_id(1) * seq, seq, [hp, nq + hp, 2 * nq + hp])

    nq = D_ATTN // LANES
    blk = lambda off: pl.BlockSpec((seq, LANES), lambda h, b: (b, h + off))
    vec = pl.BlockSpec((1, LANES), lambda h, b: (0, h))
    scratch = [pltpu.VMEM((seq, LANES), F32) for _ in range(14)]
    scratch += [pltpu.VMEM((3, seq, LANES), BF16), pltpu.SemaphoreType.DMA((3,))]
    d_proj, dg = pl.pallas_call(
        body, name="attn_bwd", grid=(nq, nbatch),
        in_specs=[blk(0), blk(nq), blk(2 * nq), blk(0), blk(0), blk(0), vec,
                  pl.BlockSpec((N_HEADS, 2 * BAND), lambda h, b: (0, 0))],
        out_specs=[pl.BlockSpec(memory_space=pl.ANY), vec],
        out_shape=[jax.ShapeDtypeStruct((t, D_IN), BF16), jax.ShapeDtypeStruct((1, D_ATTN), F32)],
        scratch_shapes=scratch,
        compiler_params=_params(("arbitrary", "arbitrary")),
    )(proj, proj, proj, o, lse, d_cat, attn_g, _slopes_table())
    return d_proj, dg


HALO = 2 * SUBLANES


def _window(ref, c, rows, nchunks, after):
    row0 = pl.multiple_of(c * rows, rows)
    prev0 = pl.multiple_of(jnp.maximum(row0 - HALO, 0), HALO)
    parts = [ref[pl.ds(prev0, HALO), :].astype(F32) * (c > 0).astype(F32), ref[pl.ds(row0, rows), :].astype(F32)]
    if after:
        next0 = pl.multiple_of(jnp.minimum(row0 + rows, (nchunks - 1) * rows), HALO)
        parts.append(ref[pl.ds(next0, HALO), :].astype(F32) * (c < nchunks - 1).astype(F32))
    return jnp.concatenate(parts, axis=0)


def _conv(z, w):
    return w[0:1] * _shift_rows(z, 2) + w[1:2] * _shift_rows(z, 1) + w[2:3] * z


def _conv_t(dy, w):
    return w[2:3] * dy + w[1:2] * _shift_rows(dy, -1) + w[0:1] * _shift_rows(dy, -2)


def _conv_wgrad(dy, z, cur):
    return [jnp.sum((dy * _shift_rows(z, 2 - k))[cur], axis=0, keepdims=True) for k in range(3)]


MIX_ROWS = 256
GATE_B_BLOCK = 3 * D_ATTN // LANES
GATE_C_BLOCK = GATE_B_BLOCK + D_CONV // LANES
U_BLOCK = GATE_C_BLOCK + D_CONV // LANES


def _convmix_fwd(proj, cat, mcw, conv_g, nbatch, seq):
    nchunks = seq // MIX_ROWS

    def body(gb_ref, gc_ref, u_ref, w_ref, g_ref, cat_in, cat_ref):
        del cat_in
        gmat = _group_matrix(LANES)
        w = w_ref[...]
        gv = g_ref[...]

        def step(c, carry):
            cur = pl.ds(pl.multiple_of(c * MIX_ROWS, MIX_ROWS), MIX_ROWS)
            z = _window(gc_ref, c, MIX_ROWS, nchunks, False) * _window(u_ref, c, MIX_ROWS, nchunks, False)
            y = gb_ref[cur, :] * _conv(z, w)[HALO:]
            ms = _group_sum(y * y, gmat) * (1.0 / HEAD_DIM)
            cat_ref[cur, :] = (y * lax.rsqrt(ms + EPS) * gv).astype(BF16)
            return carry

        lax.fori_loop(0, nchunks, step, 0)

    nc = D_CONV // LANES
    blk = lambda off: pl.BlockSpec((seq, LANES), lambda b, j: (b, j + off))
    return pl.pallas_call(
        body, name="convmix_fwd", grid=(nbatch, nc),
        in_specs=[blk(GATE_B_BLOCK), blk(GATE_C_BLOCK), blk(U_BLOCK),
                  pl.BlockSpec((3, LANES), lambda b, j: (0, j)), pl.BlockSpec((1, LANES), lambda b, j: (0, j)),
                  pl.BlockSpec(memory_space=pl.ANY)],
        out_specs=blk(D_ATTN // LANES),
        out_shape=jax.ShapeDtypeStruct(cat.shape, cat.dtype),
        input_output_aliases={5: 0},
        compiler_params=_params(("parallel", "parallel")),
    )(proj, proj, proj, mcw, conv_g, cat)


def _convmix_bwd(proj, d_cat, d_proj, mcw, conv_g, nbatch, seq):
    nchunks = seq // MIX_ROWS

    def body(gb_ref, gc_ref, u_ref, dy_ref, w_ref, g_ref, dproj_in, dproj_ref, dw_ref, dg_ref, stage, sems):
        del dproj_in
        cb = pl.program_id(0)
        b = pl.program_id(1)
        gmat = _group_matrix(LANES)
        w = w_ref[...]
        gv = g_ref[...]
        cur = slice(HALO, HALO + MIX_ROWS)

        def step(c, carry):
            rows = pl.ds(pl.multiple_of(c * MIX_ROWS, MIX_ROWS), MIX_ROWS)
            gb = _window(gb_ref, c, MIX_ROWS, nchunks, True)
            gc = _window(gc_ref, c, MIX_ROWS, nchunks, True)
            u = _window(u_ref, c, MIX_ROWS, nchunks, True)
            dyn = _window(dy_ref, c, MIX_ROWS, nchunks, True)
            z = gc * u
            conv = _conv(z, w)
            y = gb * conv
            r = lax.rsqrt(_group_sum(y * y, gmat) * (1.0 / HEAD_DIM) + EPS)
            yh = y * r
            gd = dyn * gv
            dy = r * (gd - yh * (_group_sum(gd * yh, gmat) * (1.0 / HEAD_DIM)))
            dc = dy * gb
            dz = _conv_t(dc, w)
            stage[0, rows, :] = (dy * conv)[cur].astype(BF16)
            stage[1, rows, :] = (dz * u)[cur].astype(BF16)
            stage[2, rows, :] = (dz * gc)[cur].astype(BF16)
            dws = _conv_wgrad(dc, z, cur)
            dg = jnp.sum((dyn * yh)[cur], axis=0, keepdims=True)
            return tuple(a + d for a, d in zip(carry, dws + [dg]))

        zero = jnp.zeros((1, LANES), F32)
        dw0, dw1, dw2, dg = lax.fori_loop(0, nchunks, step, (zero, zero, zero, zero))

        @pl.when(b == 0)
        def _():
            dw_ref[0:1, :] = dw0
            dw_ref[1:2, :] = dw1
            dw_ref[2:3, :] = dw2
            dg_ref[...] = dg

        @pl.when(b > 0)
        def _():
            dw_ref[0:1, :] += dw0
            dw_ref[1:2, :] += dw1
            dw_ref[2:3, :] += dw2
            dg_ref[...] += dg

        _store_columns(stage, dproj_ref, sems, b * seq, seq, [GATE_B_BLOCK + cb, GATE_C_BLOCK + cb, U_BLOCK + cb])

    nc = D_CONV // LANES
    blk = lambda off: pl.BlockSpec((seq, LANES), lambda j, b: (b, j + off))
    return pl.pallas_call(
        body, name="convmix_bwd", grid=(nc, nbatch),
        in_specs=[blk(GATE_B_BLOCK), blk(GATE_C_BLOCK), blk(U_BLOCK), blk(D_ATTN // LANES),
                  pl.BlockSpec((3, LANES), lambda j, b: (0, j)), pl.BlockSpec((1, LANES), lambda j, b: (0, j)),
                  pl.BlockSpec(memory_space=pl.ANY)],
        out_specs=[pl.BlockSpec(memory_space=pl.ANY), pl.BlockSpec((3, LANES), lambda j, b: (0, j)),
                   pl.BlockSpec((1, LANES), lambda j, b: (0, j))],
        out_shape=[jax.ShapeDtypeStruct(d_proj.shape, d_proj.dtype), jax.ShapeDtypeStruct((3, D_CONV), F32),
                   jax.ShapeDtypeStruct((1, D_CONV), F32)],
        scratch_shapes=[pltpu.VMEM((3, seq, LANES), BF16), pltpu.SemaphoreType.DMA((3,))],
        input_output_aliases={6: 0},
        compiler_params=_params(("arbitrary", "arbitrary")),
    )(proj, proj, proj, d_cat, mcw, conv_g, d_proj)


FFN_ROWS = 128


def _ffn_act_fwd(pre, fcw, nbatch, seq):
    t = nbatch * seq
    nchunks = seq // FFN_ROWS

    def body(pre_ref, w_ref, act_ref):
        wa = w_ref[0]
        wc = w_ref[1]

        def step(c, carry):
            cur = pl.ds(pl.multiple_of(c * FFN_ROWS, FFN_ROWS), FFN_ROWS)
            a = _conv(_window(pre_ref.at[0], c, FFN_ROWS, nchunks, False), wa)[HALO:]
            v = _conv(_window(pre_ref.at[1], c, FFN_ROWS, nchunks, False), wc)[HALO:]
            act_ref[cur, :] = (a * jax.nn.sigmoid(a) * v).astype(BF16)
            return carry

        lax.fori_loop(0, nchunks, step, 0)

    return pl.pallas_call(
        body, name="ffn_act_fwd", grid=(N_UP_PAIRS, nbatch),
        in_specs=[pl.BlockSpec((2, None, seq, UP_CHUNK), lambda i, b: (0, i, b, 0)),
                  pl.BlockSpec((2, None, 3, UP_CHUNK), lambda i, b: (0, i, 0, 0))],
        out_specs=pl.BlockSpec((None, seq, UP_CHUNK), lambda i, b: (i, b, 0)),
        out_shape=jax.ShapeDtypeStruct((N_UP_PAIRS, t, UP_CHUNK), BF16),
        compiler_params=_params(("parallel", "parallel")),
    )(pre, fcw)


def _ffn_act_bwd(pre, d_act, fcw, nbatch, seq):
    nchunks = seq // FFN_ROWS

    def body(pre_ref, da_ref, w_ref, dpre_ref, dw_ref):
        b = pl.program_id(1)
        wa = w_ref[0]
        wc = w_ref[1]
        cur = slice(HALO, HALO + FFN_ROWS)

        def step(c, carry):
            rows = pl.ds(pl.multiple_of(c * FFN_ROWS, FFN_ROWS), FFN_ROWS)
            pg = _window(pre_ref.at[0], c, FFN_ROWS, nchunks, True)
            pv = _window(pre_ref.at[1], c, FFN_ROWS, nchunks, True)
            dact = _window(da_ref, c, FFN_ROWS, nchunks, True)
            a = _conv(pg, wa)
            v = _conv(pv, wc)
            sg = jax.nn.sigmoid(a)
            da = dact * v * (sg * (1.0 + a * (1.0 - sg)))
            dv = dact * (a * sg)
            dpre_ref[0, rows, :] = _conv_t(da, wa)[cur].astype(BF16)
            dpre_ref[1, rows, :] = _conv_t(dv, wc)[cur].astype(BF16)
            return tuple(acc + d for acc, d in zip(carry, _conv_wgrad(da, pg, cur) + _conv_wgrad(dv, pv, cur)))

        zero = jnp.zeros((1, UP_CHUNK), F32)
        sums = lax.fori_loop(0, nchunks, step, (zero,) * 6)

        @pl.when(b == 0)
        def _():
            for i in range(6):
                dw_ref[i // 3, pl.ds(i % 3, 1), :] = sums[i]

        @pl.when(b > 0)
        def _():
            for i in range(6):
                dw_ref[i // 3, pl.ds(i % 3, 1), :] += sums[i]

    pair = pl.BlockSpec((2, None, seq, UP_CHUNK), lambda i, b: (0, i, b, 0))
    wspec = pl.BlockSpec((2, None, 3, UP_CHUNK), lambda i, b: (0, i, 0, 0))
    return pl.pallas_call(
        body, name="ffn_act_bwd", grid=(N_UP_PAIRS, nbatch),
        in_specs=[pair, pl.BlockSpec((None, seq, UP_CHUNK), lambda i, b: (i, b, 0)), wspec],
        out_specs=[pair, wspec],
        out_shape=[jax.ShapeDtypeStruct(pre.shape, BF16), jax.ShapeDtypeStruct(fcw.shape, F32)],
        compiler_params=_params(("parallel", "arbitrary")),
    )(pre, d_act, fcw)


def _adamw(lands, w, m, v, row_tile, name):
    nl = len(lands)
    _, nr, ncol = lands[0].shape
    c1 = 1.0 - ADAM_B1 ** ADAM_STEP
    c2 = 1.0 - ADAM_B2 ** ADAM_STEP

    def body(*refs):
        land_refs = refs[:nl]
        w_ref, m_ref, v_ref, g_ref, d_ref, mo_ref, vo_ref = refs[nl:]
        for l in range(nl):
            @pl.when(pl.program_id(0) == l)
            def _(l=l):
                g = land_refs[l][0].astype(F32)
                for j in range(1, N_DEV):
                    g = g + land_refs[l][j].astype(F32)
                g_ref[...] = g

        g = g_ref[...]
        m2 = ADAM_B1 * m_ref[...] + (1.0 - ADAM_B1) * g
        v2 = ADAM_B2 * v_ref[...] + (1.0 - ADAM_B2) * (g * g)
        mo_ref[...] = m2
        vo_ref[...] = v2
        d_ref[...] = -ADAM_LR * ((m2 / c1) / (jnp.sqrt(v2 / c2) + ADAM_EPS) + ADAM_WD * w_ref[...])

    def land_spec(l):
        return pl.BlockSpec((N_DEV, row_tile, ncol), lambda k, i: (0, jnp.where(k == l, i, 0), 0))

    tile = pl.BlockSpec((None, row_tile, ncol), lambda k, i: (k, i, 0))
    return pl.pallas_call(
        body, name=name, grid=(nl, nr // row_tile),
        in_specs=[land_spec(l) for l in range(nl)] + [tile, tile, tile],
        out_specs=[tile] * 4,
        out_shape=[jax.ShapeDtypeStruct(w.shape, F32)] * 4,
        compiler_params=_params(("arbitrary", "arbitrary")),
    )(*lands, w, m, v)


class _Item:
    def __init__(self, src, chunked):
        self.src, self.chunked = src, chunked
        self.land_shape = (N_DEV,) + (src.shape[1:] if chunked else src.shape)


def _mesh_place():
    x, y, c = lax.axis_index("x"), lax.axis_index("y"), lax.axis_index("c")
    return x, y, c, 4 * x + 2 * y + c


def _flipped(x, y, c, k):
    px = 1 - x if k & 4 else x
    py = 1 - y if k & 2 else y
    pc = 1 - c if k & 1 else c
    return (px, py, pc), 4 * px + 2 * py + pc


PEER_ORDER = (2, 4, 6, 3, 5, 7, 1)


def _exchange(items, name):
    n = len(items)

    def body(*refs):
        srcs, lands = refs[:n], refs[n:2 * n]
        send, recv, local = refs[2 * n:]
        x, y, c, me = _mesh_place()

        def copy(i, k, chunk, slot, dev):
            src = srcs[i].at[chunk] if items[i].chunked else srcs[i]
            return pltpu.make_async_remote_copy(
                src_ref=src, dst_ref=lands[i].at[slot],
                send_sem=send.at[i, k - 1], recv_sem=recv.at[i, k - 1], device_id=dev, device_id_type=MESH)

        own = [pltpu.make_async_copy(srcs[i].at[me] if items[i].chunked else srcs[i], lands[i].at[me], local.at[i])
               for i in range(n)]
        for k in PEER_ORDER:
            dev, idx = _flipped(x, y, c, k)
            for i in range(n):
                copy(i, k, idx, me, dev).start()
        for cp in own:
            cp.start()
        for k in PEER_ORDER:
            dev, idx = _flipped(x, y, c, k)
            for i in range(n):
                copy(i, k, me, idx, dev).wait_recv()
        for k in PEER_ORDER:
            dev, idx = _flipped(x, y, c, k)
            for i in range(n):
                copy(i, k, idx, me, dev).wait_send()
        for cp in own:
            cp.wait()

    hbm = pl.BlockSpec(memory_space=pl.ANY)
    return pl.pallas_call(
        body, name=name,
        in_specs=[hbm] * n, out_specs=[hbm] * n,
        out_shape=[jax.ShapeDtypeStruct(it.land_shape, it.src.dtype) for it in items],
        scratch_shapes=[pltpu.SemaphoreType.DMA((n, N_DEV - 1)), pltpu.SemaphoreType.DMA((n, N_DEV - 1)),
                        pltpu.SemaphoreType.DMA((n,))],
        compiler_params=pltpu.CompilerParams(has_side_effects=True),
    )(*[it.src for it in items])


HBM_SPEC = pl.BlockSpec(memory_space=pltpu.HBM)
SEM_SPEC = pl.BlockSpec(memory_space=pltpu.SEMAPHORE)
DATAFLOW = pltpu.SideEffectType.DATAFLOW_SIDE_EFFECTING


def _exchange_start(items, name, after=()):
    n = len(items)
    na = len(after)

    def body(*refs):
        srcs, land_ins = refs[:n], refs[n:2 * n]
        outs = refs[2 * n + na:6 * n + na]
        token, local = refs[6 * n + na:]
        del land_ins
        x, y, c, me = _mesh_place()
        own = [pltpu.make_async_copy(srcs[i].at[me] if items[i].chunked else srcs[i], outs[4 * i + 3].at[me],
                                     local.at[i]) for i in range(n)]
        for cp in own:
            cp.start()
        for cp in own:
            cp.wait()
        for k in PEER_ORDER:
            dev, idx = _flipped(x, y, c, k)
            for i in range(n):
                send, recv, _, land = outs[4 * i:4 * i + 4]
                src = srcs[i].at[idx] if items[i].chunked else srcs[i]
                pltpu.make_async_remote_copy(src_ref=src, dst_ref=land.at[me], send_sem=send, recv_sem=recv,
                                             device_id=dev, device_id_type=MESH).start()
        token[...] = jnp.zeros_like(token)

    out_shape, out_specs, args, lands = [], [], [], []
    for it in items:
        out_shape += [pltpu.SemaphoreType.DMA(()), pltpu.SemaphoreType.DMA(()),
                      pltpu.HBM(it.src.shape, it.src.dtype), pltpu.HBM(it.land_shape, it.src.dtype)]
        out_specs += [SEM_SPEC, SEM_SPEC, HBM_SPEC, HBM_SPEC]
        args.append(pltpu.with_memory_space_constraint(it.src, pltpu.HBM))
        lands.append(pltpu.with_memory_space_constraint(lax.empty(it.land_shape, it.src.dtype), pltpu.HBM))
    outs = pl.pallas_call(
        body, name=name,
        in_specs=[HBM_SPEC] * (2 * n) + [pl.BlockSpec(memory_space=pl.ANY)] * na,
        out_specs=out_specs + [pl.BlockSpec(memory_space=pltpu.VMEM)],
        out_shape=out_shape + [jax.ShapeDtypeStruct((SUBLANES, LANES), F32)],
        scratch_shapes=[pltpu.SemaphoreType.DMA((n,))],
        input_output_aliases={**{i: 4 * i + 2 for i in range(n)}, **{n + i: 4 * i + 3 for i in range(n)}},
        compiler_params=pltpu.CompilerParams(has_side_effects=DATAFLOW),
    )(*args, *lands, *after)
    return [tuple(outs[4 * i:4 * i + 4]) for i in range(n)], outs[-1]


def _exchange_wait(handles, after, name):
    n = len(handles)

    def body(*refs):
        x, y, c, _ = _mesh_place()
        for i in range(n):
            src, land, send, recv = refs[4 * i:4 * i + 4]
            del src
            seven = land.at[pl.ds(0, N_DEV - 1)]
            cp = pltpu.make_async_remote_copy(src_ref=seven, dst_ref=seven, send_sem=send, recv_sem=recv,
                                              device_id=(x, y, 1 - c), device_id_type=MESH)
            cp.wait_send()
            cp.wait_recv()

    args, in_specs, out_shape = [], [], []
    for send, recv, src, land in handles:
        args += [src, land, send, recv]
        in_specs += [HBM_SPEC, HBM_SPEC, SEM_SPEC, SEM_SPEC]
        out_shape += [pltpu.HBM(src.shape, src.dtype), pltpu.HBM(land.shape, land.dtype)]
    outs = pl.pallas_call(
        body, name=name,
        in_specs=in_specs + [pl.BlockSpec(memory_space=pl.ANY)] * len(after), out_specs=[HBM_SPEC] * (2 * n),
        out_shape=out_shape,
        input_output_aliases={**{4 * i: 2 * i for i in range(n)}, **{4 * i + 1: 2 * i + 1 for i in range(n)}},
        compiler_params=pltpu.CompilerParams(has_side_effects=DATAFLOW),
    )(*args, *after)
    return [outs[2 * i + 1] for i in range(n)]


TM = 1024
TM_ACC = 512


def kernel(x, norm1_g, w_in, mix_conv_w, attn_out_g, conv_out_g, w_out, norm2_g, ffn_up, ffn_conv_w, ffn_down, final_norm_g, loss_target, m_norm1_g, m_w_in, m_mix_conv_w, m_attn_out_g, m_conv_out_g, m_w_out, m_norm2_g, m_ffn_up, m_ffn_conv_w, m_ffn_down, m_final_norm_g, v_norm1_g, v_w_in, v_mix_conv_w, v_attn_out_g, v_conv_out_g, v_w_out, v_norm2_g, v_ffn_up, v_ffn_conv_w, v_ffn_down, v_final_norm_g):
    nbatch, seq, d = x.shape
    t = nbatch * seq
    nt, nta = t // TM, t // TM_ACC
    out_rows = D_MODEL // N_DEV
    down_rows = D_FF // N_DEV
    xf = x.reshape(t, d)
    target = loss_target.reshape(t, d)

    cw_local = jnp.concatenate([ffn_conv_w, mix_conv_w], axis=-1)
    first, _ = _exchange_start([_Item(cw_local, False), _Item(w_in[0].astype(BF16), False)], "gather_start_first")
    win, wout, wup, wdown = [None] * DEPTH, [None] * DEPTH, [None] * DEPTH, [None] * DEPTH

    full = lambda i, j, k: (0, 0)

    saved = []
    xin = xf
    for l in range(DEPTH):
        h1 = _rms_fwd(xin, norm1_g[l][None], f"rms1_fwd_{l}")
        if l == 0:
            cw_all, win[0] = _exchange_wait(first, [h1], "gather_wait_in_0")
            items = [_Item(w_out[0].astype(BF16), False), _Item(ffn_up[0].astype(BF16), False),
                     _Item(ffn_down[0].astype(BF16), False), _Item(w_in[1].astype(BF16), False),
                     _Item(w_out[1].astype(BF16), False), _Item(ffn_up[1].astype(BF16), False),
                     _Item(ffn_down[1].astype(BF16), False)]
            handles, _ = _exchange_start(items, "gather_start_rest", after=[win[0]])
            fcw = [cw_all[:, k, :, :UP_CHUNK].reshape(2, N_UP_PAIRS, 3, UP_CHUNK) for k in range(DEPTH)]
            mcw = [cw_all[:, k, :, UP_CHUNK:].transpose(1, 0, 2).reshape(3, D_CONV) for k in range(DEPTH)]
        proj = _matmul(
            h1, win[l], grid=(nt, N_DEV, 1), dims=NN, name=f"proj_{l}",
            a_spec=pl.BlockSpec((TM, D_MODEL), lambda i, j, k: (i, 0)),
            b_spec=pl.BlockSpec((None, D_MODEL, IN_CHUNK), lambda i, j, k: (j, 0, 0)),
            o_spec=pl.BlockSpec((TM, IN_CHUNK), lambda i, j, k: (i, j)), o_shape=(t, D_IN), o_dtype=F32)
        o, lse, cat = _attn_fwd(proj, attn_out_g[l][None], nbatch, seq)
        cat = _convmix_fwd(proj, cat, mcw[l], conv_out_g[l][None], nbatch, seq)
        if l == 0:
            (got,) = _exchange_wait([handles[0]], [cat], "gather_wait_out_0")
            wout[0] = got.reshape(D_MODEL, D_MODEL)
        xmid = _matmul(
            cat, wout[l], grid=(nta, 1, 1), dims=NN, name=f"mix_out_{l}",
            a_spec=pl.BlockSpec((TM_ACC, D_MODEL), lambda i, j, k: (i, 0)),
            b_spec=pl.BlockSpec((D_MODEL, D_MODEL), full),
            o_spec=pl.BlockSpec((TM_ACC, D_MODEL), lambda i, j, k: (i, 0)), o_shape=(t, D_MODEL), o_dtype=F32,
            res=xin, res_spec=pl.BlockSpec((TM_ACC, D_MODEL), lambda i, j, k: (i, 0)))
        h2 = _rms_fwd(xmid, norm2_g[l][None], f"rms2_fwd_{l}")
        if l == 0:
            wup[0], got = _exchange_wait([handles[1], handles[2]], [h2], "gather_wait_ffn_0")
            wdown[0] = got.reshape(N_UP_PAIRS, UP_CHUNK, D_MODEL)
        pre = _matmul(
            h2, wup[l], grid=(nt, N_DEV, 1), dims=NN, name=f"ffn_up_{l}",
            a_spec=pl.BlockSpec((TM, D_MODEL), lambda i, j, k: (i, 0)),
            b_spec=pl.BlockSpec((None, D_MODEL, UP_CHUNK), lambda i, j, k: (j, 0, 0)),
            o_spec=pl.BlockSpec((None, TM, UP_CHUNK), lambda i, j, k: (j, i, 0)),
            o_shape=(N_DEV, t, UP_CHUNK), o_dtype=BF16).reshape(2, N_UP_PAIRS, t, UP_CHUNK)
        act = _ffn_act_fwd(pre, fcw[l], nbatch, seq)
        if l == 0:
            win[1], got_out, wup[1], got_down = _exchange_wait(handles[3:7], [act], "gather_wait_1")
            wout[1] = got_out.reshape(D_MODEL, D_MODEL)
            wdown[1] = got_down.reshape(N_UP_PAIRS, UP_CHUNK, D_MODEL)
        xout = _matmul(
            act, wdown[l], grid=(nta, 1, N_UP_PAIRS), dims=NN, name=f"ffn_down_{l}",
            a_spec=pl.BlockSpec((None, TM_ACC, UP_CHUNK), lambda i, j, k: (k, i, 0)),
            b_spec=pl.BlockSpec((None, UP_CHUNK, D_MODEL), lambda i, j, k: (k, 0, 0)),
            o_spec=pl.BlockSpec((TM_ACC, D_MODEL), lambda i, j, k: (i, 0)), o_shape=(t, D_MODEL), o_dtype=F32,
            res=xmid, res_spec=pl.BlockSpec((TM_ACC, D_MODEL), lambda i, j, k: (i, 0)))
        saved.append((xin, h1, proj, o, lse, cat, xmid, h2, pre, act))
        xin = xout

    loss_part, dx, dxb, dgf = _loss_head(xin, final_norm_g[None], target, "loss_head")

    dg1, dg2, dga, dgc = [None] * DEPTH, [None] * DEPTH, [None] * DEPTH, [None] * DEPTH
    for l in reversed(range(DEPTH)):
        xin, h1, proj, o, lse, cat, xmid, h2, pre, act = saved[l]
        d_act = _matmul(
            dxb, wdown[l], grid=(nt, N_UP_PAIRS, 1), dims=NT, name=f"d_act_{l}",
            a_spec=pl.BlockSpec((TM, D_MODEL), lambda i, j, k: (i, 0)),
            b_spec=pl.BlockSpec((None, UP_CHUNK, D_MODEL), lambda i, j, k: (j, 0, 0)),
            o_spec=pl.BlockSpec((None, TM, UP_CHUNK), lambda i, j, k: (j, i, 0)),
            o_shape=(N_UP_PAIRS, t, UP_CHUNK), o_dtype=BF16)
        g_down = _matmul(
            act, dxb, grid=(N_UP_PAIRS, 1, nt), dims=TN, name=f"g_down_{l}",
            a_spec=pl.BlockSpec((None, TM, UP_CHUNK), lambda i, j, k: (i, k, 0)),
            b_spec=pl.BlockSpec((TM, D_MODEL), lambda i, j, k: (k, 0)),
            o_spec=pl.BlockSpec((None, UP_CHUNK, D_MODEL), lambda i, j, k: (i, 0, 0)),
            o_shape=(N_UP_PAIRS, UP_CHUNK, D_MODEL), o_dtype=BF16).reshape(N_DEV, down_rows, D_MODEL)
        d_pre, d_fcw = _ffn_act_bwd(pre, d_act, fcw[l], nbatch, seq)
        d_pre = d_pre.reshape(N_DEV, t, UP_CHUNK)
        dh2 = _matmul(
            d_pre, wup[l], grid=(nta, 1, N_DEV), dims=NT, name=f"d_h2_{l}",
            a_spec=pl.BlockSpec((None, TM_ACC, UP_CHUNK), lambda i, j, k: (k, i, 0)),
            b_spec=pl.BlockSpec((None, D_MODEL, UP_CHUNK), lambda i, j, k: (k, 0, 0)),
            o_spec=pl.BlockSpec((TM_ACC, D_MODEL), lambda i, j, k: (i, 0)), o_shape=(t, D_MODEL), o_dtype=F32)
        g_up = _matmul(
            h2, d_pre, grid=(1, N_DEV, nt), dims=TN, name=f"g_up_{l}",
            a_spec=pl.BlockSpec((TM, D_MODEL), lambda i, j, k: (k, 0)),
            b_spec=pl.BlockSpec((None, TM, UP_CHUNK), lambda i, j, k: (j, k, 0)),
            o_spec=pl.BlockSpec((None, D_MODEL, UP_CHUNK), lambda i, j, k: (j, 0, 0)),
            o_shape=(N_DEV, D_MODEL, UP_CHUNK), o_dtype=BF16)
        dxm, dxmb, dg2[l] = _rms_bwd(xmid, norm2_g[l][None], dh2, dx, f"rms2_bwd_{l}")
        d_cat = _matmul(
            dxmb, wout[l], grid=(nta, 1, 1), dims=NT, name=f"d_cat_{l}",
            a_spec=pl.BlockSpec((TM_ACC, D_MODEL), lambda i, j, k: (i, 0)),
            b_spec=pl.BlockSpec((D_MODEL, D_MODEL), full),
            o_spec=pl.BlockSpec((TM_ACC, D_MODEL), lambda i, j, k: (i, 0)), o_shape=(t, D_MODEL), o_dtype=BF16)
        g_out = _matmul(
            cat, dxmb, grid=(1, 1, nt), dims=TN, name=f"g_out_{l}",
            a_spec=pl.BlockSpec((TM, D_MODEL), lambda i, j, k: (k, 0)),
            b_spec=pl.BlockSpec((TM, D_MODEL), lambda i, j, k: (k, 0)),
            o_spec=pl.BlockSpec((D_MODEL, D_MODEL), full),
            o_shape=(D_MODEL, D_MODEL), o_dtype=BF16).reshape(N_DEV, out_rows, D_MODEL)
        attn_g = attn_out_g[l][None]
        if l == 0:
            early, tok = _exchange_start([_Item(g_out, True), _Item(g_up, True), _Item(g_down, True)], "scatter_start_0a")
            attn_g = attn_g + tok[0, 0]
        d_proj, dga[l] = _attn_bwd(proj, o, lse, d_cat, attn_g, nbatch, seq)
        d_proj, d_mcw, dgc[l] = _convmix_bwd(proj, d_cat, d_proj, mcw[l], conv_out_g[l][None], nbatch, seq)
        g_in = _matmul(
            h1, d_proj, grid=(1, N_DEV, nt), dims=TN, name=f"g_in_{l}",
            a_spec=pl.BlockSpec((TM, D_MODEL), lambda i, j, k: (k, 0)),
            b_spec=pl.BlockSpec((TM, IN_CHUNK), lambda i, j, k: (k, j)),
            o_spec=pl.BlockSpec((None, D_MODEL, IN_CHUNK), lambda i, j, k: (j, 0, 0)),
            o_shape=(N_DEV, D_MODEL, IN_CHUNK), o_dtype=BF16)
        g_cw = jnp.concatenate(
            [d_fcw.reshape(N_DEV, 3, UP_CHUNK), d_mcw.reshape(3, N_DEV, D_CONV // N_DEV).transpose(1, 0, 2)], axis=-1)
        if l == 0:
            late, tok = _exchange_start([_Item(g_in, True), _Item(g_cw, True)], "scatter_start_0b")
        else:
            upper, tok = _exchange_start(
                [_Item(g_in, True), _Item(g_out, True), _Item(g_up, True), _Item(g_down, True), _Item(g_cw, True)],
                "scatter_start_1")
        dh1 = _matmul(
            d_proj, win[l], grid=(nta, 1, N_DEV), dims=NT, name=f"d_h1_{l}",
            a_spec=pl.BlockSpec((TM_ACC, IN_CHUNK), lambda i, j, k: (i, k)),
            b_spec=pl.BlockSpec((None, D_MODEL, IN_CHUNK), lambda i, j, k: (k, 0, 0)),
            o_spec=pl.BlockSpec((TM_ACC, D_MODEL), lambda i, j, k: (i, 0)), o_shape=(t, D_MODEL), o_dtype=F32)
        dx, dxb, dg1[l] = _rms_bwd(xin, norm1_g[l][None] + tok[0, 0], dh1, dxm, f"rms1_bwd_{l}")

    def pack_small(n1, a, c, n2, f):
        return jnp.concatenate(
            [n1, n2, f[None], jnp.concatenate([a, c], axis=-1), jnp.zeros((1, D_MODEL), F32)], axis=0)[None]

    land_in1, land_out1, land_up1, land_down1, land_cw1 = _exchange_wait(upper, [dxb], "scatter_wait_1")
    land_out0, land_up0, land_down0 = _exchange_wait(early, [dxb], "scatter_wait_0a")
    res_out = _adamw([land_out0, land_out1], w_out, m_w_out, v_w_out, out_rows, "adamw_w_out")
    res_up = _adamw([land_up0, land_up1], ffn_up, m_ffn_up, v_ffn_up, 256, "adamw_ffn_up")
    res_down = _adamw([land_down0, land_down1], ffn_down, m_ffn_down, v_ffn_down, down_rows, "adamw_ffn_down")
    small = jnp.concatenate(
        [dg1[0], dg1[1], dg2[0], dg2[1], dgf,
         jnp.concatenate([dga[0], dgc[0]], axis=-1), jnp.concatenate([dga[1], dgc[1]], axis=-1),
         jnp.zeros((1, D_MODEL), F32)], axis=0)
    (land_small,) = _exchange([_Item(small, False)], "gather_gain_grads")
    res_small = _adamw(
        [land_small], pack_small(norm1_g, attn_out_g, conv_out_g, norm2_g, final_norm_g),
        pack_small(m_norm1_g, m_attn_out_g, m_conv_out_g, m_norm2_g, m_final_norm_g),
        pack_small(v_norm1_g, v_attn_out_g, v_conv_out_g, v_norm2_g, v_final_norm_g), SUBLANES, "adamw_gains")
    land_in0, land_cw0 = _exchange_wait(late, [res_small[0], res_down[0], res_up[0], res_out[0]], "scatter_wait_0b")
    res_in = _adamw([land_in0, land_in1], w_in, m_w_in, v_w_in, 256, "adamw_w_in")
    res_cw = _adamw(
        [land_cw0, land_cw1], cw_local, jnp.concatenate([m_ffn_conv_w, m_mix_conv_w], axis=-1),
        jnp.concatenate([v_ffn_conv_w, v_mix_conv_w], axis=-1), 3, "adamw_conv_w")

    loss = lax.psum(loss_part[0, 0], ("x", "y", "c"))

    def unpack(kind):
        s = res_small[kind][0]
        cwr = res_cw[kind]
        return (s[0:2], res_in[kind], cwr[..., UP_CHUNK:], s[5:7, :D_ATTN], s[5:7, D_ATTN:], res_out[kind],
                s[2:4], res_up[kind], cwr[..., :UP_CHUNK], res_down[kind], s[4])

    return (loss, dx.reshape(nbatch, seq, d), *unpack(0), *unpack(1), *unpack(2), *unpack(3))
```

```python
import math

import jax
import jax.numpy as jnp
from jax import lax
from jax.experimental import pallas as pl
from jax.experimental.pallas import tpu as pltpu

F32 = jnp.float32
BF16 = jnp.bfloat16

D_MODEL = 1024
D_ATTN = 512
D_CONV = 512
HEAD_DIM = 64
N_HEADS = 8
D_FF = 2816
DEPTH = 2
D_IN = 3 * D_ATTN + 3 * D_CONV
EPS = 1e-6
DILATIONS = (1, 4, 16)
BAND = 128
N_DEV = 8
IN_CHUNK = D_IN // N_DEV
UP_CHUNK = 2 * D_FF // N_DEV
N_UP_PAIRS = N_DEV // 2
CW_PACK = UP_CHUNK + D_CONV // N_DEV
ADAM_LR = 0.001
ADAM_B1 = 0.9
ADAM_B2 = 0.999
ADAM_EPS = 1e-08
ADAM_WD = 0.01
ADAM_STEP = 10
LANES = 128
SUBLANES = 8
VMEM_LIMIT = 56 * 1024 * 1024

NEG = -1e30
MESH = pl.DeviceIdType.MESH


def _params(sem=None, vmem=VMEM_LIMIT):
    return pltpu.CompilerParams(dimension_semantics=sem, vmem_limit_bytes=vmem)


NN = (((1,), (0,)), ((), ()))
NT = (((1,), (1,)), ((), ()))
TN = (((0,), (0,)), ((), ()))


def _matmul(a, b, *, grid, a_spec, b_spec, o_spec, o_shape, o_dtype, dims, name, res=None, res_spec=None, after=()):
    nk = grid[2]
    o_block = tuple(s for s in o_spec.block_shape if s is not None)
    na = len(after)

    def body(*refs):
        refs = refs[:2 + (res is not None)] + refs[2 + (res is not None) + na:]
        if res is None:
            a_ref, b_ref, o_ref, *scr = refs
            r_ref = None
        else:
            a_ref, b_ref, r_ref, o_ref, *scr = refs
        part = lax.dot_general(a_ref[...].astype(BF16), b_ref[...].astype(BF16), dims, preferred_element_type=F32)

        def finish(total):
            if r_ref is not None:
                total = total + r_ref[...]
            o_ref[...] = total.astype(o_dtype)

        if nk == 1:
            finish(part)
        else:
            acc = scr[0]
            k = pl.program_id(2)

            @pl.when(k == 0)
            def _():
                acc[...] = part

            @pl.when(k > 0)
            def _():
                acc[...] += part

            @pl.when(k == nk - 1)
            def _():
                finish(acc[...])

    in_specs = [a_spec, b_spec] + ([res_spec] if res is not None else []) + [pl.BlockSpec(memory_space=pl.ANY)] * na
    args = (a, b) + ((res,) if res is not None else ()) + tuple(after)
    return pl.pallas_call(
        body, name=name, grid=grid, in_specs=in_specs, out_specs=o_spec,
        out_shape=jax.ShapeDtypeStruct(o_shape, o_dtype),
        scratch_shapes=[pltpu.VMEM(o_block, F32)] if nk > 1 else [],
        compiler_params=_params(("parallel", "parallel", "arbitrary")),
    )(*args)


ROW_TILE = 512


def _rms_fwd(x, g, name):
    t, d = x.shape

    def body(x_ref, g_ref, h_ref):
        xv = x_ref[...]
        r = lax.rsqrt(jnp.mean(xv * xv, axis=-1, keepdims=True) + EPS)
        h_ref[...] = (xv * r * g_ref[...]).astype(BF16)

    return pl.pallas_call(
        body, name=name, grid=(t // ROW_TILE,),
        in_specs=[pl.BlockSpec((ROW_TILE, d), lambda i: (i, 0)), pl.BlockSpec((1, d), lambda i: (0, 0))],
        out_specs=pl.BlockSpec((ROW_TILE, d), lambda i: (i, 0)),
        out_shape=jax.ShapeDtypeStruct((t, d), BF16),
        compiler_params=_params(("parallel",)),
    )(x, g)


def _rms_bwd(x, g, dh, dres, name):
    t, d = x.shape

    def body(x_ref, g_ref, dh_ref, dres_ref, dx_ref, dxb_ref, dg_ref):
        xv = x_ref[...]
        r = lax.rsqrt(jnp.mean(xv * xv, axis=-1, keepdims=True) + EPS)
        xh = xv * r
        dhv = dh_ref[...]
        gd = dhv * g_ref[...]
        dx = r * (gd - xh * jnp.mean(gd * xh, axis=-1, keepdims=True)) + dres_ref[...]
        dx_ref[...] = dx
        dxb_ref[...] = dx.astype(BF16)
        part = jnp.sum(dhv * xh, axis=0, keepdims=True)

        @pl.when(pl.program_id(0) == 0)
        def _():
            dg_ref[...] = part

        @pl.when(pl.program_id(0) > 0)
        def _():
            dg_ref[...] += part

    row = pl.BlockSpec((ROW_TILE, d), lambda i: (i, 0))
    vec = pl.BlockSpec((1, d), lambda i: (0, 0))
    return pl.pallas_call(
        body, name=name, grid=(t // ROW_TILE,),
        in_specs=[row, vec, row, row], out_specs=[row, row, vec],
        out_shape=[jax.ShapeDtypeStruct((t, d), F32), jax.ShapeDtypeStruct((t, d), BF16),
                   jax.ShapeDtypeStruct((1, d), F32)],
        compiler_params=_params(("arbitrary",)),
    )(x, g, dh, dres)


def _loss_head(x, g, target, name):
    t, d = x.shape

    def body(x_ref, g_ref, t_ref, loss_ref, dx_ref, dxb_ref, dg_ref):
        xv = x_ref[...]
        r = lax.rsqrt(jnp.mean(xv * xv, axis=-1, keepdims=True) + EPS)
        xh = xv * r
        gv = g_ref[...]
        err = xh * gv - t_ref[...]
        loss = jnp.full((1, LANES), 0.5 / d, F32) * jnp.sum(err * err)
        dy = err * (1.0 / d)
        gd = dy * gv
        dx = r * (gd - xh * jnp.mean(gd * xh, axis=-1, keepdims=True))
        dx_ref[...] = dx
        dxb_ref[...] = dx.astype(BF16)
        part = jnp.sum(dy * xh, axis=0, keepdims=True)

        @pl.when(pl.program_id(0) == 0)
        def _():
            dg_ref[...] = part
            loss_ref[...] = loss

        @pl.when(pl.program_id(0) > 0)
        def _():
            dg_ref[...] += part
            loss_ref[...] += loss

    row = pl.BlockSpec((ROW_TILE, d), lambda i: (i, 0))
    vec = pl.BlockSpec((1, d), lambda i: (0, 0))
    return pl.pallas_call(
        body, name=name, grid=(t // ROW_TILE,),
        in_specs=[row, vec, row],
        out_specs=[pl.BlockSpec((1, LANES), lambda i: (0, 0)), row, row, vec],
        out_shape=[jax.ShapeDtypeStruct((1, LANES), F32), jax.ShapeDtypeStruct((t, d), F32),
                   jax.ShapeDtypeStruct((t, d), BF16), jax.ShapeDtypeStruct((1, d), F32)],
        compiler_params=_params(("arbitrary",)),
    )(x, g, target)


def _group_matrix(n):
    shift = int(math.log2(HEAD_DIM))
    r = lax.broadcasted_iota(jnp.int32, (n, n), 0) >> shift
    c = lax.broadcasted_iota(jnp.int32, (n, n), 1) >> shift
    return (r == c).astype(BF16)


def _group_sum(v, gmat):
    hi = v.astype(BF16)
    rest = v - hi.astype(F32)
    mid = rest.astype(BF16)
    lo = (rest - mid.astype(F32)).astype(BF16)

    def dot(p):
        return jnp.dot(p, gmat, preferred_element_type=F32)

    return dot(hi) + dot(mid) + dot(lo)


def _shift_rows(ext, k):
    return pltpu.roll(ext, k % ext.shape[0], 0)


def _store_columns(stage, out_hbm, sems, row0, nrows, col_blocks):
    rows = pl.ds(pl.multiple_of(row0, SUBLANES * 2), nrows)
    copies = [
        pltpu.make_async_copy(stage.at[i], out_hbm.at[rows, pl.ds(pl.multiple_of(cb * LANES, LANES), LANES)], sems.at[i])
        for i, cb in enumerate(col_blocks)
    ]
    for cp in copies:
        cp.start()
    for cp in copies:
        cp.wait()


def _attn_consts(width):
    i = lax.broadcasted_iota(jnp.int32, (BAND, width), 0)
    j = lax.broadcasted_iota(jnp.int32, (BAND, width), 1)
    dist = (width - BAND) + i - j
    inwin = (dist >= 0) & (dist <= BAND)
    return dist.astype(F32), inwin, j


def _head_masks():
    lane = lax.broadcasted_iota(jnp.int32, (1, LANES), 1)
    return [(lane < HEAD_DIM).astype(F32), (lane >= HEAD_DIM).astype(F32)]


def _permute_in(src_ref, dst_ref, dil, seq):
    length = seq // dil
    for r in range(dil):
        dst_ref[pl.ds(r * length, length), :] = src_ref[pl.ds(r, length, stride=dil), :].astype(dst_ref.dtype)


def _slopes_table():
    slopes = 2.0 ** (-8.0 * jnp.arange(1, N_HEADS + 1, dtype=F32) / N_HEADS)
    return jnp.broadcast_to(slopes[:, None], (N_HEADS, 2 * BAND))


def _attn_fwd(proj, attn_g, nbatch, seq):
    t = nbatch * seq
    nblk = seq // BAND
    scale = HEAD_DIM ** -0.5

    def body(q_ref, k_ref, v_ref, g_ref, sl_ref, o_ref, lse_ref, cat_ref, pq, pk, pv, po, pm, pll, ao, am, al):
        hp = pl.program_id(1)
        hmask = _head_masks()
        slope = [sl_ref[pl.ds(2 * hp + hh, 1), :] for hh in range(2)]

        def run_branch(dil, qs, ks, vs, osink, msink, lsink):
            nb = seq // dil // BAND
            width = 2 * BAND if nb > 1 else BAND
            distf, inwin, jcol = _attn_consts(width)
            bias = [distf * (slope[hh][:, :width] * (-float(dil))) for hh in range(2)]

            def blk(m, carry):
                row0 = pl.multiple_of(m * BAND, BAND)
                q = qs[pl.ds(row0, BAND), :] * scale
                if nb > 1:
                    prow = pl.multiple_of(jnp.maximum(m - 1, 0) * BAND, BAND)
                    kk = jnp.concatenate([ks[pl.ds(prow, BAND), :], ks[pl.ds(row0, BAND), :]], axis=0)
                    vv = jnp.concatenate([vs[pl.ds(prow, BAND), :], vs[pl.ds(row0, BAND), :]], axis=0)
                    valid = inwin & (jcol >= jnp.where((m % nb) == 0, BAND, 0))
                else:
                    kk = ks[pl.ds(row0, BAND), :]
                    vv = vs[pl.ds(row0, BAND), :]
                    valid = inwin
                kb = kk.astype(BF16)
                o = jnp.zeros((BAND, LANES), F32)
                mfull = jnp.zeros((BAND, LANES), F32)
                lfull = jnp.zeros((BAND, LANES), F32)
                for hh in range(2):
                    qh = (q * hmask[hh]).astype(BF16)
                    s = lax.dot_general(qh, kb, NT, preferred_element_type=F32)
                    s = jnp.where(valid, s + bias[hh], NEG)
                    mh = jnp.max(s, axis=1, keepdims=True)
                    p = jnp.exp(s - mh)
                    lh = jnp.sum(p, axis=1, keepdims=True)
                    o = o + jnp.dot(p.astype(BF16), (vv * hmask[hh]).astype(BF16), preferred_element_type=F32)
                    mfull = mfull + mh * hmask[hh]
                    lfull = lfull + lh * hmask[hh]
                osink[pl.ds(row0, BAND), :] = o
                msink[pl.ds(row0, BAND), :] = mfull
                lsink[pl.ds(row0, BAND), :] = lfull
                return carry

            lax.fori_loop(0, nblk, blk, 0)

        run_branch(1, q_ref, k_ref, v_ref, ao, am, al)
        for dil in DILATIONS[1:]:
            length = seq // dil
            _permute_in(q_ref, pq, dil, seq)
            _permute_in(k_ref, pk, dil, seq)
            _permute_in(v_ref, pv, dil, seq)
            run_branch(dil, pq, pk, pv, po, pm, pll)
            for r in range(dil):
                nat = pl.ds(r, length, stride=dil)
                per = pl.ds(r * length, length)
                m0 = am[nat, :]
                mb = pm[per, :]
                mn = jnp.maximum(m0, mb)
                e0 = jnp.exp(m0 - mn)
                eb = jnp.exp(mb - mn)
                ao[nat, :] = ao[nat, :] * e0 + po[per, :] * eb
                al[nat, :] = al[nat, :] * e0 + pll[per, :] * eb
                am[nat, :] = mn

        gmat = _group_matrix(LANES)
        gv = g_ref[...]

        def fin(c, carry):
            rows = pl.ds(pl.multiple_of(c * 256, 256), 256)
            lv = al[rows, :]
            o = ao[rows, :] / lv
            o_ref[rows, :] = o
            lse_ref[rows, :] = am[rows, :] + jnp.log(lv)
            ms = _group_sum(o * o, gmat) * (1.0 / HEAD_DIM)
            cat_ref[rows, :] = (o * lax.rsqrt(ms + EPS) * gv).astype(BF16)
            return carry

        lax.fori_loop(0, seq // 256, fin, 0)

    nq = D_ATTN // LANES
    blk = lambda off: pl.BlockSpec((seq, LANES), lambda b, h: (b, h + off))
    scratch = [pltpu.VMEM((seq, LANES), F32) for _ in range(9)]
    return pl.pallas_call(
        body, name="attn_fwd", grid=(nbatch, nq),
        in_specs=[blk(0), blk(nq), blk(2 * nq), pl.BlockSpec((1, LANES), lambda b, h: (0, h)),
                  pl.BlockSpec((N_HEADS, 2 * BAND), lambda b, h: (0, 0))],
        out_specs=[blk(0), blk(0), blk(0)],
        out_shape=[jax.ShapeDtypeStruct((t, D_ATTN), F32), jax.ShapeDtypeStruct((t, D_ATTN), F32),
                   jax.ShapeDtypeStruct((t, D_MODEL), BF16)],
        scratch_shapes=scratch,
        compiler_params=_params(("parallel", "parallel")),
    )(proj, proj, proj, attn_g, _slopes_table())


def _attn_bwd(proj, o, lse, d_cat, attn_g, nbatch, seq):
    t = nbatch * seq
    nblk = seq // BAND
    scale = HEAD_DIM ** -0.5

    def body(q_ref, k_ref, v_ref, o_ref, lse_ref, dy_ref, g_ref, sl_ref, dproj_ref, dg_ref,
             do_n, dl_n, dq_n, dk_n, dv_n, pq, pk, pv, pdo, plse, pdl, pdq, pdk, pdv, stage, sems):
        hp = pl.program_id(0)
        hmask = _head_masks()
        slope = [sl_ref[pl.ds(2 * hp + hh, 1), :] for hh in range(2)]
        gmat = _group_matrix(LANES)
        gv = g_ref[...]

        def prep(c, dg):
            rows = pl.ds(pl.multiple_of(c * 256, 256), 256)
            ov = o_ref[rows, :]
            dyn = dy_ref[rows, :].astype(F32)
            r = lax.rsqrt(_group_sum(ov * ov, gmat) * (1.0 / HEAD_DIM) + EPS)
            gd = dyn * gv
            oh = ov * r
            do = r * (gd - oh * (_group_sum(gd * oh, gmat) * (1.0 / HEAD_DIM)))
            do_n[rows, :] = do
            dl_n[rows, :] = _group_sum(do * ov, gmat)
            return dg + jnp.sum(dyn * oh, axis=0, keepdims=True)

        dg = lax.fori_loop(0, seq // 256, prep, jnp.zeros((1, LANES), F32))

        @pl.when(pl.program_id(1) == 0)
        def _():
            dg_ref[...] = dg

        @pl.when(pl.program_id(1) > 0)
        def _():
            dg_ref[...] += dg

        def clear(*refs):
            def step(c, carry):
                rows = pl.ds(pl.multiple_of(c * 256, 256), 256)
                for ref in refs:
                    ref[rows, :] = jnp.zeros((256, LANES), F32)
                return carry

            lax.fori_loop(0, seq // 256, step, 0)

        clear(dq_n, dk_n, dv_n)

        def run_branch(dil, qs, ks, vs, dos, lses, dls, dqs, dks, dvs):
            nb = seq // dil // BAND
            width = 2 * BAND if nb > 1 else BAND
            distf, inwin, jcol = _attn_consts(width)
            bias = [distf * (slope[hh][:, :width] * (-float(dil))) for hh in range(2)]

            def blk(m, carry):
                row0 = pl.multiple_of(m * BAND, BAND)
                cur = pl.ds(row0, BAND)
                q = qs[cur, :] * scale
                dov = dos[cur, :]
                lsev = lses[cur, :]
                dlv = dls[cur, :]
                if nb > 1:
                    prow = pl.multiple_of(jnp.maximum(m - 1, 0) * BAND, BAND)
                    prev = pl.ds(prow, BAND)
                    kk = jnp.concatenate([ks[prev, :], ks[cur, :]], axis=0)
                    vv = jnp.concatenate([vs[prev, :], vs[cur, :]], axis=0)
                    valid = inwin & (jcol >= jnp.where((m % nb) == 0, BAND, 0))
                else:
                    kk = ks[cur, :]
                    vv = vs[cur, :]
                    valid = inwin
                kb = kk.astype(BF16)
                vb = vv.astype(BF16)
                dq = jnp.zeros((BAND, LANES), F32)
                dkk = jnp.zeros((width, LANES), F32)
                dvv = jnp.zeros((width, LANES), F32)
                for hh in range(2):
                    c0 = hh * HEAD_DIM
                    qh = (q * hmask[hh]).astype(BF16)
                    doh = (dov * hmask[hh]).astype(BF16)
                    s = lax.dot_general(qh, kb, NT, preferred_element_type=F32) + bias[hh]
                    p = jnp.where(valid, jnp.exp(s - lsev[:, c0:c0 + 1]), 0.0)
                    dp = lax.dot_general(doh, vb, NT, preferred_element_type=F32)
                    ds = (p * (dp - dlv[:, c0:c0 + 1])).astype(BF16)
                    dq = dq + jnp.dot(ds, kb, preferred_element_type=F32) * hmask[hh]
                    dkk = dkk + lax.dot_general(ds, qh, TN, preferred_element_type=F32)
                    dvv = dvv + lax.dot_general(p.astype(BF16), doh, TN, preferred_element_type=F32)
                dqs[cur, :] += dq
                if nb > 1:
                    dks[prev, :] += dkk[:BAND]
                    dvs[prev, :] += dvv[:BAND]
                    dks[cur, :] += dkk[BAND:]
                    dvs[cur, :] += dvv[BAND:]
                else:
                    dks[cur, :] += dkk
                    dvs[cur, :] += dvv
                return carry

            lax.fori_loop(0, nblk, blk, 0)

        run_branch(1, q_ref, k_ref, v_ref, do_n, lse_ref, dl_n, dq_n, dk_n, dv_n)
        for dil in DILATIONS[1:]:
            length = seq // dil
            for src, dst in ((q_ref, pq), (k_ref, pk), (v_ref, pv), (do_n, pdo), (lse_ref, plse), (dl_n, pdl)):
                _permute_in(src, dst, dil, seq)
            clear(pdq, pdk, pdv)
            run_branch(dil, pq, pk, pv, pdo, plse, pdl, pdq, pdk, pdv)
            for r in range(dil):
                nat = pl.ds(r, length, stride=dil)
                per = pl.ds(r * length, length)
                dq_n[nat, :] += pdq[per, :]
                dk_n[nat, :] += pdk[per, :]
                dv_n[nat, :] += pdv[per, :]

        def emit(c, carry):
            rows = pl.ds(pl.multiple_of(c * 256, 256), 256)
            stage[0, rows, :] = (dq_n[rows, :] * scale).astype(BF16)
            stage[1, rows, :] = dk_n[rows, :].astype(BF16)
            stage[2, rows, :] = dv_n[rows, :].astype(BF16)
            return carry

        lax.fori_loop(0, seq // 256, emit, 0)
        _store_columns(stage, dproj_ref, sems, pl.program_id(1) * seq, seq, [hp, nq + hp, 2 * nq + hp])

    nq = D_ATTN // LANES
    blk = lambda off: pl.BlockSpec((seq, LANES), lambda h, b: (b, h + off))
    vec = pl.BlockSpec((1, LANES), lambda h, b: (0, h))
    scratch = [pltpu.VMEM((seq, LANES), F32) for _ in range(14)]
    scratch += [pltpu.VMEM((3, seq, LANES), BF16), pltpu.SemaphoreType.DMA((3,))]
    d_proj, dg = pl.pallas_call(
        body, name="attn_bwd", grid=(nq, nbatch),
        in_specs=[blk(0), blk(nq), blk(2 * nq), blk(0), blk(0), blk(0), vec,
                  pl.BlockSpec((N_HEADS, 2 * BAND), lambda h, b: (0, 0))],
        out_specs=[pl.BlockSpec(memory_space=pl.ANY), vec],
        out_shape=[jax.ShapeDtypeStruct((t, D_IN), BF16), jax.ShapeDtypeStruct((1, D_ATTN), F32)],
        scratch_shapes=scratch,
        compiler_params=_params(("arbitrary", "arbitrary")),
    )(proj, proj, proj, o, lse, d_cat, attn_g, _slopes_table())
    return d_proj, dg


HALO = 2 * SUBLANES


def _window(ref, c, rows, nchunks, after):
    row0 = pl.multiple_of(c * rows, rows)
    prev0 = pl.multiple_of(jnp.maximum(row0 - HALO, 0), HALO)
    parts = [ref[pl.ds(prev0, HALO), :].astype(F32) * (c > 0).astype(F32), ref[pl.ds(row0, rows), :].astype(F32)]
    if after:
        next0 = pl.multiple_of(jnp.minimum(row0 + rows, (nchunks - 1) * rows), HALO)
        parts.append(ref[pl.ds(next0, HALO), :].astype(F32) * (c < nchunks - 1).astype(F32))
    return jnp.concatenate(parts, axis=0)


def _conv(z, w):
    return w[0:1] * _shift_rows(z, 2) + w[1:2] * _shift_rows(z, 1) + w[2:3] * z


def _conv_t(dy, w):
    return w[2:3] * dy + w[1:2] * _shift_rows(dy, -1) + w[0:1] * _shift_rows(dy, -2)


def _conv_wgrad(dy, z, cur):
    return [jnp.sum((dy * _shift_rows(z, 2 - k))[cur], axis=0, keepdims=True) for k in range(3)]


MIX_ROWS = 256
GATE_B_BLOCK = 3 * D_ATTN // LANES
GATE_C_BLOCK = GATE_B_BLOCK + D_CONV // LANES
U_BLOCK = GATE_C_BLOCK + D_CONV // LANES


def _convmix_fwd(proj, cat, mcw, conv_g, nbatch, seq):
    nchunks = seq // MIX_ROWS

    def body(gb_ref, gc_ref, u_ref, w_ref, g_ref, cat_in, cat_ref):
        del cat_in
        gmat = _group_matrix(LANES)
        w = w_ref[...]
        gv = g_ref[...]

        def step(c, carry):
            cur = pl.ds(pl.multiple_of(c * MIX_ROWS, MIX_ROWS), MIX_ROWS)
            z = _window(gc_ref, c, MIX_ROWS, nchunks, False) * _window(u_ref, c, MIX_ROWS, nchunks, False)
            y = gb_ref[cur, :] * _conv(z, w)[HALO:]
            ms = _group_sum(y * y, gmat) * (1.0 / HEAD_DIM)
            cat_ref[cur, :] = (y * lax.rsqrt(ms + EPS) * gv).astype(BF16)
            return carry

        lax.fori_loop(0, nchunks, step, 0)

    nc = D_CONV // LANES
    blk = lambda off: pl.BlockSpec((seq, LANES), lambda b, j: (b, j + off))
    return pl.pallas_call(
        body, name="convmix_fwd", grid=(nbatch, nc),
        in_specs=[blk(GATE_B_BLOCK), blk(GATE_C_BLOCK), blk(U_BLOCK),
                  pl.BlockSpec((3, LANES), lambda b, j: (0, j)), pl.BlockSpec((1, LANES), lambda b, j: (0, j)),
                  pl.BlockSpec(memory_space=pl.ANY)],
        out_specs=blk(D_ATTN // LANES),
        out_shape=jax.ShapeDtypeStruct(cat.shape, cat.dtype),
        input_output_aliases={5: 0},
        compiler_params=_params(("parallel", "parallel")),
    )(proj, proj, proj, mcw, conv_g, cat)


def _convmix_bwd(proj, d_cat, d_proj, mcw, conv_g, nbatch, seq):
    nchunks = seq // MIX_ROWS

    def body(gb_ref, gc_ref, u_ref, dy_ref, w_ref, g_ref, dproj_in, dproj_ref, dw_ref, dg_ref, stage, sems):
        del dproj_in
        cb = pl.program_id(0)
        b = pl.program_id(1)
        gmat = _group_matrix(LANES)
        w = w_ref[...]
        gv = g_ref[...]
        cur = slice(HALO, HALO + MIX_ROWS)

        def step(c, carry):
            rows = pl.ds(pl.multiple_of(c * MIX_ROWS, MIX_ROWS), MIX_ROWS)
            gb = _window(gb_ref, c, MIX_ROWS, nchunks, True)
            gc = _window(gc_ref, c, MIX_ROWS, nchunks, True)
            u = _window(u_ref, c, MIX_ROWS, nchunks, True)
            dyn = _window(dy_ref, c, MIX_ROWS, nchunks, True)
            z = gc * u
            conv = _conv(z, w)
            y = gb * conv
            r = lax.rsqrt(_group_sum(y * y, gmat) * (1.0 / HEAD_DIM) + EPS)
            yh = y * r
            gd = dyn * gv
            dy = r * (gd - yh * (_group_sum(gd * yh, gmat) * (1.0 / HEAD_DIM)))
            dc = dy * gb
            dz = _conv_t(dc, w)
            stage[0, rows, :] = (dy * conv)[cur].astype(BF16)
            stage[1, rows, :] = (dz * u)[cur].astype(BF16)
            stage[2, rows, :] = (dz * gc)[cur].astype(BF16)
            dws = _conv_wgrad(dc, z, cur)
            dg = jnp.sum((dyn * yh)[cur], axis=0, keepdims=True)
            return tuple(a + d for a, d in zip(carry, dws + [dg]))

        zero = jnp.zeros((1, LANES), F32)
        dw0, dw1, dw2, dg = lax.fori_loop(0, nchunks, step, (zero, zero, zero, zero))

        @pl.when(b == 0)
        def _():
            dw_ref[0:1, :] = dw0
            dw_ref[1:2, :] = dw1
            dw_ref[2:3, :] = dw2
            dg_ref[...] = dg

        @pl.when(b > 0)
        def _():
            dw_ref[0:1, :] += dw0
            dw_ref[1:2, :] += dw1
            dw_ref[2:3, :] += dw2
            dg_ref[...] += dg

        _store_columns(stage, dproj_ref, sems, b * seq, seq, [GATE_B_BLOCK + cb, GATE_C_BLOCK + cb, U_BLOCK + cb])

    nc = D_CONV // LANES
    blk = lambda off: pl.BlockSpec((seq, LANES), lambda j, b: (b, j + off))
    return pl.pallas_call(
        body, name="convmix_bwd", grid=(nc, nbatch),
        in_specs=[blk(GATE_B_BLOCK), blk(GATE_C_BLOCK), blk(U_BLOCK), blk(D_ATTN // LANES),
                  pl.BlockSpec((3, LANES), lambda j, b: (0, j)), pl.BlockSpec((1, LANES), lambda j, b: (0, j)),
                  pl.BlockSpec(memory_space=pl.ANY)],
        out_specs=[pl.BlockSpec(memory_space=pl.ANY), pl.BlockSpec((3, LANES), lambda j, b: (0, j)),
                   pl.BlockSpec((1, LANES), lambda j, b: (0, j))],
        out_shape=[jax.ShapeDtypeStruct(d_proj.shape, d_proj.dtype), jax.ShapeDtypeStruct((3, D_CONV), F32),
                   jax.ShapeDtypeStruct((1, D_CONV), F32)],
        scratch_shapes=[pltpu.VMEM((3, seq, LANES), BF16), pltpu.SemaphoreType.DMA((3,))],
        input_output_aliases={6: 0},
        compiler_params=_params(("arbitrary", "arbitrary")),
    )(proj, proj, proj, d_cat, mcw, conv_g, d_proj)


FFN_ROWS = 128


def _ffn_act_fwd(pre, fcw, nbatch, seq):
    t = nbatch * seq
    nchunks = seq // FFN_ROWS

    def body(pre_ref, w_ref, act_ref):
        wa = w_ref[0]
        wc = w_ref[1]

        def step(c, carry):
            cur = pl.ds(pl.multiple_of(c * FFN_ROWS, FFN_ROWS), FFN_ROWS)
            a = _conv(_window(pre_ref.at[0], c, FFN_ROWS, nchunks, False), wa)[HALO:]
            v = _conv(_window(pre_ref.at[1], c, FFN_ROWS, nchunks, False), wc)[HALO:]
            act_ref[cur, :] = (a * jax.nn.sigmoid(a) * v).astype(BF16)
            return carry

        lax.fori_loop(0, nchunks, step, 0)

    return pl.pallas_call(
        body, name="ffn_act_fwd", grid=(N_UP_PAIRS, nbatch),
        in_specs=[pl.BlockSpec((2, None, seq, UP_CHUNK), lambda i, b: (0, i, b, 0)),
                  pl.BlockSpec((2, None, 3, UP_CHUNK), lambda i, b: (0, i, 0, 0))],
        out_specs=pl.BlockSpec((None, seq, UP_CHUNK), lambda i, b: (i, b, 0)),
        out_shape=jax.ShapeDtypeStruct((N_UP_PAIRS, t, UP_CHUNK), BF16),
        compiler_params=_params(("parallel", "parallel")),
    )(pre, fcw)


def _ffn_act_bwd(pre, d_act, fcw, nbatch, seq):
    nchunks = seq // FFN_ROWS

    def body(pre_ref, da_ref, w_ref, dpre_ref, dw_ref):
        b = pl.program_id(1)
        wa = w_ref[0]
        wc = w_ref[1]
        cur = slice(HALO, HALO + FFN_ROWS)

        def step(c, carry):
            rows = pl.ds(pl.multiple_of(c * FFN_ROWS, FFN_ROWS), FFN_ROWS)
            pg = _window(pre_ref.at[0], c, FFN_ROWS, nchunks, True)
            pv = _window(pre_ref.at[1], c, FFN_ROWS, nchunks, True)
            dact = _window(da_ref, c, FFN_ROWS, nchunks, True)
            a = _conv(pg, wa)
            v = _conv(pv, wc)
            sg = jax.nn.sigmoid(a)
            da = dact * v * (sg * (1.0 + a * (1.0 - sg)))
            dv = dact * (a * sg)
            dpre_ref[0, rows, :] = _conv_t(da, wa)[cur].astype(BF16)
            dpre_ref[1, rows, :] = _conv_t(dv, wc)[cur].astype(BF16)
            return tuple(acc + d for acc, d in zip(carry, _conv_wgrad(da, pg, cur) + _conv_wgrad(dv, pv, cur)))

        zero = jnp.zeros((1, UP_CHUNK), F32)
        sums = lax.fori_loop(0, nchunks, step, (zero,) * 6)

        @pl.when(b == 0)
        def _():
            for i in range(6):
                dw_ref[i // 3, pl.ds(i % 3, 1), :] = sums[i]

        @pl.when(b > 0)
        def _():
            for i in range(6):
                dw_ref[i // 3, pl.ds(i % 3, 1), :] += sums[i]

    pair = pl.BlockSpec((2, None, seq, UP_CHUNK), lambda i, b: (0, i, b, 0))
    wspec = pl.BlockSpec((2, None, 3, UP_CHUNK), lambda i, b: (0, i, 0, 0))
    return pl.pallas_call(
        body, name="ffn_act_bwd", grid=(N_UP_PAIRS, nbatch),
        in_specs=[pair, pl.BlockSpec((None, seq, UP_CHUNK), lambda i, b: (i, b, 0)), wspec],
        out_specs=[pair, wspec],
        out_shape=[jax.ShapeDtypeStruct(pre.shape, BF16), jax.ShapeDtypeStruct(fcw.shape, F32)],
        compiler_params=_params(("parallel", "arbitrary")),
    )(pre, d_act, fcw)


def _adamw(lands, w, m, v, row_tile, name):
    nl = len(lands)
    _, nr, ncol = lands[0].shape
    c1 = 1.0 - ADAM_B1 ** ADAM_STEP
    c2 = 1.0 - ADAM_B2 ** ADAM_STEP

    def body(*refs):
        land_refs = refs[:nl]
        w_ref, m_ref, v_ref, g_ref, d_ref, mo_ref, vo_ref = refs[nl:]
        for l in range(nl):
            @pl.when(pl.program_id(0) == l)
            def _(l=l):
                g = land_refs[l][0].astype(F32)
                for j in range(1, N_DEV):
                    g = g + land_refs[l][j].astype(F32)
                g_ref[...] = g

        g = g_ref[...]
        m2 = ADAM_B1 * m_ref[...] + (1.0 - ADAM_B1) * g
        v2 = ADAM_B2 * v_ref[...] + (1.0 - ADAM_B2) * (g * g)
        mo_ref[...] = m2
        vo_ref[...] = v2
        d_ref[...] = -ADAM_LR * ((m2 / c1) / (jnp.sqrt(v2 / c2) + ADAM_EPS) + ADAM_WD * w_ref[...])

    def land_spec(l):
        return pl.BlockSpec((N_DEV, row_tile, ncol), lambda k, i: (0, jnp.where(k == l, i, 0), 0))

    tile = pl.BlockSpec((None, row_tile, ncol), lambda k, i: (k, i, 0))
    return pl.pallas_call(
        body, name=name, grid=(nl, nr // row_tile),
        in_specs=[land_spec(l) for l in range(nl)] + [tile, tile, tile],
        out_specs=[tile] * 4,
        out_shape=[jax.ShapeDtypeStruct(w.shape, F32)] * 4,
        compiler_params=_params(("arbitrary", "arbitrary")),
    )(*lands, w, m, v)


class _Item:
    def __init__(self, src, chunked):
        self.src, self.chunked = src, chunked
        self.land_shape = (N_DEV,) + (src.shape[1:] if chunked else src.shape)


def _mesh_place():
    x, y, c = lax.axis_index("x"), lax.axis_index("y"), lax.axis_index("c")
    return x, y, c, 4 * x + 2 * y + c


def _flipped(x, y, c, k):
    px = 1 - x if k & 4 else x
    py = 1 - y if k & 2 else y
    pc = 1 - c if k & 1 else c
    return (px, py, pc), 4 * px + 2 * py + pc


PEER_ORDER = (2, 4, 6, 3, 5, 7, 1)


def _exchange(items, name):
    n = len(items)

    def body(*refs):
        srcs, lands = refs[:n], refs[n:2 * n]
        send, recv, local = refs[2 * n:]
        x, y, c, me = _mesh_place()

        def copy(i, k, chunk, slot, dev):
            src = srcs[i].at[chunk] if items[i].chunked else srcs[i]
            return pltpu.make_async_remote_copy(
                src_ref=src, dst_ref=lands[i].at[slot],
                send_sem=send.at[i, k - 1], recv_sem=recv.at[i, k - 1], device_id=dev, device_id_type=MESH)

        own = [pltpu.make_async_copy(srcs[i].at[me] if items[i].chunked else srcs[i], lands[i].at[me], local.at[i])
               for i in range(n)]
        for k in PEER_ORDER:
            dev, idx = _flipped(x, y, c, k)
            for i in range(n):
                copy(i, k, idx, me, dev).start()
        for cp in own:
            cp.start()
        for k in PEER_ORDER:
            dev, idx = _flipped(x, y, c, k)
            for i in range(n):
                copy(i, k, me, idx, dev).wait_recv()
        for k in PEER_ORDER:
            dev, idx = _flipped(x, y, c, k)
            for i in range(n):
                copy(i, k, idx, me, dev).wait_send()
        for cp in own:
            cp.wait()

    hbm = pl.BlockSpec(memory_space=pl.ANY)
    return pl.pallas_call(
        body, name=name,
        in_specs=[hbm] * n, out_specs=[hbm] * n,
        out_shape=[jax.ShapeDtypeStruct(it.land_shape, it.src.dtype) for it in items],
        scratch_shapes=[pltpu.SemaphoreType.DMA((n, N_DEV - 1)), pltpu.SemaphoreType.DMA((n, N_DEV - 1)),
                        pltpu.SemaphoreType.DMA((n,))],
        compiler_params=pltpu.CompilerParams(has_side_effects=True),
    )(*[it.src for it in items])


HBM_SPEC = pl.BlockSpec(memory_space=pltpu.HBM)
SEM_SPEC = pl.BlockSpec(memory_space=pltpu.SEMAPHORE)
DATAFLOW = pltpu.SideEffectType.DATAFLOW_SIDE_EFFECTING


def _exchange_start(items, name, after=()):
    n = len(items)
    na = len(after)

    def body(*refs):
        srcs, land_ins = refs[:n], refs[n:2 * n]
        outs = refs[2 * n + na:6 * n + na]
        (local,) = refs[6 * n + na:]
        del land_ins
        x, y, c, me = _mesh_place()
        own = [pltpu.make_async_copy(srcs[i].at[me] if items[i].chunked else srcs[i], outs[4 * i + 3].at[me],
                                     local.at[i]) for i in range(n)]
        for cp in own:
            cp.start()
        for cp in own:
            cp.wait()
        for k in PEER_ORDER:
            dev, idx = _flipped(x, y, c, k)
            for i in range(n):
                send, recv, _, land = outs[4 * i:4 * i + 4]
                src = srcs[i].at[idx] if items[i].chunked else srcs[i]
                pltpu.make_async_remote_copy(src_ref=src, dst_ref=land.at[me], send_sem=send, recv_sem=recv,
                                             device_id=dev, device_id_type=MESH).start()

    out_shape, out_specs, args, lands = [], [], [], []
    for it in items:
        out_shape += [pltpu.SemaphoreType.DMA(()), pltpu.SemaphoreType.DMA(()),
                      pltpu.HBM(it.src.shape, it.src.dtype), pltpu.HBM(it.land_shape, it.src.dtype)]
        out_specs += [SEM_SPEC, SEM_SPEC, HBM_SPEC, HBM_SPEC]
        args.append(pltpu.with_memory_space_constraint(it.src, pltpu.HBM))
        lands.append(pltpu.with_memory_space_constraint(lax.empty(it.land_shape, it.src.dtype), pltpu.HBM))
    outs = pl.pallas_call(
        body, name=name,
        in_specs=[HBM_SPEC] * (2 * n) + [pl.BlockSpec(memory_space=pl.ANY)] * na,
        out_specs=out_specs, out_shape=out_shape,
        scratch_shapes=[pltpu.SemaphoreType.DMA((n,))],
        input_output_aliases={**{i: 4 * i + 2 for i in range(n)}, **{n + i: 4 * i + 3 for i in range(n)}},
        compiler_params=pltpu.CompilerParams(has_side_effects=DATAFLOW),
    )(*args, *lands, *after)
    return [tuple(outs[4 * i:4 * i + 4]) for i in range(n)]


def _started(handles):
    return handles[0][2]


def _exchange_wait(handles, after, name):
    n = len(handles)

    def body(*refs):
        x, y, c, _ = _mesh_place()
        for i in range(n):
            src, land, send, recv = refs[4 * i:4 * i + 4]
            del src
            seven = land.at[pl.ds(0, N_DEV - 1)]
            cp = pltpu.make_async_remote_copy(src_ref=seven, dst_ref=seven, send_sem=send, recv_sem=recv,
                                              device_id=(x, y, 1 - c), device_id_type=MESH)
            cp.wait_send()
            cp.wait_recv()

    args, in_specs, out_shape = [], [], []
    for send, recv, src, land in handles:
        args += [src, land, send, recv]
        in_specs += [HBM_SPEC, HBM_SPEC, SEM_SPEC, SEM_SPEC]
        out_shape += [pltpu.HBM(src.shape, src.dtype), pltpu.HBM(land.shape, land.dtype)]
    outs = pl.pallas_call(
        body, name=name,
        in_specs=in_specs + [pl.BlockSpec(memory_space=pl.ANY)] * len(after), out_specs=[HBM_SPEC] * (2 * n),
        out_shape=out_shape,
        input_output_aliases={**{4 * i: 2 * i for i in range(n)}, **{4 * i + 1: 2 * i + 1 for i in range(n)}},
        compiler_params=pltpu.CompilerParams(has_side_effects=DATAFLOW),
    )(*args, *after)
    return [outs[2 * i + 1] for i in range(n)]


TM = 1024
TM_ACC = 512


def kernel(x, norm1_g, w_in, mix_conv_w, attn_out_g, conv_out_g, w_out, norm2_g, ffn_up, ffn_conv_w, ffn_down, final_norm_g, loss_target, m_norm1_g, m_w_in, m_mix_conv_w, m_attn_out_g, m_conv_out_g, m_w_out, m_norm2_g, m_ffn_up, m_ffn_conv_w, m_ffn_down, m_final_norm_g, v_norm1_g, v_w_in, v_mix_conv_w, v_attn_out_g, v_conv_out_g, v_w_out, v_norm2_g, v_ffn_up, v_ffn_conv_w, v_ffn_down, v_final_norm_g):
    nbatch, seq, d = x.shape
    t = nbatch * seq
    nt, nta = t // TM, t // TM_ACC
    out_rows = D_MODEL // N_DEV
    down_rows = D_FF // N_DEV
    xf = x.reshape(t, d)
    target = loss_target.reshape(t, d)

    cw_local = jnp.concatenate([ffn_conv_w, mix_conv_w], axis=-1)
    cast = lambda w: _Item(w.astype(BF16), False)
    batch = [_exchange_start([_Item(cw_local, False), cast(w_in[0])], "gather_start_a"), None, None, None]
    win, wout, wup, wdown = [None] * DEPTH, [None] * DEPTH, [None] * DEPTH, [None] * DEPTH

    full = lambda i, j, k: (0, 0)

    saved = []
    xin = xf
    for l in range(DEPTH):
        h1 = _rms_fwd(xin, norm1_g[l][None], f"rms1_fwd_{l}")
        pin = []
        if l == 0:
            cw_all, win[0] = _exchange_wait(batch[0], [h1], "gather_wait_in_0")
            batch[1] = _exchange_start([cast(w_out[0]), cast(ffn_up[0])], "gather_start_b", after=[win[0]])
            pin = [_started(batch[1])]
            fcw = [cw_all[:, k, :, :UP_CHUNK].reshape(2, N_UP_PAIRS, 3, UP_CHUNK) for k in range(DEPTH)]
            mcw = [cw_all[:, k, :, UP_CHUNK:].transpose(1, 0, 2).reshape(3, D_CONV) for k in range(DEPTH)]
        else:
            (win[1],) = _exchange_wait([batch[2][1]], [h1], "gather_wait_in_1")
        proj = _matmul(
            h1, win[l], grid=(nt, N_DEV, 1), dims=NN, name=f"proj_{l}", after=pin,
            a_spec=pl.BlockSpec((TM, D_MODEL), lambda i, j, k: (i, 0)),
            b_spec=pl.BlockSpec((None, D_MODEL, IN_CHUNK), lambda i, j, k: (j, 0, 0)),
            o_spec=pl.BlockSpec((TM, IN_CHUNK), lambda i, j, k: (i, j)), o_shape=(t, D_IN), o_dtype=F32)
        o, lse, cat = _attn_fwd(proj, attn_out_g[l][None], nbatch, seq)
        pin = []
        if l == 0:
            batch[2] = _exchange_start([cast(ffn_down[0]), cast(w_in[1]), cast(w_out[1])], "gather_start_c", after=[o])
            pin = [_started(batch[2])]
        cat = _convmix_fwd(proj, cat, mcw[l], conv_out_g[l][None], nbatch, seq)
        (got,) = _exchange_wait([batch[1][0] if l == 0 else batch[2][2]], [cat], f"gather_wait_out_{l}")
        wout[l] = got.reshape(D_MODEL, D_MODEL)
        xmid = _matmul(
            cat, wout[l], grid=(nta, 1, 1), dims=NN, name=f"mix_out_{l}", after=pin,
            a_spec=pl.BlockSpec((TM_ACC, D_MODEL), lambda i, j, k: (i, 0)),
            b_spec=pl.BlockSpec((D_MODEL, D_MODEL), full),
            o_spec=pl.BlockSpec((TM_ACC, D_MODEL), lambda i, j, k: (i, 0)), o_shape=(t, D_MODEL), o_dtype=F32,
            res=xin, res_spec=pl.BlockSpec((TM_ACC, D_MODEL), lambda i, j, k: (i, 0)))
        h2 = _rms_fwd(xmid, norm2_g[l][None], f"rms2_fwd_{l}")
        if l == 0:
            (wup[0],) = _exchange_wait([batch[1][1]], [h2], "gather_wait_up_0")
        else:
            wup[1], got = _exchange_wait(batch[3], [h2], "gather_wait_ffn_1")
            wdown[1] = got.reshape(N_UP_PAIRS, UP_CHUNK, D_MODEL)
        pre = _matmul(
            h2, wup[l], grid=(nt, N_DEV, 1), dims=NN, name=f"ffn_up_{l}",
            a_spec=pl.BlockSpec((TM, D_MODEL), lambda i, j, k: (i, 0)),
            b_spec=pl.BlockSpec((None, D_MODEL, UP_CHUNK), lambda i, j, k: (j, 0, 0)),
            o_spec=pl.BlockSpec((None, TM, UP_CHUNK), lambda i, j, k: (j, i, 0)),
            o_shape=(N_DEV, t, UP_CHUNK), o_dtype=BF16).reshape(2, N_UP_PAIRS, t, UP_CHUNK)
        act = _ffn_act_fwd(pre, fcw[l], nbatch, seq)
        pin = []
        if l == 0:
            (got,) = _exchange_wait([batch[2][0]], [act], "gather_wait_down_0")
            wdown[0] = got.reshape(N_UP_PAIRS, UP_CHUNK, D_MODEL)
            batch[3] = _exchange_start([cast(ffn_up[1]), cast(ffn_down[1])], "gather_start_d", after=[act])
            pin = [_started(batch[3])]
        xout = _matmul(
            act, wdown[l], grid=(nta, 1, N_UP_PAIRS), dims=NN, name=f"ffn_down_{l}", after=pin,
            a_spec=pl.BlockSpec((None, TM_ACC, UP_CHUNK), lambda i, j, k: (k, i, 0)),
            b_spec=pl.BlockSpec((None, UP_CHUNK, D_MODEL), lambda i, j, k: (k, 0, 0)),
            o_spec=pl.BlockSpec((TM_ACC, D_MODEL), lambda i, j, k: (i, 0)), o_shape=(t, D_MODEL), o_dtype=F32,
            res=xmid, res_spec=pl.BlockSpec((TM_ACC, D_MODEL), lambda i, j, k: (i, 0)))
        saved.append((xin, h1, proj, o, lse, cat, xmid, h2, pre, act))
        xin = xout

    loss_part, dx, dxb, dgf = _loss_head(xin, final_norm_g[None], target, "loss_head")

    dg1, dg2, dga, dgc = [None] * DEPTH, [None] * DEPTH, [None] * DEPTH, [None] * DEPTH
    for l in reversed(range(DEPTH)):
        xin, h1, proj, o, lse, cat, xmid, h2, pre, act = saved[l]
        d_act = _matmul(
            dxb, wdown[l], grid=(nt, N_UP_PAIRS, 1), dims=NT, name=f"d_act_{l}",
            a_spec=pl.BlockSpec((TM, D_MODEL), lambda i, j, k: (i, 0)),
            b_spec=pl.BlockSpec((None, UP_CHUNK, D_MODEL), lambda i, j, k: (j, 0, 0)),
            o_spec=pl.BlockSpec((None, TM, UP_CHUNK), lambda i, j, k: (j, i, 0)),
            o_shape=(N_UP_PAIRS, t, UP_CHUNK), o_dtype=BF16)
        g_down = _matmul(
            act, dxb, grid=(N_UP_PAIRS, 1, nt), dims=TN, name=f"g_down_{l}",
            a_spec=pl.BlockSpec((None, TM, UP_CHUNK), lambda i, j, k: (i, k, 0)),
            b_spec=pl.BlockSpec((TM, D_MODEL), lambda i, j, k: (k, 0)),
            o_spec=pl.BlockSpec((None, UP_CHUNK, D_MODEL), lambda i, j, k: (i, 0, 0)),
            o_shape=(N_UP_PAIRS, UP_CHUNK, D_MODEL), o_dtype=BF16).reshape(N_DEV, down_rows, D_MODEL)
        d_pre, d_fcw = _ffn_act_bwd(pre, d_act, fcw[l], nbatch, seq)
        d_pre = d_pre.reshape(N_DEV, t, UP_CHUNK)
        dh2 = _matmul(
            d_pre, wup[l], grid=(nta, 1, N_DEV), dims=NT, name=f"d_h2_{l}",
            a_spec=pl.BlockSpec((None, TM_ACC, UP_CHUNK), lambda i, j, k: (k, i, 0)),
            b_spec=pl.BlockSpec((None, D_MODEL, UP_CHUNK), lambda i, j, k: (k, 0, 0)),
            o_spec=pl.BlockSpec((TM_ACC, D_MODEL), lambda i, j, k: (i, 0)), o_shape=(t, D_MODEL), o_dtype=F32)
        g_up = _matmul(
            h2, d_pre, grid=(1, N_DEV, nt), dims=TN, name=f"g_up_{l}",
            a_spec=pl.BlockSpec((TM, D_MODEL), lambda i, j, k: (k, 0)),
            b_spec=pl.BlockSpec((None, TM, UP_CHUNK), lambda i, j, k: (j, k, 0)),
            o_spec=pl.BlockSpec((None, D_MODEL, UP_CHUNK), lambda i, j, k: (j, 0, 0)),
            o_shape=(N_DEV, D_MODEL, UP_CHUNK), o_dtype=BF16)
        dxm, dxmb, dg2[l] = _rms_bwd(xmid, norm2_g[l][None], dh2, dx, f"rms2_bwd_{l}")
        g_out = _matmul(
            cat, dxmb, grid=(1, 1, nt), dims=TN, name=f"g_out_{l}",
            a_spec=pl.BlockSpec((TM, D_MODEL), lambda i, j, k: (k, 0)),
            b_spec=pl.BlockSpec((TM, D_MODEL), lambda i, j, k: (k, 0)),
            o_spec=pl.BlockSpec((D_MODEL, D_MODEL), full),
            o_shape=(D_MODEL, D_MODEL), o_dtype=BF16).reshape(N_DEV, out_rows, D_MODEL)
        pin = []
        if l == 0:
            early = _exchange_start([_Item(g_out, True), _Item(g_up, True), _Item(g_down, True)], "scatter_start_0a")
            pin = [_started(early)]
        d_cat = _matmul(
            dxmb, wout[l], grid=(nta, 1, 1), dims=NT, name=f"d_cat_{l}", after=pin,
            a_spec=pl.BlockSpec((TM_ACC, D_MODEL), lambda i, j, k: (i, 0)),
            b_spec=pl.BlockSpec((D_MODEL, D_MODEL), full),
            o_spec=pl.BlockSpec((TM_ACC, D_MODEL), lambda i, j, k: (i, 0)), o_shape=(t, D_MODEL), o_dtype=BF16)
        d_proj, dga[l] = _attn_bwd(proj, o, lse, d_cat, attn_out_g[l][None], nbatch, seq)
        d_proj, d_mcw, dgc[l] = _convmix_bwd(proj, d_cat, d_proj, mcw[l], conv_out_g[l][None], nbatch, seq)
        g_in = _matmul(
            h1, d_proj, grid=(1, N_DEV, nt), dims=TN, name=f"g_in_{l}",
            a_spec=pl.BlockSpec((TM, D_MODEL), lambda i, j, k: (k, 0)),
            b_spec=pl.BlockSpec((TM, IN_CHUNK), lambda i, j, k: (k, j)),
            o_spec=pl.BlockSpec((None, D_MODEL, IN_CHUNK), lambda i, j, k: (j, 0, 0)),
            o_shape=(N_DEV, D_MODEL, IN_CHUNK), o_dtype=BF16)
        g_cw = jnp.concatenate(
            [d_fcw.reshape(N_DEV, 3, UP_CHUNK), d_mcw.reshape(3, N_DEV, D_CONV // N_DEV).transpose(1, 0, 2)], axis=-1)
        if l == 0:
            late = _exchange_start([_Item(g_in, True), _Item(g_cw, True)], "scatter_start_0b")
            pin = [_started(late)]
        else:
            upper = _exchange_start(
                [_Item(g_in, True), _Item(g_out, True), _Item(g_up, True), _Item(g_down, True), _Item(g_cw, True)],
                "scatter_start_1")
            pin = [_started(upper)]
        dh1 = _matmul(
            d_proj, win[l], grid=(nta, 1, N_DEV), dims=NT, name=f"d_h1_{l}", after=pin,
            a_spec=pl.BlockSpec((TM_ACC, IN_CHUNK), lambda i, j, k: (i, k)),
            b_spec=pl.BlockSpec((None, D_MODEL, IN_CHUNK), lambda i, j, k: (k, 0, 0)),
            o_spec=pl.BlockSpec((TM_ACC, D_MODEL), lambda i, j, k: (i, 0)), o_shape=(t, D_MODEL), o_dtype=F32)
        dx, dxb, dg1[l] = _rms_bwd(xin, norm1_g[l][None], dh1, dxm, f"rms1_bwd_{l}")

    def pack_small(n1, a, c, n2, f):
        return jnp.concatenate(
            [n1, n2, f[None], jnp.concatenate([a, c], axis=-1), jnp.zeros((1, D_MODEL), F32)], axis=0)[None]

    land_in1, land_out1, land_up1, land_down1, land_cw1 = _exchange_wait(upper, [dxb], "scatter_wait_1")
    land_out0, land_up0, land_down0 = _exchange_wait(early, [dxb], "scatter_wait_0a")
    res_out = _adamw([land_out0, land_out1], w_out, m_w_out, v_w_out, out_rows, "adamw_w_out")
    res_up = _adamw([land_up0, land_up1], ffn_up, m_ffn_up, v_ffn_up, 256, "adamw_ffn_up")
    res_down = _adamw([land_down0, land_down1], ffn_down, m_ffn_down, v_ffn_down, down_rows, "adamw_ffn_down")
    small = jnp.concatenate(
        [dg1[0], dg1[1], dg2[0], dg2[1], dgf,
         jnp.concatenate([dga[0], dgc[0]], axis=-1), jnp.concatenate([dga[1], dgc[1]], axis=-1),
         jnp.zeros((1, D_MODEL), F32)], axis=0)
    (land_small,) = _exchange([_Item(small, False)], "gather_gain_grads")
    res_small = _adamw(
        [land_small], pack_small(norm1_g, attn_out_g, conv_out_g, norm2_g, final_norm_g),
        pack_small(m_norm1_g, m_attn_out_g, m_conv_out_g, m_norm2_g, m_final_norm_g),
        pack_small(v_norm1_g, v_attn_out_g, v_conv_out_g, v_norm2_g, v_final_norm_g), SUBLANES, "adamw_gains")
    land_in0, land_cw0 = _exchange_wait(late, [res_small[0], res_down[0], res_up[0], res_out[0]], "scatter_wait_0b")
    res_in = _adamw([land_in0, land_in1], w_in, m_w_in, v_w_in, 256, "adamw_w_in")
    res_cw = _adamw(
        [land_cw0, land_cw1], cw_local, jnp.concatenate([m_ffn_conv_w, m_mix_conv_w], axis=-1),
        jnp.concatenate([v_ffn_conv_w, v_mix_conv_w], axis=-1), 3, "adamw_conv_w")

    loss = lax.psum(loss_part[0, 0], ("x", "y", "c"))

    def unpack(kind):
        s = res_small[kind][0]
        cwr = res_cw[kind]
        return (s[0:2], res_in[kind], cwr[..., UP_CHUNK:], s[5:7, :D_ATTN], s[5:7, D_ATTN:], res_out[kind],
                s[2:4], res_up[kind], cwr[..., :UP_CHUNK], res_down[kind], s[4])

    return (loss, dx.reshape(nbatch, seq, d), *unpack(0), *unpack(1), *unpack(2), *unpack(3))
```

```python
import math

import jax
import jax.numpy as jnp
from jax import lax
from jax.experimental import pallas as pl
from jax.experimental.pallas import tpu as pltpu

F32 = jnp.float32
BF16 = jnp.bfloat16

D_MODEL = 1024
D_ATTN = 512
D_CONV = 512
HEAD_DIM = 64
N_HEADS = 8
D_FF = 2816
DEPTH = 2
D_IN = 3 * D_ATTN + 3 * D_CONV
EPS = 1e-6
DILATIONS = (1, 4, 16)
BAND = 128
N_DEV = 8
IN_CHUNK = D_IN // N_DEV
UP_CHUNK = 2 * D_FF // N_DEV
N_UP_PAIRS = N_DEV // 2
CW_PACK = UP_CHUNK + D_CONV // N_DEV
ADAM_LR = 0.001
ADAM_B1 = 0.9
ADAM_B2 = 0.999
ADAM_EPS = 1e-08
ADAM_WD = 0.01
ADAM_STEP = 10
LANES = 128
SUBLANES = 8
VMEM_LIMIT = 56 * 1024 * 1024

NEG = -1e30
MESH = pl.DeviceIdType.MESH


def _params(sem=None, vmem=VMEM_LIMIT):
    return pltpu.CompilerParams(dimension_semantics=sem, vmem_limit_bytes=vmem)


NN = (((1,), (0,)), ((), ()))
NT = (((1,), (1,)), ((), ()))
TN = (((0,), (0,)), ((), ()))


def _matmul(a, b, *, grid, a_spec, b_spec, o_spec, o_shape, o_dtype, dims, name, res=None, res_spec=None, after=()):
    nk = grid[2]
    o_block = tuple(s for s in o_spec.block_shape if s is not None)
    na = len(after)

    def body(*refs):
        refs = refs[:2 + (res is not None)] + refs[2 + (res is not None) + na:]
        if res is None:
            a_ref, b_ref, o_ref, *scr = refs
            r_ref = None
        else:
            a_ref, b_ref, r_ref, o_ref, *scr = refs
        def dot(av, bv):
            return lax.dot_general(av.astype(BF16), bv.astype(BF16), dims, preferred_element_type=F32)

        if len(a_ref.shape) == 3:
            part = dot(a_ref[0], b_ref[0])
            for c in range(1, a_ref.shape[0]):
                part = part + dot(a_ref[c], b_ref[c])
        else:
            part = dot(a_ref[...], b_ref[...])

        def finish(total):
            if r_ref is not None:
                total = total + r_ref[...]
            o_ref[...] = total.astype(o_dtype)

        if nk == 1:
            finish(part)
        else:
            acc = scr[0]
            k = pl.program_id(2)

            @pl.when(k == 0)
            def _():
                acc[...] = part

            @pl.when(k > 0)
            def _():
                acc[...] += part

            @pl.when(k == nk - 1)
            def _():
                finish(acc[...])

    in_specs = [a_spec, b_spec] + ([res_spec] if res is not None else []) + [pl.BlockSpec(memory_space=pl.ANY)] * na
    args = (a, b) + ((res,) if res is not None else ()) + tuple(after)
    return pl.pallas_call(
        body, name=name, grid=grid, in_specs=in_specs, out_specs=o_spec,
        out_shape=jax.ShapeDtypeStruct(o_shape, o_dtype),
        scratch_shapes=[pltpu.VMEM(o_block, F32)] if nk > 1 else [],
        compiler_params=_params(("parallel", "parallel", "arbitrary")),
    )(*args)


ROW_TILE = 512


def _rms_fwd(x, g, name):
    t, d = x.shape

    def body(x_ref, g_ref, h_ref):
        xv = x_ref[...]
        r = lax.rsqrt(jnp.mean(xv * xv, axis=-1, keepdims=True) + EPS)
        h_ref[...] = (xv * r * g_ref[...]).astype(BF16)

    return pl.pallas_call(
        body, name=name, grid=(t // ROW_TILE,),
        in_specs=[pl.BlockSpec((ROW_TILE, d), lambda i: (i, 0)), pl.BlockSpec((1, d), lambda i: (0, 0))],
        out_specs=pl.BlockSpec((ROW_TILE, d), lambda i: (i, 0)),
        out_shape=jax.ShapeDtypeStruct((t, d), BF16),
        compiler_params=_params(("parallel",)),
    )(x, g)


def _rms_bwd(x, g, dh, dres, name):
    t, d = x.shape

    def body(x_ref, g_ref, dh_ref, dres_ref, dx_ref, dxb_ref, dg_ref):
        xv = x_ref[...]
        r = lax.rsqrt(jnp.mean(xv * xv, axis=-1, keepdims=True) + EPS)
        xh = xv * r
        dhv = dh_ref[...]
        gd = dhv * g_ref[...]
        dx = r * (gd - xh * jnp.mean(gd * xh, axis=-1, keepdims=True)) + dres_ref[...]
        dx_ref[...] = dx
        dxb_ref[...] = dx.astype(BF16)
        part = jnp.sum(dhv * xh, axis=0, keepdims=True)

        @pl.when(pl.program_id(0) == 0)
        def _():
            dg_ref[...] = part

        @pl.when(pl.program_id(0) > 0)
        def _():
            dg_ref[...] += part

    row = pl.BlockSpec((ROW_TILE, d), lambda i: (i, 0))
    vec = pl.BlockSpec((1, d), lambda i: (0, 0))
    return pl.pallas_call(
        body, name=name, grid=(t // ROW_TILE,),
        in_specs=[row, vec, row, row], out_specs=[row, row, vec],
        out_shape=[jax.ShapeDtypeStruct((t, d), F32), jax.ShapeDtypeStruct((t, d), BF16),
                   jax.ShapeDtypeStruct((1, d), F32)],
        compiler_params=_params(("arbitrary",)),
    )(x, g, dh, dres)


def _loss_head(x, g, target, name):
    t, d = x.shape

    def body(x_ref, g_ref, t_ref, loss_ref, dx_ref, dxb_ref, dg_ref):
        xv = x_ref[...]
        r = lax.rsqrt(jnp.mean(xv * xv, axis=-1, keepdims=True) + EPS)
        xh = xv * r
        gv = g_ref[...]
        err = xh * gv - t_ref[...]
        loss = jnp.full((1, LANES), 0.5 / d, F32) * jnp.sum(err * err)
        dy = err * (1.0 / d)
        gd = dy * gv
        dx = r * (gd - xh * jnp.mean(gd * xh, axis=-1, keepdims=True))
        dx_ref[...] = dx
        dxb_ref[...] = dx.astype(BF16)
        part = jnp.sum(dy * xh, axis=0, keepdims=True)

        @pl.when(pl.program_id(0) == 0)
        def _():
            dg_ref[...] = part
            loss_ref[...] = loss

        @pl.when(pl.program_id(0) > 0)
        def _():
            dg_ref[...] += part
            loss_ref[...] += loss

    row = pl.BlockSpec((ROW_TILE, d), lambda i: (i, 0))
    vec = pl.BlockSpec((1, d), lambda i: (0, 0))
    return pl.pallas_call(
        body, name=name, grid=(t // ROW_TILE,),
        in_specs=[row, vec, row],
        out_specs=[pl.BlockSpec((1, LANES), lambda i: (0, 0)), row, row, vec],
        out_shape=[jax.ShapeDtypeStruct((1, LANES), F32), jax.ShapeDtypeStruct((t, d), F32),
                   jax.ShapeDtypeStruct((t, d), BF16), jax.ShapeDtypeStruct((1, d), F32)],
        compiler_params=_params(("arbitrary",)),
    )(x, g, target)


def _group_matrix(n):
    shift = int(math.log2(HEAD_DIM))
    r = lax.broadcasted_iota(jnp.int32, (n, n), 0) >> shift
    c = lax.broadcasted_iota(jnp.int32, (n, n), 1) >> shift
    return (r == c).astype(BF16)


def _group_sum(v, gmat):
    hi = v.astype(BF16)
    rest = v - hi.astype(F32)
    mid = rest.astype(BF16)
    lo = (rest - mid.astype(F32)).astype(BF16)

    def dot(p):
        return jnp.dot(p, gmat, preferred_element_type=F32)

    return dot(hi) + dot(mid) + dot(lo)


def _shift_rows(ext, k):
    return pltpu.roll(ext, k % ext.shape[0], 0)


def _store_columns(stage, out_hbm, sems, row0, nrows, col_blocks):
    rows = pl.ds(pl.multiple_of(row0, SUBLANES * 2), nrows)
    copies = [
        pltpu.make_async_copy(stage.at[i], out_hbm.at[rows, pl.ds(pl.multiple_of(cb * LANES, LANES), LANES)], sems.at[i])
        for i, cb in enumerate(col_blocks)
    ]
    for cp in copies:
        cp.start()
    for cp in copies:
        cp.wait()


def _attn_consts(width):
    i = lax.broadcasted_iota(jnp.int32, (BAND, width), 0)
    j = lax.broadcasted_iota(jnp.int32, (BAND, width), 1)
    dist = (width - BAND) + i - j
    inwin = (dist >= 0) & (dist <= BAND)
    return dist.astype(F32), inwin, j


def _head_masks():
    lane = lax.broadcasted_iota(jnp.int32, (1, LANES), 1)
    return [(lane < HEAD_DIM).astype(F32), (lane >= HEAD_DIM).astype(F32)]


def _permute_in(src_ref, dst_ref, dil, seq):
    length = seq // dil
    for r in range(dil):
        dst_ref[pl.ds(r * length, length), :] = src_ref[pl.ds(r, length, stride=dil), :].astype(dst_ref.dtype)


def _slopes_table():
    slopes = 2.0 ** (-8.0 * jnp.arange(1, N_HEADS + 1, dtype=F32) / N_HEADS)
    return jnp.broadcast_to(slopes[:, None], (N_HEADS, 2 * BAND))


def _attn_fwd(proj, attn_g, nbatch, seq):
    t = nbatch * seq
    nblk = seq // BAND
    scale = HEAD_DIM ** -0.5

    def body(q_ref, k_ref, v_ref, g_ref, sl_ref, o_ref, lse_ref, cat_ref, pq, pk, pv, po, pm, pll, ao, am, al):
        hp = pl.program_id(1)
        hmask = _head_masks()
        slope = [sl_ref[pl.ds(2 * hp + hh, 1), :] for hh in range(2)]

        def run_branch(dil, qs, ks, vs, osink, msink, lsink):
            nb = seq // dil // BAND
            width = 2 * BAND if nb > 1 else BAND
            distf, inwin, jcol = _attn_consts(width)
            bias = [distf * (slope[hh][:, :width] * (-float(dil))) for hh in range(2)]

            def blk(m, carry):
                row0 = pl.multiple_of(m * BAND, BAND)
                q = qs[pl.ds(row0, BAND), :] * scale
                if nb > 1:
                    prow = pl.multiple_of(jnp.maximum(m - 1, 0) * BAND, BAND)
                    kk = jnp.concatenate([ks[pl.ds(prow, BAND), :], ks[pl.ds(row0, BAND), :]], axis=0)
                    vv = jnp.concatenate([vs[pl.ds(prow, BAND), :], vs[pl.ds(row0, BAND), :]], axis=0)
                    valid = inwin & (jcol >= jnp.where((m % nb) == 0, BAND, 0))
                else:
                    kk = ks[pl.ds(row0, BAND), :]
                    vv = vs[pl.ds(row0, BAND), :]
                    valid = inwin
                kb = kk.astype(BF16)
                o = jnp.zeros((BAND, LANES), F32)
                mfull = jnp.zeros((BAND, LANES), F32)
                lfull = jnp.zeros((BAND, LANES), F32)
                for hh in range(2):
                    qh = (q * hmask[hh]).astype(BF16)
                    s = lax.dot_general(qh, kb, NT, preferred_element_type=F32)
                    s = jnp.where(valid, s + bias[hh], NEG)
                    mh = jnp.max(s, axis=1, keepdims=True)
                    p = jnp.exp(s - mh)
                    lh = jnp.sum(p, axis=1, keepdims=True)
                    o = o + jnp.dot(p.astype(BF16), (vv * hmask[hh]).astype(BF16), preferred_element_type=F32)
                    mfull = mfull + mh * hmask[hh]
                    lfull = lfull + lh * hmask[hh]
                osink[pl.ds(row0, BAND), :] = o
                msink[pl.ds(row0, BAND), :] = mfull
                lsink[pl.ds(row0, BAND), :] = lfull
                return carry

            lax.fori_loop(0, nblk, blk, 0)

        run_branch(1, q_ref, k_ref, v_ref, ao, am, al)
        for dil in DILATIONS[1:]:
            length = seq // dil
            _permute_in(q_ref, pq, dil, seq)
            _permute_in(k_ref, pk, dil, seq)
            _permute_in(v_ref, pv, dil, seq)
            run_branch(dil, pq, pk, pv, po, pm, pll)
            for r in range(dil):
                nat = pl.ds(r, length, stride=dil)
                per = pl.ds(r * length, length)
                m0 = am[nat, :]
                mb = pm[per, :]
                mn = jnp.maximum(m0, mb)
                e0 = jnp.exp(m0 - mn)
                eb = jnp.exp(mb - mn)
                ao[nat, :] = ao[nat, :] * e0 + po[per, :] * eb
                al[nat, :] = al[nat, :] * e0 + pll[per, :] * eb
                am[nat, :] = mn

        gmat = _group_matrix(LANES)
        gv = g_ref[...]

        def fin(c, carry):
            rows = pl.ds(pl.multiple_of(c * 256, 256), 256)
            lv = al[rows, :]
            o = ao[rows, :] / lv
            o_ref[rows, :] = o
            lse_ref[rows, :] = am[rows, :] + jnp.log(lv)
            ms = _group_sum(o * o, gmat) * (1.0 / HEAD_DIM)
            cat_ref[rows, :] = (o * lax.rsqrt(ms + EPS) * gv).astype(BF16)
            return carry

        lax.fori_loop(0, seq // 256, fin, 0)

    nq = D_ATTN // LANES
    blk = lambda off: pl.BlockSpec((seq, LANES), lambda b, h: (b, h + off))
    scratch = [pltpu.VMEM((seq, LANES), F32) for _ in range(9)]
    return pl.pallas_call(
        body, name="attn_fwd", grid=(nbatch, nq),
        in_specs=[blk(0), blk(nq), blk(2 * nq), pl.BlockSpec((1, LANES), lambda b, h: (0, h)),
                  pl.BlockSpec((N_HEADS, 2 * BAND), lambda b, h: (0, 0))],
        out_specs=[blk(0), blk(0), blk(0)],
        out_shape=[jax.ShapeDtypeStruct((t, D_ATTN), F32), jax.ShapeDtypeStruct((t, D_ATTN), F32),
                   jax.ShapeDtypeStruct((t, D_MODEL), BF16)],
        scratch_shapes=scratch,
        compiler_params=_params(("parallel", "parallel")),
    )(proj, proj, proj, attn_g, _slopes_table())


def _attn_bwd(proj, o, lse, d_cat, attn_g, nbatch, seq):
    t = nbatch * seq
    nblk = seq // BAND
    scale = HEAD_DIM ** -0.5

    def body(q_ref, k_ref, v_ref, o_ref, lse_ref, dy_ref, g_ref, sl_ref, dproj_ref, dg_ref,
             do_n, dl_n, dq_n, dk_n, dv_n, pq, pk, pv, pdo, plse, pdl, pdq, pdk, pdv, stage, sems):
        hp = pl.program_id(0)
        hmask = _head_masks()
        slope = [sl_ref[pl.ds(2 * hp + hh, 1), :] for hh in range(2)]
        gmat = _group_matrix(LANES)
        gv = g_ref[...]

        def prep(c, dg):
            rows = pl.ds(pl.multiple_of(c * 256, 256), 256)
            ov = o_ref[rows, :]
            dyn = dy_ref[rows, :].astype(F32)
            r = lax.rsqrt(_group_sum(ov * ov, gmat) * (1.0 / HEAD_DIM) + EPS)
            gd = dyn * gv
            oh = ov * r
            do = r * (gd - oh * (_group_sum(gd * oh, gmat) * (1.0 / HEAD_DIM)))
            do_n[rows, :] = do
            dl_n[rows, :] = _group_sum(do * ov, gmat)
            return dg + jnp.sum(dyn * oh, axis=0, keepdims=True)

        dg = lax.fori_loop(0, seq // 256, prep, jnp.zeros((1, LANES), F32))

        @pl.when(pl.program_id(1) == 0)
        def _():
            dg_ref[...] = dg

        @pl.when(pl.program_id(1) > 0)
        def _():
            dg_ref[...] += dg

        def clear(*refs):
            def step(c, carry):
                rows = pl.ds(pl.multiple_of(c * 256, 256), 256)
                for ref in refs:
                    ref[rows, :] = jnp.zeros((256, LANES), F32)
                return carry

            lax.fori_loop(0, seq // 256, step, 0)

        clear(dq_n, dk_n, dv_n)

        def run_branch(dil, qs, ks, vs, dos, lses, dls, dqs, dks, dvs):
            nb = seq // dil // BAND
            width = 2 * BAND if nb > 1 else BAND
            distf, inwin, jcol = _attn_consts(width)
            bias = [distf * (slope[hh][:, :width] * (-float(dil))) for hh in range(2)]

            def blk(m, carry):
                row0 = pl.multiple_of(m * BAND, BAND)
                cur = pl.ds(row0, BAND)
                q = qs[cur, :] * scale
                dov = dos[cur, :]
                lsev = lses[cur, :]
                dlv = dls[cur, :]
                if nb > 1:
                    prow = pl.multiple_of(jnp.maximum(m - 1, 0) * BAND, BAND)
                    prev = pl.ds(prow, BAND)
                    kk = jnp.concatenate([ks[prev, :], ks[cur, :]], axis=0)
                    vv = jnp.concatenate([vs[prev, :], vs[cur, :]], axis=0)
                    valid = inwin & (jcol >= jnp.where((m % nb) == 0, BAND, 0))
                else:
                    kk = ks[cur, :]
                    vv = vs[cur, :]
                    valid = inwin
                kb = kk.astype(BF16)
                vb = vv.astype(BF16)
                dq = jnp.zeros((BAND, LANES), F32)
                dkk = jnp.zeros((width, LANES), F32)
                dvv = jnp.zeros((width, LANES), F32)
                for hh in range(2):
                    c0 = hh * HEAD_DIM
                    qh = (q * hmask[hh]).astype(BF16)
                    doh = (dov * hmask[hh]).astype(BF16)
                    s = lax.dot_general(qh, kb, NT, preferred_element_type=F32) + bias[hh]
                    p = jnp.where(valid, jnp.exp(s - lsev[:, c0:c0 + 1]), 0.0)
                    dp = lax.dot_general(doh, vb, NT, preferred_element_type=F32)
                    ds = (p * (dp - dlv[:, c0:c0 + 1])).astype(BF16)
                    dq = dq + jnp.dot(ds, kb, preferred_element_type=F32) * hmask[hh]
                    dkk = dkk + lax.dot_general(ds, qh, TN, preferred_element_type=F32)
                    dvv = dvv + lax.dot_general(p.astype(BF16), doh, TN, preferred_element_type=F32)
                dqs[cur, :] += dq
                if nb > 1:
                    dks[prev, :] += dkk[:BAND]
                    dvs[prev, :] += dvv[:BAND]
                    dks[cur, :] += dkk[BAND:]
                    dvs[cur, :] += dvv[BAND:]
                else:
                    dks[cur, :] += dkk
                    dvs[cur, :] += dvv
                return carry

            lax.fori_loop(0, nblk, blk, 0)

        run_branch(1, q_ref, k_ref, v_ref, do_n, lse_ref, dl_n, dq_n, dk_n, dv_n)
        for dil in DILATIONS[1:]:
            length = seq // dil
            for src, dst in ((q_ref, pq), (k_ref, pk), (v_ref, pv), (do_n, pdo), (lse_ref, plse), (dl_n, pdl)):
                _permute_in(src, dst, dil, seq)
            clear(pdq, pdk, pdv)
            run_branch(dil, pq, pk, pv, pdo, plse, pdl, pdq, pdk, pdv)
            for r in range(dil):
                nat = pl.ds(r, length, stride=dil)
                per = pl.ds(r * length, length)
                dq_n[nat, :] += pdq[per, :]
                dk_n[nat, :] += pdk[per, :]
                dv_n[nat, :] += pdv[per, :]

        def emit(c, carry):
            rows = pl.ds(pl.multiple_of(c * 256, 256), 256)
            stage[0, rows, :] = (dq_n[rows, :] * scale).astype(BF16)
            stage[1, rows, :] = dk_n[rows, :].astype(BF16)
            stage[2, rows, :] = dv_n[rows, :].astype(BF16)
            return carry

        lax.fori_loop(0, seq // 256, emit, 0)
        _store_columns(stage, dproj_ref, sems, pl.program_id(1) * seq, seq, [hp, nq + hp, 2 * nq + hp])

    nq = D_ATTN // LANES
    blk = lambda off: pl.BlockSpec((seq, LANES), lambda h, b: (b, h + off))
    vec = pl.BlockSpec((1, LANES), lambda h, b: (0, h))
    scratch = [pltpu.VMEM((seq, LANES), F32) for _ in range(14)]
    scratch += [pltpu.VMEM((3, seq, LANES), BF16), pltpu.SemaphoreType.DMA((3,))]
    d_proj, dg = pl.pallas_call(
        body, name="attn_bwd", grid=(nq, nbatch),
        in_specs=[blk(0), blk(nq), blk(2 * nq), blk(0), blk(0), blk(0), vec,
                  pl.BlockSpec((N_HEADS, 2 * BAND), lambda h, b: (0, 0))],
        out_specs=[pl.BlockSpec(memory_space=pl.ANY), vec],
        out_shape=[jax.ShapeDtypeStruct((t, D_IN), BF16), jax.ShapeDtypeStruct((1, D_ATTN), F32)],
        scratch_shapes=scratch,
        compiler_params=_params(("arbitrary", "arbitrary")),
    )(proj, proj, proj, o, lse, d_cat, attn_g, _slopes_table())
    return d_proj, dg


HALO = 2 * SUBLANES


def _window(ref, c, rows, nchunks, after):
    row0 = pl.multiple_of(c * rows, rows)
    prev0 = pl.multiple_of(jnp.maximum(row0 - HALO, 0), HALO)
    parts = [ref[pl.ds(prev0, HALO), :].astype(F32) * (c > 0).astype(F32), ref[pl.ds(row0, rows), :].astype(F32)]
    if after:
        next0 = pl.multiple_of(jnp.minimum(row0 + rows, (nchunks - 1) * rows), HALO)
        parts.append(ref[pl.ds(next0, HALO), :].astype(F32) * (c < nchunks - 1).astype(F32))
    return jnp.concatenate(parts, axis=0)


def _conv(z, w):
    return w[0:1] * _shift_rows(z, 2) + w[1:2] * _shift_rows(z, 1) + w[2:3] * z


def _conv_t(dy, w):
    return w[2:3] * dy + w[1:2] * _shift_rows(dy, -1) + w[0:1] * _shift_rows(dy, -2)


def _conv_wgrad(dy, z, cur):
    return [jnp.sum((dy * _shift_rows(z, 2 - k))[cur], axis=0, keepdims=True) for k in range(3)]


MIX_ROWS = 256
GATE_B_BLOCK = 3 * D_ATTN // LANES
GATE_C_BLOCK = GATE_B_BLOCK + D_CONV // LANES
U_BLOCK = GATE_C_BLOCK + D_CONV // LANES


def _convmix_fwd(proj, cat, mcw, conv_g, nbatch, seq):
    nchunks = seq // MIX_ROWS

    def body(gb_ref, gc_ref, u_ref, w_ref, g_ref, cat_in, cat_ref):
        del cat_in
        gmat = _group_matrix(LANES)
        w = w_ref[...]
        gv = g_ref[...]

        def step(c, carry):
            cur = pl.ds(pl.multiple_of(c * MIX_ROWS, MIX_ROWS), MIX_ROWS)
            z = _window(gc_ref, c, MIX_ROWS, nchunks, False) * _window(u_ref, c, MIX_ROWS, nchunks, False)
            y = gb_ref[cur, :] * _conv(z, w)[HALO:]
            ms = _group_sum(y * y, gmat) * (1.0 / HEAD_DIM)
            cat_ref[cur, :] = (y * lax.rsqrt(ms + EPS) * gv).astype(BF16)
            return carry

        lax.fori_loop(0, nchunks, step, 0)

    nc = D_CONV // LANES
    blk = lambda off: pl.BlockSpec((seq, LANES), lambda b, j: (b, j + off))
    return pl.pallas_call(
        body, name="convmix_fwd", grid=(nbatch, nc),
        in_specs=[blk(GATE_B_BLOCK), blk(GATE_C_BLOCK), blk(U_BLOCK),
                  pl.BlockSpec((3, LANES), lambda b, j: (0, j)), pl.BlockSpec((1, LANES), lambda b, j: (0, j)),
                  pl.BlockSpec(memory_space=pl.ANY)],
        out_specs=blk(D_ATTN // LANES),
        out_shape=jax.ShapeDtypeStruct(cat.shape, cat.dtype),
        input_output_aliases={5: 0},
        compiler_params=_params(("parallel", "parallel")),
    )(proj, proj, proj, mcw, conv_g, cat)


def _convmix_bwd(proj, d_cat, d_proj, mcw, conv_g, nbatch, seq):
    nchunks = seq // MIX_ROWS

    def body(gb_ref, gc_ref, u_ref, dy_ref, w_ref, g_ref, dproj_in, dproj_ref, dw_ref, dg_ref, stage, sems):
        del dproj_in
        cb = pl.program_id(0)
        b = pl.program_id(1)
        gmat = _group_matrix(LANES)
        w = w_ref[...]
        gv = g_ref[...]
        cur = slice(HALO, HALO + MIX_ROWS)

        def step(c, carry):
            rows = pl.ds(pl.multiple_of(c * MIX_ROWS, MIX_ROWS), MIX_ROWS)
            gb = _window(gb_ref, c, MIX_ROWS, nchunks, True)
            gc = _window(gc_ref, c, MIX_ROWS, nchunks, True)
            u = _window(u_ref, c, MIX_ROWS, nchunks, True)
            dyn = _window(dy_ref, c, MIX_ROWS, nchunks, True)
            z = gc * u
            conv = _conv(z, w)
            y = gb * conv
            r = lax.rsqrt(_group_sum(y * y, gmat) * (1.0 / HEAD_DIM) + EPS)
            yh = y * r
            gd = dyn * gv
            dy = r * (gd - yh * (_group_sum(gd * yh, gmat) * (1.0 / HEAD_DIM)))
            dc = dy * gb
            dz = _conv_t(dc, w)
            stage[0, rows, :] = (dy * conv)[cur].astype(BF16)
            stage[1, rows, :] = (dz * u)[cur].astype(BF16)
            stage[2, rows, :] = (dz * gc)[cur].astype(BF16)
            dws = _conv_wgrad(dc, z, cur)
            dg = jnp.sum((dyn * yh)[cur], axis=0, keepdims=True)
            return tuple(a + d for a, d in zip(carry, dws + [dg]))

        zero = jnp.zeros((1, LANES), F32)
        dw0, dw1, dw2, dg = lax.fori_loop(0, nchunks, step, (zero, zero, zero, zero))

        @pl.when(b == 0)
        def _():
            dw_ref[0:1, :] = dw0
            dw_ref[1:2, :] = dw1
            dw_ref[2:3, :] = dw2
            dg_ref[...] = dg

        @pl.when(b > 0)
        def _():
            dw_ref[0:1, :] += dw0
            dw_ref[1:2, :] += dw1
            dw_ref[2:3, :] += dw2
            dg_ref[...] += dg

        _store_columns(stage, dproj_ref, sems, b * seq, seq, [GATE_B_BLOCK + cb, GATE_C_BLOCK + cb, U_BLOCK + cb])

    nc = D_CONV // LANES
    blk = lambda off: pl.BlockSpec((seq, LANES), lambda j, b: (b, j + off))
    return pl.pallas_call(
        body, name="convmix_bwd", grid=(nc, nbatch),
        in_specs=[blk(GATE_B_BLOCK), blk(GATE_C_BLOCK), blk(U_BLOCK), blk(D_ATTN // LANES),
                  pl.BlockSpec((3, LANES), lambda j, b: (0, j)), pl.BlockSpec((1, LANES), lambda j, b: (0, j)),
                  pl.BlockSpec(memory_space=pl.ANY)],
        out_specs=[pl.BlockSpec(memory_space=pl.ANY), pl.BlockSpec((3, LANES), lambda j, b: (0, j)),
                   pl.BlockSpec((1, LANES), lambda j, b: (0, j))],
        out_shape=[jax.ShapeDtypeStruct(d_proj.shape, d_proj.dtype), jax.ShapeDtypeStruct((3, D_CONV), F32),
                   jax.ShapeDtypeStruct((1, D_CONV), F32)],
        scratch_shapes=[pltpu.VMEM((3, seq, LANES), BF16), pltpu.SemaphoreType.DMA((3,))],
        input_output_aliases={6: 0},
        compiler_params=_params(("arbitrary", "arbitrary")),
    )(proj, proj, proj, d_cat, mcw, conv_g, d_proj)


FFN_ROWS = 128


def _ffn_act_fwd(pre, fcw, nbatch, seq):
    t = nbatch * seq
    nchunks = seq // FFN_ROWS

    def body(pre_ref, w_ref, act_ref):
        wa = w_ref[0]
        wc = w_ref[1]

        def step(c, carry):
            cur = pl.ds(pl.multiple_of(c * FFN_ROWS, FFN_ROWS), FFN_ROWS)
            a = _conv(_window(pre_ref.at[0], c, FFN_ROWS, nchunks, False), wa)[HALO:]
            v = _conv(_window(pre_ref.at[1], c, FFN_ROWS, nchunks, False), wc)[HALO:]
            act_ref[cur, :] = (a * jax.nn.sigmoid(a) * v).astype(BF16)
            return carry

        lax.fori_loop(0, nchunks, step, 0)

    return pl.pallas_call(
        body, name="ffn_act_fwd", grid=(N_UP_PAIRS, nbatch),
        in_specs=[pl.BlockSpec((2, None, seq, UP_CHUNK), lambda i, b: (0, i, b, 0)),
                  pl.BlockSpec((2, None, 3, UP_CHUNK), lambda i, b: (0, i, 0, 0))],
        out_specs=pl.BlockSpec((None, seq, UP_CHUNK), lambda i, b: (i, b, 0)),
        out_shape=jax.ShapeDtypeStruct((N_UP_PAIRS, t, UP_CHUNK), BF16),
        compiler_params=_params(("parallel", "parallel")),
    )(pre, fcw)


def _ffn_act_bwd(pre, d_act, fcw, nbatch, seq):
    nchunks = seq // FFN_ROWS

    def body(pre_ref, da_ref, w_ref, dpre_ref, dw_ref):
        b = pl.program_id(1)
        wa = w_ref[0]
        wc = w_ref[1]
        cur = slice(HALO, HALO + FFN_ROWS)

        def step(c, carry):
            rows = pl.ds(pl.multiple_of(c * FFN_ROWS, FFN_ROWS), FFN_ROWS)
            pg = _window(pre_ref.at[0], c, FFN_ROWS, nchunks, True)
            pv = _window(pre_ref.at[1], c, FFN_ROWS, nchunks, True)
            dact = _window(da_ref, c, FFN_ROWS, nchunks, True)
            a = _conv(pg, wa)
            v = _conv(pv, wc)
            sg = jax.nn.sigmoid(a)
            da = dact * v * (sg * (1.0 + a * (1.0 - sg)))
            dv = dact * (a * sg)
            dpre_ref[0, rows, :] = _conv_t(da, wa)[cur].astype(BF16)
            dpre_ref[1, rows, :] = _conv_t(dv, wc)[cur].astype(BF16)
            return tuple(acc + d for acc, d in zip(carry, _conv_wgrad(da, pg, cur) + _conv_wgrad(dv, pv, cur)))

        zero = jnp.zeros((1, UP_CHUNK), F32)
        sums = lax.fori_loop(0, nchunks, step, (zero,) * 6)

        @pl.when(b == 0)
        def _():
            for i in range(6):
                dw_ref[i // 3, pl.ds(i % 3, 1), :] = sums[i]

        @pl.when(b > 0)
        def _():
            for i in range(6):
                dw_ref[i // 3, pl.ds(i % 3, 1), :] += sums[i]

    pair = pl.BlockSpec((2, None, seq, UP_CHUNK), lambda i, b: (0, i, b, 0))
    wspec = pl.BlockSpec((2, None, 3, UP_CHUNK), lambda i, b: (0, i, 0, 0))
    return pl.pallas_call(
        body, name="ffn_act_bwd", grid=(N_UP_PAIRS, nbatch),
        in_specs=[pair, pl.BlockSpec((None, seq, UP_CHUNK), lambda i, b: (i, b, 0)), wspec],
        out_specs=[pair, wspec],
        out_shape=[jax.ShapeDtypeStruct(pre.shape, BF16), jax.ShapeDtypeStruct(fcw.shape, F32)],
        compiler_params=_params(("parallel", "arbitrary")),
    )(pre, d_act, fcw)


def _adamw(lands, w, m, v, row_tile, name):
    nl = len(lands)
    _, nr, ncol = lands[0].shape
    c1 = 1.0 - ADAM_B1 ** ADAM_STEP
    c2 = 1.0 - ADAM_B2 ** ADAM_STEP

    def body(*refs):
        land_refs = refs[:nl]
        w_ref, m_ref, v_ref, g_ref, d_ref, mo_ref, vo_ref = refs[nl:]
        for l in range(nl):
            @pl.when(pl.program_id(0) == l)
            def _(l=l):
                g = land_refs[l][0].astype(F32)
                for j in range(1, N_DEV):
                    g = g + land_refs[l][j].astype(F32)
                g_ref[...] = g

        g = g_ref[...]
        m2 = ADAM_B1 * m_ref[...] + (1.0 - ADAM_B1) * g
        v2 = ADAM_B2 * v_ref[...] + (1.0 - ADAM_B2) * (g * g)
        mo_ref[...] = m2
        vo_ref[...] = v2
        d_ref[...] = -ADAM_LR * ((m2 / c1) / (jnp.sqrt(v2 / c2) + ADAM_EPS) + ADAM_WD * w_ref[...])

    def land_spec(l):
        return pl.BlockSpec((N_DEV, row_tile, ncol), lambda k, i: (0, jnp.where(k == l, i, 0), 0))

    tile = pl.BlockSpec((None, row_tile, ncol), lambda k, i: (k, i, 0))
    return pl.pallas_call(
        body, name=name, grid=(nl, nr // row_tile),
        in_specs=[land_spec(l) for l in range(nl)] + [tile, tile, tile],
        out_specs=[tile] * 4,
        out_shape=[jax.ShapeDtypeStruct(w.shape, F32)] * 4,
        compiler_params=_params(("arbitrary", "arbitrary")),
    )(*lands, w, m, v)


class _Item:
    def __init__(self, src, chunked, land_cols=False):
        self.src, self.chunked, self.land_cols = src, chunked, land_cols
        if chunked == "cols":
            block = (src.shape[0], src.shape[1] // N_DEV)
        else:
            block = src.shape[1:] if chunked else src.shape
        self.width = block[-1]
        self.land_shape = (block[0], N_DEV * block[1]) if land_cols else (N_DEV,) + block

    def _cols(self, first, count=1):
        return pl.ds(pl.multiple_of(first * self.width, LANES), count * self.width)

    def part(self, src_ref, j):
        if self.chunked == "cols":
            return src_ref.at[:, self._cols(j)]
        return src_ref.at[j] if self.chunked else src_ref

    def slot(self, land_ref, s):
        return land_ref.at[:, self._cols(s)] if self.land_cols else land_ref.at[s]

    def seven(self, land_ref):
        return land_ref.at[:, self._cols(0, N_DEV - 1)] if self.land_cols else land_ref.at[pl.ds(0, N_DEV - 1)]


def _mesh_place():
    x, y, c = lax.axis_index("x"), lax.axis_index("y"), lax.axis_index("c")
    return x, y, c, 4 * x + 2 * y + c


def _flipped(x, y, c, k):
    px = 1 - x if k & 4 else x
    py = 1 - y if k & 2 else y
    pc = 1 - c if k & 1 else c
    return (px, py, pc), 4 * px + 2 * py + pc


PEER_ORDER = (2, 4, 6, 3, 5, 7, 1)


def _exchange(items, name):
    n = len(items)

    def body(*refs):
        srcs, lands = refs[:n], refs[n:2 * n]
        send, recv, local = refs[2 * n:]
        x, y, c, me = _mesh_place()

        def copy(i, k, chunk, slot, dev):
            return pltpu.make_async_remote_copy(
                src_ref=items[i].part(srcs[i], chunk), dst_ref=items[i].slot(lands[i], slot),
                send_sem=send.at[i, k - 1], recv_sem=recv.at[i, k - 1], device_id=dev, device_id_type=MESH)

        own = [pltpu.make_async_copy(items[i].part(srcs[i], me), items[i].slot(lands[i], me), local.at[i])
               for i in range(n)]
        for k in PEER_ORDER:
            dev, idx = _flipped(x, y, c, k)
            for i in range(n):
                copy(i, k, idx, me, dev).start()
        for cp in own:
            cp.start()
        for k in PEER_ORDER:
            dev, idx = _flipped(x, y, c, k)
            for i in range(n):
                copy(i, k, me, idx, dev).wait_recv()
        for k in PEER_ORDER:
            dev, idx = _flipped(x, y, c, k)
            for i in range(n):
                copy(i, k, idx, me, dev).wait_send()
        for cp in own:
            cp.wait()

    hbm = pl.BlockSpec(memory_space=pl.ANY)
    return pl.pallas_call(
        body, name=name,
        in_specs=[hbm] * n, out_specs=[hbm] * n,
        out_shape=[jax.ShapeDtypeStruct(it.land_shape, it.src.dtype) for it in items],
        scratch_shapes=[pltpu.SemaphoreType.DMA((n, N_DEV - 1)), pltpu.SemaphoreType.DMA((n, N_DEV - 1)),
                        pltpu.SemaphoreType.DMA((n,))],
        compiler_params=pltpu.CompilerParams(has_side_effects=True),
    )(*[it.src for it in items])


HBM_SPEC = pl.BlockSpec(memory_space=pltpu.HBM)
SEM_SPEC = pl.BlockSpec(memory_space=pltpu.SEMAPHORE)
DATAFLOW = pltpu.SideEffectType.DATAFLOW_SIDE_EFFECTING


def _exchange_start(items, name, after=()):
    n = len(items)
    na = len(after)

    def body(*refs):
        srcs, land_ins = refs[:n], refs[n:2 * n]
        outs = refs[2 * n + na:6 * n + na]
        (local,) = refs[6 * n + na:]
        del land_ins
        x, y, c, me = _mesh_place()
        own = [pltpu.make_async_copy(items[i].part(srcs[i], me), items[i].slot(outs[4 * i + 3], me), local.at[i])
               for i in range(n)]
        for cp in own:
            cp.start()
        for cp in own:
            cp.wait()
        for k in PEER_ORDER:
            dev, idx = _flipped(x, y, c, k)
            for i in range(n):
                send, recv, _, land = outs[4 * i:4 * i + 4]
                pltpu.make_async_remote_copy(
                    src_ref=items[i].part(srcs[i], idx), dst_ref=items[i].slot(land, me), send_sem=send, recv_sem=recv,
                    device_id=dev, device_id_type=MESH).start()

    out_shape, out_specs, args, lands = [], [], [], []
    for it in items:
        out_shape += [pltpu.SemaphoreType.DMA(()), pltpu.SemaphoreType.DMA(()),
                      pltpu.HBM(it.src.shape, it.src.dtype), pltpu.HBM(it.land_shape, it.src.dtype)]
        out_specs += [SEM_SPEC, SEM_SPEC, HBM_SPEC, HBM_SPEC]
        args.append(pltpu.with_memory_space_constraint(it.src, pltpu.HBM))
        lands.append(pltpu.with_memory_space_constraint(lax.empty(it.land_shape, it.src.dtype), pltpu.HBM))
    outs = pl.pallas_call(
        body, name=name,
        in_specs=[HBM_SPEC] * (2 * n) + [pl.BlockSpec(memory_space=pl.ANY)] * na,
        out_specs=out_specs, out_shape=out_shape,
        scratch_shapes=[pltpu.SemaphoreType.DMA((n,))],
        input_output_aliases={**{i: 4 * i + 2 for i in range(n)}, **{n + i: 4 * i + 3 for i in range(n)}},
        compiler_params=pltpu.CompilerParams(has_side_effects=DATAFLOW),
    )(*args, *lands, *after)
    return [tuple(outs[4 * i:4 * i + 4]) + (items[i],) for i in range(n)]


def _started(handles):
    return handles[0][2]


def _exchange_wait(handles, after, name):
    n = len(handles)

    def body(*refs):
        x, y, c, _ = _mesh_place()
        for i in range(n):
            src, land, send, recv = refs[4 * i:4 * i + 4]
            del src
            seven = handles[i][4].seven(land)
            cp = pltpu.make_async_remote_copy(src_ref=seven, dst_ref=seven, send_sem=send, recv_sem=recv,
                                              device_id=(x, y, 1 - c), device_id_type=MESH)
            cp.wait_send()
            cp.wait_recv()

    args, in_specs, out_shape = [], [], []
    for send, recv, src, land, _ in handles:
        args += [src, land, send, recv]
        in_specs += [HBM_SPEC, HBM_SPEC, SEM_SPEC, SEM_SPEC]
        out_shape += [pltpu.HBM(src.shape, src.dtype), pltpu.HBM(land.shape, land.dtype)]
    outs = pl.pallas_call(
        body, name=name,
        in_specs=in_specs + [pl.BlockSpec(memory_space=pl.ANY)] * len(after), out_specs=[HBM_SPEC] * (2 * n),
        out_shape=out_shape,
        input_output_aliases={**{4 * i: 2 * i for i in range(n)}, **{4 * i + 1: 2 * i + 1 for i in range(n)}},
        compiler_params=pltpu.CompilerParams(has_side_effects=DATAFLOW),
    )(*args, *after)
    return [outs[2 * i + 1] for i in range(n)]


TM = 1024
TM_ACC = 512
TN_IN = 768


def kernel(x, norm1_g, w_in, mix_conv_w, attn_out_g, conv_out_g, w_out, norm2_g, ffn_up, ffn_conv_w, ffn_down, final_norm_g, loss_target, m_norm1_g, m_w_in, m_mix_conv_w, m_attn_out_g, m_conv_out_g, m_w_out, m_norm2_g, m_ffn_up, m_ffn_conv_w, m_ffn_down, m_final_norm_g, v_norm1_g, v_w_in, v_mix_conv_w, v_attn_out_g, v_conv_out_g, v_w_out, v_norm2_g, v_ffn_up, v_ffn_conv_w, v_ffn_down, v_final_norm_g):
    nbatch, seq, d = x.shape
    t = nbatch * seq
    nt, nta = t // TM, t // TM_ACC
    out_rows = D_MODEL // N_DEV
    down_rows = D_FF // N_DEV
    xf = x.reshape(t, d)
    target = loss_target.reshape(t, d)

    cw_local = jnp.concatenate([ffn_conv_w, mix_conv_w], axis=-1)
    cast = lambda w: _Item(w.astype(BF16), False)
    cast_in = lambda w: _Item(w.astype(BF16), False, land_cols=True)
    batch = [_exchange_start([_Item(cw_local, False), cast_in(w_in[0])], "gather_start_a"), None, None, None]
    win, wout, wup, wdown = [None] * DEPTH, [None] * DEPTH, [None] * DEPTH, [None] * DEPTH

    full = lambda i, j, k: (0, 0)

    saved = []
    xin = xf
    for l in range(DEPTH):
        h1 = _rms_fwd(xin, norm1_g[l][None], f"rms1_fwd_{l}")
        pin = []
        if l == 0:
            cw_all, win[0] = _exchange_wait(batch[0], [h1], "gather_wait_in_0")
            batch[1] = _exchange_start([cast(w_out[0]), cast(ffn_up[0])], "gather_start_b", after=[win[0]])
            pin = [_started(batch[1])]
            fcw = [cw_all[:, k, :, :UP_CHUNK].reshape(2, N_UP_PAIRS, 3, UP_CHUNK) for k in range(DEPTH)]
            mcw = [cw_all[:, k, :, UP_CHUNK:].transpose(1, 0, 2).reshape(3, D_CONV) for k in range(DEPTH)]
        else:
            (win[1],) = _exchange_wait([batch[2][1]], [h1], "gather_wait_in_1")
        proj = _matmul(
            h1, win[l], grid=(nt, D_IN // TN_IN, 1), dims=NN, name=f"proj_{l}", after=pin,
            a_spec=pl.BlockSpec((TM, D_MODEL), lambda i, j, k: (i, 0)),
            b_spec=pl.BlockSpec((D_MODEL, TN_IN), lambda i, j, k: (0, j)),
            o_spec=pl.BlockSpec((TM, TN_IN), lambda i, j, k: (i, j)), o_shape=(t, D_IN), o_dtype=F32)
        o, lse, cat = _attn_fwd(proj, attn_out_g[l][None], nbatch, seq)
        pin = []
        if l == 0:
            batch[2] = _exchange_start([cast(ffn_down[0]), cast_in(w_in[1]), cast(w_out[1])], "gather_start_c", after=[o])
            pin = [_started(batch[2])]
        cat = _convmix_fwd(proj, cat, mcw[l], conv_out_g[l][None], nbatch, seq)
        (got,) = _exchange_wait([batch[1][0] if l == 0 else batch[2][2]], [cat], f"gather_wait_out_{l}")
        wout[l] = got.reshape(D_MODEL, D_MODEL)
        xmid = _matmul(
            cat, wout[l], grid=(nta, 1, 1), dims=NN, name=f"mix_out_{l}", after=pin,
            a_spec=pl.BlockSpec((TM_ACC, D_MODEL), lambda i, j, k: (i, 0)),
            b_spec=pl.BlockSpec((D_MODEL, D_MODEL), full),
            o_spec=pl.BlockSpec((TM_ACC, D_MODEL), lambda i, j, k: (i, 0)), o_shape=(t, D_MODEL), o_dtype=F32,
            res=xin, res_spec=pl.BlockSpec((TM_ACC, D_MODEL), lambda i, j, k: (i, 0)))
        h2 = _rms_fwd(xmid, norm2_g[l][None], f"rms2_fwd_{l}")
        if l == 0:
            (wup[0],) = _exchange_wait([batch[1][1]], [h2], "gather_wait_up_0")
        else:
            wup[1], got = _exchange_wait(batch[3], [h2], "gather_wait_ffn_1")
            wdown[1] = got.reshape(N_UP_PAIRS, UP_CHUNK, D_MODEL)
        pre = _matmul(
            h2, wup[l], grid=(nt, N_DEV, 1), dims=NN, name=f"ffn_up_{l}",
            a_spec=pl.BlockSpec((TM, D_MODEL), lambda i, j, k: (i, 0)),
            b_spec=pl.BlockSpec((None, D_MODEL, UP_CHUNK), lambda i, j, k: (j, 0, 0)),
            o_spec=pl.BlockSpec((None, TM, UP_CHUNK), lambda i, j, k: (j, i, 0)),
            o_shape=(N_DEV, t, UP_CHUNK), o_dtype=BF16).reshape(2, N_UP_PAIRS, t, UP_CHUNK)
        act = _ffn_act_fwd(pre, fcw[l], nbatch, seq)
        pin = []
        if l == 0:
            (got,) = _exchange_wait([batch[2][0]], [act], "gather_wait_down_0")
            wdown[0] = got.reshape(N_UP_PAIRS, UP_CHUNK, D_MODEL)
            batch[3] = _exchange_start([cast(ffn_up[1]), cast(ffn_down[1])], "gather_start_d", after=[act])
            pin = [_started(batch[3])]
        xout = _matmul(
            act, wdown[l], grid=(nta, 1, 1), dims=NN, name=f"ffn_down_{l}", after=pin,
            a_spec=pl.BlockSpec((N_UP_PAIRS, TM_ACC, UP_CHUNK), lambda i, j, k: (0, i, 0)),
            b_spec=pl.BlockSpec((N_UP_PAIRS, UP_CHUNK, D_MODEL), lambda i, j, k: (0, 0, 0)),
            o_spec=pl.BlockSpec((TM_ACC, D_MODEL), lambda i, j, k: (i, 0)), o_shape=(t, D_MODEL), o_dtype=F32,
            res=xmid, res_spec=pl.BlockSpec((TM_ACC, D_MODEL), lambda i, j, k: (i, 0)))
        saved.append((xin, h1, proj, o, lse, cat, xmid, h2, pre, act))
        xin = xout

    loss_part, dx, dxb, dgf = _loss_head(xin, final_norm_g[None], target, "loss_head")

    dg1, dg2, dga, dgc = [None] * DEPTH, [None] * DEPTH, [None] * DEPTH, [None] * DEPTH
    for l in reversed(range(DEPTH)):
        xin, h1, proj, o, lse, cat, xmid, h2, pre, act = saved[l]
        d_act = _matmul(
            dxb, wdown[l], grid=(nt, N_UP_PAIRS, 1), dims=NT, name=f"d_act_{l}",
            a_spec=pl.BlockSpec((TM, D_MODEL), lambda i, j, k: (i, 0)),
            b_spec=pl.BlockSpec((None, UP_CHUNK, D_MODEL), lambda i, j, k: (j, 0, 0)),
            o_spec=pl.BlockSpec((None, TM, UP_CHUNK), lambda i, j, k: (j, i, 0)),
            o_shape=(N_UP_PAIRS, t, UP_CHUNK), o_dtype=BF16)
        g_down = _matmul(
            act, dxb, grid=(N_UP_PAIRS, 1, 1), dims=TN, name=f"g_down_{l}",
            a_spec=pl.BlockSpec((None, t, UP_CHUNK), lambda i, j, k: (i, 0, 0)),
            b_spec=pl.BlockSpec((t, D_MODEL), full),
            o_spec=pl.BlockSpec((None, UP_CHUNK, D_MODEL), lambda i, j, k: (i, 0, 0)),
            o_shape=(N_UP_PAIRS, UP_CHUNK, D_MODEL), o_dtype=BF16).reshape(N_DEV, down_rows, D_MODEL)
        d_pre, d_fcw = _ffn_act_bwd(pre, d_act, fcw[l], nbatch, seq)
        d_pre = d_pre.reshape(N_DEV, t, UP_CHUNK)
        dh2 = _matmul(
            d_pre, wup[l], grid=(nta, 1, 1), dims=NT, name=f"d_h2_{l}",
            a_spec=pl.BlockSpec((N_DEV, TM_ACC, UP_CHUNK), lambda i, j, k: (0, i, 0)),
            b_spec=pl.BlockSpec((N_DEV, D_MODEL, UP_CHUNK), lambda i, j, k: (0, 0, 0)),
            o_spec=pl.BlockSpec((TM_ACC, D_MODEL), lambda i, j, k: (i, 0)), o_shape=(t, D_MODEL), o_dtype=F32)
        g_up = _matmul(
            h2, d_pre, grid=(1, N_DEV, 1), dims=TN, name=f"g_up_{l}",
            a_spec=pl.BlockSpec((t, D_MODEL), full),
            b_spec=pl.BlockSpec((None, t, UP_CHUNK), lambda i, j, k: (j, 0, 0)),
            o_spec=pl.BlockSpec((None, D_MODEL, UP_CHUNK), lambda i, j, k: (j, 0, 0)),
            o_shape=(N_DEV, D_MODEL, UP_CHUNK), o_dtype=BF16)
        dxm, dxmb, dg2[l] = _rms_bwd(xmid, norm2_g[l][None], dh2, dx, f"rms2_bwd_{l}")
        g_out = _matmul(
            cat, dxmb, grid=(1, 1, nt), dims=TN, name=f"g_out_{l}",
            a_spec=pl.BlockSpec((TM, D_MODEL), lambda i, j, k: (k, 0)),
            b_spec=pl.BlockSpec((TM, D_MODEL), lambda i, j, k: (k, 0)),
            o_spec=pl.BlockSpec((D_MODEL, D_MODEL), full),
            o_shape=(D_MODEL, D_MODEL), o_dtype=BF16).reshape(N_DEV, out_rows, D_MODEL)
        pin = []
        if l == 0:
            early = _exchange_start([_Item(g_out, True), _Item(g_up, True), _Item(g_down, True)], "scatter_start_0a")
            pin = [_started(early)]
        d_cat = _matmul(
            dxmb, wout[l], grid=(nta, 1, 1), dims=NT, name=f"d_cat_{l}", after=pin,
            a_spec=pl.BlockSpec((TM_ACC, D_MODEL), lambda i, j, k: (i, 0)),
            b_spec=pl.BlockSpec((D_MODEL, D_MODEL), full),
            o_spec=pl.BlockSpec((TM_ACC, D_MODEL), lambda i, j, k: (i, 0)), o_shape=(t, D_MODEL), o_dtype=BF16)
        d_proj, dga[l] = _attn_bwd(proj, o, lse, d_cat, attn_out_g[l][None], nbatch, seq)
        d_proj, d_mcw, dgc[l] = _convmix_bwd(proj, d_cat, d_proj, mcw[l], conv_out_g[l][None], nbatch, seq)
        g_in = _matmul(
            h1, d_proj, grid=(1, D_IN // TN_IN, 1), dims=TN, name=f"g_in_{l}",
            a_spec=pl.BlockSpec((t, D_MODEL), full),
            b_spec=pl.BlockSpec((t, TN_IN), lambda i, j, k: (0, j)),
            o_spec=pl.BlockSpec((D_MODEL, TN_IN), lambda i, j, k: (0, j)),
            o_shape=(D_MODEL, D_IN), o_dtype=BF16)
        g_cw = jnp.concatenate(
            [d_fcw.reshape(N_DEV, 3, UP_CHUNK), d_mcw.reshape(3, N_DEV, D_CONV // N_DEV).transpose(1, 0, 2)], axis=-1)
        if l == 0:
            late = _exchange_start([_Item(g_in, "cols"), _Item(g_cw, True)], "scatter_start_0b")
            pin = [_started(late)]
        else:
            upper = _exchange_start(
                [_Item(g_in, "cols"), _Item(g_out, True), _Item(g_up, True), _Item(g_down, True), _Item(g_cw, True)],
                "scatter_start_1")
            pin = [_started(upper)]
        dh1 = _matmul(
            d_proj, win[l], grid=(nta, 1, 1), dims=NT, name=f"d_h1_{l}", after=pin,
            a_spec=pl.BlockSpec((TM_ACC, D_IN), lambda i, j, k: (i, 0)),
            b_spec=pl.BlockSpec((D_MODEL, D_IN), full),
            o_spec=pl.BlockSpec((TM_ACC, D_MODEL), lambda i, j, k: (i, 0)), o_shape=(t, D_MODEL), o_dtype=F32)
        dx, dxb, dg1[l] = _rms_bwd(xin, norm1_g[l][None], dh1, dxm, f"rms1_bwd_{l}")

    def pack_small(n1, a, c, n2, f):
        return jnp.concatenate(
            [n1, n2, f[None], jnp.concatenate([a, c], axis=-1), jnp.zeros((1, D_MODEL), F32)], axis=0)[None]

    land_in1, land_out1, land_up1, land_down1, land_cw1 = _exchange_wait(upper, [dxb], "scatter_wait_1")
    land_out0, land_up0, land_down0 = _exchange_wait(early, [dxb], "scatter_wait_0a")
    res_out = _adamw([land_out0, land_out1], w_out, m_w_out, v_w_out, out_rows, "adamw_w_out")
    res_up = _adamw([land_up0, land_up1], ffn_up, m_ffn_up, v_ffn_up, 256, "adamw_ffn_up")
    res_down = _adamw([land_down0, land_down1], ffn_down, m_ffn_down, v_ffn_down, down_rows, "adamw_ffn_down")
    small = jnp.concatenate(
        [dg1[0], dg1[1], dg2[0], dg2[1], dgf,
         jnp.concatenate([dga[0], dgc[0]], axis=-1), jnp.concatenate([dga[1], dgc[1]], axis=-1),
         jnp.zeros((1, D_MODEL), F32)], axis=0)
    (land_small,) = _exchange([_Item(small, False)], "gather_gain_grads")
    res_small = _adamw(
        [land_small], pack_small(norm1_g, attn_out_g, conv_out_g, norm2_g, final_norm_g),
        pack_small(m_norm1_g, m_attn_out_g, m_conv_out_g, m_norm2_g, m_final_norm_g),
        pack_small(v_norm1_g, v_attn_out_g, v_conv_out_g, v_norm2_g, v_final_norm_g), SUBLANES, "adamw_gains")
    land_in0, land_cw0 = _exchange_wait(late, [res_small[0], res_down[0], res_up[0], res_out[0]], "scatter_wait_0b")
    res_in = _adamw([land_in0, land_in1], w_in, m_w_in, v_w_in, 256, "adamw_w_in")
    res_cw = _adamw(
        [land_cw0, land_cw1], cw_local, jnp.concatenate([m_ffn_conv_w, m_mix_conv_w], axis=-1),
        jnp.concatenate([v_ffn_conv_w, v_mix_conv_w], axis=-1), 3, "adamw_conv_w")

    loss = lax.psum(loss_part[0, 0], ("x", "y", "c"))

    def unpack(kind):
        s = res_small[kind][0]
        cwr = res_cw[kind]
        return (s[0:2], res_in[kind], cwr[..., UP_CHUNK:], s[5:7, :D_ATTN], s[5:7, D_ATTN:], res_out[kind],
                s[2:4], res_up[kind], cwr[..., :UP_CHUNK], res_down[kind], s[4])

    return (loss, dx.reshape(nbatch, seq, d), *unpack(0), *unpack(1), *unpack(2), *unpack(3))
```

```python
import math

import jax
import jax.numpy as jnp
from jax import lax
from jax.experimental import pallas as pl
from jax.experimental.pallas import tpu as pltpu
from jax.experimental.pallas import tpu_sc as plsc

F32 = jnp.float32
BF16 = jnp.bfloat16

D_MODEL = 1024
D_ATTN = 512
D_CONV = 512
HEAD_DIM = 64
N_HEADS = 8
D_FF = 2816
DEPTH = 2
D_IN = 3 * D_ATTN + 3 * D_CONV
EPS = 1e-6
DILATIONS = (1, 4, 16)
BAND = 128
N_DEV = 8
IN_CHUNK = D_IN // N_DEV
UP_CHUNK = 2 * D_FF // N_DEV
N_UP_PAIRS = N_DEV // 2
CW_PACK = UP_CHUNK + D_CONV // N_DEV
ADAM_LR = 0.001
ADAM_B1 = 0.9
ADAM_B2 = 0.999
ADAM_EPS = 1e-08
ADAM_WD = 0.01
ADAM_STEP = 10
LANES = 128
SUBLANES = 8
VMEM_LIMIT = 56 * 1024 * 1024

NEG = -1e30
MESH = pl.DeviceIdType.MESH


def _params(sem=None, vmem=VMEM_LIMIT):
    return pltpu.CompilerParams(dimension_semantics=sem, vmem_limit_bytes=vmem)


NN = (((1,), (0,)), ((), ()))
NT = (((1,), (1,)), ((), ()))
TN = (((0,), (0,)), ((), ()))


def _matmul(a, b, *, grid, a_spec, b_spec, o_spec, o_shape, o_dtype, dims, name, res=None, res_spec=None, after=()):
    nk = grid[2]
    o_block = tuple(s for s in o_spec.block_shape if s is not None)
    na = len(after)

    def body(*refs):
        refs = refs[:2 + (res is not None)] + refs[2 + (res is not None) + na:]
        if res is None:
            a_ref, b_ref, o_ref, *scr = refs
            r_ref = None
        else:
            a_ref, b_ref, r_ref, o_ref, *scr = refs
        def dot(av, bv):
            return lax.dot_general(av.astype(BF16), bv.astype(BF16), dims, preferred_element_type=F32)

        if len(a_ref.shape) == 3:
            part = dot(a_ref[0], b_ref[0])
            for c in range(1, a_ref.shape[0]):
                part = part + dot(a_ref[c], b_ref[c])
        else:
            part = dot(a_ref[...], b_ref[...])

        def finish(total):
            if r_ref is not None:
                total = total + r_ref[...]
            o_ref[...] = total.astype(o_dtype)

        if nk == 1:
            finish(part)
        else:
            acc = scr[0]
            k = pl.program_id(2)

            @pl.when(k == 0)
            def _():
                acc[...] = part

            @pl.when(k > 0)
            def _():
                acc[...] += part

            @pl.when(k == nk - 1)
            def _():
                finish(acc[...])

    in_specs = [a_spec, b_spec] + ([res_spec] if res is not None else []) + [pl.BlockSpec(memory_space=pl.ANY)] * na
    args = (a, b) + ((res,) if res is not None else ()) + tuple(after)
    return pl.pallas_call(
        body, name=name, grid=grid, in_specs=in_specs, out_specs=o_spec,
        out_shape=jax.ShapeDtypeStruct(o_shape, o_dtype),
        scratch_shapes=[pltpu.VMEM(o_block, F32)] if nk > 1 else [],
        compiler_params=_params(("parallel", "parallel", "arbitrary")),
    )(*args)


ROW_TILE = 512


def _rms_fwd(x, g, name):
    t, d = x.shape

    def body(x_ref, g_ref, h_ref):
        xv = x_ref[...]
        r = lax.rsqrt(jnp.mean(xv * xv, axis=-1, keepdims=True) + EPS)
        h_ref[...] = (xv * r * g_ref[...]).astype(BF16)

    return pl.pallas_call(
        body, name=name, grid=(t // ROW_TILE,),
        in_specs=[pl.BlockSpec((ROW_TILE, d), lambda i: (i, 0)), pl.BlockSpec((1, d), lambda i: (0, 0))],
        out_specs=pl.BlockSpec((ROW_TILE, d), lambda i: (i, 0)),
        out_shape=jax.ShapeDtypeStruct((t, d), BF16),
        compiler_params=_params(("parallel",)),
    )(x, g)


def _rms_bwd(x, g, dh, dres, name):
    t, d = x.shape

    def body(x_ref, g_ref, dh_ref, dres_ref, dx_ref, dxb_ref, dg_ref):
        xv = x_ref[...]
        r = lax.rsqrt(jnp.mean(xv * xv, axis=-1, keepdims=True) + EPS)
        xh = xv * r
        dhv = dh_ref[...]
        gd = dhv * g_ref[...]
        dx = r * (gd - xh * jnp.mean(gd * xh, axis=-1, keepdims=True)) + dres_ref[...]
        dx_ref[...] = dx
        dxb_ref[...] = dx.astype(BF16)
        part = jnp.sum(dhv * xh, axis=0, keepdims=True)

        @pl.when(pl.program_id(0) == 0)
        def _():
            dg_ref[...] = part

        @pl.when(pl.program_id(0) > 0)
        def _():
            dg_ref[...] += part

    row = pl.BlockSpec((ROW_TILE, d), lambda i: (i, 0))
    vec = pl.BlockSpec((1, d), lambda i: (0, 0))
    return pl.pallas_call(
        body, name=name, grid=(t // ROW_TILE,),
        in_specs=[row, vec, row, row], out_specs=[row, row, vec],
        out_shape=[jax.ShapeDtypeStruct((t, d), F32), jax.ShapeDtypeStruct((t, d), BF16),
                   jax.ShapeDtypeStruct((1, d), F32)],
        compiler_params=_params(("arbitrary",)),
    )(x, g, dh, dres)


def _loss_head(x, g, target, name):
    t, d = x.shape

    def body(x_ref, g_ref, t_ref, loss_ref, dx_ref, dxb_ref, dg_ref):
        xv = x_ref[...]
        r = lax.rsqrt(jnp.mean(xv * xv, axis=-1, keepdims=True) + EPS)
        xh = xv * r
        gv = g_ref[...]
        err = xh * gv - t_ref[...]
        loss = jnp.full((1, LANES), 0.5 / d, F32) * jnp.sum(err * err)
        dy = err * (1.0 / d)
        gd = dy * gv
        dx = r * (gd - xh * jnp.mean(gd * xh, axis=-1, keepdims=True))
        dx_ref[...] = dx
        dxb_ref[...] = dx.astype(BF16)
        part = jnp.sum(dy * xh, axis=0, keepdims=True)

        @pl.when(pl.program_id(0) == 0)
        def _():
            dg_ref[...] = part
            loss_ref[...] = loss

        @pl.when(pl.program_id(0) > 0)
        def _():
            dg_ref[...] += part
            loss_ref[...] += loss

    row = pl.BlockSpec((ROW_TILE, d), lambda i: (i, 0))
    vec = pl.BlockSpec((1, d), lambda i: (0, 0))
    return pl.pallas_call(
        body, name=name, grid=(t // ROW_TILE,),
        in_specs=[row, vec, row],
        out_specs=[pl.BlockSpec((1, LANES), lambda i: (0, 0)), row, row, vec],
        out_shape=[jax.ShapeDtypeStruct((1, LANES), F32), jax.ShapeDtypeStruct((t, d), F32),
                   jax.ShapeDtypeStruct((t, d), BF16), jax.ShapeDtypeStruct((1, d), F32)],
        compiler_params=_params(("arbitrary",)),
    )(x, g, target)


def _group_matrix(n):
    shift = int(math.log2(HEAD_DIM))
    r = lax.broadcasted_iota(jnp.int32, (n, n), 0) >> shift
    c = lax.broadcasted_iota(jnp.int32, (n, n), 1) >> shift
    return (r == c).astype(BF16)


def _group_sum(v, gmat):
    hi = v.astype(BF16)
    rest = v - hi.astype(F32)
    mid = rest.astype(BF16)
    lo = (rest - mid.astype(F32)).astype(BF16)

    def dot(p):
        return jnp.dot(p, gmat, preferred_element_type=F32)

    return dot(hi) + dot(mid) + dot(lo)


def _shift_rows(ext, k):
    return pltpu.roll(ext, k % ext.shape[0], 0)


def _store_columns(stage, out_hbm, sems, row0, nrows, col_blocks):
    rows = pl.ds(pl.multiple_of(row0, SUBLANES * 2), nrows)
    copies = [
        pltpu.make_async_copy(stage.at[i], out_hbm.at[rows, pl.ds(pl.multiple_of(cb * LANES, LANES), LANES)], sems.at[i])
        for i, cb in enumerate(col_blocks)
    ]
    for cp in copies:
        cp.start()
    for cp in copies:
        cp.wait()


def _attn_consts(width):
    i = lax.broadcasted_iota(jnp.int32, (BAND, width), 0)
    j = lax.broadcasted_iota(jnp.int32, (BAND, width), 1)
    dist = (width - BAND) + i - j
    inwin = (dist >= 0) & (dist <= BAND)
    return dist.astype(F32), inwin, j


def _head_masks():
    lane = lax.broadcasted_iota(jnp.int32, (1, LANES), 1)
    return [(lane < HEAD_DIM).astype(F32), (lane >= HEAD_DIM).astype(F32)]


def _permute_in(src_ref, dst_ref, dil, seq):
    length = seq // dil
    for r in range(dil):
        dst_ref[pl.ds(r * length, length), :] = src_ref[pl.ds(r, length, stride=dil), :].astype(dst_ref.dtype)


def _slopes_table():
    slopes = 2.0 ** (-8.0 * jnp.arange(1, N_HEADS + 1, dtype=F32) / N_HEADS)
    return jnp.broadcast_to(slopes[:, None], (N_HEADS, 2 * BAND))


def _attn_fwd(proj, attn_g, nbatch, seq):
    t = nbatch * seq
    nblk = seq // BAND
    scale = HEAD_DIM ** -0.5

    def body(q_ref, k_ref, v_ref, g_ref, sl_ref, o_ref, lse_ref, cat_ref, pq, pk, pv, po, pm, pll, ao, am, al):
        hp = pl.program_id(1)
        hmask = _head_masks()
        slope = [sl_ref[pl.ds(2 * hp + hh, 1), :] for hh in range(2)]

        def run_branch(dil, qs, ks, vs, osink, msink, lsink):
            nb = seq // dil // BAND
            width = 2 * BAND if nb > 1 else BAND
            distf, inwin, jcol = _attn_consts(width)
            bias = [distf * (slope[hh][:, :width] * (-float(dil))) for hh in range(2)]

            def blk(m, carry):
                row0 = pl.multiple_of(m * BAND, BAND)
                q = qs[pl.ds(row0, BAND), :] * scale
                if nb > 1:
                    prow = pl.multiple_of(jnp.maximum(m - 1, 0) * BAND, BAND)
                    kk = jnp.concatenate([ks[pl.ds(prow, BAND), :], ks[pl.ds(row0, BAND), :]], axis=0)
                    vv = jnp.concatenate([vs[pl.ds(prow, BAND), :], vs[pl.ds(row0, BAND), :]], axis=0)
                    valid = inwin & (jcol >= jnp.where((m % nb) == 0, BAND, 0))
                else:
                    kk = ks[pl.ds(row0, BAND), :]
                    vv = vs[pl.ds(row0, BAND), :]
                    valid = inwin
                kb = kk.astype(BF16)
                o = jnp.zeros((BAND, LANES), F32)
                mfull = jnp.zeros((BAND, LANES), F32)
                lfull = jnp.zeros((BAND, LANES), F32)
                for hh in range(2):
                    qh = (q * hmask[hh]).astype(BF16)
                    s = lax.dot_general(qh, kb, NT, preferred_element_type=F32)
                    s = jnp.where(valid, s + bias[hh], NEG)
                    mh = jnp.max(s, axis=1, keepdims=True)
                    p = jnp.exp(s - mh)
                    lh = jnp.sum(p, axis=1, keepdims=True)
                    o = o + jnp.dot(p.astype(BF16), (vv * hmask[hh]).astype(BF16), preferred_element_type=F32)
                    mfull = mfull + mh * hmask[hh]
                    lfull = lfull + lh * hmask[hh]
                osink[pl.ds(row0, BAND), :] = o
                msink[pl.ds(row0, BAND), :] = mfull
                lsink[pl.ds(row0, BAND), :] = lfull
                return carry

            lax.fori_loop(0, nblk, blk, 0)

        run_branch(1, q_ref, k_ref, v_ref, ao, am, al)
        for dil in DILATIONS[1:]:
            length = seq // dil
            _permute_in(q_ref, pq, dil, seq)
            _permute_in(k_ref, pk, dil, seq)
            _permute_in(v_ref, pv, dil, seq)
            run_branch(dil, pq, pk, pv, po, pm, pll)
            for r in range(dil):
                nat = pl.ds(r, length, stride=dil)
                per = pl.ds(r * length, length)
                m0 = am[nat, :]
                mb = pm[per, :]
                mn = jnp.maximum(m0, mb)
                e0 = jnp.exp(m0 - mn)
                eb = jnp.exp(mb - mn)
                ao[nat, :] = ao[nat, :] * e0 + po[per, :] * eb
                al[nat, :] = al[nat, :] * e0 + pll[per, :] * eb
                am[nat, :] = mn

        gmat = _group_matrix(LANES)
        gv = g_ref[...]

        def fin(c, carry):
            rows = pl.ds(pl.multiple_of(c * 256, 256), 256)
            lv = al[rows, :]
            o = ao[rows, :] / lv
            o_ref[rows, :] = o
            lse_ref[rows, :] = am[rows, :] + jnp.log(lv)
            ms = _group_sum(o * o, gmat) * (1.0 / HEAD_DIM)
            cat_ref[rows, :] = (o * lax.rsqrt(ms + EPS) * gv).astype(BF16)
            return carry

        lax.fori_loop(0, seq // 256, fin, 0)

    nq = D_ATTN // LANES
    blk = lambda off: pl.BlockSpec((seq, LANES), lambda b, h: (b, h + off))
    scratch = [pltpu.VMEM((seq, LANES), F32) for _ in range(9)]
    return pl.pallas_call(
        body, name="attn_fwd", grid=(nbatch, nq),
        in_specs=[blk(0), blk(nq), blk(2 * nq), pl.BlockSpec((1, LANES), lambda b, h: (0, h)),
                  pl.BlockSpec((N_HEADS, 2 * BAND), lambda b, h: (0, 0))],
        out_specs=[blk(0), blk(0), blk(0)],
        out_shape=[jax.ShapeDtypeStruct((t, D_ATTN), F32), jax.ShapeDtypeStruct((t, D_ATTN), F32),
                   jax.ShapeDtypeStruct((t, D_MODEL), BF16)],
        scratch_shapes=scratch,
        compiler_params=_params(("parallel", "parallel")),
    )(proj, proj, proj, attn_g, _slopes_table())


def _attn_bwd(proj, o, lse, d_cat, attn_g, nbatch, seq):
    t = nbatch * seq
    nblk = seq // BAND
    scale = HEAD_DIM ** -0.5

    def body(q_ref, k_ref, v_ref, o_ref, lse_ref, dy_ref, g_ref, sl_ref, dproj_ref, dg_ref,
             do_n, dl_n, dq_n, dk_n, dv_n, pq, pk, pv, pdo, plse, pdl, pdq, pdk, pdv, stage, sems):
        hp = pl.program_id(0)
        hmask = _head_masks()
        slope = [sl_ref[pl.ds(2 * hp + hh, 1), :] for hh in range(2)]
        gmat = _group_matrix(LANES)
        gv = g_ref[...]

        def prep(c, dg):
            rows = pl.ds(pl.multiple_of(c * 256, 256), 256)
            ov = o_ref[rows, :]
            dyn = dy_ref[rows, :].astype(F32)
            r = lax.rsqrt(_group_sum(ov * ov, gmat) * (1.0 / HEAD_DIM) + EPS)
            gd = dyn * gv
            oh = ov * r
            do = r * (gd - oh * (_group_sum(gd * oh, gmat) * (1.0 / HEAD_DIM)))
            do_n[rows, :] = do
            dl_n[rows, :] = _group_sum(do * ov, gmat)
            return dg + jnp.sum(dyn * oh, axis=0, keepdims=True)

        dg = lax.fori_loop(0, seq // 256, prep, jnp.zeros((1, LANES), F32))

        @pl.when(pl.program_id(1) == 0)
        def _():
            dg_ref[...] = dg

        @pl.when(pl.program_id(1) > 0)
        def _():
            dg_ref[...] += dg

        def clear(*refs):
            def step(c, carry):
                rows = pl.ds(pl.multiple_of(c * 256, 256), 256)
                for ref in refs:
                    ref[rows, :] = jnp.zeros((256, LANES), F32)
                return carry

            lax.fori_loop(0, seq // 256, step, 0)

        clear(dq_n, dk_n, dv_n)

        def run_branch(dil, qs, ks, vs, dos, lses, dls, dqs, dks, dvs):
            nb = seq // dil // BAND
            width = 2 * BAND if nb > 1 else BAND
            distf, inwin, jcol = _attn_consts(width)
            bias = [distf * (slope[hh][:, :width] * (-float(dil))) for hh in range(2)]

            def blk(m, carry):
                row0 = pl.multiple_of(m * BAND, BAND)
                cur = pl.ds(row0, BAND)
                q = qs[cur, :] * scale
                dov = dos[cur, :]
                lsev = lses[cur, :]
                dlv = dls[cur, :]
                if nb > 1:
                    prow = pl.multiple_of(jnp.maximum(m - 1, 0) * BAND, BAND)
                    prev = pl.ds(prow, BAND)
                    kk = jnp.concatenate([ks[prev, :], ks[cur, :]], axis=0)
                    vv = jnp.concatenate([vs[prev, :], vs[cur, :]], axis=0)
                    valid = inwin & (jcol >= jnp.where((m % nb) == 0, BAND, 0))
                else:
                    kk = ks[cur, :]
                    vv = vs[cur, :]
                    valid = inwin
                kb = kk.astype(BF16)
                vb = vv.astype(BF16)
                dq = jnp.zeros((BAND, LANES), F32)
                dkk = jnp.zeros((width, LANES), F32)
                dvv = jnp.zeros((width, LANES), F32)
                for hh in range(2):
                    c0 = hh * HEAD_DIM
                    qh = (q * hmask[hh]).astype(BF16)
                    doh = (dov * hmask[hh]).astype(BF16)
                    s = lax.dot_general(qh, kb, NT, preferred_element_type=F32) + bias[hh]
                    p = jnp.where(valid, jnp.exp(s - lsev[:, c0:c0 + 1]), 0.0)
                    dp = lax.dot_general(doh, vb, NT, preferred_element_type=F32)
                    ds = (p * (dp - dlv[:, c0:c0 + 1])).astype(BF16)
                    dq = dq + jnp.dot(ds, kb, preferred_element_type=F32) * hmask[hh]
                    dkk = dkk + lax.dot_general(ds, qh, TN, preferred_element_type=F32)
                    dvv = dvv + lax.dot_general(p.astype(BF16), doh, TN, preferred_element_type=F32)
                dqs[cur, :] += dq
                if nb > 1:
                    dks[prev, :] += dkk[:BAND]
                    dvs[prev, :] += dvv[:BAND]
                    dks[cur, :] += dkk[BAND:]
                    dvs[cur, :] += dvv[BAND:]
                else:
                    dks[cur, :] += dkk
                    dvs[cur, :] += dvv
                return carry

            lax.fori_loop(0, nblk, blk, 0)

        run_branch(1, q_ref, k_ref, v_ref, do_n, lse_ref, dl_n, dq_n, dk_n, dv_n)
        for dil in DILATIONS[1:]:
            length = seq // dil
            for src, dst in ((q_ref, pq), (k_ref, pk), (v_ref, pv), (do_n, pdo), (lse_ref, plse), (dl_n, pdl)):
                _permute_in(src, dst, dil, seq)
            clear(pdq, pdk, pdv)
            run_branch(dil, pq, pk, pv, pdo, plse, pdl, pdq, pdk, pdv)
            for r in range(dil):
                nat = pl.ds(r, length, stride=dil)
                per = pl.ds(r * length, length)
                dq_n[nat, :] += pdq[per, :]
                dk_n[nat, :] += pdk[per, :]
                dv_n[nat, :] += pdv[per, :]

        def emit(c, carry):
            rows = pl.ds(pl.multiple_of(c * 256, 256), 256)
            stage[0, rows, :] = (dq_n[rows, :] * scale).astype(BF16)
            stage[1, rows, :] = dk_n[rows, :].astype(BF16)
            stage[2, rows, :] = dv_n[rows, :].astype(BF16)
            return carry

        lax.fori_loop(0, seq // 256, emit, 0)
        _store_columns(stage, dproj_ref, sems, pl.program_id(1) * seq, seq, [hp, nq + hp, 2 * nq + hp])

    nq = D_ATTN // LANES
    blk = lambda off: pl.BlockSpec((seq, LANES), lambda h, b: (b, h + off))
    vec = pl.BlockSpec((1, LANES), lambda h, b: (0, h))
    scratch = [pltpu.VMEM((seq, LANES), F32) for _ in range(14)]
    scratch += [pltpu.VMEM((3, seq, LANES), BF16), pltpu.SemaphoreType.DMA((3,))]
    d_proj, dg = pl.pallas_call(
        body, name="attn_bwd", grid=(nq, nbatch),
        in_specs=[blk(0), blk(nq), blk(2 * nq), blk(0), blk(0), blk(0), vec,
                  pl.BlockSpec((N_HEADS, 2 * BAND), lambda h, b: (0, 0))],
        out_specs=[pl.BlockSpec(memory_space=pl.ANY), vec],
        out_shape=[jax.ShapeDtypeStruct((t, D_IN), BF16), jax.ShapeDtypeStruct((1, D_ATTN), F32)],
        scratch_shapes=scratch,
        compiler_params=_params(("arbitrary", "arbitrary")),
    )(proj, proj, proj, o, lse, d_cat, attn_g, _slopes_table())
    return d_proj, dg


HALO = 2 * SUBLANES


def _window(ref, c, rows, nchunks, after):
    row0 = pl.multiple_of(c * rows, rows)
    prev0 = pl.multiple_of(jnp.maximum(row0 - HALO, 0), HALO)
    parts = [ref[pl.ds(prev0, HALO), :].astype(F32) * (c > 0).astype(F32), ref[pl.ds(row0, rows), :].astype(F32)]
    if after:
        next0 = pl.multiple_of(jnp.minimum(row0 + rows, (nchunks - 1) * rows), HALO)
        parts.append(ref[pl.ds(next0, HALO), :].astype(F32) * (c < nchunks - 1).astype(F32))
    return jnp.concatenate(parts, axis=0)


def _conv(z, w):
    return w[0:1] * _shift_rows(z, 2) + w[1:2] * _shift_rows(z, 1) + w[2:3] * z


def _conv_t(dy, w):
    return w[2:3] * dy + w[1:2] * _shift_rows(dy, -1) + w[0:1] * _shift_rows(dy, -2)


def _conv_wgrad(dy, z, cur):
    return [jnp.sum((dy * _shift_rows(z, 2 - k))[cur], axis=0, keepdims=True) for k in range(3)]


MIX_ROWS = 256
GATE_B_BLOCK = 3 * D_ATTN // LANES
GATE_C_BLOCK = GATE_B_BLOCK + D_CONV // LANES
U_BLOCK = GATE_C_BLOCK + D_CONV // LANES


def _convmix_fwd(proj, cat, mcw, conv_g, nbatch, seq):
    nchunks = seq // MIX_ROWS

    def body(gb_ref, gc_ref, u_ref, w_ref, g_ref, cat_in, cat_ref):
        del cat_in
        gmat = _group_matrix(LANES)
        w = w_ref[...]
        gv = g_ref[...]

        def step(c, carry):
            cur = pl.ds(pl.multiple_of(c * MIX_ROWS, MIX_ROWS), MIX_ROWS)
            z = _window(gc_ref, c, MIX_ROWS, nchunks, False) * _window(u_ref, c, MIX_ROWS, nchunks, False)
            y = gb_ref[cur, :] * _conv(z, w)[HALO:]
            ms = _group_sum(y * y, gmat) * (1.0 / HEAD_DIM)
            cat_ref[cur, :] = (y * lax.rsqrt(ms + EPS) * gv).astype(BF16)
            return carry

        lax.fori_loop(0, nchunks, step, 0)

    nc = D_CONV // LANES
    blk = lambda off: pl.BlockSpec((seq, LANES), lambda b, j: (b, j + off))
    return pl.pallas_call(
        body, name="convmix_fwd", grid=(nbatch, nc),
        in_specs=[blk(GATE_B_BLOCK), blk(GATE_C_BLOCK), blk(U_BLOCK),
                  pl.BlockSpec((3, LANES), lambda b, j: (0, j)), pl.BlockSpec((1, LANES), lambda b, j: (0, j)),
                  pl.BlockSpec(memory_space=pl.ANY)],
        out_specs=blk(D_ATTN // LANES),
        out_shape=jax.ShapeDtypeStruct(cat.shape, cat.dtype),
        input_output_aliases={5: 0},
        compiler_params=_params(("parallel", "parallel")),
    )(proj, proj, proj, mcw, conv_g, cat)


def _convmix_bwd(proj, d_cat, d_proj, mcw, conv_g, nbatch, seq):
    nchunks = seq // MIX_ROWS

    def body(gb_ref, gc_ref, u_ref, dy_ref, w_ref, g_ref, dproj_in, dproj_ref, dw_ref, dg_ref, stage, sems):
        del dproj_in
        cb = pl.program_id(0)
        b = pl.program_id(1)
        gmat = _group_matrix(LANES)
        w = w_ref[...]
        gv = g_ref[...]
        cur = slice(HALO, HALO + MIX_ROWS)

        def step(c, carry):
            rows = pl.ds(pl.multiple_of(c * MIX_ROWS, MIX_ROWS), MIX_ROWS)
            gb = _window(gb_ref, c, MIX_ROWS, nchunks, True)
            gc = _window(gc_ref, c, MIX_ROWS, nchunks, True)
            u = _window(u_ref, c, MIX_ROWS, nchunks, True)
            dyn = _window(dy_ref, c, MIX_ROWS, nchunks, True)
            z = gc * u
            conv = _conv(z, w)
            y = gb * conv
            r = lax.rsqrt(_group_sum(y * y, gmat) * (1.0 / HEAD_DIM) + EPS)
            yh = y * r
            gd = dyn * gv
            dy = r * (gd - yh * (_group_sum(gd * yh, gmat) * (1.0 / HEAD_DIM)))
            dc = dy * gb
            dz = _conv_t(dc, w)
            stage[0, rows, :] = (dy * conv)[cur].astype(BF16)
            stage[1, rows, :] = (dz * u)[cur].astype(BF16)
            stage[2, rows, :] = (dz * gc)[cur].astype(BF16)
            dws = _conv_wgrad(dc, z, cur)
            dg = jnp.sum((dyn * yh)[cur], axis=0, keepdims=True)
            return tuple(a + d for a, d in zip(carry, dws + [dg]))

        zero = jnp.zeros((1, LANES), F32)
        dw0, dw1, dw2, dg = lax.fori_loop(0, nchunks, step, (zero, zero, zero, zero))

        @pl.when(b == 0)
        def _():
            dw_ref[0:1, :] = dw0
            dw_ref[1:2, :] = dw1
            dw_ref[2:3, :] = dw2
            dg_ref[...] = dg

        @pl.when(b > 0)
        def _():
            dw_ref[0:1, :] += dw0
            dw_ref[1:2, :] += dw1
            dw_ref[2:3, :] += dw2
            dg_ref[...] += dg

        _store_columns(stage, dproj_ref, sems, b * seq, seq, [GATE_B_BLOCK + cb, GATE_C_BLOCK + cb, U_BLOCK + cb])

    nc = D_CONV // LANES
    blk = lambda off: pl.BlockSpec((seq, LANES), lambda j, b: (b, j + off))
    return pl.pallas_call(
        body, name="convmix_bwd", grid=(nc, nbatch),
        in_specs=[blk(GATE_B_BLOCK), blk(GATE_C_BLOCK), blk(U_BLOCK), blk(D_ATTN // LANES),
                  pl.BlockSpec((3, LANES), lambda j, b: (0, j)), pl.BlockSpec((1, LANES), lambda j, b: (0, j)),
                  pl.BlockSpec(memory_space=pl.ANY)],
        out_specs=[pl.BlockSpec(memory_space=pl.ANY), pl.BlockSpec((3, LANES), lambda j, b: (0, j)),
                   pl.BlockSpec((1, LANES), lambda j, b: (0, j))],
        out_shape=[jax.ShapeDtypeStruct(d_proj.shape, d_proj.dtype), jax.ShapeDtypeStruct((3, D_CONV), F32),
                   jax.ShapeDtypeStruct((1, D_CONV), F32)],
        scratch_shapes=[pltpu.VMEM((3, seq, LANES), BF16), pltpu.SemaphoreType.DMA((3,))],
        input_output_aliases={6: 0},
        compiler_params=_params(("arbitrary", "arbitrary")),
    )(proj, proj, proj, d_cat, mcw, conv_g, d_proj)


FFN_ROWS = 128


def _ffn_act_fwd(pre, fcw, nbatch, seq):
    t = nbatch * seq
    nchunks = seq // FFN_ROWS

    def body(pre_ref, w_ref, act_ref):
        wa = w_ref[0]
        wc = w_ref[1]

        def step(c, carry):
            cur = pl.ds(pl.multiple_of(c * FFN_ROWS, FFN_ROWS), FFN_ROWS)
            a = _conv(_window(pre_ref.at[0], c, FFN_ROWS, nchunks, False), wa)[HALO:]
            v = _conv(_window(pre_ref.at[1], c, FFN_ROWS, nchunks, False), wc)[HALO:]
            act_ref[cur, :] = (a * jax.nn.sigmoid(a) * v).astype(BF16)
            return carry

        lax.fori_loop(0, nchunks, step, 0)

    return pl.pallas_call(
        body, name="ffn_act_fwd", grid=(N_UP_PAIRS, nbatch),
        in_specs=[pl.BlockSpec((2, None, seq, UP_CHUNK), lambda i, b: (0, i, b, 0)),
                  pl.BlockSpec((2, None, 3, UP_CHUNK), lambda i, b: (0, i, 0, 0))],
        out_specs=pl.BlockSpec((None, seq, UP_CHUNK), lambda i, b: (i, b, 0)),
        out_shape=jax.ShapeDtypeStruct((N_UP_PAIRS, t, UP_CHUNK), BF16),
        compiler_params=_params(("parallel", "parallel")),
    )(pre, fcw)


def _ffn_act_bwd(pre, d_act, fcw, nbatch, seq):
    nchunks = seq // FFN_ROWS

    def body(pre_ref, da_ref, w_ref, dpre_ref, dw_ref):
        b = pl.program_id(1)
        wa = w_ref[0]
        wc = w_ref[1]
        cur = slice(HALO, HALO + FFN_ROWS)

        def step(c, carry):
            rows = pl.ds(pl.multiple_of(c * FFN_ROWS, FFN_ROWS), FFN_ROWS)
            pg = _window(pre_ref.at[0], c, FFN_ROWS, nchunks, True)
            pv = _window(pre_ref.at[1], c, FFN_ROWS, nchunks, True)
            dact = _window(da_ref, c, FFN_ROWS, nchunks, True)
            a = _conv(pg, wa)
            v = _conv(pv, wc)
            sg = jax.nn.sigmoid(a)
            da = dact * v * (sg * (1.0 + a * (1.0 - sg)))
            dv = dact * (a * sg)
            dpre_ref[0, rows, :] = _conv_t(da, wa)[cur].astype(BF16)
            dpre_ref[1, rows, :] = _conv_t(dv, wc)[cur].astype(BF16)
            return tuple(acc + d for acc, d in zip(carry, _conv_wgrad(da, pg, cur) + _conv_wgrad(dv, pv, cur)))

        zero = jnp.zeros((1, UP_CHUNK), F32)
        sums = lax.fori_loop(0, nchunks, step, (zero,) * 6)

        @pl.when(b == 0)
        def _():
            for i in range(6):
                dw_ref[i // 3, pl.ds(i % 3, 1), :] = sums[i]

        @pl.when(b > 0)
        def _():
            for i in range(6):
                dw_ref[i // 3, pl.ds(i % 3, 1), :] += sums[i]

    pair = pl.BlockSpec((2, None, seq, UP_CHUNK), lambda i, b: (0, i, b, 0))
    wspec = pl.BlockSpec((2, None, 3, UP_CHUNK), lambda i, b: (0, i, 0, 0))
    return pl.pallas_call(
        body, name="ffn_act_bwd", grid=(N_UP_PAIRS, nbatch),
        in_specs=[pair, pl.BlockSpec((None, seq, UP_CHUNK), lambda i, b: (i, b, 0)), wspec],
        out_specs=[pair, wspec],
        out_shape=[jax.ShapeDtypeStruct(pre.shape, BF16), jax.ShapeDtypeStruct(fcw.shape, F32)],
        compiler_params=_params(("parallel", "arbitrary")),
    )(pre, d_act, fcw)


def _adamw(lands, w, m, v, row_tile, name):
    nl = len(lands)
    _, nr, ncol = lands[0].shape
    c1 = 1.0 - ADAM_B1 ** ADAM_STEP
    c2 = 1.0 - ADAM_B2 ** ADAM_STEP

    def body(*refs):
        land_refs = refs[:nl]
        w_ref, m_ref, v_ref, g_ref, d_ref, mo_ref, vo_ref = refs[nl:]
        for l in range(nl):
            @pl.when(pl.program_id(0) == l)
            def _(l=l):
                g = land_refs[l][0].astype(F32)
                for j in range(1, N_DEV):
                    g = g + land_refs[l][j].astype(F32)
                g_ref[...] = g

        g = g_ref[...]
        m2 = ADAM_B1 * m_ref[...] + (1.0 - ADAM_B1) * g
        v2 = ADAM_B2 * v_ref[...] + (1.0 - ADAM_B2) * (g * g)
        mo_ref[...] = m2
        vo_ref[...] = v2
        d_ref[...] = -ADAM_LR * ((m2 / c1) / (jnp.sqrt(v2 / c2) + ADAM_EPS) + ADAM_WD * w_ref[...])

    def land_spec(l):
        return pl.BlockSpec((N_DEV, row_tile, ncol), lambda k, i: (0, jnp.where(k == l, i, 0), 0))

    tile = pl.BlockSpec((None, row_tile, ncol), lambda k, i: (k, i, 0))
    return pl.pallas_call(
        body, name=name, grid=(nl, nr // row_tile),
        in_specs=[land_spec(l) for l in range(nl)] + [tile, tile, tile],
        out_specs=[tile] * 4,
        out_shape=[jax.ShapeDtypeStruct(w.shape, F32)] * 4,
        compiler_params=_params(("arbitrary", "arbitrary")),
    )(*lands, w, m, v)


class _Item:
    def __init__(self, src, chunked, land_cols=False):
        self.src, self.chunked, self.land_cols = src, chunked, land_cols
        if chunked == "cols":
            block = (src.shape[0], src.shape[1] // N_DEV)
        else:
            block = src.shape[1:] if chunked else src.shape
        self.width = block[-1]
        self.land_shape = (block[0], N_DEV * block[1]) if land_cols else (N_DEV,) + block

    def _cols(self, first, count=1):
        return pl.ds(pl.multiple_of(first * self.width, LANES), count * self.width)

    def part(self, src_ref, j):
        if self.chunked == "cols":
            return src_ref.at[:, self._cols(j)]
        return src_ref.at[j] if self.chunked else src_ref

    def slot(self, land_ref, s):
        return land_ref.at[:, self._cols(s)] if self.land_cols else land_ref.at[s]

    def seven(self, land_ref):
        return land_ref.at[:, self._cols(0, N_DEV - 1)] if self.land_cols else land_ref.at[pl.ds(0, N_DEV - 1)]


def _mesh_place():
    x, y, c = lax.axis_index("x"), lax.axis_index("y"), lax.axis_index("c")
    return x, y, c, 4 * x + 2 * y + c


def _flipped(x, y, c, k):
    px = 1 - x if k & 4 else x
    py = 1 - y if k & 2 else y
    pc = 1 - c if k & 1 else c
    return (px, py, pc), 4 * px + 2 * py + pc


PEER_ORDER = (2, 4, 6, 3, 5, 7, 1)


def _exchange(items, name):
    n = len(items)

    def body(*refs):
        srcs, lands = refs[:n], refs[n:2 * n]
        send, recv, local = refs[2 * n:]
        x, y, c, me = _mesh_place()

        def copy(i, k, chunk, slot, dev):
            return pltpu.make_async_remote_copy(
                src_ref=items[i].part(srcs[i], chunk), dst_ref=items[i].slot(lands[i], slot),
                send_sem=send.at[i, k - 1], recv_sem=recv.at[i, k - 1], device_id=dev, device_id_type=MESH)

        own = [pltpu.make_async_copy(items[i].part(srcs[i], me), items[i].slot(lands[i], me), local.at[i])
               for i in range(n)]
        for k in PEER_ORDER:
            dev, idx = _flipped(x, y, c, k)
            for i in range(n):
                copy(i, k, idx, me, dev).start()
        for cp in own:
            cp.start()
        for k in PEER_ORDER:
            dev, idx = _flipped(x, y, c, k)
            for i in range(n):
                copy(i, k, me, idx, dev).wait_recv()
        for k in PEER_ORDER:
            dev, idx = _flipped(x, y, c, k)
            for i in range(n):
                copy(i, k, idx, me, dev).wait_send()
        for cp in own:
            cp.wait()

    hbm = pl.BlockSpec(memory_space=pl.ANY)
    return pl.pallas_call(
        body, name=name,
        in_specs=[hbm] * n, out_specs=[hbm] * n,
        out_shape=[jax.ShapeDtypeStruct(it.land_shape, it.src.dtype) for it in items],
        scratch_shapes=[pltpu.SemaphoreType.DMA((n, N_DEV - 1)), pltpu.SemaphoreType.DMA((n, N_DEV - 1)),
                        pltpu.SemaphoreType.DMA((n,))],
        compiler_params=pltpu.CompilerParams(has_side_effects=True),
    )(*[it.src for it in items])


def _sequencer_exchange(items, name, collective_id):
    n = len(items)

    def body(*refs):
        srcs, lands = refs[:n], refs[n:2 * n]
        send, recv, local = refs[2 * n:]
        x, y, c, me = _mesh_place()
        barrier = pltpu.get_barrier_semaphore()
        for k in PEER_ORDER:
            pl.semaphore_signal(barrier, inc=1, device_id=_flipped(x, y, c, k)[0], device_id_type=MESH)
        pl.semaphore_wait(barrier, N_DEV - 1)

        def copy(i, k, chunk, slot, dev):
            return pltpu.make_async_remote_copy(
                src_ref=items[i].part(srcs[i], chunk), dst_ref=items[i].slot(lands[i], slot),
                send_sem=send.at[i, k - 1], recv_sem=recv.at[i, k - 1], device_id=dev, device_id_type=MESH)

        own = [pltpu.make_async_copy(items[i].part(srcs[i], me), items[i].slot(lands[i], me), local.at[i])
               for i in range(n)]
        for cp in own:
            cp.start()
        for k in PEER_ORDER:
            dev, idx = _flipped(x, y, c, k)
            for i in range(n):
                copy(i, k, idx, me, dev).start()
        for k in PEER_ORDER:
            dev, idx = _flipped(x, y, c, k)
            for i in range(n):
                copy(i, k, me, idx, dev).wait_recv()
        for k in PEER_ORDER:
            dev, idx = _flipped(x, y, c, k)
            for i in range(n):
                copy(i, k, idx, me, dev).wait_send()
        for cp in own:
            cp.wait()

    return pl.kernel(
        body, name=name,
        out_type=[jax.ShapeDtypeStruct(it.land_shape, it.src.dtype) for it in items],
        mesh=plsc.ScalarSubcoreMesh(axis_name="sequencer", num_cores=1),
        scratch_types=[pltpu.SemaphoreType.DMA((n, N_DEV - 1)), pltpu.SemaphoreType.DMA((n, N_DEV - 1)),
                       pltpu.SemaphoreType.DMA((n,))],
        compiler_params=pltpu.CompilerParams(collective_id=collective_id),
    )(*[it.src for it in items])


HBM_SPEC = pl.BlockSpec(memory_space=pltpu.HBM)
SEM_SPEC = pl.BlockSpec(memory_space=pltpu.SEMAPHORE)
DATAFLOW = pltpu.SideEffectType.DATAFLOW_SIDE_EFFECTING


def _exchange_start(items, name, after=()):
    n = len(items)
    na = len(after)

    def body(*refs):
        srcs, land_ins = refs[:n], refs[n:2 * n]
        outs = refs[2 * n + na:6 * n + na]
        (local,) = refs[6 * n + na:]
        del land_ins
        x, y, c, me = _mesh_place()
        own = [pltpu.make_async_copy(items[i].part(srcs[i], me), items[i].slot(outs[4 * i + 3], me), local.at[i])
               for i in range(n)]
        for cp in own:
            cp.start()
        for cp in own:
            cp.wait()
        for k in PEER_ORDER:
            dev, idx = _flipped(x, y, c, k)
            for i in range(n):
                send, recv, _, land = outs[4 * i:4 * i + 4]
                pltpu.make_async_remote_copy(
                    src_ref=items[i].part(srcs[i], idx), dst_ref=items[i].slot(land, me), send_sem=send, recv_sem=recv,
                    device_id=dev, device_id_type=MESH).start()

    out_shape, out_specs, args, lands = [], [], [], []
    for it in items:
        out_shape += [pltpu.SemaphoreType.DMA(()), pltpu.SemaphoreType.DMA(()),
                      pltpu.HBM(it.src.shape, it.src.dtype), pltpu.HBM(it.land_shape, it.src.dtype)]
        out_specs += [SEM_SPEC, SEM_SPEC, HBM_SPEC, HBM_SPEC]
        args.append(pltpu.with_memory_space_constraint(it.src, pltpu.HBM))
        lands.append(pltpu.with_memory_space_constraint(lax.empty(it.land_shape, it.src.dtype), pltpu.HBM))
    outs = pl.pallas_call(
        body, name=name,
        in_specs=[HBM_SPEC] * (2 * n) + [pl.BlockSpec(memory_space=pl.ANY)] * na,
        out_specs=out_specs, out_shape=out_shape,
        scratch_shapes=[pltpu.SemaphoreType.DMA((n,))],
        input_output_aliases={**{i: 4 * i + 2 for i in range(n)}, **{n + i: 4 * i + 3 for i in range(n)}},
        compiler_params=pltpu.CompilerParams(has_side_effects=DATAFLOW),
    )(*args, *lands, *after)
    return [tuple(outs[4 * i:4 * i + 4]) + (items[i],) for i in range(n)]


def _started(handles):
    return handles[0][2]


def _exchange_wait(handles, after, name):
    n = len(handles)

    def body(*refs):
        x, y, c, _ = _mesh_place()
        for i in range(n):
            src, land, send, recv = refs[4 * i:4 * i + 4]
            del src
            seven = handles[i][4].seven(land)
            cp = pltpu.make_async_remote_copy(src_ref=seven, dst_ref=seven, send_sem=send, recv_sem=recv,
                                              device_id=(x, y, 1 - c), device_id_type=MESH)
            cp.wait_send()
            cp.wait_recv()

    args, in_specs, out_shape = [], [], []
    for send, recv, src, land, _ in handles:
        args += [src, land, send, recv]
        in_specs += [HBM_SPEC, HBM_SPEC, SEM_SPEC, SEM_SPEC]
        out_shape += [pltpu.HBM(src.shape, src.dtype), pltpu.HBM(land.shape, land.dtype)]
    outs = pl.pallas_call(
        body, name=name,
        in_specs=in_specs + [pl.BlockSpec(memory_space=pl.ANY)] * len(after), out_specs=[HBM_SPEC] * (2 * n),
        out_shape=out_shape,
        input_output_aliases={**{4 * i: 2 * i for i in range(n)}, **{4 * i + 1: 2 * i + 1 for i in range(n)}},
        compiler_params=pltpu.CompilerParams(has_side_effects=DATAFLOW),
    )(*args, *after)
    return [outs[2 * i + 1] for i in range(n)]


TM = 1024
TM_ACC = 512
TN_IN = 768


def kernel(x, norm1_g, w_in, mix_conv_w, attn_out_g, conv_out_g, w_out, norm2_g, ffn_up, ffn_conv_w, ffn_down, final_norm_g, loss_target, m_norm1_g, m_w_in, m_mix_conv_w, m_attn_out_g, m_conv_out_g, m_w_out, m_norm2_g, m_ffn_up, m_ffn_conv_w, m_ffn_down, m_final_norm_g, v_norm1_g, v_w_in, v_mix_conv_w, v_attn_out_g, v_conv_out_g, v_w_out, v_norm2_g, v_ffn_up, v_ffn_conv_w, v_ffn_down, v_final_norm_g):
    nbatch, seq, d = x.shape
    t = nbatch * seq
    nt, nta = t // TM, t // TM_ACC
    out_rows = D_MODEL // N_DEV
    down_rows = D_FF // N_DEV
    xf = x.reshape(t, d)
    target = loss_target.reshape(t, d)

    cw_local = jnp.concatenate([ffn_conv_w, mix_conv_w], axis=-1)
    cast = lambda w: _Item(w.astype(BF16), False)
    cast_in = lambda w: _Item(w.astype(BF16), False, land_cols=True)
    cw_all, win0 = _sequencer_exchange([_Item(cw_local, False), cast_in(w_in[0])], "gather_a", 0)
    wout0, wup0 = _sequencer_exchange([cast(w_out[0]), cast(ffn_up[0])], "gather_b", 1)
    wdown0, win1, wout1 = _sequencer_exchange([cast(ffn_down[0]), cast_in(w_in[1]), cast(w_out[1])], "gather_c", 2)
    wup1, wdown1 = _sequencer_exchange([cast(ffn_up[1]), cast(ffn_down[1])], "gather_d", 3)
    win, wup = [win0, win1], [wup0, wup1]
    wout = [w.reshape(D_MODEL, D_MODEL) for w in (wout0, wout1)]
    wdown = [w.reshape(N_UP_PAIRS, UP_CHUNK, D_MODEL) for w in (wdown0, wdown1)]
    fcw = [cw_all[:, k, :, :UP_CHUNK].reshape(2, N_UP_PAIRS, 3, UP_CHUNK) for k in range(DEPTH)]
    mcw = [cw_all[:, k, :, UP_CHUNK:].transpose(1, 0, 2).reshape(3, D_CONV) for k in range(DEPTH)]

    full = lambda i, j, k: (0, 0)

    saved = []
    xin = xf
    for l in range(DEPTH):
        h1 = _rms_fwd(xin, norm1_g[l][None], f"rms1_fwd_{l}")
        proj = _matmul(
            h1, win[l], grid=(nt, D_IN // TN_IN, 1), dims=NN, name=f"proj_{l}",
            a_spec=pl.BlockSpec((TM, D_MODEL), lambda i, j, k: (i, 0)),
            b_spec=pl.BlockSpec((D_MODEL, TN_IN), lambda i, j, k: (0, j)),
            o_spec=pl.BlockSpec((TM, TN_IN), lambda i, j, k: (i, j)), o_shape=(t, D_IN), o_dtype=F32)
        o, lse, cat = _attn_fwd(proj, attn_out_g[l][None], nbatch, seq)
        cat = _convmix_fwd(proj, cat, mcw[l], conv_out_g[l][None], nbatch, seq)
        xmid = _matmul(
            cat, wout[l], grid=(nta, 1, 1), dims=NN, name=f"mix_out_{l}",
            a_spec=pl.BlockSpec((TM_ACC, D_MODEL), lambda i, j, k: (i, 0)),
            b_spec=pl.BlockSpec((D_MODEL, D_MODEL), full),
            o_spec=pl.BlockSpec((TM_ACC, D_MODEL), lambda i, j, k: (i, 0)), o_shape=(t, D_MODEL), o_dtype=F32,
            res=xin, res_spec=pl.BlockSpec((TM_ACC, D_MODEL), lambda i, j, k: (i, 0)))
        h2 = _rms_fwd(xmid, norm2_g[l][None], f"rms2_fwd_{l}")
        pre = _matmul(
            h2, wup[l], grid=(nt, N_DEV, 1), dims=NN, name=f"ffn_up_{l}",
            a_spec=pl.BlockSpec((TM, D_MODEL), lambda i, j, k: (i, 0)),
            b_spec=pl.BlockSpec((None, D_MODEL, UP_CHUNK), lambda i, j, k: (j, 0, 0)),
            o_spec=pl.BlockSpec((None, TM, UP_CHUNK), lambda i, j, k: (j, i, 0)),
            o_shape=(N_DEV, t, UP_CHUNK), o_dtype=BF16).reshape(2, N_UP_PAIRS, t, UP_CHUNK)
        act = _ffn_act_fwd(pre, fcw[l], nbatch, seq)
        xout = _matmul(
            act, wdown[l], grid=(nta, 1, 1), dims=NN, name=f"ffn_down_{l}",
            a_spec=pl.BlockSpec((N_UP_PAIRS, TM_ACC, UP_CHUNK), lambda i, j, k: (0, i, 0)),
            b_spec=pl.BlockSpec((N_UP_PAIRS, UP_CHUNK, D_MODEL), lambda i, j, k: (0, 0, 0)),
            o_spec=pl.BlockSpec((TM_ACC, D_MODEL), lambda i, j, k: (i, 0)), o_shape=(t, D_MODEL), o_dtype=F32,
            res=xmid, res_spec=pl.BlockSpec((TM_ACC, D_MODEL), lambda i, j, k: (i, 0)))
        saved.append((xin, h1, proj, o, lse, cat, xmid, h2, pre, act))
        xin = xout

    loss_part, dx, dxb, dgf = _loss_head(xin, final_norm_g[None], target, "loss_head")

    dg1, dg2, dga, dgc = [None] * DEPTH, [None] * DEPTH, [None] * DEPTH, [None] * DEPTH
    for l in reversed(range(DEPTH)):
        xin, h1, proj, o, lse, cat, xmid, h2, pre, act = saved[l]
        d_act = _matmul(
            dxb, wdown[l], grid=(nt, N_UP_PAIRS, 1), dims=NT, name=f"d_act_{l}",
            a_spec=pl.BlockSpec((TM, D_MODEL), lambda i, j, k: (i, 0)),
            b_spec=pl.BlockSpec((None, UP_CHUNK, D_MODEL), lambda i, j, k: (j, 0, 0)),
            o_spec=pl.BlockSpec((None, TM, UP_CHUNK), lambda i, j, k: (j, i, 0)),
            o_shape=(N_UP_PAIRS, t, UP_CHUNK), o_dtype=BF16)
        g_down = _matmul(
            act, dxb, grid=(N_UP_PAIRS, 1, 1), dims=TN, name=f"g_down_{l}",
            a_spec=pl.BlockSpec((None, t, UP_CHUNK), lambda i, j, k: (i, 0, 0)),
            b_spec=pl.BlockSpec((t, D_MODEL), full),
            o_spec=pl.BlockSpec((None, UP_CHUNK, D_MODEL), lambda i, j, k: (i, 0, 0)),
            o_shape=(N_UP_PAIRS, UP_CHUNK, D_MODEL), o_dtype=BF16).reshape(N_DEV, down_rows, D_MODEL)
        d_pre, d_fcw = _ffn_act_bwd(pre, d_act, fcw[l], nbatch, seq)
        d_pre = d_pre.reshape(N_DEV, t, UP_CHUNK)
        dh2 = _matmul(
            d_pre, wup[l], grid=(nta, 1, 1), dims=NT, name=f"d_h2_{l}",
            a_spec=pl.BlockSpec((N_DEV, TM_ACC, UP_CHUNK), lambda i, j, k: (0, i, 0)),
            b_spec=pl.BlockSpec((N_DEV, D_MODEL, UP_CHUNK), lambda i, j, k: (0, 0, 0)),
            o_spec=pl.BlockSpec((TM_ACC, D_MODEL), lambda i, j, k: (i, 0)), o_shape=(t, D_MODEL), o_dtype=F32)
        g_up = _matmul(
            h2, d_pre, grid=(1, N_DEV, 1), dims=TN, name=f"g_up_{l}",
            a_spec=pl.BlockSpec((t, D_MODEL), full),
            b_spec=pl.BlockSpec((None, t, UP_CHUNK), lambda i, j, k: (j, 0, 0)),
            o_spec=pl.BlockSpec((None, D_MODEL, UP_CHUNK), lambda i, j, k: (j, 0, 0)),
            o_shape=(N_DEV, D_MODEL, UP_CHUNK), o_dtype=BF16)
        dxm, dxmb, dg2[l] = _rms_bwd(xmid, norm2_g[l][None], dh2, dx, f"rms2_bwd_{l}")
        g_out = _matmul(
            cat, dxmb, grid=(1, 1, nt), dims=TN, name=f"g_out_{l}",
            a_spec=pl.BlockSpec((TM, D_MODEL), lambda i, j, k: (k, 0)),
            b_spec=pl.BlockSpec((TM, D_MODEL), lambda i, j, k: (k, 0)),
            o_spec=pl.BlockSpec((D_MODEL, D_MODEL), full),
            o_shape=(D_MODEL, D_MODEL), o_dtype=BF16).reshape(N_DEV, out_rows, D_MODEL)
        if l == 0:
            land_out0, land_up0, land_down0 = _sequencer_exchange(
                [_Item(g_out, True), _Item(g_up, True), _Item(g_down, True)], "scatter_0a", 5)
        d_cat = _matmul(
            dxmb, wout[l], grid=(nta, 1, 1), dims=NT, name=f"d_cat_{l}",
            a_spec=pl.BlockSpec((TM_ACC, D_MODEL), lambda i, j, k: (i, 0)),
            b_spec=pl.BlockSpec((D_MODEL, D_MODEL), full),
            o_spec=pl.BlockSpec((TM_ACC, D_MODEL), lambda i, j, k: (i, 0)), o_shape=(t, D_MODEL), o_dtype=BF16)
        d_proj, dga[l] = _attn_bwd(proj, o, lse, d_cat, attn_out_g[l][None], nbatch, seq)
        d_proj, d_mcw, dgc[l] = _convmix_bwd(proj, d_cat, d_proj, mcw[l], conv_out_g[l][None], nbatch, seq)
        g_in = _matmul(
            h1, d_proj, grid=(1, D_IN // TN_IN, 1), dims=TN, name=f"g_in_{l}",
            a_spec=pl.BlockSpec((t, D_MODEL), full),
            b_spec=pl.BlockSpec((t, TN_IN), lambda i, j, k: (0, j)),
            o_spec=pl.BlockSpec((D_MODEL, TN_IN), lambda i, j, k: (0, j)),
            o_shape=(D_MODEL, D_IN), o_dtype=BF16)
        g_cw = jnp.concatenate(
            [d_fcw.reshape(N_DEV, 3, UP_CHUNK), d_mcw.reshape(3, N_DEV, D_CONV // N_DEV).transpose(1, 0, 2)], axis=-1)
        if l == 0:
            land_in0, land_cw0 = _sequencer_exchange([_Item(g_in, "cols"), _Item(g_cw, True)], "scatter_0b", 6)
        else:
            land_in1, land_out1, land_up1, land_down1, land_cw1 = _sequencer_exchange(
                [_Item(g_in, "cols"), _Item(g_out, True), _Item(g_up, True), _Item(g_down, True), _Item(g_cw, True)],
                "scatter_1", 4)
        dh1 = _matmul(
            d_proj, win[l], grid=(nta, 1, 1), dims=NT, name=f"d_h1_{l}",
            a_spec=pl.BlockSpec((TM_ACC, D_IN), lambda i, j, k: (i, 0)),
            b_spec=pl.BlockSpec((D_MODEL, D_IN), full),
            o_spec=pl.BlockSpec((TM_ACC, D_MODEL), lambda i, j, k: (i, 0)), o_shape=(t, D_MODEL), o_dtype=F32)
        dx, dxb, dg1[l] = _rms_bwd(xin, norm1_g[l][None], dh1, dxm, f"rms1_bwd_{l}")

    def pack_small(n1, a, c, n2, f):
        return jnp.concatenate(
            [n1, n2, f[None], jnp.concatenate([a, c], axis=-1), jnp.zeros((1, D_MODEL), F32)], axis=0)[None]

    res_out = _adamw([land_out0, land_out1], w_out, m_w_out, v_w_out, out_rows, "adamw_w_out")
    res_up = _adamw([land_up0, land_up1], ffn_up, m_ffn_up, v_ffn_up, 256, "adamw_ffn_up")
    res_down = _adamw([land_down0, land_down1], ffn_down, m_ffn_down, v_ffn_down, down_rows, "adamw_ffn_down")
    small = jnp.concatenate(
        [dg1[0], dg1[1], dg2[0], dg2[1], dgf,
         jnp.concatenate([dga[0], dgc[0]], axis=-1), jnp.concatenate([dga[1], dgc[1]], axis=-1),
         jnp.zeros((1, D_MODEL), F32)], axis=0)
    (land_small,) = _exchange([_Item(small, False)], "gather_gain_grads")
    res_small = _adamw(
        [land_small], pack_small(norm1_g, attn_out_g, conv_out_g, norm2_g, final_norm_g),
        pack_small(m_norm1_g, m_attn_out_g, m_conv_out_g, m_norm2_g, m_final_norm_g),
        pack_small(v_norm1_g, v_attn_out_g, v_conv_out_g, v_norm2_g, v_final_norm_g), SUBLANES, "adamw_gains")
    res_in = _adamw([land_in0, land_in1], w_in, m_w_in, v_w_in, 256, "adamw_w_in")
    res_cw = _adamw(
        [land_cw0, land_cw1], cw_local, jnp.concatenate([m_ffn_conv_w, m_mix_conv_w], axis=-1),
        jnp.concatenate([v_ffn_conv_w, v_mix_conv_w], axis=-1), 3, "adamw_conv_w")

    loss = lax.psum(loss_part[0, 0], ("x", "y", "c"))

    def unpack(kind):
        s = res_small[kind][0]
        cwr = res_cw[kind]
        return (s[0:2], res_in[kind], cwr[..., UP_CHUNK:], s[5:7, :D_ATTN], s[5:7, D_ATTN:], res_out[kind],
                s[2:4], res_up[kind], cwr[..., :UP_CHUNK], res_down[kind], s[4])

    return (loss, dx.reshape(nbatch, seq, d), *unpack(0), *unpack(1), *unpack(2), *unpack(3))
```

```python
import math

import jax
import jax.numpy as jnp
from jax import lax
from jax.experimental import pallas as pl
from jax.experimental.pallas import tpu as pltpu
from jax.experimental.pallas import tpu_sc as plsc

F32 = jnp.float32
BF16 = jnp.bfloat16

D_MODEL = 1024
D_ATTN = 512
D_CONV = 512
HEAD_DIM = 64
N_HEADS = 8
D_FF = 2816
DEPTH = 2
D_IN = 3 * D_ATTN + 3 * D_CONV
EPS = 1e-6
DILATIONS = (1, 4, 16)
BAND = 128
N_DEV = 8
IN_CHUNK = D_IN // N_DEV
UP_CHUNK = 2 * D_FF // N_DEV
N_UP_PAIRS = N_DEV // 2
CW_PACK = UP_CHUNK + D_CONV // N_DEV
ADAM_LR = 0.001
ADAM_B1 = 0.9
ADAM_B2 = 0.999
ADAM_EPS = 1e-08
ADAM_WD = 0.01
ADAM_STEP = 10
LANES = 128
SUBLANES = 8
VMEM_LIMIT = 56 * 1024 * 1024

NEG = -1e30
MESH = pl.DeviceIdType.MESH


def _params(sem=None, vmem=VMEM_LIMIT):
    return pltpu.CompilerParams(dimension_semantics=sem, vmem_limit_bytes=vmem)


NN = (((1,), (0,)), ((), ()))
NT = (((1,), (1,)), ((), ()))
TN = (((0,), (0,)), ((), ()))


def _matmul(a, b, *, grid, a_spec, b_spec, o_spec, o_shape, o_dtype, dims, name, res=None, res_spec=None, after=()):
    nk = grid[2]
    o_block = tuple(s for s in o_spec.block_shape if s is not None)
    na = len(after)

    def body(*refs):
        refs = refs[:2 + (res is not None)] + refs[2 + (res is not None) + na:]
        if res is None:
            a_ref, b_ref, o_ref, *scr = refs
            r_ref = None
        else:
            a_ref, b_ref, r_ref, o_ref, *scr = refs
        def dot(av, bv):
            return lax.dot_general(av.astype(BF16), bv.astype(BF16), dims, preferred_element_type=F32)

        if len(a_ref.shape) == 3:
            part = dot(a_ref[0], b_ref[0])
            for c in range(1, a_ref.shape[0]):
                part = part + dot(a_ref[c], b_ref[c])
        else:
            part = dot(a_ref[...], b_ref[...])

        def finish(total):
            if r_ref is not None:
                total = total + r_ref[...]
            o_ref[...] = total.astype(o_dtype)

        if nk == 1:
            finish(part)
        else:
            acc = scr[0]
            k = pl.program_id(2)

            @pl.when(k == 0)
            def _():
                acc[...] = part

            @pl.when(k > 0)
            def _():
                acc[...] += part

            @pl.when(k == nk - 1)
            def _():
                finish(acc[...])

    in_specs = [a_spec, b_spec] + ([res_spec] if res is not None else []) + [pl.BlockSpec(memory_space=pl.ANY)] * na
    args = (a, b) + ((res,) if res is not None else ()) + tuple(after)
    return pl.pallas_call(
        body, name=name, grid=grid, in_specs=in_specs, out_specs=o_spec,
        out_shape=jax.ShapeDtypeStruct(o_shape, o_dtype),
        scratch_shapes=[pltpu.VMEM(o_block, F32)] if nk > 1 else [],
        compiler_params=_params(("parallel", "parallel", "arbitrary")),
    )(*args)


ROW_TILE = 512


def _rms_fwd(x, g, name):
    t, d = x.shape

    def body(x_ref, g_ref, h_ref):
        xv = x_ref[...]
        r = lax.rsqrt(jnp.mean(xv * xv, axis=-1, keepdims=True) + EPS)
        h_ref[...] = (xv * r * g_ref[...]).astype(BF16)

    return pl.pallas_call(
        body, name=name, grid=(t // ROW_TILE,),
        in_specs=[pl.BlockSpec((ROW_TILE, d), lambda i: (i, 0)), pl.BlockSpec((1, d), lambda i: (0, 0))],
        out_specs=pl.BlockSpec((ROW_TILE, d), lambda i: (i, 0)),
        out_shape=jax.ShapeDtypeStruct((t, d), BF16),
        compiler_params=_params(("parallel",)),
    )(x, g)


def _rms_bwd(x, g, dh, dres, name):
    t, d = x.shape

    def body(x_ref, g_ref, dh_ref, dres_ref, dx_ref, dxb_ref, dg_ref):
        xv = x_ref[...]
        r = lax.rsqrt(jnp.mean(xv * xv, axis=-1, keepdims=True) + EPS)
        xh = xv * r
        dhv = dh_ref[...]
        gd = dhv * g_ref[...]
        dx = r * (gd - xh * jnp.mean(gd * xh, axis=-1, keepdims=True)) + dres_ref[...]
        dx_ref[...] = dx
        dxb_ref[...] = dx.astype(BF16)
        part = jnp.sum(dhv * xh, axis=0, keepdims=True)

        @pl.when(pl.program_id(0) == 0)
        def _():
            dg_ref[...] = part

        @pl.when(pl.program_id(0) > 0)
        def _():
            dg_ref[...] += part

    row = pl.BlockSpec((ROW_TILE, d), lambda i: (i, 0))
    vec = pl.BlockSpec((1, d), lambda i: (0, 0))
    return pl.pallas_call(
        body, name=name, grid=(t // ROW_TILE,),
        in_specs=[row, vec, row, row], out_specs=[row, row, vec],
        out_shape=[jax.ShapeDtypeStruct((t, d), F32), jax.ShapeDtypeStruct((t, d), BF16),
                   jax.ShapeDtypeStruct((1, d), F32)],
        compiler_params=_params(("arbitrary",)),
    )(x, g, dh, dres)


def _loss_head(x, g, target, name):
    t, d = x.shape

    def body(x_ref, g_ref, t_ref, loss_ref, dx_ref, dxb_ref, dg_ref):
        xv = x_ref[...]
        r = lax.rsqrt(jnp.mean(xv * xv, axis=-1, keepdims=True) + EPS)
        xh = xv * r
        gv = g_ref[...]
        err = xh * gv - t_ref[...]
        loss = jnp.full((1, LANES), 0.5 / d, F32) * jnp.sum(err * err)
        dy = err * (1.0 / d)
        gd = dy * gv
        dx = r * (gd - xh * jnp.mean(gd * xh, axis=-1, keepdims=True))
        dx_ref[...] = dx
        dxb_ref[...] = dx.astype(BF16)
        part = jnp.sum(dy * xh, axis=0, keepdims=True)

        @pl.when(pl.program_id(0) == 0)
        def _():
            dg_ref[...] = part
            loss_ref[...] = loss

        @pl.when(pl.program_id(0) > 0)
        def _():
            dg_ref[...] += part
            loss_ref[...] += loss

    row = pl.BlockSpec((ROW_TILE, d), lambda i: (i, 0))
    vec = pl.BlockSpec((1, d), lambda i: (0, 0))
    return pl.pallas_call(
        body, name=name, grid=(t // ROW_TILE,),
        in_specs=[row, vec, row],
        out_specs=[pl.BlockSpec((1, LANES), lambda i: (0, 0)), row, row, vec],
        out_shape=[jax.ShapeDtypeStruct((1, LANES), F32), jax.ShapeDtypeStruct((t, d), F32),
                   jax.ShapeDtypeStruct((t, d), BF16), jax.ShapeDtypeStruct((1, d), F32)],
        compiler_params=_params(("arbitrary",)),
    )(x, g, target)


def _group_matrix(n):
    shift = int(math.log2(HEAD_DIM))
    r = lax.broadcasted_iota(jnp.int32, (n, n), 0) >> shift
    c = lax.broadcasted_iota(jnp.int32, (n, n), 1) >> shift
    return (r == c).astype(BF16)


def _group_sum(v, gmat):
    hi = v.astype(BF16)
    lo = (v - hi.astype(F32)).astype(BF16)

    def dot(p):
        return jnp.dot(p, gmat, preferred_element_type=F32)

    return dot(hi) + dot(lo)


def _shift_rows(ext, k):
    return pltpu.roll(ext, k % ext.shape[0], 0)


def _store_columns(stage, out_hbm, sems, row0, nrows, col_blocks):
    rows = pl.ds(pl.multiple_of(row0, SUBLANES * 2), nrows)
    copies = [
        pltpu.make_async_copy(stage.at[i], out_hbm.at[rows, pl.ds(pl.multiple_of(cb * LANES, LANES), LANES)], sems.at[i])
        for i, cb in enumerate(col_blocks)
    ]
    for cp in copies:
        cp.start()
    for cp in copies:
        cp.wait()


def _attn_consts(width):
    i = lax.broadcasted_iota(jnp.int32, (BAND, width), 0)
    j = lax.broadcasted_iota(jnp.int32, (BAND, width), 1)
    dist = (width - BAND) + i - j
    inwin = (dist >= 0) & (dist <= BAND)
    return dist.astype(F32), inwin, j


def _head_masks():
    lane = lax.broadcasted_iota(jnp.int32, (1, LANES), 1)
    return [(lane < HEAD_DIM).astype(F32), (lane >= HEAD_DIM).astype(F32)]


def _pair_bias(slope, dil):
    distf, inwin, _ = _attn_consts(2 * BAND)
    return jnp.concatenate([jnp.where(inwin, distf * (slope[hh] * (-float(dil))), NEG) for hh in range(2)], axis=0)


def _stack_heads(xv, hmask):
    return jnp.concatenate([xv * hmask[0], xv * hmask[1]], axis=0).astype(BF16)


BLOCK_UNROLL = 3


def _for_blocks(seq, dil, block):
    nb = seq // dil // BAND

    def residue(r, carry):
        base = r * nb
        block(pl.multiple_of(base * BAND, BAND), None)
        if nb > 1:
            def rest(n, c):
                block(pl.multiple_of((base + n) * BAND, BAND), pl.multiple_of((base + n - 1) * BAND, BAND))
                return c

            lax.fori_loop(1, nb, rest, 0, unroll=BLOCK_UNROLL)
        return carry

    if dil == 1:
        residue(0, 0)
    else:
        lax.fori_loop(0, dil, residue, 0, unroll=BLOCK_UNROLL + 1 if nb == 1 else 1)


def _permute_in(src_ref, dst_ref, dil, seq):
    length = seq // dil
    for r in range(dil):
        dst_ref[pl.ds(r * length, length), :] = src_ref[pl.ds(r, length, stride=dil), :].astype(dst_ref.dtype)


def _slopes_table():
    slopes = 2.0 ** (-8.0 * jnp.arange(1, N_HEADS + 1, dtype=F32) / N_HEADS)
    return jnp.broadcast_to(slopes[:, None], (N_HEADS, 2 * BAND))


def _attn_fwd(proj, attn_g, nbatch, seq):
    t = nbatch * seq
    scale = HEAD_DIM ** -0.5

    def body(q_ref, k_ref, v_ref, g_ref, sl_ref, o_ref, lse_ref, cat_ref, pq, pk, pv, po, pm, pll, ao, am, al):
        hp = pl.program_id(1)
        hmask = _head_masks()
        slope = [sl_ref[pl.ds(2 * hp + hh, 1), :] for hh in range(2)]

        def run_branch(dil, qs, ks, vs, osink, msink, lsink):
            bias = _pair_bias(slope, dil)

            def block(row0, prow):
                cur = pl.ds(row0, BAND)
                q2 = _stack_heads(qs[cur, :] * scale, hmask)
                if prow is None:
                    kk, vv, bias_b = ks[cur, :], vs[cur, :], bias[:, BAND:]
                else:
                    prev = pl.ds(prow, BAND)
                    kk = jnp.concatenate([ks[prev, :], ks[cur, :]], axis=0)
                    vv = jnp.concatenate([vs[prev, :], vs[cur, :]], axis=0)
                    bias_b = bias
                s = lax.dot_general(q2, kk.astype(BF16), NT, preferred_element_type=F32) + bias_b
                m = jnp.max(s, axis=1, keepdims=True)
                p = jnp.exp(s - m)
                l = jnp.sum(p, axis=1, keepdims=True)
                pb = p.astype(BF16)
                o = jnp.dot(jnp.concatenate([pb[:BAND], pb[BAND:]], axis=1), _stack_heads(vv, hmask),
                            preferred_element_type=F32)
                osink[cur, :] = o
                msink[cur, :] = m[:BAND] * hmask[0] + m[BAND:] * hmask[1]
                lsink[cur, :] = l[:BAND] * hmask[0] + l[BAND:] * hmask[1]

            _for_blocks(seq, dil, block)

        run_branch(1, q_ref, k_ref, v_ref, ao, am, al)
        for dil in DILATIONS[1:]:
            length = seq // dil
            _permute_in(q_ref, pq, dil, seq)
            _permute_in(k_ref, pk, dil, seq)
            _permute_in(v_ref, pv, dil, seq)
            run_branch(dil, pq, pk, pv, po, pm, pll)
            for r in range(dil):
                nat = pl.ds(r, length, stride=dil)
                per = pl.ds(r * length, length)
                m0 = am[nat, :]
                mb = pm[per, :]
                mn = jnp.maximum(m0, mb)
                e0 = jnp.exp(m0 - mn)
                eb = jnp.exp(mb - mn)
                ao[nat, :] = ao[nat, :] * e0 + po[per, :] * eb
                al[nat, :] = al[nat, :] * e0 + pll[per, :] * eb
                am[nat, :] = mn

        gmat = _group_matrix(LANES)
        gv = g_ref[...]

        def fin(c, carry):
            rows = pl.ds(pl.multiple_of(c * 256, 256), 256)
            lv = al[rows, :]
            o = ao[rows, :] / lv
            o_ref[rows, :] = o
            lse_ref[rows, :] = am[rows, :] + jnp.log(lv)
            ms = _group_sum(o * o, gmat) * (1.0 / HEAD_DIM)
            cat_ref[rows, :] = (o * lax.rsqrt(ms + EPS) * gv).astype(BF16)
            return carry

        lax.fori_loop(0, seq // 256, fin, 0)

    nq = D_ATTN // LANES
    blk = lambda off: pl.BlockSpec((seq, LANES), lambda b, h: (b, h + off))
    scratch = [pltpu.VMEM((seq, LANES), F32) for _ in range(9)]
    return pl.pallas_call(
        body, name="attn_fwd", grid=(nbatch, nq),
        in_specs=[blk(0), blk(nq), blk(2 * nq), pl.BlockSpec((1, LANES), lambda b, h: (0, h)),
                  pl.BlockSpec((N_HEADS, 2 * BAND), lambda b, h: (0, 0))],
        out_specs=[blk(0), blk(0), blk(0)],
        out_shape=[jax.ShapeDtypeStruct((t, D_ATTN), F32), jax.ShapeDtypeStruct((t, D_ATTN), F32),
                   jax.ShapeDtypeStruct((t, D_MODEL), BF16)],
        scratch_shapes=scratch,
        compiler_params=_params(("parallel", "parallel")),
    )(proj, proj, proj, attn_g, _slopes_table())


def _attn_bwd(proj, o, lse, d_cat, attn_g, nbatch, seq):
    t = nbatch * seq
    scale = HEAD_DIM ** -0.5

    def body(q_ref, k_ref, v_ref, o_ref, lse_ref, dy_ref, g_ref, sl_ref, dproj_ref, dg_ref,
             do_n, dl_n, dq_n, dk_n, dv_n, pq, pk, pv, pdo, plse, pdl, pdq, pdk, pdv, stage, sems):
        hp = pl.program_id(0)
        hmask = _head_masks()
        slope = [sl_ref[pl.ds(2 * hp + hh, 1), :] for hh in range(2)]
        gmat = _group_matrix(LANES)
        gv = g_ref[...]

        def prep(c, dg):
            rows = pl.ds(pl.multiple_of(c * 256, 256), 256)
            ov = o_ref[rows, :]
            dyn = dy_ref[rows, :].astype(F32)
            r = lax.rsqrt(_group_sum(ov * ov, gmat) * (1.0 / HEAD_DIM) + EPS)
            gd = dyn * gv
            oh = ov * r
            do = r * (gd - oh * (_group_sum(gd * oh, gmat) * (1.0 / HEAD_DIM)))
            do_n[rows, :] = do
            dl_n[rows, :] = _group_sum(do * ov, gmat)
            return dg + jnp.sum(dyn * oh, axis=0, keepdims=True)

        dg = lax.fori_loop(0, seq // 256, prep, jnp.zeros((1, LANES), F32))

        @pl.when(pl.program_id(1) == 0)
        def _():
            dg_ref[...] = dg

        @pl.when(pl.program_id(1) > 0)
        def _():
            dg_ref[...] += dg

        def clear(*refs):
            def step(c, carry):
                rows = pl.ds(pl.multiple_of(c * 256, 256), 256)
                for ref in refs:
                    ref[rows, :] = jnp.zeros((256, LANES), F32)
                return carry

            lax.fori_loop(0, seq // 256, step, 0)

        clear(dq_n, dk_n, dv_n)

        def run_branch(dil, qs, ks, vs, dos, lses, dls, dqs, dks, dvs):
            bias = _pair_bias(slope, dil)

            def per_head(xv):
                return jnp.concatenate([xv[:, 0:1], xv[:, HEAD_DIM:HEAD_DIM + 1]], axis=0)

            def block(row0, prow):
                cur = pl.ds(row0, BAND)
                q2 = _stack_heads(qs[cur, :] * scale, hmask)
                do2 = _stack_heads(dos[cur, :], hmask)
                if prow is None:
                    kk, vv, bias_b = ks[cur, :], vs[cur, :], bias[:, BAND:]
                else:
                    prev = pl.ds(prow, BAND)
                    kk = jnp.concatenate([ks[prev, :], ks[cur, :]], axis=0)
                    vv = jnp.concatenate([vs[prev, :], vs[cur, :]], axis=0)
                    bias_b = bias
                s = lax.dot_general(q2, kk.astype(BF16), NT, preferred_element_type=F32) + bias_b
                p = jnp.exp(s - per_head(lses[cur, :]))
                dp = lax.dot_general(do2, vv.astype(BF16), NT, preferred_element_type=F32)
                ds = (p * (dp - per_head(dls[cur, :]))).astype(BF16)
                pb = p.astype(BF16)
                dqs[cur, :] += jnp.dot(jnp.concatenate([ds[:BAND], ds[BAND:]], axis=1), _stack_heads(kk, hmask),
                                       preferred_element_type=F32)
                dkk = lax.dot_general(ds, q2, TN, preferred_element_type=F32)
                dvv = lax.dot_general(pb, do2, TN, preferred_element_type=F32)
                if prow is None:
                    dks[cur, :] += dkk
                    dvs[cur, :] += dvv
                else:
                    dks[prev, :] += dkk[:BAND]
                    dvs[prev, :] += dvv[:BAND]
                    dks[cur, :] += dkk[BAND:]
                    dvs[cur, :] += dvv[BAND:]

            _for_blocks(seq, dil, block)

        run_branch(1, q_ref, k_ref, v_ref, do_n, lse_ref, dl_n, dq_n, dk_n, dv_n)
        for dil in DILATIONS[1:]:
            length = seq // dil
            for src, dst in ((q_ref, pq), (k_ref, pk), (v_ref, pv), (do_n, pdo), (lse_ref, plse), (dl_n, pdl)):
                _permute_in(src, dst, dil, seq)
            clear(pdq, pdk, pdv)
            run_branch(dil, pq, pk, pv, pdo, plse, pdl, pdq, pdk, pdv)
            for r in range(dil):
                nat = pl.ds(r, length, stride=dil)
                per = pl.ds(r * length, length)
                dq_n[nat, :] += pdq[per, :]
                dk_n[nat, :] += pdk[per, :]
                dv_n[nat, :] += pdv[per, :]

        def emit(c, carry):
            rows = pl.ds(pl.multiple_of(c * 256, 256), 256)
            stage[0, rows, :] = (dq_n[rows, :] * scale).astype(BF16)
            stage[1, rows, :] = dk_n[rows, :].astype(BF16)
            stage[2, rows, :] = dv_n[rows, :].astype(BF16)
            return carry

        lax.fori_loop(0, seq // 256, emit, 0)
        _store_columns(stage, dproj_ref, sems, pl.program_id(1) * seq, seq, [hp, nq + hp, 2 * nq + hp])

    nq = D_ATTN // LANES
    blk = lambda off: pl.BlockSpec((seq, LANES), lambda h, b: (b, h + off))
    vec = pl.BlockSpec((1, LANES), lambda h, b: (0, h))
    scratch = [pltpu.VMEM((seq, LANES), F32) for _ in range(14)]
    scratch += [pltpu.VMEM((3, seq, LANES), BF16), pltpu.SemaphoreType.DMA((3,))]
    d_proj, dg = pl.pallas_call(
        body, name="attn_bwd", grid=(nq, nbatch),
        in_specs=[blk(0), blk(nq), blk(2 * nq), blk(0), blk(0), blk(0), vec,
                  pl.BlockSpec((N_HEADS, 2 * BAND), lambda h, b: (0, 0))],
        out_specs=[pl.BlockSpec(memory_space=pl.ANY), vec],
        out_shape=[jax.ShapeDtypeStruct((t, D_IN), BF16), jax.ShapeDtypeStruct((1, D_ATTN), F32)],
        scratch_shapes=scratch,
        compiler_params=_params(("arbitrary", "arbitrary")),
    )(proj, proj, proj, o, lse, d_cat, attn_g, _slopes_table())
    return d_proj, dg


HALO = 2 * SUBLANES


def _window(ref, c, rows, nchunks, after):
    row0 = pl.multiple_of(c * rows, rows)
    prev0 = pl.multiple_of(jnp.maximum(row0 - HALO, 0), HALO)
    parts = [ref[pl.ds(prev0, HALO), :].astype(F32) * (c > 0).astype(F32), ref[pl.ds(row0, rows), :].astype(F32)]
    if after:
        next0 = pl.multiple_of(jnp.minimum(row0 + rows, (nchunks - 1) * rows), HALO)
        parts.append(ref[pl.ds(next0, HALO), :].astype(F32) * (c < nchunks - 1).astype(F32))
    return jnp.concatenate(parts, axis=0)


def _conv(z, w):
    return w[0:1] * _shift_rows(z, 2) + w[1:2] * _shift_rows(z, 1) + w[2:3] * z


def _conv_t(dy, w):
    return w[2:3] * dy + w[1:2] * _shift_rows(dy, -1) + w[0:1] * _shift_rows(dy, -2)


def _conv_wgrad(dy, z, cur):
    return [jnp.sum((dy * _shift_rows(z, 2 - k))[cur], axis=0, keepdims=True) for k in range(3)]


MIX_ROWS = 256
GATE_B_BLOCK = 3 * D_ATTN // LANES
GATE_C_BLOCK = GATE_B_BLOCK + D_CONV // LANES
U_BLOCK = GATE_C_BLOCK + D_CONV // LANES


def _convmix_fwd(proj, cat, mcw, conv_g, nbatch, seq):
    nchunks = seq // MIX_ROWS

    def body(gb_ref, gc_ref, u_ref, w_ref, g_ref, cat_in, cat_ref):
        del cat_in
        gmat = _group_matrix(LANES)
        w = w_ref[...]
        gv = g_ref[...]

        def step(c, carry):
            cur = pl.ds(pl.multiple_of(c * MIX_ROWS, MIX_ROWS), MIX_ROWS)
            z = _window(gc_ref, c, MIX_ROWS, nchunks, False) * _window(u_ref, c, MIX_ROWS, nchunks, False)
            y = gb_ref[cur, :] * _conv(z, w)[HALO:]
            ms = _group_sum(y * y, gmat) * (1.0 / HEAD_DIM)
            cat_ref[cur, :] = (y * lax.rsqrt(ms + EPS) * gv).astype(BF16)
            return carry

        lax.fori_loop(0, nchunks, step, 0)

    nc = D_CONV // LANES
    blk = lambda off: pl.BlockSpec((seq, LANES), lambda b, j: (b, j + off))
    return pl.pallas_call(
        body, name="convmix_fwd", grid=(nbatch, nc),
        in_specs=[blk(GATE_B_BLOCK), blk(GATE_C_BLOCK), blk(U_BLOCK),
                  pl.BlockSpec((3, LANES), lambda b, j: (0, j)), pl.BlockSpec((1, LANES), lambda b, j: (0, j)),
                  pl.BlockSpec(memory_space=pl.ANY)],
        out_specs=blk(D_ATTN // LANES),
        out_shape=jax.ShapeDtypeStruct(cat.shape, cat.dtype),
        input_output_aliases={5: 0},
        compiler_params=_params(("parallel", "parallel")),
    )(proj, proj, proj, mcw, conv_g, cat)


def _convmix_bwd(proj, d_cat, d_proj, mcw, conv_g, nbatch, seq):
    nchunks = seq // MIX_ROWS

    def body(gb_ref, gc_ref, u_ref, dy_ref, w_ref, g_ref, dproj_in, dproj_ref, dw_ref, dg_ref, stage, sems):
        del dproj_in
        cb = pl.program_id(0)
        b = pl.program_id(1)
        gmat = _group_matrix(LANES)
        w = w_ref[...]
        gv = g_ref[...]
        cur = slice(HALO, HALO + MIX_ROWS)

        def step(c, carry):
            rows = pl.ds(pl.multiple_of(c * MIX_ROWS, MIX_ROWS), MIX_ROWS)
            gb = _window(gb_ref, c, MIX_ROWS, nchunks, True)
            gc = _window(gc_ref, c, MIX_ROWS, nchunks, True)
            u = _window(u_ref, c, MIX_ROWS, nchunks, True)
            dyn = _window(dy_ref, c, MIX_ROWS, nchunks, True)
            z = gc * u
            conv = _conv(z, w)
            y = gb * conv
            r = lax.rsqrt(_group_sum(y * y, gmat) * (1.0 / HEAD_DIM) + EPS)
            yh = y * r
            gd = dyn * gv
            dy = r * (gd - yh * (_group_sum(gd * yh, gmat) * (1.0 / HEAD_DIM)))
            dc = dy * gb
            dz = _conv_t(dc, w)
            stage[0, rows, :] = (dy * conv)[cur].astype(BF16)
            stage[1, rows, :] = (dz * u)[cur].astype(BF16)
            stage[2, rows, :] = (dz * gc)[cur].astype(BF16)
            dws = _conv_wgrad(dc, z, cur)
            dg = jnp.sum((dyn * yh)[cur], axis=0, keepdims=True)
            return tuple(a + d for a, d in zip(carry, dws + [dg]))

        zero = jnp.zeros((1, LANES), F32)
        dw0, dw1, dw2, dg = lax.fori_loop(0, nchunks, step, (zero, zero, zero, zero))

        @pl.when(b == 0)
        def _():
            dw_ref[0:1, :] = dw0
            dw_ref[1:2, :] = dw1
            dw_ref[2:3, :] = dw2
            dg_ref[...] = dg

        @pl.when(b > 0)
        def _():
            dw_ref[0:1, :] += dw0
            dw_ref[1:2, :] += dw1
            dw_ref[2:3, :] += dw2
            dg_ref[...] += dg

        _store_columns(stage, dproj_ref, sems, b * seq, seq, [GATE_B_BLOCK + cb, GATE_C_BLOCK + cb, U_BLOCK + cb])

    nc = D_CONV // LANES
    blk = lambda off: pl.BlockSpec((seq, LANES), lambda j, b: (b, j + off))
    return pl.pallas_call(
        body, name="convmix_bwd", grid=(nc, nbatch),
        in_specs=[blk(GATE_B_BLOCK), blk(GATE_C_BLOCK), blk(U_BLOCK), blk(D_ATTN // LANES),
                  pl.BlockSpec((3, LANES), lambda j, b: (0, j)), pl.BlockSpec((1, LANES), lambda j, b: (0, j)),
                  pl.BlockSpec(memory_space=pl.ANY)],
        out_specs=[pl.BlockSpec(memory_space=pl.ANY), pl.BlockSpec((3, LANES), lambda j, b: (0, j)),
                   pl.BlockSpec((1, LANES), lambda j, b: (0, j))],
        out_shape=[jax.ShapeDtypeStruct(d_proj.shape, d_proj.dtype), jax.ShapeDtypeStruct((3, D_CONV), F32),
                   jax.ShapeDtypeStruct((1, D_CONV), F32)],
        scratch_shapes=[pltpu.VMEM((3, seq, LANES), BF16), pltpu.SemaphoreType.DMA((3,))],
        input_output_aliases={6: 0},
        compiler_params=_params(("arbitrary", "arbitrary")),
    )(proj, proj, proj, d_cat, mcw, conv_g, d_proj)


FFN_ROWS = 128


def _ffn_act_fwd(pre, fcw, nbatch, seq):
    t = nbatch * seq
    nchunks = seq // FFN_ROWS

    def body(pre_ref, w_ref, act_ref):
        wa = w_ref[0]
        wc = w_ref[1]

        def step(c, carry):
            cur = pl.ds(pl.multiple_of(c * FFN_ROWS, FFN_ROWS), FFN_ROWS)
            a = _conv(_window(pre_ref.at[0], c, FFN_ROWS, nchunks, False), wa)[HALO:]
            v = _conv(_window(pre_ref.at[1], c, FFN_ROWS, nchunks, False), wc)[HALO:]
            act_ref[cur, :] = (a * jax.nn.sigmoid(a) * v).astype(BF16)
            return carry

        lax.fori_loop(0, nchunks, step, 0)

    return pl.pallas_call(
        body, name="ffn_act_fwd", grid=(N_UP_PAIRS, nbatch),
        in_specs=[pl.BlockSpec((2, None, seq, UP_CHUNK), lambda i, b: (0, i, b, 0)),
                  pl.BlockSpec((2, None, 3, UP_CHUNK), lambda i, b: (0, i, 0, 0))],
        out_specs=pl.BlockSpec((None, seq, UP_CHUNK), lambda i, b: (i, b, 0)),
        out_shape=jax.ShapeDtypeStruct((N_UP_PAIRS, t, UP_CHUNK), BF16),
        compiler_params=_params(("parallel", "parallel")),
    )(pre, fcw)


def _ffn_act_bwd(pre, d_act, fcw, nbatch, seq):
    nchunks = seq // FFN_ROWS

    def body(pre_ref, da_ref, w_ref, dpre_ref, dw_ref):
        b = pl.program_id(1)
        wa = w_ref[0]
        wc = w_ref[1]
        cur = slice(HALO, HALO + FFN_ROWS)

        def step(c, carry):
            rows = pl.ds(pl.multiple_of(c * FFN_ROWS, FFN_ROWS), FFN_ROWS)
            pg = _window(pre_ref.at[0], c, FFN_ROWS, nchunks, True)
            pv = _window(pre_ref.at[1], c, FFN_ROWS, nchunks, True)
            dact = _window(da_ref, c, FFN_ROWS, nchunks, True)
            a = _conv(pg, wa)
            v = _conv(pv, wc)
            sg = jax.nn.sigmoid(a)
            da = dact * v * (sg * (1.0 + a * (1.0 - sg)))
            dv = dact * (a * sg)
            dpre_ref[0, rows, :] = _conv_t(da, wa)[cur].astype(BF16)
            dpre_ref[1, rows, :] = _conv_t(dv, wc)[cur].astype(BF16)
            return tuple(acc + d for acc, d in zip(carry, _conv_wgrad(da, pg, cur) + _conv_wgrad(dv, pv, cur)))

        zero = jnp.zeros((1, UP_CHUNK), F32)
        sums = lax.fori_loop(0, nchunks, step, (zero,) * 6)

        @pl.when(b == 0)
        def _():
            for i in range(6):
                dw_ref[i // 3, pl.ds(i % 3, 1), :] = sums[i]

        @pl.when(b > 0)
        def _():
            for i in range(6):
                dw_ref[i // 3, pl.ds(i % 3, 1), :] += sums[i]

    pair = pl.BlockSpec((2, None, seq, UP_CHUNK), lambda i, b: (0, i, b, 0))
    wspec = pl.BlockSpec((2, None, 3, UP_CHUNK), lambda i, b: (0, i, 0, 0))
    return pl.pallas_call(
        body, name="ffn_act_bwd", grid=(N_UP_PAIRS, nbatch),
        in_specs=[pair, pl.BlockSpec((None, seq, UP_CHUNK), lambda i, b: (i, b, 0)), wspec],
        out_specs=[pair, wspec],
        out_shape=[jax.ShapeDtypeStruct(pre.shape, BF16), jax.ShapeDtypeStruct(fcw.shape, F32)],
        compiler_params=_params(("parallel", "arbitrary")),
    )(pre, d_act, fcw)


def _adamw(lands, w, m, v, row_tile, name):
    nl = len(lands)
    _, nr, ncol = lands[0].shape
    c1 = 1.0 - ADAM_B1 ** ADAM_STEP
    c2 = 1.0 - ADAM_B2 ** ADAM_STEP

    def body(*refs):
        land_refs = refs[:nl]
        w_ref, m_ref, v_ref, g_ref, d_ref, mo_ref, vo_ref = refs[nl:]
        for l in range(nl):
            @pl.when(pl.program_id(0) == l)
            def _(l=l):
                g = land_refs[l][0].astype(F32)
                for j in range(1, N_DEV):
                    g = g + land_refs[l][j].astype(F32)
                g_ref[...] = g

        g = g_ref[...]
        m2 = ADAM_B1 * m_ref[...] + (1.0 - ADAM_B1) * g
        v2 = ADAM_B2 * v_ref[...] + (1.0 - ADAM_B2) * (g * g)
        mo_ref[...] = m2
        vo_ref[...] = v2
        d_ref[...] = -ADAM_LR * ((m2 / c1) / (jnp.sqrt(v2 / c2) + ADAM_EPS) + ADAM_WD * w_ref[...])

    def land_spec(l):
        return pl.BlockSpec((N_DEV, row_tile, ncol), lambda k, i: (0, jnp.where(k == l, i, 0), 0))

    tile = pl.BlockSpec((None, row_tile, ncol), lambda k, i: (k, i, 0))
    return pl.pallas_call(
        body, name=name, grid=(nl, nr // row_tile),
        in_specs=[land_spec(l) for l in range(nl)] + [tile, tile, tile],
        out_specs=[tile] * 4,
        out_shape=[jax.ShapeDtypeStruct(w.shape, F32)] * 4,
        compiler_params=_params(("arbitrary", "arbitrary")),
    )(*lands, w, m, v)


class _Item:
    def __init__(self, src, chunked, land_cols=False):
        self.src, self.chunked, self.land_cols = src, chunked, land_cols
        if chunked == "cols":
            block = (src.shape[0], src.shape[1] // N_DEV)
        else:
            block = src.shape[1:] if chunked else src.shape
        self.width = block[-1]
        self.land_shape = (block[0], N_DEV * block[1]) if land_cols else (N_DEV,) + block

    def _cols(self, first, count=1):
        return pl.ds(pl.multiple_of(first * self.width, LANES), count * self.width)

    def part(self, src_ref, j):
        if self.chunked == "cols":
            return src_ref.at[:, self._cols(j)]
        return src_ref.at[j] if self.chunked else src_ref

    def slot(self, land_ref, s):
        return land_ref.at[:, self._cols(s)] if self.land_cols else land_ref.at[s]

    def seven(self, land_ref):
        return land_ref.at[:, self._cols(0, N_DEV - 1)] if self.land_cols else land_ref.at[pl.ds(0, N_DEV - 1)]


def _mesh_place():
    x, y, c = lax.axis_index("x"), lax.axis_index("y"), lax.axis_index("c")
    return x, y, c, 4 * x + 2 * y + c


def _flipped(x, y, c, k):
    px = 1 - x if k & 4 else x
    py = 1 - y if k & 2 else y
    pc = 1 - c if k & 1 else c
    return (px, py, pc), 4 * px + 2 * py + pc


PEER_ORDER = (2, 4, 6, 3, 5, 7, 1)


def _exchange(items, name):
    n = len(items)

    def body(*refs):
        srcs, lands = refs[:n], refs[n:2 * n]
        send, recv, local = refs[2 * n:]
        x, y, c, me = _mesh_place()

        def copy(i, k, chunk, slot, dev):
            return pltpu.make_async_remote_copy(
                src_ref=items[i].part(srcs[i], chunk), dst_ref=items[i].slot(lands[i], slot),
                send_sem=send.at[i, k - 1], recv_sem=recv.at[i, k - 1], device_id=dev, device_id_type=MESH)

        own = [pltpu.make_async_copy(items[i].part(srcs[i], me), items[i].slot(lands[i], me), local.at[i])
               for i in range(n)]
        for k in PEER_ORDER:
            dev, idx = _flipped(x, y, c, k)
            for i in range(n):
                copy(i, k, idx, me, dev).start()
        for cp in own:
            cp.start()
        for k in PEER_ORDER:
            dev, idx = _flipped(x, y, c, k)
            for i in range(n):
                copy(i, k, me, idx, dev).wait_recv()
        for k in PEER_ORDER:
            dev, idx = _flipped(x, y, c, k)
            for i in range(n):
                copy(i, k, idx, me, dev).wait_send()
        for cp in own:
            cp.wait()

    hbm = pl.BlockSpec(memory_space=pl.ANY)
    return pl.pallas_call(
        body, name=name,
        in_specs=[hbm] * n, out_specs=[hbm] * n,
        out_shape=[jax.ShapeDtypeStruct(it.land_shape, it.src.dtype) for it in items],
        scratch_shapes=[pltpu.SemaphoreType.DMA((n, N_DEV - 1)), pltpu.SemaphoreType.DMA((n, N_DEV - 1)),
                        pltpu.SemaphoreType.DMA((n,))],
        compiler_params=pltpu.CompilerParams(has_side_effects=True),
    )(*[it.src for it in items])


def _sequencer_exchange(items, name, collective_id):
    n = len(items)

    def body(*refs):
        srcs, lands = refs[:n], refs[n:2 * n]
        send, recv, local = refs[2 * n:]
        x, y, c, me = _mesh_place()
        barrier = pltpu.get_barrier_semaphore()
        for k in PEER_ORDER:
            pl.semaphore_signal(barrier, inc=1, device_id=_flipped(x, y, c, k)[0], device_id_type=MESH)
        pl.semaphore_wait(barrier, N_DEV - 1)

        def copy(i, k, chunk, slot, dev):
            return pltpu.make_async_remote_copy(
                src_ref=items[i].part(srcs[i], chunk), dst_ref=items[i].slot(lands[i], slot),
                send_sem=send.at[i, k - 1], recv_sem=recv.at[i, k - 1], device_id=dev, device_id_type=MESH)

        own = [pltpu.make_async_copy(items[i].part(srcs[i], me), items[i].slot(lands[i], me), local.at[i])
               for i in range(n)]
        for cp in own:
            cp.start()
        for k in PEER_ORDER:
            dev, idx = _flipped(x, y, c, k)
            for i in range(n):
                copy(i, k, idx, me, dev).start()
        for k in PEER_ORDER:
            dev, idx = _flipped(x, y, c, k)
            for i in range(n):
                copy(i, k, me, idx, dev).wait_recv()
        for k in PEER_ORDER:
            dev, idx = _flipped(x, y, c, k)
            for i in range(n):
                copy(i, k, idx, me, dev).wait_send()
        for cp in own:
            cp.wait()

    return pl.kernel(
        body, name=name,
        out_type=[jax.ShapeDtypeStruct(it.land_shape, it.src.dtype) for it in items],
        mesh=plsc.ScalarSubcoreMesh(axis_name="sequencer", num_cores=1),
        scratch_types=[pltpu.SemaphoreType.DMA((n, N_DEV - 1)), pltpu.SemaphoreType.DMA((n, N_DEV - 1)),
                       pltpu.SemaphoreType.DMA((n,))],
        compiler_params=pltpu.CompilerParams(collective_id=collective_id),
    )(*[it.src for it in items])


HBM_SPEC = pl.BlockSpec(memory_space=pltpu.HBM)
SEM_SPEC = pl.BlockSpec(memory_space=pltpu.SEMAPHORE)
DATAFLOW = pltpu.SideEffectType.DATAFLOW_SIDE_EFFECTING


def _exchange_start(items, name, after=()):
    n = len(items)
    na = len(after)

    def body(*refs):
        srcs, land_ins = refs[:n], refs[n:2 * n]
        outs = refs[2 * n + na:6 * n + na]
        (local,) = refs[6 * n + na:]
        del land_ins
        x, y, c, me = _mesh_place()
        own = [pltpu.make_async_copy(items[i].part(srcs[i], me), items[i].slot(outs[4 * i + 3], me), local.at[i])
               for i in range(n)]
        for cp in own:
            cp.start()
        for cp in own:
            cp.wait()
        for k in PEER_ORDER:
            dev, idx = _flipped(x, y, c, k)
            for i in range(n):
                send, recv, _, land = outs[4 * i:4 * i + 4]
                pltpu.make_async_remote_copy(
                    src_ref=items[i].part(srcs[i], idx), dst_ref=items[i].slot(land, me), send_sem=send, recv_sem=recv,
                    device_id=dev, device_id_type=MESH).start()

    out_shape, out_specs, args, lands = [], [], [], []
    for it in items:
        out_shape += [pltpu.SemaphoreType.DMA(()), pltpu.SemaphoreType.DMA(()),
                      pltpu.HBM(it.src.shape, it.src.dtype), pltpu.HBM(it.land_shape, it.src.dtype)]
        out_specs += [SEM_SPEC, SEM_SPEC, HBM_SPEC, HBM_SPEC]
        args.append(pltpu.with_memory_space_constraint(it.src, pltpu.HBM))
        lands.append(pltpu.with_memory_space_constraint(lax.empty(it.land_shape, it.src.dtype), pltpu.HBM))
    outs = pl.pallas_call(
        body, name=name,
        in_specs=[HBM_SPEC] * (2 * n) + [pl.BlockSpec(memory_space=pl.ANY)] * na,
        out_specs=out_specs, out_shape=out_shape,
        scratch_shapes=[pltpu.SemaphoreType.DMA((n,))],
        input_output_aliases={**{i: 4 * i + 2 for i in range(n)}, **{n + i: 4 * i + 3 for i in range(n)}},
        compiler_params=pltpu.CompilerParams(has_side_effects=DATAFLOW),
    )(*args, *lands, *after)
    return [tuple(outs[4 * i:4 * i + 4]) + (items[i],) for i in range(n)]


def _started(handles):
    return handles[0][2]


def _exchange_wait(handles, after, name):
    n = len(handles)

    def body(*refs):
        x, y, c, _ = _mesh_place()
        for i in range(n):
            src, land, send, recv = refs[4 * i:4 * i + 4]
            del src
            seven = handles[i][4].seven(land)
            cp = pltpu.make_async_remote_copy(src_ref=seven, dst_ref=seven, send_sem=send, recv_sem=recv,
                                              device_id=(x, y, 1 - c), device_id_type=MESH)
            cp.wait_send()
            cp.wait_recv()

    args, in_specs, out_shape = [], [], []
    for send, recv, src, land, _ in handles:
        args += [src, land, send, recv]
        in_specs += [HBM_SPEC, HBM_SPEC, SEM_SPEC, SEM_SPEC]
        out_shape += [pltpu.HBM(src.shape, src.dtype), pltpu.HBM(land.shape, land.dtype)]
    outs = pl.pallas_call(
        body, name=name,
        in_specs=in_specs + [pl.BlockSpec(memory_space=pl.ANY)] * len(after), out_specs=[HBM_SPEC] * (2 * n),
        out_shape=out_shape,
        input_output_aliases={**{4 * i: 2 * i for i in range(n)}, **{4 * i + 1: 2 * i + 1 for i in range(n)}},
        compiler_params=pltpu.CompilerParams(has_side_effects=DATAFLOW),
    )(*args, *after)
    return [outs[2 * i + 1] for i in range(n)]


TM = 1024
TM_ACC = 512
TN_IN = 768


def kernel(x, norm1_g, w_in, mix_conv_w, attn_out_g, conv_out_g, w_out, norm2_g, ffn_up, ffn_conv_w, ffn_down, final_norm_g, loss_target, m_norm1_g, m_w_in, m_mix_conv_w, m_attn_out_g, m_conv_out_g, m_w_out, m_norm2_g, m_ffn_up, m_ffn_conv_w, m_ffn_down, m_final_norm_g, v_norm1_g, v_w_in, v_mix_conv_w, v_attn_out_g, v_conv_out_g, v_w_out, v_norm2_g, v_ffn_up, v_ffn_conv_w, v_ffn_down, v_final_norm_g):
    nbatch, seq, d = x.shape
    t = nbatch * seq
    nt, nta = t // TM, t // TM_ACC
    out_rows = D_MODEL // N_DEV
    down_rows = D_FF // N_DEV
    xf = x.reshape(t, d)
    target = loss_target.reshape(t, d)

    cw_local = jnp.concatenate([ffn_conv_w, mix_conv_w], axis=-1)
    cast = lambda w: _Item(w.astype(BF16), False)
    cast_in = lambda w: _Item(w.astype(BF16), False, land_cols=True)
    cw_all, win0 = _sequencer_exchange([_Item(cw_local, False), cast_in(w_in[0])], "gather_a", 0)
    wout0, wup0 = _sequencer_exchange([cast(w_out[0]), cast(ffn_up[0])], "gather_b", 1)
    wdown0, win1, wout1 = _sequencer_exchange([cast(ffn_down[0]), cast_in(w_in[1]), cast(w_out[1])], "gather_c", 2)
    wup1, wdown1 = _sequencer_exchange([cast(ffn_up[1]), cast(ffn_down[1])], "gather_d", 3)
    win, wup = [win0, win1], [wup0, wup1]
    wout = [w.reshape(D_MODEL, D_MODEL) for w in (wout0, wout1)]
    wdown = [w.reshape(N_UP_PAIRS, UP_CHUNK, D_MODEL) for w in (wdown0, wdown1)]
    fcw = [cw_all[:, k, :, :UP_CHUNK].reshape(2, N_UP_PAIRS, 3, UP_CHUNK) for k in range(DEPTH)]
    mcw = [cw_all[:, k, :, UP_CHUNK:].transpose(1, 0, 2).reshape(3, D_CONV) for k in range(DEPTH)]

    full = lambda i, j, k: (0, 0)

    saved = []
    xin = xf
    for l in range(DEPTH):
        h1 = _rms_fwd(xin, norm1_g[l][None], f"rms1_fwd_{l}")
        proj = _matmul(
            h1, win[l], grid=(nt, D_IN // TN_IN, 1), dims=NN, name=f"proj_{l}",
            a_spec=pl.BlockSpec((TM, D_MODEL), lambda i, j, k: (i, 0)),
            b_spec=pl.BlockSpec((D_MODEL, TN_IN), lambda i, j, k: (0, j)),
            o_spec=pl.BlockSpec((TM, TN_IN), lambda i, j, k: (i, j)), o_shape=(t, D_IN), o_dtype=F32)
        o, lse, cat = _attn_fwd(proj, attn_out_g[l][None], nbatch, seq)
        cat = _convmix_fwd(proj, cat, mcw[l], conv_out_g[l][None], nbatch, seq)
        xmid = _matmul(
            cat, wout[l], grid=(nta, 1, 1), dims=NN, name=f"mix_out_{l}",
            a_spec=pl.BlockSpec((TM_ACC, D_MODEL), lambda i, j, k: (i, 0)),
            b_spec=pl.BlockSpec((D_MODEL, D_MODEL), full),
            o_spec=pl.BlockSpec((TM_ACC, D_MODEL), lambda i, j, k: (i, 0)), o_shape=(t, D_MODEL), o_dtype=F32,
            res=xin, res_spec=pl.BlockSpec((TM_ACC, D_MODEL), lambda i, j, k: (i, 0)))
        h2 = _rms_fwd(xmid, norm2_g[l][None], f"rms2_fwd_{l}")
        pre = _matmul(
            h2, wup[l], grid=(nt, N_DEV, 1), dims=NN, name=f"ffn_up_{l}",
            a_spec=pl.BlockSpec((TM, D_MODEL), lambda i, j, k: (i, 0)),
            b_spec=pl.BlockSpec((None, D_MODEL, UP_CHUNK), lambda i, j, k: (j, 0, 0)),
            o_spec=pl.BlockSpec((None, TM, UP_CHUNK), lambda i, j, k: (j, i, 0)),
            o_shape=(N_DEV, t, UP_CHUNK), o_dtype=BF16).reshape(2, N_UP_PAIRS, t, UP_CHUNK)
        act = _ffn_act_fwd(pre, fcw[l], nbatch, seq)
        xout = _matmul(
            act, wdown[l], grid=(nta, 1, 1), dims=NN, name=f"ffn_down_{l}",
            a_spec=pl.BlockSpec((N_UP_PAIRS, TM_ACC, UP_CHUNK), lambda i, j, k: (0, i, 0)),
            b_spec=pl.BlockSpec((N_UP_PAIRS, UP_CHUNK, D_MODEL), lambda i, j, k: (0, 0, 0)),
            o_spec=pl.BlockSpec((TM_ACC, D_MODEL), lambda i, j, k: (i, 0)), o_shape=(t, D_MODEL), o_dtype=F32,
            res=xmid, res_spec=pl.BlockSpec((TM_ACC, D_MODEL), lambda i, j, k: (i, 0)))
        saved.append((xin, h1, proj, o, lse, cat, xmid, h2, pre, act))
        xin = xout

    loss_part, dx, dxb, dgf = _loss_head(xin, final_norm_g[None], target, "loss_head")

    dg1, dg2, dga, dgc = [None] * DEPTH, [None] * DEPTH, [None] * DEPTH, [None] * DEPTH
    for l in reversed(range(DEPTH)):
        xin, h1, proj, o, lse, cat, xmid, h2, pre, act = saved[l]
        d_act = _matmul(
            dxb, wdown[l], grid=(nt, N_UP_PAIRS, 1), dims=NT, name=f"d_act_{l}",
            a_spec=pl.BlockSpec((TM, D_MODEL), lambda i, j, k: (i, 0)),
            b_spec=pl.BlockSpec((None, UP_CHUNK, D_MODEL), lambda i, j, k: (j, 0, 0)),
            o_spec=pl.BlockSpec((None, TM, UP_CHUNK), lambda i, j, k: (j, i, 0)),
            o_shape=(N_UP_PAIRS, t, UP_CHUNK), o_dtype=BF16)
        g_down = _matmul(
            act, dxb, grid=(N_UP_PAIRS, 1, 1), dims=TN, name=f"g_down_{l}",
            a_spec=pl.BlockSpec((None, t, UP_CHUNK), lambda i, j, k: (i, 0, 0)),
            b_spec=pl.BlockSpec((t, D_MODEL), full),
            o_spec=pl.BlockSpec((None, UP_CHUNK, D_MODEL), lambda i, j, k: (i, 0, 0)),
            o_shape=(N_UP_PAIRS, UP_CHUNK, D_MODEL), o_dtype=BF16).reshape(N_DEV, down_rows, D_MODEL)
        d_pre, d_fcw = _ffn_act_bwd(pre, d_act, fcw[l], nbatch, seq)
        d_pre = d_pre.reshape(N_DEV, t, UP_CHUNK)
        dh2 = _matmul(
            d_pre, wup[l], grid=(nta, 1, 1), dims=NT, name=f"d_h2_{l}",
            a_spec=pl.BlockSpec((N_DEV, TM_ACC, UP_CHUNK), lambda i, j, k: (0, i, 0)),
            b_spec=pl.BlockSpec((N_DEV, D_MODEL, UP_CHUNK), lambda i, j, k: (0, 0, 0)),
            o_spec=pl.BlockSpec((TM_ACC, D_MODEL), lambda i, j, k: (i, 0)), o_shape=(t, D_MODEL), o_dtype=F32)
        g_up = _matmul(
            h2, d_pre, grid=(1, N_DEV, 1), dims=TN, name=f"g_up_{l}",
            a_spec=pl.BlockSpec((t, D_MODEL), full),
            b_spec=pl.BlockSpec((None, t, UP_CHUNK), lambda i, j, k: (j, 0, 0)),
            o_spec=pl.BlockSpec((None, D_MODEL, UP_CHUNK), lambda i, j, k: (j, 0, 0)),
            o_shape=(N_DEV, D_MODEL, UP_CHUNK), o_dtype=BF16)
        dxm, dxmb, dg2[l] = _rms_bwd(xmid, norm2_g[l][None], dh2, dx, f"rms2_bwd_{l}")
        g_out = _matmul(
            cat, dxmb, grid=(1, 1, nt), dims=TN, name=f"g_out_{l}",
            a_spec=pl.BlockSpec((TM, D_MODEL), lambda i, j, k: (k, 0)),
            b_spec=pl.BlockSpec((TM, D_MODEL), lambda i, j, k: (k, 0)),
            o_spec=pl.BlockSpec((D_MODEL, D_MODEL), full),
            o_shape=(D_MODEL, D_MODEL), o_dtype=BF16).reshape(N_DEV, out_rows, D_MODEL)
        if l == 0:
            land_out0, land_up0, land_down0 = _sequencer_exchange(
                [_Item(g_out, True), _Item(g_up, True), _Item(g_down, True)], "scatter_0a", 5)
        d_cat = _matmul(
            dxmb, wout[l], grid=(nta, 1, 1), dims=NT, name=f"d_cat_{l}",
            a_spec=pl.BlockSpec((TM_ACC, D_MODEL), lambda i, j, k: (i, 0)),
            b_spec=pl.BlockSpec((D_MODEL, D_MODEL), full),
            o_spec=pl.BlockSpec((TM_ACC, D_MODEL), lambda i, j, k: (i, 0)), o_shape=(t, D_MODEL), o_dtype=BF16)
        d_proj, dga[l] = _attn_bwd(proj, o, lse, d_cat, attn_out_g[l][None], nbatch, seq)
        d_proj, d_mcw, dgc[l] = _convmix_bwd(proj, d_cat, d_proj, mcw[l], conv_out_g[l][None], nbatch, seq)
        g_in = _matmul(
            h1, d_proj, grid=(1, D_IN // TN_IN, 1), dims=TN, name=f"g_in_{l}",
            a_spec=pl.BlockSpec((t, D_MODEL), full),
            b_spec=pl.BlockSpec((t, TN_IN), lambda i, j, k: (0, j)),
            o_spec=pl.BlockSpec((D_MODEL, TN_IN), lambda i, j, k: (0, j)),
            o_shape=(D_MODEL, D_IN), o_dtype=BF16)
        g_cw = jnp.concatenate(
            [d_fcw.reshape(N_DEV, 3, UP_CHUNK), d_mcw.reshape(3, N_DEV, D_CONV // N_DEV).transpose(1, 0, 2)], axis=-1)
        if l == 0:
            land_in0, land_cw0 = _sequencer_exchange([_Item(g_in, "cols"), _Item(g_cw, True)], "scatter_0b", 6)
        else:
            land_in1, land_out1, land_up1, land_down1, land_cw1 = _sequencer_exchange(
                [_Item(g_in, "cols"), _Item(g_out, True), _Item(g_up, True), _Item(g_down, True), _Item(g_cw, True)],
                "scatter_1", 4)
        dh1 = _matmul(
            d_proj, win[l], grid=(nta, 1, 1), dims=NT, name=f"d_h1_{l}",
            a_spec=pl.BlockSpec((TM_ACC, D_IN), lambda i, j, k: (i, 0)),
            b_spec=pl.BlockSpec((D_MODEL, D_IN), full),
            o_spec=pl.BlockSpec((TM_ACC, D_MODEL), lambda i, j, k: (i, 0)), o_shape=(t, D_MODEL), o_dtype=F32)
        dx, dxb, dg1[l] = _rms_bwd(xin, norm1_g[l][None], dh1, dxm, f"rms1_bwd_{l}")

    def pack_small(n1, a, c, n2, f):
        return jnp.concatenate(
            [n1, n2, f[None], jnp.concatenate([a, c], axis=-1), jnp.zeros((1, D_MODEL), F32)], axis=0)[None]

    res_out = _adamw([land_out0, land_out1], w_out, m_w_out, v_w_out, out_rows, "adamw_w_out")
    res_up = _adamw([land_up0, land_up1], ffn_up, m_ffn_up, v_ffn_up, 256, "adamw_ffn_up")
    res_down = _adamw([land_down0, land_down1], ffn_down, m_ffn_down, v_ffn_down, down_rows, "adamw_ffn_down")
    small = jnp.concatenate(
        [dg1[0], dg1[1], dg2[0], dg2[1], dgf,
         jnp.concatenate([dga[0], dgc[0]], axis=-1), jnp.concatenate([dga[1], dgc[1]], axis=-1),
         jnp.zeros((1, D_MODEL), F32)], axis=0)
    (land_small,) = _exchange([_Item(small, False)], "gather_gain_grads")
    res_small = _adamw(
        [land_small], pack_small(norm1_g, attn_out_g, conv_out_g, norm2_g, final_norm_g),
        pack_small(m_norm1_g, m_attn_out_g, m_conv_out_g, m_norm2_g, m_final_norm_g),
        pack_small(v_norm1_g, v_attn_out_g, v_conv_out_g, v_norm2_g, v_final_norm_g), SUBLANES, "adamw_gains")
    res_in = _adamw([land_in0, land_in1], w_in, m_w_in, v_w_in, 256, "adamw_w_in")
    res_cw = _adamw(
        [land_cw0, land_cw1], cw_local, jnp.concatenate([m_ffn_conv_w, m_mix_conv_w], axis=-1),
        jnp.concatenate([v_ffn_conv_w, v_mix_conv_w], axis=-1), 3, "adamw_conv_w")

    loss = lax.psum(loss_part[0, 0], ("x", "y", "c"))

    def unpack(kind):
        s = res_small[kind][0]
        cwr = res_cw[kind]
        return (s[0:2], res_in[kind], cwr[..., UP_CHUNK:], s[5:7, :D_ATTN], s[5:7, D_ATTN:], res_out[kind],
                s[2:4], res_up[kind], cwr[..., :UP_CHUNK], res_down[kind], s[4])

    return (loss, dx.reshape(nbatch, seq, d), *unpack(0), *unpack(1), *unpack(2), *unpack(3))
```

```python
import math

import jax
import jax.numpy as jnp
from jax import lax
from jax.experimental import pallas as pl
from jax.experimental.pallas import tpu as pltpu
from jax.experimental.pallas import tpu_sc as plsc

F32 = jnp.float32
BF16 = jnp.bfloat16

D_MODEL = 1024
D_ATTN = 512
D_CONV = 512
HEAD_DIM = 64
N_HEADS = 8
D_FF = 2816
DEPTH = 2
D_IN = 3 * D_ATTN + 3 * D_CONV
EPS = 1e-6
DILATIONS = (1, 4, 16)
BAND = 128
N_DEV = 8
IN_CHUNK = D_IN // N_DEV
UP_CHUNK = 2 * D_FF // N_DEV
N_UP_PAIRS = N_DEV // 2
CW_PACK = UP_CHUNK + D_CONV // N_DEV
ADAM_LR = 0.001
ADAM_B1 = 0.9
ADAM_B2 = 0.999
ADAM_EPS = 1e-08
ADAM_WD = 0.01
ADAM_STEP = 10
LANES = 128
SUBLANES = 8
VMEM_LIMIT = 56 * 1024 * 1024

NEG = -1e30
MESH = pl.DeviceIdType.MESH


def _params(sem=None, vmem=VMEM_LIMIT):
    return pltpu.CompilerParams(dimension_semantics=sem, vmem_limit_bytes=vmem)


NN = (((1,), (0,)), ((), ()))
NT = (((1,), (1,)), ((), ()))
TN = (((0,), (0,)), ((), ()))


def _matmul(a, b, *, grid, a_spec, b_spec, o_spec, o_shape, o_dtype, dims, name, res=None, res_spec=None, after=()):
    nk = grid[2]
    o_block = tuple(s for s in o_spec.block_shape if s is not None)
    na = len(after)

    def body(*refs):
        refs = refs[:2 + (res is not None)] + refs[2 + (res is not None) + na:]
        if res is None:
            a_ref, b_ref, o_ref, *scr = refs
            r_ref = None
        else:
            a_ref, b_ref, r_ref, o_ref, *scr = refs
        def dot(av, bv):
            return lax.dot_general(av.astype(BF16), bv.astype(BF16), dims, preferred_element_type=F32)

        if len(a_ref.shape) == 3:
            part = dot(a_ref[0], b_ref[0])
            for c in range(1, a_ref.shape[0]):
                part = part + dot(a_ref[c], b_ref[c])
        else:
            part = dot(a_ref[...], b_ref[...])

        def finish(total):
            if r_ref is not None:
                total = total + r_ref[...]
            o_ref[...] = total.astype(o_dtype)

        if nk == 1:
            finish(part)
        else:
            acc = scr[0]
            k = pl.program_id(2)

            @pl.when(k == 0)
            def _():
                acc[...] = part

            @pl.when(k > 0)
            def _():
                acc[...] += part

            @pl.when(k == nk - 1)
            def _():
                finish(acc[...])

    in_specs = [a_spec, b_spec] + ([res_spec] if res is not None else []) + [pl.BlockSpec(memory_space=pl.ANY)] * na
    args = (a, b) + ((res,) if res is not None else ()) + tuple(after)
    return pl.pallas_call(
        body, name=name, grid=grid, in_specs=in_specs, out_specs=o_spec,
        out_shape=jax.ShapeDtypeStruct(o_shape, o_dtype),
        scratch_shapes=[pltpu.VMEM(o_block, F32)] if nk > 1 else [],
        compiler_params=_params(("parallel", "parallel", "arbitrary")),
    )(*args)


ROW_TILE = 512


def _rms_fwd(x, g, name):
    t, d = x.shape

    def body(x_ref, g_ref, h_ref):
        xv = x_ref[...]
        r = lax.rsqrt(jnp.mean(xv * xv, axis=-1, keepdims=True) + EPS)
        h_ref[...] = (xv * r * g_ref[...]).astype(BF16)

    return pl.pallas_call(
        body, name=name, grid=(t // ROW_TILE,),
        in_specs=[pl.BlockSpec((ROW_TILE, d), lambda i: (i, 0)), pl.BlockSpec((1, d), lambda i: (0, 0))],
        out_specs=pl.BlockSpec((ROW_TILE, d), lambda i: (i, 0)),
        out_shape=jax.ShapeDtypeStruct((t, d), BF16),
        compiler_params=_params(("parallel",)),
    )(x, g)


def _rms_bwd(x, g, dh, dres, name):
    t, d = x.shape

    def body(x_ref, g_ref, dh_ref, dres_ref, dx_ref, dxb_ref, dg_ref):
        xv = x_ref[...]
        r = lax.rsqrt(jnp.mean(xv * xv, axis=-1, keepdims=True) + EPS)
        xh = xv * r
        dhv = dh_ref[...]
        gd = dhv * g_ref[...]
        dx = r * (gd - xh * jnp.mean(gd * xh, axis=-1, keepdims=True)) + dres_ref[...]
        dx_ref[...] = dx
        dxb_ref[...] = dx.astype(BF16)
        part = jnp.sum(dhv * xh, axis=0, keepdims=True)

        @pl.when(pl.program_id(0) == 0)
        def _():
            dg_ref[...] = part

        @pl.when(pl.program_id(0) > 0)
        def _():
            dg_ref[...] += part

    row = pl.BlockSpec((ROW_TILE, d), lambda i: (i, 0))
    vec = pl.BlockSpec((1, d), lambda i: (0, 0))
    return pl.pallas_call(
        body, name=name, grid=(t // ROW_TILE,),
        in_specs=[row, vec, row, row], out_specs=[row, row, vec],
        out_shape=[jax.ShapeDtypeStruct((t, d), F32), jax.ShapeDtypeStruct((t, d), BF16),
                   jax.ShapeDtypeStruct((1, d), F32)],
        compiler_params=_params(("arbitrary",)),
    )(x, g, dh, dres)


def _loss_head(x, g, target, name):
    t, d = x.shape

    def body(x_ref, g_ref, t_ref, loss_ref, dx_ref, dxb_ref, dg_ref):
        xv = x_ref[...]
        r = lax.rsqrt(jnp.mean(xv * xv, axis=-1, keepdims=True) + EPS)
        xh = xv * r
        gv = g_ref[...]
        err = xh * gv - t_ref[...]
        loss = jnp.full((1, LANES), 0.5 / d, F32) * jnp.sum(err * err)
        dy = err * (1.0 / d)
        gd = dy * gv
        dx = r * (gd - xh * jnp.mean(gd * xh, axis=-1, keepdims=True))
        dx_ref[...] = dx
        dxb_ref[...] = dx.astype(BF16)
        part = jnp.sum(dy * xh, axis=0, keepdims=True)

        @pl.when(pl.program_id(0) == 0)
        def _():
            dg_ref[...] = part
            loss_ref[...] = loss

        @pl.when(pl.program_id(0) > 0)
        def _():
            dg_ref[...] += part
            loss_ref[...] += loss

    row = pl.BlockSpec((ROW_TILE, d), lambda i: (i, 0))
    vec = pl.BlockSpec((1, d), lambda i: (0, 0))
    return pl.pallas_call(
        body, name=name, grid=(t // ROW_TILE,),
        in_specs=[row, vec, row],
        out_specs=[pl.BlockSpec((1, LANES), lambda i: (0, 0)), row, row, vec],
        out_shape=[jax.ShapeDtypeStruct((1, LANES), F32), jax.ShapeDtypeStruct((t, d), F32),
                   jax.ShapeDtypeStruct((t, d), BF16), jax.ShapeDtypeStruct((1, d), F32)],
        compiler_params=_params(("arbitrary",)),
    )(x, g, target)


def _group_matrix(n):
    shift = int(math.log2(HEAD_DIM))
    r = lax.broadcasted_iota(jnp.int32, (n, n), 0) >> shift
    c = lax.broadcasted_iota(jnp.int32, (n, n), 1) >> shift
    return (r == c).astype(BF16)


def _group_sum(v, gmat):
    hi = v.astype(BF16)
    lo = (v - hi.astype(F32)).astype(BF16)

    def dot(p):
        return jnp.dot(p, gmat, preferred_element_type=F32)

    return dot(hi) + dot(lo)


def _shift_rows(ext, k):
    return pltpu.roll(ext, k % ext.shape[0], 0)


def _store_columns(stage, out_hbm, sems, row0, nrows, col_blocks):
    rows = pl.ds(pl.multiple_of(row0, SUBLANES * 2), nrows)
    copies = [
        pltpu.make_async_copy(stage.at[i], out_hbm.at[rows, pl.ds(pl.multiple_of(cb * LANES, LANES), LANES)], sems.at[i])
        for i, cb in enumerate(col_blocks)
    ]
    for cp in copies:
        cp.start()
    for cp in copies:
        cp.wait()


def _attn_consts(width):
    i = lax.broadcasted_iota(jnp.int32, (BAND, width), 0)
    j = lax.broadcasted_iota(jnp.int32, (BAND, width), 1)
    dist = (width - BAND) + i - j
    inwin = (dist >= 0) & (dist <= BAND)
    return dist.astype(F32), inwin, j


def _head_masks():
    lane = lax.broadcasted_iota(jnp.int32, (1, LANES), 1)
    return [(lane < HEAD_DIM).astype(F32), (lane >= HEAD_DIM).astype(F32)]


def _pair_bias(slope, dil):
    distf, inwin, _ = _attn_consts(2 * BAND)
    return jnp.concatenate([jnp.where(inwin, distf * (slope[hh] * (-float(dil))), NEG) for hh in range(2)], axis=0)


def _stack_heads(xv, hmask):
    return jnp.concatenate([xv * hmask[0], xv * hmask[1]], axis=0).astype(BF16)


FWD_UNROLL = 8
BWD_UNROLL = 8


def _unroll(trips, most):
    return max(u for u in range(1, most + 1) if trips % u == 0)


def _for_blocks(seq, dil, block, most):
    nb = seq // dil // BAND

    def residue(r, carry):
        base = r * nb
        block(pl.multiple_of(base * BAND, BAND), None)
        if nb > 1:
            def rest(n, c):
                block(pl.multiple_of((base + n) * BAND, BAND), pl.multiple_of((base + n - 1) * BAND, BAND))
                return c

            lax.fori_loop(1, nb, rest, 0, unroll=_unroll(nb - 1, most))
        return carry

    if dil == 1:
        residue(0, 0)
    else:
        lax.fori_loop(0, dil, residue, 0, unroll=_unroll(dil, max(1, most // nb)))


def _permute_in(src_ref, dst_ref, dil, seq):
    length = seq // dil
    for r in range(dil):
        dst_ref[pl.ds(r * length, length), :] = src_ref[pl.ds(r, length, stride=dil), :].astype(dst_ref.dtype)


def _slopes_table():
    slopes = 2.0 ** (-8.0 * jnp.arange(1, N_HEADS + 1, dtype=F32) / N_HEADS)
    return jnp.broadcast_to(slopes[:, None], (N_HEADS, 2 * BAND))


def _attn_fwd(proj, attn_g, nbatch, seq):
    t = nbatch * seq
    scale = HEAD_DIM ** -0.5

    def body(q_ref, k_ref, v_ref, g_ref, sl_ref, o_ref, lse_ref, cat_ref, pq, pk, pv, po, pm, pll, ao, am, al):
        hp = pl.program_id(1)
        hmask = _head_masks()
        slope = [sl_ref[pl.ds(2 * hp + hh, 1), :] for hh in range(2)]

        def run_branch(dil, qs, ks, vs, osink, msink, lsink):
            bias = _pair_bias(slope, dil)

            def block(row0, prow):
                cur = pl.ds(row0, BAND)
                q2 = _stack_heads(qs[cur, :] * scale, hmask)
                if prow is None:
                    kk, vv, bias_b = ks[cur, :], vs[cur, :], bias[:, BAND:]
                else:
                    prev = pl.ds(prow, BAND)
                    kk = jnp.concatenate([ks[prev, :], ks[cur, :]], axis=0)
                    vv = jnp.concatenate([vs[prev, :], vs[cur, :]], axis=0)
                    bias_b = bias
                s = lax.dot_general(q2, kk.astype(BF16), NT, preferred_element_type=F32) + bias_b
                m = jnp.max(s, axis=1, keepdims=True)
                p = jnp.exp(s - m)
                l = jnp.sum(p, axis=1, keepdims=True)
                pb = p.astype(BF16)
                o = jnp.dot(jnp.concatenate([pb[:BAND], pb[BAND:]], axis=1), _stack_heads(vv, hmask),
                            preferred_element_type=F32)
                osink[cur, :] = o
                msink[cur, :] = m[:BAND] * hmask[0] + m[BAND:] * hmask[1]
                lsink[cur, :] = l[:BAND] * hmask[0] + l[BAND:] * hmask[1]

            _for_blocks(seq, dil, block, FWD_UNROLL)

        run_branch(1, q_ref, k_ref, v_ref, ao, am, al)
        for dil in DILATIONS[1:]:
            length = seq // dil
            _permute_in(q_ref, pq, dil, seq)
            _permute_in(k_ref, pk, dil, seq)
            _permute_in(v_ref, pv, dil, seq)
            run_branch(dil, pq, pk, pv, po, pm, pll)
            for r in range(dil):
                nat = pl.ds(r, length, stride=dil)
                per = pl.ds(r * length, length)
                m0 = am[nat, :]
                mb = pm[per, :]
                mn = jnp.maximum(m0, mb)
                e0 = jnp.exp(m0 - mn)
                eb = jnp.exp(mb - mn)
                ao[nat, :] = ao[nat, :] * e0 + po[per, :] * eb
                al[nat, :] = al[nat, :] * e0 + pll[per, :] * eb
                am[nat, :] = mn

        gmat = _group_matrix(LANES)
        gv = g_ref[...]

        def fin(c, carry):
            rows = pl.ds(pl.multiple_of(c * 256, 256), 256)
            lv = al[rows, :]
            o = ao[rows, :] / lv
            o_ref[rows, :] = o
            lse_ref[rows, :] = am[rows, :] + jnp.log(lv)
            ms = _group_sum(o * o, gmat) * (1.0 / HEAD_DIM)
            cat_ref[rows, :] = (o * lax.rsqrt(ms + EPS) * gv).astype(BF16)
            return carry

        lax.fori_loop(0, seq // 256, fin, 0)

    nq = D_ATTN // LANES
    blk = lambda off: pl.BlockSpec((seq, LANES), lambda b, h: (b, h + off))
    scratch = [pltpu.VMEM((seq, LANES), F32) for _ in range(9)]
    return pl.pallas_call(
        body, name="attn_fwd", grid=(nbatch, nq),
        in_specs=[blk(0), blk(nq), blk(2 * nq), pl.BlockSpec((1, LANES), lambda b, h: (0, h)),
                  pl.BlockSpec((N_HEADS, 2 * BAND), lambda b, h: (0, 0))],
        out_specs=[blk(0), blk(0), blk(0)],
        out_shape=[jax.ShapeDtypeStruct((t, D_ATTN), F32), jax.ShapeDtypeStruct((t, D_ATTN), F32),
                   jax.ShapeDtypeStruct((t, D_MODEL), BF16)],
        scratch_shapes=scratch,
        compiler_params=_params(("parallel", "parallel")),
    )(proj, proj, proj, attn_g, _slopes_table())


def _attn_bwd(proj, o, lse, d_cat, attn_g, nbatch, seq):
    t = nbatch * seq
    scale = HEAD_DIM ** -0.5

    def body(q_ref, k_ref, v_ref, o_ref, lse_ref, dy_ref, g_ref, sl_ref, dproj_ref, dg_ref,
             do_n, dl_n, dq_n, dk_n, dv_n, pq, pk, pv, pdo, plse, pdl, pdq, pdk, pdv, stage, sems):
        hp = pl.program_id(0)
        hmask = _head_masks()
        slope = [sl_ref[pl.ds(2 * hp + hh, 1), :] for hh in range(2)]
        gmat = _group_matrix(LANES)
        gv = g_ref[...]

        def prep(c, dg):
            rows = pl.ds(pl.multiple_of(c * 256, 256), 256)
            ov = o_ref[rows, :]
            dyn = dy_ref[rows, :].astype(F32)
            r = lax.rsqrt(_group_sum(ov * ov, gmat) * (1.0 / HEAD_DIM) + EPS)
            gd = dyn * gv
            oh = ov * r
            do = r * (gd - oh * (_group_sum(gd * oh, gmat) * (1.0 / HEAD_DIM)))
            do_n[rows, :] = do
            dl_n[rows, :] = _group_sum(do * ov, gmat)
            return dg + jnp.sum(dyn * oh, axis=0, keepdims=True)

        dg = lax.fori_loop(0, seq // 256, prep, jnp.zeros((1, LANES), F32))

        @pl.when(pl.program_id(1) == 0)
        def _():
            dg_ref[...] = dg

        @pl.when(pl.program_id(1) > 0)
        def _():
            dg_ref[...] += dg

        def clear(*refs):
            def step(c, carry):
                rows = pl.ds(pl.multiple_of(c * 256, 256), 256)
                for ref in refs:
                    ref[rows, :] = jnp.zeros((256, LANES), F32)
                return carry

            lax.fori_loop(0, seq // 256, step, 0)

        clear(dq_n, dk_n, dv_n)

        def run_branch(dil, qs, ks, vs, dos, lses, dls, dqs, dks, dvs):
            bias = _pair_bias(slope, dil)

            def per_head(xv):
                return jnp.concatenate([xv[:, 0:1], xv[:, HEAD_DIM:HEAD_DIM + 1]], axis=0)

            def block(row0, prow):
                cur = pl.ds(row0, BAND)
                keys = cur if prow is None else pl.ds(prow, 2 * BAND)
                q2 = _stack_heads(qs[cur, :] * scale, hmask)
                do2 = _stack_heads(dos[cur, :], hmask)
                kk, vv = ks[keys, :], vs[keys, :]
                s = lax.dot_general(q2, kk.astype(BF16), NT, preferred_element_type=F32)
                s = s + (bias[:, BAND:] if prow is None else bias)
                p = jnp.exp(s - per_head(lses[cur, :]))
                dp = lax.dot_general(do2, vv.astype(BF16), NT, preferred_element_type=F32)
                ds = (p * (dp - per_head(dls[cur, :]))).astype(BF16)
                dqs[cur, :] += jnp.dot(jnp.concatenate([ds[:BAND], ds[BAND:]], axis=1), _stack_heads(kk, hmask),
                                       preferred_element_type=F32)
                dks[keys, :] += lax.dot_general(ds, q2, TN, preferred_element_type=F32)
                dvs[keys, :] += lax.dot_general(p.astype(BF16), do2, TN, preferred_element_type=F32)

            _for_blocks(seq, dil, block, BWD_UNROLL)

        run_branch(1, q_ref, k_ref, v_ref, do_n, lse_ref, dl_n, dq_n, dk_n, dv_n)
        for dil in DILATIONS[1:]:
            length = seq // dil
            for src, dst in ((q_ref, pq), (k_ref, pk), (v_ref, pv), (do_n, pdo), (lse_ref, plse), (dl_n, pdl)):
                _permute_in(src, dst, dil, seq)
            clear(pdq, pdk, pdv)
            run_branch(dil, pq, pk, pv, pdo, plse, pdl, pdq, pdk, pdv)
            for r in range(dil):
                nat = pl.ds(r, length, stride=dil)
                per = pl.ds(r * length, length)
                dq_n[nat, :] += pdq[per, :]
                dk_n[nat, :] += pdk[per, :]
                dv_n[nat, :] += pdv[per, :]

        def emit(c, carry):
            rows = pl.ds(pl.multiple_of(c * 256, 256), 256)
            stage[0, rows, :] = (dq_n[rows, :] * scale).astype(BF16)
            stage[1, rows, :] = dk_n[rows, :].astype(BF16)
            stage[2, rows, :] = dv_n[rows, :].astype(BF16)
            return carry

        lax.fori_loop(0, seq // 256, emit, 0)
        _store_columns(stage, dproj_ref, sems, pl.program_id(1) * seq, seq, [hp, nq + hp, 2 * nq + hp])

    nq = D_ATTN // LANES
    blk = lambda off: pl.BlockSpec((seq, LANES), lambda h, b: (b, h + off))
    vec = pl.BlockSpec((1, LANES), lambda h, b: (0, h))
    scratch = [pltpu.VMEM((seq, LANES), F32) for _ in range(14)]
    scratch += [pltpu.VMEM((3, seq, LANES), BF16), pltpu.SemaphoreType.DMA((3,))]
    d_proj, dg = pl.pallas_call(
        body, name="attn_bwd", grid=(nq, nbatch),
        in_specs=[blk(0), blk(nq), blk(2 * nq), blk(0), blk(0), blk(0), vec,
                  pl.BlockSpec((N_HEADS, 2 * BAND), lambda h, b: (0, 0))],
        out_specs=[pl.BlockSpec(memory_space=pl.ANY), vec],
        out_shape=[jax.ShapeDtypeStruct((t, D_IN), BF16), jax.ShapeDtypeStruct((1, D_ATTN), F32)],
        scratch_shapes=scratch,
        compiler_params=_params(("arbitrary", "arbitrary")),
    )(proj, proj, proj, o, lse, d_cat, attn_g, _slopes_table())
    return d_proj, dg


HALO = 2 * SUBLANES


def _window(ref, c, rows, nchunks, after):
    row0 = pl.multiple_of(c * rows, rows)
    prev0 = pl.multiple_of(jnp.maximum(row0 - HALO, 0), HALO)
    parts = [ref[pl.ds(prev0, HALO), :].astype(F32) * (c > 0).astype(F32), ref[pl.ds(row0, rows), :].astype(F32)]
    if after:
        next0 = pl.multiple_of(jnp.minimum(row0 + rows, (nchunks - 1) * rows), HALO)
        parts.append(ref[pl.ds(next0, HALO), :].astype(F32) * (c < nchunks - 1).astype(F32))
    return jnp.concatenate(parts, axis=0)


def _conv(z, w):
    return w[0:1] * _shift_rows(z, 2) + w[1:2] * _shift_rows(z, 1) + w[2:3] * z


def _conv_t(dy, w):
    return w[2:3] * dy + w[1:2] * _shift_rows(dy, -1) + w[0:1] * _shift_rows(dy, -2)


def _conv_wgrad(dy, z, cur):
    return [jnp.sum((dy * _shift_rows(z, 2 - k))[cur], axis=0, keepdims=True) for k in range(3)]


MIX_ROWS = 256
GATE_B_BLOCK = 3 * D_ATTN // LANES
GATE_C_BLOCK = GATE_B_BLOCK + D_CONV // LANES
U_BLOCK = GATE_C_BLOCK + D_CONV // LANES


def _convmix_fwd(proj, cat, mcw, conv_g, nbatch, seq):
    nchunks = seq // MIX_ROWS

    def body(gb_ref, gc_ref, u_ref, w_ref, g_ref, cat_in, cat_ref):
        del cat_in
        gmat = _group_matrix(LANES)
        w = w_ref[...]
        gv = g_ref[...]

        def step(c, carry):
            cur = pl.ds(pl.multiple_of(c * MIX_ROWS, MIX_ROWS), MIX_ROWS)
            z = _window(gc_ref, c, MIX_ROWS, nchunks, False) * _window(u_ref, c, MIX_ROWS, nchunks, False)
            y = gb_ref[cur, :] * _conv(z, w)[HALO:]
            ms = _group_sum(y * y, gmat) * (1.0 / HEAD_DIM)
            cat_ref[cur, :] = (y * lax.rsqrt(ms + EPS) * gv).astype(BF16)
            return carry

        lax.fori_loop(0, nchunks, step, 0)

    nc = D_CONV // LANES
    blk = lambda off: pl.BlockSpec((seq, LANES), lambda b, j: (b, j + off))
    return pl.pallas_call(
        body, name="convmix_fwd", grid=(nbatch, nc),
        in_specs=[blk(GATE_B_BLOCK), blk(GATE_C_BLOCK), blk(U_BLOCK),
                  pl.BlockSpec((3, LANES), lambda b, j: (0, j)), pl.BlockSpec((1, LANES), lambda b, j: (0, j)),
                  pl.BlockSpec(memory_space=pl.ANY)],
        out_specs=blk(D_ATTN // LANES),
        out_shape=jax.ShapeDtypeStruct(cat.shape, cat.dtype),
        input_output_aliases={5: 0},
        compiler_params=_params(("parallel", "parallel")),
    )(proj, proj, proj, mcw, conv_g, cat)


def _convmix_bwd(proj, d_cat, d_proj, mcw, conv_g, nbatch, seq):
    nchunks = seq // MIX_ROWS

    def body(gb_ref, gc_ref, u_ref, dy_ref, w_ref, g_ref, dproj_in, dproj_ref, dw_ref, dg_ref, stage, sems):
        del dproj_in
        cb = pl.program_id(0)
        b = pl.program_id(1)
        gmat = _group_matrix(LANES)
        w = w_ref[...]
        gv = g_ref[...]
        cur = slice(HALO, HALO + MIX_ROWS)

        def step(c, carry):
            rows = pl.ds(pl.multiple_of(c * MIX_ROWS, MIX_ROWS), MIX_ROWS)
            gb = _window(gb_ref, c, MIX_ROWS, nchunks, True)
            gc = _window(gc_ref, c, MIX_ROWS, nchunks, True)
            u = _window(u_ref, c, MIX_ROWS, nchunks, True)
            dyn = _window(dy_ref, c, MIX_ROWS, nchunks, True)
            z = gc * u
            conv = _conv(z, w)
            y = gb * conv
            r = lax.rsqrt(_group_sum(y * y, gmat) * (1.0 / HEAD_DIM) + EPS)
            yh = y * r
            gd = dyn * gv
            dy = r * (gd - yh * (_group_sum(gd * yh, gmat) * (1.0 / HEAD_DIM)))
            dc = dy * gb
            dz = _conv_t(dc, w)
            stage[0, rows, :] = (dy * conv)[cur].astype(BF16)
            stage[1, rows, :] = (dz * u)[cur].astype(BF16)
            stage[2, rows, :] = (dz * gc)[cur].astype(BF16)
            dws = _conv_wgrad(dc, z, cur)
            dg = jnp.sum((dyn * yh)[cur], axis=0, keepdims=True)
            return tuple(a + d for a, d in zip(carry, dws + [dg]))

        zero = jnp.zeros((1, LANES), F32)
        dw0, dw1, dw2, dg = lax.fori_loop(0, nchunks, step, (zero, zero, zero, zero))

        @pl.when(b == 0)
        def _():
            dw_ref[0:1, :] = dw0
            dw_ref[1:2, :] = dw1
            dw_ref[2:3, :] = dw2
            dg_ref[...] = dg

        @pl.when(b > 0)
        def _():
            dw_ref[0:1, :] += dw0
            dw_ref[1:2, :] += dw1
            dw_ref[2:3, :] += dw2
            dg_ref[...] += dg

        _store_columns(stage, dproj_ref, sems, b * seq, seq, [GATE_B_BLOCK + cb, GATE_C_BLOCK + cb, U_BLOCK + cb])

    nc = D_CONV // LANES
    blk = lambda off: pl.BlockSpec((seq, LANES), lambda j, b: (b, j + off))
    return pl.pallas_call(
        body, name="convmix_bwd", grid=(nc, nbatch),
        in_specs=[blk(GATE_B_BLOCK), blk(GATE_C_BLOCK), blk(U_BLOCK), blk(D_ATTN // LANES),
                  pl.BlockSpec((3, LANES), lambda j, b: (0, j)), pl.BlockSpec((1, LANES), lambda j, b: (0, j)),
                  pl.BlockSpec(memory_space=pl.ANY)],
        out_specs=[pl.BlockSpec(memory_space=pl.ANY), pl.BlockSpec((3, LANES), lambda j, b: (0, j)),
                   pl.BlockSpec((1, LANES), lambda j, b: (0, j))],
        out_shape=[jax.ShapeDtypeStruct(d_proj.shape, d_proj.dtype), jax.ShapeDtypeStruct((3, D_CONV), F32),
                   jax.ShapeDtypeStruct((1, D_CONV), F32)],
        scratch_shapes=[pltpu.VMEM((3, seq, LANES), BF16), pltpu.SemaphoreType.DMA((3,))],
        input_output_aliases={6: 0},
        compiler_params=_params(("arbitrary", "arbitrary")),
    )(proj, proj, proj, d_cat, mcw, conv_g, d_proj)


FFN_ROWS = 128


def _ffn_act_fwd(pre, fcw, nbatch, seq):
    t = nbatch * seq
    nchunks = seq // FFN_ROWS

    def body(pre_ref, w_ref, act_ref):
        wa = w_ref[0]
        wc = w_ref[1]

        def step(c, carry):
            cur = pl.ds(pl.multiple_of(c * FFN_ROWS, FFN_ROWS), FFN_ROWS)
            a = _conv(_window(pre_ref.at[0], c, FFN_ROWS, nchunks, False), wa)[HALO:]
            v = _conv(_window(pre_ref.at[1], c, FFN_ROWS, nchunks, False), wc)[HALO:]
            act_ref[cur, :] = (a * jax.nn.sigmoid(a) * v).astype(BF16)
            return carry

        lax.fori_loop(0, nchunks, step, 0)

    return pl.pallas_call(
        body, name="ffn_act_fwd", grid=(N_UP_PAIRS, nbatch),
        in_specs=[pl.BlockSpec((2, None, seq, UP_CHUNK), lambda i, b: (0, i, b, 0)),
                  pl.BlockSpec((2, None, 3, UP_CHUNK), lambda i, b: (0, i, 0, 0))],
        out_specs=pl.BlockSpec((None, seq, UP_CHUNK), lambda i, b: (i, b, 0)),
        out_shape=jax.ShapeDtypeStruct((N_UP_PAIRS, t, UP_CHUNK), BF16),
        compiler_params=_params(("parallel", "parallel")),
    )(pre, fcw)


def _ffn_act_bwd(pre, d_act, fcw, nbatch, seq):
    nchunks = seq // FFN_ROWS

    def body(pre_ref, da_ref, w_ref, dpre_ref, dw_ref):
        b = pl.program_id(1)
        wa = w_ref[0]
        wc = w_ref[1]
        cur = slice(HALO, HALO + FFN_ROWS)

        def step(c, carry):
            rows = pl.ds(pl.multiple_of(c * FFN_ROWS, FFN_ROWS), FFN_ROWS)
            pg = _window(pre_ref.at[0], c, FFN_ROWS, nchunks, True)
            pv = _window(pre_ref.at[1], c, FFN_ROWS, nchunks, True)
            dact = _window(da_ref, c, FFN_ROWS, nchunks, True)
            a = _conv(pg, wa)
            v = _conv(pv, wc)
            sg = jax.nn.sigmoid(a)
            da = dact * v * (sg * (1.0 + a * (1.0 - sg)))
            dv = dact * (a * sg)
            dpre_ref[0, rows, :] = _conv_t(da, wa)[cur].astype(BF16)
            dpre_ref[1, rows, :] = _conv_t(dv, wc)[cur].astype(BF16)
            return tuple(acc + d for acc, d in zip(carry, _conv_wgrad(da, pg, cur) + _conv_wgrad(dv, pv, cur)))

        zero = jnp.zeros((1, UP_CHUNK), F32)
        sums = lax.fori_loop(0, nchunks, step, (zero,) * 6)

        @pl.when(b == 0)
        def _():
            for i in range(6):
                dw_ref[i // 3, pl.ds(i % 3, 1), :] = sums[i]

        @pl.when(b > 0)
        def _():
            for i in range(6):
                dw_ref[i // 3, pl.ds(i % 3, 1), :] += sums[i]

    pair = pl.BlockSpec((2, None, seq, UP_CHUNK), lambda i, b: (0, i, b, 0))
    wspec = pl.BlockSpec((2, None, 3, UP_CHUNK), lambda i, b: (0, i, 0, 0))
    return pl.pallas_call(
        body, name="ffn_act_bwd", grid=(N_UP_PAIRS, nbatch),
        in_specs=[pair, pl.BlockSpec((None, seq, UP_CHUNK), lambda i, b: (i, b, 0)), wspec],
        out_specs=[pair, wspec],
        out_shape=[jax.ShapeDtypeStruct(pre.shape, BF16), jax.ShapeDtypeStruct(fcw.shape, F32)],
        compiler_params=_params(("parallel", "arbitrary")),
    )(pre, d_act, fcw)


def _adamw(lands, w, m, v, row_tile, name):
    nl = len(lands)
    _, nr, ncol = lands[0].shape
    c1 = 1.0 - ADAM_B1 ** ADAM_STEP
    c2 = 1.0 - ADAM_B2 ** ADAM_STEP

    def body(*refs):
        land_refs = refs[:nl]
        w_ref, m_ref, v_ref, g_ref, d_ref, mo_ref, vo_ref = refs[nl:]
        for l in range(nl):
            @pl.when(pl.program_id(0) == l)
            def _(l=l):
                g = land_refs[l][0].astype(F32)
                for j in range(1, N_DEV):
                    g = g + land_refs[l][j].astype(F32)
                g_ref[...] = g

        g = g_ref[...]
        m2 = ADAM_B1 * m_ref[...] + (1.0 - ADAM_B1) * g
        v2 = ADAM_B2 * v_ref[...] + (1.0 - ADAM_B2) * (g * g)
        mo_ref[...] = m2
        vo_ref[...] = v2
        d_ref[...] = -ADAM_LR * ((m2 / c1) / (jnp.sqrt(v2 / c2) + ADAM_EPS) + ADAM_WD * w_ref[...])

    def land_spec(l):
        return pl.BlockSpec((N_DEV, row_tile, ncol), lambda k, i: (0, jnp.where(k == l, i, 0), 0))

    tile = pl.BlockSpec((None, row_tile, ncol), lambda k, i: (k, i, 0))
    return pl.pallas_call(
        body, name=name, grid=(nl, nr // row_tile),
        in_specs=[land_spec(l) for l in range(nl)] + [tile, tile, tile],
        out_specs=[tile] * 4,
        out_shape=[jax.ShapeDtypeStruct(w.shape, F32)] * 4,
        compiler_params=_params(("arbitrary", "arbitrary")),
    )(*lands, w, m, v)


class _Item:
    def __init__(self, src, chunked, land_cols=False):
        self.src, self.chunked, self.land_cols = src, chunked, land_cols
        if chunked == "cols":
            block = (src.shape[0], src.shape[1] // N_DEV)
        else:
            block = src.shape[1:] if chunked else src.shape
        self.width = block[-1]
        self.land_shape = (block[0], N_DEV * block[1]) if land_cols else (N_DEV,) + block

    def _cols(self, first, count=1):
        return pl.ds(pl.multiple_of(first * self.width, LANES), count * self.width)

    def part(self, src_ref, j):
        if self.chunked == "cols":
            return src_ref.at[:, self._cols(j)]
        return src_ref.at[j] if self.chunked else src_ref

    def slot(self, land_ref, s):
        return land_ref.at[:, self._cols(s)] if self.land_cols else land_ref.at[s]

    def seven(self, land_ref):
        return land_ref.at[:, self._cols(0, N_DEV - 1)] if self.land_cols else land_ref.at[pl.ds(0, N_DEV - 1)]


def _mesh_place():
    x, y, c = lax.axis_index("x"), lax.axis_index("y"), lax.axis_index("c")
    return x, y, c, 4 * x + 2 * y + c


def _flipped(x, y, c, k):
    px = 1 - x if k & 4 else x
    py = 1 - y if k & 2 else y
    pc = 1 - c if k & 1 else c
    return (px, py, pc), 4 * px + 2 * py + pc


PEER_ORDER = (2, 4, 6, 3, 5, 7, 1)


def _exchange(items, name):
    n = len(items)

    def body(*refs):
        srcs, lands = refs[:n], refs[n:2 * n]
        send, recv, local = refs[2 * n:]
        x, y, c, me = _mesh_place()

        def copy(i, k, chunk, slot, dev):
            return pltpu.make_async_remote_copy(
                src_ref=items[i].part(srcs[i], chunk), dst_ref=items[i].slot(lands[i], slot),
                send_sem=send.at[i, k - 1], recv_sem=recv.at[i, k - 1], device_id=dev, device_id_type=MESH)

        own = [pltpu.make_async_copy(items[i].part(srcs[i], me), items[i].slot(lands[i], me), local.at[i])
               for i in range(n)]
        for k in PEER_ORDER:
            dev, idx = _flipped(x, y, c, k)
            for i in range(n):
                copy(i, k, idx, me, dev).start()
        for cp in own:
            cp.start()
        for k in PEER_ORDER:
            dev, idx = _flipped(x, y, c, k)
            for i in range(n):
                copy(i, k, me, idx, dev).wait_recv()
        for k in PEER_ORDER:
            dev, idx = _flipped(x, y, c, k)
            for i in range(n):
                copy(i, k, idx, me, dev).wait_send()
        for cp in own:
            cp.wait()

    hbm = pl.BlockSpec(memory_space=pl.ANY)
    return pl.pallas_call(
        body, name=name,
        in_specs=[hbm] * n, out_specs=[hbm] * n,
        out_shape=[jax.ShapeDtypeStruct(it.land_shape, it.src.dtype) for it in items],
        scratch_shapes=[pltpu.SemaphoreType.DMA((n, N_DEV - 1)), pltpu.SemaphoreType.DMA((n, N_DEV - 1)),
                        pltpu.SemaphoreType.DMA((n,))],
        compiler_params=pltpu.CompilerParams(has_side_effects=True),
    )(*[it.src for it in items])


def _sequencer_exchange(items, name, collective_id):
    n = len(items)

    def body(*refs):
        srcs, lands = refs[:n], refs[n:2 * n]
        send, recv, local = refs[2 * n:]
        x, y, c, me = _mesh_place()
        barrier = pltpu.get_barrier_semaphore()
        for k in PEER_ORDER:
            pl.semaphore_signal(barrier, inc=1, device_id=_flipped(x, y, c, k)[0], device_id_type=MESH)
        pl.semaphore_wait(barrier, N_DEV - 1)

        def copy(i, k, chunk, slot, dev):
            return pltpu.make_async_remote_copy(
                src_ref=items[i].part(srcs[i], chunk), dst_ref=items[i].slot(lands[i], slot),
                send_sem=send.at[i, k - 1], recv_sem=recv.at[i, k - 1], device_id=dev, device_id_type=MESH)

        own = [pltpu.make_async_copy(items[i].part(srcs[i], me), items[i].slot(lands[i], me), local.at[i])
               for i in range(n)]
        for cp in own:
            cp.start()
        for k in PEER_ORDER:
            dev, idx = _flipped(x, y, c, k)
            for i in range(n):
                copy(i, k, idx, me, dev).start()
        for k in PEER_ORDER:
            dev, idx = _flipped(x, y, c, k)
            for i in range(n):
                copy(i, k, me, idx, dev).wait_recv()
        for k in PEER_ORDER:
            dev, idx = _flipped(x, y, c, k)
            for i in range(n):
                copy(i, k, idx, me, dev).wait_send()
        for cp in own:
            cp.wait()

    return pl.kernel(
        body, name=name,
        out_type=[jax.ShapeDtypeStruct(it.land_shape, it.src.dtype) for it in items],
        mesh=plsc.ScalarSubcoreMesh(axis_name="sequencer", num_cores=1),
        scratch_types=[pltpu.SemaphoreType.DMA((n, N_DEV - 1)), pltpu.SemaphoreType.DMA((n, N_DEV - 1)),
                       pltpu.SemaphoreType.DMA((n,))],
        compiler_params=pltpu.CompilerParams(collective_id=collective_id),
    )(*[it.src for it in items])


HBM_SPEC = pl.BlockSpec(memory_space=pltpu.HBM)
SEM_SPEC = pl.BlockSpec(memory_space=pltpu.SEMAPHORE)
DATAFLOW = pltpu.SideEffectType.DATAFLOW_SIDE_EFFECTING


def _exchange_start(items, name, after=()):
    n = len(items)
    na = len(after)

    def body(*refs):
        srcs, land_ins = refs[:n], refs[n:2 * n]
        outs = refs[2 * n + na:6 * n + na]
        (local,) = refs[6 * n + na:]
        del land_ins
        x, y, c, me = _mesh_place()
        own = [pltpu.make_async_copy(items[i].part(srcs[i], me), items[i].slot(outs[4 * i + 3], me), local.at[i])
               for i in range(n)]
        for cp in own:
            cp.start()
        for cp in own:
            cp.wait()
        for k in PEER_ORDER:
            dev, idx = _flipped(x, y, c, k)
            for i in range(n):
                send, recv, _, land = outs[4 * i:4 * i + 4]
                pltpu.make_async_remote_copy(
                    src_ref=items[i].part(srcs[i], idx), dst_ref=items[i].slot(land, me), send_sem=send, recv_sem=recv,
                    device_id=dev, device_id_type=MESH).start()

    out_shape, out_specs, args, lands = [], [], [], []
    for it in items:
        out_shape += [pltpu.SemaphoreType.DMA(()), pltpu.SemaphoreType.DMA(()),
                      pltpu.HBM(it.src.shape, it.src.dtype), pltpu.HBM(it.land_shape, it.src.dtype)]
        out_specs += [SEM_SPEC, SEM_SPEC, HBM_SPEC, HBM_SPEC]
        args.append(pltpu.with_memory_space_constraint(it.src, pltpu.HBM))
        lands.append(pltpu.with_memory_space_constraint(lax.empty(it.land_shape, it.src.dtype), pltpu.HBM))
    outs = pl.pallas_call(
        body, name=name,
        in_specs=[HBM_SPEC] * (2 * n) + [pl.BlockSpec(memory_space=pl.ANY)] * na,
        out_specs=out_specs, out_shape=out_shape,
        scratch_shapes=[pltpu.SemaphoreType.DMA((n,))],
        input_output_aliases={**{i: 4 * i + 2 for i in range(n)}, **{n + i: 4 * i + 3 for i in range(n)}},
        compiler_params=pltpu.CompilerParams(has_side_effects=DATAFLOW),
    )(*args, *lands, *after)
    return [tuple(outs[4 * i:4 * i + 4]) + (items[i],) for i in range(n)]


def _started(handles):
    return handles[0][2]


def _exchange_wait(handles, after, name):
    n = len(handles)

    def body(*refs):
        x, y, c, _ = _mesh_place()
        for i in range(n):
            src, land, send, recv = refs[4 * i:4 * i + 4]
            del src
            seven = handles[i][4].seven(land)
            cp = pltpu.make_async_remote_copy(src_ref=seven, dst_ref=seven, send_sem=send, recv_sem=recv,
                                              device_id=(x, y, 1 - c), device_id_type=MESH)
            cp.wait_send()
            cp.wait_recv()

    args, in_specs, out_shape = [], [], []
    for send, recv, src, land, _ in handles:
        args += [src, land, send, recv]
        in_specs += [HBM_SPEC, HBM_SPEC, SEM_SPEC, SEM_SPEC]
        out_shape += [pltpu.HBM(src.shape, src.dtype), pltpu.HBM(land.shape, land.dtype)]
    outs = pl.pallas_call(
        body, name=name,
        in_specs=in_specs + [pl.BlockSpec(memory_space=pl.ANY)] * len(after), out_specs=[HBM_SPEC] * (2 * n),
        out_shape=out_shape,
        input_output_aliases={**{4 * i: 2 * i for i in range(n)}, **{4 * i + 1: 2 * i + 1 for i in range(n)}},
        compiler_params=pltpu.CompilerParams(has_side_effects=DATAFLOW),
    )(*args, *after)
    return [outs[2 * i + 1] for i in range(n)]


TM = 1024
TM_ACC = 512
TN_IN = 768


def kernel(x, norm1_g, w_in, mix_conv_w, attn_out_g, conv_out_g, w_out, norm2_g, ffn_up, ffn_conv_w, ffn_down, final_norm_g, loss_target, m_norm1_g, m_w_in, m_mix_conv_w, m_attn_out_g, m_conv_out_g, m_w_out, m_norm2_g, m_ffn_up, m_ffn_conv_w, m_ffn_down, m_final_norm_g, v_norm1_g, v_w_in, v_mix_conv_w, v_attn_out_g, v_conv_out_g, v_w_out, v_norm2_g, v_ffn_up, v_ffn_conv_w, v_ffn_down, v_final_norm_g):
    nbatch, seq, d = x.shape
    t = nbatch * seq
    nt, nta = t // TM, t // TM_ACC
    out_rows = D_MODEL // N_DEV
    down_rows = D_FF // N_DEV
    xf = x.reshape(t, d)
    target = loss_target.reshape(t, d)

    cw_local = jnp.concatenate([ffn_conv_w, mix_conv_w], axis=-1)
    cast = lambda w: _Item(w.astype(BF16), False)
    cast_in = lambda w: _Item(w.astype(BF16), False, land_cols=True)
    cw_all, win0 = _sequencer_exchange([_Item(cw_local, False), cast_in(w_in[0])], "gather_a", 0)
    wout0, wup0 = _sequencer_exchange([cast(w_out[0]), cast(ffn_up[0])], "gather_b", 1)
    wdown0, win1, wout1 = _sequencer_exchange([cast(ffn_down[0]), cast_in(w_in[1]), cast(w_out[1])], "gather_c", 2)
    wup1, wdown1 = _sequencer_exchange([cast(ffn_up[1]), cast(ffn_down[1])], "gather_d", 3)
    win, wup = [win0, win1], [wup0, wup1]
    wout = [w.reshape(D_MODEL, D_MODEL) for w in (wout0, wout1)]
    wdown = [w.reshape(N_UP_PAIRS, UP_CHUNK, D_MODEL) for w in (wdown0, wdown1)]
    fcw = [cw_all[:, k, :, :UP_CHUNK].reshape(2, N_UP_PAIRS, 3, UP_CHUNK) for k in range(DEPTH)]
    mcw = [cw_all[:, k, :, UP_CHUNK:].transpose(1, 0, 2).reshape(3, D_CONV) for k in range(DEPTH)]

    full = lambda i, j, k: (0, 0)

    saved = []
    xin = xf
    for l in range(DEPTH):
        h1 = _rms_fwd(xin, norm1_g[l][None], f"rms1_fwd_{l}")
        proj = _matmul(
            h1, win[l], grid=(nt, D_IN // TN_IN, 1), dims=NN, name=f"proj_{l}",
            a_spec=pl.BlockSpec((TM, D_MODEL), lambda i, j, k: (i, 0)),
            b_spec=pl.BlockSpec((D_MODEL, TN_IN), lambda i, j, k: (0, j)),
            o_spec=pl.BlockSpec((TM, TN_IN), lambda i, j, k: (i, j)), o_shape=(t, D_IN), o_dtype=F32)
        o, lse, cat = _attn_fwd(proj, attn_out_g[l][None], nbatch, seq)
        cat = _convmix_fwd(proj, cat, mcw[l], conv_out_g[l][None], nbatch, seq)
        xmid = _matmul(
            cat, wout[l], grid=(nta, 1, 1), dims=NN, name=f"mix_out_{l}",
            a_spec=pl.BlockSpec((TM_ACC, D_MODEL), lambda i, j, k: (i, 0)),
            b_spec=pl.BlockSpec((D_MODEL, D_MODEL), full),
            o_spec=pl.BlockSpec((TM_ACC, D_MODEL), lambda i, j, k: (i, 0)), o_shape=(t, D_MODEL), o_dtype=F32,
            res=xin, res_spec=pl.BlockSpec((TM_ACC, D_MODEL), lambda i, j, k: (i, 0)))
        h2 = _rms_fwd(xmid, norm2_g[l][None], f"rms2_fwd_{l}")
        pre = _matmul(
            h2, wup[l], grid=(nt, N_DEV, 1), dims=NN, name=f"ffn_up_{l}",
            a_spec=pl.BlockSpec((TM, D_MODEL), lambda i, j, k: (i, 0)),
            b_spec=pl.BlockSpec((None, D_MODEL, UP_CHUNK), lambda i, j, k: (j, 0, 0)),
            o_spec=pl.BlockSpec((None, TM, UP_CHUNK), lambda i, j, k: (j, i, 0)),
            o_shape=(N_DEV, t, UP_CHUNK), o_dtype=BF16).reshape(2, N_UP_PAIRS, t, UP_CHUNK)
        act = _ffn_act_fwd(pre, fcw[l], nbatch, seq)
        xout = _matmul(
            act, wdown[l], grid=(nta, 1, 1), dims=NN, name=f"ffn_down_{l}",
            a_spec=pl.BlockSpec((N_UP_PAIRS, TM_ACC, UP_CHUNK), lambda i, j, k: (0, i, 0)),
            b_spec=pl.BlockSpec((N_UP_PAIRS, UP_CHUNK, D_MODEL), lambda i, j, k: (0, 0, 0)),
            o_spec=pl.BlockSpec((TM_ACC, D_MODEL), lambda i, j, k: (i, 0)), o_shape=(t, D_MODEL), o_dtype=F32,
            res=xmid, res_spec=pl.BlockSpec((TM_ACC, D_MODEL), lambda i, j, k: (i, 0)))
        saved.append((xin, h1, proj, o, lse, cat, xmid, h2, pre, act))
        xin = xout

    loss_part, dx, dxb, dgf = _loss_head(xin, final_norm_g[None], target, "loss_head")

    dg1, dg2, dga, dgc = [None] * DEPTH, [None] * DEPTH, [None] * DEPTH, [None] * DEPTH
    for l in reversed(range(DEPTH)):
        xin, h1, proj, o, lse, cat, xmid, h2, pre, act = saved[l]
        d_act = _matmul(
            dxb, wdown[l], grid=(nt, N_UP_PAIRS, 1), dims=NT, name=f"d_act_{l}",
            a_spec=pl.BlockSpec((TM, D_MODEL), lambda i, j, k: (i, 0)),
            b_spec=pl.BlockSpec((None, UP_CHUNK, D_MODEL), lambda i, j, k: (j, 0, 0)),
            o_spec=pl.BlockSpec((None, TM, UP_CHUNK), lambda i, j, k: (j, i, 0)),
            o_shape=(N_UP_PAIRS, t, UP_CHUNK), o_dtype=BF16)
        g_down = _matmul(
            act, dxb, grid=(N_UP_PAIRS, 1, 1), dims=TN, name=f"g_down_{l}",
            a_spec=pl.BlockSpec((None, t, UP_CHUNK), lambda i, j, k: (i, 0, 0)),
            b_spec=pl.BlockSpec((t, D_MODEL), full),
            o_spec=pl.BlockSpec((None, UP_CHUNK, D_MODEL), lambda i, j, k: (i, 0, 0)),
            o_shape=(N_UP_PAIRS, UP_CHUNK, D_MODEL), o_dtype=BF16).reshape(N_DEV, down_rows, D_MODEL)
        d_pre, d_fcw = _ffn_act_bwd(pre, d_act, fcw[l], nbatch, seq)
        d_pre = d_pre.reshape(N_DEV, t, UP_CHUNK)
        dh2 = _matmul(
            d_pre, wup[l], grid=(nta, 1, 1), dims=NT, name=f"d_h2_{l}",
            a_spec=pl.BlockSpec((N_DEV, TM_ACC, UP_CHUNK), lambda i, j, k: (0, i, 0)),
            b_spec=pl.BlockSpec((N_DEV, D_MODEL, UP_CHUNK), lambda i, j, k: (0, 0, 0)),
            o_spec=pl.BlockSpec((TM_ACC, D_MODEL), lambda i, j, k: (i, 0)), o_shape=(t, D_MODEL), o_dtype=F32)
        g_up = _matmul(
            h2, d_pre, grid=(1, N_DEV, 1), dims=TN, name=f"g_up_{l}",
            a_spec=pl.BlockSpec((t, D_MODEL), full),
            b_spec=pl.BlockSpec((None, t, UP_CHUNK), lambda i, j, k: (j, 0, 0)),
            o_spec=pl.BlockSpec((None, D_MODEL, UP_CHUNK), lambda i, j, k: (j, 0, 0)),
            o_shape=(N_DEV, D_MODEL, UP_CHUNK), o_dtype=BF16)
        dxm, dxmb, dg2[l] = _rms_bwd(xmid, norm2_g[l][None], dh2, dx, f"rms2_bwd_{l}")
        g_out = _matmul(
            cat, dxmb, grid=(1, 1, nt), dims=TN, name=f"g_out_{l}",
            a_spec=pl.BlockSpec((TM, D_MODEL), lambda i, j, k: (k, 0)),
            b_spec=pl.BlockSpec((TM, D_MODEL), lambda i, j, k: (k, 0)),
            o_spec=pl.BlockSpec((D_MODEL, D_MODEL), full),
            o_shape=(D_MODEL, D_MODEL), o_dtype=BF16).reshape(N_DEV, out_rows, D_MODEL)
        if l == 0:
            land_out0, land_up0, land_down0 = _sequencer_exchange(
                [_Item(g_out, True), _Item(g_up, True), _Item(g_down, True)], "scatter_0a", 5)
        d_cat = _matmul(
            dxmb, wout[l], grid=(nta, 1, 1), dims=NT, name=f"d_cat_{l}",
            a_spec=pl.BlockSpec((TM_ACC, D_MODEL), lambda i, j, k: (i, 0)),
            b_spec=pl.BlockSpec((D_MODEL, D_MODEL), full),
            o_spec=pl.BlockSpec((TM_ACC, D_MODEL), lambda i, j, k: (i, 0)), o_shape=(t, D_MODEL), o_dtype=BF16)
        d_proj, dga[l] = _attn_bwd(proj, o, lse, d_cat, attn_out_g[l][None], nbatch, seq)
        d_proj, d_mcw, dgc[l] = _convmix_bwd(proj, d_cat, d_proj, mcw[l], conv_out_g[l][None], nbatch, seq)
        g_in = _matmul(
            h1, d_proj, grid=(1, D_IN // TN_IN, 1), dims=TN, name=f"g_in_{l}",
            a_spec=pl.BlockSpec((t, D_MODEL), full),
            b_spec=pl.BlockSpec((t, TN_IN), lambda i, j, k: (0, j)),
            o_spec=pl.BlockSpec((D_MODEL, TN_IN), lambda i, j, k: (0, j)),
            o_shape=(D_MODEL, D_IN), o_dtype=BF16)
        g_cw = jnp.concatenate(
            [d_fcw.reshape(N_DEV, 3, UP_CHUNK), d_mcw.reshape(3, N_DEV, D_CONV // N_DEV).transpose(1, 0, 2)], axis=-1)
        if l == 0:
            land_in0, land_cw0 = _sequencer_exchange([_Item(g_in, "cols"), _Item(g_cw, True)], "scatter_0b", 6)
        else:
            land_in1, land_out1, land_up1, land_down1, land_cw1 = _sequencer_exchange(
                [_Item(g_in, "cols"), _Item(g_out, True), _Item(g_up, True), _Item(g_down, True), _Item(g_cw, True)],
                "scatter_1", 4)
        dh1 = _matmul(
            d_proj, win[l], grid=(nta, 1, 1), dims=NT, name=f"d_h1_{l}",
            a_spec=pl.BlockSpec((TM_ACC, D_IN), lambda i, j, k: (i, 0)),
            b_spec=pl.BlockSpec((D_MODEL, D_IN), full),
            o_spec=pl.BlockSpec((TM_ACC, D_MODEL), lambda i, j, k: (i, 0)), o_shape=(t, D_MODEL), o_dtype=F32)
        dx, dxb, dg1[l] = _rms_bwd(xin, norm1_g[l][None], dh1, dxm, f"rms1_bwd_{l}")

    def pack_small(n1, a, c, n2, f):
        return jnp.concatenate(
            [n1, n2, f[None], jnp.concatenate([a, c], axis=-1), jnp.zeros((1, D_MODEL), F32)], axis=0)[None]

    res_out = _adamw([land_out0, land_out1], w_out, m_w_out, v_w_out, out_rows, "adamw_w_out")
    res_up = _adamw([land_up0, land_up1], ffn_up, m_ffn_up, v_ffn_up, 256, "adamw_ffn_up")
    res_down = _adamw([land_down0, land_down1], ffn_down, m_ffn_down, v_ffn_down, down_rows, "adamw_ffn_down")
    small = jnp.concatenate(
        [dg1[0], dg1[1], dg2[0], dg2[1], dgf,
         jnp.concatenate([dga[0], dgc[0]], axis=-1), jnp.concatenate([dga[1], dgc[1]], axis=-1),
         jnp.zeros((1, D_MODEL), F32)], axis=0)
    (land_small,) = _exchange([_Item(small, False)], "gather_gain_grads")
    res_small = _adamw(
        [land_small], pack_small(norm1_g, attn_out_g, conv_out_g, norm2_g, final_norm_g),
        pack_small(m_norm1_g, m_attn_out_g, m_conv_out_g, m_norm2_g, m_final_norm_g),
        pack_small(v_norm1_g, v_attn_out_g, v_conv_out_g, v_norm2_g, v_final_norm_g), SUBLANES, "adamw_gains")
    res_in = _adamw([land_in0, land_in1], w_in, m_w_in, v_w_in, 256, "adamw_w_in")
    res_cw = _adamw(
        [land_cw0, land_cw1], cw_local, jnp.concatenate([m_ffn_conv_w, m_mix_conv_w], axis=-1),
        jnp.concatenate([v_ffn_conv_w, v_mix_conv_w], axis=-1), 3, "adamw_conv_w")

    loss = lax.psum(loss_part[0, 0], ("x", "y", "c"))

    def unpack(kind):
        s = res_small[kind][0]
        cwr = res_cw[kind]
        return (s[0:2], res_in[kind], cwr[..., UP_CHUNK:], s[5:7, :D_ATTN], s[5:7, D_ATTN:], res_out[kind],
                s[2:4], res_up[kind], cwr[..., :UP_CHUNK], res_down[kind], s[4])

    return (loss, dx.reshape(nbatch, seq, d), *unpack(0), *unpack(1), *unpack(2), *unpack(3))
```

```python
import math

import jax
import jax.numpy as jnp
from jax import lax
from jax.experimental import pallas as pl
from jax.experimental.pallas import tpu as pltpu
from jax.experimental.pallas import tpu_sc as plsc

F32 = jnp.float32
BF16 = jnp.bfloat16

D_MODEL = 1024
D_ATTN = 512
D_CONV = 512
HEAD_DIM = 64
N_HEADS = 8
D_FF = 2816
DEPTH = 2
D_IN = 3 * D_ATTN + 3 * D_CONV
EPS = 1e-6
DILATIONS = (1, 4, 16)
BAND = 128
N_DEV = 8
IN_CHUNK = D_IN // N_DEV
UP_CHUNK = 2 * D_FF // N_DEV
N_UP_PAIRS = N_DEV // 2
CW_PACK = UP_CHUNK + D_CONV // N_DEV
ADAM_LR = 0.001
ADAM_B1 = 0.9
ADAM_B2 = 0.999
ADAM_EPS = 1e-08
ADAM_WD = 0.01
ADAM_STEP = 10
LANES = 128
SUBLANES = 8
VMEM_LIMIT = 56 * 1024 * 1024

NEG = -1e30
MESH = pl.DeviceIdType.MESH


def _params(sem=None, vmem=VMEM_LIMIT):
    return pltpu.CompilerParams(dimension_semantics=sem, vmem_limit_bytes=vmem)


NN = (((1,), (0,)), ((), ()))
NT = (((1,), (1,)), ((), ()))
TN = (((0,), (0,)), ((), ()))


def _matmul(a, b, *, grid, a_spec, b_spec, o_spec, o_shape, o_dtype, dims, name, res=None, res_spec=None, after=()):
    nk = grid[2]
    o_block = tuple(s for s in o_spec.block_shape if s is not None)
    na = len(after)

    def body(*refs):
        refs = refs[:2 + (res is not None)] + refs[2 + (res is not None) + na:]
        if res is None:
            a_ref, b_ref, o_ref, *scr = refs
            r_ref = None
        else:
            a_ref, b_ref, r_ref, o_ref, *scr = refs
        def dot(av, bv):
            return lax.dot_general(av.astype(BF16), bv.astype(BF16), dims, preferred_element_type=F32)

        if len(a_ref.shape) == 3:
            part = dot(a_ref[0], b_ref[0])
            for c in range(1, a_ref.shape[0]):
                part = part + dot(a_ref[c], b_ref[c])
        else:
            part = dot(a_ref[...], b_ref[...])

        def finish(total):
            if r_ref is not None:
                total = total + r_ref[...]
            o_ref[...] = total.astype(o_dtype)

        if nk == 1:
            finish(part)
        else:
            acc = scr[0]
            k = pl.program_id(2)

            @pl.when(k == 0)
            def _():
                acc[...] = part

            @pl.when(k > 0)
            def _():
                acc[...] += part

            @pl.when(k == nk - 1)
            def _():
                finish(acc[...])

    in_specs = [a_spec, b_spec] + ([res_spec] if res is not None else []) + [pl.BlockSpec(memory_space=pl.ANY)] * na
    args = (a, b) + ((res,) if res is not None else ()) + tuple(after)
    return pl.pallas_call(
        body, name=name, grid=grid, in_specs=in_specs, out_specs=o_spec,
        out_shape=jax.ShapeDtypeStruct(o_shape, o_dtype),
        scratch_shapes=[pltpu.VMEM(o_block, F32)] if nk > 1 else [],
        compiler_params=_params(("parallel", "parallel", "arbitrary")),
    )(*args)


ROW_TILE = 512


def _rms_fwd(x, g, name):
    t, d = x.shape

    def body(x_ref, g_ref, h_ref):
        xv = x_ref[...]
        r = lax.rsqrt(jnp.mean(xv * xv, axis=-1, keepdims=True) + EPS)
        h_ref[...] = (xv * r * g_ref[...]).astype(BF16)

    return pl.pallas_call(
        body, name=name, grid=(t // ROW_TILE,),
        in_specs=[pl.BlockSpec((ROW_TILE, d), lambda i: (i, 0)), pl.BlockSpec((1, d), lambda i: (0, 0))],
        out_specs=pl.BlockSpec((ROW_TILE, d), lambda i: (i, 0)),
        out_shape=jax.ShapeDtypeStruct((t, d), BF16),
        compiler_params=_params(("parallel",)),
    )(x, g)


def _rms_bwd(x, g, dh, dres, name):
    t, d = x.shape

    def body(x_ref, g_ref, dh_ref, dres_ref, dx_ref, dxb_ref, dg_ref):
        xv = x_ref[...]
        r = lax.rsqrt(jnp.mean(xv * xv, axis=-1, keepdims=True) + EPS)
        xh = xv * r
        dhv = dh_ref[...]
        gd = dhv * g_ref[...]
        dx = r * (gd - xh * jnp.mean(gd * xh, axis=-1, keepdims=True)) + dres_ref[...]
        dx_ref[...] = dx
        dxb_ref[...] = dx.astype(BF16)
        part = jnp.sum(dhv * xh, axis=0, keepdims=True)

        @pl.when(pl.program_id(0) == 0)
        def _():
            dg_ref[...] = part

        @pl.when(pl.program_id(0) > 0)
        def _():
            dg_ref[...] += part

    row = pl.BlockSpec((ROW_TILE, d), lambda i: (i, 0))
    vec = pl.BlockSpec((1, d), lambda i: (0, 0))
    return pl.pallas_call(
        body, name=name, grid=(t // ROW_TILE,),
        in_specs=[row, vec, row, row], out_specs=[row, row, vec],
        out_shape=[jax.ShapeDtypeStruct((t, d), F32), jax.ShapeDtypeStruct((t, d), BF16),
                   jax.ShapeDtypeStruct((1, d), F32)],
        compiler_params=_params(("arbitrary",)),
    )(x, g, dh, dres)


def _loss_head(x, g, target, name):
    t, d = x.shape

    def body(x_ref, g_ref, t_ref, loss_ref, dx_ref, dxb_ref, dg_ref):
        xv = x_ref[...]
        r = lax.rsqrt(jnp.mean(xv * xv, axis=-1, keepdims=True) + EPS)
        xh = xv * r
        gv = g_ref[...]
        err = xh * gv - t_ref[...]
        loss = jnp.full((1, LANES), 0.5 / d, F32) * jnp.sum(err * err)
        dy = err * (1.0 / d)
        gd = dy * gv
        dx = r * (gd - xh * jnp.mean(gd * xh, axis=-1, keepdims=True))
        dx_ref[...] = dx
        dxb_ref[...] = dx.astype(BF16)
        part = jnp.sum(dy * xh, axis=0, keepdims=True)

        @pl.when(pl.program_id(0) == 0)
        def _():
            dg_ref[...] = part
            loss_ref[...] = loss

        @pl.when(pl.program_id(0) > 0)
        def _():
            dg_ref[...] += part
            loss_ref[...] += loss

    row = pl.BlockSpec((ROW_TILE, d), lambda i: (i, 0))
    vec = pl.BlockSpec((1, d), lambda i: (0, 0))
    return pl.pallas_call(
        body, name=name, grid=(t // ROW_TILE,),
        in_specs=[row, vec, row],
        out_specs=[pl.BlockSpec((1, LANES), lambda i: (0, 0)), row, row, vec],
        out_shape=[jax.ShapeDtypeStruct((1, LANES), F32), jax.ShapeDtypeStruct((t, d), F32),
                   jax.ShapeDtypeStruct((t, d), BF16), jax.ShapeDtypeStruct((1, d), F32)],
        compiler_params=_params(("arbitrary",)),
    )(x, g, target)


def _group_matrix(n):
    shift = int(math.log2(HEAD_DIM))
    r = lax.broadcasted_iota(jnp.int32, (n, n), 0) >> shift
    c = lax.broadcasted_iota(jnp.int32, (n, n), 1) >> shift
    return (r == c).astype(BF16)


def _group_sum(v, gmat):
    hi = v.astype(BF16)
    lo = (v - hi.astype(F32)).astype(BF16)

    def dot(p):
        return jnp.dot(p, gmat, preferred_element_type=F32)

    return dot(hi) + dot(lo)


def _shift_rows(ext, k):
    return pltpu.roll(ext, k % ext.shape[0], 0)


def _store_columns(stage, out_hbm, sems, row0, nrows, col_blocks):
    rows = pl.ds(pl.multiple_of(row0, SUBLANES * 2), nrows)
    copies = [
        pltpu.make_async_copy(stage.at[i], out_hbm.at[rows, pl.ds(pl.multiple_of(cb * LANES, LANES), LANES)], sems.at[i])
        for i, cb in enumerate(col_blocks)
    ]
    for cp in copies:
        cp.start()
    for cp in copies:
        cp.wait()


def _attn_consts(width):
    i = lax.broadcasted_iota(jnp.int32, (BAND, width), 0)
    j = lax.broadcasted_iota(jnp.int32, (BAND, width), 1)
    dist = (width - BAND) + i - j
    inwin = (dist >= 0) & (dist <= BAND)
    return dist.astype(F32), inwin, j


def _head_masks():
    lane = lax.broadcasted_iota(jnp.int32, (1, LANES), 1)
    return [(lane < HEAD_DIM).astype(F32), (lane >= HEAD_DIM).astype(F32)]


def _pair_bias(slope, dil):
    distf, inwin, _ = _attn_consts(2 * BAND)
    return jnp.concatenate([jnp.where(inwin, distf * (slope[hh] * (-float(dil))), NEG) for hh in range(2)], axis=0)


def _stack_heads(xv, hmask):
    return jnp.concatenate([xv * hmask[0], xv * hmask[1]], axis=0).astype(BF16)


FWD_UNROLL = 8
BWD_UNROLL = 8


def _unroll(trips, most):
    return max(u for u in range(1, most + 1) if trips % u == 0)


def _for_blocks(seq, dil, block, most):
    nb = seq // dil // BAND

    def residue(r, carry):
        base = r * nb
        block(pl.multiple_of(base * BAND, BAND), None)
        if nb > 1:
            def rest(n, c):
                block(pl.multiple_of((base + n) * BAND, BAND), pl.multiple_of((base + n - 1) * BAND, BAND))
                return c

            lax.fori_loop(1, nb, rest, 0, unroll=_unroll(nb - 1, most))
        return carry

    if dil == 1:
        residue(0, 0)
    else:
        lax.fori_loop(0, dil, residue, 0, unroll=_unroll(dil, max(1, most // nb)))


def _permute_in(src_ref, dst_ref, dil, seq):
    length = seq // dil
    for r in range(dil):
        dst_ref[pl.ds(r * length, length), :] = src_ref[pl.ds(r, length, stride=dil), :].astype(dst_ref.dtype)


def _slopes_table():
    slopes = 2.0 ** (-8.0 * jnp.arange(1, N_HEADS + 1, dtype=F32) / N_HEADS)
    return jnp.broadcast_to(slopes[:, None], (N_HEADS, 2 * BAND))


def _attn_fwd(proj, attn_g, nbatch, seq):
    t = nbatch * seq
    scale = HEAD_DIM ** -0.5

    def body(q_ref, k_ref, v_ref, g_ref, sl_ref, o_ref, lse_ref, cat_ref, pq, pk, pv, po, pm, pll, ao, am, al):
        hp = pl.program_id(1)
        hmask = _head_masks()
        slope = [sl_ref[pl.ds(2 * hp + hh, 1), :] for hh in range(2)]

        def run_branch(dil, qs, ks, vs, osink, msink, lsink):
            bias = _pair_bias(slope, dil)

            def block(row0, prow):
                cur = pl.ds(row0, BAND)
                q2 = _stack_heads(qs[cur, :] * scale, hmask)
                if prow is None:
                    kk, vv, bias_b = ks[cur, :], vs[cur, :], bias[:, BAND:]
                else:
                    prev = pl.ds(prow, BAND)
                    kk = jnp.concatenate([ks[prev, :], ks[cur, :]], axis=0)
                    vv = jnp.concatenate([vs[prev, :], vs[cur, :]], axis=0)
                    bias_b = bias
                s = lax.dot_general(q2, kk.astype(BF16), NT, preferred_element_type=F32) + bias_b
                m = jnp.max(s, axis=1, keepdims=True)
                p = jnp.exp(s - m)
                l = jnp.sum(p, axis=1, keepdims=True)
                pb = p.astype(BF16)
                o = jnp.dot(jnp.concatenate([pb[:BAND], pb[BAND:]], axis=1), _stack_heads(vv, hmask),
                            preferred_element_type=F32)
                osink[cur, :] = o
                msink[cur, :] = m[:BAND] * hmask[0] + m[BAND:] * hmask[1]
                lsink[cur, :] = l[:BAND] * hmask[0] + l[BAND:] * hmask[1]

            _for_blocks(seq, dil, block, FWD_UNROLL)

        run_branch(1, q_ref, k_ref, v_ref, ao, am, al)
        for dil in DILATIONS[1:]:
            length = seq // dil
            _permute_in(q_ref, pq, dil, seq)
            _permute_in(k_ref, pk, dil, seq)
            _permute_in(v_ref, pv, dil, seq)
            run_branch(dil, pq, pk, pv, po, pm, pll)
            for r in range(dil):
                nat = pl.ds(r, length, stride=dil)
                per = pl.ds(r * length, length)
                m0 = am[nat, :]
                mb = pm[per, :]
                mn = jnp.maximum(m0, mb)
                e0 = jnp.exp(m0 - mn)
                eb = jnp.exp(mb - mn)
                ao[nat, :] = ao[nat, :] * e0 + po[per, :] * eb
                al[nat, :] = al[nat, :] * e0 + pll[per, :] * eb
                am[nat, :] = mn

        gmat = _group_matrix(LANES)
        gv = g_ref[...]

        def fin(c, carry):
            rows = pl.ds(pl.multiple_of(c * 256, 256), 256)
            lv = al[rows, :]
            o = ao[rows, :] / lv
            o_ref[rows, :] = o
            lse_ref[rows, :] = am[rows, :] + jnp.log(lv)
            ms = _group_sum(o * o, gmat) * (1.0 / HEAD_DIM)
            cat_ref[rows, :] = (o * lax.rsqrt(ms + EPS) * gv).astype(BF16)
            return carry

        lax.fori_loop(0, seq // 256, fin, 0)

    nq = D_ATTN // LANES
    blk = lambda off: pl.BlockSpec((seq, LANES), lambda b, h: (b, h + off))
    scratch = [pltpu.VMEM((seq, LANES), F32) for _ in range(9)]
    return pl.pallas_call(
        body, name="attn_fwd", grid=(nbatch, nq),
        in_specs=[blk(0), blk(nq), blk(2 * nq), pl.BlockSpec((1, LANES), lambda b, h: (0, h)),
                  pl.BlockSpec((N_HEADS, 2 * BAND), lambda b, h: (0, 0))],
        out_specs=[blk(0), blk(0), blk(0)],
        out_shape=[jax.ShapeDtypeStruct((t, D_ATTN), F32), jax.ShapeDtypeStruct((t, D_ATTN), F32),
                   jax.ShapeDtypeStruct((t, D_MODEL), BF16)],
        scratch_shapes=scratch,
        compiler_params=_params(("parallel", "parallel")),
    )(proj, proj, proj, attn_g, _slopes_table())


def _attn_bwd(proj, o, lse, d_cat, attn_g, nbatch, seq):
    t = nbatch * seq
    scale = HEAD_DIM ** -0.5

    def body(q_ref, k_ref, v_ref, o_ref, lse_ref, dy_ref, g_ref, sl_ref, dproj_ref, dg_ref,
             do_n, dl_n, dq_n, dk_n, dv_n, pq, pk, pv, pdo, plse, pdl, pdq, pdk, pdv, stage, sems):
        hp = pl.program_id(0)
        hmask = _head_masks()
        slope = [sl_ref[pl.ds(2 * hp + hh, 1), :] for hh in range(2)]
        gmat = _group_matrix(LANES)
        gv = g_ref[...]

        def prep(c, dg):
            rows = pl.ds(pl.multiple_of(c * 256, 256), 256)
            ov = o_ref[rows, :]
            dyn = dy_ref[rows, :].astype(F32)
            r = lax.rsqrt(_group_sum(ov * ov, gmat) * (1.0 / HEAD_DIM) + EPS)
            gd = dyn * gv
            oh = ov * r
            do = r * (gd - oh * (_group_sum(gd * oh, gmat) * (1.0 / HEAD_DIM)))
            do_n[rows, :] = do
            dl_n[rows, :] = _group_sum(do * ov, gmat)
            return dg + jnp.sum(dyn * oh, axis=0, keepdims=True)

        dg = lax.fori_loop(0, seq // 256, prep, jnp.zeros((1, LANES), F32))

        @pl.when(pl.program_id(1) == 0)
        def _():
            dg_ref[...] = dg

        @pl.when(pl.program_id(1) > 0)
        def _():
            dg_ref[...] += dg

        def clear(*refs):
            def step(c, carry):
                rows = pl.ds(pl.multiple_of(c * 256, 256), 256)
                for ref in refs:
                    ref[rows, :] = jnp.zeros((256, LANES), F32)
                return carry

            lax.fori_loop(0, seq // 256, step, 0)

        clear(dq_n, dk_n, dv_n)

        def run_branch(dil, qs, ks, vs, dos, lses, dls, dqs, dks, dvs):
            bias = _pair_bias(slope, dil)

            def per_head(xv):
                return jnp.concatenate([xv[:, 0:1], xv[:, HEAD_DIM:HEAD_DIM + 1]], axis=0)

            def block(row0, prow):
                cur = pl.ds(row0, BAND)
                keys = cur if prow is None else pl.ds(prow, 2 * BAND)
                q2 = _stack_heads(qs[cur, :] * scale, hmask)
                do2 = _stack_heads(dos[cur, :], hmask)
                kk, vv = ks[keys, :], vs[keys, :]
                s = lax.dot_general(q2, kk.astype(BF16), NT, preferred_element_type=F32)
                s = s + (bias[:, BAND:] if prow is None else bias)
                p = jnp.exp(s - per_head(lses[cur, :]))
                dp = lax.dot_general(do2, vv.astype(BF16), NT, preferred_element_type=F32)
                ds = (p * (dp - per_head(dls[cur, :]))).astype(BF16)
                dqs[cur, :] += jnp.dot(jnp.concatenate([ds[:BAND], ds[BAND:]], axis=1), _stack_heads(kk, hmask),
                                       preferred_element_type=F32)
                dks[keys, :] += lax.dot_general(ds, q2, TN, preferred_element_type=F32)
                dvs[keys, :] += lax.dot_general(p.astype(BF16), do2, TN, preferred_element_type=F32)

            _for_blocks(seq, dil, block, BWD_UNROLL)

        run_branch(1, q_ref, k_ref, v_ref, do_n, lse_ref, dl_n, dq_n, dk_n, dv_n)
        for dil in DILATIONS[1:]:
            length = seq // dil
            for src, dst in ((q_ref, pq), (k_ref, pk), (v_ref, pv), (do_n, pdo), (lse_ref, plse), (dl_n, pdl)):
                _permute_in(src, dst, dil, seq)
            clear(pdq, pdk, pdv)
            run_branch(dil, pq, pk, pv, pdo, plse, pdl, pdq, pdk, pdv)
            for r in range(dil):
                nat = pl.ds(r, length, stride=dil)
                per = pl.ds(r * length, length)
                dq_n[nat, :] += pdq[per, :]
                dk_n[nat, :] += pdk[per, :]
                dv_n[nat, :] += pdv[per, :]

        def emit(c, carry):
            rows = pl.ds(pl.multiple_of(c * 256, 256), 256)
            stage[0, rows, :] = (dq_n[rows, :] * scale).astype(BF16)
            stage[1, rows, :] = dk_n[rows, :].astype(BF16)
            stage[2, rows, :] = dv_n[rows, :].astype(BF16)
            return carry

        lax.fori_loop(0, seq // 256, emit, 0)
        _store_columns(stage, dproj_ref, sems, pl.program_id(1) * seq, seq, [hp, nq + hp, 2 * nq + hp])

    nq = D_ATTN // LANES
    blk = lambda off: pl.BlockSpec((seq, LANES), lambda h, b: (b, h + off))
    vec = pl.BlockSpec((1, LANES), lambda h, b: (0, h))
    scratch = [pltpu.VMEM((seq, LANES), F32) for _ in range(14)]
    scratch += [pltpu.VMEM((3, seq, LANES), BF16), pltpu.SemaphoreType.DMA((3,))]
    d_proj, dg = pl.pallas_call(
        body, name="attn_bwd", grid=(nq, nbatch),
        in_specs=[blk(0), blk(nq), blk(2 * nq), blk(0), blk(0), blk(0), vec,
                  pl.BlockSpec((N_HEADS, 2 * BAND), lambda h, b: (0, 0))],
        out_specs=[pl.BlockSpec(memory_space=pl.ANY), vec],
        out_shape=[jax.ShapeDtypeStruct((t, D_IN), BF16), jax.ShapeDtypeStruct((1, D_ATTN), F32)],
        scratch_shapes=scratch,
        compiler_params=_params(("arbitrary", "arbitrary")),
    )(proj, proj, proj, o, lse, d_cat, attn_g, _slopes_table())
    return d_proj, dg


HALO = SUBLANES
PACKED_ROWS = 2 * SUBLANES


def _window(ref, c, rows, nchunks, after):
    row0 = pl.multiple_of(c * rows, rows)
    prev0 = pl.multiple_of(jnp.maximum(row0 - PACKED_ROWS, 0), PACKED_ROWS)
    before = ref[pl.ds(prev0, PACKED_ROWS), :].astype(F32)[PACKED_ROWS - HALO:] * (c > 0).astype(F32)
    parts = [before, ref[pl.ds(row0, rows), :].astype(F32)]
    if after:
        next0 = pl.multiple_of(jnp.minimum(row0 + rows, (nchunks - 1) * rows), PACKED_ROWS)
        parts.append(ref[pl.ds(next0, PACKED_ROWS), :].astype(F32)[:HALO] * (c < nchunks - 1).astype(F32))
    return jnp.concatenate(parts, axis=0)


def _behind(z):
    z1 = _shift_rows(z, 1)
    return z1, _shift_rows(z1, 1)


def _ahead(dy):
    d1 = _shift_rows(dy, -1)
    return d1, _shift_rows(d1, -1)


def _conv(z, w):
    z1, z2 = _behind(z)
    return w[0:1] * z2 + w[1:2] * z1 + w[2:3] * z


def _conv_bwd(dy, z, w, cur):
    d1, d2 = _ahead(dy)
    dz = w[2:3] * dy + w[1:2] * d1 + w[0:1] * d2
    return dz, [jnp.sum((d * z)[cur], axis=0, keepdims=True) for d in (d2, d1, dy)]


def _sigmoid(a):
    return 0.5 * jnp.tanh(0.5 * a) + 0.5


MIX_ROWS = 256
GATE_B_BLOCK = 3 * D_ATTN // LANES
GATE_C_BLOCK = GATE_B_BLOCK + D_CONV // LANES
U_BLOCK = GATE_C_BLOCK + D_CONV // LANES


def _convmix_fwd(proj, cat, mcw, conv_g, nbatch, seq):
    nchunks = seq // MIX_ROWS

    def body(gb_ref, gc_ref, u_ref, w_ref, g_ref, cat_in, cat_ref):
        del cat_in
        gmat = _group_matrix(LANES)
        w = w_ref[...]
        gv = g_ref[...]

        def step(c, carry):
            cur = pl.ds(pl.multiple_of(c * MIX_ROWS, MIX_ROWS), MIX_ROWS)
            z = _window(gc_ref, c, MIX_ROWS, nchunks, False) * _window(u_ref, c, MIX_ROWS, nchunks, False)
            y = gb_ref[cur, :] * _conv(z, w)[HALO:]
            ms = _group_sum(y * y, gmat) * (1.0 / HEAD_DIM)
            cat_ref[cur, :] = (y * lax.rsqrt(ms + EPS) * gv).astype(BF16)
            return carry

        lax.fori_loop(0, nchunks, step, 0)

    nc = D_CONV // LANES
    blk = lambda off: pl.BlockSpec((seq, LANES), lambda b, j: (b, j + off))
    return pl.pallas_call(
        body, name="convmix_fwd", grid=(nbatch, nc),
        in_specs=[blk(GATE_B_BLOCK), blk(GATE_C_BLOCK), blk(U_BLOCK),
                  pl.BlockSpec((3, LANES), lambda b, j: (0, j)), pl.BlockSpec((1, LANES), lambda b, j: (0, j)),
                  pl.BlockSpec(memory_space=pl.ANY)],
        out_specs=blk(D_ATTN // LANES),
        out_shape=jax.ShapeDtypeStruct(cat.shape, cat.dtype),
        input_output_aliases={5: 0},
        compiler_params=_params(("parallel", "parallel")),
    )(proj, proj, proj, mcw, conv_g, cat)


def _convmix_bwd(proj, d_cat, d_proj, mcw, conv_g, nbatch, seq):
    nchunks = seq // MIX_ROWS

    def body(gb_ref, gc_ref, u_ref, dy_ref, w_ref, g_ref, dproj_in, dproj_ref, dw_ref, dg_ref, stage, sems):
        del dproj_in
        cb = pl.program_id(0)
        b = pl.program_id(1)
        gmat = _group_matrix(LANES)
        w = w_ref[...]
        gv = g_ref[...]
        cur = slice(HALO, HALO + MIX_ROWS)

        def step(c, carry):
            rows = pl.ds(pl.multiple_of(c * MIX_ROWS, MIX_ROWS), MIX_ROWS)
            gb = _window(gb_ref, c, MIX_ROWS, nchunks, True)
            gc = _window(gc_ref, c, MIX_ROWS, nchunks, True)
            u = _window(u_ref, c, MIX_ROWS, nchunks, True)
            dyn = _window(dy_ref, c, MIX_ROWS, nchunks, True)
            z = gc * u
            conv = _conv(z, w)
            y = gb * conv
            r = lax.rsqrt(_group_sum(y * y, gmat) * (1.0 / HEAD_DIM) + EPS)
            yh = y * r
            gd = dyn * gv
            dy = r * (gd - yh * (_group_sum(gd * yh, gmat) * (1.0 / HEAD_DIM)))
            dz, dws = _conv_bwd(dy * gb, z, w, cur)
            stage[0, rows, :] = (dy * conv)[cur].astype(BF16)
            stage[1, rows, :] = (dz * u)[cur].astype(BF16)
            stage[2, rows, :] = (dz * gc)[cur].astype(BF16)
            dg = jnp.sum((dyn * yh)[cur], axis=0, keepdims=True)
            return tuple(a + d for a, d in zip(carry, dws + [dg]))

        zero = jnp.zeros((1, LANES), F32)
        dw0, dw1, dw2, dg = lax.fori_loop(0, nchunks, step, (zero, zero, zero, zero))

        @pl.when(b == 0)
        def _():
            dw_ref[0:1, :] = dw0
            dw_ref[1:2, :] = dw1
            dw_ref[2:3, :] = dw2
            dg_ref[...] = dg

        @pl.when(b > 0)
        def _():
            dw_ref[0:1, :] += dw0
            dw_ref[1:2, :] += dw1
            dw_ref[2:3, :] += dw2
            dg_ref[...] += dg

        _store_columns(stage, dproj_ref, sems, b * seq, seq, [GATE_B_BLOCK + cb, GATE_C_BLOCK + cb, U_BLOCK + cb])

    nc = D_CONV // LANES
    blk = lambda off: pl.BlockSpec((seq, LANES), lambda j, b: (b, j + off))
    return pl.pallas_call(
        body, name="convmix_bwd", grid=(nc, nbatch),
        in_specs=[blk(GATE_B_BLOCK), blk(GATE_C_BLOCK), blk(U_BLOCK), blk(D_ATTN // LANES),
                  pl.BlockSpec((3, LANES), lambda j, b: (0, j)), pl.BlockSpec((1, LANES), lambda j, b: (0, j)),
                  pl.BlockSpec(memory_space=pl.ANY)],
        out_specs=[pl.BlockSpec(memory_space=pl.ANY), pl.BlockSpec((3, LANES), lambda j, b: (0, j)),
                   pl.BlockSpec((1, LANES), lambda j, b: (0, j))],
        out_shape=[jax.ShapeDtypeStruct(d_proj.shape, d_proj.dtype), jax.ShapeDtypeStruct((3, D_CONV), F32),
                   jax.ShapeDtypeStruct((1, D_CONV), F32)],
        scratch_shapes=[pltpu.VMEM((3, seq, LANES), BF16), pltpu.SemaphoreType.DMA((3,))],
        input_output_aliases={6: 0},
        compiler_params=_params(("arbitrary", "arbitrary")),
    )(proj, proj, proj, d_cat, mcw, conv_g, d_proj)


FFN_ROWS = 256


def _ffn_act_fwd(pre, fcw, nbatch, seq):
    t = nbatch * seq
    nchunks = seq // FFN_ROWS

    def body(pre_ref, w_ref, act_ref):
        wa = w_ref[0]
        wc = w_ref[1]

        def step(c, carry):
            cur = pl.ds(pl.multiple_of(c * FFN_ROWS, FFN_ROWS), FFN_ROWS)
            a = _conv(_window(pre_ref.at[0], c, FFN_ROWS, nchunks, False), wa)[HALO:]
            v = _conv(_window(pre_ref.at[1], c, FFN_ROWS, nchunks, False), wc)[HALO:]
            act_ref[cur, :] = (a * _sigmoid(a) * v).astype(BF16)
            return carry

        lax.fori_loop(0, nchunks, step, 0)

    return pl.pallas_call(
        body, name="ffn_act_fwd", grid=(N_UP_PAIRS, nbatch),
        in_specs=[pl.BlockSpec((2, None, seq, UP_CHUNK), lambda i, b: (0, i, b, 0)),
                  pl.BlockSpec((2, None, 3, UP_CHUNK), lambda i, b: (0, i, 0, 0))],
        out_specs=pl.BlockSpec((None, seq, UP_CHUNK), lambda i, b: (i, b, 0)),
        out_shape=jax.ShapeDtypeStruct((N_UP_PAIRS, t, UP_CHUNK), BF16),
        compiler_params=_params(("parallel", "parallel")),
    )(pre, fcw)


def _ffn_act_bwd(pre, d_act, fcw, nbatch, seq):
    nchunks = seq // FFN_ROWS

    def body(pre_ref, da_ref, w_ref, dpre_ref, dw_ref):
        b = pl.program_id(1)
        wa = w_ref[0]
        wc = w_ref[1]
        cur = slice(HALO, HALO + FFN_ROWS)

        def step(c, carry):
            rows = pl.ds(pl.multiple_of(c * FFN_ROWS, FFN_ROWS), FFN_ROWS)
            pg = _window(pre_ref.at[0], c, FFN_ROWS, nchunks, True)
            pv = _window(pre_ref.at[1], c, FFN_ROWS, nchunks, True)
            dact = _window(da_ref, c, FFN_ROWS, nchunks, True)
            a = _conv(pg, wa)
            v = _conv(pv, wc)
            sg = _sigmoid(a)
            asg = a * sg
            dzg, dwg = _conv_bwd(dact * v * (sg + asg - asg * sg), pg, wa, cur)
            dzv, dwv = _conv_bwd(dact * asg, pv, wc, cur)
            dpre_ref[0, rows, :] = dzg[cur].astype(BF16)
            dpre_ref[1, rows, :] = dzv[cur].astype(BF16)
            return tuple(acc + d for acc, d in zip(carry, dwg + dwv))

        zero = jnp.zeros((1, UP_CHUNK), F32)
        sums = lax.fori_loop(0, nchunks, step, (zero,) * 6)

        @pl.when(b == 0)
        def _():
            for i in range(6):
                dw_ref[i // 3, pl.ds(i % 3, 1), :] = sums[i]

        @pl.when(b > 0)
        def _():
            for i in range(6):
                dw_ref[i // 3, pl.ds(i % 3, 1), :] += sums[i]

    pair = pl.BlockSpec((2, None, seq, UP_CHUNK), lambda i, b: (0, i, b, 0))
    wspec = pl.BlockSpec((2, None, 3, UP_CHUNK), lambda i, b: (0, i, 0, 0))
    return pl.pallas_call(
        body, name="ffn_act_bwd", grid=(N_UP_PAIRS, nbatch),
        in_specs=[pair, pl.BlockSpec((None, seq, UP_CHUNK), lambda i, b: (i, b, 0)), wspec],
        out_specs=[pair, wspec],
        out_shape=[jax.ShapeDtypeStruct(pre.shape, BF16), jax.ShapeDtypeStruct(fcw.shape, F32)],
        compiler_params=_params(("parallel", "arbitrary")),
    )(pre, d_act, fcw)


def _adamw(lands, w, m, v, row_tile, name):
    nl = len(lands)
    _, nr, ncol = lands[0].shape
    c1 = 1.0 - ADAM_B1 ** ADAM_STEP
    c2 = 1.0 - ADAM_B2 ** ADAM_STEP

    def body(*refs):
        land_refs = refs[:nl]
        w_ref, m_ref, v_ref, g_ref, d_ref, mo_ref, vo_ref = refs[nl:]
        for l in range(nl):
            @pl.when(pl.program_id(0) == l)
            def _(l=l):
                g = land_refs[l][0].astype(F32)
                for j in range(1, N_DEV):
                    g = g + land_refs[l][j].astype(F32)
                g_ref[...] = g

        g = g_ref[...]
        m2 = ADAM_B1 * m_ref[...] + (1.0 - ADAM_B1) * g
        v2 = ADAM_B2 * v_ref[...] + (1.0 - ADAM_B2) * (g * g)
        mo_ref[...] = m2
        vo_ref[...] = v2
        d_ref[...] = -ADAM_LR * ((m2 / c1) / (jnp.sqrt(v2 / c2) + ADAM_EPS) + ADAM_WD * w_ref[...])

    def land_spec(l):
        return pl.BlockSpec((N_DEV, row_tile, ncol), lambda k, i: (0, jnp.where(k == l, i, 0), 0))

    tile = pl.BlockSpec((None, row_tile, ncol), lambda k, i: (k, i, 0))
    return pl.pallas_call(
        body, name=name, grid=(nl, nr // row_tile),
        in_specs=[land_spec(l) for l in range(nl)] + [tile, tile, tile],
        out_specs=[tile] * 4,
        out_shape=[jax.ShapeDtypeStruct(w.shape, F32)] * 4,
        compiler_params=_params(("arbitrary", "arbitrary")),
    )(*lands, w, m, v)


class _Item:
    def __init__(self, src, chunked, land_cols=False):
        self.src, self.chunked, self.land_cols = src, chunked, land_cols
        if chunked == "cols":
            block = (src.shape[0], src.shape[1] // N_DEV)
        else:
            block = src.shape[1:] if chunked else src.shape
        self.width = block[-1]
        self.land_shape = (block[0], N_DEV * block[1]) if land_cols else (N_DEV,) + block

    def _cols(self, first, count=1):
        return pl.ds(pl.multiple_of(first * self.width, LANES), count * self.width)

    def part(self, src_ref, j):
        if self.chunked == "cols":
            return src_ref.at[:, self._cols(j)]
        return src_ref.at[j] if self.chunked else src_ref

    def slot(self, land_ref, s):
        return land_ref.at[:, self._cols(s)] if self.land_cols else land_ref.at[s]

    def seven(self, land_ref):
        return land_ref.at[:, self._cols(0, N_DEV - 1)] if self.land_cols else land_ref.at[pl.ds(0, N_DEV - 1)]


def _mesh_place():
    x, y, c = lax.axis_index("x"), lax.axis_index("y"), lax.axis_index("c")
    return x, y, c, 4 * x + 2 * y + c


def _flipped(x, y, c, k):
    px = 1 - x if k & 4 else x
    py = 1 - y if k & 2 else y
    pc = 1 - c if k & 1 else c
    return (px, py, pc), 4 * px + 2 * py + pc


PEER_ORDER = (2, 4, 6, 3, 5, 7, 1)


def _exchange(items, name):
    n = len(items)

    def body(*refs):
        srcs, lands = refs[:n], refs[n:2 * n]
        send, recv, local = refs[2 * n:]
        x, y, c, me = _mesh_place()

        def copy(i, k, chunk, slot, dev):
            return pltpu.make_async_remote_copy(
                src_ref=items[i].part(srcs[i], chunk), dst_ref=items[i].slot(lands[i], slot),
                send_sem=send.at[i, k - 1], recv_sem=recv.at[i, k - 1], device_id=dev, device_id_type=MESH)

        own = [pltpu.make_async_copy(items[i].part(srcs[i], me), items[i].slot(lands[i], me), local.at[i])
               for i in range(n)]
        for k in PEER_ORDER:
            dev, idx = _flipped(x, y, c, k)
            for i in range(n):
                copy(i, k, idx, me, dev).start()
        for cp in own:
            cp.start()
        for k in PEER_ORDER:
            dev, idx = _flipped(x, y, c, k)
            for i in range(n):
                copy(i, k, me, idx, dev).wait_recv()
        for k in PEER_ORDER:
            dev, idx = _flipped(x, y, c, k)
            for i in range(n):
                copy(i, k, idx, me, dev).wait_send()
        for cp in own:
            cp.wait()

    hbm = pl.BlockSpec(memory_space=pl.ANY)
    return pl.pallas_call(
        body, name=name,
        in_specs=[hbm] * n, out_specs=[hbm] * n,
        out_shape=[jax.ShapeDtypeStruct(it.land_shape, it.src.dtype) for it in items],
        scratch_shapes=[pltpu.SemaphoreType.DMA((n, N_DEV - 1)), pltpu.SemaphoreType.DMA((n, N_DEV - 1)),
                        pltpu.SemaphoreType.DMA((n,))],
        compiler_params=pltpu.CompilerParams(has_side_effects=True),
    )(*[it.src for it in items])


def _sequencer_exchange(items, name, collective_id):
    n = len(items)

    def body(*refs):
        srcs, lands = refs[:n], refs[n:2 * n]
        send, recv, local = refs[2 * n:]
        x, y, c, me = _mesh_place()
        barrier = pltpu.get_barrier_semaphore()
        for k in PEER_ORDER:
            pl.semaphore_signal(barrier, inc=1, device_id=_flipped(x, y, c, k)[0], device_id_type=MESH)
        pl.semaphore_wait(barrier, N_DEV - 1)

        def copy(i, k, chunk, slot, dev):
            return pltpu.make_async_remote_copy(
                src_ref=items[i].part(srcs[i], chunk), dst_ref=items[i].slot(lands[i], slot),
                send_sem=send.at[i, k - 1], recv_sem=recv.at[i, k - 1], device_id=dev, device_id_type=MESH)

        own = [pltpu.make_async_copy(items[i].part(srcs[i], me), items[i].slot(lands[i], me), local.at[i])
               for i in range(n)]
        for cp in own:
            cp.start()
        for k in PEER_ORDER:
            dev, idx = _flipped(x, y, c, k)
            for i in range(n):
                copy(i, k, idx, me, dev).start()
        for k in PEER_ORDER:
            dev, idx = _flipped(x, y, c, k)
            for i in range(n):
                copy(i, k, me, idx, dev).wait_recv()
        for k in PEER_ORDER:
            dev, idx = _flipped(x, y, c, k)
            for i in range(n):
                copy(i, k, idx, me, dev).wait_send()
        for cp in own:
            cp.wait()

    return pl.kernel(
        body, name=name,
        out_type=[jax.ShapeDtypeStruct(it.land_shape, it.src.dtype) for it in items],
        mesh=plsc.ScalarSubcoreMesh(axis_name="sequencer", num_cores=1),
        scratch_types=[pltpu.SemaphoreType.DMA((n, N_DEV - 1)), pltpu.SemaphoreType.DMA((n, N_DEV - 1)),
                       pltpu.SemaphoreType.DMA((n,))],
        compiler_params=pltpu.CompilerParams(collective_id=collective_id),
    )(*[it.src for it in items])


HBM_SPEC = pl.BlockSpec(memory_space=pltpu.HBM)
SEM_SPEC = pl.BlockSpec(memory_space=pltpu.SEMAPHORE)
DATAFLOW = pltpu.SideEffectType.DATAFLOW_SIDE_EFFECTING


def _exchange_start(items, name, after=()):
    n = len(items)
    na = len(after)

    def body(*refs):
        srcs, land_ins = refs[:n], refs[n:2 * n]
        outs = refs[2 * n + na:6 * n + na]
        (local,) = refs[6 * n + na:]
        del land_ins
        x, y, c, me = _mesh_place()
        own = [pltpu.make_async_copy(items[i].part(srcs[i], me), items[i].slot(outs[4 * i + 3], me), local.at[i])
               for i in range(n)]
        for cp in own:
            cp.start()
        for cp in own:
            cp.wait()
        for k in PEER_ORDER:
            dev, idx = _flipped(x, y, c, k)
            for i in range(n):
                send, recv, _, land = outs[4 * i:4 * i + 4]
                pltpu.make_async_remote_copy(
                    src_ref=items[i].part(srcs[i], idx), dst_ref=items[i].slot(land, me), send_sem=send, recv_sem=recv,
                    device_id=dev, device_id_type=MESH).start()

    out_shape, out_specs, args, lands = [], [], [], []
    for it in items:
        out_shape += [pltpu.SemaphoreType.DMA(()), pltpu.SemaphoreType.DMA(()),
                      pltpu.HBM(it.src.shape, it.src.dtype), pltpu.HBM(it.land_shape, it.src.dtype)]
        out_specs += [SEM_SPEC, SEM_SPEC, HBM_SPEC, HBM_SPEC]
        args.append(pltpu.with_memory_space_constraint(it.src, pltpu.HBM))
        lands.append(pltpu.with_memory_space_constraint(lax.empty(it.land_shape, it.src.dtype), pltpu.HBM))
    outs = pl.pallas_call(
        body, name=name,
        in_specs=[HBM_SPEC] * (2 * n) + [pl.BlockSpec(memory_space=pl.ANY)] * na,
        out_specs=out_specs, out_shape=out_shape,
        scratch_shapes=[pltpu.SemaphoreType.DMA((n,))],
        input_output_aliases={**{i: 4 * i + 2 for i in range(n)}, **{n + i: 4 * i + 3 for i in range(n)}},
        compiler_params=pltpu.CompilerParams(has_side_effects=DATAFLOW),
    )(*args, *lands, *after)
    return [tuple(outs[4 * i:4 * i + 4]) + (items[i],) for i in range(n)]


def _started(handles):
    return handles[0][2]


def _exchange_wait(handles, after, name):
    n = len(handles)

    def body(*refs):
        x, y, c, _ = _mesh_place()
        for i in range(n):
            src, land, send, recv = refs[4 * i:4 * i + 4]
            del src
            seven = handles[i][4].seven(land)
            cp = pltpu.make_async_remote_copy(src_ref=seven, dst_ref=seven, send_sem=send, recv_sem=recv,
                                              device_id=(x, y, 1 - c), device_id_type=MESH)
            cp.wait_send()
            cp.wait_recv()

    args, in_specs, out_shape = [], [], []
    for send, recv, src, land, _ in handles:
        args += [src, land, send, recv]
        in_specs += [HBM_SPEC, HBM_SPEC, SEM_SPEC, SEM_SPEC]
        out_shape += [pltpu.HBM(src.shape, src.dtype), pltpu.HBM(land.shape, land.dtype)]
    outs = pl.pallas_call(
        body, name=name,
        in_specs=in_specs + [pl.BlockSpec(memory_space=pl.ANY)] * len(after), out_specs=[HBM_SPEC] * (2 * n),
        out_shape=out_shape,
        input_output_aliases={**{4 * i: 2 * i for i in range(n)}, **{4 * i + 1: 2 * i + 1 for i in range(n)}},
        compiler_params=pltpu.CompilerParams(has_side_effects=DATAFLOW),
    )(*args, *after)
    return [outs[2 * i + 1] for i in range(n)]


TM = 1024
TM_ACC = 512
TN_IN = 768


def kernel(x, norm1_g, w_in, mix_conv_w, attn_out_g, conv_out_g, w_out, norm2_g, ffn_up, ffn_conv_w, ffn_down, final_norm_g, loss_target, m_norm1_g, m_w_in, m_mix_conv_w, m_attn_out_g, m_conv_out_g, m_w_out, m_norm2_g, m_ffn_up, m_ffn_conv_w, m_ffn_down, m_final_norm_g, v_norm1_g, v_w_in, v_mix_conv_w, v_attn_out_g, v_conv_out_g, v_w_out, v_norm2_g, v_ffn_up, v_ffn_conv_w, v_ffn_down, v_final_norm_g):
    nbatch, seq, d = x.shape
    t = nbatch * seq
    nt, nta = t // TM, t // TM_ACC
    out_rows = D_MODEL // N_DEV
    down_rows = D_FF // N_DEV
    xf = x.reshape(t, d)
    target = loss_target.reshape(t, d)

    cw_local = jnp.concatenate([ffn_conv_w, mix_conv_w], axis=-1)
    cast = lambda w: _Item(w.astype(BF16), False)
    cast_in = lambda w: _Item(w.astype(BF16), False, land_cols=True)
    cw_all, win0 = _sequencer_exchange([_Item(cw_local, False), cast_in(w_in[0])], "gather_a", 0)
    wout0, wup0 = _sequencer_exchange([cast(w_out[0]), cast(ffn_up[0])], "gather_b", 1)
    wdown0, win1, wout1 = _sequencer_exchange([cast(ffn_down[0]), cast_in(w_in[1]), cast(w_out[1])], "gather_c", 2)
    wup1, wdown1 = _sequencer_exchange([cast(ffn_up[1]), cast(ffn_down[1])], "gather_d", 3)
    win, wup = [win0, win1], [wup0, wup1]
    wout = [w.reshape(D_MODEL, D_MODEL) for w in (wout0, wout1)]
    wdown = [w.reshape(N_UP_PAIRS, UP_CHUNK, D_MODEL) for w in (wdown0, wdown1)]
    fcw = [cw_all[:, k, :, :UP_CHUNK].reshape(2, N_UP_PAIRS, 3, UP_CHUNK) for k in range(DEPTH)]
    mcw = [cw_all[:, k, :, UP_CHUNK:].transpose(1, 0, 2).reshape(3, D_CONV) for k in range(DEPTH)]

    full = lambda i, j, k: (0, 0)

    saved = []
    xin = xf
    for l in range(DEPTH):
        h1 = _rms_fwd(xin, norm1_g[l][None], f"rms1_fwd_{l}")
        proj = _matmul(
            h1, win[l], grid=(nt, D_IN // TN_IN, 1), dims=NN, name=f"proj_{l}",
            a_spec=pl.BlockSpec((TM, D_MODEL), lambda i, j, k: (i, 0)),
            b_spec=pl.BlockSpec((D_MODEL, TN_IN), lambda i, j, k: (0, j)),
            o_spec=pl.BlockSpec((TM, TN_IN), lambda i, j, k: (i, j)), o_shape=(t, D_IN), o_dtype=F32)
        o, lse, cat = _attn_fwd(proj, attn_out_g[l][None], nbatch, seq)
        cat = _convmix_fwd(proj, cat, mcw[l], conv_out_g[l][None], nbatch, seq)
        xmid = _matmul(
            cat, wout[l], grid=(nta, 1, 1), dims=NN, name=f"mix_out_{l}",
            a_spec=pl.BlockSpec((TM_ACC, D_MODEL), lambda i, j, k: (i, 0)),
            b_spec=pl.BlockSpec((D_MODEL, D_MODEL), full),
            o_spec=pl.BlockSpec((TM_ACC, D_MODEL), lambda i, j, k: (i, 0)), o_shape=(t, D_MODEL), o_dtype=F32,
            res=xin, res_spec=pl.BlockSpec((TM_ACC, D_MODEL), lambda i, j, k: (i, 0)))
        h2 = _rms_fwd(xmid, norm2_g[l][None], f"rms2_fwd_{l}")
        pre = _matmul(
            h2, wup[l], grid=(nt, N_DEV, 1), dims=NN, name=f"ffn_up_{l}",
            a_spec=pl.BlockSpec((TM, D_MODEL), lambda i, j, k: (i, 0)),
            b_spec=pl.BlockSpec((None, D_MODEL, UP_CHUNK), lambda i, j, k: (j, 0, 0)),
            o_spec=pl.BlockSpec((None, TM, UP_CHUNK), lambda i, j, k: (j, i, 0)),
            o_shape=(N_DEV, t, UP_CHUNK), o_dtype=BF16).reshape(2, N_UP_PAIRS, t, UP_CHUNK)
        act = _ffn_act_fwd(pre, fcw[l], nbatch, seq)
        xout = _matmul(
            act, wdown[l], grid=(nta, 1, 1), dims=NN, name=f"ffn_down_{l}",
            a_spec=pl.BlockSpec((N_UP_PAIRS, TM_ACC, UP_CHUNK), lambda i, j, k: (0, i, 0)),
            b_spec=pl.BlockSpec((N_UP_PAIRS, UP_CHUNK, D_MODEL), lambda i, j, k: (0, 0, 0)),
            o_spec=pl.BlockSpec((TM_ACC, D_MODEL), lambda i, j, k: (i, 0)), o_shape=(t, D_MODEL), o_dtype=F32,
            res=xmid, res_spec=pl.BlockSpec((TM_ACC, D_MODEL), lambda i, j, k: (i, 0)))
        saved.append((xin, h1, proj, o, lse, cat, xmid, h2, pre, act))
        xin = xout

    loss_part, dx, dxb, dgf = _loss_head(xin, final_norm_g[None], target, "loss_head")

    dg1, dg2, dga, dgc = [None] * DEPTH, [None] * DEPTH, [None] * DEPTH, [None] * DEPTH
    for l in reversed(range(DEPTH)):
        xin, h1, proj, o, lse, cat, xmid, h2, pre, act = saved[l]
        d_act = _matmul(
            dxb, wdown[l], grid=(nt, N_UP_PAIRS, 1), dims=NT, name=f"d_act_{l}",
            a_spec=pl.BlockSpec((TM, D_MODEL), lambda i, j, k: (i, 0)),
            b_spec=pl.BlockSpec((None, UP_CHUNK, D_MODEL), lambda i, j, k: (j, 0, 0)),
            o_spec=pl.BlockSpec((None, TM, UP_CHUNK), lambda i, j, k: (j, i, 0)),
            o_shape=(N_UP_PAIRS, t, UP_CHUNK), o_dtype=BF16)
        g_down = _matmul(
            act, dxb, grid=(N_UP_PAIRS, 1, 1), dims=TN, name=f"g_down_{l}",
            a_spec=pl.BlockSpec((None, t, UP_CHUNK), lambda i, j, k: (i, 0, 0)),
            b_spec=pl.BlockSpec((t, D_MODEL), full),
            o_spec=pl.BlockSpec((None, UP_CHUNK, D_MODEL), lambda i, j, k: (i, 0, 0)),
            o_shape=(N_UP_PAIRS, UP_CHUNK, D_MODEL), o_dtype=BF16).reshape(N_DEV, down_rows, D_MODEL)
        d_pre, d_fcw = _ffn_act_bwd(pre, d_act, fcw[l], nbatch, seq)
        d_pre = d_pre.reshape(N_DEV, t, UP_CHUNK)
        dh2 = _matmul(
            d_pre, wup[l], grid=(nta, 1, 1), dims=NT, name=f"d_h2_{l}",
            a_spec=pl.BlockSpec((N_DEV, TM_ACC, UP_CHUNK), lambda i, j, k: (0, i, 0)),
            b_spec=pl.BlockSpec((N_DEV, D_MODEL, UP_CHUNK), lambda i, j, k: (0, 0, 0)),
            o_spec=pl.BlockSpec((TM_ACC, D_MODEL), lambda i, j, k: (i, 0)), o_shape=(t, D_MODEL), o_dtype=F32)
        g_up = _matmul(
            h2, d_pre, grid=(1, N_DEV, 1), dims=TN, name=f"g_up_{l}",
            a_spec=pl.BlockSpec((t, D_MODEL), full),
            b_spec=pl.BlockSpec((None, t, UP_CHUNK), lambda i, j, k: (j, 0, 0)),
            o_spec=pl.BlockSpec((None, D_MODEL, UP_CHUNK), lambda i, j, k: (j, 0, 0)),
            o_shape=(N_DEV, D_MODEL, UP_CHUNK), o_dtype=BF16)
        dxm, dxmb, dg2[l] = _rms_bwd(xmid, norm2_g[l][None], dh2, dx, f"rms2_bwd_{l}")
        g_out = _matmul(
            cat, dxmb, grid=(1, 1, nt), dims=TN, name=f"g_out_{l}",
            a_spec=pl.BlockSpec((TM, D_MODEL), lambda i, j, k: (k, 0)),
            b_spec=pl.BlockSpec((TM, D_MODEL), lambda i, j, k: (k, 0)),
            o_spec=pl.BlockSpec((D_MODEL, D_MODEL), full),
            o_shape=(D_MODEL, D_MODEL), o_dtype=BF16).reshape(N_DEV, out_rows, D_MODEL)
        if l == 0:
            land_out0, land_up0, land_down0 = _sequencer_exchange(
                [_Item(g_out, True), _Item(g_up, True), _Item(g_down, True)], "scatter_0a", 5)
        d_cat = _matmul(
            dxmb, wout[l], grid=(nta, 1, 1), dims=NT, name=f"d_cat_{l}",
            a_spec=pl.BlockSpec((TM_ACC, D_MODEL), lambda i, j, k: (i, 0)),
            b_spec=pl.BlockSpec((D_MODEL, D_MODEL), full),
            o_spec=pl.BlockSpec((TM_ACC, D_MODEL), lambda i, j, k: (i, 0)), o_shape=(t, D_MODEL), o_dtype=BF16)
        d_proj, dga[l] = _attn_bwd(proj, o, lse, d_cat, attn_out_g[l][None], nbatch, seq)
        d_proj, d_mcw, dgc[l] = _convmix_bwd(proj, d_cat, d_proj, mcw[l], conv_out_g[l][None], nbatch, seq)
        g_in = _matmul(
            h1, d_proj, grid=(1, D_IN // TN_IN, 1), dims=TN, name=f"g_in_{l}",
            a_spec=pl.BlockSpec((t, D_MODEL), full),
            b_spec=pl.BlockSpec((t, TN_IN), lambda i, j, k: (0, j)),
            o_spec=pl.BlockSpec((D_MODEL, TN_IN), lambda i, j, k: (0, j)),
            o_shape=(D_MODEL, D_IN), o_dtype=BF16)
        g_cw = jnp.concatenate(
            [d_fcw.reshape(N_DEV, 3, UP_CHUNK), d_mcw.reshape(3, N_DEV, D_CONV // N_DEV).transpose(1, 0, 2)], axis=-1)
        if l == 0:
            land_in0, land_cw0 = _sequencer_exchange([_Item(g_in, "cols"), _Item(g_cw, True)], "scatter_0b", 6)
        else:
            land_in1, land_out1, land_up1, land_down1, land_cw1 = _sequencer_exchange(
                [_Item(g_in, "cols"), _Item(g_out, True), _Item(g_up, True), _Item(g_down, True), _Item(g_cw, True)],
                "scatter_1", 4)
        dh1 = _matmul(
            d_proj, win[l], grid=(nta, 1, 1), dims=NT, name=f"d_h1_{l}",
            a_spec=pl.BlockSpec((TM_ACC, D_IN), lambda i, j, k: (i, 0)),
            b_spec=pl.BlockSpec((D_MODEL, D_IN), full),
            o_spec=pl.BlockSpec((TM_ACC, D_MODEL), lambda i, j, k: (i, 0)), o_shape=(t, D_MODEL), o_dtype=F32)
        dx, dxb, dg1[l] = _rms_bwd(xin, norm1_g[l][None], dh1, dxm, f"rms1_bwd_{l}")

    def pack_small(n1, a, c, n2, f):
        return jnp.concatenate(
            [n1, n2, f[None], jnp.concatenate([a, c], axis=-1), jnp.zeros((1, D_MODEL), F32)], axis=0)[None]

    res_out = _adamw([land_out0, land_out1], w_out, m_w_out, v_w_out, out_rows, "adamw_w_out")
    res_up = _adamw([land_up0, land_up1], ffn_up, m_ffn_up, v_ffn_up, 256, "adamw_ffn_up")
    res_down = _adamw([land_down0, land_down1], ffn_down, m_ffn_down, v_ffn_down, down_rows, "adamw_ffn_down")
    small = jnp.concatenate(
        [dg1[0], dg1[1], dg2[0], dg2[1], dgf,
         jnp.concatenate([dga[0], dgc[0]], axis=-1), jnp.concatenate([dga[1], dgc[1]], axis=-1),
         jnp.zeros((1, D_MODEL), F32)], axis=0)
    (land_small,) = _exchange([_Item(small, False)], "gather_gain_grads")
    res_small = _adamw(
        [land_small], pack_small(norm1_g, attn_out_g, conv_out_g, norm2_g, final_norm_g),
        pack_small(m_norm1_g, m_attn_out_g, m_conv_out_g, m_norm2_g, m_final_norm_g),
        pack_small(v_norm1_g, v_attn_out_g, v_conv_out_g, v_norm2_g, v_final_norm_g), SUBLANES, "adamw_gains")
    res_in = _adamw([land_in0, land_in1], w_in, m_w_in, v_w_in, 256, "adamw_w_in")
    res_cw = _adamw(
        [land_cw0, land_cw1], cw_local, jnp.concatenate([m_ffn_conv_w, m_mix_conv_w], axis=-1),
        jnp.concatenate([v_ffn_conv_w, v_mix_conv_w], axis=-1), 3, "adamw_conv_w")

    loss = lax.psum(loss_part[0, 0], ("x", "y", "c"))

    def unpack(kind):
        s = res_small[kind][0]
        cwr = res_cw[kind]
        return (s[0:2], res_in[kind], cwr[..., UP_CHUNK:], s[5:7, :D_ATTN], s[5:7, D_ATTN:], res_out[kind],
                s[2:4], res_up[kind], cwr[..., :UP_CHUNK], res_down[kind], s[4])

    return (loss, dx.reshape(nbatch, seq, d), *unpack(0), *unpack(1), *unpack(2), *unpack(3))
```

```python
import math

import jax
import jax.numpy as jnp
from jax import lax
from jax.experimental import pallas as pl
from jax.experimental.pallas import tpu as pltpu
from jax.experimental.pallas import tpu_sc as plsc

F32 = jnp.float32
BF16 = jnp.bfloat16

D_MODEL = 1024
D_ATTN = 512
D_CONV = 512
HEAD_DIM = 64
N_HEADS = 8
D_FF = 2816
DEPTH = 2
D_IN = 3 * D_ATTN + 3 * D_CONV
EPS = 1e-6
DILATIONS = (1, 4, 16)
BAND = 128
N_DEV = 8
IN_CHUNK = D_IN // N_DEV
UP_CHUNK = 2 * D_FF // N_DEV
N_UP_PAIRS = N_DEV // 2
CW_PACK = UP_CHUNK + D_CONV // N_DEV
ADAM_LR = 0.001
ADAM_B1 = 0.9
ADAM_B2 = 0.999
ADAM_EPS = 1e-08
ADAM_WD = 0.01
ADAM_STEP = 10
LANES = 128
SUBLANES = 8
VMEM_LIMIT = 56 * 1024 * 1024

NEG = -1e30
MESH = pl.DeviceIdType.MESH


def _params(sem=None, vmem=VMEM_LIMIT):
    return pltpu.CompilerParams(dimension_semantics=sem, vmem_limit_bytes=vmem)


NN = (((1,), (0,)), ((), ()))
NT = (((1,), (1,)), ((), ()))
TN = (((0,), (0,)), ((), ()))


def _matmul(a, b, *, grid, a_spec, b_spec, o_spec, o_shape, o_dtype, dims, name, res=None, res_spec=None, after=()):
    nk = grid[2]
    o_block = tuple(s for s in o_spec.block_shape if s is not None)
    na = len(after)

    def body(*refs):
        refs = refs[:2 + (res is not None)] + refs[2 + (res is not None) + na:]
        if res is None:
            a_ref, b_ref, o_ref, *scr = refs
            r_ref = None
        else:
            a_ref, b_ref, r_ref, o_ref, *scr = refs
        def dot(av, bv):
            return lax.dot_general(av.astype(BF16), bv.astype(BF16), dims, preferred_element_type=F32)

        if len(a_ref.shape) == 3:
            part = dot(a_ref[0], b_ref[0])
            for c in range(1, a_ref.shape[0]):
                part = part + dot(a_ref[c], b_ref[c])
        else:
            part = dot(a_ref[...], b_ref[...])

        def finish(total):
            if r_ref is not None:
                total = total + r_ref[...]
            o_ref[...] = total.astype(o_dtype)

        if nk == 1:
            finish(part)
        else:
            acc = scr[0]
            k = pl.program_id(2)

            @pl.when(k == 0)
            def _():
                acc[...] = part

            @pl.when(k > 0)
            def _():
                acc[...] += part

            @pl.when(k == nk - 1)
            def _():
                finish(acc[...])

    in_specs = [a_spec, b_spec] + ([res_spec] if res is not None else []) + [pl.BlockSpec(memory_space=pl.ANY)] * na
    args = (a, b) + ((res,) if res is not None else ()) + tuple(after)
    return pl.pallas_call(
        body, name=name, grid=grid, in_specs=in_specs, out_specs=o_spec,
        out_shape=jax.ShapeDtypeStruct(o_shape, o_dtype),
        scratch_shapes=[pltpu.VMEM(o_block, F32)] if nk > 1 else [],
        compiler_params=_params(("parallel", "parallel", "arbitrary")),
    )(*args)


ROW_TILE = 512


def _rms_fwd(x, g, name):
    t, d = x.shape

    def body(x_ref, g_ref, h_ref):
        xv = x_ref[...]
        r = lax.rsqrt(jnp.mean(xv * xv, axis=-1, keepdims=True) + EPS)
        h_ref[...] = (xv * r * g_ref[...]).astype(BF16)

    return pl.pallas_call(
        body, name=name, grid=(t // ROW_TILE,),
        in_specs=[pl.BlockSpec((ROW_TILE, d), lambda i: (i, 0)), pl.BlockSpec((1, d), lambda i: (0, 0))],
        out_specs=pl.BlockSpec((ROW_TILE, d), lambda i: (i, 0)),
        out_shape=jax.ShapeDtypeStruct((t, d), BF16),
        compiler_params=_params(("parallel",)),
    )(x, g)


def _rms_bwd(x, g, dh, dres, name):
    t, d = x.shape

    def body(x_ref, g_ref, dh_ref, dres_ref, dx_ref, dxb_ref, dg_ref):
        xv = x_ref[...]
        r = lax.rsqrt(jnp.mean(xv * xv, axis=-1, keepdims=True) + EPS)
        xh = xv * r
        dhv = dh_ref[...]
        gd = dhv * g_ref[...]
        dx = r * (gd - xh * jnp.mean(gd * xh, axis=-1, keepdims=True)) + dres_ref[...]
        dx_ref[...] = dx
        dxb_ref[...] = dx.astype(BF16)
        part = jnp.sum(dhv * xh, axis=0, keepdims=True)

        @pl.when(pl.program_id(0) == 0)
        def _():
            dg_ref[...] = part

        @pl.when(pl.program_id(0) > 0)
        def _():
            dg_ref[...] += part

    row = pl.BlockSpec((ROW_TILE, d), lambda i: (i, 0))
    vec = pl.BlockSpec((1, d), lambda i: (0, 0))
    return pl.pallas_call(
        body, name=name, grid=(t // ROW_TILE,),
        in_specs=[row, vec, row, row], out_specs=[row, row, vec],
        out_shape=[jax.ShapeDtypeStruct((t, d), F32), jax.ShapeDtypeStruct((t, d), BF16),
                   jax.ShapeDtypeStruct((1, d), F32)],
        compiler_params=_params(("arbitrary",)),
    )(x, g, dh, dres)


def _loss_head(x, g, target, name):
    t, d = x.shape

    def body(x_ref, g_ref, t_ref, loss_ref, dx_ref, dxb_ref, dg_ref):
        xv = x_ref[...]
        r = lax.rsqrt(jnp.mean(xv * xv, axis=-1, keepdims=True) + EPS)
        xh = xv * r
        gv = g_ref[...]
        err = xh * gv - t_ref[...]
        loss = jnp.full((1, LANES), 0.5 / d, F32) * jnp.sum(err * err)
        dy = err * (1.0 / d)
        gd = dy * gv
        dx = r * (gd - xh * jnp.mean(gd * xh, axis=-1, keepdims=True))
        dx_ref[...] = dx
        dxb_ref[...] = dx.astype(BF16)
        part = jnp.sum(dy * xh, axis=0, keepdims=True)

        @pl.when(pl.program_id(0) == 0)
        def _():
            dg_ref[...] = part
            loss_ref[...] = loss

        @pl.when(pl.program_id(0) > 0)
        def _():
            dg_ref[...] += part
            loss_ref[...] += loss

    row = pl.BlockSpec((ROW_TILE, d), lambda i: (i, 0))
    vec = pl.BlockSpec((1, d), lambda i: (0, 0))
    return pl.pallas_call(
        body, name=name, grid=(t // ROW_TILE,),
        in_specs=[row, vec, row],
        out_specs=[pl.BlockSpec((1, LANES), lambda i: (0, 0)), row, row, vec],
        out_shape=[jax.ShapeDtypeStruct((1, LANES), F32), jax.ShapeDtypeStruct((t, d), F32),
                   jax.ShapeDtypeStruct((t, d), BF16), jax.ShapeDtypeStruct((1, d), F32)],
        compiler_params=_params(("arbitrary",)),
    )(x, g, target)


def _group_matrix(n):
    shift = int(math.log2(HEAD_DIM))
    r = lax.broadcasted_iota(jnp.int32, (n, n), 0) >> shift
    c = lax.broadcasted_iota(jnp.int32, (n, n), 1) >> shift
    return (r == c).astype(BF16)


def _group_sum(v, gmat):
    hi = v.astype(BF16)
    lo = (v - hi.astype(F32)).astype(BF16)

    def dot(p):
        return jnp.dot(p, gmat, preferred_element_type=F32)

    return dot(hi) + dot(lo)


def _shift_rows(ext, k):
    return pltpu.roll(ext, k % ext.shape[0], 0)


def _store_columns(stage, out_hbm, sems, row0, nrows, col_blocks):
    rows = pl.ds(pl.multiple_of(row0, SUBLANES * 2), nrows)
    copies = [
        pltpu.make_async_copy(stage.at[i], out_hbm.at[rows, pl.ds(pl.multiple_of(cb * LANES, LANES), LANES)], sems.at[i])
        for i, cb in enumerate(col_blocks)
    ]
    for cp in copies:
        cp.start()
    for cp in copies:
        cp.wait()


def _attn_consts(width):
    i = lax.broadcasted_iota(jnp.int32, (BAND, width), 0)
    j = lax.broadcasted_iota(jnp.int32, (BAND, width), 1)
    dist = (width - BAND) + i - j
    inwin = (dist >= 0) & (dist <= BAND)
    return dist.astype(F32), inwin, j


def _head_masks():
    lane = lax.broadcasted_iota(jnp.int32, (1, LANES), 1)
    return [(lane < HEAD_DIM).astype(F32), (lane >= HEAD_DIM).astype(F32)]


def _pair_bias(slope, dil):
    distf, inwin, _ = _attn_consts(2 * BAND)
    return jnp.concatenate([jnp.where(inwin, distf * (slope[hh] * (-float(dil))), NEG) for hh in range(2)], axis=0)


def _stack_heads(xv, hmask):
    return jnp.concatenate([xv * hmask[0], xv * hmask[1]], axis=0).astype(BF16)


FWD_UNROLL = 8
BWD_UNROLL = 8


def _unroll(trips, most):
    return max(u for u in range(1, most + 1) if trips % u == 0)


def _for_blocks(seq, dil, block, most):
    nb = seq // dil // BAND

    def residue(r, carry):
        base = r * nb
        block(pl.multiple_of(base * BAND, BAND), None)
        if nb > 1:
            def rest(n, c):
                block(pl.multiple_of((base + n) * BAND, BAND), pl.multiple_of((base + n - 1) * BAND, BAND))
                return c

            lax.fori_loop(1, nb, rest, 0, unroll=_unroll(nb - 1, most))
        return carry

    if dil == 1:
        residue(0, 0)
    else:
        lax.fori_loop(0, dil, residue, 0, unroll=_unroll(dil, max(1, most // nb)))


def _permute_in(src_ref, dst_ref, dil, seq):
    length = seq // dil
    for r in range(dil):
        dst_ref[pl.ds(r * length, length), :] = src_ref[pl.ds(r, length, stride=dil), :].astype(dst_ref.dtype)


def _slopes_table():
    slopes = 2.0 ** (-8.0 * jnp.arange(1, N_HEADS + 1, dtype=F32) / N_HEADS)
    return jnp.broadcast_to(slopes[:, None], (N_HEADS, 2 * BAND))


def _attn_fwd(proj, attn_g, nbatch, seq):
    t = nbatch * seq
    scale = HEAD_DIM ** -0.5

    def body(q_ref, k_ref, v_ref, g_ref, sl_ref, o_ref, lse_ref, cat_ref, pq, pk, pv, po, pm, pll, ao, am, al):
        hp = pl.program_id(1)
        hmask = _head_masks()
        slope = [sl_ref[pl.ds(2 * hp + hh, 1), :] for hh in range(2)]

        def run_branch(dil, qs, ks, vs, osink, msink, lsink):
            bias = _pair_bias(slope, dil)

            def block(row0, prow):
                cur = pl.ds(row0, BAND)
                q2 = _stack_heads(qs[cur, :] * scale, hmask)
                if prow is None:
                    kk, vv, bias_b = ks[cur, :], vs[cur, :], bias[:, BAND:]
                else:
                    prev = pl.ds(prow, BAND)
                    kk = jnp.concatenate([ks[prev, :], ks[cur, :]], axis=0)
                    vv = jnp.concatenate([vs[prev, :], vs[cur, :]], axis=0)
                    bias_b = bias
                s = lax.dot_general(q2, kk.astype(BF16), NT, preferred_element_type=F32) + bias_b
                m = jnp.max(s, axis=1, keepdims=True)
                p = jnp.exp(s - m)
                l = jnp.sum(p, axis=1, keepdims=True)
                pb = p.astype(BF16)
                o = jnp.dot(jnp.concatenate([pb[:BAND], pb[BAND:]], axis=1), _stack_heads(vv, hmask),
                            preferred_element_type=F32)
                osink[cur, :] = o
                msink[cur, :] = m[:BAND] * hmask[0] + m[BAND:] * hmask[1]
                lsink[cur, :] = l[:BAND] * hmask[0] + l[BAND:] * hmask[1]

            _for_blocks(seq, dil, block, FWD_UNROLL)

        run_branch(1, q_ref, k_ref, v_ref, ao, am, al)
        for dil in DILATIONS[1:]:
            length = seq // dil
            _permute_in(q_ref, pq, dil, seq)
            _permute_in(k_ref, pk, dil, seq)
            _permute_in(v_ref, pv, dil, seq)
            run_branch(dil, pq, pk, pv, po, pm, pll)
            for r in range(dil):
                nat = pl.ds(r, length, stride=dil)
                per = pl.ds(r * length, length)
                m0 = am[nat, :]
                mb = pm[per, :]
                mn = jnp.maximum(m0, mb)
                e0 = jnp.exp(m0 - mn)
                eb = jnp.exp(mb - mn)
                ao[nat, :] = ao[nat, :] * e0 + po[per, :] * eb
                al[nat, :] = al[nat, :] * e0 + pll[per, :] * eb
                am[nat, :] = mn

        gmat = _group_matrix(LANES)
        gv = g_ref[...]

        def fin(c, carry):
            rows = pl.ds(pl.multiple_of(c * 256, 256), 256)
            lv = al[rows, :]
            o = ao[rows, :] / lv
            o_ref[rows, :] = o
            lse_ref[rows, :] = am[rows, :] + jnp.log(lv)
            ms = _group_sum(o * o, gmat) * (1.0 / HEAD_DIM)
            cat_ref[rows, :] = (o * lax.rsqrt(ms + EPS) * gv).astype(BF16)
            return carry

        lax.fori_loop(0, seq // 256, fin, 0)

    nq = D_ATTN // LANES
    blk = lambda off: pl.BlockSpec((seq, LANES), lambda b, h: (b, h + off))
    scratch = [pltpu.VMEM((seq, LANES), F32) for _ in range(9)]
    return pl.pallas_call(
        body, name="attn_fwd", grid=(nbatch, nq),
        in_specs=[blk(0), blk(nq), blk(2 * nq), pl.BlockSpec((1, LANES), lambda b, h: (0, h)),
                  pl.BlockSpec((N_HEADS, 2 * BAND), lambda b, h: (0, 0))],
        out_specs=[blk(0), blk(0), blk(0)],
        out_shape=[jax.ShapeDtypeStruct((t, D_ATTN), F32), jax.ShapeDtypeStruct((t, D_ATTN), F32),
                   jax.ShapeDtypeStruct((t, D_MODEL), BF16)],
        scratch_shapes=scratch,
        compiler_params=_params(("parallel", "parallel")),
    )(proj, proj, proj, attn_g, _slopes_table())


def _attn_bwd(proj, o, lse, d_cat, attn_g, nbatch, seq):
    t = nbatch * seq
    scale = HEAD_DIM ** -0.5

    def body(q_ref, k_ref, v_ref, o_ref, lse_ref, dy_ref, g_ref, sl_ref, dproj_ref, dg_ref,
             do_n, dl_n, dq_n, dk_n, dv_n, pq, pk, pv, pdo, plse, pdl, pdq, pdk, pdv, stage, sems):
        hp = pl.program_id(0)
        hmask = _head_masks()
        slope = [sl_ref[pl.ds(2 * hp + hh, 1), :] for hh in range(2)]
        gmat = _group_matrix(LANES)
        gv = g_ref[...]

        def prep(c, dg):
            rows = pl.ds(pl.multiple_of(c * 256, 256), 256)
            ov = o_ref[rows, :]
            dyn = dy_ref[rows, :].astype(F32)
            r = lax.rsqrt(_group_sum(ov * ov, gmat) * (1.0 / HEAD_DIM) + EPS)
            gd = dyn * gv
            oh = ov * r
            do = r * (gd - oh * (_group_sum(gd * oh, gmat) * (1.0 / HEAD_DIM)))
            do_n[rows, :] = do
            dl_n[rows, :] = _group_sum(do * ov, gmat)
            return dg + jnp.sum(dyn * oh, axis=0, keepdims=True)

        dg = lax.fori_loop(0, seq // 256, prep, jnp.zeros((1, LANES), F32))

        @pl.when(pl.program_id(1) == 0)
        def _():
            dg_ref[...] = dg

        @pl.when(pl.program_id(1) > 0)
        def _():
            dg_ref[...] += dg

        def clear(*refs):
            def step(c, carry):
                rows = pl.ds(pl.multiple_of(c * 256, 256), 256)
                for ref in refs:
                    ref[rows, :] = jnp.zeros((256, LANES), F32)
                return carry

            lax.fori_loop(0, seq // 256, step, 0)

        clear(dq_n, dk_n, dv_n)

        def run_branch(dil, qs, ks, vs, dos, lses, dls, dqs, dks, dvs):
            bias = _pair_bias(slope, dil)

            def per_head(xv):
                return jnp.concatenate([xv[:, 0:1], xv[:, HEAD_DIM:HEAD_DIM + 1]], axis=0)

            def block(row0, prow):
                cur = pl.ds(row0, BAND)
                keys = cur if prow is None else pl.ds(prow, 2 * BAND)
                q2 = _stack_heads(qs[cur, :] * scale, hmask)
                do2 = _stack_heads(dos[cur, :], hmask)
                kk, vv = ks[keys, :], vs[keys, :]
                s = lax.dot_general(q2, kk.astype(BF16), NT, preferred_element_type=F32)
                s = s + (bias[:, BAND:] if prow is None else bias)
                p = jnp.exp(s - per_head(lses[cur, :]))
                dp = lax.dot_general(do2, vv.astype(BF16), NT, preferred_element_type=F32)
                ds = (p * (dp - per_head(dls[cur, :]))).astype(BF16)
                dqs[cur, :] += jnp.dot(jnp.concatenate([ds[:BAND], ds[BAND:]], axis=1), _stack_heads(kk, hmask),
                                       preferred_element_type=F32)
                dks[keys, :] += lax.dot_general(ds, q2, TN, preferred_element_type=F32)
                dvs[keys, :] += lax.dot_general(p.astype(BF16), do2, TN, preferred_element_type=F32)

            _for_blocks(seq, dil, block, BWD_UNROLL)

        run_branch(1, q_ref, k_ref, v_ref, do_n, lse_ref, dl_n, dq_n, dk_n, dv_n)
        for dil in DILATIONS[1:]:
            length = seq // dil
            for src, dst in ((q_ref, pq), (k_ref, pk), (v_ref, pv), (do_n, pdo), (lse_ref, plse), (dl_n, pdl)):
                _permute_in(src, dst, dil, seq)
            clear(pdq, pdk, pdv)
            run_branch(dil, pq, pk, pv, pdo, plse, pdl, pdq, pdk, pdv)
            for r in range(dil):
                nat = pl.ds(r, length, stride=dil)
                per = pl.ds(r * length, length)
                dq_n[nat, :] += pdq[per, :]
                dk_n[nat, :] += pdk[per, :]
                dv_n[nat, :] += pdv[per, :]

        def emit(c, carry):
            rows = pl.ds(pl.multiple_of(c * 256, 256), 256)
            stage[0, rows, :] = (dq_n[rows, :] * scale).astype(BF16)
            stage[1, rows, :] = dk_n[rows, :].astype(BF16)
            stage[2, rows, :] = dv_n[rows, :].astype(BF16)
            return carry

        lax.fori_loop(0, seq // 256, emit, 0)
        _store_columns(stage, dproj_ref, sems, pl.program_id(1) * seq, seq, [hp, nq + hp, 2 * nq + hp])

    nq = D_ATTN // LANES
    blk = lambda off: pl.BlockSpec((seq, LANES), lambda h, b: (b, h + off))
    vec = pl.BlockSpec((1, LANES), lambda h, b: (0, h))
    scratch = [pltpu.VMEM((seq, LANES), F32) for _ in range(14)]
    scratch += [pltpu.VMEM((3, seq, LANES), BF16), pltpu.SemaphoreType.DMA((3,))]
    d_proj, dg = pl.pallas_call(
        body, name="attn_bwd", grid=(nq, nbatch),
        in_specs=[blk(0), blk(nq), blk(2 * nq), blk(0), blk(0), blk(0), vec,
                  pl.BlockSpec((N_HEADS, 2 * BAND), lambda h, b: (0, 0))],
        out_specs=[pl.BlockSpec(memory_space=pl.ANY), vec],
        out_shape=[jax.ShapeDtypeStruct((t, D_IN), BF16), jax.ShapeDtypeStruct((1, D_ATTN), F32)],
        scratch_shapes=scratch,
        compiler_params=_params(("arbitrary", "arbitrary")),
    )(proj, proj, proj, o, lse, d_cat, attn_g, _slopes_table())
    return d_proj, dg


HALO = SUBLANES
PACKED_ROWS = 2 * SUBLANES


def _window(ref, c, rows, nchunks, after):
    row0 = pl.multiple_of(c * rows, rows)
    prev0 = pl.multiple_of(jnp.maximum(row0 - PACKED_ROWS, 0), PACKED_ROWS)
    before = ref[pl.ds(prev0, PACKED_ROWS), :].astype(F32)[PACKED_ROWS - HALO:] * (c > 0).astype(F32)
    parts = [before, ref[pl.ds(row0, rows), :].astype(F32)]
    if after:
        next0 = pl.multiple_of(jnp.minimum(row0 + rows, (nchunks - 1) * rows), PACKED_ROWS)
        parts.append(ref[pl.ds(next0, PACKED_ROWS), :].astype(F32)[:HALO] * (c < nchunks - 1).astype(F32))
    return jnp.concatenate(parts, axis=0)


def _behind(z):
    z1 = _shift_rows(z, 1)
    return z1, _shift_rows(z1, 1)


def _ahead(dy):
    d1 = _shift_rows(dy, -1)
    return d1, _shift_rows(d1, -1)


def _conv(z, w):
    z1, z2 = _behind(z)
    return w[0:1] * z2 + w[1:2] * z1 + w[2:3] * z


def _conv_bwd(dy, z, w, cur):
    d1, d2 = _ahead(dy)
    dz = w[2:3] * dy + w[1:2] * d1 + w[0:1] * d2
    return dz, [jnp.sum((d * z)[cur], axis=0, keepdims=True) for d in (d2, d1, dy)]


def _sigmoid(a):
    return 0.5 * jnp.tanh(0.5 * a) + 0.5


MIX_ROWS = 256
GATE_B_BLOCK = 3 * D_ATTN // LANES
GATE_C_BLOCK = GATE_B_BLOCK + D_CONV // LANES
U_BLOCK = GATE_C_BLOCK + D_CONV // LANES


def _convmix_fwd(proj, cat, mcw, conv_g, nbatch, seq):
    nchunks = seq // MIX_ROWS

    def body(gb_ref, gc_ref, u_ref, w_ref, g_ref, cat_in, cat_ref):
        del cat_in
        gmat = _group_matrix(LANES)
        w = w_ref[...]
        gv = g_ref[...]

        def step(c, carry):
            cur = pl.ds(pl.multiple_of(c * MIX_ROWS, MIX_ROWS), MIX_ROWS)
            z = _window(gc_ref, c, MIX_ROWS, nchunks, False) * _window(u_ref, c, MIX_ROWS, nchunks, False)
            y = gb_ref[cur, :] * _conv(z, w)[HALO:]
            ms = _group_sum(y * y, gmat) * (1.0 / HEAD_DIM)
            cat_ref[cur, :] = (y * lax.rsqrt(ms + EPS) * gv).astype(BF16)
            return carry

        lax.fori_loop(0, nchunks, step, 0)

    nc = D_CONV // LANES
    blk = lambda off: pl.BlockSpec((seq, LANES), lambda b, j: (b, j + off))
    return pl.pallas_call(
        body, name="convmix_fwd", grid=(nbatch, nc),
        in_specs=[blk(GATE_B_BLOCK), blk(GATE_C_BLOCK), blk(U_BLOCK),
                  pl.BlockSpec((3, LANES), lambda b, j: (0, j)), pl.BlockSpec((1, LANES), lambda b, j: (0, j)),
                  pl.BlockSpec(memory_space=pl.ANY)],
        out_specs=blk(D_ATTN // LANES),
        out_shape=jax.ShapeDtypeStruct(cat.shape, cat.dtype),
        input_output_aliases={5: 0},
        compiler_params=_params(("parallel", "parallel")),
    )(proj, proj, proj, mcw, conv_g, cat)


def _convmix_bwd(proj, d_cat, d_proj, mcw, conv_g, nbatch, seq):
    nchunks = seq // MIX_ROWS

    def body(gb_ref, gc_ref, u_ref, dy_ref, w_ref, g_ref, dproj_in, dproj_ref, dw_ref, dg_ref, stage, sems):
        del dproj_in
        cb = pl.program_id(0)
        b = pl.program_id(1)
        gmat = _group_matrix(LANES)
        w = w_ref[...]
        gv = g_ref[...]
        cur = slice(HALO, HALO + MIX_ROWS)

        def step(c, carry):
            rows = pl.ds(pl.multiple_of(c * MIX_ROWS, MIX_ROWS), MIX_ROWS)
            gb = _window(gb_ref, c, MIX_ROWS, nchunks, True)
            gc = _window(gc_ref, c, MIX_ROWS, nchunks, True)
            u = _window(u_ref, c, MIX_ROWS, nchunks, True)
            dyn = _window(dy_ref, c, MIX_ROWS, nchunks, True)
            z = gc * u
            conv = _conv(z, w)
            y = gb * conv
            r = lax.rsqrt(_group_sum(y * y, gmat) * (1.0 / HEAD_DIM) + EPS)
            yh = y * r
            gd = dyn * gv
            dy = r * (gd - yh * (_group_sum(gd * yh, gmat) * (1.0 / HEAD_DIM)))
            dz, dws = _conv_bwd(dy * gb, z, w, cur)
            stage[0, rows, :] = (dy * conv)[cur].astype(BF16)
            stage[1, rows, :] = (dz * u)[cur].astype(BF16)
            stage[2, rows, :] = (dz * gc)[cur].astype(BF16)
            dg = jnp.sum((dyn * yh)[cur], axis=0, keepdims=True)
            return tuple(a + d for a, d in zip(carry, dws + [dg]))

        zero = jnp.zeros((1, LANES), F32)
        dw0, dw1, dw2, dg = lax.fori_loop(0, nchunks, step, (zero, zero, zero, zero))

        @pl.when(b == 0)
        def _():
            dw_ref[0:1, :] = dw0
            dw_ref[1:2, :] = dw1
            dw_ref[2:3, :] = dw2
            dg_ref[...] = dg

        @pl.when(b > 0)
        def _():
            dw_ref[0:1, :] += dw0
            dw_ref[1:2, :] += dw1
            dw_ref[2:3, :] += dw2
            dg_ref[...] += dg

        _store_columns(stage, dproj_ref, sems, b * seq, seq, [GATE_B_BLOCK + cb, GATE_C_BLOCK + cb, U_BLOCK + cb])

    nc = D_CONV // LANES
    blk = lambda off: pl.BlockSpec((seq, LANES), lambda j, b: (b, j + off))
    return pl.pallas_call(
        body, name="convmix_bwd", grid=(nc, nbatch),
        in_specs=[blk(GATE_B_BLOCK), blk(GATE_C_BLOCK), blk(U_BLOCK), blk(D_ATTN // LANES),
                  pl.BlockSpec((3, LANES), lambda j, b: (0, j)), pl.BlockSpec((1, LANES), lambda j, b: (0, j)),
                  pl.BlockSpec(memory_space=pl.ANY)],
        out_specs=[pl.BlockSpec(memory_space=pl.ANY), pl.BlockSpec((3, LANES), lambda j, b: (0, j)),
                   pl.BlockSpec((1, LANES), lambda j, b: (0, j))],
        out_shape=[jax.ShapeDtypeStruct(d_proj.shape, d_proj.dtype), jax.ShapeDtypeStruct((3, D_CONV), F32),
                   jax.ShapeDtypeStruct((1, D_CONV), F32)],
        scratch_shapes=[pltpu.VMEM((3, seq, LANES), BF16), pltpu.SemaphoreType.DMA((3,))],
        input_output_aliases={6: 0},
        compiler_params=_params(("arbitrary", "arbitrary")),
    )(proj, proj, proj, d_cat, mcw, conv_g, d_proj)


FFN_ROWS = 256


def _ffn_act_fwd(pre, fcw, nbatch, seq):
    t = nbatch * seq
    nchunks = seq // FFN_ROWS

    def body(pre_ref, w_ref, act_ref):
        wa = w_ref[0]
        wc = w_ref[1]

        def step(c, carry):
            cur = pl.ds(pl.multiple_of(c * FFN_ROWS, FFN_ROWS), FFN_ROWS)
            a = _conv(_window(pre_ref.at[0], c, FFN_ROWS, nchunks, False), wa)[HALO:]
            v = _conv(_window(pre_ref.at[1], c, FFN_ROWS, nchunks, False), wc)[HALO:]
            act_ref[cur, :] = (a * _sigmoid(a) * v).astype(BF16)
            return carry

        lax.fori_loop(0, nchunks, step, 0)

    return pl.pallas_call(
        body, name="ffn_act_fwd", grid=(N_UP_PAIRS, nbatch),
        in_specs=[pl.BlockSpec((2, None, seq, UP_CHUNK), lambda i, b: (0, i, b, 0)),
                  pl.BlockSpec((2, None, 3, UP_CHUNK), lambda i, b: (0, i, 0, 0))],
        out_specs=pl.BlockSpec((None, seq, UP_CHUNK), lambda i, b: (i, b, 0)),
        out_shape=jax.ShapeDtypeStruct((N_UP_PAIRS, t, UP_CHUNK), BF16),
        compiler_params=_params(("parallel", "parallel")),
    )(pre, fcw)


def _ffn_act_bwd(pre, d_act, fcw, nbatch, seq):
    nchunks = seq // FFN_ROWS

    def body(pre_ref, da_ref, w_ref, dpre_ref, dw_ref):
        b = pl.program_id(1)
        wa = w_ref[0]
        wc = w_ref[1]
        cur = slice(HALO, HALO + FFN_ROWS)

        def step(c, carry):
            rows = pl.ds(pl.multiple_of(c * FFN_ROWS, FFN_ROWS), FFN_ROWS)
            pg = _window(pre_ref.at[0], c, FFN_ROWS, nchunks, True)
            pv = _window(pre_ref.at[1], c, FFN_ROWS, nchunks, True)
            dact = _window(da_ref, c, FFN_ROWS, nchunks, True)
            a = _conv(pg, wa)
            v = _conv(pv, wc)
            sg = _sigmoid(a)
            asg = a * sg
            dzg, dwg = _conv_bwd(dact * v * (sg + asg - asg * sg), pg, wa, cur)
            dzv, dwv = _conv_bwd(dact * asg, pv, wc, cur)
            dpre_ref[0, rows, :] = dzg[cur].astype(BF16)
            dpre_ref[1, rows, :] = dzv[cur].astype(BF16)
            return tuple(acc + d for acc, d in zip(carry, dwg + dwv))

        zero = jnp.zeros((1, UP_CHUNK), F32)
        sums = lax.fori_loop(0, nchunks, step, (zero,) * 6)

        @pl.when(b == 0)
        def _():
            for i in range(6):
                dw_ref[i // 3, pl.ds(i % 3, 1), :] = sums[i]

        @pl.when(b > 0)
        def _():
            for i in range(6):
                dw_ref[i // 3, pl.ds(i % 3, 1), :] += sums[i]

    pair = pl.BlockSpec((2, None, seq, UP_CHUNK), lambda i, b: (0, i, b, 0))
    wspec = pl.BlockSpec((2, None, 3, UP_CHUNK), lambda i, b: (0, i, 0, 0))
    return pl.pallas_call(
        body, name="ffn_act_bwd", grid=(N_UP_PAIRS, nbatch),
        in_specs=[pair, pl.BlockSpec((None, seq, UP_CHUNK), lambda i, b: (i, b, 0)), wspec],
        out_specs=[pair, wspec],
        out_shape=[jax.ShapeDtypeStruct(pre.shape, BF16), jax.ShapeDtypeStruct(fcw.shape, F32)],
        compiler_params=_params(("parallel", "arbitrary")),
    )(pre, d_act, fcw)


def _adamw(lands, w, m, v, row_tile, name):
    nl = len(lands)
    _, nr, ncol = lands[0].shape
    c1 = 1.0 - ADAM_B1 ** ADAM_STEP
    c2 = 1.0 - ADAM_B2 ** ADAM_STEP

    def body(*refs):
        land_refs = refs[:nl]
        w_ref, m_ref, v_ref, g_ref, d_ref, mo_ref, vo_ref = refs[nl:]
        for l in range(nl):
            @pl.when(pl.program_id(0) == l)
            def _(l=l):
                g = land_refs[l][0].astype(F32)
                for j in range(1, N_DEV):
                    g = g + land_refs[l][j].astype(F32)
                g_ref[...] = g

        g = g_ref[...]
        m2 = ADAM_B1 * m_ref[...] + (1.0 - ADAM_B1) * g
        v2 = ADAM_B2 * v_ref[...] + (1.0 - ADAM_B2) * (g * g)
        mo_ref[...] = m2
        vo_ref[...] = v2
        d_ref[...] = -ADAM_LR * ((m2 / c1) / (jnp.sqrt(v2 / c2) + ADAM_EPS) + ADAM_WD * w_ref[...])

    def land_spec(l):
        return pl.BlockSpec((N_DEV, row_tile, ncol), lambda k, i: (0, jnp.where(k == l, i, 0), 0))

    tile = pl.BlockSpec((None, row_tile, ncol), lambda k, i: (k, i, 0))
    return pl.pallas_call(
        body, name=name, grid=(nl, nr // row_tile),
        in_specs=[land_spec(l) for l in range(nl)] + [tile, tile, tile],
        out_specs=[tile] * 4,
        out_shape=[jax.ShapeDtypeStruct(w.shape, F32)] * 4,
        compiler_params=_params(("arbitrary", "arbitrary")),
    )(*lands, w, m, v)


class _Item:
    def __init__(self, src, chunked, land_cols=False):
        self.src, self.chunked, self.land_cols = src, chunked, land_cols
        if chunked == "cols":
            block = (src.shape[0], src.shape[1] // N_DEV)
        else:
            block = src.shape[1:] if chunked else src.shape
        self.width = block[-1]
        self.land_shape = (block[0], N_DEV * block[1]) if land_cols else (N_DEV,) + block

    def _cols(self, first, count=1):
        return pl.ds(pl.multiple_of(first * self.width, LANES), count * self.width)

    def part(self, src_ref, j):
        if self.chunked == "cols":
            return src_ref.at[:, self._cols(j)]
        return src_ref.at[j] if self.chunked else src_ref

    def slot(self, land_ref, s):
        return land_ref.at[:, self._cols(s)] if self.land_cols else land_ref.at[s]

    def seven(self, land_ref):
        return land_ref.at[:, self._cols(0, N_DEV - 1)] if self.land_cols else land_ref.at[pl.ds(0, N_DEV - 1)]


def _mesh_place():
    x, y, c = lax.axis_index("x"), lax.axis_index("y"), lax.axis_index("c")
    return x, y, c, 4 * x + 2 * y + c


def _flipped(x, y, c, k):
    px = 1 - x if k & 4 else x
    py = 1 - y if k & 2 else y
    pc = 1 - c if k & 1 else c
    return (px, py, pc), 4 * px + 2 * py + pc


PEER_ORDER = (2, 4, 6, 3, 5, 7, 1)


def _exchange(items, name):
    n = len(items)

    def body(*refs):
        srcs, lands = refs[:n], refs[n:2 * n]
        send, recv, local = refs[2 * n:]
        x, y, c, me = _mesh_place()

        def copy(i, k, chunk, slot, dev):
            return pltpu.make_async_remote_copy(
                src_ref=items[i].part(srcs[i], chunk), dst_ref=items[i].slot(lands[i], slot),
                send_sem=send.at[i, k - 1], recv_sem=recv.at[i, k - 1], device_id=dev, device_id_type=MESH)

        own = [pltpu.make_async_copy(items[i].part(srcs[i], me), items[i].slot(lands[i], me), local.at[i])
               for i in range(n)]
        for k in PEER_ORDER:
            dev, idx = _flipped(x, y, c, k)
            for i in range(n):
                copy(i, k, idx, me, dev).start()
        for cp in own:
            cp.start()
        for k in PEER_ORDER:
            dev, idx = _flipped(x, y, c, k)
            for i in range(n):
                copy(i, k, me, idx, dev).wait_recv()
        for k in PEER_ORDER:
            dev, idx = _flipped(x, y, c, k)
            for i in range(n):
                copy(i, k, idx, me, dev).wait_send()
        for cp in own:
            cp.wait()

    hbm = pl.BlockSpec(memory_space=pl.ANY)
    return pl.pallas_call(
        body, name=name,
        in_specs=[hbm] * n, out_specs=[hbm] * n,
        out_shape=[jax.ShapeDtypeStruct(it.land_shape, it.src.dtype) for it in items],
        scratch_shapes=[pltpu.SemaphoreType.DMA((n, N_DEV - 1)), pltpu.SemaphoreType.DMA((n, N_DEV - 1)),
                        pltpu.SemaphoreType.DMA((n,))],
        compiler_params=pltpu.CompilerParams(has_side_effects=True),
    )(*[it.src for it in items])


def _sequencer_exchange(items, name, collective_id):
    n = len(items)

    def body(*refs):
        srcs, lands = refs[:n], refs[n:2 * n]
        send, recv, local = refs[2 * n:]
        x, y, c, me = _mesh_place()
        barrier = pltpu.get_barrier_semaphore()
        for k in PEER_ORDER:
            pl.semaphore_signal(barrier, inc=1, device_id=_flipped(x, y, c, k)[0], device_id_type=MESH)
        pl.semaphore_wait(barrier, N_DEV - 1)

        def copy(i, k, chunk, slot, dev):
            return pltpu.make_async_remote_copy(
                src_ref=items[i].part(srcs[i], chunk), dst_ref=items[i].slot(lands[i], slot),
                send_sem=send.at[i, k - 1], recv_sem=recv.at[i, k - 1], device_id=dev, device_id_type=MESH)

        own = [pltpu.make_async_copy(items[i].part(srcs[i], me), items[i].slot(lands[i], me), local.at[i])
               for i in range(n)]
        for cp in own:
            cp.start()
        for k in PEER_ORDER:
            dev, idx = _flipped(x, y, c, k)
            for i in range(n):
                copy(i, k, idx, me, dev).start()
        for k in PEER_ORDER:
            dev, idx = _flipped(x, y, c, k)
            for i in range(n):
                copy(i, k, me, idx, dev).wait_recv()
        for k in PEER_ORDER:
            dev, idx = _flipped(x, y, c, k)
            for i in range(n):
                copy(i, k, idx, me, dev).wait_send()
        for cp in own:
            cp.wait()

    return pl.kernel(
        body, name=name,
        out_type=[jax.ShapeDtypeStruct(it.land_shape, it.src.dtype) for it in items],
        mesh=plsc.ScalarSubcoreMesh(axis_name="sequencer", num_cores=1),
        scratch_types=[pltpu.SemaphoreType.DMA((n, N_DEV - 1)), pltpu.SemaphoreType.DMA((n, N_DEV - 1)),
                       pltpu.SemaphoreType.DMA((n,))],
        compiler_params=pltpu.CompilerParams(collective_id=collective_id),
    )(*[it.src for it in items])


HBM_SPEC = pl.BlockSpec(memory_space=pltpu.HBM)
SEM_SPEC = pl.BlockSpec(memory_space=pltpu.SEMAPHORE)
DATAFLOW = pltpu.SideEffectType.DATAFLOW_SIDE_EFFECTING


def _exchange_start(items, name, after=()):
    n = len(items)
    na = len(after)

    def body(*refs):
        srcs, land_ins = refs[:n], refs[n:2 * n]
        outs = refs[2 * n + na:6 * n + na]
        (local,) = refs[6 * n + na:]
        del land_ins
        x, y, c, me = _mesh_place()
        own = [pltpu.make_async_copy(items[i].part(srcs[i], me), items[i].slot(outs[4 * i + 3], me), local.at[i])
               for i in range(n)]
        for cp in own:
            cp.start()
        for cp in own:
            cp.wait()
        for k in PEER_ORDER:
            dev, idx = _flipped(x, y, c, k)
            for i in range(n):
                send, recv, _, land = outs[4 * i:4 * i + 4]
                pltpu.make_async_remote_copy(
                    src_ref=items[i].part(srcs[i], idx), dst_ref=items[i].slot(land, me), send_sem=send, recv_sem=recv,
                    device_id=dev, device_id_type=MESH).start()

    out_shape, out_specs, args, lands = [], [], [], []
    for it in items:
        out_shape += [pltpu.SemaphoreType.DMA(()), pltpu.SemaphoreType.DMA(()),
                      pltpu.HBM(it.src.shape, it.src.dtype), pltpu.HBM(it.land_shape, it.src.dtype)]
        out_specs += [SEM_SPEC, SEM_SPEC, HBM_SPEC, HBM_SPEC]
        args.append(pltpu.with_memory_space_constraint(it.src, pltpu.HBM))
        lands.append(pltpu.with_memory_space_constraint(lax.empty(it.land_shape, it.src.dtype), pltpu.HBM))
    outs = pl.pallas_call(
        body, name=name,
        in_specs=[HBM_SPEC] * (2 * n) + [pl.BlockSpec(memory_space=pl.ANY)] * na,
        out_specs=out_specs, out_shape=out_shape,
        scratch_shapes=[pltpu.SemaphoreType.DMA((n,))],
        input_output_aliases={**{i: 4 * i + 2 for i in range(n)}, **{n + i: 4 * i + 3 for i in range(n)}},
        compiler_params=pltpu.CompilerParams(has_side_effects=DATAFLOW),
    )(*args, *lands, *after)
    return [tuple(outs[4 * i:4 * i + 4]) + (items[i],) for i in range(n)]


def _started(handles):
    return handles[0][2]


def _exchange_wait(handles, after, name):
    n = len(handles)

    def body(*refs):
        x, y, c, _ = _mesh_place()
        for i in range(n):
            src, land, send, recv = refs[4 * i:4 * i + 4]
            del src
            seven = handles[i][4].seven(land)
            cp = pltpu.make_async_remote_copy(src_ref=seven, dst_ref=seven, send_sem=send, recv_sem=recv,
                                              device_id=(x, y, 1 - c), device_id_type=MESH)
            cp.wait_send()
            cp.wait_recv()

    args, in_specs, out_shape = [], [], []
    for send, recv, src, land, _ in handles:
        args += [src, land, send, recv]
        in_specs += [HBM_SPEC, HBM_SPEC, SEM_SPEC, SEM_SPEC]
        out_shape += [pltpu.HBM(src.shape, src.dtype), pltpu.HBM(land.shape, land.dtype)]
    outs = pl.pallas_call(
        body, name=name,
        in_specs=in_specs + [pl.BlockSpec(memory_space=pl.ANY)] * len(after), out_specs=[HBM_SPEC] * (2 * n),
        out_shape=out_shape,
        input_output_aliases={**{4 * i: 2 * i for i in range(n)}, **{4 * i + 1: 2 * i + 1 for i in range(n)}},
        compiler_params=pltpu.CompilerParams(has_side_effects=DATAFLOW),
    )(*args, *after)
    return [outs[2 * i + 1] for i in range(n)]


TM = 1024
TM_ACC = 512
TN_IN = 768


def kernel(x, norm1_g, w_in, mix_conv_w, attn_out_g, conv_out_g, w_out, norm2_g, ffn_up, ffn_conv_w, ffn_down, final_norm_g, loss_target, m_norm1_g, m_w_in, m_mix_conv_w, m_attn_out_g, m_conv_out_g, m_w_out, m_norm2_g, m_ffn_up, m_ffn_conv_w, m_ffn_down, m_final_norm_g, v_norm1_g, v_w_in, v_mix_conv_w, v_attn_out_g, v_conv_out_g, v_w_out, v_norm2_g, v_ffn_up, v_ffn_conv_w, v_ffn_down, v_final_norm_g):
    nbatch, seq, d = x.shape
    t = nbatch * seq
    nt, nta = t // TM, t // TM_ACC
    out_rows = D_MODEL // N_DEV
    down_rows = D_FF // N_DEV
    xf = x.reshape(t, d)
    target = loss_target.reshape(t, d)

    cw_local = jnp.concatenate([ffn_conv_w, mix_conv_w], axis=-1)
    cast = lambda w: _Item(w.astype(BF16), False)
    cast_in = lambda w: _Item(w.astype(BF16), False, land_cols=True)
    cw_all, win0 = _sequencer_exchange([_Item(cw_local, False), cast_in(w_in[0])], "gather_a", 0)
    up_t, m_up_t, v_up_t = (jnp.swapaxes(a, 1, 2) for a in (ffn_up, m_ffn_up, v_ffn_up))
    wout0, wup0 = _sequencer_exchange([cast(w_out[0]), cast(up_t[0])], "gather_b", 1)
    wdown0, win1, wout1 = _sequencer_exchange([cast(ffn_down[0]), cast_in(w_in[1]), cast(w_out[1])], "gather_c", 2)
    wup1, wdown1 = _sequencer_exchange([cast(up_t[1]), cast(ffn_down[1])], "gather_d", 3)
    win, wup = [win0, win1], [wup0, wup1]
    wout = [w.reshape(D_MODEL, D_MODEL) for w in (wout0, wout1)]
    wdown = [w.reshape(N_UP_PAIRS, UP_CHUNK, D_MODEL) for w in (wdown0, wdown1)]
    fcw = [cw_all[:, k, :, :UP_CHUNK].reshape(2, N_UP_PAIRS, 3, UP_CHUNK) for k in range(DEPTH)]
    mcw = [cw_all[:, k, :, UP_CHUNK:].transpose(1, 0, 2).reshape(3, D_CONV) for k in range(DEPTH)]

    full = lambda i, j, k: (0, 0)

    saved = []
    xin = xf
    for l in range(DEPTH):
        h1 = _rms_fwd(xin, norm1_g[l][None], f"rms1_fwd_{l}")
        proj = _matmul(
            h1, win[l], grid=(nt, D_IN // TN_IN, 1), dims=NN, name=f"proj_{l}",
            a_spec=pl.BlockSpec((TM, D_MODEL), lambda i, j, k: (i, 0)),
            b_spec=pl.BlockSpec((D_MODEL, TN_IN), lambda i, j, k: (0, j)),
            o_spec=pl.BlockSpec((TM, TN_IN), lambda i, j, k: (i, j)), o_shape=(t, D_IN), o_dtype=F32)
        o, lse, cat = _attn_fwd(proj, attn_out_g[l][None], nbatch, seq)
        cat = _convmix_fwd(proj, cat, mcw[l], conv_out_g[l][None], nbatch, seq)
        xmid = _matmul(
            cat, wout[l], grid=(nta, 1, 1), dims=NN, name=f"mix_out_{l}",
            a_spec=pl.BlockSpec((TM_ACC, D_MODEL), lambda i, j, k: (i, 0)),
            b_spec=pl.BlockSpec((D_MODEL, D_MODEL), full),
            o_spec=pl.BlockSpec((TM_ACC, D_MODEL), lambda i, j, k: (i, 0)), o_shape=(t, D_MODEL), o_dtype=F32,
            res=xin, res_spec=pl.BlockSpec((TM_ACC, D_MODEL), lambda i, j, k: (i, 0)))
        h2 = _rms_fwd(xmid, norm2_g[l][None], f"rms2_fwd_{l}")
        pre = _matmul(
            h2, wup[l], grid=(nt, N_DEV, 1), dims=NT, name=f"ffn_up_{l}",
            a_spec=pl.BlockSpec((TM, D_MODEL), lambda i, j, k: (i, 0)),
            b_spec=pl.BlockSpec((None, UP_CHUNK, D_MODEL), lambda i, j, k: (j, 0, 0)),
            o_spec=pl.BlockSpec((None, TM, UP_CHUNK), lambda i, j, k: (j, i, 0)),
            o_shape=(N_DEV, t, UP_CHUNK), o_dtype=BF16).reshape(2, N_UP_PAIRS, t, UP_CHUNK)
        act = _ffn_act_fwd(pre, fcw[l], nbatch, seq)
        xout = _matmul(
            act, wdown[l], grid=(nta, 1, 1), dims=NN, name=f"ffn_down_{l}",
            a_spec=pl.BlockSpec((N_UP_PAIRS, TM_ACC, UP_CHUNK), lambda i, j, k: (0, i, 0)),
            b_spec=pl.BlockSpec((N_UP_PAIRS, UP_CHUNK, D_MODEL), lambda i, j, k: (0, 0, 0)),
            o_spec=pl.BlockSpec((TM_ACC, D_MODEL), lambda i, j, k: (i, 0)), o_shape=(t, D_MODEL), o_dtype=F32,
            res=xmid, res_spec=pl.BlockSpec((TM_ACC, D_MODEL), lambda i, j, k: (i, 0)))
        saved.append((xin, h1, proj, o, lse, cat, xmid, h2, pre, act))
        xin = xout

    loss_part, dx, dxb, dgf = _loss_head(xin, final_norm_g[None], target, "loss_head")

    dg1, dg2, dga, dgc = [None] * DEPTH, [None] * DEPTH, [None] * DEPTH, [None] * DEPTH
    for l in reversed(range(DEPTH)):
        xin, h1, proj, o, lse, cat, xmid, h2, pre, act = saved[l]
        d_act = _matmul(
            dxb, wdown[l], grid=(nt, N_UP_PAIRS, 1), dims=NT, name=f"d_act_{l}",
            a_spec=pl.BlockSpec((TM, D_MODEL), lambda i, j, k: (i, 0)),
            b_spec=pl.BlockSpec((None, UP_CHUNK, D_MODEL), lambda i, j, k: (j, 0, 0)),
            o_spec=pl.BlockSpec((None, TM, UP_CHUNK), lambda i, j, k: (j, i, 0)),
            o_shape=(N_UP_PAIRS, t, UP_CHUNK), o_dtype=BF16)
        g_down = _matmul(
            act, dxb, grid=(N_UP_PAIRS, 1, 1), dims=TN, name=f"g_down_{l}",
            a_spec=pl.BlockSpec((None, t, UP_CHUNK), lambda i, j, k: (i, 0, 0)),
            b_spec=pl.BlockSpec((t, D_MODEL), full),
            o_spec=pl.BlockSpec((None, UP_CHUNK, D_MODEL), lambda i, j, k: (i, 0, 0)),
            o_shape=(N_UP_PAIRS, UP_CHUNK, D_MODEL), o_dtype=BF16).reshape(N_DEV, down_rows, D_MODEL)
        d_pre, d_fcw = _ffn_act_bwd(pre, d_act, fcw[l], nbatch, seq)
        d_pre = d_pre.reshape(N_DEV, t, UP_CHUNK)
        dh2 = _matmul(
            d_pre, wup[l], grid=(nta, 1, 1), dims=NN, name=f"d_h2_{l}",
            a_spec=pl.BlockSpec((N_DEV, TM_ACC, UP_CHUNK), lambda i, j, k: (0, i, 0)),
            b_spec=pl.BlockSpec((N_DEV, UP_CHUNK, D_MODEL), lambda i, j, k: (0, 0, 0)),
            o_spec=pl.BlockSpec((TM_ACC, D_MODEL), lambda i, j, k: (i, 0)), o_shape=(t, D_MODEL), o_dtype=F32)
        g_up = _matmul(
            d_pre, h2, grid=(N_DEV, 1, 1), dims=TN, name=f"g_up_{l}",
            a_spec=pl.BlockSpec((None, t, UP_CHUNK), lambda i, j, k: (i, 0, 0)),
            b_spec=pl.BlockSpec((t, D_MODEL), full),
            o_spec=pl.BlockSpec((None, UP_CHUNK, D_MODEL), lambda i, j, k: (i, 0, 0)),
            o_shape=(N_DEV, UP_CHUNK, D_MODEL), o_dtype=BF16)
        dxm, dxmb, dg2[l] = _rms_bwd(xmid, norm2_g[l][None], dh2, dx, f"rms2_bwd_{l}")
        g_out = _matmul(
            cat, dxmb, grid=(1, 1, nt), dims=TN, name=f"g_out_{l}",
            a_spec=pl.BlockSpec((TM, D_MODEL), lambda i, j, k: (k, 0)),
            b_spec=pl.BlockSpec((TM, D_MODEL), lambda i, j, k: (k, 0)),
            o_spec=pl.BlockSpec((D_MODEL, D_MODEL), full),
            o_shape=(D_MODEL, D_MODEL), o_dtype=BF16).reshape(N_DEV, out_rows, D_MODEL)
        if l == 0:
            land_out0, land_up0, land_down0 = _sequencer_exchange(
                [_Item(g_out, True), _Item(g_up, True), _Item(g_down, True)], "scatter_0a", 5)
        d_cat = _matmul(
            dxmb, wout[l], grid=(nta, 1, 1), dims=NT, name=f"d_cat_{l}",
            a_spec=pl.BlockSpec((TM_ACC, D_MODEL), lambda i, j, k: (i, 0)),
            b_spec=pl.BlockSpec((D_MODEL, D_MODEL), full),
            o_spec=pl.BlockSpec((TM_ACC, D_MODEL), lambda i, j, k: (i, 0)), o_shape=(t, D_MODEL), o_dtype=BF16)
        d_proj, dga[l] = _attn_bwd(proj, o, lse, d_cat, attn_out_g[l][None], nbatch, seq)
        d_proj, d_mcw, dgc[l] = _convmix_bwd(proj, d_cat, d_proj, mcw[l], conv_out_g[l][None], nbatch, seq)
        g_in = _matmul(
            h1, d_proj, grid=(1, D_IN // TN_IN, 1), dims=TN, name=f"g_in_{l}",
            a_spec=pl.BlockSpec((t, D_MODEL), full),
            b_spec=pl.BlockSpec((t, TN_IN), lambda i, j, k: (0, j)),
            o_spec=pl.BlockSpec((D_MODEL, TN_IN), lambda i, j, k: (0, j)),
            o_shape=(D_MODEL, D_IN), o_dtype=BF16)
        g_cw = jnp.concatenate(
            [d_fcw.reshape(N_DEV, 3, UP_CHUNK), d_mcw.reshape(3, N_DEV, D_CONV // N_DEV).transpose(1, 0, 2)], axis=-1)
        if l == 0:
            land_in0, land_cw0 = _sequencer_exchange([_Item(g_in, "cols"), _Item(g_cw, True)], "scatter_0b", 6)
        else:
            land_in1, land_out1, land_up1, land_down1, land_cw1 = _sequencer_exchange(
                [_Item(g_in, "cols"), _Item(g_out, True), _Item(g_up, True), _Item(g_down, True), _Item(g_cw, True)],
                "scatter_1", 4)
        dh1 = _matmul(
            d_proj, win[l], grid=(nta, 1, 1), dims=NT, name=f"d_h1_{l}",
            a_spec=pl.BlockSpec((TM_ACC, D_IN), lambda i, j, k: (i, 0)),
            b_spec=pl.BlockSpec((D_MODEL, D_IN), full),
            o_spec=pl.BlockSpec((TM_ACC, D_MODEL), lambda i, j, k: (i, 0)), o_shape=(t, D_MODEL), o_dtype=F32)
        dx, dxb, dg1[l] = _rms_bwd(xin, norm1_g[l][None], dh1, dxm, f"rms1_bwd_{l}")

    def pack_small(n1, a, c, n2, f):
        return jnp.concatenate(
            [n1, n2, f[None], jnp.concatenate([a, c], axis=-1), jnp.zeros((1, D_MODEL), F32)], axis=0)[None]

    res_out = _adamw([land_out0, land_out1], w_out, m_w_out, v_w_out, out_rows, "adamw_w_out")
    res_up = [jnp.swapaxes(r, 1, 2)
              for r in _adamw([land_up0, land_up1], up_t, m_up_t, v_up_t, UP_CHUNK // 4, "adamw_ffn_up")]
    res_down = _adamw([land_down0, land_down1], ffn_down, m_ffn_down, v_ffn_down, down_rows, "adamw_ffn_down")
    small = jnp.concatenate(
        [dg1[0], dg1[1], dg2[0], dg2[1], dgf,
         jnp.concatenate([dga[0], dgc[0]], axis=-1), jnp.concatenate([dga[1], dgc[1]], axis=-1),
         jnp.pad(loss_part, ((0, 0), (0, D_MODEL - LANES)))], axis=0)
    (land_small,) = _exchange([_Item(small, False)], "gather_gain_grads")
    res_small = _adamw(
        [land_small], pack_small(norm1_g, attn_out_g, conv_out_g, norm2_g, final_norm_g),
        pack_small(m_norm1_g, m_attn_out_g, m_conv_out_g, m_norm2_g, m_final_norm_g),
        pack_small(v_norm1_g, v_attn_out_g, v_conv_out_g, v_norm2_g, v_final_norm_g), SUBLANES, "adamw_gains")
    res_in = _adamw([land_in0, land_in1], w_in, m_w_in, v_w_in, 256, "adamw_w_in")
    res_cw = _adamw(
        [land_cw0, land_cw1], cw_local, jnp.concatenate([m_ffn_conv_w, m_mix_conv_w], axis=-1),
        jnp.concatenate([v_ffn_conv_w, v_mix_conv_w], axis=-1), 3, "adamw_conv_w")

    loss = res_small[0][0, SUBLANES - 1, 0]

    def unpack(kind):
        s = res_small[kind][0]
        cwr = res_cw[kind]
        return (s[0:2], res_in[kind], cwr[..., UP_CHUNK:], s[5:7, :D_ATTN], s[5:7, D_ATTN:], res_out[kind],
                s[2:4], res_up[kind], cwr[..., :UP_CHUNK], res_down[kind], s[4])

    return (loss, dx.reshape(nbatch, seq, d), *unpack(0), *unpack(1), *unpack(2), *unpack(3))
```

```python
import math

import jax
import jax.numpy as jnp
from jax import lax
from jax.experimental import pallas as pl
from jax.experimental.pallas import tpu as pltpu
from jax.experimental.pallas import tpu_sc as plsc

F32 = jnp.float32
BF16 = jnp.bfloat16

D_MODEL = 1024
D_ATTN = 512
D_CONV = 512
HEAD_DIM = 64
N_HEADS = 8
D_FF = 2816
DEPTH = 2
D_IN = 3 * D_ATTN + 3 * D_CONV
EPS = 1e-6
DILATIONS = (1, 4, 16)
BAND = 128
N_DEV = 8
IN_CHUNK = D_IN // N_DEV
UP_CHUNK = 2 * D_FF // N_DEV
N_UP_PAIRS = N_DEV // 2
CW_PACK = UP_CHUNK + D_CONV // N_DEV
ADAM_LR = 0.001
ADAM_B1 = 0.9
ADAM_B2 = 0.999
ADAM_EPS = 1e-08
ADAM_WD = 0.01
ADAM_STEP = 10
LANES = 128
SUBLANES = 8
VMEM_LIMIT = 56 * 1024 * 1024

NEG = -1e30
MESH = pl.DeviceIdType.MESH


def _params(sem=None, vmem=VMEM_LIMIT):
    return pltpu.CompilerParams(dimension_semantics=sem, vmem_limit_bytes=vmem)


NN = (((1,), (0,)), ((), ()))
NT = (((1,), (1,)), ((), ()))
TN = (((0,), (0,)), ((), ()))


def _matmul(a, b, *, grid, a_spec, b_spec, o_spec, o_shape, o_dtype, dims, name, res=None, res_spec=None, after=()):
    nk = grid[2]
    o_block = tuple(s for s in o_spec.block_shape if s is not None)
    na = len(after)

    def body(*refs):
        refs = refs[:2 + (res is not None)] + refs[2 + (res is not None) + na:]
        if res is None:
            a_ref, b_ref, o_ref, *scr = refs
            r_ref = None
        else:
            a_ref, b_ref, r_ref, o_ref, *scr = refs
        def dot(av, bv):
            return lax.dot_general(av.astype(BF16), bv.astype(BF16), dims, preferred_element_type=F32)

        if len(a_ref.shape) == 3:
            part = dot(a_ref[0], b_ref[0])
            for c in range(1, a_ref.shape[0]):
                part = part + dot(a_ref[c], b_ref[c])
        else:
            part = dot(a_ref[...], b_ref[...])

        def finish(total):
            if r_ref is not None:
                total = total + r_ref[...]
            o_ref[...] = total.astype(o_dtype)

        if nk == 1:
            finish(part)
        else:
            acc = scr[0]
            k = pl.program_id(2)

            @pl.when(k == 0)
            def _():
                acc[...] = part

            @pl.when(k > 0)
            def _():
                acc[...] += part

            @pl.when(k == nk - 1)
            def _():
                finish(acc[...])

    in_specs = [a_spec, b_spec] + ([res_spec] if res is not None else []) + [pl.BlockSpec(memory_space=pl.ANY)] * na
    args = (a, b) + ((res,) if res is not None else ()) + tuple(after)
    return pl.pallas_call(
        body, name=name, grid=grid, in_specs=in_specs, out_specs=o_spec,
        out_shape=jax.ShapeDtypeStruct(o_shape, o_dtype),
        scratch_shapes=[pltpu.VMEM(o_block, F32)] if nk > 1 else [],
        compiler_params=_params(("parallel", "parallel", "arbitrary")),
    )(*args)


ROW_TILE = 512


def _rms_fwd(x, g, name):
    t, d = x.shape

    def body(x_ref, g_ref, h_ref):
        xv = x_ref[...]
        r = lax.rsqrt(jnp.mean(xv * xv, axis=-1, keepdims=True) + EPS)
        h_ref[...] = (xv * r * g_ref[...]).astype(BF16)

    return pl.pallas_call(
        body, name=name, grid=(t // ROW_TILE,),
        in_specs=[pl.BlockSpec((ROW_TILE, d), lambda i: (i, 0)), pl.BlockSpec((1, d), lambda i: (0, 0))],
        out_specs=pl.BlockSpec((ROW_TILE, d), lambda i: (i, 0)),
        out_shape=jax.ShapeDtypeStruct((t, d), BF16),
        compiler_params=_params(("parallel",)),
    )(x, g)


def _rms_bwd(x, g, dh, dres, name):
    t, d = x.shape

    def body(x_ref, g_ref, dh_ref, dres_ref, dx_ref, dxb_ref, dg_ref):
        xv = x_ref[...]
        r = lax.rsqrt(jnp.mean(xv * xv, axis=-1, keepdims=True) + EPS)
        xh = xv * r
        dhv = dh_ref[...]
        gd = dhv * g_ref[...]
        dx = r * (gd - xh * jnp.mean(gd * xh, axis=-1, keepdims=True)) + dres_ref[...]
        dx_ref[...] = dx
        dxb_ref[...] = dx.astype(BF16)
        part = jnp.sum(dhv * xh, axis=0, keepdims=True)

        @pl.when(pl.program_id(0) == 0)
        def _():
            dg_ref[...] = part

        @pl.when(pl.program_id(0) > 0)
        def _():
            dg_ref[...] += part

    row = pl.BlockSpec((ROW_TILE, d), lambda i: (i, 0))
    vec = pl.BlockSpec((1, d), lambda i: (0, 0))
    return pl.pallas_call(
        body, name=name, grid=(t // ROW_TILE,),
        in_specs=[row, vec, row, row], out_specs=[row, row, vec],
        out_shape=[jax.ShapeDtypeStruct((t, d), F32), jax.ShapeDtypeStruct((t, d), BF16),
                   jax.ShapeDtypeStruct((1, d), F32)],
        compiler_params=_params(("arbitrary",)),
    )(x, g, dh, dres)


def _loss_head(x, g, target, name):
    t, d = x.shape

    def body(x_ref, g_ref, t_ref, loss_ref, dx_ref, dxb_ref, dg_ref):
        xv = x_ref[...]
        r = lax.rsqrt(jnp.mean(xv * xv, axis=-1, keepdims=True) + EPS)
        xh = xv * r
        gv = g_ref[...]
        err = xh * gv - t_ref[...]
        loss = jnp.full((1, LANES), 0.5 / d, F32) * jnp.sum(err * err)
        dy = err * (1.0 / d)
        gd = dy * gv
        dx = r * (gd - xh * jnp.mean(gd * xh, axis=-1, keepdims=True))
        dx_ref[...] = dx
        dxb_ref[...] = dx.astype(BF16)
        part = jnp.sum(dy * xh, axis=0, keepdims=True)

        @pl.when(pl.program_id(0) == 0)
        def _():
            dg_ref[...] = part
            loss_ref[...] = loss

        @pl.when(pl.program_id(0) > 0)
        def _():
            dg_ref[...] += part
            loss_ref[...] += loss

    row = pl.BlockSpec((ROW_TILE, d), lambda i: (i, 0))
    vec = pl.BlockSpec((1, d), lambda i: (0, 0))
    return pl.pallas_call(
        body, name=name, grid=(t // ROW_TILE,),
        in_specs=[row, vec, row],
        out_specs=[pl.BlockSpec((1, LANES), lambda i: (0, 0)), row, row, vec],
        out_shape=[jax.ShapeDtypeStruct((1, LANES), F32), jax.ShapeDtypeStruct((t, d), F32),
                   jax.ShapeDtypeStruct((t, d), BF16), jax.ShapeDtypeStruct((1, d), F32)],
        compiler_params=_params(("arbitrary",)),
    )(x, g, target)


def _group_matrix(n):
    shift = int(math.log2(HEAD_DIM))
    r = lax.broadcasted_iota(jnp.int32, (n, n), 0) >> shift
    c = lax.broadcasted_iota(jnp.int32, (n, n), 1) >> shift
    return (r == c).astype(BF16)


def _group_sum(v, gmat):
    hi = v.astype(BF16)
    lo = (v - hi.astype(F32)).astype(BF16)

    def dot(p):
        return jnp.dot(p, gmat, preferred_element_type=F32)

    return dot(hi) + dot(lo)


def _shift_rows(ext, k):
    return pltpu.roll(ext, k % ext.shape[0], 0)


def _store_columns(stage, out_hbm, sems, row0, nrows, col_blocks):
    rows = pl.ds(pl.multiple_of(row0, SUBLANES * 2), nrows)
    copies = [
        pltpu.make_async_copy(stage.at[i], out_hbm.at[rows, pl.ds(pl.multiple_of(cb * LANES, LANES), LANES)], sems.at[i])
        for i, cb in enumerate(col_blocks)
    ]
    for cp in copies:
        cp.start()
    for cp in copies:
        cp.wait()


def _attn_consts(width):
    i = lax.broadcasted_iota(jnp.int32, (BAND, width), 0)
    j = lax.broadcasted_iota(jnp.int32, (BAND, width), 1)
    dist = (width - BAND) + i - j
    inwin = (dist >= 0) & (dist <= BAND)
    return dist.astype(F32), inwin, j


def _head_masks():
    lane = lax.broadcasted_iota(jnp.int32, (1, LANES), 1)
    return [(lane < HEAD_DIM).astype(F32), (lane >= HEAD_DIM).astype(F32)]


def _pair_bias(slope, dil):
    distf, inwin, _ = _attn_consts(2 * BAND)
    return jnp.concatenate([jnp.where(inwin, distf * (slope[hh] * (-float(dil))), NEG) for hh in range(2)], axis=0)


def _stack_heads(xv, hmask):
    return jnp.concatenate([xv * hmask[0], xv * hmask[1]], axis=0).astype(BF16)


FWD_UNROLL = 8
BWD_UNROLL = 8


def _unroll(trips, most):
    return max(u for u in range(1, most + 1) if trips % u == 0)


def _for_blocks(seq, dil, block, most):
    nb = seq // dil // BAND

    def residue(r, carry):
        base = r * nb
        block(pl.multiple_of(base * BAND, BAND), None)
        if nb > 1:
            def rest(n, c):
                block(pl.multiple_of((base + n) * BAND, BAND), pl.multiple_of((base + n - 1) * BAND, BAND))
                return c

            lax.fori_loop(1, nb, rest, 0, unroll=_unroll(nb - 1, most))
        return carry

    if dil == 1:
        residue(0, 0)
    else:
        lax.fori_loop(0, dil, residue, 0, unroll=_unroll(dil, max(1, most // nb)))


def _permute_in(src_ref, dst_ref, dil, seq):
    length = seq // dil
    for r in range(dil):
        dst_ref[pl.ds(r * length, length), :] = src_ref[pl.ds(r, length, stride=dil), :].astype(dst_ref.dtype)


def _slopes_table():
    slopes = 2.0 ** (-8.0 * jnp.arange(1, N_HEADS + 1, dtype=F32) / N_HEADS)
    return jnp.broadcast_to(slopes[:, None], (N_HEADS, 2 * BAND))


def _attn_fwd(proj, attn_g, nbatch, seq):
    t = nbatch * seq
    scale = HEAD_DIM ** -0.5

    def body(q_ref, k_ref, v_ref, g_ref, sl_ref, o_ref, lse_ref, cat_ref, pq, pk, pv, po, pm, pll, ao, am, al):
        hp = pl.program_id(1)
        hmask = _head_masks()
        slope = [sl_ref[pl.ds(2 * hp + hh, 1), :] for hh in range(2)]

        def run_branch(dil, qs, ks, vs, osink, msink, lsink):
            bias = _pair_bias(slope, dil)

            def block(row0, prow):
                cur = pl.ds(row0, BAND)
                q2 = _stack_heads(qs[cur, :] * scale, hmask)
                if prow is None:
                    kk, vv, bias_b = ks[cur, :], vs[cur, :], bias[:, BAND:]
                else:
                    prev = pl.ds(prow, BAND)
                    kk = jnp.concatenate([ks[prev, :], ks[cur, :]], axis=0)
                    vv = jnp.concatenate([vs[prev, :], vs[cur, :]], axis=0)
                    bias_b = bias
                s = lax.dot_general(q2, kk.astype(BF16), NT, preferred_element_type=F32) + bias_b
                m = jnp.max(s, axis=1, keepdims=True)
                p = jnp.exp(s - m)
                l = jnp.sum(p, axis=1, keepdims=True)
                pb = p.astype(BF16)
                o = jnp.dot(jnp.concatenate([pb[:BAND], pb[BAND:]], axis=1), _stack_heads(vv, hmask),
                            preferred_element_type=F32)
                osink[cur, :] = o
                msink[cur, :] = m[:BAND] * hmask[0] + m[BAND:] * hmask[1]
                lsink[cur, :] = l[:BAND] * hmask[0] + l[BAND:] * hmask[1]

            _for_blocks(seq, dil, block, FWD_UNROLL)

        run_branch(1, q_ref, k_ref, v_ref, ao, am, al)
        for dil in DILATIONS[1:]:
            length = seq // dil
            _permute_in(q_ref, pq, dil, seq)
            _permute_in(k_ref, pk, dil, seq)
            _permute_in(v_ref, pv, dil, seq)
            run_branch(dil, pq, pk, pv, po, pm, pll)
            for r in range(dil):
                nat = pl.ds(r, length, stride=dil)
                per = pl.ds(r * length, length)
                m0 = am[nat, :]
                mb = pm[per, :]
                mn = jnp.maximum(m0, mb)
                e0 = jnp.exp(m0 - mn)
                eb = jnp.exp(mb - mn)
                ao[nat, :] = ao[nat, :] * e0 + po[per, :] * eb
                al[nat, :] = al[nat, :] * e0 + pll[per, :] * eb
                am[nat, :] = mn

        gmat = _group_matrix(LANES)
        gv = g_ref[...]

        def fin(c, carry):
            rows = pl.ds(pl.multiple_of(c * 256, 256), 256)
            lv = al[rows, :]
            o = ao[rows, :] / lv
            o_ref[rows, :] = o
            lse_ref[rows, :] = am[rows, :] + jnp.log(lv)
            ms = _group_sum(o * o, gmat) * (1.0 / HEAD_DIM)
            cat_ref[rows, :] = (o * lax.rsqrt(ms + EPS) * gv).astype(BF16)
            return carry

        lax.fori_loop(0, seq // 256, fin, 0)

    nq = D_ATTN // LANES
    blk = lambda off: pl.BlockSpec((seq, LANES), lambda b, h: (b, h + off))
    scratch = [pltpu.VMEM((seq, LANES), F32) for _ in range(9)]
    return pl.pallas_call(
        body, name="attn_fwd", grid=(nbatch, nq),
        in_specs=[blk(0), blk(nq), blk(2 * nq), pl.BlockSpec((1, LANES), lambda b, h: (0, h)),
                  pl.BlockSpec((N_HEADS, 2 * BAND), lambda b, h: (0, 0))],
        out_specs=[blk(0), blk(0), blk(0)],
        out_shape=[jax.ShapeDtypeStruct((t, D_ATTN), F32), jax.ShapeDtypeStruct((t, D_ATTN), F32),
                   jax.ShapeDtypeStruct((t, D_MODEL), BF16)],
        scratch_shapes=scratch,
        compiler_params=_params(("parallel", "parallel")),
    )(proj, proj, proj, attn_g, _slopes_table())


def _attn_bwd(proj, o, lse, d_cat, attn_g, nbatch, seq):
    t = nbatch * seq
    scale = HEAD_DIM ** -0.5

    def body(q_ref, k_ref, v_ref, o_ref, lse_ref, dy_ref, g_ref, sl_ref, dproj_ref, dg_ref,
             do_n, dl_n, dq_n, dk_n, dv_n, pq, pk, pv, pdo, plse, pdl, pdq, pdk, pdv, stage, sems):
        hp = pl.program_id(0)
        hmask = _head_masks()
        slope = [sl_ref[pl.ds(2 * hp + hh, 1), :] for hh in range(2)]
        gmat = _group_matrix(LANES)
        gv = g_ref[...]

        def prep(c, dg):
            rows = pl.ds(pl.multiple_of(c * 256, 256), 256)
            ov = o_ref[rows, :]
            dyn = dy_ref[rows, :].astype(F32)
            r = lax.rsqrt(_group_sum(ov * ov, gmat) * (1.0 / HEAD_DIM) + EPS)
            gd = dyn * gv
            oh = ov * r
            do = r * (gd - oh * (_group_sum(gd * oh, gmat) * (1.0 / HEAD_DIM)))
            do_n[rows, :] = do
            dl_n[rows, :] = _group_sum(do * ov, gmat)
            return dg + jnp.sum(dyn * oh, axis=0, keepdims=True)

        dg = lax.fori_loop(0, seq // 256, prep, jnp.zeros((1, LANES), F32))

        @pl.when(pl.program_id(1) == 0)
        def _():
            dg_ref[...] = dg

        @pl.when(pl.program_id(1) > 0)
        def _():
            dg_ref[...] += dg

        def clear(*refs):
            def step(c, carry):
                rows = pl.ds(pl.multiple_of(c * 256, 256), 256)
                for ref in refs:
                    ref[rows, :] = jnp.zeros((256, LANES), F32)
                return carry

            lax.fori_loop(0, seq // 256, step, 0)

        clear(dq_n, dk_n, dv_n)

        def run_branch(dil, qs, ks, vs, dos, lses, dls, dqs, dks, dvs):
            bias = _pair_bias(slope, dil)

            def per_head(xv):
                return jnp.concatenate([xv[:, 0:1], xv[:, HEAD_DIM:HEAD_DIM + 1]], axis=0)

            def block(row0, prow):
                cur = pl.ds(row0, BAND)
                keys = cur if prow is None else pl.ds(prow, 2 * BAND)
                q2 = _stack_heads(qs[cur, :] * scale, hmask)
                do2 = _stack_heads(dos[cur, :], hmask)
                kk, vv = ks[keys, :], vs[keys, :]
                s = lax.dot_general(q2, kk.astype(BF16), NT, preferred_element_type=F32)
                s = s + (bias[:, BAND:] if prow is None else bias)
                p = jnp.exp(s - per_head(lses[cur, :]))
                dp = lax.dot_general(do2, vv.astype(BF16), NT, preferred_element_type=F32)
                ds = (p * (dp - per_head(dls[cur, :]))).astype(BF16)
                dqs[cur, :] += jnp.dot(jnp.concatenate([ds[:BAND], ds[BAND:]], axis=1), _stack_heads(kk, hmask),
                                       preferred_element_type=F32)
                dks[keys, :] += lax.dot_general(ds, q2, TN, preferred_element_type=F32)
                dvs[keys, :] += lax.dot_general(p.astype(BF16), do2, TN, preferred_element_type=F32)

            _for_blocks(seq, dil, block, BWD_UNROLL)

        run_branch(1, q_ref, k_ref, v_ref, do_n, lse_ref, dl_n, dq_n, dk_n, dv_n)
        for dil in DILATIONS[1:]:
            length = seq // dil
            for src, dst in ((q_ref, pq), (k_ref, pk), (v_ref, pv), (do_n, pdo), (lse_ref, plse), (dl_n, pdl)):
                _permute_in(src, dst, dil, seq)
            clear(pdq, pdk, pdv)
            run_branch(dil, pq, pk, pv, pdo, plse, pdl, pdq, pdk, pdv)
            for r in range(dil):
                nat = pl.ds(r, length, stride=dil)
                per = pl.ds(r * length, length)
                dq_n[nat, :] += pdq[per, :]
                dk_n[nat, :] += pdk[per, :]
                dv_n[nat, :] += pdv[per, :]

        def emit(c, carry):
            rows = pl.ds(pl.multiple_of(c * 256, 256), 256)
            stage[0, rows, :] = (dq_n[rows, :] * scale).astype(BF16)
            stage[1, rows, :] = dk_n[rows, :].astype(BF16)
            stage[2, rows, :] = dv_n[rows, :].astype(BF16)
            return carry

        lax.fori_loop(0, seq // 256, emit, 0)
        _store_columns(stage, dproj_ref, sems, pl.program_id(1) * seq, seq, [hp, nq + hp, 2 * nq + hp])

    nq = D_ATTN // LANES
    blk = lambda off: pl.BlockSpec((seq, LANES), lambda h, b: (b, h + off))
    vec = pl.BlockSpec((1, LANES), lambda h, b: (0, h))
    scratch = [pltpu.VMEM((seq, LANES), F32) for _ in range(14)]
    scratch += [pltpu.VMEM((3, seq, LANES), BF16), pltpu.SemaphoreType.DMA((3,))]
    d_proj, dg = pl.pallas_call(
        body, name="attn_bwd", grid=(nq, nbatch),
        in_specs=[blk(0), blk(nq), blk(2 * nq), blk(0), blk(0), blk(0), vec,
                  pl.BlockSpec((N_HEADS, 2 * BAND), lambda h, b: (0, 0))],
        out_specs=[pl.BlockSpec(memory_space=pl.ANY), vec],
        out_shape=[jax.ShapeDtypeStruct((t, D_IN), BF16), jax.ShapeDtypeStruct((1, D_ATTN), F32)],
        scratch_shapes=scratch,
        compiler_params=_params(("arbitrary", "arbitrary")),
    )(proj, proj, proj, o, lse, d_cat, attn_g, _slopes_table())
    return d_proj, dg


HALO = SUBLANES
PACKED_ROWS = 2 * SUBLANES


def _window(ref, c, rows, nchunks, after):
    row0 = pl.multiple_of(c * rows, rows)
    prev0 = pl.multiple_of(jnp.maximum(row0 - PACKED_ROWS, 0), PACKED_ROWS)
    before = ref[pl.ds(prev0, PACKED_ROWS), :].astype(F32)[PACKED_ROWS - HALO:] * (c > 0).astype(F32)
    parts = [before, ref[pl.ds(row0, rows), :].astype(F32)]
    if after:
        next0 = pl.multiple_of(jnp.minimum(row0 + rows, (nchunks - 1) * rows), PACKED_ROWS)
        parts.append(ref[pl.ds(next0, PACKED_ROWS), :].astype(F32)[:HALO] * (c < nchunks - 1).astype(F32))
    return jnp.concatenate(parts, axis=0)


def _behind(z):
    z1 = _shift_rows(z, 1)
    return z1, _shift_rows(z1, 1)


def _ahead(dy):
    d1 = _shift_rows(dy, -1)
    return d1, _shift_rows(d1, -1)


def _conv(z, w):
    z1, z2 = _behind(z)
    return w[0:1] * z2 + w[1:2] * z1 + w[2:3] * z


def _conv_bwd(dy, z, w, cur):
    d1, d2 = _ahead(dy)
    dz = w[2:3] * dy + w[1:2] * d1 + w[0:1] * d2
    return dz, [jnp.sum((d * z)[cur], axis=0, keepdims=True) for d in (d2, d1, dy)]


def _sigmoid(a):
    return 0.5 * jnp.tanh(0.5 * a) + 0.5


MIX_ROWS = 256
GATE_B_BLOCK = 3 * D_ATTN // LANES
GATE_C_BLOCK = GATE_B_BLOCK + D_CONV // LANES
U_BLOCK = GATE_C_BLOCK + D_CONV // LANES


def _convmix_fwd(proj, cat, mcw, conv_g, nbatch, seq):
    nchunks = seq // MIX_ROWS

    def body(gb_ref, gc_ref, u_ref, w_ref, g_ref, cat_in, cat_ref):
        del cat_in
        gmat = _group_matrix(LANES)
        w = w_ref[...]
        gv = g_ref[...]

        def step(c, carry):
            cur = pl.ds(pl.multiple_of(c * MIX_ROWS, MIX_ROWS), MIX_ROWS)
            z = _window(gc_ref, c, MIX_ROWS, nchunks, False) * _window(u_ref, c, MIX_ROWS, nchunks, False)
            y = gb_ref[cur, :] * _conv(z, w)[HALO:]
            ms = _group_sum(y * y, gmat) * (1.0 / HEAD_DIM)
            cat_ref[cur, :] = (y * lax.rsqrt(ms + EPS) * gv).astype(BF16)
            return carry

        lax.fori_loop(0, nchunks, step, 0)

    nc = D_CONV // LANES
    blk = lambda off: pl.BlockSpec((seq, LANES), lambda b, j: (b, j + off))
    return pl.pallas_call(
        body, name="convmix_fwd", grid=(nbatch, nc),
        in_specs=[blk(GATE_B_BLOCK), blk(GATE_C_BLOCK), blk(U_BLOCK),
                  pl.BlockSpec((3, LANES), lambda b, j: (0, j)), pl.BlockSpec((1, LANES), lambda b, j: (0, j)),
                  pl.BlockSpec(memory_space=pl.ANY)],
        out_specs=blk(D_ATTN // LANES),
        out_shape=jax.ShapeDtypeStruct(cat.shape, cat.dtype),
        input_output_aliases={5: 0},
        compiler_params=_params(("parallel", "parallel")),
    )(proj, proj, proj, mcw, conv_g, cat)


def _convmix_bwd(proj, d_cat, d_proj, mcw, conv_g, nbatch, seq):
    nchunks = seq // MIX_ROWS

    def body(gb_ref, gc_ref, u_ref, dy_ref, w_ref, g_ref, dproj_in, dproj_ref, dw_ref, dg_ref, stage, sems):
        del dproj_in
        cb = pl.program_id(0)
        b = pl.program_id(1)
        gmat = _group_matrix(LANES)
        w = w_ref[...]
        gv = g_ref[...]
        cur = slice(HALO, HALO + MIX_ROWS)

        def step(c, carry):
            rows = pl.ds(pl.multiple_of(c * MIX_ROWS, MIX_ROWS), MIX_ROWS)
            gb = _window(gb_ref, c, MIX_ROWS, nchunks, True)
            gc = _window(gc_ref, c, MIX_ROWS, nchunks, True)
            u = _window(u_ref, c, MIX_ROWS, nchunks, True)
            dyn = _window(dy_ref, c, MIX_ROWS, nchunks, True)
            z = gc * u
            conv = _conv(z, w)
            y = gb * conv
            r = lax.rsqrt(_group_sum(y * y, gmat) * (1.0 / HEAD_DIM) + EPS)
            yh = y * r
            gd = dyn * gv
            dy = r * (gd - yh * (_group_sum(gd * yh, gmat) * (1.0 / HEAD_DIM)))
            dz, dws = _conv_bwd(dy * gb, z, w, cur)
            stage[0, rows, :] = (dy * conv)[cur].astype(BF16)
            stage[1, rows, :] = (dz * u)[cur].astype(BF16)
            stage[2, rows, :] = (dz * gc)[cur].astype(BF16)
            dg = jnp.sum((dyn * yh)[cur], axis=0, keepdims=True)
            return tuple(a + d for a, d in zip(carry, dws + [dg]))

        zero = jnp.zeros((1, LANES), F32)
        dw0, dw1, dw2, dg = lax.fori_loop(0, nchunks, step, (zero, zero, zero, zero))

        @pl.when(b == 0)
        def _():
            dw_ref[0:1, :] = dw0
            dw_ref[1:2, :] = dw1
            dw_ref[2:3, :] = dw2
            dg_ref[...] = dg

        @pl.when(b > 0)
        def _():
            dw_ref[0:1, :] += dw0
            dw_ref[1:2, :] += dw1
            dw_ref[2:3, :] += dw2
            dg_ref[...] += dg

        _store_columns(stage, dproj_ref, sems, b * seq, seq, [GATE_B_BLOCK + cb, GATE_C_BLOCK + cb, U_BLOCK + cb])

    nc = D_CONV // LANES
    blk = lambda off: pl.BlockSpec((seq, LANES), lambda j, b: (b, j + off))
    return pl.pallas_call(
        body, name="convmix_bwd", grid=(nc, nbatch),
        in_specs=[blk(GATE_B_BLOCK), blk(GATE_C_BLOCK), blk(U_BLOCK), blk(D_ATTN // LANES),
                  pl.BlockSpec((3, LANES), lambda j, b: (0, j)), pl.BlockSpec((1, LANES), lambda j, b: (0, j)),
                  pl.BlockSpec(memory_space=pl.ANY)],
        out_specs=[pl.BlockSpec(memory_space=pl.ANY), pl.BlockSpec((3, LANES), lambda j, b: (0, j)),
                   pl.BlockSpec((1, LANES), lambda j, b: (0, j))],
        out_shape=[jax.ShapeDtypeStruct(d_proj.shape, d_proj.dtype), jax.ShapeDtypeStruct((3, D_CONV), F32),
                   jax.ShapeDtypeStruct((1, D_CONV), F32)],
        scratch_shapes=[pltpu.VMEM((3, seq, LANES), BF16), pltpu.SemaphoreType.DMA((3,))],
        input_output_aliases={6: 0},
        compiler_params=_params(("arbitrary", "arbitrary")),
    )(proj, proj, proj, d_cat, mcw, conv_g, d_proj)


FFN_ROWS = 256


def _ffn_act_fwd(pre, fcw, nbatch, seq):
    t = nbatch * seq
    nchunks = seq // FFN_ROWS

    def body(pre_ref, w_ref, act_ref):
        wa = w_ref[0]
        wc = w_ref[1]

        def step(c, carry):
            cur = pl.ds(pl.multiple_of(c * FFN_ROWS, FFN_ROWS), FFN_ROWS)
            a = _conv(_window(pre_ref.at[0], c, FFN_ROWS, nchunks, False), wa)[HALO:]
            v = _conv(_window(pre_ref.at[1], c, FFN_ROWS, nchunks, False), wc)[HALO:]
            act_ref[cur, :] = (a * _sigmoid(a) * v).astype(BF16)
            return carry

        lax.fori_loop(0, nchunks, step, 0)

    return pl.pallas_call(
        body, name="ffn_act_fwd", grid=(N_UP_PAIRS, nbatch),
        in_specs=[pl.BlockSpec((2, None, seq, UP_CHUNK), lambda i, b: (0, i, b, 0)),
                  pl.BlockSpec((2, None, 3, UP_CHUNK), lambda i, b: (0, i, 0, 0))],
        out_specs=pl.BlockSpec((None, seq, UP_CHUNK), lambda i, b: (i, b, 0)),
        out_shape=jax.ShapeDtypeStruct((N_UP_PAIRS, t, UP_CHUNK), BF16),
        compiler_params=_params(("parallel", "parallel")),
    )(pre, fcw)


def _ffn_act_bwd(pre, d_act, fcw, nbatch, seq):
    nchunks = seq // FFN_ROWS

    def body(pre_ref, da_ref, w_ref, dpre_ref, dw_ref):
        b = pl.program_id(1)
        wa = w_ref[0]
        wc = w_ref[1]
        cur = slice(HALO, HALO + FFN_ROWS)

        def step(c, carry):
            rows = pl.ds(pl.multiple_of(c * FFN_ROWS, FFN_ROWS), FFN_ROWS)
            pg = _window(pre_ref.at[0], c, FFN_ROWS, nchunks, True)
            pv = _window(pre_ref.at[1], c, FFN_ROWS, nchunks, True)
            dact = _window(da_ref, c, FFN_ROWS, nchunks, True)
            a = _conv(pg, wa)
            v = _conv(pv, wc)
            sg = _sigmoid(a)
            asg = a * sg
            dzg, dwg = _conv_bwd(dact * v * (sg + asg - asg * sg), pg, wa, cur)
            dzv, dwv = _conv_bwd(dact * asg, pv, wc, cur)
            dpre_ref[0, rows, :] = dzg[cur].astype(BF16)
            dpre_ref[1, rows, :] = dzv[cur].astype(BF16)
            return tuple(acc + d for acc, d in zip(carry, dwg + dwv))

        zero = jnp.zeros((1, UP_CHUNK), F32)
        sums = lax.fori_loop(0, nchunks, step, (zero,) * 6)

        @pl.when(b == 0)
        def _():
            for i in range(6):
                dw_ref[i // 3, pl.ds(i % 3, 1), :] = sums[i]

        @pl.when(b > 0)
        def _():
            for i in range(6):
                dw_ref[i // 3, pl.ds(i % 3, 1), :] += sums[i]

    pair = pl.BlockSpec((2, None, seq, UP_CHUNK), lambda i, b: (0, i, b, 0))
    wspec = pl.BlockSpec((2, None, 3, UP_CHUNK), lambda i, b: (0, i, 0, 0))
    return pl.pallas_call(
        body, name="ffn_act_bwd", grid=(N_UP_PAIRS, nbatch),
        in_specs=[pair, pl.BlockSpec((None, seq, UP_CHUNK), lambda i, b: (i, b, 0)), wspec],
        out_specs=[pair, wspec],
        out_shape=[jax.ShapeDtypeStruct(pre.shape, BF16), jax.ShapeDtypeStruct(fcw.shape, F32)],
        compiler_params=_params(("parallel", "arbitrary")),
    )(pre, d_act, fcw)


def _adamw(lands, w, m, v, row_tile, name, after=()):
    nl = len(lands)
    _, nr, ncol = lands[0].shape
    c1 = 1.0 - ADAM_B1 ** ADAM_STEP
    c2 = 1.0 - ADAM_B2 ** ADAM_STEP

    def body(*refs):
        land_refs = refs[:nl]
        w_ref, m_ref, v_ref = refs[nl:nl + 3]
        g_ref, d_ref, mo_ref, vo_ref = refs[nl + 3 + len(after):]
        for l in range(nl):
            @pl.when(pl.program_id(0) == l)
            def _(l=l):
                g = land_refs[l][0].astype(F32)
                for j in range(1, N_DEV):
                    g = g + land_refs[l][j].astype(F32)
                g_ref[...] = g

        g = g_ref[...]
        m2 = ADAM_B1 * m_ref[...] + (1.0 - ADAM_B1) * g
        v2 = ADAM_B2 * v_ref[...] + (1.0 - ADAM_B2) * (g * g)
        mo_ref[...] = m2
        vo_ref[...] = v2
        d_ref[...] = -ADAM_LR * ((m2 / c1) / (jnp.sqrt(v2 / c2) + ADAM_EPS) + ADAM_WD * w_ref[...])

    def land_spec(l):
        return pl.BlockSpec((N_DEV, row_tile, ncol), lambda k, i: (0, jnp.where(k == l, i, 0), 0))

    tile = pl.BlockSpec((None, row_tile, ncol), lambda k, i: (k, i, 0))
    return pl.pallas_call(
        body, name=name, grid=(nl, nr // row_tile),
        in_specs=[land_spec(l) for l in range(nl)] + [tile, tile, tile] + [pl.BlockSpec(memory_space=pl.ANY)] * len(after),
        out_specs=[tile] * 4,
        out_shape=[jax.ShapeDtypeStruct(w.shape, F32)] * 4,
        compiler_params=_params(("arbitrary", "arbitrary")),
    )(*lands, w, m, v, *after)


class _Item:
    def __init__(self, src, chunked, land_cols=False):
        self.src, self.chunked, self.land_cols = src, chunked, land_cols
        if chunked == "cols":
            block = (src.shape[0], src.shape[1] // N_DEV)
        else:
            block = src.shape[1:] if chunked else src.shape
        self.width = block[-1]
        self.land_shape = (block[0], N_DEV * block[1]) if land_cols else (N_DEV,) + block

    def _cols(self, first, count=1):
        return pl.ds(pl.multiple_of(first * self.width, LANES), count * self.width)

    def part(self, src_ref, j):
        if self.chunked == "cols":
            return src_ref.at[:, self._cols(j)]
        return src_ref.at[j] if self.chunked else src_ref

    def slot(self, land_ref, s):
        return land_ref.at[:, self._cols(s)] if self.land_cols else land_ref.at[s]

    def seven(self, land_ref):
        return land_ref.at[:, self._cols(0, N_DEV - 1)] if self.land_cols else land_ref.at[pl.ds(0, N_DEV - 1)]


def _mesh_place():
    x, y, c = lax.axis_index("x"), lax.axis_index("y"), lax.axis_index("c")
    return x, y, c, 4 * x + 2 * y + c


def _flipped(x, y, c, k):
    px = 1 - x if k & 4 else x
    py = 1 - y if k & 2 else y
    pc = 1 - c if k & 1 else c
    return (px, py, pc), 4 * px + 2 * py + pc


PEER_ORDER = (2, 4, 6, 3, 5, 7, 1)


def _exchange(items, name):
    n = len(items)

    def body(*refs):
        srcs, lands = refs[:n], refs[n:2 * n]
        send, recv, local = refs[2 * n:]
        x, y, c, me = _mesh_place()

        def copy(i, k, chunk, slot, dev):
            return pltpu.make_async_remote_copy(
                src_ref=items[i].part(srcs[i], chunk), dst_ref=items[i].slot(lands[i], slot),
                send_sem=send.at[i, k - 1], recv_sem=recv.at[i, k - 1], device_id=dev, device_id_type=MESH)

        own = [pltpu.make_async_copy(items[i].part(srcs[i], me), items[i].slot(lands[i], me), local.at[i])
               for i in range(n)]
        for k in PEER_ORDER:
            dev, idx = _flipped(x, y, c, k)
            for i in range(n):
                copy(i, k, idx, me, dev).start()
        for cp in own:
            cp.start()
        for k in PEER_ORDER:
            dev, idx = _flipped(x, y, c, k)
            for i in range(n):
                copy(i, k, me, idx, dev).wait_recv()
        for k in PEER_ORDER:
            dev, idx = _flipped(x, y, c, k)
            for i in range(n):
                copy(i, k, idx, me, dev).wait_send()
        for cp in own:
            cp.wait()

    hbm = pl.BlockSpec(memory_space=pl.ANY)
    return pl.pallas_call(
        body, name=name,
        in_specs=[hbm] * n, out_specs=[hbm] * n,
        out_shape=[jax.ShapeDtypeStruct(it.land_shape, it.src.dtype) for it in items],
        scratch_shapes=[pltpu.SemaphoreType.DMA((n, N_DEV - 1)), pltpu.SemaphoreType.DMA((n, N_DEV - 1)),
                        pltpu.SemaphoreType.DMA((n,))],
        compiler_params=pltpu.CompilerParams(has_side_effects=True),
    )(*[it.src for it in items])


def _sequencer_exchange(items, name, collective_id):
    n = len(items)

    def body(*refs):
        srcs, lands = refs[:n], refs[n:2 * n]
        send, recv, local = refs[2 * n:]
        x, y, c, me = _mesh_place()
        barrier = pltpu.get_barrier_semaphore()
        for k in PEER_ORDER:
            pl.semaphore_signal(barrier, inc=1, device_id=_flipped(x, y, c, k)[0], device_id_type=MESH)
        pl.semaphore_wait(barrier, N_DEV - 1)

        def copy(i, k, chunk, slot, dev):
            return pltpu.make_async_remote_copy(
                src_ref=items[i].part(srcs[i], chunk), dst_ref=items[i].slot(lands[i], slot),
                send_sem=send.at[i, k - 1], recv_sem=recv.at[i, k - 1], device_id=dev, device_id_type=MESH)

        own = [pltpu.make_async_copy(items[i].part(srcs[i], me), items[i].slot(lands[i], me), local.at[i])
               for i in range(n)]
        for cp in own:
            cp.start()
        for k in PEER_ORDER:
            dev, idx = _flipped(x, y, c, k)
            for i in range(n):
                copy(i, k, idx, me, dev).start()
        for k in PEER_ORDER:
            dev, idx = _flipped(x, y, c, k)
            for i in range(n):
                copy(i, k, me, idx, dev).wait_recv()
        for k in PEER_ORDER:
            dev, idx = _flipped(x, y, c, k)
            for i in range(n):
                copy(i, k, idx, me, dev).wait_send()
        for cp in own:
            cp.wait()

    return pl.kernel(
        body, name=name,
        out_type=[jax.ShapeDtypeStruct(it.land_shape, it.src.dtype) for it in items],
        mesh=plsc.ScalarSubcoreMesh(axis_name="sequencer", num_cores=1),
        scratch_types=[pltpu.SemaphoreType.DMA((n, N_DEV - 1)), pltpu.SemaphoreType.DMA((n, N_DEV - 1)),
                       pltpu.SemaphoreType.DMA((n,))],
        compiler_params=pltpu.CompilerParams(collective_id=collective_id),
    )(*[it.src for it in items])


HBM_SPEC = pl.BlockSpec(memory_space=pltpu.HBM)
SEM_SPEC = pl.BlockSpec(memory_space=pltpu.SEMAPHORE)
DATAFLOW = pltpu.SideEffectType.DATAFLOW_SIDE_EFFECTING


def _exchange_start(items, name, after=()):
    n = len(items)
    na = len(after)

    def body(*refs):
        srcs, land_ins = refs[:n], refs[n:2 * n]
        outs = refs[2 * n + na:6 * n + na]
        (local,) = refs[6 * n + na:]
        del land_ins
        x, y, c, me = _mesh_place()
        own = [pltpu.make_async_copy(items[i].part(srcs[i], me), items[i].slot(outs[4 * i + 3], me), local.at[i])
               for i in range(n)]
        for cp in own:
            cp.start()
        for cp in own:
            cp.wait()
        for k in PEER_ORDER:
            dev, idx = _flipped(x, y, c, k)
            for i in range(n):
                send, recv, _, land = outs[4 * i:4 * i + 4]
                pltpu.make_async_remote_copy(
                    src_ref=items[i].part(srcs[i], idx), dst_ref=items[i].slot(land, me), send_sem=send, recv_sem=recv,
                    device_id=dev, device_id_type=MESH).start()

    out_shape, out_specs, args, lands = [], [], [], []
    for it in items:
        out_shape += [pltpu.SemaphoreType.DMA(()), pltpu.SemaphoreType.DMA(()),
                      pltpu.HBM(it.src.shape, it.src.dtype), pltpu.HBM(it.land_shape, it.src.dtype)]
        out_specs += [SEM_SPEC, SEM_SPEC, HBM_SPEC, HBM_SPEC]
        args.append(pltpu.with_memory_space_constraint(it.src, pltpu.HBM))
        lands.append(pltpu.with_memory_space_constraint(lax.empty(it.land_shape, it.src.dtype), pltpu.HBM))
    outs = pl.pallas_call(
        body, name=name,
        in_specs=[HBM_SPEC] * (2 * n) + [pl.BlockSpec(memory_space=pl.ANY)] * na,
        out_specs=out_specs, out_shape=out_shape,
        scratch_shapes=[pltpu.SemaphoreType.DMA((n,))],
        input_output_aliases={**{i: 4 * i + 2 for i in range(n)}, **{n + i: 4 * i + 3 for i in range(n)}},
        compiler_params=pltpu.CompilerParams(has_side_effects=DATAFLOW),
    )(*args, *lands, *after)
    return [tuple(outs[4 * i:4 * i + 4]) + (items[i],) for i in range(n)]


def _started(handles):
    return handles[0][2]


def _exchange_wait(handles, after, name):
    n = len(handles)

    def body(*refs):
        x, y, c, _ = _mesh_place()
        for i in range(n):
            src, land, send, recv = refs[4 * i:4 * i + 4]
            del src
            seven = handles[i][4].seven(land)
            cp = pltpu.make_async_remote_copy(src_ref=seven, dst_ref=seven, send_sem=send, recv_sem=recv,
                                              device_id=(x, y, 1 - c), device_id_type=MESH)
            cp.wait_send()
            cp.wait_recv()

    args, in_specs, out_shape = [], [], []
    for send, recv, src, land, _ in handles:
        args += [src, land, send, recv]
        in_specs += [HBM_SPEC, HBM_SPEC, SEM_SPEC, SEM_SPEC]
        out_shape += [pltpu.HBM(src.shape, src.dtype), pltpu.HBM(land.shape, land.dtype)]
    outs = pl.pallas_call(
        body, name=name,
        in_specs=in_specs + [pl.BlockSpec(memory_space=pl.ANY)] * len(after), out_specs=[HBM_SPEC] * (2 * n),
        out_shape=out_shape,
        input_output_aliases={**{4 * i: 2 * i for i in range(n)}, **{4 * i + 1: 2 * i + 1 for i in range(n)}},
        compiler_params=pltpu.CompilerParams(has_side_effects=DATAFLOW),
    )(*args, *after)
    return [outs[2 * i + 1] for i in range(n)]


TM = 1024
TM_ACC = 512
TN_IN = 768


def kernel(x, norm1_g, w_in, mix_conv_w, attn_out_g, conv_out_g, w_out, norm2_g, ffn_up, ffn_conv_w, ffn_down, final_norm_g, loss_target, m_norm1_g, m_w_in, m_mix_conv_w, m_attn_out_g, m_conv_out_g, m_w_out, m_norm2_g, m_ffn_up, m_ffn_conv_w, m_ffn_down, m_final_norm_g, v_norm1_g, v_w_in, v_mix_conv_w, v_attn_out_g, v_conv_out_g, v_w_out, v_norm2_g, v_ffn_up, v_ffn_conv_w, v_ffn_down, v_final_norm_g):
    nbatch, seq, d = x.shape
    t = nbatch * seq
    nt, nta = t // TM, t // TM_ACC
    out_rows = D_MODEL // N_DEV
    down_rows = D_FF // N_DEV
    xf = x.reshape(t, d)
    target = loss_target.reshape(t, d)

    cw_local = jnp.concatenate([ffn_conv_w, mix_conv_w], axis=-1)
    cast = lambda w: _Item(w.astype(BF16), False)
    cast_in = lambda w: _Item(w.astype(BF16), False, land_cols=True)
    cw_all, win0 = _sequencer_exchange([_Item(cw_local, False), cast_in(w_in[0])], "gather_a", 0)
    up_t, m_up_t, v_up_t = (jnp.swapaxes(a, 1, 2) for a in (ffn_up, m_ffn_up, v_ffn_up))
    wout0, wup0 = _sequencer_exchange([cast(w_out[0]), cast(up_t[0])], "gather_b", 1)
    wdown0, win1, wout1 = _sequencer_exchange([cast(ffn_down[0]), cast_in(w_in[1]), cast(w_out[1])], "gather_c", 2)
    wup1, wdown1 = _sequencer_exchange([cast(up_t[1]), cast(ffn_down[1])], "gather_d", 3)
    win, wup = [win0, win1], [wup0, wup1]
    wout = [w.reshape(D_MODEL, D_MODEL) for w in (wout0, wout1)]
    wdown = [w.reshape(N_UP_PAIRS, UP_CHUNK, D_MODEL) for w in (wdown0, wdown1)]
    fcw = [cw_all[:, k, :, :UP_CHUNK].reshape(2, N_UP_PAIRS, 3, UP_CHUNK) for k in range(DEPTH)]
    mcw = [cw_all[:, k, :, UP_CHUNK:].transpose(1, 0, 2).reshape(3, D_CONV) for k in range(DEPTH)]

    full = lambda i, j, k: (0, 0)

    saved = []
    xin = xf
    for l in range(DEPTH):
        h1 = _rms_fwd(xin, norm1_g[l][None], f"rms1_fwd_{l}")
        proj = _matmul(
            h1, win[l], grid=(nt, D_IN // TN_IN, 1), dims=NN, name=f"proj_{l}",
            a_spec=pl.BlockSpec((TM, D_MODEL), lambda i, j, k: (i, 0)),
            b_spec=pl.BlockSpec((D_MODEL, TN_IN), lambda i, j, k: (0, j)),
            o_spec=pl.BlockSpec((TM, TN_IN), lambda i, j, k: (i, j)), o_shape=(t, D_IN), o_dtype=F32)
        o, lse, cat = _attn_fwd(proj, attn_out_g[l][None], nbatch, seq)
        cat = _convmix_fwd(proj, cat, mcw[l], conv_out_g[l][None], nbatch, seq)
        xmid = _matmul(
            cat, wout[l], grid=(nta, 1, 1), dims=NN, name=f"mix_out_{l}",
            a_spec=pl.BlockSpec((TM_ACC, D_MODEL), lambda i, j, k: (i, 0)),
            b_spec=pl.BlockSpec((D_MODEL, D_MODEL), full),
            o_spec=pl.BlockSpec((TM_ACC, D_MODEL), lambda i, j, k: (i, 0)), o_shape=(t, D_MODEL), o_dtype=F32,
            res=xin, res_spec=pl.BlockSpec((TM_ACC, D_MODEL), lambda i, j, k: (i, 0)))
        h2 = _rms_fwd(xmid, norm2_g[l][None], f"rms2_fwd_{l}")
        pre = _matmul(
            h2, wup[l], grid=(nt, N_DEV, 1), dims=NT, name=f"ffn_up_{l}",
            a_spec=pl.BlockSpec((TM, D_MODEL), lambda i, j, k: (i, 0)),
            b_spec=pl.BlockSpec((None, UP_CHUNK, D_MODEL), lambda i, j, k: (j, 0, 0)),
            o_spec=pl.BlockSpec((None, TM, UP_CHUNK), lambda i, j, k: (j, i, 0)),
            o_shape=(N_DEV, t, UP_CHUNK), o_dtype=BF16).reshape(2, N_UP_PAIRS, t, UP_CHUNK)
        act = _ffn_act_fwd(pre, fcw[l], nbatch, seq)
        xout = _matmul(
            act, wdown[l], grid=(nta, 1, 1), dims=NN, name=f"ffn_down_{l}",
            a_spec=pl.BlockSpec((N_UP_PAIRS, TM_ACC, UP_CHUNK), lambda i, j, k: (0, i, 0)),
            b_spec=pl.BlockSpec((N_UP_PAIRS, UP_CHUNK, D_MODEL), lambda i, j, k: (0, 0, 0)),
            o_spec=pl.BlockSpec((TM_ACC, D_MODEL), lambda i, j, k: (i, 0)), o_shape=(t, D_MODEL), o_dtype=F32,
            res=xmid, res_spec=pl.BlockSpec((TM_ACC, D_MODEL), lambda i, j, k: (i, 0)))
        saved.append((xin, h1, proj, o, lse, cat, xmid, h2, pre, act))
        xin = xout

    loss_part, dx, dxb, dgf = _loss_head(xin, final_norm_g[None], target, "loss_head")

    dg1, dg2, dga, dgc = [None] * DEPTH, [None] * DEPTH, [None] * DEPTH, [None] * DEPTH
    for l in reversed(range(DEPTH)):
        xin, h1, proj, o, lse, cat, xmid, h2, pre, act = saved[l]
        d_act = _matmul(
            dxb, wdown[l], grid=(nt, N_UP_PAIRS, 1), dims=NT, name=f"d_act_{l}",
            a_spec=pl.BlockSpec((TM, D_MODEL), lambda i, j, k: (i, 0)),
            b_spec=pl.BlockSpec((None, UP_CHUNK, D_MODEL), lambda i, j, k: (j, 0, 0)),
            o_spec=pl.BlockSpec((None, TM, UP_CHUNK), lambda i, j, k: (j, i, 0)),
            o_shape=(N_UP_PAIRS, t, UP_CHUNK), o_dtype=BF16)
        g_down = _matmul(
            act, dxb, grid=(N_UP_PAIRS, 1, 1), dims=TN, name=f"g_down_{l}",
            a_spec=pl.BlockSpec((None, t, UP_CHUNK), lambda i, j, k: (i, 0, 0)),
            b_spec=pl.BlockSpec((t, D_MODEL), full),
            o_spec=pl.BlockSpec((None, UP_CHUNK, D_MODEL), lambda i, j, k: (i, 0, 0)),
            o_shape=(N_UP_PAIRS, UP_CHUNK, D_MODEL), o_dtype=BF16).reshape(N_DEV, down_rows, D_MODEL)
        d_pre, d_fcw = _ffn_act_bwd(pre, d_act, fcw[l], nbatch, seq)
        d_pre = d_pre.reshape(N_DEV, t, UP_CHUNK)
        dh2 = _matmul(
            d_pre, wup[l], grid=(nta, 1, 1), dims=NN, name=f"d_h2_{l}",
            a_spec=pl.BlockSpec((N_DEV, TM_ACC, UP_CHUNK), lambda i, j, k: (0, i, 0)),
            b_spec=pl.BlockSpec((N_DEV, UP_CHUNK, D_MODEL), lambda i, j, k: (0, 0, 0)),
            o_spec=pl.BlockSpec((TM_ACC, D_MODEL), lambda i, j, k: (i, 0)), o_shape=(t, D_MODEL), o_dtype=F32)
        g_up = _matmul(
            d_pre, h2, grid=(N_DEV, 1, 1), dims=TN, name=f"g_up_{l}",
            a_spec=pl.BlockSpec((None, t, UP_CHUNK), lambda i, j, k: (i, 0, 0)),
            b_spec=pl.BlockSpec((t, D_MODEL), full),
            o_spec=pl.BlockSpec((None, UP_CHUNK, D_MODEL), lambda i, j, k: (i, 0, 0)),
            o_shape=(N_DEV, UP_CHUNK, D_MODEL), o_dtype=BF16)
        dxm, dxmb, dg2[l] = _rms_bwd(xmid, norm2_g[l][None], dh2, dx, f"rms2_bwd_{l}")
        g_out = _matmul(
            cat, dxmb, grid=(1, 1, nt), dims=TN, name=f"g_out_{l}",
            a_spec=pl.BlockSpec((TM, D_MODEL), lambda i, j, k: (k, 0)),
            b_spec=pl.BlockSpec((TM, D_MODEL), lambda i, j, k: (k, 0)),
            o_spec=pl.BlockSpec((D_MODEL, D_MODEL), full),
            o_shape=(D_MODEL, D_MODEL), o_dtype=BF16).reshape(N_DEV, out_rows, D_MODEL)
        if l == 0:
            land_out0, land_up0, land_down0 = _sequencer_exchange(
                [_Item(g_out, True), _Item(g_up, True), _Item(g_down, True)], "scatter_0a", 5)
        d_cat = _matmul(
            dxmb, wout[l], grid=(nta, 1, 1), dims=NT, name=f"d_cat_{l}",
            a_spec=pl.BlockSpec((TM_ACC, D_MODEL), lambda i, j, k: (i, 0)),
            b_spec=pl.BlockSpec((D_MODEL, D_MODEL), full),
            o_spec=pl.BlockSpec((TM_ACC, D_MODEL), lambda i, j, k: (i, 0)), o_shape=(t, D_MODEL), o_dtype=BF16)
        d_proj, dga[l] = _attn_bwd(proj, o, lse, d_cat, attn_out_g[l][None], nbatch, seq)
        d_proj, d_mcw, dgc[l] = _convmix_bwd(proj, d_cat, d_proj, mcw[l], conv_out_g[l][None], nbatch, seq)
        g_in = _matmul(
            h1, d_proj, grid=(1, D_IN // TN_IN, 1), dims=TN, name=f"g_in_{l}",
            a_spec=pl.BlockSpec((t, D_MODEL), full),
            b_spec=pl.BlockSpec((t, TN_IN), lambda i, j, k: (0, j)),
            o_spec=pl.BlockSpec((D_MODEL, TN_IN), lambda i, j, k: (0, j)),
            o_shape=(D_MODEL, D_IN), o_dtype=BF16)
        g_cw = jnp.concatenate(
            [d_fcw.reshape(N_DEV, 3, UP_CHUNK), d_mcw.reshape(3, N_DEV, D_CONV // N_DEV).transpose(1, 0, 2)], axis=-1)
        if l == 0:
            land_in0, land_cw0 = _sequencer_exchange([_Item(g_in, "cols"), _Item(g_cw, True)], "scatter_0b", 6)
        else:
            land_in1, land_out1, land_up1, land_down1, land_cw1 = _sequencer_exchange(
                [_Item(g_in, "cols"), _Item(g_out, True), _Item(g_up, True), _Item(g_down, True), _Item(g_cw, True)],
                "scatter_1", 4)
        dh1 = _matmul(
            d_proj, win[l], grid=(nta, 1, 1), dims=NT, name=f"d_h1_{l}",
            a_spec=pl.BlockSpec((TM_ACC, D_IN), lambda i, j, k: (i, 0)),
            b_spec=pl.BlockSpec((D_MODEL, D_IN), full),
            o_spec=pl.BlockSpec((TM_ACC, D_MODEL), lambda i, j, k: (i, 0)), o_shape=(t, D_MODEL), o_dtype=F32)
        dx, dxb, dg1[l] = _rms_bwd(xin, norm1_g[l][None], dh1, dxm, f"rms1_bwd_{l}")

    def pack_small(n1, a, c, n2, f):
        return jnp.concatenate(
            [n1, n2, f[None], jnp.concatenate([a, c], axis=-1), jnp.zeros((1, D_MODEL), F32)], axis=0)[None]

    small = jnp.concatenate(
        [dg1[0], dg1[1], dg2[0], dg2[1], dgf,
         jnp.concatenate([dga[0], dgc[0]], axis=-1), jnp.concatenate([dga[1], dgc[1]], axis=-1),
         jnp.pad(loss_part, ((0, 0), (0, D_MODEL - LANES)))], axis=0)
    (land_small,) = _exchange([_Item(small, False)], "gather_gain_grads")
    res_small = _adamw(
        [land_small], pack_small(norm1_g, attn_out_g, conv_out_g, norm2_g, final_norm_g),
        pack_small(m_norm1_g, m_attn_out_g, m_conv_out_g, m_norm2_g, m_final_norm_g),
        pack_small(v_norm1_g, v_attn_out_g, v_conv_out_g, v_norm2_g, v_final_norm_g), SUBLANES, "adamw_gains")
    res_out = _adamw([land_out0, land_out1], w_out, m_w_out, v_w_out, out_rows, "adamw_w_out", after=[res_small[0]])
    res_up_t = _adamw([land_up0, land_up1], up_t, m_up_t, v_up_t, UP_CHUNK // 4, "adamw_ffn_up", after=[res_out[0]])
    res_up = [jnp.swapaxes(r, 1, 2) for r in res_up_t]
    res_down = _adamw([land_down0, land_down1], ffn_down, m_ffn_down, v_ffn_down, down_rows, "adamw_ffn_down",
                      after=[res_up_t[0]])
    res_in = _adamw([land_in0, land_in1], w_in, m_w_in, v_w_in, 256, "adamw_w_in", after=[res_down[0]])
    res_cw = _adamw(
        [land_cw0, land_cw1], cw_local, jnp.concatenate([m_ffn_conv_w, m_mix_conv_w], axis=-1),
        jnp.concatenate([v_ffn_conv_w, v_mix_conv_w], axis=-1), 3, "adamw_conv_w", after=[res_in[0]])

    loss = res_small[0][0, SUBLANES - 1, 0]

    def unpack(kind):
        s = res_small[kind][0]
        cwr = res_cw[kind]
        return (s[0:2], res_in[kind], cwr[..., UP_CHUNK:], s[5:7, :D_ATTN], s[5:7, D_ATTN:], res_out[kind],
                s[2:4], res_up[kind], cwr[..., :UP_CHUNK], res_down[kind], s[4])

    return (loss, dx.reshape(nbatch, seq, d), *unpack(0), *unpack(1), *unpack(2), *unpack(3))
```

```python
import math

import jax
import jax.numpy as jnp
from jax import lax
from jax.experimental import pallas as pl
from jax.experimental.pallas import tpu as pltpu
from jax.experimental.pallas import tpu_sc as plsc

F32 = jnp.float32
BF16 = jnp.bfloat16

D_MODEL = 1024
D_ATTN = 512
D_CONV = 512
HEAD_DIM = 64
N_HEADS = 8
D_FF = 2816
DEPTH = 2
D_IN = 3 * D_ATTN + 3 * D_CONV
EPS = 1e-6
DILATIONS = (1, 4, 16)
BAND = 128
N_DEV = 8
IN_CHUNK = D_IN // N_DEV
UP_CHUNK = 2 * D_FF // N_DEV
N_UP_PAIRS = N_DEV // 2
CW_PACK = UP_CHUNK + D_CONV // N_DEV
ADAM_LR = 0.001
ADAM_B1 = 0.9
ADAM_B2 = 0.999
ADAM_EPS = 1e-08
ADAM_WD = 0.01
ADAM_STEP = 10
LANES = 128
SUBLANES = 8
VMEM_LIMIT = 56 * 1024 * 1024

NEG = -1e30
MESH = pl.DeviceIdType.MESH


def _params(sem=None, vmem=VMEM_LIMIT):
    return pltpu.CompilerParams(dimension_semantics=sem, vmem_limit_bytes=vmem)


NN = (((1,), (0,)), ((), ()))
NT = (((1,), (1,)), ((), ()))
TN = (((0,), (0,)), ((), ()))


def _contract(a_ref, b_ref, dims):
    def dot(av, bv):
        return lax.dot_general(av.astype(BF16), bv.astype(BF16), dims, preferred_element_type=F32)

    if len(a_ref.shape) == 2:
        return dot(a_ref[...], b_ref[...])
    part = dot(a_ref[0], b_ref[0])
    for c in range(1, a_ref.shape[0]):
        part = part + dot(a_ref[c], b_ref[c])
    return part


def _matmul(a, b, *, grid, a_spec, b_spec, o_spec, o_shape, o_dtype, dims, name, res=None, res_spec=None, after=()):
    nk = grid[2]
    o_block = tuple(s for s in o_spec.block_shape if s is not None)
    na = len(after)

    def body(*refs):
        refs = refs[:2 + (res is not None)] + refs[2 + (res is not None) + na:]
        if res is None:
            a_ref, b_ref, o_ref, *scr = refs
            r_ref = None
        else:
            a_ref, b_ref, r_ref, o_ref, *scr = refs
        part = _contract(a_ref, b_ref, dims)

        def finish(total):
            if r_ref is not None:
                total = total + r_ref[...]
            o_ref[...] = total.astype(o_dtype)

        if nk == 1:
            finish(part)
        else:
            acc = scr[0]
            k = pl.program_id(2)

            @pl.when(k == 0)
            def _():
                acc[...] = part

            @pl.when(k > 0)
            def _():
                acc[...] += part

            @pl.when(k == nk - 1)
            def _():
                finish(acc[...])

    in_specs = [a_spec, b_spec] + ([res_spec] if res is not None else []) + [pl.BlockSpec(memory_space=pl.ANY)] * na
    args = (a, b) + ((res,) if res is not None else ()) + tuple(after)
    return pl.pallas_call(
        body, name=name, grid=grid, in_specs=in_specs, out_specs=o_spec,
        out_shape=jax.ShapeDtypeStruct(o_shape, o_dtype),
        scratch_shapes=[pltpu.VMEM(o_block, F32)] if nk > 1 else [],
        compiler_params=_params(("parallel", "parallel", "arbitrary")),
    )(*args)


ROW_TILE = 512


def _rms_fwd(x, g, name):
    t, d = x.shape

    def body(x_ref, g_ref, h_ref):
        xv = x_ref[...]
        r = lax.rsqrt(jnp.mean(xv * xv, axis=-1, keepdims=True) + EPS)
        h_ref[...] = (xv * r * g_ref[...]).astype(BF16)

    return pl.pallas_call(
        body, name=name, grid=(t // ROW_TILE,),
        in_specs=[pl.BlockSpec((ROW_TILE, d), lambda i: (i, 0)), pl.BlockSpec((1, d), lambda i: (0, 0))],
        out_specs=pl.BlockSpec((ROW_TILE, d), lambda i: (i, 0)),
        out_shape=jax.ShapeDtypeStruct((t, d), BF16),
        compiler_params=_params(("parallel",)),
    )(x, g)


def _rms_bwd(x, g, dh, dres, name):
    t, d = x.shape

    def body(x_ref, g_ref, dh_ref, dres_ref, dx_ref, dxb_ref, dg_ref):
        xv = x_ref[...]
        r = lax.rsqrt(jnp.mean(xv * xv, axis=-1, keepdims=True) + EPS)
        xh = xv * r
        dhv = dh_ref[...]
        gd = dhv * g_ref[...]
        dx = r * (gd - xh * jnp.mean(gd * xh, axis=-1, keepdims=True)) + dres_ref[...]
        dx_ref[...] = dx
        dxb_ref[...] = dx.astype(BF16)
        part = jnp.sum(dhv * xh, axis=0, keepdims=True)

        @pl.when(pl.program_id(0) == 0)
        def _():
            dg_ref[...] = part

        @pl.when(pl.program_id(0) > 0)
        def _():
            dg_ref[...] += part

    row = pl.BlockSpec((ROW_TILE, d), lambda i: (i, 0))
    vec = pl.BlockSpec((1, d), lambda i: (0, 0))
    return pl.pallas_call(
        body, name=name, grid=(t // ROW_TILE,),
        in_specs=[row, vec, row, row], out_specs=[row, row, vec],
        out_shape=[jax.ShapeDtypeStruct((t, d), F32), jax.ShapeDtypeStruct((t, d), BF16),
                   jax.ShapeDtypeStruct((1, d), F32)],
        compiler_params=_params(("arbitrary",)),
    )(x, g, dh, dres)


def _matmul_norm(a, b, res, g, *, a_spec, b_spec, dims, name):
    t, d = res.shape

    def body(a_ref, b_ref, r_ref, g_ref, x_ref, h_ref):
        xv = _contract(a_ref, b_ref, dims) + r_ref[...]
        x_ref[...] = xv
        h_ref[...] = (xv * lax.rsqrt(jnp.mean(xv * xv, axis=-1, keepdims=True) + EPS) * g_ref[...]).astype(BF16)

    row = pl.BlockSpec((TM_ACC, d), lambda i: (i, 0))
    return pl.pallas_call(
        body, name=name, grid=(t // TM_ACC,),
        in_specs=[a_spec, b_spec, row, pl.BlockSpec((1, d), lambda i: (0, 0))], out_specs=[row, row],
        out_shape=[jax.ShapeDtypeStruct((t, d), F32), jax.ShapeDtypeStruct((t, d), BF16)],
        compiler_params=_params(("parallel",)),
    )(a, b, res, g)


def _matmul_norm_bwd(a, b, x, g, dres, *, a_spec, b_spec, dims, name):
    t, d = x.shape

    def body(a_ref, b_ref, x_ref, g_ref, dres_ref, dx_ref, dxb_ref, dg_ref):
        dhv = _contract(a_ref, b_ref, dims)
        xv = x_ref[...]
        r = lax.rsqrt(jnp.mean(xv * xv, axis=-1, keepdims=True) + EPS)
        xh = xv * r
        gd = dhv * g_ref[...]
        dx = r * (gd - xh * jnp.mean(gd * xh, axis=-1, keepdims=True)) + dres_ref[...]
        dx_ref[...] = dx
        dxb_ref[...] = dx.astype(BF16)
        part = jnp.sum(dhv * xh, axis=0, keepdims=True)

        @pl.when(pl.program_id(0) == 0)
        def _():
            dg_ref[...] = part

        @pl.when(pl.program_id(0) > 0)
        def _():
            dg_ref[...] += part

    row = pl.BlockSpec((TM_ACC, d), lambda i: (i, 0))
    vec = pl.BlockSpec((1, d), lambda i: (0, 0))
    return pl.pallas_call(
        body, name=name, grid=(t // TM_ACC,),
        in_specs=[a_spec, b_spec, row, vec, row], out_specs=[row, row, vec],
        out_shape=[jax.ShapeDtypeStruct((t, d), F32), jax.ShapeDtypeStruct((t, d), BF16),
                   jax.ShapeDtypeStruct((1, d), F32)],
        compiler_params=_params(("arbitrary",)),
    )(a, b, x, g, dres)


def _loss_head(x, g, target, name):
    t, d = x.shape

    def body(x_ref, g_ref, t_ref, loss_ref, dx_ref, dxb_ref, dg_ref):
        xv = x_ref[...]
        r = lax.rsqrt(jnp.mean(xv * xv, axis=-1, keepdims=True) + EPS)
        xh = xv * r
        gv = g_ref[...]
        err = xh * gv - t_ref[...]
        loss = jnp.full((1, LANES), 0.5 / d, F32) * jnp.sum(err * err)
        dy = err * (1.0 / d)
        gd = dy * gv
        dx = r * (gd - xh * jnp.mean(gd * xh, axis=-1, keepdims=True))
        dx_ref[...] = dx
        dxb_ref[...] = dx.astype(BF16)
        part = jnp.sum(dy * xh, axis=0, keepdims=True)

        @pl.when(pl.program_id(0) == 0)
        def _():
            dg_ref[...] = part
            loss_ref[...] = loss

        @pl.when(pl.program_id(0) > 0)
        def _():
            dg_ref[...] += part
            loss_ref[...] += loss

    row = pl.BlockSpec((ROW_TILE, d), lambda i: (i, 0))
    vec = pl.BlockSpec((1, d), lambda i: (0, 0))
    return pl.pallas_call(
        body, name=name, grid=(t // ROW_TILE,),
        in_specs=[row, vec, row],
        out_specs=[pl.BlockSpec((1, LANES), lambda i: (0, 0)), row, row, vec],
        out_shape=[jax.ShapeDtypeStruct((1, LANES), F32), jax.ShapeDtypeStruct((t, d), F32),
                   jax.ShapeDtypeStruct((t, d), BF16), jax.ShapeDtypeStruct((1, d), F32)],
        compiler_params=_params(("arbitrary",)),
    )(x, g, target)


def _group_matrix(n):
    shift = int(math.log2(HEAD_DIM))
    r = lax.broadcasted_iota(jnp.int32, (n, n), 0) >> shift
    c = lax.broadcasted_iota(jnp.int32, (n, n), 1) >> shift
    return (r == c).astype(BF16)


def _group_sum(v, gmat):
    hi = v.astype(BF16)
    lo = (v - hi.astype(F32)).astype(BF16)

    def dot(p):
        return jnp.dot(p, gmat, preferred_element_type=F32)

    return dot(hi) + dot(lo)


def _shift_rows(ext, k):
    return pltpu.roll(ext, k % ext.shape[0], 0)


def _store_columns(stage, out_hbm, sems, row0, nrows, col_blocks):
    rows = pl.ds(pl.multiple_of(row0, SUBLANES * 2), nrows)
    copies = [
        pltpu.make_async_copy(stage.at[i], out_hbm.at[rows, pl.ds(pl.multiple_of(cb * LANES, LANES), LANES)], sems.at[i])
        for i, cb in enumerate(col_blocks)
    ]
    for cp in copies:
        cp.start()
    for cp in copies:
        cp.wait()


def _attn_consts(width):
    i = lax.broadcasted_iota(jnp.int32, (BAND, width), 0)
    j = lax.broadcasted_iota(jnp.int32, (BAND, width), 1)
    dist = (width - BAND) + i - j
    inwin = (dist >= 0) & (dist <= BAND)
    return dist.astype(F32), inwin, j


def _head_masks():
    lane = lax.broadcasted_iota(jnp.int32, (1, LANES), 1)
    return [(lane < HEAD_DIM).astype(F32), (lane >= HEAD_DIM).astype(F32)]


def _pair_bias(slope, dil):
    distf, inwin, _ = _attn_consts(2 * BAND)
    return jnp.concatenate([jnp.where(inwin, distf * (slope[hh] * (-float(dil))), NEG) for hh in range(2)], axis=0)


def _stack_heads(xv, hmask):
    return jnp.concatenate([xv * hmask[0], xv * hmask[1]], axis=0).astype(BF16)


FWD_UNROLL = 8
BWD_UNROLL = 8


def _unroll(trips, most):
    return max(u for u in range(1, most + 1) if trips % u == 0)


def _for_blocks(seq, dil, block, most):
    nb = seq // dil // BAND

    def residue(r, carry):
        base = r * nb
        block(pl.multiple_of(base * BAND, BAND), None)
        if nb > 1:
            def rest(n, c):
                block(pl.multiple_of((base + n) * BAND, BAND), pl.multiple_of((base + n - 1) * BAND, BAND))
                return c

            lax.fori_loop(1, nb, rest, 0, unroll=_unroll(nb - 1, most))
        return carry

    if dil == 1:
        residue(0, 0)
    else:
        lax.fori_loop(0, dil, residue, 0, unroll=_unroll(dil, max(1, most // nb)))


def _permute_in(src_ref, dst_ref, dil, seq):
    length = seq // dil
    for r in range(dil):
        dst_ref[pl.ds(r * length, length), :] = src_ref[pl.ds(r, length, stride=dil), :].astype(dst_ref.dtype)


def _slopes_table():
    slopes = 2.0 ** (-8.0 * jnp.arange(1, N_HEADS + 1, dtype=F32) / N_HEADS)
    return jnp.broadcast_to(slopes[:, None], (N_HEADS, 2 * BAND))


def _attn_fwd(proj, attn_g, nbatch, seq):
    t = nbatch * seq
    scale = HEAD_DIM ** -0.5

    def body(q_ref, k_ref, v_ref, g_ref, sl_ref, o_ref, lse_ref, cat_ref, pq, pk, pv, po, pm, pll, ao, am, al):
        hp = pl.program_id(1)
        hmask = _head_masks()
        slope = [sl_ref[pl.ds(2 * hp + hh, 1), :] for hh in range(2)]

        def run_branch(dil, qs, ks, vs, osink, msink, lsink):
            bias = _pair_bias(slope, dil)

            def block(row0, prow):
                cur = pl.ds(row0, BAND)
                q2 = _stack_heads(qs[cur, :] * scale, hmask)
                if prow is None:
                    kk, vv, bias_b = ks[cur, :], vs[cur, :], bias[:, BAND:]
                else:
                    prev = pl.ds(prow, BAND)
                    kk = jnp.concatenate([ks[prev, :], ks[cur, :]], axis=0)
                    vv = jnp.concatenate([vs[prev, :], vs[cur, :]], axis=0)
                    bias_b = bias
                s = lax.dot_general(q2, kk.astype(BF16), NT, preferred_element_type=F32) + bias_b
                m = jnp.max(s, axis=1, keepdims=True)
                p = jnp.exp(s - m)
                l = jnp.sum(p, axis=1, keepdims=True)
                pb = p.astype(BF16)
                o = jnp.dot(jnp.concatenate([pb[:BAND], pb[BAND:]], axis=1), _stack_heads(vv, hmask),
                            preferred_element_type=F32)
                osink[cur, :] = o
                msink[cur, :] = m[:BAND] * hmask[0] + m[BAND:] * hmask[1]
                lsink[cur, :] = l[:BAND] * hmask[0] + l[BAND:] * hmask[1]

            _for_blocks(seq, dil, block, FWD_UNROLL)

        run_branch(1, q_ref, k_ref, v_ref, ao, am, al)
        for dil in DILATIONS[1:]:
            length = seq // dil
            _permute_in(q_ref, pq, dil, seq)
            _permute_in(k_ref, pk, dil, seq)
            _permute_in(v_ref, pv, dil, seq)
            run_branch(dil, pq, pk, pv, po, pm, pll)
            for r in range(dil):
                nat = pl.ds(r, length, stride=dil)
                per = pl.ds(r * length, length)
                m0 = am[nat, :]
                mb = pm[per, :]
                mn = jnp.maximum(m0, mb)
                e0 = jnp.exp(m0 - mn)
                eb = jnp.exp(mb - mn)
                ao[nat, :] = ao[nat, :] * e0 + po[per, :] * eb
                al[nat, :] = al[nat, :] * e0 + pll[per, :] * eb
                am[nat, :] = mn

        gmat = _group_matrix(LANES)
        gv = g_ref[...]

        def fin(c, carry):
            rows = pl.ds(pl.multiple_of(c * 256, 256), 256)
            lv = al[rows, :]
            o = ao[rows, :] / lv
            o_ref[rows, :] = o
            lse_ref[rows, :] = am[rows, :] + jnp.log(lv)
            ms = _group_sum(o * o, gmat) * (1.0 / HEAD_DIM)
            cat_ref[rows, :] = (o * lax.rsqrt(ms + EPS) * gv).astype(BF16)
            return carry

        lax.fori_loop(0, seq // 256, fin, 0)

    nq = D_ATTN // LANES
    blk = lambda off: pl.BlockSpec((seq, LANES), lambda b, h: (b, h + off))
    scratch = [pltpu.VMEM((seq, LANES), F32) for _ in range(9)]
    return pl.pallas_call(
        body, name="attn_fwd", grid=(nbatch, nq),
        in_specs=[blk(0), blk(nq), blk(2 * nq), pl.BlockSpec((1, LANES), lambda b, h: (0, h)),
                  pl.BlockSpec((N_HEADS, 2 * BAND), lambda b, h: (0, 0))],
        out_specs=[blk(0), blk(0), blk(0)],
        out_shape=[jax.ShapeDtypeStruct((t, D_ATTN), F32), jax.ShapeDtypeStruct((t, D_ATTN), F32),
                   jax.ShapeDtypeStruct((t, D_MODEL), BF16)],
        scratch_shapes=scratch,
        compiler_params=_params(("parallel", "parallel")),
    )(proj, proj, proj, attn_g, _slopes_table())


def _attn_bwd(proj, o, lse, d_cat, attn_g, nbatch, seq):
    t = nbatch * seq
    scale = HEAD_DIM ** -0.5

    def body(q_ref, k_ref, v_ref, o_ref, lse_ref, dy_ref, g_ref, sl_ref, dproj_ref, dg_ref,
             do_n, dl_n, dq_n, dk_n, dv_n, pq, pk, pv, pdo, plse, pdl, pdq, pdk, pdv, stage, sems):
        hp = pl.program_id(0)
        hmask = _head_masks()
        slope = [sl_ref[pl.ds(2 * hp + hh, 1), :] for hh in range(2)]
        gmat = _group_matrix(LANES)
        gv = g_ref[...]

        def prep(c, dg):
            rows = pl.ds(pl.multiple_of(c * 256, 256), 256)
            ov = o_ref[rows, :]
            dyn = dy_ref[rows, :].astype(F32)
            r = lax.rsqrt(_group_sum(ov * ov, gmat) * (1.0 / HEAD_DIM) + EPS)
            gd = dyn * gv
            oh = ov * r
            do = r * (gd - oh * (_group_sum(gd * oh, gmat) * (1.0 / HEAD_DIM)))
            do_n[rows, :] = do
            dl_n[rows, :] = _group_sum(do * ov, gmat)
            return dg + jnp.sum(dyn * oh, axis=0, keepdims=True)

        dg = lax.fori_loop(0, seq // 256, prep, jnp.zeros((1, LANES), F32))

        @pl.when(pl.program_id(1) == 0)
        def _():
            dg_ref[...] = dg

        @pl.when(pl.program_id(1) > 0)
        def _():
            dg_ref[...] += dg

        def clear(*refs):
            def step(c, carry):
                rows = pl.ds(pl.multiple_of(c * 256, 256), 256)
                for ref in refs:
                    ref[rows, :] = jnp.zeros((256, LANES), F32)
                return carry

            lax.fori_loop(0, seq // 256, step, 0)

        clear(dq_n, dk_n, dv_n)

        def run_branch(dil, qs, ks, vs, dos, lses, dls, dqs, dks, dvs):
            bias = _pair_bias(slope, dil)

            def per_head(xv):
                return jnp.concatenate([xv[:, 0:1], xv[:, HEAD_DIM:HEAD_DIM + 1]], axis=0)

            def block(row0, prow):
                cur = pl.ds(row0, BAND)
                keys = cur if prow is None else pl.ds(prow, 2 * BAND)
                q2 = _stack_heads(qs[cur, :] * scale, hmask)
                do2 = _stack_heads(dos[cur, :], hmask)
                kk, vv = ks[keys, :], vs[keys, :]
                s = lax.dot_general(q2, kk.astype(BF16), NT, preferred_element_type=F32)
                s = s + (bias[:, BAND:] if prow is None else bias)
                p = jnp.exp(s - per_head(lses[cur, :]))
                dp = lax.dot_general(do2, vv.astype(BF16), NT, preferred_element_type=F32)
                ds = (p * (dp - per_head(dls[cur, :]))).astype(BF16)
                dqs[cur, :] += jnp.dot(jnp.concatenate([ds[:BAND], ds[BAND:]], axis=1), _stack_heads(kk, hmask),
                                       preferred_element_type=F32)
                dks[keys, :] += lax.dot_general(ds, q2, TN, preferred_element_type=F32)
                dvs[keys, :] += lax.dot_general(p.astype(BF16), do2, TN, preferred_element_type=F32)

            _for_blocks(seq, dil, block, BWD_UNROLL)

        run_branch(1, q_ref, k_ref, v_ref, do_n, lse_ref, dl_n, dq_n, dk_n, dv_n)
        for dil in DILATIONS[1:]:
            length = seq // dil
            for src, dst in ((q_ref, pq), (k_ref, pk), (v_ref, pv), (do_n, pdo), (lse_ref, plse), (dl_n, pdl)):
                _permute_in(src, dst, dil, seq)
            clear(pdq, pdk, pdv)
            run_branch(dil, pq, pk, pv, pdo, plse, pdl, pdq, pdk, pdv)
            for r in range(dil):
                nat = pl.ds(r, length, stride=dil)
                per = pl.ds(r * length, length)
                dq_n[nat, :] += pdq[per, :]
                dk_n[nat, :] += pdk[per, :]
                dv_n[nat, :] += pdv[per, :]

        def emit(c, carry):
            rows = pl.ds(pl.multiple_of(c * 256, 256), 256)
            stage[0, rows, :] = (dq_n[rows, :] * scale).astype(BF16)
            stage[1, rows, :] = dk_n[rows, :].astype(BF16)
            stage[2, rows, :] = dv_n[rows, :].astype(BF16)
            return carry

        lax.fori_loop(0, seq // 256, emit, 0)
        _store_columns(stage, dproj_ref, sems, pl.program_id(1) * seq, seq, [hp, nq + hp, 2 * nq + hp])

    nq = D_ATTN // LANES
    blk = lambda off: pl.BlockSpec((seq, LANES), lambda h, b: (b, h + off))
    vec = pl.BlockSpec((1, LANES), lambda h, b: (0, h))
    scratch = [pltpu.VMEM((seq, LANES), F32) for _ in range(14)]
    scratch += [pltpu.VMEM((3, seq, LANES), BF16), pltpu.SemaphoreType.DMA((3,))]
    d_proj, dg = pl.pallas_call(
        body, name="attn_bwd", grid=(nq, nbatch),
        in_specs=[blk(0), blk(nq), blk(2 * nq), blk(0), blk(0), blk(0), vec,
                  pl.BlockSpec((N_HEADS, 2 * BAND), lambda h, b: (0, 0))],
        out_specs=[pl.BlockSpec(memory_space=pl.ANY), vec],
        out_shape=[jax.ShapeDtypeStruct((t, D_IN), BF16), jax.ShapeDtypeStruct((1, D_ATTN), F32)],
        scratch_shapes=scratch,
        compiler_params=_params(("arbitrary", "arbitrary")),
    )(proj, proj, proj, o, lse, d_cat, attn_g, _slopes_table())
    return d_proj, dg


HALO = SUBLANES
PACKED_ROWS = 2 * SUBLANES


def _window(ref, c, rows, nchunks, after):
    row0 = pl.multiple_of(c * rows, rows)
    prev0 = pl.multiple_of(jnp.maximum(row0 - PACKED_ROWS, 0), PACKED_ROWS)
    before = ref[pl.ds(prev0, PACKED_ROWS), :].astype(F32)[PACKED_ROWS - HALO:] * (c > 0).astype(F32)
    parts = [before, ref[pl.ds(row0, rows), :].astype(F32)]
    if after:
        next0 = pl.multiple_of(jnp.minimum(row0 + rows, (nchunks - 1) * rows), PACKED_ROWS)
        parts.append(ref[pl.ds(next0, PACKED_ROWS), :].astype(F32)[:HALO] * (c < nchunks - 1).astype(F32))
    return jnp.concatenate(parts, axis=0)


def _behind(z):
    z1 = _shift_rows(z, 1)
    return z1, _shift_rows(z1, 1)


def _ahead(dy):
    d1 = _shift_rows(dy, -1)
    return d1, _shift_rows(d1, -1)


def _conv(z, w):
    z1, z2 = _behind(z)
    return w[0:1] * z2 + w[1:2] * z1 + w[2:3] * z


def _conv_bwd(dy, z, w, cur):
    d1, d2 = _ahead(dy)
    dz = w[2:3] * dy + w[1:2] * d1 + w[0:1] * d2
    return dz, [jnp.sum((d * z)[cur], axis=0, keepdims=True) for d in (d2, d1, dy)]


def _sigmoid(a):
    return 0.5 * jnp.tanh(0.5 * a) + 0.5


MIX_ROWS = 256
GATE_B_BLOCK = 3 * D_ATTN // LANES
GATE_C_BLOCK = GATE_B_BLOCK + D_CONV // LANES
U_BLOCK = GATE_C_BLOCK + D_CONV // LANES


def _convmix_fwd(proj, cat, mcw, conv_g, nbatch, seq):
    nchunks = seq // MIX_ROWS

    def body(gb_ref, gc_ref, u_ref, w_ref, g_ref, cat_in, cat_ref):
        del cat_in
        gmat = _group_matrix(LANES)
        w = w_ref[...]
        gv = g_ref[...]

        def step(c, carry):
            cur = pl.ds(pl.multiple_of(c * MIX_ROWS, MIX_ROWS), MIX_ROWS)
            z = _window(gc_ref, c, MIX_ROWS, nchunks, False) * _window(u_ref, c, MIX_ROWS, nchunks, False)
            y = gb_ref[cur, :] * _conv(z, w)[HALO:]
            ms = _group_sum(y * y, gmat) * (1.0 / HEAD_DIM)
            cat_ref[cur, :] = (y * lax.rsqrt(ms + EPS) * gv).astype(BF16)
            return carry

        lax.fori_loop(0, nchunks, step, 0)

    nc = D_CONV // LANES
    blk = lambda off: pl.BlockSpec((seq, LANES), lambda b, j: (b, j + off))
    return pl.pallas_call(
        body, name="convmix_fwd", grid=(nbatch, nc),
        in_specs=[blk(GATE_B_BLOCK), blk(GATE_C_BLOCK), blk(U_BLOCK),
                  pl.BlockSpec((3, LANES), lambda b, j: (0, j)), pl.BlockSpec((1, LANES), lambda b, j: (0, j)),
                  pl.BlockSpec(memory_space=pl.ANY)],
        out_specs=blk(D_ATTN // LANES),
        out_shape=jax.ShapeDtypeStruct(cat.shape, cat.dtype),
        input_output_aliases={5: 0},
        compiler_params=_params(("parallel", "parallel")),
    )(proj, proj, proj, mcw, conv_g, cat)


def _convmix_bwd(proj, d_cat, d_proj, mcw, conv_g, nbatch, seq):
    nchunks = seq // MIX_ROWS

    def body(gb_ref, gc_ref, u_ref, dy_ref, w_ref, g_ref, dproj_in, dproj_ref, dw_ref, dg_ref, stage, sems):
        del dproj_in
        cb = pl.program_id(0)
        b = pl.program_id(1)
        gmat = _group_matrix(LANES)
        w = w_ref[...]
        gv = g_ref[...]
        cur = slice(HALO, HALO + MIX_ROWS)

        def step(c, carry):
            rows = pl.ds(pl.multiple_of(c * MIX_ROWS, MIX_ROWS), MIX_ROWS)
            gb = _window(gb_ref, c, MIX_ROWS, nchunks, True)
            gc = _window(gc_ref, c, MIX_ROWS, nchunks, True)
            u = _window(u_ref, c, MIX_ROWS, nchunks, True)
            dyn = _window(dy_ref, c, MIX_ROWS, nchunks, True)
            z = gc * u
            conv = _conv(z, w)
            y = gb * conv
            r = lax.rsqrt(_group_sum(y * y, gmat) * (1.0 / HEAD_DIM) + EPS)
            yh = y * r
            gd = dyn * gv
            dy = r * (gd - yh * (_group_sum(gd * yh, gmat) * (1.0 / HEAD_DIM)))
            dz, dws = _conv_bwd(dy * gb, z, w, cur)
            stage[0, rows, :] = (dy * conv)[cur].astype(BF16)
            stage[1, rows, :] = (dz * u)[cur].astype(BF16)
            stage[2, rows, :] = (dz * gc)[cur].astype(BF16)
            dg = jnp.sum((dyn * yh)[cur], axis=0, keepdims=True)
            return tuple(a + d for a, d in zip(carry, dws + [dg]))

        zero = jnp.zeros((1, LANES), F32)
        dw0, dw1, dw2, dg = lax.fori_loop(0, nchunks, step, (zero, zero, zero, zero))

        @pl.when(b == 0)
        def _():
            dw_ref[0:1, :] = dw0
            dw_ref[1:2, :] = dw1
            dw_ref[2:3, :] = dw2
            dg_ref[...] = dg

        @pl.when(b > 0)
        def _():
            dw_ref[0:1, :] += dw0
            dw_ref[1:2, :] += dw1
            dw_ref[2:3, :] += dw2
            dg_ref[...] += dg

        _store_columns(stage, dproj_ref, sems, b * seq, seq, [GATE_B_BLOCK + cb, GATE_C_BLOCK + cb, U_BLOCK + cb])

    nc = D_CONV // LANES
    blk = lambda off: pl.BlockSpec((seq, LANES), lambda j, b: (b, j + off))
    return pl.pallas_call(
        body, name="convmix_bwd", grid=(nc, nbatch),
        in_specs=[blk(GATE_B_BLOCK), blk(GATE_C_BLOCK), blk(U_BLOCK), blk(D_ATTN // LANES),
                  pl.BlockSpec((3, LANES), lambda j, b: (0, j)), pl.BlockSpec((1, LANES), lambda j, b: (0, j)),
                  pl.BlockSpec(memory_space=pl.ANY)],
        out_specs=[pl.BlockSpec(memory_space=pl.ANY), pl.BlockSpec((3, LANES), lambda j, b: (0, j)),
                   pl.BlockSpec((1, LANES), lambda j, b: (0, j))],
        out_shape=[jax.ShapeDtypeStruct(d_proj.shape, d_proj.dtype), jax.ShapeDtypeStruct((3, D_CONV), F32),
                   jax.ShapeDtypeStruct((1, D_CONV), F32)],
        scratch_shapes=[pltpu.VMEM((3, seq, LANES), BF16), pltpu.SemaphoreType.DMA((3,))],
        input_output_aliases={6: 0},
        compiler_params=_params(("arbitrary", "arbitrary")),
    )(proj, proj, proj, d_cat, mcw, conv_g, d_proj)


FFN_ROWS = 256


def _ffn_act_fwd(pre, fcw, nbatch, seq):
    t = nbatch * seq
    nchunks = seq // FFN_ROWS

    def body(pre_ref, w_ref, act_ref):
        wa = w_ref[0]
        wc = w_ref[1]

        def step(c, carry):
            cur = pl.ds(pl.multiple_of(c * FFN_ROWS, FFN_ROWS), FFN_ROWS)
            a = _conv(_window(pre_ref.at[0], c, FFN_ROWS, nchunks, False), wa)[HALO:]
            v = _conv(_window(pre_ref.at[1], c, FFN_ROWS, nchunks, False), wc)[HALO:]
            act_ref[cur, :] = (a * _sigmoid(a) * v).astype(BF16)
            return carry

        lax.fori_loop(0, nchunks, step, 0)

    return pl.pallas_call(
        body, name="ffn_act_fwd", grid=(N_UP_PAIRS, nbatch),
        in_specs=[pl.BlockSpec((2, None, seq, UP_CHUNK), lambda i, b: (0, i, b, 0)),
                  pl.BlockSpec((2, None, 3, UP_CHUNK), lambda i, b: (0, i, 0, 0))],
        out_specs=pl.BlockSpec((None, seq, UP_CHUNK), lambda i, b: (i, b, 0)),
        out_shape=jax.ShapeDtypeStruct((N_UP_PAIRS, t, UP_CHUNK), BF16),
        compiler_params=_params(("parallel", "parallel")),
    )(pre, fcw)


def _ffn_act_bwd(pre, d_act, fcw, nbatch, seq):
    nchunks = seq // FFN_ROWS

    def body(pre_ref, da_ref, w_ref, dpre_ref, dw_ref):
        b = pl.program_id(1)
        wa = w_ref[0]
        wc = w_ref[1]
        cur = slice(HALO, HALO + FFN_ROWS)

        def step(c, carry):
            rows = pl.ds(pl.multiple_of(c * FFN_ROWS, FFN_ROWS), FFN_ROWS)
            pg = _window(pre_ref.at[0], c, FFN_ROWS, nchunks, True)
            pv = _window(pre_ref.at[1], c, FFN_ROWS, nchunks, True)
            dact = _window(da_ref, c, FFN_ROWS, nchunks, True)
            a = _conv(pg, wa)
            v = _conv(pv, wc)
            sg = _sigmoid(a)
            asg = a * sg
            dzg, dwg = _conv_bwd(dact * v * (sg + asg - asg * sg), pg, wa, cur)
            dzv, dwv = _conv_bwd(dact * asg, pv, wc, cur)
            dpre_ref[0, rows, :] = dzg[cur].astype(BF16)
            dpre_ref[1, rows, :] = dzv[cur].astype(BF16)
            return tuple(acc + d for acc, d in zip(carry, dwg + dwv))

        zero = jnp.zeros((1, UP_CHUNK), F32)
        sums = lax.fori_loop(0, nchunks, step, (zero,) * 6)

        @pl.when(b == 0)
        def _():
            for i in range(6):
                dw_ref[i // 3, pl.ds(i % 3, 1), :] = sums[i]

        @pl.when(b > 0)
        def _():
            for i in range(6):
                dw_ref[i // 3, pl.ds(i % 3, 1), :] += sums[i]

    pair = pl.BlockSpec((2, None, seq, UP_CHUNK), lambda i, b: (0, i, b, 0))
    wspec = pl.BlockSpec((2, None, 3, UP_CHUNK), lambda i, b: (0, i, 0, 0))
    return pl.pallas_call(
        body, name="ffn_act_bwd", grid=(N_UP_PAIRS, nbatch),
        in_specs=[pair, pl.BlockSpec((None, seq, UP_CHUNK), lambda i, b: (i, b, 0)), wspec],
        out_specs=[pair, wspec],
        out_shape=[jax.ShapeDtypeStruct(pre.shape, BF16), jax.ShapeDtypeStruct(fcw.shape, F32)],
        compiler_params=_params(("parallel", "arbitrary")),
    )(pre, d_act, fcw)


def _adamw(lands, w, m, v, row_tile, name, after=()):
    nl = len(lands)
    _, nr, ncol = lands[0].shape
    c1 = 1.0 - ADAM_B1 ** ADAM_STEP
    c2 = 1.0 - ADAM_B2 ** ADAM_STEP

    def body(*refs):
        land_refs = refs[:nl]
        w_ref, m_ref, v_ref = refs[nl:nl + 3]
        g_ref, d_ref, mo_ref, vo_ref = refs[nl + 3 + len(after):]
        for l in range(nl):
            @pl.when(pl.program_id(0) == l)
            def _(l=l):
                g = land_refs[l][0].astype(F32)
                for j in range(1, N_DEV):
                    g = g + land_refs[l][j].astype(F32)
                g_ref[...] = g

        g = g_ref[...]
        m2 = ADAM_B1 * m_ref[...] + (1.0 - ADAM_B1) * g
        v2 = ADAM_B2 * v_ref[...] + (1.0 - ADAM_B2) * (g * g)
        mo_ref[...] = m2
        vo_ref[...] = v2
        d_ref[...] = -ADAM_LR * ((m2 / c1) / (jnp.sqrt(v2 / c2) + ADAM_EPS) + ADAM_WD * w_ref[...])

    def land_spec(l):
        return pl.BlockSpec((N_DEV, row_tile, ncol), lambda k, i: (0, jnp.where(k == l, i, 0), 0))

    tile = pl.BlockSpec((None, row_tile, ncol), lambda k, i: (k, i, 0))
    return pl.pallas_call(
        body, name=name, grid=(nl, nr // row_tile),
        in_specs=[land_spec(l) for l in range(nl)] + [tile, tile, tile] + [pl.BlockSpec(memory_space=pl.ANY)] * len(after),
        out_specs=[tile] * 4,
        out_shape=[jax.ShapeDtypeStruct(w.shape, F32)] * 4,
        compiler_params=_params(("arbitrary", "arbitrary")),
    )(*lands, w, m, v, *after)


class _Item:
    def __init__(self, src, chunked, land_cols=False):
        self.src, self.chunked, self.land_cols = src, chunked, land_cols
        if chunked == "cols":
            block = (src.shape[0], src.shape[1] // N_DEV)
        else:
            block = src.shape[1:] if chunked else src.shape
        self.width = block[-1]
        self.land_shape = (block[0], N_DEV * block[1]) if land_cols else (N_DEV,) + block

    def _cols(self, first, count=1):
        return pl.ds(pl.multiple_of(first * self.width, LANES), count * self.width)

    def part(self, src_ref, j):
        if self.chunked == "cols":
            return src_ref.at[:, self._cols(j)]
        return src_ref.at[j] if self.chunked else src_ref

    def slot(self, land_ref, s):
        return land_ref.at[:, self._cols(s)] if self.land_cols else land_ref.at[s]

    def seven(self, land_ref):
        return land_ref.at[:, self._cols(0, N_DEV - 1)] if self.land_cols else land_ref.at[pl.ds(0, N_DEV - 1)]


def _mesh_place():
    x, y, c = lax.axis_index("x"), lax.axis_index("y"), lax.axis_index("c")
    return x, y, c, 4 * x + 2 * y + c


def _flipped(x, y, c, k):
    px = 1 - x if k & 4 else x
    py = 1 - y if k & 2 else y
    pc = 1 - c if k & 1 else c
    return (px, py, pc), 4 * px + 2 * py + pc


PEER_ORDER = (2, 4, 6, 3, 5, 7, 1)


def _exchange(items, name):
    n = len(items)

    def body(*refs):
        srcs, lands = refs[:n], refs[n:2 * n]
        send, recv, local = refs[2 * n:]
        x, y, c, me = _mesh_place()

        def copy(i, k, chunk, slot, dev):
            return pltpu.make_async_remote_copy(
                src_ref=items[i].part(srcs[i], chunk), dst_ref=items[i].slot(lands[i], slot),
                send_sem=send.at[i, k - 1], recv_sem=recv.at[i, k - 1], device_id=dev, device_id_type=MESH)

        own = [pltpu.make_async_copy(items[i].part(srcs[i], me), items[i].slot(lands[i], me), local.at[i])
               for i in range(n)]
        for k in PEER_ORDER:
            dev, idx = _flipped(x, y, c, k)
            for i in range(n):
                copy(i, k, idx, me, dev).start()
        for cp in own:
            cp.start()
        for k in PEER_ORDER:
            dev, idx = _flipped(x, y, c, k)
            for i in range(n):
                copy(i, k, me, idx, dev).wait_recv()
        for k in PEER_ORDER:
            dev, idx = _flipped(x, y, c, k)
            for i in range(n):
                copy(i, k, idx, me, dev).wait_send()
        for cp in own:
            cp.wait()

    hbm = pl.BlockSpec(memory_space=pl.ANY)
    return pl.pallas_call(
        body, name=name,
        in_specs=[hbm] * n, out_specs=[hbm] * n,
        out_shape=[jax.ShapeDtypeStruct(it.land_shape, it.src.dtype) for it in items],
        scratch_shapes=[pltpu.SemaphoreType.DMA((n, N_DEV - 1)), pltpu.SemaphoreType.DMA((n, N_DEV - 1)),
                        pltpu.SemaphoreType.DMA((n,))],
        compiler_params=pltpu.CompilerParams(has_side_effects=True),
    )(*[it.src for it in items])


def _sequencer_exchange(items, name, collective_id):
    n = len(items)

    def body(*refs):
        srcs, lands = refs[:n], refs[n:2 * n]
        send, recv, local = refs[2 * n:]
        x, y, c, me = _mesh_place()
        barrier = pltpu.get_barrier_semaphore()
        for k in PEER_ORDER:
            pl.semaphore_signal(barrier, inc=1, device_id=_flipped(x, y, c, k)[0], device_id_type=MESH)
        pl.semaphore_wait(barrier, N_DEV - 1)

        def copy(i, k, chunk, slot, dev):
            return pltpu.make_async_remote_copy(
                src_ref=items[i].part(srcs[i], chunk), dst_ref=items[i].slot(lands[i], slot),
                send_sem=send.at[i, k - 1], recv_sem=recv.at[i, k - 1], device_id=dev, device_id_type=MESH)

        own = [pltpu.make_async_copy(items[i].part(srcs[i], me), items[i].slot(lands[i], me), local.at[i])
               for i in range(n)]
        for cp in own:
            cp.start()
        for k in PEER_ORDER:
            dev, idx = _flipped(x, y, c, k)
            for i in range(n):
                copy(i, k, idx, me, dev).start()
        for k in PEER_ORDER:
            dev, idx = _flipped(x, y, c, k)
            for i in range(n):
                copy(i, k, me, idx, dev).wait_recv()
        for k in PEER_ORDER:
            dev, idx = _flipped(x, y, c, k)
            for i in range(n):
                copy(i, k, idx, me, dev).wait_send()
        for cp in own:
            cp.wait()

    return pl.kernel(
        body, name=name,
        out_type=[jax.ShapeDtypeStruct(it.land_shape, it.src.dtype) for it in items],
        mesh=plsc.ScalarSubcoreMesh(axis_name="sequencer", num_cores=1),
        scratch_types=[pltpu.SemaphoreType.DMA((n, N_DEV - 1)), pltpu.SemaphoreType.DMA((n, N_DEV - 1)),
                       pltpu.SemaphoreType.DMA((n,))],
        compiler_params=pltpu.CompilerParams(collective_id=collective_id),
    )(*[it.src for it in items])


HBM_SPEC = pl.BlockSpec(memory_space=pltpu.HBM)
SEM_SPEC = pl.BlockSpec(memory_space=pltpu.SEMAPHORE)
DATAFLOW = pltpu.SideEffectType.DATAFLOW_SIDE_EFFECTING


def _exchange_start(items, name, after=()):
    n = len(items)
    na = len(after)

    def body(*refs):
        srcs, land_ins = refs[:n], refs[n:2 * n]
        outs = refs[2 * n + na:6 * n + na]
        (local,) = refs[6 * n + na:]
        del land_ins
        x, y, c, me = _mesh_place()
        own = [pltpu.make_async_copy(items[i].part(srcs[i], me), items[i].slot(outs[4 * i + 3], me), local.at[i])
               for i in range(n)]
        for cp in own:
            cp.start()
        for cp in own:
            cp.wait()
        for k in PEER_ORDER:
            dev, idx = _flipped(x, y, c, k)
            for i in range(n):
                send, recv, _, land = outs[4 * i:4 * i + 4]
                pltpu.make_async_remote_copy(
                    src_ref=items[i].part(srcs[i], idx), dst_ref=items[i].slot(land, me), send_sem=send, recv_sem=recv,
                    device_id=dev, device_id_type=MESH).start()

    out_shape, out_specs, args, lands = [], [], [], []
    for it in items:
        out_shape += [pltpu.SemaphoreType.DMA(()), pltpu.SemaphoreType.DMA(()),
                      pltpu.HBM(it.src.shape, it.src.dtype), pltpu.HBM(it.land_shape, it.src.dtype)]
        out_specs += [SEM_SPEC, SEM_SPEC, HBM_SPEC, HBM_SPEC]
        args.append(pltpu.with_memory_space_constraint(it.src, pltpu.HBM))
        lands.append(pltpu.with_memory_space_constraint(lax.empty(it.land_shape, it.src.dtype), pltpu.HBM))
    outs = pl.pallas_call(
        body, name=name,
        in_specs=[HBM_SPEC] * (2 * n) + [pl.BlockSpec(memory_space=pl.ANY)] * na,
        out_specs=out_specs, out_shape=out_shape,
        scratch_shapes=[pltpu.SemaphoreType.DMA((n,))],
        input_output_aliases={**{i: 4 * i + 2 for i in range(n)}, **{n + i: 4 * i + 3 for i in range(n)}},
        compiler_params=pltpu.CompilerParams(has_side_effects=DATAFLOW),
    )(*args, *lands, *after)
    return [tuple(outs[4 * i:4 * i + 4]) + (items[i],) for i in range(n)]


def _started(handles):
    return handles[0][2]


def _exchange_wait(handles, after, name):
    n = len(handles)

    def body(*refs):
        x, y, c, _ = _mesh_place()
        for i in range(n):
            src, land, send, recv = refs[4 * i:4 * i + 4]
            del src
            seven = handles[i][4].seven(land)
            cp = pltpu.make_async_remote_copy(src_ref=seven, dst_ref=seven, send_sem=send, recv_sem=recv,
                                              device_id=(x, y, 1 - c), device_id_type=MESH)
            cp.wait_send()
            cp.wait_recv()

    args, in_specs, out_shape = [], [], []
    for send, recv, src, land, _ in handles:
        args += [src, land, send, recv]
        in_specs += [HBM_SPEC, HBM_SPEC, SEM_SPEC, SEM_SPEC]
        out_shape += [pltpu.HBM(src.shape, src.dtype), pltpu.HBM(land.shape, land.dtype)]
    outs = pl.pallas_call(
        body, name=name,
        in_specs=in_specs + [pl.BlockSpec(memory_space=pl.ANY)] * len(after), out_specs=[HBM_SPEC] * (2 * n),
        out_shape=out_shape,
        input_output_aliases={**{4 * i: 2 * i for i in range(n)}, **{4 * i + 1: 2 * i + 1 for i in range(n)}},
        compiler_params=pltpu.CompilerParams(has_side_effects=DATAFLOW),
    )(*args, *after)
    return [outs[2 * i + 1] for i in range(n)]


TM = 1024
TM_ACC = 512
TN_IN = 768


def kernel(x, norm1_g, w_in, mix_conv_w, attn_out_g, conv_out_g, w_out, norm2_g, ffn_up, ffn_conv_w, ffn_down, final_norm_g, loss_target, m_norm1_g, m_w_in, m_mix_conv_w, m_attn_out_g, m_conv_out_g, m_w_out, m_norm2_g, m_ffn_up, m_ffn_conv_w, m_ffn_down, m_final_norm_g, v_norm1_g, v_w_in, v_mix_conv_w, v_attn_out_g, v_conv_out_g, v_w_out, v_norm2_g, v_ffn_up, v_ffn_conv_w, v_ffn_down, v_final_norm_g):
    nbatch, seq, d = x.shape
    t = nbatch * seq
    nt, nta = t // TM, t // TM_ACC
    out_rows = D_MODEL // N_DEV
    down_rows = D_FF // N_DEV
    xf = x.reshape(t, d)
    target = loss_target.reshape(t, d)

    cw_local = jnp.concatenate([ffn_conv_w, mix_conv_w], axis=-1)
    cast = lambda w: _Item(w.astype(BF16), False)
    cast_in = lambda w: _Item(w.astype(BF16), False, land_cols=True)
    cw_all, win0 = _sequencer_exchange([_Item(cw_local, False), cast_in(w_in[0])], "gather_a", 0)
    up_t, m_up_t, v_up_t = (jnp.swapaxes(a, 1, 2) for a in (ffn_up, m_ffn_up, v_ffn_up))
    wout0, wup0 = _sequencer_exchange([cast(w_out[0]), cast(up_t[0])], "gather_b", 1)
    wdown0, win1, wout1 = _sequencer_exchange([cast(ffn_down[0]), cast_in(w_in[1]), cast(w_out[1])], "gather_c", 2)
    wup1, wdown1 = _sequencer_exchange([cast(up_t[1]), cast(ffn_down[1])], "gather_d", 3)
    win, wup = [win0, win1], [wup0, wup1]
    wout = [w.reshape(D_MODEL, D_MODEL) for w in (wout0, wout1)]
    wdown = [w.reshape(N_UP_PAIRS, UP_CHUNK, D_MODEL) for w in (wdown0, wdown1)]
    fcw = [cw_all[:, k, :, :UP_CHUNK].reshape(2, N_UP_PAIRS, 3, UP_CHUNK) for k in range(DEPTH)]
    mcw = [cw_all[:, k, :, UP_CHUNK:].transpose(1, 0, 2).reshape(3, D_CONV) for k in range(DEPTH)]

    full = lambda i, j, k: (0, 0)

    saved = []
    xin = xf
    h1 = _rms_fwd(xin, norm1_g[0][None], "rms1_fwd_0")
    rows_of = lambda width: pl.BlockSpec((TM_ACC, width), lambda i: (i, 0))
    whole = lambda *shape: pl.BlockSpec(shape, lambda i: (0,) * len(shape))
    chunks_of = lambda n: pl.BlockSpec((n, TM_ACC, UP_CHUNK), lambda i: (0, i, 0))
    for l in range(DEPTH):
        proj = _matmul(
            h1, win[l], grid=(nt, D_IN // TN_IN, 1), dims=NN, name=f"proj_{l}",
            a_spec=pl.BlockSpec((TM, D_MODEL), lambda i, j, k: (i, 0)),
            b_spec=pl.BlockSpec((D_MODEL, TN_IN), lambda i, j, k: (0, j)),
            o_spec=pl.BlockSpec((TM, TN_IN), lambda i, j, k: (i, j)), o_shape=(t, D_IN), o_dtype=F32)
        o, lse, cat = _attn_fwd(proj, attn_out_g[l][None], nbatch, seq)
        cat = _convmix_fwd(proj, cat, mcw[l], conv_out_g[l][None], nbatch, seq)
        xmid, h2 = _matmul_norm(cat, wout[l], xin, norm2_g[l][None], dims=NN, name=f"mix_out_{l}",
                                a_spec=rows_of(D_MODEL), b_spec=whole(D_MODEL, D_MODEL))
        pre = _matmul(
            h2, wup[l], grid=(nt, N_DEV, 1), dims=NT, name=f"ffn_up_{l}",
            a_spec=pl.BlockSpec((TM, D_MODEL), lambda i, j, k: (i, 0)),
            b_spec=pl.BlockSpec((None, UP_CHUNK, D_MODEL), lambda i, j, k: (j, 0, 0)),
            o_spec=pl.BlockSpec((None, TM, UP_CHUNK), lambda i, j, k: (j, i, 0)),
            o_shape=(N_DEV, t, UP_CHUNK), o_dtype=BF16).reshape(2, N_UP_PAIRS, t, UP_CHUNK)
        act = _ffn_act_fwd(pre, fcw[l], nbatch, seq)
        if l + 1 < DEPTH:
            xout, h_next = _matmul_norm(act, wdown[l], xmid, norm1_g[l + 1][None], dims=NN, name=f"ffn_down_{l}",
                                        a_spec=chunks_of(N_UP_PAIRS), b_spec=whole(N_UP_PAIRS, UP_CHUNK, D_MODEL))
        else:
            h_next = None
            xout = _matmul(
                act, wdown[l], grid=(nta, 1, 1), dims=NN, name=f"ffn_down_{l}",
                a_spec=pl.BlockSpec((N_UP_PAIRS, TM_ACC, UP_CHUNK), lambda i, j, k: (0, i, 0)),
                b_spec=pl.BlockSpec((N_UP_PAIRS, UP_CHUNK, D_MODEL), lambda i, j, k: (0, 0, 0)),
                o_spec=pl.BlockSpec((TM_ACC, D_MODEL), lambda i, j, k: (i, 0)), o_shape=(t, D_MODEL), o_dtype=F32,
                res=xmid, res_spec=pl.BlockSpec((TM_ACC, D_MODEL), lambda i, j, k: (i, 0)))
        saved.append((xin, h1, proj, o, lse, cat, xmid, h2, pre, act))
        xin, h1 = xout, h_next

    loss_part, dx, dxb, dgf = _loss_head(xin, final_norm_g[None], target, "loss_head")

    dg1, dg2, dga, dgc = [None] * DEPTH, [None] * DEPTH, [None] * DEPTH, [None] * DEPTH
    for l in reversed(range(DEPTH)):
        xin, h1, proj, o, lse, cat, xmid, h2, pre, act = saved[l]
        d_act = _matmul(
            dxb, wdown[l], grid=(nt, N_UP_PAIRS, 1), dims=NT, name=f"d_act_{l}",
            a_spec=pl.BlockSpec((TM, D_MODEL), lambda i, j, k: (i, 0)),
            b_spec=pl.BlockSpec((None, UP_CHUNK, D_MODEL), lambda i, j, k: (j, 0, 0)),
            o_spec=pl.BlockSpec((None, TM, UP_CHUNK), lambda i, j, k: (j, i, 0)),
            o_shape=(N_UP_PAIRS, t, UP_CHUNK), o_dtype=BF16)
        g_down = _matmul(
            act, dxb, grid=(N_UP_PAIRS, 1, 1), dims=TN, name=f"g_down_{l}",
            a_spec=pl.BlockSpec((None, t, UP_CHUNK), lambda i, j, k: (i, 0, 0)),
            b_spec=pl.BlockSpec((t, D_MODEL), full),
            o_spec=pl.BlockSpec((None, UP_CHUNK, D_MODEL), lambda i, j, k: (i, 0, 0)),
            o_shape=(N_UP_PAIRS, UP_CHUNK, D_MODEL), o_dtype=BF16).reshape(N_DEV, down_rows, D_MODEL)
        d_pre, d_fcw = _ffn_act_bwd(pre, d_act, fcw[l], nbatch, seq)
        d_pre = d_pre.reshape(N_DEV, t, UP_CHUNK)
        dxm, dxmb, dg2[l] = _matmul_norm_bwd(
            d_pre, wup[l], xmid, norm2_g[l][None], dx, dims=NN, name=f"d_h2_{l}",
            a_spec=chunks_of(N_DEV), b_spec=whole(N_DEV, UP_CHUNK, D_MODEL))
        g_up = _matmul(
            d_pre, h2, grid=(N_DEV, 1, 1), dims=TN, name=f"g_up_{l}",
            a_spec=pl.BlockSpec((None, t, UP_CHUNK), lambda i, j, k: (i, 0, 0)),
            b_spec=pl.BlockSpec((t, D_MODEL), full),
            o_spec=pl.BlockSpec((None, UP_CHUNK, D_MODEL), lambda i, j, k: (i, 0, 0)),
            o_shape=(N_DEV, UP_CHUNK, D_MODEL), o_dtype=BF16)
        g_out = _matmul(
            cat, dxmb, grid=(1, 1, nt), dims=TN, name=f"g_out_{l}",
            a_spec=pl.BlockSpec((TM, D_MODEL), lambda i, j, k: (k, 0)),
            b_spec=pl.BlockSpec((TM, D_MODEL), lambda i, j, k: (k, 0)),
            o_spec=pl.BlockSpec((D_MODEL, D_MODEL), full),
            o_shape=(D_MODEL, D_MODEL), o_dtype=BF16).reshape(N_DEV, out_rows, D_MODEL)
        if l == 0:
            land_out0, land_up0, land_down0 = _sequencer_exchange(
                [_Item(g_out, True), _Item(g_up, True), _Item(g_down, True)], "scatter_0a", 5)
        d_cat = _matmul(
            dxmb, wout[l], grid=(nta, 1, 1), dims=NT, name=f"d_cat_{l}",
            a_spec=pl.BlockSpec((TM_ACC, D_MODEL), lambda i, j, k: (i, 0)),
            b_spec=pl.BlockSpec((D_MODEL, D_MODEL), full),
            o_spec=pl.BlockSpec((TM_ACC, D_MODEL), lambda i, j, k: (i, 0)), o_shape=(t, D_MODEL), o_dtype=BF16)
        d_proj, dga[l] = _attn_bwd(proj, o, lse, d_cat, attn_out_g[l][None], nbatch, seq)
        d_proj, d_mcw, dgc[l] = _convmix_bwd(proj, d_cat, d_proj, mcw[l], conv_out_g[l][None], nbatch, seq)
        g_in = _matmul(
            h1, d_proj, grid=(1, D_IN // TN_IN, 1), dims=TN, name=f"g_in_{l}",
            a_spec=pl.BlockSpec((t, D_MODEL), full),
            b_spec=pl.BlockSpec((t, TN_IN), lambda i, j, k: (0, j)),
            o_spec=pl.BlockSpec((D_MODEL, TN_IN), lambda i, j, k: (0, j)),
            o_shape=(D_MODEL, D_IN), o_dtype=BF16)
        g_cw = jnp.concatenate(
            [d_fcw.reshape(N_DEV, 3, UP_CHUNK), d_mcw.reshape(3, N_DEV, D_CONV // N_DEV).transpose(1, 0, 2)], axis=-1)
        if l == 0:
            land_in0, land_cw0 = _sequencer_exchange([_Item(g_in, "cols"), _Item(g_cw, True)], "scatter_0b", 6)
        else:
            land_in1, land_out1, land_up1, land_down1, land_cw1 = _sequencer_exchange(
                [_Item(g_in, "cols"), _Item(g_out, True), _Item(g_up, True), _Item(g_down, True), _Item(g_cw, True)],
                "scatter_1", 4)
        dx, dxb, dg1[l] = _matmul_norm_bwd(
            d_proj, win[l], xin, norm1_g[l][None], dxm, dims=NT, name=f"d_h1_{l}",
            a_spec=rows_of(D_IN), b_spec=whole(D_MODEL, D_IN))

    def pack_small(n1, a, c, n2, f):
        return jnp.concatenate(
            [n1, n2, f[None], jnp.concatenate([a, c], axis=-1), jnp.zeros((1, D_MODEL), F32)], axis=0)[None]

    small = jnp.concatenate(
        [dg1[0], dg1[1], dg2[0], dg2[1], dgf,
         jnp.concatenate([dga[0], dgc[0]], axis=-1), jnp.concatenate([dga[1], dgc[1]], axis=-1),
         jnp.pad(loss_part, ((0, 0), (0, D_MODEL - LANES)))], axis=0)
    (land_small,) = _exchange([_Item(small, False)], "gather_gain_grads")
    res_small = _adamw(
        [land_small], pack_small(norm1_g, attn_out_g, conv_out_g, norm2_g, final_norm_g),
        pack_small(m_norm1_g, m_attn_out_g, m_conv_out_g, m_norm2_g, m_final_norm_g),
        pack_small(v_norm1_g, v_attn_out_g, v_conv_out_g, v_norm2_g, v_final_norm_g), SUBLANES, "adamw_gains")
    res_out = _adamw([land_out0, land_out1], w_out, m_w_out, v_w_out, out_rows, "adamw_w_out", after=[res_small[0]])
    res_up_t = _adamw([land_up0, land_up1], up_t, m_up_t, v_up_t, UP_CHUNK // 4, "adamw_ffn_up", after=[res_out[0]])
    res_up = [jnp.swapaxes(r, 1, 2) for r in res_up_t]
    res_down = _adamw([land_down0, land_down1], ffn_down, m_ffn_down, v_ffn_down, down_rows, "adamw_ffn_down",
                      after=[res_up_t[0]])
    res_in = _adamw([land_in0, land_in1], w_in, m_w_in, v_w_in, 256, "adamw_w_in", after=[res_down[0]])
    res_cw = _adamw(
        [land_cw0, land_cw1], cw_local, jnp.concatenate([m_ffn_conv_w, m_mix_conv_w], axis=-1),
        jnp.concatenate([v_ffn_conv_w, v_mix_conv_w], axis=-1), 3, "adamw_conv_w", after=[res_in[0]])

    loss = res_small[0][0, SUBLANES - 1, 0]

    def unpack(kind):
        s = res_small[kind][0]
        cwr = res_cw[kind]
        return (s[0:2], res_in[kind], cwr[..., UP_CHUNK:], s[5:7, :D_ATTN], s[5:7, D_ATTN:], res_out[kind],
                s[2:4], res_up[kind], cwr[..., :UP_CHUNK], res_down[kind], s[4])

    return (loss, dx.reshape(nbatch, seq, d), *unpack(0), *unpack(1), *unpack(2), *unpack(3))
```

```python
import math

import jax
import jax.numpy as jnp
from jax import lax
from jax.experimental import pallas as pl
from jax.experimental.pallas import tpu as pltpu
from jax.experimental.pallas import tpu_sc as plsc

F32 = jnp.float32
BF16 = jnp.bfloat16

D_MODEL = 1024
D_ATTN = 512
D_CONV = 512
HEAD_DIM = 64
N_HEADS = 8
D_FF = 2816
DEPTH = 2
D_IN = 3 * D_ATTN + 3 * D_CONV
EPS = 1e-6
DILATIONS = (1, 4, 16)
BAND = 128
N_DEV = 8
IN_CHUNK = D_IN // N_DEV
UP_CHUNK = 2 * D_FF // N_DEV
N_UP_PAIRS = N_DEV // 2
CW_PACK = UP_CHUNK + D_CONV // N_DEV
ADAM_LR = 0.001
ADAM_B1 = 0.9
ADAM_B2 = 0.999
ADAM_EPS = 1e-08
ADAM_WD = 0.01
ADAM_STEP = 10
LANES = 128
SUBLANES = 8
VMEM_LIMIT = 56 * 1024 * 1024

NEG = -1e30
MESH = pl.DeviceIdType.MESH


def _params(sem=None, vmem=VMEM_LIMIT):
    return pltpu.CompilerParams(dimension_semantics=sem, vmem_limit_bytes=vmem)


NN = (((1,), (0,)), ((), ()))
NT = (((1,), (1,)), ((), ()))
TN = (((0,), (0,)), ((), ()))


def _contract(a_ref, b_ref, dims):
    def dot(av, bv):
        return lax.dot_general(av.astype(BF16), bv.astype(BF16), dims, preferred_element_type=F32)

    if len(a_ref.shape) == 2:
        return dot(a_ref[...], b_ref[...])
    part = dot(a_ref[0], b_ref[0])
    for c in range(1, a_ref.shape[0]):
        part = part + dot(a_ref[c], b_ref[c])
    return part


def _matmul(a, b, *, grid, a_spec, b_spec, o_spec, o_shape, o_dtype, dims, name, res=None, res_spec=None, after=()):
    nk = grid[2]
    o_block = tuple(s for s in o_spec.block_shape if s is not None)
    na = len(after)

    def body(*refs):
        refs = refs[:2 + (res is not None)] + refs[2 + (res is not None) + na:]
        if res is None:
            a_ref, b_ref, o_ref, *scr = refs
            r_ref = None
        else:
            a_ref, b_ref, r_ref, o_ref, *scr = refs
        part = _contract(a_ref, b_ref, dims)

        def finish(total):
            if r_ref is not None:
                total = total + r_ref[...]
            o_ref[...] = total.astype(o_dtype)

        if nk == 1:
            finish(part)
        else:
            acc = scr[0]
            k = pl.program_id(2)

            @pl.when(k == 0)
            def _():
                acc[...] = part

            @pl.when(k > 0)
            def _():
                acc[...] += part

            @pl.when(k == nk - 1)
            def _():
                finish(acc[...])

    in_specs = [a_spec, b_spec] + ([res_spec] if res is not None else []) + [pl.BlockSpec(memory_space=pl.ANY)] * na
    args = (a, b) + ((res,) if res is not None else ()) + tuple(after)
    return pl.pallas_call(
        body, name=name, grid=grid, in_specs=in_specs, out_specs=o_spec,
        out_shape=jax.ShapeDtypeStruct(o_shape, o_dtype),
        scratch_shapes=[pltpu.VMEM(o_block, F32)] if nk > 1 else [],
        compiler_params=_params(("parallel", "parallel", "arbitrary")),
    )(*args)


ROW_TILE = 512


def _rms_fwd(x, g, name):
    t, d = x.shape

    def body(x_ref, g_ref, h_ref):
        xv = x_ref[...]
        r = lax.rsqrt(jnp.mean(xv * xv, axis=-1, keepdims=True) + EPS)
        h_ref[...] = (xv * r * g_ref[...]).astype(BF16)

    return pl.pallas_call(
        body, name=name, grid=(t // ROW_TILE,),
        in_specs=[pl.BlockSpec((ROW_TILE, d), lambda i: (i, 0)), pl.BlockSpec((1, d), lambda i: (0, 0))],
        out_specs=pl.BlockSpec((ROW_TILE, d), lambda i: (i, 0)),
        out_shape=jax.ShapeDtypeStruct((t, d), BF16),
        compiler_params=_params(("parallel",)),
    )(x, g)


def _rms_bwd(x, g, dh, dres, name):
    t, d = x.shape

    def body(x_ref, g_ref, dh_ref, dres_ref, dx_ref, dxb_ref, dg_ref):
        xv = x_ref[...]
        r = lax.rsqrt(jnp.mean(xv * xv, axis=-1, keepdims=True) + EPS)
        xh = xv * r
        dhv = dh_ref[...]
        gd = dhv * g_ref[...]
        dx = r * (gd - xh * jnp.mean(gd * xh, axis=-1, keepdims=True)) + dres_ref[...]
        dx_ref[...] = dx
        dxb_ref[...] = dx.astype(BF16)
        part = jnp.sum(dhv * xh, axis=0, keepdims=True)

        @pl.when(pl.program_id(0) == 0)
        def _():
            dg_ref[...] = part

        @pl.when(pl.program_id(0) > 0)
        def _():
            dg_ref[...] += part

    row = pl.BlockSpec((ROW_TILE, d), lambda i: (i, 0))
    vec = pl.BlockSpec((1, d), lambda i: (0, 0))
    return pl.pallas_call(
        body, name=name, grid=(t // ROW_TILE,),
        in_specs=[row, vec, row, row], out_specs=[row, row, vec],
        out_shape=[jax.ShapeDtypeStruct((t, d), F32), jax.ShapeDtypeStruct((t, d), BF16),
                   jax.ShapeDtypeStruct((1, d), F32)],
        compiler_params=_params(("arbitrary",)),
    )(x, g, dh, dres)


def _matmul_norm(a, b, res, g, *, a_spec, b_spec, dims, name):
    t, d = res.shape

    def body(a_ref, b_ref, r_ref, g_ref, x_ref, h_ref):
        xv = _contract(a_ref, b_ref, dims) + r_ref[...]
        x_ref[...] = xv
        h_ref[...] = (xv * lax.rsqrt(jnp.mean(xv * xv, axis=-1, keepdims=True) + EPS) * g_ref[...]).astype(BF16)

    row = pl.BlockSpec((TM_ACC, d), lambda i: (i, 0))
    return pl.pallas_call(
        body, name=name, grid=(t // TM_ACC,),
        in_specs=[a_spec, b_spec, row, pl.BlockSpec((1, d), lambda i: (0, 0))], out_specs=[row, row],
        out_shape=[jax.ShapeDtypeStruct((t, d), F32), jax.ShapeDtypeStruct((t, d), BF16)],
        compiler_params=_params(("parallel",)),
    )(a, b, res, g)


def _matmul_norm_bwd(a, b, x, g, dres, *, a_spec, b_spec, dims, name):
    t, d = x.shape

    def body(a_ref, b_ref, x_ref, g_ref, dres_ref, dx_ref, dxb_ref, dg_ref):
        dhv = _contract(a_ref, b_ref, dims)
        xv = x_ref[...]
        r = lax.rsqrt(jnp.mean(xv * xv, axis=-1, keepdims=True) + EPS)
        xh = xv * r
        gd = dhv * g_ref[...]
        dx = r * (gd - xh * jnp.mean(gd * xh, axis=-1, keepdims=True)) + dres_ref[...]
        dx_ref[...] = dx
        dxb_ref[...] = dx.astype(BF16)
        part = jnp.sum(dhv * xh, axis=0, keepdims=True)

        @pl.when(pl.program_id(0) == 0)
        def _():
            dg_ref[...] = part

        @pl.when(pl.program_id(0) > 0)
        def _():
            dg_ref[...] += part

    row = pl.BlockSpec((TM_ACC, d), lambda i: (i, 0))
    vec = pl.BlockSpec((1, d), lambda i: (0, 0))
    return pl.pallas_call(
        body, name=name, grid=(t // TM_ACC,),
        in_specs=[a_spec, b_spec, row, vec, row], out_specs=[row, row, vec],
        out_shape=[jax.ShapeDtypeStruct((t, d), F32), jax.ShapeDtypeStruct((t, d), BF16),
                   jax.ShapeDtypeStruct((1, d), F32)],
        compiler_params=_params(("arbitrary",)),
    )(a, b, x, g, dres)


def _loss_head(x, g, target, name):
    t, d = x.shape

    def body(x_ref, g_ref, t_ref, loss_ref, dx_ref, dxb_ref, dg_ref):
        xv = x_ref[...]
        r = lax.rsqrt(jnp.mean(xv * xv, axis=-1, keepdims=True) + EPS)
        xh = xv * r
        gv = g_ref[...]
        err = xh * gv - t_ref[...]
        loss = jnp.full((1, LANES), 0.5 / d, F32) * jnp.sum(err * err)
        dy = err * (1.0 / d)
        gd = dy * gv
        dx = r * (gd - xh * jnp.mean(gd * xh, axis=-1, keepdims=True))
        dx_ref[...] = dx
        dxb_ref[...] = dx.astype(BF16)
        part = jnp.sum(dy * xh, axis=0, keepdims=True)

        @pl.when(pl.program_id(0) == 0)
        def _():
            dg_ref[...] = part
            loss_ref[...] = loss

        @pl.when(pl.program_id(0) > 0)
        def _():
            dg_ref[...] += part
            loss_ref[...] += loss

    row = pl.BlockSpec((ROW_TILE, d), lambda i: (i, 0))
    vec = pl.BlockSpec((1, d), lambda i: (0, 0))
    return pl.pallas_call(
        body, name=name, grid=(t // ROW_TILE,),
        in_specs=[row, vec, row],
        out_specs=[pl.BlockSpec((1, LANES), lambda i: (0, 0)), row, row, vec],
        out_shape=[jax.ShapeDtypeStruct((1, LANES), F32), jax.ShapeDtypeStruct((t, d), F32),
                   jax.ShapeDtypeStruct((t, d), BF16), jax.ShapeDtypeStruct((1, d), F32)],
        compiler_params=_params(("arbitrary",)),
    )(x, g, target)


def _group_matrix(n):
    shift = int(math.log2(HEAD_DIM))
    r = lax.broadcasted_iota(jnp.int32, (n, n), 0) >> shift
    c = lax.broadcasted_iota(jnp.int32, (n, n), 1) >> shift
    return (r == c).astype(BF16)


def _group_sum(v, gmat):
    hi = v.astype(BF16)
    lo = (v - hi.astype(F32)).astype(BF16)

    def dot(p):
        return jnp.dot(p, gmat, preferred_element_type=F32)

    return dot(hi) + dot(lo)


def _shift_rows(ext, k):
    return pltpu.roll(ext, k % ext.shape[0], 0)


def _store_columns(stage, out_hbm, sems, row0, nrows, col_blocks):
    rows = pl.ds(pl.multiple_of(row0, SUBLANES * 2), nrows)
    copies = [
        pltpu.make_async_copy(stage.at[i], out_hbm.at[rows, pl.ds(pl.multiple_of(cb * LANES, LANES), LANES)], sems.at[i])
        for i, cb in enumerate(col_blocks)
    ]
    for cp in copies:
        cp.start()
    for cp in copies:
        cp.wait()


def _attn_consts(width):
    i = lax.broadcasted_iota(jnp.int32, (BAND, width), 0)
    j = lax.broadcasted_iota(jnp.int32, (BAND, width), 1)
    dist = (width - BAND) + i - j
    inwin = (dist >= 0) & (dist <= BAND)
    return dist.astype(F32), inwin, j


def _head_masks():
    lane = lax.broadcasted_iota(jnp.int32, (1, LANES), 1)
    return [(lane < HEAD_DIM).astype(F32), (lane >= HEAD_DIM).astype(F32)]


def _pair_bias(slope, dil):
    distf, inwin, _ = _attn_consts(2 * BAND)
    return jnp.concatenate([jnp.where(inwin, distf * (slope[hh] * (-float(dil))), NEG) for hh in range(2)], axis=0)


def _stack_heads(xv, hmask):
    return jnp.concatenate([xv * hmask[0], xv * hmask[1]], axis=0).astype(BF16)


FWD_UNROLL = 8
BWD_UNROLL = 8


def _unroll(trips, most):
    return max(u for u in range(1, most + 1) if trips % u == 0)


def _for_blocks(seq, dil, block, most):
    nb = seq // dil // BAND

    def residue(r, carry):
        base = r * nb
        block(pl.multiple_of(base * BAND, BAND), None)
        if nb > 1:
            def rest(n, c):
                block(pl.multiple_of((base + n) * BAND, BAND), pl.multiple_of((base + n - 1) * BAND, BAND))
                return c

            lax.fori_loop(1, nb, rest, 0, unroll=_unroll(nb - 1, most))
        return carry

    if dil == 1:
        residue(0, 0)
    else:
        lax.fori_loop(0, dil, residue, 0, unroll=_unroll(dil, max(1, most // nb)))


def _permute_in(src_ref, dst_ref, dil, seq):
    length = seq // dil
    for r in range(dil):
        dst_ref[pl.ds(r * length, length), :] = src_ref[pl.ds(r, length, stride=dil), :].astype(dst_ref.dtype)


def _slopes_table():
    slopes = 2.0 ** (-8.0 * jnp.arange(1, N_HEADS + 1, dtype=F32) / N_HEADS)
    return jnp.broadcast_to(slopes[:, None], (N_HEADS, 2 * BAND))


def _attn_fwd(proj, attn_g, nbatch, seq):
    t = nbatch * seq
    scale = HEAD_DIM ** -0.5

    def body(q_ref, k_ref, v_ref, g_ref, sl_ref, o_ref, lse_ref, cat_ref, pq, pk, pv, po, pm, pll, ao, am, al):
        hp = pl.program_id(1)
        hmask = _head_masks()
        slope = [sl_ref[pl.ds(2 * hp + hh, 1), :] for hh in range(2)]

        def run_branch(dil, qs, ks, vs, osink, msink, lsink):
            bias = _pair_bias(slope, dil)

            def block(row0, prow):
                cur = pl.ds(row0, BAND)
                q2 = _stack_heads(qs[cur, :] * scale, hmask)
                if prow is None:
                    kk, vv, bias_b = ks[cur, :], vs[cur, :], bias[:, BAND:]
                else:
                    prev = pl.ds(prow, BAND)
                    kk = jnp.concatenate([ks[prev, :], ks[cur, :]], axis=0)
                    vv = jnp.concatenate([vs[prev, :], vs[cur, :]], axis=0)
                    bias_b = bias
                s = lax.dot_general(q2, kk.astype(BF16), NT, preferred_element_type=F32) + bias_b
                m = jnp.max(s, axis=1, keepdims=True)
                p = jnp.exp(s - m)
                l = jnp.sum(p, axis=1, keepdims=True)
                pb = p.astype(BF16)
                o = jnp.dot(jnp.concatenate([pb[:BAND], pb[BAND:]], axis=1), _stack_heads(vv, hmask),
                            preferred_element_type=F32)
                osink[cur, :] = o
                msink[cur, :] = m[:BAND] * hmask[0] + m[BAND:] * hmask[1]
                lsink[cur, :] = l[:BAND] * hmask[0] + l[BAND:] * hmask[1]

            _for_blocks(seq, dil, block, FWD_UNROLL)

        run_branch(1, q_ref, k_ref, v_ref, ao, am, al)
        for dil in DILATIONS[1:]:
            length = seq // dil
            _permute_in(q_ref, pq, dil, seq)
            _permute_in(k_ref, pk, dil, seq)
            _permute_in(v_ref, pv, dil, seq)
            run_branch(dil, pq, pk, pv, po, pm, pll)
            for r in range(dil):
                nat = pl.ds(r, length, stride=dil)
                per = pl.ds(r * length, length)
                m0 = am[nat, :]
                mb = pm[per, :]
                mn = jnp.maximum(m0, mb)
                e0 = jnp.exp(m0 - mn)
                eb = jnp.exp(mb - mn)
                ao[nat, :] = ao[nat, :] * e0 + po[per, :] * eb
                al[nat, :] = al[nat, :] * e0 + pll[per, :] * eb
                am[nat, :] = mn

        gmat = _group_matrix(LANES)
        gv = g_ref[...]

        def fin(c, carry):
            rows = pl.ds(pl.multiple_of(c * 256, 256), 256)
            lv = al[rows, :]
            o = ao[rows, :] / lv
            o_ref[rows, :] = o
            lse_ref[rows, :] = am[rows, :] + jnp.log(lv)
            ms = _group_sum(o * o, gmat) * (1.0 / HEAD_DIM)
            cat_ref[rows, :] = (o * lax.rsqrt(ms + EPS) * gv).astype(BF16)
            return carry

        lax.fori_loop(0, seq // 256, fin, 0)

    nq = D_ATTN // LANES
    blk = lambda off: pl.BlockSpec((seq, LANES), lambda b, h: (b, h + off))
    scratch = [pltpu.VMEM((seq, LANES), F32) for _ in range(9)]
    return pl.pallas_call(
        body, name="attn_fwd", grid=(nbatch, nq),
        in_specs=[blk(0), blk(nq), blk(2 * nq), pl.BlockSpec((1, LANES), lambda b, h: (0, h)),
                  pl.BlockSpec((N_HEADS, 2 * BAND), lambda b, h: (0, 0))],
        out_specs=[blk(0), blk(0), blk(0)],
        out_shape=[jax.ShapeDtypeStruct((t, D_ATTN), F32), jax.ShapeDtypeStruct((t, D_ATTN), F32),
                   jax.ShapeDtypeStruct((t, D_MODEL), BF16)],
        scratch_shapes=scratch,
        compiler_params=_params(("parallel", "parallel")),
    )(proj, proj, proj, attn_g, _slopes_table())


def _attn_bwd(proj, o, lse, d_cat, attn_g, nbatch, seq):
    t = nbatch * seq
    scale = HEAD_DIM ** -0.5

    def body(q_ref, k_ref, v_ref, o_ref, lse_ref, dy_ref, g_ref, sl_ref, dproj_ref, dg_ref,
             do_n, dl_n, dq_n, dk_n, dv_n, pq, pk, pv, pdo, plse, pdl, pdq, pdk, pdv, stage, sems):
        hp = pl.program_id(0)
        hmask = _head_masks()
        slope = [sl_ref[pl.ds(2 * hp + hh, 1), :] for hh in range(2)]
        gmat = _group_matrix(LANES)
        gv = g_ref[...]

        def prep(c, dg):
            rows = pl.ds(pl.multiple_of(c * 256, 256), 256)
            ov = o_ref[rows, :]
            dyn = dy_ref[rows, :].astype(F32)
            r = lax.rsqrt(_group_sum(ov * ov, gmat) * (1.0 / HEAD_DIM) + EPS)
            gd = dyn * gv
            oh = ov * r
            do = r * (gd - oh * (_group_sum(gd * oh, gmat) * (1.0 / HEAD_DIM)))
            do_n[rows, :] = do
            dl_n[rows, :] = _group_sum(do * ov, gmat)
            return dg + jnp.sum(dyn * oh, axis=0, keepdims=True)

        dg = lax.fori_loop(0, seq // 256, prep, jnp.zeros((1, LANES), F32))

        @pl.when(pl.program_id(1) == 0)
        def _():
            dg_ref[...] = dg

        @pl.when(pl.program_id(1) > 0)
        def _():
            dg_ref[...] += dg

        def clear(*refs):
            def step(c, carry):
                rows = pl.ds(pl.multiple_of(c * 256, 256), 256)
                for ref in refs:
                    ref[rows, :] = jnp.zeros((256, LANES), F32)
                return carry

            lax.fori_loop(0, seq // 256, step, 0)

        clear(dq_n, dk_n, dv_n)

        def run_branch(dil, qs, ks, vs, dos, lses, dls, dqs, dks, dvs):
            bias = _pair_bias(slope, dil)

            def per_head(xv):
                return jnp.concatenate([xv[:, 0:1], xv[:, HEAD_DIM:HEAD_DIM + 1]], axis=0)

            def block(row0, prow):
                cur = pl.ds(row0, BAND)
                keys = cur if prow is None else pl.ds(prow, 2 * BAND)
                q2 = _stack_heads(qs[cur, :] * scale, hmask)
                do2 = _stack_heads(dos[cur, :], hmask)
                kk, vv = ks[keys, :], vs[keys, :]
                s = lax.dot_general(q2, kk.astype(BF16), NT, preferred_element_type=F32)
                s = s + (bias[:, BAND:] if prow is None else bias)
                p = jnp.exp(s - per_head(lses[cur, :]))
                dp = lax.dot_general(do2, vv.astype(BF16), NT, preferred_element_type=F32)
                ds = (p * (dp - per_head(dls[cur, :]))).astype(BF16)
                dqs[cur, :] += jnp.dot(jnp.concatenate([ds[:BAND], ds[BAND:]], axis=1), _stack_heads(kk, hmask),
                                       preferred_element_type=F32)
                dks[keys, :] += lax.dot_general(ds, q2, TN, preferred_element_type=F32)
                dvs[keys, :] += lax.dot_general(p.astype(BF16), do2, TN, preferred_element_type=F32)

            _for_blocks(seq, dil, block, BWD_UNROLL)

        run_branch(1, q_ref, k_ref, v_ref, do_n, lse_ref, dl_n, dq_n, dk_n, dv_n)
        for dil in DILATIONS[1:]:
            length = seq // dil
            for src, dst in ((q_ref, pq), (k_ref, pk), (v_ref, pv), (do_n, pdo), (lse_ref, plse), (dl_n, pdl)):
                _permute_in(src, dst, dil, seq)
            clear(pdq, pdk, pdv)
            run_branch(dil, pq, pk, pv, pdo, plse, pdl, pdq, pdk, pdv)
            for r in range(dil):
                nat = pl.ds(r, length, stride=dil)
                per = pl.ds(r * length, length)
                dq_n[nat, :] += pdq[per, :]
                dk_n[nat, :] += pdk[per, :]
                dv_n[nat, :] += pdv[per, :]

        def emit(c, carry):
            rows = pl.ds(pl.multiple_of(c * 256, 256), 256)
            stage[0, rows, :] = (dq_n[rows, :] * scale).astype(BF16)
            stage[1, rows, :] = dk_n[rows, :].astype(BF16)
            stage[2, rows, :] = dv_n[rows, :].astype(BF16)
            return carry

        lax.fori_loop(0, seq // 256, emit, 0)
        _store_columns(stage, dproj_ref, sems, pl.program_id(1) * seq, seq, [hp, nq + hp, 2 * nq + hp])

    nq = D_ATTN // LANES
    blk = lambda off: pl.BlockSpec((seq, LANES), lambda h, b: (b, h + off))
    vec = pl.BlockSpec((1, LANES), lambda h, b: (0, h))
    scratch = [pltpu.VMEM((seq, LANES), F32) for _ in range(14)]
    scratch += [pltpu.VMEM((3, seq, LANES), BF16), pltpu.SemaphoreType.DMA((3,))]
    d_proj, dg = pl.pallas_call(
        body, name="attn_bwd", grid=(nq, nbatch),
        in_specs=[blk(0), blk(nq), blk(2 * nq), blk(0), blk(0), blk(0), vec,
                  pl.BlockSpec((N_HEADS, 2 * BAND), lambda h, b: (0, 0))],
        out_specs=[pl.BlockSpec(memory_space=pl.ANY), vec],
        out_shape=[jax.ShapeDtypeStruct((t, D_IN), BF16), jax.ShapeDtypeStruct((1, D_ATTN), F32)],
        scratch_shapes=scratch,
        compiler_params=_params(("arbitrary", "arbitrary")),
    )(proj, proj, proj, o, lse, d_cat, attn_g, _slopes_table())
    return d_proj, dg


HALO = SUBLANES
PACKED_ROWS = 2 * SUBLANES


def _window(ref, c, rows, nchunks, after):
    row0 = pl.multiple_of(c * rows, rows)
    prev0 = pl.multiple_of(jnp.maximum(row0 - PACKED_ROWS, 0), PACKED_ROWS)
    before = ref[pl.ds(prev0, PACKED_ROWS), :].astype(F32)[PACKED_ROWS - HALO:] * (c > 0).astype(F32)
    parts = [before, ref[pl.ds(row0, rows), :].astype(F32)]
    if after:
        next0 = pl.multiple_of(jnp.minimum(row0 + rows, (nchunks - 1) * rows), PACKED_ROWS)
        parts.append(ref[pl.ds(next0, PACKED_ROWS), :].astype(F32)[:HALO] * (c < nchunks - 1).astype(F32))
    return jnp.concatenate(parts, axis=0)


def _behind(z):
    z1 = _shift_rows(z, 1)
    return z1, _shift_rows(z1, 1)


def _ahead(dy):
    d1 = _shift_rows(dy, -1)
    return d1, _shift_rows(d1, -1)


def _conv(z, w):
    z1, z2 = _behind(z)
    return w[0:1] * z2 + w[1:2] * z1 + w[2:3] * z


def _conv_bwd(dy, z, w, cur):
    d1, d2 = _ahead(dy)
    dz = w[2:3] * dy + w[1:2] * d1 + w[0:1] * d2
    return dz, [jnp.sum((d * z)[cur], axis=0, keepdims=True) for d in (d2, d1, dy)]


def _sigmoid(a):
    return 0.5 * jnp.tanh(0.5 * a) + 0.5


MIX_ROWS = 256
GATE_B_BLOCK = 3 * D_ATTN // LANES
GATE_C_BLOCK = GATE_B_BLOCK + D_CONV // LANES
U_BLOCK = GATE_C_BLOCK + D_CONV // LANES


def _convmix_fwd(proj, cat, mcw, conv_g, nbatch, seq):
    nchunks = seq // MIX_ROWS

    def body(gb_ref, gc_ref, u_ref, w_ref, g_ref, cat_in, cat_ref):
        del cat_in
        gmat = _group_matrix(LANES)
        w = w_ref[...]
        gv = g_ref[...]

        def step(c, carry):
            cur = pl.ds(pl.multiple_of(c * MIX_ROWS, MIX_ROWS), MIX_ROWS)
            z = _window(gc_ref, c, MIX_ROWS, nchunks, False) * _window(u_ref, c, MIX_ROWS, nchunks, False)
            y = gb_ref[cur, :] * _conv(z, w)[HALO:]
            ms = _group_sum(y * y, gmat) * (1.0 / HEAD_DIM)
            cat_ref[cur, :] = (y * lax.rsqrt(ms + EPS) * gv).astype(BF16)
            return carry

        lax.fori_loop(0, nchunks, step, 0)

    nc = D_CONV // LANES
    blk = lambda off: pl.BlockSpec((seq, LANES), lambda b, j: (b, j + off))
    return pl.pallas_call(
        body, name="convmix_fwd", grid=(nbatch, nc),
        in_specs=[blk(GATE_B_BLOCK), blk(GATE_C_BLOCK), blk(U_BLOCK),
                  pl.BlockSpec((3, LANES), lambda b, j: (0, j)), pl.BlockSpec((1, LANES), lambda b, j: (0, j)),
                  pl.BlockSpec(memory_space=pl.ANY)],
        out_specs=blk(D_ATTN // LANES),
        out_shape=jax.ShapeDtypeStruct(cat.shape, cat.dtype),
        input_output_aliases={5: 0},
        compiler_params=_params(("parallel", "parallel")),
    )(proj, proj, proj, mcw, conv_g, cat)


def _convmix_bwd(proj, d_cat, d_proj, mcw, conv_g, nbatch, seq):
    nchunks = seq // MIX_ROWS

    def body(gb_ref, gc_ref, u_ref, dy_ref, w_ref, g_ref, dproj_in, dproj_ref, dw_ref, dg_ref, stage, sems):
        del dproj_in
        cb = pl.program_id(0)
        b = pl.program_id(1)
        gmat = _group_matrix(LANES)
        w = w_ref[...]
        gv = g_ref[...]
        cur = slice(HALO, HALO + MIX_ROWS)

        def step(c, carry):
            rows = pl.ds(pl.multiple_of(c * MIX_ROWS, MIX_ROWS), MIX_ROWS)
            gb = _window(gb_ref, c, MIX_ROWS, nchunks, True)
            gc = _window(gc_ref, c, MIX_ROWS, nchunks, True)
            u = _window(u_ref, c, MIX_ROWS, nchunks, True)
            dyn = _window(dy_ref, c, MIX_ROWS, nchunks, True)
            z = gc * u
            conv = _conv(z, w)
            y = gb * conv
            r = lax.rsqrt(_group_sum(y * y, gmat) * (1.0 / HEAD_DIM) + EPS)
            yh = y * r
            gd = dyn * gv
            dy = r * (gd - yh * (_group_sum(gd * yh, gmat) * (1.0 / HEAD_DIM)))
            dz, dws = _conv_bwd(dy * gb, z, w, cur)
            stage[0, rows, :] = (dy * conv)[cur].astype(BF16)
            stage[1, rows, :] = (dz * u)[cur].astype(BF16)
            stage[2, rows, :] = (dz * gc)[cur].astype(BF16)
            dg = jnp.sum((dyn * yh)[cur], axis=0, keepdims=True)
            return tuple(a + d for a, d in zip(carry, dws + [dg]))

        zero = jnp.zeros((1, LANES), F32)
        dw0, dw1, dw2, dg = lax.fori_loop(0, nchunks, step, (zero, zero, zero, zero))

        @pl.when(b == 0)
        def _():
            dw_ref[0:1, :] = dw0
            dw_ref[1:2, :] = dw1
            dw_ref[2:3, :] = dw2
            dg_ref[...] = dg

        @pl.when(b > 0)
        def _():
            dw_ref[0:1, :] += dw0
            dw_ref[1:2, :] += dw1
            dw_ref[2:3, :] += dw2
            dg_ref[...] += dg

        _store_columns(stage, dproj_ref, sems, b * seq, seq, [GATE_B_BLOCK + cb, GATE_C_BLOCK + cb, U_BLOCK + cb])

    nc = D_CONV // LANES
    blk = lambda off: pl.BlockSpec((seq, LANES), lambda j, b: (b, j + off))
    return pl.pallas_call(
        body, name="convmix_bwd", grid=(nc, nbatch),
        in_specs=[blk(GATE_B_BLOCK), blk(GATE_C_BLOCK), blk(U_BLOCK), blk(D_ATTN // LANES),
                  pl.BlockSpec((3, LANES), lambda j, b: (0, j)), pl.BlockSpec((1, LANES), lambda j, b: (0, j)),
                  pl.BlockSpec(memory_space=pl.ANY)],
        out_specs=[pl.BlockSpec(memory_space=pl.ANY), pl.BlockSpec((3, LANES), lambda j, b: (0, j)),
                   pl.BlockSpec((1, LANES), lambda j, b: (0, j))],
        out_shape=[jax.ShapeDtypeStruct(d_proj.shape, d_proj.dtype), jax.ShapeDtypeStruct((3, D_CONV), F32),
                   jax.ShapeDtypeStruct((1, D_CONV), F32)],
        scratch_shapes=[pltpu.VMEM((3, seq, LANES), BF16), pltpu.SemaphoreType.DMA((3,))],
        input_output_aliases={6: 0},
        compiler_params=_params(("arbitrary", "arbitrary")),
    )(proj, proj, proj, d_cat, mcw, conv_g, d_proj)


FFN_ROWS = 256


def _ffn_act_fwd(pre, fcw, nbatch, seq):
    t = nbatch * seq
    nchunks = seq // FFN_ROWS

    def body(pre_ref, w_ref, act_ref):
        wa = w_ref[0]
        wc = w_ref[1]

        def step(c, carry):
            cur = pl.ds(pl.multiple_of(c * FFN_ROWS, FFN_ROWS), FFN_ROWS)
            a = _conv(_window(pre_ref.at[0], c, FFN_ROWS, nchunks, False), wa)[HALO:]
            v = _conv(_window(pre_ref.at[1], c, FFN_ROWS, nchunks, False), wc)[HALO:]
            act_ref[cur, :] = (a * _sigmoid(a) * v).astype(BF16)
            return carry

        lax.fori_loop(0, nchunks, step, 0)

    return pl.pallas_call(
        body, name="ffn_act_fwd", grid=(N_UP_PAIRS, nbatch),
        in_specs=[pl.BlockSpec((2, None, seq, UP_CHUNK), lambda i, b: (0, i, b, 0)),
                  pl.BlockSpec((2, None, 3, UP_CHUNK), lambda i, b: (0, i, 0, 0))],
        out_specs=pl.BlockSpec((None, seq, UP_CHUNK), lambda i, b: (i, b, 0)),
        out_shape=jax.ShapeDtypeStruct((N_UP_PAIRS, t, UP_CHUNK), BF16),
        compiler_params=_params(("parallel", "parallel")),
    )(pre, fcw)


def _ffn_down(pre, fcw, wdown, res, g, seq, name):
    t, d = res.shape
    tiles_per_seq = seq // FFN_ROWS

    def body(main_ref, halo_ref, w_ref, wd_ref, r_ref, *rest):
        if g is None:
            x_ref, act_ref = rest
        else:
            g_ref, x_ref, h_ref, act_ref = rest
        inside = ((pl.program_id(0) % tiles_per_seq) > 0).astype(F32)

        def window(part, p):
            before = halo_ref[part, p].astype(F32)[PACKED_ROWS - HALO:] * inside
            return jnp.concatenate([before, main_ref[part, p].astype(F32)], axis=0)

        total = r_ref[...]
        for p in range(N_UP_PAIRS):
            a = _conv(window(0, p), w_ref[0, p])[HALO:]
            v = _conv(window(1, p), w_ref[1, p])[HALO:]
            act = (a * _sigmoid(a) * v).astype(BF16)
            act_ref[p] = act
            total = total + jnp.dot(act, wd_ref[p], preferred_element_type=F32)
        x_ref[...] = total
        if g is not None:
            h_ref[...] = (total * lax.rsqrt(jnp.mean(total * total, axis=-1, keepdims=True) + EPS) * g_ref[...]).astype(BF16)

    row = pl.BlockSpec((FFN_ROWS, d), lambda i: (i, 0))
    tiles_per_halo = FFN_ROWS // PACKED_ROWS
    in_specs = [
        pl.BlockSpec((2, N_UP_PAIRS, FFN_ROWS, UP_CHUNK), lambda i: (0, 0, i, 0)),
        pl.BlockSpec((2, N_UP_PAIRS, PACKED_ROWS, UP_CHUNK), lambda i: (0, 0, jnp.maximum(i * tiles_per_halo - 1, 0), 0)),
        pl.BlockSpec((2, N_UP_PAIRS, 3, UP_CHUNK), lambda i: (0, 0, 0, 0)),
        pl.BlockSpec((N_UP_PAIRS, UP_CHUNK, d), lambda i: (0, 0, 0)), row]
    out_specs = [row]
    out_shape = [jax.ShapeDtypeStruct((t, d), F32)]
    args = [pre, pre, fcw, wdown, res]
    if g is not None:
        in_specs.append(pl.BlockSpec((1, d), lambda i: (0, 0)))
        out_specs.append(row)
        out_shape.append(jax.ShapeDtypeStruct((t, d), BF16))
        args.append(g)
    out_specs.append(pl.BlockSpec((N_UP_PAIRS, FFN_ROWS, UP_CHUNK), lambda i: (0, i, 0)))
    out_shape.append(jax.ShapeDtypeStruct((N_UP_PAIRS, t, UP_CHUNK), BF16))
    return pl.pallas_call(
        body, name=name, grid=(t // FFN_ROWS,), in_specs=in_specs, out_specs=out_specs, out_shape=out_shape,
        compiler_params=_params(("parallel",)),
    )(*args)


def _ffn_act_bwd(pre, d_act, fcw, nbatch, seq):
    nchunks = seq // FFN_ROWS

    def body(pre_ref, da_ref, w_ref, dpre_ref, dw_ref):
        b = pl.program_id(1)
        wa = w_ref[0]
        wc = w_ref[1]
        cur = slice(HALO, HALO + FFN_ROWS)

        def step(c, carry):
            rows = pl.ds(pl.multiple_of(c * FFN_ROWS, FFN_ROWS), FFN_ROWS)
            pg = _window(pre_ref.at[0], c, FFN_ROWS, nchunks, True)
            pv = _window(pre_ref.at[1], c, FFN_ROWS, nchunks, True)
            dact = _window(da_ref, c, FFN_ROWS, nchunks, True)
            a = _conv(pg, wa)
            v = _conv(pv, wc)
            sg = _sigmoid(a)
            asg = a * sg
            dzg, dwg = _conv_bwd(dact * v * (sg + asg - asg * sg), pg, wa, cur)
            dzv, dwv = _conv_bwd(dact * asg, pv, wc, cur)
            dpre_ref[0, rows, :] = dzg[cur].astype(BF16)
            dpre_ref[1, rows, :] = dzv[cur].astype(BF16)
            return tuple(acc + d for acc, d in zip(carry, dwg + dwv))

        zero = jnp.zeros((1, UP_CHUNK), F32)
        sums = lax.fori_loop(0, nchunks, step, (zero,) * 6)

        @pl.when(b == 0)
        def _():
            for i in range(6):
                dw_ref[i // 3, pl.ds(i % 3, 1), :] = sums[i]

        @pl.when(b > 0)
        def _():
            for i in range(6):
                dw_ref[i // 3, pl.ds(i % 3, 1), :] += sums[i]

    pair = pl.BlockSpec((2, None, seq, UP_CHUNK), lambda i, b: (0, i, b, 0))
    wspec = pl.BlockSpec((2, None, 3, UP_CHUNK), lambda i, b: (0, i, 0, 0))
    return pl.pallas_call(
        body, name="ffn_act_bwd", grid=(N_UP_PAIRS, nbatch),
        in_specs=[pair, pl.BlockSpec((None, seq, UP_CHUNK), lambda i, b: (i, b, 0)), wspec],
        out_specs=[pair, wspec],
        out_shape=[jax.ShapeDtypeStruct(pre.shape, BF16), jax.ShapeDtypeStruct(fcw.shape, F32)],
        compiler_params=_params(("parallel", "arbitrary")),
    )(pre, d_act, fcw)


def _adamw(lands, w, m, v, row_tile, name, after=()):
    nl = len(lands)
    _, nr, ncol = lands[0].shape
    c1 = 1.0 - ADAM_B1 ** ADAM_STEP
    c2 = 1.0 - ADAM_B2 ** ADAM_STEP

    def body(*refs):
        land_refs = refs[:nl]
        w_ref, m_ref, v_ref = refs[nl:nl + 3]
        g_ref, d_ref, mo_ref, vo_ref = refs[nl + 3 + len(after):]
        for l in range(nl):
            @pl.when(pl.program_id(0) == l)
            def _(l=l):
                g = land_refs[l][0].astype(F32)
                for j in range(1, N_DEV):
                    g = g + land_refs[l][j].astype(F32)
                g_ref[...] = g

        g = g_ref[...]
        m2 = ADAM_B1 * m_ref[...] + (1.0 - ADAM_B1) * g
        v2 = ADAM_B2 * v_ref[...] + (1.0 - ADAM_B2) * (g * g)
        mo_ref[...] = m2
        vo_ref[...] = v2
        d_ref[...] = -ADAM_LR * ((m2 / c1) / (jnp.sqrt(v2 / c2) + ADAM_EPS) + ADAM_WD * w_ref[...])

    def land_spec(l):
        return pl.BlockSpec((N_DEV, row_tile, ncol), lambda k, i: (0, jnp.where(k == l, i, 0), 0))

    tile = pl.BlockSpec((None, row_tile, ncol), lambda k, i: (k, i, 0))
    return pl.pallas_call(
        body, name=name, grid=(nl, nr // row_tile),
        in_specs=[land_spec(l) for l in range(nl)] + [tile, tile, tile] + [pl.BlockSpec(memory_space=pl.ANY)] * len(after),
        out_specs=[tile] * 4,
        out_shape=[jax.ShapeDtypeStruct(w.shape, F32)] * 4,
        compiler_params=_params(("arbitrary", "arbitrary")),
    )(*lands, w, m, v, *after)


class _Item:
    def __init__(self, src, chunked, land_cols=False):
        self.src, self.chunked, self.land_cols = src, chunked, land_cols
        if chunked == "cols":
            block = (src.shape[0], src.shape[1] // N_DEV)
        else:
            block = src.shape[1:] if chunked else src.shape
        self.width = block[-1]
        self.land_shape = (block[0], N_DEV * block[1]) if land_cols else (N_DEV,) + block

    def _cols(self, first, count=1):
        return pl.ds(pl.multiple_of(first * self.width, LANES), count * self.width)

    def part(self, src_ref, j):
        if self.chunked == "cols":
            return src_ref.at[:, self._cols(j)]
        return src_ref.at[j] if self.chunked else src_ref

    def slot(self, land_ref, s):
        return land_ref.at[:, self._cols(s)] if self.land_cols else land_ref.at[s]

    def seven(self, land_ref):
        return land_ref.at[:, self._cols(0, N_DEV - 1)] if self.land_cols else land_ref.at[pl.ds(0, N_DEV - 1)]


def _mesh_place():
    x, y, c = lax.axis_index("x"), lax.axis_index("y"), lax.axis_index("c")
    return x, y, c, 4 * x + 2 * y + c


def _flipped(x, y, c, k):
    px = 1 - x if k & 4 else x
    py = 1 - y if k & 2 else y
    pc = 1 - c if k & 1 else c
    return (px, py, pc), 4 * px + 2 * py + pc


PEER_ORDER = (2, 4, 6, 3, 5, 7, 1)


def _exchange(items, name):
    n = len(items)

    def body(*refs):
        srcs, lands = refs[:n], refs[n:2 * n]
        send, recv, local = refs[2 * n:]
        x, y, c, me = _mesh_place()

        def copy(i, k, chunk, slot, dev):
            return pltpu.make_async_remote_copy(
                src_ref=items[i].part(srcs[i], chunk), dst_ref=items[i].slot(lands[i], slot),
                send_sem=send.at[i, k - 1], recv_sem=recv.at[i, k - 1], device_id=dev, device_id_type=MESH)

        own = [pltpu.make_async_copy(items[i].part(srcs[i], me), items[i].slot(lands[i], me), local.at[i])
               for i in range(n)]
        for k in PEER_ORDER:
            dev, idx = _flipped(x, y, c, k)
            for i in range(n):
                copy(i, k, idx, me, dev).start()
        for cp in own:
            cp.start()
        for k in PEER_ORDER:
            dev, idx = _flipped(x, y, c, k)
            for i in range(n):
                copy(i, k, me, idx, dev).wait_recv()
        for k in PEER_ORDER:
            dev, idx = _flipped(x, y, c, k)
            for i in range(n):
                copy(i, k, idx, me, dev).wait_send()
        for cp in own:
            cp.wait()

    hbm = pl.BlockSpec(memory_space=pl.ANY)
    return pl.pallas_call(
        body, name=name,
        in_specs=[hbm] * n, out_specs=[hbm] * n,
        out_shape=[jax.ShapeDtypeStruct(it.land_shape, it.src.dtype) for it in items],
        scratch_shapes=[pltpu.SemaphoreType.DMA((n, N_DEV - 1)), pltpu.SemaphoreType.DMA((n, N_DEV - 1)),
                        pltpu.SemaphoreType.DMA((n,))],
        compiler_params=pltpu.CompilerParams(has_side_effects=True),
    )(*[it.src for it in items])


def _sequencer_exchange(items, name, collective_id):
    n = len(items)

    def body(*refs):
        srcs, lands = refs[:n], refs[n:2 * n]
        send, recv, local = refs[2 * n:]
        x, y, c, me = _mesh_place()
        barrier = pltpu.get_barrier_semaphore()
        for k in PEER_ORDER:
            pl.semaphore_signal(barrier, inc=1, device_id=_flipped(x, y, c, k)[0], device_id_type=MESH)
        pl.semaphore_wait(barrier, N_DEV - 1)

        def copy(i, k, chunk, slot, dev):
            return pltpu.make_async_remote_copy(
                src_ref=items[i].part(srcs[i], chunk), dst_ref=items[i].slot(lands[i], slot),
                send_sem=send.at[i, k - 1], recv_sem=recv.at[i, k - 1], device_id=dev, device_id_type=MESH)

        own = [pltpu.make_async_copy(items[i].part(srcs[i], me), items[i].slot(lands[i], me), local.at[i])
               for i in range(n)]
        for cp in own:
            cp.start()
        for k in PEER_ORDER:
            dev, idx = _flipped(x, y, c, k)
            for i in range(n):
                copy(i, k, idx, me, dev).start()
        for k in PEER_ORDER:
            dev, idx = _flipped(x, y, c, k)
            for i in range(n):
                copy(i, k, me, idx, dev).wait_recv()
        for k in PEER_ORDER:
            dev, idx = _flipped(x, y, c, k)
            for i in range(n):
                copy(i, k, idx, me, dev).wait_send()
        for cp in own:
            cp.wait()

    return pl.kernel(
        body, name=name,
        out_type=[jax.ShapeDtypeStruct(it.land_shape, it.src.dtype) for it in items],
        mesh=plsc.ScalarSubcoreMesh(axis_name="sequencer", num_cores=1),
        scratch_types=[pltpu.SemaphoreType.DMA((n, N_DEV - 1)), pltpu.SemaphoreType.DMA((n, N_DEV - 1)),
                       pltpu.SemaphoreType.DMA((n,))],
        compiler_params=pltpu.CompilerParams(collective_id=collective_id),
    )(*[it.src for it in items])


HBM_SPEC = pl.BlockSpec(memory_space=pltpu.HBM)
SEM_SPEC = pl.BlockSpec(memory_space=pltpu.SEMAPHORE)
DATAFLOW = pltpu.SideEffectType.DATAFLOW_SIDE_EFFECTING


def _exchange_start(items, name, after=()):
    n = len(items)
    na = len(after)

    def body(*refs):
        srcs, land_ins = refs[:n], refs[n:2 * n]
        outs = refs[2 * n + na:6 * n + na]
        (local,) = refs[6 * n + na:]
        del land_ins
        x, y, c, me = _mesh_place()
        own = [pltpu.make_async_copy(items[i].part(srcs[i], me), items[i].slot(outs[4 * i + 3], me), local.at[i])
               for i in range(n)]
        for cp in own:
            cp.start()
        for cp in own:
            cp.wait()
        for k in PEER_ORDER:
            dev, idx = _flipped(x, y, c, k)
            for i in range(n):
                send, recv, _, land = outs[4 * i:4 * i + 4]
                pltpu.make_async_remote_copy(
                    src_ref=items[i].part(srcs[i], idx), dst_ref=items[i].slot(land, me), send_sem=send, recv_sem=recv,
                    device_id=dev, device_id_type=MESH).start()

    out_shape, out_specs, args, lands = [], [], [], []
    for it in items:
        out_shape += [pltpu.SemaphoreType.DMA(()), pltpu.SemaphoreType.DMA(()),
                      pltpu.HBM(it.src.shape, it.src.dtype), pltpu.HBM(it.land_shape, it.src.dtype)]
        out_specs += [SEM_SPEC, SEM_SPEC, HBM_SPEC, HBM_SPEC]
        args.append(pltpu.with_memory_space_constraint(it.src, pltpu.HBM))
        lands.append(pltpu.with_memory_space_constraint(lax.empty(it.land_shape, it.src.dtype), pltpu.HBM))
    outs = pl.pallas_call(
        body, name=name,
        in_specs=[HBM_SPEC] * (2 * n) + [pl.BlockSpec(memory_space=pl.ANY)] * na,
        out_specs=out_specs, out_shape=out_shape,
        scratch_shapes=[pltpu.SemaphoreType.DMA((n,))],
        input_output_aliases={**{i: 4 * i + 2 for i in range(n)}, **{n + i: 4 * i + 3 for i in range(n)}},
        compiler_params=pltpu.CompilerParams(has_side_effects=DATAFLOW),
    )(*args, *lands, *after)
    return [tuple(outs[4 * i:4 * i + 4]) + (items[i],) for i in range(n)]


def _started(handles):
    return handles[0][2]


def _exchange_wait(handles, after, name):
    n = len(handles)

    def body(*refs):
        x, y, c, _ = _mesh_place()
        for i in range(n):
            src, land, send, recv = refs[4 * i:4 * i + 4]
            del src
            seven = handles[i][4].seven(land)
            cp = pltpu.make_async_remote_copy(src_ref=seven, dst_ref=seven, send_sem=send, recv_sem=recv,
                                              device_id=(x, y, 1 - c), device_id_type=MESH)
            cp.wait_send()
            cp.wait_recv()

    args, in_specs, out_shape = [], [], []
    for send, recv, src, land, _ in handles:
        args += [src, land, send, recv]
        in_specs += [HBM_SPEC, HBM_SPEC, SEM_SPEC, SEM_SPEC]
        out_shape += [pltpu.HBM(src.shape, src.dtype), pltpu.HBM(land.shape, land.dtype)]
    outs = pl.pallas_call(
        body, name=name,
        in_specs=in_specs + [pl.BlockSpec(memory_space=pl.ANY)] * len(after), out_specs=[HBM_SPEC] * (2 * n),
        out_shape=out_shape,
        input_output_aliases={**{4 * i: 2 * i for i in range(n)}, **{4 * i + 1: 2 * i + 1 for i in range(n)}},
        compiler_params=pltpu.CompilerParams(has_side_effects=DATAFLOW),
    )(*args, *after)
    return [outs[2 * i + 1] for i in range(n)]


TM = 1024
TM_ACC = 512
TN_IN = 768


def kernel(x, norm1_g, w_in, mix_conv_w, attn_out_g, conv_out_g, w_out, norm2_g, ffn_up, ffn_conv_w, ffn_down, final_norm_g, loss_target, m_norm1_g, m_w_in, m_mix_conv_w, m_attn_out_g, m_conv_out_g, m_w_out, m_norm2_g, m_ffn_up, m_ffn_conv_w, m_ffn_down, m_final_norm_g, v_norm1_g, v_w_in, v_mix_conv_w, v_attn_out_g, v_conv_out_g, v_w_out, v_norm2_g, v_ffn_up, v_ffn_conv_w, v_ffn_down, v_final_norm_g):
    nbatch, seq, d = x.shape
    t = nbatch * seq
    nt, nta = t // TM, t // TM_ACC
    out_rows = D_MODEL // N_DEV
    down_rows = D_FF // N_DEV
    xf = x.reshape(t, d)
    target = loss_target.reshape(t, d)

    cw_local = jnp.concatenate([ffn_conv_w, mix_conv_w], axis=-1)
    cast = lambda w: _Item(w.astype(BF16), False)
    cast_in = lambda w: _Item(w.astype(BF16), False, land_cols=True)
    cw_all, win0 = _sequencer_exchange([_Item(cw_local, False), cast_in(w_in[0])], "gather_a", 0)
    up_t, m_up_t, v_up_t = (jnp.swapaxes(a, 1, 2) for a in (ffn_up, m_ffn_up, v_ffn_up))
    wout0, wup0 = _sequencer_exchange([cast(w_out[0]), cast(up_t[0])], "gather_b", 1)
    wdown0, win1, wout1 = _sequencer_exchange([cast(ffn_down[0]), cast_in(w_in[1]), cast(w_out[1])], "gather_c", 2)
    wup1, wdown1 = _sequencer_exchange([cast(up_t[1]), cast(ffn_down[1])], "gather_d", 3)
    win, wup = [win0, win1], [wup0, wup1]
    wout = [w.reshape(D_MODEL, D_MODEL) for w in (wout0, wout1)]
    wdown = [w.reshape(N_UP_PAIRS, UP_CHUNK, D_MODEL) for w in (wdown0, wdown1)]
    fcw = [cw_all[:, k, :, :UP_CHUNK].reshape(2, N_UP_PAIRS, 3, UP_CHUNK) for k in range(DEPTH)]
    mcw = [cw_all[:, k, :, UP_CHUNK:].transpose(1, 0, 2).reshape(3, D_CONV) for k in range(DEPTH)]

    full = lambda i, j, k: (0, 0)

    saved = []
    xin = xf
    h1 = _rms_fwd(xin, norm1_g[0][None], "rms1_fwd_0")
    rows_of = lambda width: pl.BlockSpec((TM_ACC, width), lambda i: (i, 0))
    whole = lambda *shape: pl.BlockSpec(shape, lambda i: (0,) * len(shape))
    chunks_of = lambda n: pl.BlockSpec((n, TM_ACC, UP_CHUNK), lambda i: (0, i, 0))
    for l in range(DEPTH):
        proj = _matmul(
            h1, win[l], grid=(nt, D_IN // TN_IN, 1), dims=NN, name=f"proj_{l}",
            a_spec=pl.BlockSpec((TM, D_MODEL), lambda i, j, k: (i, 0)),
            b_spec=pl.BlockSpec((D_MODEL, TN_IN), lambda i, j, k: (0, j)),
            o_spec=pl.BlockSpec((TM, TN_IN), lambda i, j, k: (i, j)), o_shape=(t, D_IN), o_dtype=F32)
        o, lse, cat = _attn_fwd(proj, attn_out_g[l][None], nbatch, seq)
        cat = _convmix_fwd(proj, cat, mcw[l], conv_out_g[l][None], nbatch, seq)
        xmid, h2 = _matmul_norm(cat, wout[l], xin, norm2_g[l][None], dims=NN, name=f"mix_out_{l}",
                                a_spec=rows_of(D_MODEL), b_spec=whole(D_MODEL, D_MODEL))
        pre = _matmul(
            h2, wup[l], grid=(nt, N_DEV, 1), dims=NT, name=f"ffn_up_{l}",
            a_spec=pl.BlockSpec((TM, D_MODEL), lambda i, j, k: (i, 0)),
            b_spec=pl.BlockSpec((None, UP_CHUNK, D_MODEL), lambda i, j, k: (j, 0, 0)),
            o_spec=pl.BlockSpec((None, TM, UP_CHUNK), lambda i, j, k: (j, i, 0)),
            o_shape=(N_DEV, t, UP_CHUNK), o_dtype=BF16).reshape(2, N_UP_PAIRS, t, UP_CHUNK)
        if l + 1 < DEPTH:
            xout, h_next, act = _ffn_down(pre, fcw[l], wdown[l], xmid, norm1_g[l + 1][None], seq, f"ffn_down_{l}")
        else:
            h_next = None
            xout, act = _ffn_down(pre, fcw[l], wdown[l], xmid, None, seq, f"ffn_down_{l}")
        saved.append((xin, h1, proj, o, lse, cat, xmid, h2, pre, act))
        xin, h1 = xout, h_next

    loss_part, dx, dxb, dgf = _loss_head(xin, final_norm_g[None], target, "loss_head")

    dg1, dg2, dga, dgc = [None] * DEPTH, [None] * DEPTH, [None] * DEPTH, [None] * DEPTH
    for l in reversed(range(DEPTH)):
        xin, h1, proj, o, lse, cat, xmid, h2, pre, act = saved[l]
        d_act = _matmul(
            dxb, wdown[l], grid=(nt, N_UP_PAIRS, 1), dims=NT, name=f"d_act_{l}",
            a_spec=pl.BlockSpec((TM, D_MODEL), lambda i, j, k: (i, 0)),
            b_spec=pl.BlockSpec((None, UP_CHUNK, D_MODEL), lambda i, j, k: (j, 0, 0)),
            o_spec=pl.BlockSpec((None, TM, UP_CHUNK), lambda i, j, k: (j, i, 0)),
            o_shape=(N_UP_PAIRS, t, UP_CHUNK), o_dtype=BF16)
        g_down = _matmul(
            act, dxb, grid=(N_UP_PAIRS, 1, 1), dims=TN, name=f"g_down_{l}",
            a_spec=pl.BlockSpec((None, t, UP_CHUNK), lambda i, j, k: (i, 0, 0)),
            b_spec=pl.BlockSpec((t, D_MODEL), full),
            o_spec=pl.BlockSpec((None, UP_CHUNK, D_MODEL), lambda i, j, k: (i, 0, 0)),
            o_shape=(N_UP_PAIRS, UP_CHUNK, D_MODEL), o_dtype=BF16).reshape(N_DEV, down_rows, D_MODEL)
        d_pre, d_fcw = _ffn_act_bwd(pre, d_act, fcw[l], nbatch, seq)
        d_pre = d_pre.reshape(N_DEV, t, UP_CHUNK)
        dxm, dxmb, dg2[l] = _matmul_norm_bwd(
            d_pre, wup[l], xmid, norm2_g[l][None], dx, dims=NN, name=f"d_h2_{l}",
            a_spec=chunks_of(N_DEV), b_spec=whole(N_DEV, UP_CHUNK, D_MODEL))
        g_up = _matmul(
            d_pre, h2, grid=(N_DEV, 1, 1), dims=TN, name=f"g_up_{l}",
            a_spec=pl.BlockSpec((None, t, UP_CHUNK), lambda i, j, k: (i, 0, 0)),
            b_spec=pl.BlockSpec((t, D_MODEL), full),
            o_spec=pl.BlockSpec((None, UP_CHUNK, D_MODEL), lambda i, j, k: (i, 0, 0)),
            o_shape=(N_DEV, UP_CHUNK, D_MODEL), o_dtype=BF16)
        g_out = _matmul(
            cat, dxmb, grid=(1, 1, nt), dims=TN, name=f"g_out_{l}",
            a_spec=pl.BlockSpec((TM, D_MODEL), lambda i, j, k: (k, 0)),
            b_spec=pl.BlockSpec((TM, D_MODEL), lambda i, j, k: (k, 0)),
            o_spec=pl.BlockSpec((D_MODEL, D_MODEL), full),
            o_shape=(D_MODEL, D_MODEL), o_dtype=BF16).reshape(N_DEV, out_rows, D_MODEL)
        if l == 0:
            land_out0, land_up0, land_down0 = _sequencer_exchange(
                [_Item(g_out, True), _Item(g_up, True), _Item(g_down, True)], "scatter_0a", 5)
        d_cat = _matmul(
            dxmb, wout[l], grid=(nta, 1, 1), dims=NT, name=f"d_cat_{l}",
            a_spec=pl.BlockSpec((TM_ACC, D_MODEL), lambda i, j, k: (i, 0)),
            b_spec=pl.BlockSpec((D_MODEL, D_MODEL), full),
            o_spec=pl.BlockSpec((TM_ACC, D_MODEL), lambda i, j, k: (i, 0)), o_shape=(t, D_MODEL), o_dtype=BF16)
        d_proj, dga[l] = _attn_bwd(proj, o, lse, d_cat, attn_out_g[l][None], nbatch, seq)
        d_proj, d_mcw, dgc[l] = _convmix_bwd(proj, d_cat, d_proj, mcw[l], conv_out_g[l][None], nbatch, seq)
        g_in = _matmul(
            h1, d_proj, grid=(1, D_IN // TN_IN, 1), dims=TN, name=f"g_in_{l}",
            a_spec=pl.BlockSpec((t, D_MODEL), full),
            b_spec=pl.BlockSpec((t, TN_IN), lambda i, j, k: (0, j)),
            o_spec=pl.BlockSpec((D_MODEL, TN_IN), lambda i, j, k: (0, j)),
            o_shape=(D_MODEL, D_IN), o_dtype=BF16)
        g_cw = jnp.concatenate(
            [d_fcw.reshape(N_DEV, 3, UP_CHUNK), d_mcw.reshape(3, N_DEV, D_CONV // N_DEV).transpose(1, 0, 2)], axis=-1)
        if l == 0:
            land_in0, land_cw0 = _sequencer_exchange([_Item(g_in, "cols"), _Item(g_cw, True)], "scatter_0b", 6)
        else:
            land_in1, land_out1, land_up1, land_down1, land_cw1 = _sequencer_exchange(
                [_Item(g_in, "cols"), _Item(g_out, True), _Item(g_up, True), _Item(g_down, True), _Item(g_cw, True)],
                "scatter_1", 4)
        dx, dxb, dg1[l] = _matmul_norm_bwd(
            d_proj, win[l], xin, norm1_g[l][None], dxm, dims=NT, name=f"d_h1_{l}",
            a_spec=rows_of(D_IN), b_spec=whole(D_MODEL, D_IN))

    def pack_small(n1, a, c, n2, f):
        return jnp.concatenate(
            [n1, n2, f[None], jnp.concatenate([a, c], axis=-1), jnp.zeros((1, D_MODEL), F32)], axis=0)[None]

    small = jnp.concatenate(
        [dg1[0], dg1[1], dg2[0], dg2[1], dgf,
         jnp.concatenate([dga[0], dgc[0]], axis=-1), jnp.concatenate([dga[1], dgc[1]], axis=-1),
         jnp.pad(loss_part, ((0, 0), (0, D_MODEL - LANES)))], axis=0)
    (land_small,) = _exchange([_Item(small, False)], "gather_gain_grads")
    res_small = _adamw(
        [land_small], pack_small(norm1_g, attn_out_g, conv_out_g, norm2_g, final_norm_g),
        pack_small(m_norm1_g, m_attn_out_g, m_conv_out_g, m_norm2_g, m_final_norm_g),
        pack_small(v_norm1_g, v_attn_out_g, v_conv_out_g, v_norm2_g, v_final_norm_g), SUBLANES, "adamw_gains")
    res_out = _adamw([land_out0, land_out1], w_out, m_w_out, v_w_out, out_rows, "adamw_w_out", after=[res_small[0]])
    res_up_t = _adamw([land_up0, land_up1], up_t, m_up_t, v_up_t, UP_CHUNK // 4, "adamw_ffn_up", after=[res_out[0]])
    res_up = [jnp.swapaxes(r, 1, 2) for r in res_up_t]
    res_down = _adamw([land_down0, land_down1], ffn_down, m_ffn_down, v_ffn_down, down_rows, "adamw_ffn_down",
                      after=[res_up_t[0]])
    res_in = _adamw([land_in0, land_in1], w_in, m_w_in, v_w_in, 256, "adamw_w_in", after=[res_down[0]])
    res_cw = _adamw(
        [land_cw0, land_cw1], cw_local, jnp.concatenate([m_ffn_conv_w, m_mix_conv_w], axis=-1),
        jnp.concatenate([v_ffn_conv_w, v_mix_conv_w], axis=-1), 3, "adamw_conv_w", after=[res_in[0]])

    loss = res_small[0][0, SUBLANES - 1, 0]

    def unpack(kind):
        s = res_small[kind][0]
        cwr = res_cw[kind]
        return (s[0:2], res_in[kind], cwr[..., UP_CHUNK:], s[5:7, :D_ATTN], s[5:7, D_ATTN:], res_out[kind],
                s[2:4], res_up[kind], cwr[..., :UP_CHUNK], res_down[kind], s[4])

    return (loss, dx.reshape(nbatch, seq, d), *unpack(0), *unpack(1), *unpack(2), *unpack(3))
```

```python
import math

import jax
import jax.numpy as jnp
from jax import lax
from jax.experimental import pallas as pl
from jax.experimental.pallas import tpu as pltpu
from jax.experimental.pallas import tpu_sc as plsc

F32 = jnp.float32
BF16 = jnp.bfloat16

D_MODEL = 1024
D_ATTN = 512
D_CONV = 512
HEAD_DIM = 64
N_HEADS = 8
D_FF = 2816
DEPTH = 2
D_IN = 3 * D_ATTN + 3 * D_CONV
EPS = 1e-6
DILATIONS = (1, 4, 16)
BAND = 128
N_DEV = 8
IN_CHUNK = D_IN // N_DEV
UP_CHUNK = 2 * D_FF // N_DEV
N_UP_PAIRS = N_DEV // 2
CW_PACK = UP_CHUNK + D_CONV // N_DEV
ADAM_LR = 0.001
ADAM_B1 = 0.9
ADAM_B2 = 0.999
ADAM_EPS = 1e-08
ADAM_WD = 0.01
ADAM_STEP = 10
LANES = 128
SUBLANES = 8
VMEM_LIMIT = 56 * 1024 * 1024

NEG = -1e30
MESH = pl.DeviceIdType.MESH


def _params(sem=None, vmem=VMEM_LIMIT):
    return pltpu.CompilerParams(dimension_semantics=sem, vmem_limit_bytes=vmem)


NN = (((1,), (0,)), ((), ()))
NT = (((1,), (1,)), ((), ()))
TN = (((0,), (0,)), ((), ()))


def _contract(a_ref, b_ref, dims):
    def dot(av, bv):
        return lax.dot_general(av.astype(BF16), bv.astype(BF16), dims, preferred_element_type=F32)

    if len(a_ref.shape) == 2:
        return dot(a_ref[...], b_ref[...])
    part = dot(a_ref[0], b_ref[0])
    for c in range(1, a_ref.shape[0]):
        part = part + dot(a_ref[c], b_ref[c])
    return part


def _matmul(a, b, *, grid, a_spec, b_spec, o_spec, o_shape, o_dtype, dims, name, res=None, res_spec=None, after=()):
    nk = grid[2]
    o_block = tuple(s for s in o_spec.block_shape if s is not None)
    na = len(after)

    def body(*refs):
        refs = refs[:2 + (res is not None)] + refs[2 + (res is not None) + na:]
        if res is None:
            a_ref, b_ref, o_ref, *scr = refs
            r_ref = None
        else:
            a_ref, b_ref, r_ref, o_ref, *scr = refs
        part = _contract(a_ref, b_ref, dims)

        def finish(total):
            if r_ref is not None:
                total = total + r_ref[...]
            o_ref[...] = total.astype(o_dtype)

        if nk == 1:
            finish(part)
        else:
            acc = scr[0]
            k = pl.program_id(2)

            @pl.when(k == 0)
            def _():
                acc[...] = part

            @pl.when(k > 0)
            def _():
                acc[...] += part

            @pl.when(k == nk - 1)
            def _():
                finish(acc[...])

    in_specs = [a_spec, b_spec] + ([res_spec] if res is not None else []) + [pl.BlockSpec(memory_space=pl.ANY)] * na
    args = (a, b) + ((res,) if res is not None else ()) + tuple(after)
    return pl.pallas_call(
        body, name=name, grid=grid, in_specs=in_specs, out_specs=o_spec,
        out_shape=jax.ShapeDtypeStruct(o_shape, o_dtype),
        scratch_shapes=[pltpu.VMEM(o_block, F32)] if nk > 1 else [],
        compiler_params=_params(("parallel", "parallel", "arbitrary")),
    )(*args)


ROW_TILE = 512


def _rms_fwd(x, g, name):
    t, d = x.shape

    def body(x_ref, g_ref, h_ref):
        xv = x_ref[...]
        r = lax.rsqrt(jnp.mean(xv * xv, axis=-1, keepdims=True) + EPS)
        h_ref[...] = (xv * r * g_ref[...]).astype(BF16)

    return pl.pallas_call(
        body, name=name, grid=(t // ROW_TILE,),
        in_specs=[pl.BlockSpec((ROW_TILE, d), lambda i: (i, 0)), pl.BlockSpec((1, d), lambda i: (0, 0))],
        out_specs=pl.BlockSpec((ROW_TILE, d), lambda i: (i, 0)),
        out_shape=jax.ShapeDtypeStruct((t, d), BF16),
        compiler_params=_params(("parallel",)),
    )(x, g)


def _rms_bwd(x, g, dh, dres, name):
    t, d = x.shape

    def body(x_ref, g_ref, dh_ref, dres_ref, dx_ref, dxb_ref, dg_ref):
        xv = x_ref[...]
        r = lax.rsqrt(jnp.mean(xv * xv, axis=-1, keepdims=True) + EPS)
        xh = xv * r
        dhv = dh_ref[...]
        gd = dhv * g_ref[...]
        dx = r * (gd - xh * jnp.mean(gd * xh, axis=-1, keepdims=True)) + dres_ref[...]
        dx_ref[...] = dx
        dxb_ref[...] = dx.astype(BF16)
        part = jnp.sum(dhv * xh, axis=0, keepdims=True)

        @pl.when(pl.program_id(0) == 0)
        def _():
            dg_ref[...] = part

        @pl.when(pl.program_id(0) > 0)
        def _():
            dg_ref[...] += part

    row = pl.BlockSpec((ROW_TILE, d), lambda i: (i, 0))
    vec = pl.BlockSpec((1, d), lambda i: (0, 0))
    return pl.pallas_call(
        body, name=name, grid=(t // ROW_TILE,),
        in_specs=[row, vec, row, row], out_specs=[row, row, vec],
        out_shape=[jax.ShapeDtypeStruct((t, d), F32), jax.ShapeDtypeStruct((t, d), BF16),
                   jax.ShapeDtypeStruct((1, d), F32)],
        compiler_params=_params(("arbitrary",)),
    )(x, g, dh, dres)


def _matmul_norm(a, b, res, g, *, a_spec, b_spec, dims, name):
    t, d = res.shape

    def body(a_ref, b_ref, r_ref, g_ref, x_ref, h_ref):
        xv = _contract(a_ref, b_ref, dims) + r_ref[...]
        x_ref[...] = xv
        h_ref[...] = (xv * lax.rsqrt(jnp.mean(xv * xv, axis=-1, keepdims=True) + EPS) * g_ref[...]).astype(BF16)

    row = pl.BlockSpec((TM_ACC, d), lambda i: (i, 0))
    return pl.pallas_call(
        body, name=name, grid=(t // TM_ACC,),
        in_specs=[a_spec, b_spec, row, pl.BlockSpec((1, d), lambda i: (0, 0))], out_specs=[row, row],
        out_shape=[jax.ShapeDtypeStruct((t, d), F32), jax.ShapeDtypeStruct((t, d), BF16)],
        compiler_params=_params(("parallel",)),
    )(a, b, res, g)


def _matmul_norm_bwd(a, b, x, g, dres, *, a_spec, b_spec, dims, name):
    t, d = x.shape

    def body(a_ref, b_ref, x_ref, g_ref, dres_ref, dx_ref, dxb_ref, dg_ref):
        dhv = _contract(a_ref, b_ref, dims)
        xv = x_ref[...]
        r = lax.rsqrt(jnp.mean(xv * xv, axis=-1, keepdims=True) + EPS)
        xh = xv * r
        gd = dhv * g_ref[...]
        dx = r * (gd - xh * jnp.mean(gd * xh, axis=-1, keepdims=True)) + dres_ref[...]
        dx_ref[...] = dx
        dxb_ref[...] = dx.astype(BF16)
        part = jnp.sum(dhv * xh, axis=0, keepdims=True)

        @pl.when(pl.program_id(0) == 0)
        def _():
            dg_ref[...] = part

        @pl.when(pl.program_id(0) > 0)
        def _():
            dg_ref[...] += part

    row = pl.BlockSpec((TM_ACC, d), lambda i: (i, 0))
    vec = pl.BlockSpec((1, d), lambda i: (0, 0))
    return pl.pallas_call(
        body, name=name, grid=(t // TM_ACC,),
        in_specs=[a_spec, b_spec, row, vec, row], out_specs=[row, row, vec],
        out_shape=[jax.ShapeDtypeStruct((t, d), F32), jax.ShapeDtypeStruct((t, d), BF16),
                   jax.ShapeDtypeStruct((1, d), F32)],
        compiler_params=_params(("arbitrary",)),
    )(a, b, x, g, dres)


def _loss_head(x, g, target, name):
    t, d = x.shape

    def body(x_ref, g_ref, t_ref, loss_ref, dx_ref, dxb_ref, dg_ref):
        xv = x_ref[...]
        r = lax.rsqrt(jnp.mean(xv * xv, axis=-1, keepdims=True) + EPS)
        xh = xv * r
        gv = g_ref[...]
        err = xh * gv - t_ref[...]
        loss = jnp.full((1, LANES), 0.5 / d, F32) * jnp.sum(err * err)
        dy = err * (1.0 / d)
        gd = dy * gv
        dx = r * (gd - xh * jnp.mean(gd * xh, axis=-1, keepdims=True))
        dx_ref[...] = dx
        dxb_ref[...] = dx.astype(BF16)
        part = jnp.sum(dy * xh, axis=0, keepdims=True)

        @pl.when(pl.program_id(0) == 0)
        def _():
            dg_ref[...] = part
            loss_ref[...] = loss

        @pl.when(pl.program_id(0) > 0)
        def _():
            dg_ref[...] += part
            loss_ref[...] += loss

    row = pl.BlockSpec((ROW_TILE, d), lambda i: (i, 0))
    vec = pl.BlockSpec((1, d), lambda i: (0, 0))
    return pl.pallas_call(
        body, name=name, grid=(t // ROW_TILE,),
        in_specs=[row, vec, row],
        out_specs=[pl.BlockSpec((1, LANES), lambda i: (0, 0)), row, row, vec],
        out_shape=[jax.ShapeDtypeStruct((1, LANES), F32), jax.ShapeDtypeStruct((t, d), F32),
                   jax.ShapeDtypeStruct((t, d), BF16), jax.ShapeDtypeStruct((1, d), F32)],
        compiler_params=_params(("arbitrary",)),
    )(x, g, target)


def _group_matrix(n):
    shift = int(math.log2(HEAD_DIM))
    r = lax.broadcasted_iota(jnp.int32, (n, n), 0) >> shift
    c = lax.broadcasted_iota(jnp.int32, (n, n), 1) >> shift
    return (r == c).astype(BF16)


def _group_sum(v, gmat):
    hi = v.astype(BF16)
    lo = (v - hi.astype(F32)).astype(BF16)

    def dot(p):
        return jnp.dot(p, gmat, preferred_element_type=F32)

    return dot(hi) + dot(lo)


def _shift_rows(ext, k):
    return pltpu.roll(ext, k % ext.shape[0], 0)


def _store_columns(stage, out_hbm, sems, row0, nrows, col_blocks):
    rows = pl.ds(pl.multiple_of(row0, SUBLANES * 2), nrows)
    copies = [
        pltpu.make_async_copy(stage.at[i], out_hbm.at[rows, pl.ds(pl.multiple_of(cb * LANES, LANES), LANES)], sems.at[i])
        for i, cb in enumerate(col_blocks)
    ]
    for cp in copies:
        cp.start()
    for cp in copies:
        cp.wait()


def _attn_consts(width):
    i = lax.broadcasted_iota(jnp.int32, (BAND, width), 0)
    j = lax.broadcasted_iota(jnp.int32, (BAND, width), 1)
    dist = (width - BAND) + i - j
    inwin = (dist >= 0) & (dist <= BAND)
    return dist.astype(F32), inwin, j


def _head_masks():
    lane = lax.broadcasted_iota(jnp.int32, (1, LANES), 1)
    return [(lane < HEAD_DIM).astype(F32), (lane >= HEAD_DIM).astype(F32)]


def _pair_bias(slope, dil):
    distf, inwin, _ = _attn_consts(2 * BAND)
    return jnp.concatenate([jnp.where(inwin, distf * (slope[hh] * (-float(dil))), NEG) for hh in range(2)], axis=0)


def _stack_heads(xv, hmask):
    return jnp.concatenate([xv * hmask[0], xv * hmask[1]], axis=0).astype(BF16)


FWD_UNROLL = 8
BWD_UNROLL = 8


def _unroll(trips, most):
    return max(u for u in range(1, most + 1) if trips % u == 0)


def _for_blocks(seq, dil, block, most):
    nb = seq // dil // BAND

    def residue(r, carry):
        base = r * nb
        block(pl.multiple_of(base * BAND, BAND), None)
        if nb > 1:
            def rest(n, c):
                block(pl.multiple_of((base + n) * BAND, BAND), pl.multiple_of((base + n - 1) * BAND, BAND))
                return c

            lax.fori_loop(1, nb, rest, 0, unroll=_unroll(nb - 1, most))
        return carry

    if dil == 1:
        residue(0, 0)
    else:
        lax.fori_loop(0, dil, residue, 0, unroll=_unroll(dil, max(1, most // nb)))


def _permute_in(src_ref, dst_ref, dil, seq):
    length = seq // dil
    for r in range(dil):
        dst_ref[pl.ds(r * length, length), :] = src_ref[pl.ds(r, length, stride=dil), :].astype(dst_ref.dtype)


def _slopes_table():
    slopes = 2.0 ** (-8.0 * jnp.arange(1, N_HEADS + 1, dtype=F32) / N_HEADS)
    return jnp.broadcast_to(slopes[:, None], (N_HEADS, 2 * BAND))


def _attn_fwd(proj, attn_g, nbatch, seq):
    t = nbatch * seq
    scale = HEAD_DIM ** -0.5

    def body(q_ref, k_ref, v_ref, g_ref, sl_ref, o_ref, lse_ref, cat_ref, pq, pk, pv, po, pm, pll, ao, am, al):
        hp = pl.program_id(1)
        hmask = _head_masks()
        slope = [sl_ref[pl.ds(2 * hp + hh, 1), :] for hh in range(2)]

        def run_branch(dil, qs, ks, vs, osink, msink, lsink):
            bias = _pair_bias(slope, dil)

            def block(row0, prow):
                cur = pl.ds(row0, BAND)
                q2 = _stack_heads(qs[cur, :] * scale, hmask)
                if prow is None:
                    kk, vv, bias_b = ks[cur, :], vs[cur, :], bias[:, BAND:]
                else:
                    prev = pl.ds(prow, BAND)
                    kk = jnp.concatenate([ks[prev, :], ks[cur, :]], axis=0)
                    vv = jnp.concatenate([vs[prev, :], vs[cur, :]], axis=0)
                    bias_b = bias
                s = lax.dot_general(q2, kk.astype(BF16), NT, preferred_element_type=F32) + bias_b
                m = jnp.max(s, axis=1, keepdims=True)
                p = jnp.exp(s - m)
                l = jnp.sum(p, axis=1, keepdims=True)
                pb = p.astype(BF16)
                o = jnp.dot(jnp.concatenate([pb[:BAND], pb[BAND:]], axis=1), _stack_heads(vv, hmask),
                            preferred_element_type=F32)
                osink[cur, :] = o
                msink[cur, :] = m[:BAND] * hmask[0] + m[BAND:] * hmask[1]
                lsink[cur, :] = l[:BAND] * hmask[0] + l[BAND:] * hmask[1]

            _for_blocks(seq, dil, block, FWD_UNROLL)

        run_branch(1, q_ref, k_ref, v_ref, ao, am, al)
        for dil in DILATIONS[1:]:
            length = seq // dil
            _permute_in(q_ref, pq, dil, seq)
            _permute_in(k_ref, pk, dil, seq)
            _permute_in(v_ref, pv, dil, seq)
            run_branch(dil, pq, pk, pv, po, pm, pll)
            for r in range(dil):
                nat = pl.ds(r, length, stride=dil)
                per = pl.ds(r * length, length)
                m0 = am[nat, :]
                mb = pm[per, :]
                mn = jnp.maximum(m0, mb)
                e0 = jnp.exp(m0 - mn)
                eb = jnp.exp(mb - mn)
                ao[nat, :] = ao[nat, :] * e0 + po[per, :] * eb
                al[nat, :] = al[nat, :] * e0 + pll[per, :] * eb
                am[nat, :] = mn

        gmat = _group_matrix(LANES)
        gv = g_ref[...]

        def fin(c, carry):
            rows = pl.ds(pl.multiple_of(c * 256, 256), 256)
            lv = al[rows, :]
            o = ao[rows, :] / lv
            o_ref[rows, :] = o
            lse_ref[rows, :] = am[rows, :] + jnp.log(lv)
            ms = _group_sum(o * o, gmat) * (1.0 / HEAD_DIM)
            cat_ref[rows, :] = (o * lax.rsqrt(ms + EPS) * gv).astype(BF16)
            return carry

        lax.fori_loop(0, seq // 256, fin, 0)

    nq = D_ATTN // LANES
    blk = lambda off: pl.BlockSpec((seq, LANES), lambda b, h: (b, h + off))
    scratch = [pltpu.VMEM((seq, LANES), F32) for _ in range(9)]
    return pl.pallas_call(
        body, name="attn_fwd", grid=(nbatch, nq),
        in_specs=[blk(0), blk(nq), blk(2 * nq), pl.BlockSpec((1, LANES), lambda b, h: (0, h)),
                  pl.BlockSpec((N_HEADS, 2 * BAND), lambda b, h: (0, 0))],
        out_specs=[blk(0), blk(0), blk(0)],
        out_shape=[jax.ShapeDtypeStruct((t, D_ATTN), F32), jax.ShapeDtypeStruct((t, D_ATTN), F32),
                   jax.ShapeDtypeStruct((t, D_MODEL), BF16)],
        scratch_shapes=scratch,
        compiler_params=_params(("parallel", "parallel")),
    )(proj, proj, proj, attn_g, _slopes_table())


def _attn_bwd(proj, o, lse, d_cat, attn_g, nbatch, seq):
    t = nbatch * seq
    scale = HEAD_DIM ** -0.5

    def body(q_ref, k_ref, v_ref, o_ref, lse_ref, dy_ref, g_ref, sl_ref, dproj_ref, dg_ref,
             do_n, dl_n, dq_n, dk_n, dv_n, pq, pk, pv, pdo, plse, pdl, pdq, pdk, pdv, stage, sems):
        hp = pl.program_id(0)
        hmask = _head_masks()
        slope = [sl_ref[pl.ds(2 * hp + hh, 1), :] for hh in range(2)]
        gmat = _group_matrix(LANES)
        gv = g_ref[...]

        def prep(c, dg):
            rows = pl.ds(pl.multiple_of(c * 256, 256), 256)
            ov = o_ref[rows, :]
            dyn = dy_ref[rows, :].astype(F32)
            r = lax.rsqrt(_group_sum(ov * ov, gmat) * (1.0 / HEAD_DIM) + EPS)
            gd = dyn * gv
            oh = ov * r
            do = r * (gd - oh * (_group_sum(gd * oh, gmat) * (1.0 / HEAD_DIM)))
            do_n[rows, :] = do
            dl_n[rows, :] = _group_sum(do * ov, gmat)
            return dg + jnp.sum(dyn * oh, axis=0, keepdims=True)

        dg = lax.fori_loop(0, seq // 256, prep, jnp.zeros((1, LANES), F32))

        @pl.when(pl.program_id(1) == 0)
        def _():
            dg_ref[...] = dg

        @pl.when(pl.program_id(1) > 0)
        def _():
            dg_ref[...] += dg

        def clear(*refs):
            def step(c, carry):
                rows = pl.ds(pl.multiple_of(c * 256, 256), 256)
                for ref in refs:
                    ref[rows, :] = jnp.zeros((256, LANES), F32)
                return carry

            lax.fori_loop(0, seq // 256, step, 0)

        clear(dq_n, dk_n, dv_n)

        def run_branch(dil, qs, ks, vs, dos, lses, dls, dqs, dks, dvs):
            bias = _pair_bias(slope, dil)

            def per_head(xv):
                return jnp.concatenate([xv[:, 0:1], xv[:, HEAD_DIM:HEAD_DIM + 1]], axis=0)

            def block(row0, prow):
                cur = pl.ds(row0, BAND)
                keys = cur if prow is None else pl.ds(prow, 2 * BAND)
                q2 = _stack_heads(qs[cur, :] * scale, hmask)
                do2 = _stack_heads(dos[cur, :], hmask)
                kk, vv = ks[keys, :], vs[keys, :]
                s = lax.dot_general(q2, kk.astype(BF16), NT, preferred_element_type=F32)
                s = s + (bias[:, BAND:] if prow is None else bias)
                p = jnp.exp(s - per_head(lses[cur, :]))
                dp = lax.dot_general(do2, vv.astype(BF16), NT, preferred_element_type=F32)
                ds = (p * (dp - per_head(dls[cur, :]))).astype(BF16)
                dqs[cur, :] += jnp.dot(jnp.concatenate([ds[:BAND], ds[BAND:]], axis=1), _stack_heads(kk, hmask),
                                       preferred_element_type=F32)
                dks[keys, :] += lax.dot_general(ds, q2, TN, preferred_element_type=F32)
                dvs[keys, :] += lax.dot_general(p.astype(BF16), do2, TN, preferred_element_type=F32)

            _for_blocks(seq, dil, block, BWD_UNROLL)

        run_branch(1, q_ref, k_ref, v_ref, do_n, lse_ref, dl_n, dq_n, dk_n, dv_n)
        for dil in DILATIONS[1:]:
            length = seq // dil
            for src, dst in ((q_ref, pq), (k_ref, pk), (v_ref, pv), (do_n, pdo), (lse_ref, plse), (dl_n, pdl)):
                _permute_in(src, dst, dil, seq)
            clear(pdq, pdk, pdv)
            run_branch(dil, pq, pk, pv, pdo, plse, pdl, pdq, pdk, pdv)
            for r in range(dil):
                nat = pl.ds(r, length, stride=dil)
                per = pl.ds(r * length, length)
                dq_n[nat, :] += pdq[per, :]
                dk_n[nat, :] += pdk[per, :]
                dv_n[nat, :] += pdv[per, :]

        def emit(c, carry):
            rows = pl.ds(pl.multiple_of(c * 256, 256), 256)
            stage[0, rows, :] = (dq_n[rows, :] * scale).astype(BF16)
            stage[1, rows, :] = dk_n[rows, :].astype(BF16)
            stage[2, rows, :] = dv_n[rows, :].astype(BF16)
            return carry

        lax.fori_loop(0, seq // 256, emit, 0)
        _store_columns(stage, dproj_ref, sems, pl.program_id(1) * seq, seq, [hp, nq + hp, 2 * nq + hp])

    nq = D_ATTN // LANES
    blk = lambda off: pl.BlockSpec((seq, LANES), lambda h, b: (b, h + off))
    vec = pl.BlockSpec((1, LANES), lambda h, b: (0, h))
    scratch = [pltpu.VMEM((seq, LANES), F32) for _ in range(14)]
    scratch += [pltpu.VMEM((3, seq, LANES), BF16), pltpu.SemaphoreType.DMA((3,))]
    d_proj, dg = pl.pallas_call(
        body, name="attn_bwd", grid=(nq, nbatch),
        in_specs=[blk(0), blk(nq), blk(2 * nq), blk(0), blk(0), blk(0), vec,
                  pl.BlockSpec((N_HEADS, 2 * BAND), lambda h, b: (0, 0))],
        out_specs=[pl.BlockSpec(memory_space=pl.ANY), vec],
        out_shape=[jax.ShapeDtypeStruct((t, D_IN), BF16), jax.ShapeDtypeStruct((1, D_ATTN), F32)],
        scratch_shapes=scratch,
        compiler_params=_params(("arbitrary", "arbitrary")),
    )(proj, proj, proj, o, lse, d_cat, attn_g, _slopes_table())
    return d_proj, dg


HALO = SUBLANES
PACKED_ROWS = 2 * SUBLANES


def _window(ref, c, rows, nchunks, after):
    row0 = pl.multiple_of(c * rows, rows)
    prev0 = pl.multiple_of(jnp.maximum(row0 - PACKED_ROWS, 0), PACKED_ROWS)
    before = ref[pl.ds(prev0, PACKED_ROWS), :].astype(F32)[PACKED_ROWS - HALO:] * (c > 0).astype(F32)
    parts = [before, ref[pl.ds(row0, rows), :].astype(F32)]
    if after:
        next0 = pl.multiple_of(jnp.minimum(row0 + rows, (nchunks - 1) * rows), PACKED_ROWS)
        parts.append(ref[pl.ds(next0, PACKED_ROWS), :].astype(F32)[:HALO] * (c < nchunks - 1).astype(F32))
    return jnp.concatenate(parts, axis=0)


def _behind(z):
    z1 = _shift_rows(z, 1)
    return z1, _shift_rows(z1, 1)


def _ahead(dy):
    d1 = _shift_rows(dy, -1)
    return d1, _shift_rows(d1, -1)


def _conv(z, w):
    z1, z2 = _behind(z)
    return w[0:1] * z2 + w[1:2] * z1 + w[2:3] * z


def _conv_bwd(dy, z, w, cur):
    d1, d2 = _ahead(dy)
    dz = w[2:3] * dy + w[1:2] * d1 + w[0:1] * d2
    return dz, [jnp.sum((d * z)[cur], axis=0, keepdims=True) for d in (d2, d1, dy)]


def _sigmoid(a):
    return 0.5 * jnp.tanh(0.5 * a) + 0.5


MIX_ROWS = 256
GATE_B_BLOCK = 3 * D_ATTN // LANES
GATE_C_BLOCK = GATE_B_BLOCK + D_CONV // LANES
U_BLOCK = GATE_C_BLOCK + D_CONV // LANES


def _convmix_fwd(proj, cat, mcw, conv_g, nbatch, seq):
    nchunks = seq // MIX_ROWS

    def body(gb_ref, gc_ref, u_ref, w_ref, g_ref, cat_in, cat_ref):
        del cat_in
        gmat = _group_matrix(LANES)
        w = w_ref[...]
        gv = g_ref[...]

        def step(c, carry):
            cur = pl.ds(pl.multiple_of(c * MIX_ROWS, MIX_ROWS), MIX_ROWS)
            z = _window(gc_ref, c, MIX_ROWS, nchunks, False) * _window(u_ref, c, MIX_ROWS, nchunks, False)
            y = gb_ref[cur, :] * _conv(z, w)[HALO:]
            ms = _group_sum(y * y, gmat) * (1.0 / HEAD_DIM)
            cat_ref[cur, :] = (y * lax.rsqrt(ms + EPS) * gv).astype(BF16)
            return carry

        lax.fori_loop(0, nchunks, step, 0)

    nc = D_CONV // LANES
    blk = lambda off: pl.BlockSpec((seq, LANES), lambda b, j: (b, j + off))
    return pl.pallas_call(
        body, name="convmix_fwd", grid=(nbatch, nc),
        in_specs=[blk(GATE_B_BLOCK), blk(GATE_C_BLOCK), blk(U_BLOCK),
                  pl.BlockSpec((3, LANES), lambda b, j: (0, j)), pl.BlockSpec((1, LANES), lambda b, j: (0, j)),
                  pl.BlockSpec(memory_space=pl.ANY)],
        out_specs=blk(D_ATTN // LANES),
        out_shape=jax.ShapeDtypeStruct(cat.shape, cat.dtype),
        input_output_aliases={5: 0},
        compiler_params=_params(("parallel", "parallel")),
    )(proj, proj, proj, mcw, conv_g, cat)


def _convmix_bwd(proj, d_cat, d_proj, mcw, conv_g, nbatch, seq):
    nchunks = seq // MIX_ROWS

    def body(gb_ref, gc_ref, u_ref, dy_ref, w_ref, g_ref, dproj_in, dproj_ref, dw_ref, dg_ref, stage, sems):
        del dproj_in
        cb = pl.program_id(0)
        b = pl.program_id(1)
        gmat = _group_matrix(LANES)
        w = w_ref[...]
        gv = g_ref[...]
        cur = slice(HALO, HALO + MIX_ROWS)

        def step(c, carry):
            rows = pl.ds(pl.multiple_of(c * MIX_ROWS, MIX_ROWS), MIX_ROWS)
            gb = _window(gb_ref, c, MIX_ROWS, nchunks, True)
            gc = _window(gc_ref, c, MIX_ROWS, nchunks, True)
            u = _window(u_ref, c, MIX_ROWS, nchunks, True)
            dyn = _window(dy_ref, c, MIX_ROWS, nchunks, True)
            z = gc * u
            conv = _conv(z, w)
            y = gb * conv
            r = lax.rsqrt(_group_sum(y * y, gmat) * (1.0 / HEAD_DIM) + EPS)
            yh = y * r
            gd = dyn * gv
            dy = r * (gd - yh * (_group_sum(gd * yh, gmat) * (1.0 / HEAD_DIM)))
            dz, dws = _conv_bwd(dy * gb, z, w, cur)
            stage[0, rows, :] = (dy * conv)[cur].astype(BF16)
            stage[1, rows, :] = (dz * u)[cur].astype(BF16)
            stage[2, rows, :] = (dz * gc)[cur].astype(BF16)
            dg = jnp.sum((dyn * yh)[cur], axis=0, keepdims=True)
            return tuple(a + d for a, d in zip(carry, dws + [dg]))

        zero = jnp.zeros((1, LANES), F32)
        dw0, dw1, dw2, dg = lax.fori_loop(0, nchunks, step, (zero, zero, zero, zero))

        @pl.when(b == 0)
        def _():
            dw_ref[0:1, :] = dw0
            dw_ref[1:2, :] = dw1
            dw_ref[2:3, :] = dw2
            dg_ref[...] = dg

        @pl.when(b > 0)
        def _():
            dw_ref[0:1, :] += dw0
            dw_ref[1:2, :] += dw1
            dw_ref[2:3, :] += dw2
            dg_ref[...] += dg

        _store_columns(stage, dproj_ref, sems, b * seq, seq, [GATE_B_BLOCK + cb, GATE_C_BLOCK + cb, U_BLOCK + cb])

    nc = D_CONV // LANES
    blk = lambda off: pl.BlockSpec((seq, LANES), lambda j, b: (b, j + off))
    return pl.pallas_call(
        body, name="convmix_bwd", grid=(nc, nbatch),
        in_specs=[blk(GATE_B_BLOCK), blk(GATE_C_BLOCK), blk(U_BLOCK), blk(D_ATTN // LANES),
                  pl.BlockSpec((3, LANES), lambda j, b: (0, j)), pl.BlockSpec((1, LANES), lambda j, b: (0, j)),
                  pl.BlockSpec(memory_space=pl.ANY)],
        out_specs=[pl.BlockSpec(memory_space=pl.ANY), pl.BlockSpec((3, LANES), lambda j, b: (0, j)),
                   pl.BlockSpec((1, LANES), lambda j, b: (0, j))],
        out_shape=[jax.ShapeDtypeStruct(d_proj.shape, d_proj.dtype), jax.ShapeDtypeStruct((3, D_CONV), F32),
                   jax.ShapeDtypeStruct((1, D_CONV), F32)],
        scratch_shapes=[pltpu.VMEM((3, seq, LANES), BF16), pltpu.SemaphoreType.DMA((3,))],
        input_output_aliases={6: 0},
        compiler_params=_params(("arbitrary", "arbitrary")),
    )(proj, proj, proj, d_cat, mcw, conv_g, d_proj)


FFN_ROWS = 256


def _ffn_act_fwd(pre, fcw, nbatch, seq):
    t = nbatch * seq
    nchunks = seq // FFN_ROWS

    def body(pre_ref, w_ref, act_ref):
        wa = w_ref[0]
        wc = w_ref[1]

        def step(c, carry):
            cur = pl.ds(pl.multiple_of(c * FFN_ROWS, FFN_ROWS), FFN_ROWS)
            a = _conv(_window(pre_ref.at[0], c, FFN_ROWS, nchunks, False), wa)[HALO:]
            v = _conv(_window(pre_ref.at[1], c, FFN_ROWS, nchunks, False), wc)[HALO:]
            act_ref[cur, :] = (a * _sigmoid(a) * v).astype(BF16)
            return carry

        lax.fori_loop(0, nchunks, step, 0)

    return pl.pallas_call(
        body, name="ffn_act_fwd", grid=(N_UP_PAIRS, nbatch),
        in_specs=[pl.BlockSpec((2, None, seq, UP_CHUNK), lambda i, b: (0, i, b, 0)),
                  pl.BlockSpec((2, None, 3, UP_CHUNK), lambda i, b: (0, i, 0, 0))],
        out_specs=pl.BlockSpec((None, seq, UP_CHUNK), lambda i, b: (i, b, 0)),
        out_shape=jax.ShapeDtypeStruct((N_UP_PAIRS, t, UP_CHUNK), BF16),
        compiler_params=_params(("parallel", "parallel")),
    )(pre, fcw)


def _ffn_down(pre, fcw, wdown, res, g, seq, name):
    t, d = res.shape
    tiles_per_seq = seq // FFN_ROWS

    def body(main_ref, halo_ref, w_ref, wd_ref, r_ref, *rest):
        if g is None:
            x_ref, act_ref = rest
        else:
            g_ref, x_ref, h_ref, act_ref = rest
        inside = ((pl.program_id(0) % tiles_per_seq) > 0).astype(F32)

        def window(part, p):
            before = halo_ref[part, p].astype(F32)[PACKED_ROWS - HALO:] * inside
            return jnp.concatenate([before, main_ref[part, p].astype(F32)], axis=0)

        total = r_ref[...]
        for p in range(N_UP_PAIRS):
            a = _conv(window(0, p), w_ref[0, p])[HALO:]
            v = _conv(window(1, p), w_ref[1, p])[HALO:]
            act = (a * _sigmoid(a) * v).astype(BF16)
            act_ref[p] = act
            total = total + jnp.dot(act, wd_ref[p], preferred_element_type=F32)
        x_ref[...] = total
        if g is not None:
            h_ref[...] = (total * lax.rsqrt(jnp.mean(total * total, axis=-1, keepdims=True) + EPS) * g_ref[...]).astype(BF16)

    row = pl.BlockSpec((FFN_ROWS, d), lambda i: (i, 0))
    tiles_per_halo = FFN_ROWS // PACKED_ROWS
    in_specs = [
        pl.BlockSpec((2, N_UP_PAIRS, FFN_ROWS, UP_CHUNK), lambda i: (0, 0, i, 0)),
        pl.BlockSpec((2, N_UP_PAIRS, PACKED_ROWS, UP_CHUNK), lambda i: (0, 0, jnp.maximum(i * tiles_per_halo - 1, 0), 0)),
        pl.BlockSpec((2, N_UP_PAIRS, 3, UP_CHUNK), lambda i: (0, 0, 0, 0)),
        pl.BlockSpec((N_UP_PAIRS, UP_CHUNK, d), lambda i: (0, 0, 0)), row]
    out_specs = [row]
    out_shape = [jax.ShapeDtypeStruct((t, d), F32)]
    args = [pre, pre, fcw, wdown, res]
    if g is not None:
        in_specs.append(pl.BlockSpec((1, d), lambda i: (0, 0)))
        out_specs.append(row)
        out_shape.append(jax.ShapeDtypeStruct((t, d), BF16))
        args.append(g)
    out_specs.append(pl.BlockSpec((N_UP_PAIRS, FFN_ROWS, UP_CHUNK), lambda i: (0, i, 0)))
    out_shape.append(jax.ShapeDtypeStruct((N_UP_PAIRS, t, UP_CHUNK), BF16))
    return pl.pallas_call(
        body, name=name, grid=(t // FFN_ROWS,), in_specs=in_specs, out_specs=out_specs, out_shape=out_shape,
        compiler_params=_params(("parallel",)),
    )(*args)


def _ffn_act_bwd(pre, d_act, fcw, nbatch, seq):
    nchunks = seq // FFN_ROWS

    def body(pre_ref, da_ref, w_ref, dpre_ref, dw_ref):
        b = pl.program_id(1)
        wa = w_ref[0]
        wc = w_ref[1]
        cur = slice(HALO, HALO + FFN_ROWS)

        def step(c, carry):
            rows = pl.ds(pl.multiple_of(c * FFN_ROWS, FFN_ROWS), FFN_ROWS)
            pg = _window(pre_ref.at[0], c, FFN_ROWS, nchunks, True)
            pv = _window(pre_ref.at[1], c, FFN_ROWS, nchunks, True)
            dact = _window(da_ref, c, FFN_ROWS, nchunks, True)
            a = _conv(pg, wa)
            v = _conv(pv, wc)
            sg = _sigmoid(a)
            asg = a * sg
            dzg, dwg = _conv_bwd(dact * v * (sg + asg - asg * sg), pg, wa, cur)
            dzv, dwv = _conv_bwd(dact * asg, pv, wc, cur)
            dpre_ref[0, rows, :] = dzg[cur].astype(BF16)
            dpre_ref[1, rows, :] = dzv[cur].astype(BF16)
            return tuple(acc + d for acc, d in zip(carry, dwg + dwv))

        zero = jnp.zeros((1, UP_CHUNK), F32)
        sums = lax.fori_loop(0, nchunks, step, (zero,) * 6)

        @pl.when(b == 0)
        def _():
            for i in range(6):
                dw_ref[i // 3, pl.ds(i % 3, 1), :] = sums[i]

        @pl.when(b > 0)
        def _():
            for i in range(6):
                dw_ref[i // 3, pl.ds(i % 3, 1), :] += sums[i]

    pair = pl.BlockSpec((2, None, seq, UP_CHUNK), lambda i, b: (0, i, b, 0))
    wspec = pl.BlockSpec((2, None, 3, UP_CHUNK), lambda i, b: (0, i, 0, 0))
    return pl.pallas_call(
        body, name="ffn_act_bwd", grid=(N_UP_PAIRS, nbatch),
        in_specs=[pair, pl.BlockSpec((None, seq, UP_CHUNK), lambda i, b: (i, b, 0)), wspec],
        out_specs=[pair, wspec],
        out_shape=[jax.ShapeDtypeStruct(pre.shape, BF16), jax.ShapeDtypeStruct(fcw.shape, F32)],
        compiler_params=_params(("parallel", "arbitrary")),
    )(pre, d_act, fcw)


def _adamw(lands, w, m, v, row_tile, name, after=()):
    nl = len(lands)
    _, nr, ncol = lands[0].shape
    c1 = 1.0 - ADAM_B1 ** ADAM_STEP
    c2 = 1.0 - ADAM_B2 ** ADAM_STEP

    def body(*refs):
        land_refs = refs[:nl]
        w_ref, m_ref, v_ref = refs[nl:nl + 3]
        g_ref, d_ref, mo_ref, vo_ref = refs[nl + 3 + len(after):]
        for l in range(nl):
            @pl.when(pl.program_id(0) == l)
            def _(l=l):
                g = land_refs[l][0].astype(F32)
                for j in range(1, N_DEV):
                    g = g + land_refs[l][j].astype(F32)
                g_ref[...] = g

        g = g_ref[...]
        m2 = ADAM_B1 * m_ref[...] + (1.0 - ADAM_B1) * g
        v2 = ADAM_B2 * v_ref[...] + (1.0 - ADAM_B2) * (g * g)
        mo_ref[...] = m2
        vo_ref[...] = v2
        d_ref[...] = -ADAM_LR * ((m2 / c1) / (jnp.sqrt(v2 / c2) + ADAM_EPS) + ADAM_WD * w_ref[...])

    def land_spec(l):
        return pl.BlockSpec((N_DEV, row_tile, ncol), lambda k, i: (0, jnp.where(k == l, i, 0), 0))

    tile = pl.BlockSpec((None, row_tile, ncol), lambda k, i: (k, i, 0))
    return pl.pallas_call(
        body, name=name, grid=(nl, nr // row_tile),
        in_specs=[land_spec(l) for l in range(nl)] + [tile, tile, tile] + [pl.BlockSpec(memory_space=pl.ANY)] * len(after),
        out_specs=[tile] * 4,
        out_shape=[jax.ShapeDtypeStruct(w.shape, F32)] * 4,
        compiler_params=_params(("arbitrary", "arbitrary")),
    )(*lands, w, m, v, *after)


class _Item:
    def __init__(self, src, chunked, land_cols=False):
        self.src, self.chunked, self.land_cols = src, chunked, land_cols
        if chunked == "cols":
            block = (src.shape[0], src.shape[1] // N_DEV)
        else:
            block = src.shape[1:] if chunked else src.shape
        self.width = block[-1]
        self.land_shape = (block[0], N_DEV * block[1]) if land_cols else (N_DEV,) + block

    def _cols(self, first, count=1):
        return pl.ds(pl.multiple_of(first * self.width, LANES), count * self.width)

    def part(self, src_ref, j):
        if self.chunked == "cols":
            return src_ref.at[:, self._cols(j)]
        return src_ref.at[j] if self.chunked else src_ref

    def slot(self, land_ref, s):
        return land_ref.at[:, self._cols(s)] if self.land_cols else land_ref.at[s]

    def seven(self, land_ref):
        return land_ref.at[:, self._cols(0, N_DEV - 1)] if self.land_cols else land_ref.at[pl.ds(0, N_DEV - 1)]


def _mesh_place():
    x, y, c = lax.axis_index("x"), lax.axis_index("y"), lax.axis_index("c")
    return x, y, c, 4 * x + 2 * y + c


def _flipped(x, y, c, k):
    px = 1 - x if k & 4 else x
    py = 1 - y if k & 2 else y
    pc = 1 - c if k & 1 else c
    return (px, py, pc), 4 * px + 2 * py + pc


PEER_ORDER = (2, 4, 6, 3, 5, 7, 1)


def _exchange(items, name):
    n = len(items)

    def body(*refs):
        srcs, lands = refs[:n], refs[n:2 * n]
        send, recv, local = refs[2 * n:]
        x, y, c, me = _mesh_place()

        def copy(i, k, chunk, slot, dev):
            return pltpu.make_async_remote_copy(
                src_ref=items[i].part(srcs[i], chunk), dst_ref=items[i].slot(lands[i], slot),
                send_sem=send.at[i, k - 1], recv_sem=recv.at[i, k - 1], device_id=dev, device_id_type=MESH)

        own = [pltpu.make_async_copy(items[i].part(srcs[i], me), items[i].slot(lands[i], me), local.at[i])
               for i in range(n)]
        for k in PEER_ORDER:
            dev, idx = _flipped(x, y, c, k)
            for i in range(n):
                copy(i, k, idx, me, dev).start()
        for cp in own:
            cp.start()
        for k in PEER_ORDER:
            dev, idx = _flipped(x, y, c, k)
            for i in range(n):
                copy(i, k, me, idx, dev).wait_recv()
        for k in PEER_ORDER:
            dev, idx = _flipped(x, y, c, k)
            for i in range(n):
                copy(i, k, idx, me, dev).wait_send()
        for cp in own:
            cp.wait()

    hbm = pl.BlockSpec(memory_space=pl.ANY)
    return pl.pallas_call(
        body, name=name,
        in_specs=[hbm] * n, out_specs=[hbm] * n,
        out_shape=[jax.ShapeDtypeStruct(it.land_shape, it.src.dtype) for it in items],
        scratch_shapes=[pltpu.SemaphoreType.DMA((n, N_DEV - 1)), pltpu.SemaphoreType.DMA((n, N_DEV - 1)),
                        pltpu.SemaphoreType.DMA((n,))],
        compiler_params=pltpu.CompilerParams(has_side_effects=True),
    )(*[it.src for it in items])


def _sequencer_exchange(items, name, collective_id):
    n = len(items)

    def body(*refs):
        srcs, lands = refs[:n], refs[n:2 * n]
        send, recv, local = refs[2 * n:]
        x, y, c, me = _mesh_place()
        barrier = pltpu.get_barrier_semaphore()
        for k in PEER_ORDER:
            pl.semaphore_signal(barrier, inc=1, device_id=_flipped(x, y, c, k)[0], device_id_type=MESH)
        pl.semaphore_wait(barrier, N_DEV - 1)

        def copy(i, k, chunk, slot, dev):
            return pltpu.make_async_remote_copy(
                src_ref=items[i].part(srcs[i], chunk), dst_ref=items[i].slot(lands[i], slot),
                send_sem=send.at[i, k - 1], recv_sem=recv.at[i, k - 1], device_id=dev, device_id_type=MESH)

        own = [pltpu.make_async_copy(items[i].part(srcs[i], me), items[i].slot(lands[i], me), local.at[i])
               for i in range(n)]
        for cp in own:
            cp.start()
        for k in PEER_ORDER:
            dev, idx = _flipped(x, y, c, k)
            for i in range(n):
                copy(i, k, idx, me, dev).start()
        for k in PEER_ORDER:
            dev, idx = _flipped(x, y, c, k)
            for i in range(n):
                copy(i, k, me, idx, dev).wait_recv()
        for k in PEER_ORDER:
            dev, idx = _flipped(x, y, c, k)
            for i in range(n):
                copy(i, k, idx, me, dev).wait_send()
        for cp in own:
            cp.wait()

    return pl.kernel(
        body, name=name,
        out_type=[jax.ShapeDtypeStruct(it.land_shape, it.src.dtype) for it in items],
        mesh=plsc.ScalarSubcoreMesh(axis_name="sequencer", num_cores=1),
        scratch_types=[pltpu.SemaphoreType.DMA((n, N_DEV - 1)), pltpu.SemaphoreType.DMA((n, N_DEV - 1)),
                       pltpu.SemaphoreType.DMA((n,))],
        compiler_params=pltpu.CompilerParams(collective_id=collective_id),
    )(*[it.src for it in items])


HBM_SPEC = pl.BlockSpec(memory_space=pltpu.HBM)
SEM_SPEC = pl.BlockSpec(memory_space=pltpu.SEMAPHORE)
DATAFLOW = pltpu.SideEffectType.DATAFLOW_SIDE_EFFECTING


def _exchange_start(items, name, after=()):
    n = len(items)
    na = len(after)

    def body(*refs):
        srcs, land_ins = refs[:n], refs[n:2 * n]
        outs = refs[2 * n + na:6 * n + na]
        (local,) = refs[6 * n + na:]
        del land_ins
        x, y, c, me = _mesh_place()
        own = [pltpu.make_async_copy(items[i].part(srcs[i], me), items[i].slot(outs[4 * i + 3], me), local.at[i])
               for i in range(n)]
        for cp in own:
            cp.start()
        for cp in own:
            cp.wait()
        for k in PEER_ORDER:
            dev, idx = _flipped(x, y, c, k)
            for i in range(n):
                send, recv, _, land = outs[4 * i:4 * i + 4]
                pltpu.make_async_remote_copy(
                    src_ref=items[i].part(srcs[i], idx), dst_ref=items[i].slot(land, me), send_sem=send, recv_sem=recv,
                    device_id=dev, device_id_type=MESH).start()

    out_shape, out_specs, args, lands = [], [], [], []
    for it in items:
        out_shape += [pltpu.SemaphoreType.DMA(()), pltpu.SemaphoreType.DMA(()),
                      pltpu.HBM(it.src.shape, it.src.dtype), pltpu.HBM(it.land_shape, it.src.dtype)]
        out_specs += [SEM_SPEC, SEM_SPEC, HBM_SPEC, HBM_SPEC]
        args.append(pltpu.with_memory_space_constraint(it.src, pltpu.HBM))
        lands.append(pltpu.with_memory_space_constraint(lax.empty(it.land_shape, it.src.dtype), pltpu.HBM))
    outs = pl.pallas_call(
        body, name=name,
        in_specs=[HBM_SPEC] * (2 * n) + [pl.BlockSpec(memory_space=pl.ANY)] * na,
        out_specs=out_specs, out_shape=out_shape,
        scratch_shapes=[pltpu.SemaphoreType.DMA((n,))],
        input_output_aliases={**{i: 4 * i + 2 for i in range(n)}, **{n + i: 4 * i + 3 for i in range(n)}},
        compiler_params=pltpu.CompilerParams(has_side_effects=DATAFLOW),
    )(*args, *lands, *after)
    return [tuple(outs[4 * i:4 * i + 4]) + (items[i],) for i in range(n)]


def _started(handles):
    return handles[0][2]


def _exchange_wait(handles, after, name):
    n = len(handles)

    def body(*refs):
        x, y, c, _ = _mesh_place()
        for i in range(n):
            src, land, send, recv = refs[4 * i:4 * i + 4]
            del src
            seven = handles[i][4].seven(land)
            cp = pltpu.make_async_remote_copy(src_ref=seven, dst_ref=seven, send_sem=send, recv_sem=recv,
                                              device_id=(x, y, 1 - c), device_id_type=MESH)
            cp.wait_send()
            cp.wait_recv()

    args, in_specs, out_shape = [], [], []
    for send, recv, src, land, _ in handles:
        args += [src, land, send, recv]
        in_specs += [HBM_SPEC, HBM_SPEC, SEM_SPEC, SEM_SPEC]
        out_shape += [pltpu.HBM(src.shape, src.dtype), pltpu.HBM(land.shape, land.dtype)]
    outs = pl.pallas_call(
        body, name=name,
        in_specs=in_specs + [pl.BlockSpec(memory_space=pl.ANY)] * len(after), out_specs=[HBM_SPEC] * (2 * n),
        out_shape=out_shape,
        input_output_aliases={**{4 * i: 2 * i for i in range(n)}, **{4 * i + 1: 2 * i + 1 for i in range(n)}},
        compiler_params=pltpu.CompilerParams(has_side_effects=DATAFLOW),
    )(*args, *after)
    return [outs[2 * i + 1] for i in range(n)]


TM = 1024
TM_ACC = 512
TN_IN = 768


def kernel(x, norm1_g, w_in, mix_conv_w, attn_out_g, conv_out_g, w_out, norm2_g, ffn_up, ffn_conv_w, ffn_down, final_norm_g, loss_target, m_norm1_g, m_w_in, m_mix_conv_w, m_attn_out_g, m_conv_out_g, m_w_out, m_norm2_g, m_ffn_up, m_ffn_conv_w, m_ffn_down, m_final_norm_g, v_norm1_g, v_w_in, v_mix_conv_w, v_attn_out_g, v_conv_out_g, v_w_out, v_norm2_g, v_ffn_up, v_ffn_conv_w, v_ffn_down, v_final_norm_g):
    nbatch, seq, d = x.shape
    t = nbatch * seq
    nt, nta = t // TM, t // TM_ACC
    out_rows = D_MODEL // N_DEV
    down_rows = D_FF // N_DEV
    xf = x.reshape(t, d)
    target = loss_target.reshape(t, d)

    cw_local = jnp.concatenate([ffn_conv_w, mix_conv_w], axis=-1)
    cast = lambda w: _Item(w.astype(BF16), False)
    cast_in = lambda w: _Item(w.astype(BF16), False, land_cols=True)
    cw_all, win0 = _sequencer_exchange([_Item(cw_local, False), cast_in(w_in[0])], "gather_a", 0)
    up_t, m_up_t, v_up_t = (jnp.swapaxes(a, 1, 2) for a in (ffn_up, m_ffn_up, v_ffn_up))
    wout0, wup0 = _sequencer_exchange([cast(w_out[0]), cast(up_t[0])], "gather_b", 1)
    (wdown0,) = _sequencer_exchange([cast(ffn_down[0])], "gather_c", 2)
    win1, wout1 = _sequencer_exchange([cast_in(w_in[1]), cast(w_out[1])], "gather_d", 3)
    wup1, wdown1 = _sequencer_exchange([cast(up_t[1]), cast(ffn_down[1])], "gather_e", 7)
    win, wup = [win0, win1], [wup0, wup1]
    wout = [w.reshape(D_MODEL, D_MODEL) for w in (wout0, wout1)]
    wdown = [w.reshape(N_UP_PAIRS, UP_CHUNK, D_MODEL) for w in (wdown0, wdown1)]
    fcw = [cw_all[:, k, :, :UP_CHUNK].reshape(2, N_UP_PAIRS, 3, UP_CHUNK) for k in range(DEPTH)]
    mcw = [cw_all[:, k, :, UP_CHUNK:].transpose(1, 0, 2).reshape(3, D_CONV) for k in range(DEPTH)]

    full = lambda i, j, k: (0, 0)

    saved = []
    xin = xf
    h1 = _rms_fwd(xin, norm1_g[0][None], "rms1_fwd_0")
    rows_of = lambda width: pl.BlockSpec((TM_ACC, width), lambda i: (i, 0))
    whole = lambda *shape: pl.BlockSpec(shape, lambda i: (0,) * len(shape))
    chunks_of = lambda n: pl.BlockSpec((n, TM_ACC, UP_CHUNK), lambda i: (0, i, 0))
    for l in range(DEPTH):
        proj = _matmul(
            h1, win[l], grid=(nt, D_IN // TN_IN, 1), dims=NN, name=f"proj_{l}",
            a_spec=pl.BlockSpec((TM, D_MODEL), lambda i, j, k: (i, 0)),
            b_spec=pl.BlockSpec((D_MODEL, TN_IN), lambda i, j, k: (0, j)),
            o_spec=pl.BlockSpec((TM, TN_IN), lambda i, j, k: (i, j)), o_shape=(t, D_IN), o_dtype=F32)
        o, lse, cat = _attn_fwd(proj, attn_out_g[l][None], nbatch, seq)
        cat = _convmix_fwd(proj, cat, mcw[l], conv_out_g[l][None], nbatch, seq)
        xmid, h2 = _matmul_norm(cat, wout[l], xin, norm2_g[l][None], dims=NN, name=f"mix_out_{l}",
                                a_spec=rows_of(D_MODEL), b_spec=whole(D_MODEL, D_MODEL))
        pre = _matmul(
            h2, wup[l], grid=(nt, N_DEV, 1), dims=NT, name=f"ffn_up_{l}",
            a_spec=pl.BlockSpec((TM, D_MODEL), lambda i, j, k: (i, 0)),
            b_spec=pl.BlockSpec((None, UP_CHUNK, D_MODEL), lambda i, j, k: (j, 0, 0)),
            o_spec=pl.BlockSpec((None, TM, UP_CHUNK), lambda i, j, k: (j, i, 0)),
            o_shape=(N_DEV, t, UP_CHUNK), o_dtype=BF16).reshape(2, N_UP_PAIRS, t, UP_CHUNK)
        if l + 1 < DEPTH:
            xout, h_next, act = _ffn_down(pre, fcw[l], wdown[l], xmid, norm1_g[l + 1][None], seq, f"ffn_down_{l}")
        else:
            h_next = None
            xout, act = _ffn_down(pre, fcw[l], wdown[l], xmid, None, seq, f"ffn_down_{l}")
        saved.append((xin, h1, proj, o, lse, cat, xmid, h2, pre, act))
        xin, h1 = xout, h_next

    loss_part, dx, dxb, dgf = _loss_head(xin, final_norm_g[None], target, "loss_head")

    dg1, dg2, dga, dgc = [None] * DEPTH, [None] * DEPTH, [None] * DEPTH, [None] * DEPTH
    for l in reversed(range(DEPTH)):
        xin, h1, proj, o, lse, cat, xmid, h2, pre, act = saved[l]
        d_act = _matmul(
            dxb, wdown[l], grid=(nt, N_UP_PAIRS, 1), dims=NT, name=f"d_act_{l}",
            a_spec=pl.BlockSpec((TM, D_MODEL), lambda i, j, k: (i, 0)),
            b_spec=pl.BlockSpec((None, UP_CHUNK, D_MODEL), lambda i, j, k: (j, 0, 0)),
            o_spec=pl.BlockSpec((None, TM, UP_CHUNK), lambda i, j, k: (j, i, 0)),
            o_shape=(N_UP_PAIRS, t, UP_CHUNK), o_dtype=BF16)
        g_down = _matmul(
            act, dxb, grid=(N_UP_PAIRS, 1, 1), dims=TN, name=f"g_down_{l}",
            a_spec=pl.BlockSpec((None, t, UP_CHUNK), lambda i, j, k: (i, 0, 0)),
            b_spec=pl.BlockSpec((t, D_MODEL), full),
            o_spec=pl.BlockSpec((None, UP_CHUNK, D_MODEL), lambda i, j, k: (i, 0, 0)),
            o_shape=(N_UP_PAIRS, UP_CHUNK, D_MODEL), o_dtype=BF16).reshape(N_DEV, down_rows, D_MODEL)
        d_pre, d_fcw = _ffn_act_bwd(pre, d_act, fcw[l], nbatch, seq)
        d_pre = d_pre.reshape(N_DEV, t, UP_CHUNK)
        dxm, dxmb, dg2[l] = _matmul_norm_bwd(
            d_pre, wup[l], xmid, norm2_g[l][None], dx, dims=NN, name=f"d_h2_{l}",
            a_spec=chunks_of(N_DEV), b_spec=whole(N_DEV, UP_CHUNK, D_MODEL))
        g_up = _matmul(
            d_pre, h2, grid=(N_DEV, 1, 1), dims=TN, name=f"g_up_{l}",
            a_spec=pl.BlockSpec((None, t, UP_CHUNK), lambda i, j, k: (i, 0, 0)),
            b_spec=pl.BlockSpec((t, D_MODEL), full),
            o_spec=pl.BlockSpec((None, UP_CHUNK, D_MODEL), lambda i, j, k: (i, 0, 0)),
            o_shape=(N_DEV, UP_CHUNK, D_MODEL), o_dtype=BF16)
        g_out = _matmul(
            cat, dxmb, grid=(1, 1, nt), dims=TN, name=f"g_out_{l}",
            a_spec=pl.BlockSpec((TM, D_MODEL), lambda i, j, k: (k, 0)),
            b_spec=pl.BlockSpec((TM, D_MODEL), lambda i, j, k: (k, 0)),
            o_spec=pl.BlockSpec((D_MODEL, D_MODEL), full),
            o_shape=(D_MODEL, D_MODEL), o_dtype=BF16).reshape(N_DEV, out_rows, D_MODEL)
        if l == 0:
            land_out0, land_up0, land_down0 = _sequencer_exchange(
                [_Item(g_out, True), _Item(g_up, True), _Item(g_down, True)], "scatter_0a", 5)
        d_cat = _matmul(
            dxmb, wout[l], grid=(nta, 1, 1), dims=NT, name=f"d_cat_{l}",
            a_spec=pl.BlockSpec((TM_ACC, D_MODEL), lambda i, j, k: (i, 0)),
            b_spec=pl.BlockSpec((D_MODEL, D_MODEL), full),
            o_spec=pl.BlockSpec((TM_ACC, D_MODEL), lambda i, j, k: (i, 0)), o_shape=(t, D_MODEL), o_dtype=BF16)
        d_proj, dga[l] = _attn_bwd(proj, o, lse, d_cat, attn_out_g[l][None], nbatch, seq)
        d_proj, d_mcw, dgc[l] = _convmix_bwd(proj, d_cat, d_proj, mcw[l], conv_out_g[l][None], nbatch, seq)
        g_in = _matmul(
            h1, d_proj, grid=(1, D_IN // TN_IN, 1), dims=TN, name=f"g_in_{l}",
            a_spec=pl.BlockSpec((t, D_MODEL), full),
            b_spec=pl.BlockSpec((t, TN_IN), lambda i, j, k: (0, j)),
            o_spec=pl.BlockSpec((D_MODEL, TN_IN), lambda i, j, k: (0, j)),
            o_shape=(D_MODEL, D_IN), o_dtype=BF16)
        g_cw = jnp.concatenate(
            [d_fcw.reshape(N_DEV, 3, UP_CHUNK), d_mcw.reshape(3, N_DEV, D_CONV // N_DEV).transpose(1, 0, 2)], axis=-1)
        if l == 0:
            land_in0, land_cw0 = _sequencer_exchange([_Item(g_in, "cols"), _Item(g_cw, True)], "scatter_0b", 6)
        else:
            land_in1, land_out1, land_up1, land_down1, land_cw1 = _sequencer_exchange(
                [_Item(g_in, "cols"), _Item(g_out, True), _Item(g_up, True), _Item(g_down, True), _Item(g_cw, True)],
                "scatter_1", 4)
        dx, dxb, dg1[l] = _matmul_norm_bwd(
            d_proj, win[l], xin, norm1_g[l][None], dxm, dims=NT, name=f"d_h1_{l}",
            a_spec=rows_of(D_IN), b_spec=whole(D_MODEL, D_IN))

    def pack_small(n1, a, c, n2, f):
        return jnp.concatenate(
            [n1, n2, f[None], jnp.concatenate([a, c], axis=-1), jnp.zeros((1, D_MODEL), F32)], axis=0)[None]

    small = jnp.concatenate(
        [dg1[0], dg1[1], dg2[0], dg2[1], dgf,
         jnp.concatenate([dga[0], dgc[0]], axis=-1), jnp.concatenate([dga[1], dgc[1]], axis=-1),
         jnp.pad(loss_part, ((0, 0), (0, D_MODEL - LANES)))], axis=0)
    (land_small,) = _exchange([_Item(small, False)], "gather_gain_grads")
    res_small = _adamw(
        [land_small], pack_small(norm1_g, attn_out_g, conv_out_g, norm2_g, final_norm_g),
        pack_small(m_norm1_g, m_attn_out_g, m_conv_out_g, m_norm2_g, m_final_norm_g),
        pack_small(v_norm1_g, v_attn_out_g, v_conv_out_g, v_norm2_g, v_final_norm_g), SUBLANES, "adamw_gains")
    res_out = _adamw([land_out0, land_out1], w_out, m_w_out, v_w_out, out_rows, "adamw_w_out", after=[res_small[0]])
    res_up_t = _adamw([land_up0, land_up1], up_t, m_up_t, v_up_t, UP_CHUNK // 4, "adamw_ffn_up", after=[res_out[0]])
    res_up = [jnp.swapaxes(r, 1, 2) for r in res_up_t]
    res_down = _adamw([land_down0, land_down1], ffn_down, m_ffn_down, v_ffn_down, down_rows, "adamw_ffn_down",
                      after=[res_up_t[0]])
    res_in = _adamw([land_in0, land_in1], w_in, m_w_in, v_w_in, 256, "adamw_w_in", after=[res_down[0]])
    res_cw = _adamw(
        [land_cw0, land_cw1], cw_local, jnp.concatenate([m_ffn_conv_w, m_mix_conv_w], axis=-1),
        jnp.concatenate([v_ffn_conv_w, v_mix_conv_w], axis=-1), 3, "adamw_conv_w", after=[res_in[0]])

    loss = res_small[0][0, SUBLANES - 1, 0]

    def unpack(kind):
        s = res_small[kind][0]
        cwr = res_cw[kind]
        return (s[0:2], res_in[kind], cwr[..., UP_CHUNK:], s[5:7, :D_ATTN], s[5:7, D_ATTN:], res_out[kind],
                s[2:4], res_up[kind], cwr[..., :UP_CHUNK], res_down[kind], s[4])

    return (loss, dx.reshape(nbatch, seq, d), *unpack(0), *unpack(1), *unpack(2), *unpack(3))
```

```python
import math

import jax
import jax.numpy as jnp
from jax import lax
from jax.experimental import pallas as pl
from jax.experimental.pallas import tpu as pltpu
from jax.experimental.pallas import tpu_sc as plsc

F32 = jnp.float32
BF16 = jnp.bfloat16

D_MODEL = 1024
D_ATTN = 512
D_CONV = 512
HEAD_DIM = 64
N_HEADS = 8
D_FF = 2816
DEPTH = 2
D_IN = 3 * D_ATTN + 3 * D_CONV
EPS = 1e-6
DILATIONS = (1, 4, 16)
BAND = 128
N_DEV = 8
IN_CHUNK = D_IN // N_DEV
UP_CHUNK = 2 * D_FF // N_DEV
N_UP_PAIRS = N_DEV // 2
CW_PACK = UP_CHUNK + D_CONV // N_DEV
ADAM_LR = 0.001
ADAM_B1 = 0.9
ADAM_B2 = 0.999
ADAM_EPS = 1e-08
ADAM_WD = 0.01
ADAM_STEP = 10
LANES = 128
SUBLANES = 8
VMEM_LIMIT = 56 * 1024 * 1024

NEG = -1e30
MESH = pl.DeviceIdType.MESH


def _params(sem=None, vmem=VMEM_LIMIT):
    return pltpu.CompilerParams(dimension_semantics=sem, vmem_limit_bytes=vmem)


NN = (((1,), (0,)), ((), ()))
NT = (((1,), (1,)), ((), ()))
TN = (((0,), (0,)), ((), ()))


def _contract(a_ref, b_ref, dims):
    def dot(av, bv):
        return lax.dot_general(av.astype(BF16), bv.astype(BF16), dims, preferred_element_type=F32)

    if len(a_ref.shape) == 2:
        return dot(a_ref[...], b_ref[...])
    part = dot(a_ref[0], b_ref[0])
    for c in range(1, a_ref.shape[0]):
        part = part + dot(a_ref[c], b_ref[c])
    return part


def _matmul(a, b, *, grid, a_spec, b_spec, o_spec, o_shape, o_dtype, dims, name, res=None, res_spec=None, after=()):
    nk = grid[2]
    o_block = tuple(s for s in o_spec.block_shape if s is not None)
    na = len(after)

    def body(*refs):
        refs = refs[:2 + (res is not None)] + refs[2 + (res is not None) + na:]
        if res is None:
            a_ref, b_ref, o_ref, *scr = refs
            r_ref = None
        else:
            a_ref, b_ref, r_ref, o_ref, *scr = refs
        part = _contract(a_ref, b_ref, dims)

        def finish(total):
            if r_ref is not None:
                total = total + r_ref[...]
            o_ref[...] = total.astype(o_dtype)

        if nk == 1:
            finish(part)
        else:
            acc = scr[0]
            k = pl.program_id(2)

            @pl.when(k == 0)
            def _():
                acc[...] = part

            @pl.when(k > 0)
            def _():
                acc[...] += part

            @pl.when(k == nk - 1)
            def _():
                finish(acc[...])

    in_specs = [a_spec, b_spec] + ([res_spec] if res is not None else []) + [pl.BlockSpec(memory_space=pl.ANY)] * na
    args = (a, b) + ((res,) if res is not None else ()) + tuple(after)
    return pl.pallas_call(
        body, name=name, grid=grid, in_specs=in_specs, out_specs=o_spec,
        out_shape=jax.ShapeDtypeStruct(o_shape, o_dtype),
        scratch_shapes=[pltpu.VMEM(o_block, F32)] if nk > 1 else [],
        compiler_params=_params(("parallel", "parallel", "arbitrary")),
    )(*args)


ROW_TILE = 512


def _rms_fwd(x, g, name):
    t, d = x.shape

    def body(x_ref, g_ref, h_ref):
        xv = x_ref[...]
        r = lax.rsqrt(jnp.mean(xv * xv, axis=-1, keepdims=True) + EPS)
        h_ref[...] = (xv * r * g_ref[...]).astype(BF16)

    return pl.pallas_call(
        body, name=name, grid=(t // ROW_TILE,),
        in_specs=[pl.BlockSpec((ROW_TILE, d), lambda i: (i, 0)), pl.BlockSpec((1, d), lambda i: (0, 0))],
        out_specs=pl.BlockSpec((ROW_TILE, d), lambda i: (i, 0)),
        out_shape=jax.ShapeDtypeStruct((t, d), BF16),
        compiler_params=_params(("parallel",)),
    )(x, g)


def _rms_bwd(x, g, dh, dres, name):
    t, d = x.shape

    def body(x_ref, g_ref, dh_ref, dres_ref, dx_ref, dxb_ref, dg_ref):
        xv = x_ref[...]
        r = lax.rsqrt(jnp.mean(xv * xv, axis=-1, keepdims=True) + EPS)
        xh = xv * r
        dhv = dh_ref[...]
        gd = dhv * g_ref[...]
        dx = r * (gd - xh * jnp.mean(gd * xh, axis=-1, keepdims=True)) + dres_ref[...]
        dx_ref[...] = dx
        dxb_ref[...] = dx.astype(BF16)
        part = jnp.sum(dhv * xh, axis=0, keepdims=True)

        @pl.when(pl.program_id(0) == 0)
        def _():
            dg_ref[...] = part

        @pl.when(pl.program_id(0) > 0)
        def _():
            dg_ref[...] += part

    row = pl.BlockSpec((ROW_TILE, d), lambda i: (i, 0))
    vec = pl.BlockSpec((1, d), lambda i: (0, 0))
    return pl.pallas_call(
        body, name=name, grid=(t // ROW_TILE,),
        in_specs=[row, vec, row, row], out_specs=[row, row, vec],
        out_shape=[jax.ShapeDtypeStruct((t, d), F32), jax.ShapeDtypeStruct((t, d), BF16),
                   jax.ShapeDtypeStruct((1, d), F32)],
        compiler_params=_params(("arbitrary",)),
    )(x, g, dh, dres)


def _matmul_norm(a, b, res, g, *, a_spec, b_spec, dims, name):
    t, d = res.shape

    def body(a_ref, b_ref, r_ref, g_ref, x_ref, h_ref):
        xv = _contract(a_ref, b_ref, dims) + r_ref[...]
        x_ref[...] = xv
        h_ref[...] = (xv * lax.rsqrt(jnp.mean(xv * xv, axis=-1, keepdims=True) + EPS) * g_ref[...]).astype(BF16)

    row = pl.BlockSpec((TM_ACC, d), lambda i: (i, 0))
    return pl.pallas_call(
        body, name=name, grid=(t // TM_ACC,),
        in_specs=[a_spec, b_spec, row, pl.BlockSpec((1, d), lambda i: (0, 0))], out_specs=[row, row],
        out_shape=[jax.ShapeDtypeStruct((t, d), F32), jax.ShapeDtypeStruct((t, d), BF16)],
        compiler_params=_params(("parallel",)),
    )(a, b, res, g)


def _matmul_norm_bwd(a, b, x, g, dres, *, a_spec, b_spec, dims, name):
    t, d = x.shape

    def body(a_ref, b_ref, x_ref, g_ref, dres_ref, dx_ref, dxb_ref, dg_ref):
        dhv = _contract(a_ref, b_ref, dims)
        xv = x_ref[...]
        r = lax.rsqrt(jnp.mean(xv * xv, axis=-1, keepdims=True) + EPS)
        xh = xv * r
        gd = dhv * g_ref[...]
        dx = r * (gd - xh * jnp.mean(gd * xh, axis=-1, keepdims=True)) + dres_ref[...]
        dx_ref[...] = dx
        dxb_ref[...] = dx.astype(BF16)
        part = jnp.sum(dhv * xh, axis=0, keepdims=True)

        @pl.when(pl.program_id(0) == 0)
        def _():
            dg_ref[...] = part

        @pl.when(pl.program_id(0) > 0)
        def _():
            dg_ref[...] += part

    row = pl.BlockSpec((TM_ACC, d), lambda i: (i, 0))
    vec = pl.BlockSpec((1, d), lambda i: (0, 0))
    return pl.pallas_call(
        body, name=name, grid=(t // TM_ACC,),
        in_specs=[a_spec, b_spec, row, vec, row], out_specs=[row, row, vec],
        out_shape=[jax.ShapeDtypeStruct((t, d), F32), jax.ShapeDtypeStruct((t, d), BF16),
                   jax.ShapeDtypeStruct((1, d), F32)],
        compiler_params=_params(("arbitrary",)),
    )(a, b, x, g, dres)


def _loss_head(x, g, target, name):
    t, d = x.shape

    def body(x_ref, g_ref, t_ref, loss_ref, dx_ref, dxb_ref, dg_ref):
        xv = x_ref[...]
        r = lax.rsqrt(jnp.mean(xv * xv, axis=-1, keepdims=True) + EPS)
        xh = xv * r
        gv = g_ref[...]
        err = xh * gv - t_ref[...]
        loss = jnp.full((1, LANES), 0.5 / d, F32) * jnp.sum(err * err)
        dy = err * (1.0 / d)
        gd = dy * gv
        dx = r * (gd - xh * jnp.mean(gd * xh, axis=-1, keepdims=True))
        dx_ref[...] = dx
        dxb_ref[...] = dx.astype(BF16)
        part = jnp.sum(dy * xh, axis=0, keepdims=True)

        @pl.when(pl.program_id(0) == 0)
        def _():
            dg_ref[...] = part
            loss_ref[...] = loss

        @pl.when(pl.program_id(0) > 0)
        def _():
            dg_ref[...] += part
            loss_ref[...] += loss

    row = pl.BlockSpec((ROW_TILE, d), lambda i: (i, 0))
    vec = pl.BlockSpec((1, d), lambda i: (0, 0))
    return pl.pallas_call(
        body, name=name, grid=(t // ROW_TILE,),
        in_specs=[row, vec, row],
        out_specs=[pl.BlockSpec((1, LANES), lambda i: (0, 0)), row, row, vec],
        out_shape=[jax.ShapeDtypeStruct((1, LANES), F32), jax.ShapeDtypeStruct((t, d), F32),
                   jax.ShapeDtypeStruct((t, d), BF16), jax.ShapeDtypeStruct((1, d), F32)],
        compiler_params=_params(("arbitrary",)),
    )(x, g, target)


def _group_matrix(n):
    shift = int(math.log2(HEAD_DIM))
    r = lax.broadcasted_iota(jnp.int32, (n, n), 0) >> shift
    c = lax.broadcasted_iota(jnp.int32, (n, n), 1) >> shift
    return (r == c).astype(BF16)


def _group_sum(v, gmat):
    hi = v.astype(BF16)
    lo = (v - hi.astype(F32)).astype(BF16)

    def dot(p):
        return jnp.dot(p, gmat, preferred_element_type=F32)

    return dot(hi) + dot(lo)


def _shift_rows(ext, k):
    return pltpu.roll(ext, k % ext.shape[0], 0)


def _store_columns(stage, out_hbm, sems, row0, nrows, col_blocks):
    rows = pl.ds(pl.multiple_of(row0, SUBLANES * 2), nrows)
    copies = [
        pltpu.make_async_copy(stage.at[i], out_hbm.at[rows, pl.ds(pl.multiple_of(cb * LANES, LANES), LANES)], sems.at[i])
        for i, cb in enumerate(col_blocks)
    ]
    for cp in copies:
        cp.start()
    for cp in copies:
        cp.wait()


def _attn_consts(width):
    i = lax.broadcasted_iota(jnp.int32, (BAND, width), 0)
    j = lax.broadcasted_iota(jnp.int32, (BAND, width), 1)
    dist = (width - BAND) + i - j
    inwin = (dist >= 0) & (dist <= BAND)
    return dist.astype(F32), inwin, j


def _head_masks():
    lane = lax.broadcasted_iota(jnp.int32, (1, LANES), 1)
    return [(lane < HEAD_DIM).astype(F32), (lane >= HEAD_DIM).astype(F32)]


def _pair_bias(slope, dil):
    distf, inwin, _ = _attn_consts(2 * BAND)
    return jnp.concatenate([jnp.where(inwin, distf * (slope[hh] * (-float(dil))), NEG) for hh in range(2)], axis=0)


def _stack_heads(xv, hmask):
    return jnp.concatenate([xv * hmask[0], xv * hmask[1]], axis=0).astype(BF16)


FWD_UNROLL = 8
BWD_UNROLL = 8


def _unroll(trips, most):
    return max(u for u in range(1, most + 1) if trips % u == 0)


def _for_blocks(seq, dil, block, most):
    nb = seq // dil // BAND

    def residue(r, carry):
        base = r * nb
        block(pl.multiple_of(base * BAND, BAND), None)
        if nb > 1:
            def rest(n, c):
                block(pl.multiple_of((base + n) * BAND, BAND), pl.multiple_of((base + n - 1) * BAND, BAND))
                return c

            lax.fori_loop(1, nb, rest, 0, unroll=_unroll(nb - 1, most))
        return carry

    if dil == 1:
        residue(0, 0)
    else:
        lax.fori_loop(0, dil, residue, 0, unroll=_unroll(dil, max(1, most // nb)))


def _permute_in(src_ref, dst_ref, dil, seq):
    length = seq // dil
    for r in range(dil):
        dst_ref[pl.ds(r * length, length), :] = src_ref[pl.ds(r, length, stride=dil), :].astype(dst_ref.dtype)


def _slopes_table():
    slopes = 2.0 ** (-8.0 * jnp.arange(1, N_HEADS + 1, dtype=F32) / N_HEADS)
    return jnp.broadcast_to(slopes[:, None], (N_HEADS, 2 * BAND))


def _attn_fwd(proj, attn_g, nbatch, seq):
    t = nbatch * seq
    scale = HEAD_DIM ** -0.5

    def body(q_ref, k_ref, v_ref, g_ref, sl_ref, o_ref, lse_ref, cat_ref, pq, pk, pv, po, pm, pll, ao, am, al):
        hp = pl.program_id(1)
        hmask = _head_masks()
        slope = [sl_ref[pl.ds(2 * hp + hh, 1), :] for hh in range(2)]

        def run_branch(dil, qs, ks, vs, osink, msink, lsink):
            bias = _pair_bias(slope, dil)

            def block(row0, prow):
                cur = pl.ds(row0, BAND)
                q2 = _stack_heads(qs[cur, :] * scale, hmask)
                if prow is None:
                    kk, vv, bias_b = ks[cur, :], vs[cur, :], bias[:, BAND:]
                else:
                    prev = pl.ds(prow, BAND)
                    kk = jnp.concatenate([ks[prev, :], ks[cur, :]], axis=0)
                    vv = jnp.concatenate([vs[prev, :], vs[cur, :]], axis=0)
                    bias_b = bias
                s = lax.dot_general(q2, kk.astype(BF16), NT, preferred_element_type=F32) + bias_b
                m = jnp.max(s, axis=1, keepdims=True)
                p = jnp.exp(s - m)
                l = jnp.sum(p, axis=1, keepdims=True)
                pb = p.astype(BF16)
                o = jnp.dot(jnp.concatenate([pb[:BAND], pb[BAND:]], axis=1), _stack_heads(vv, hmask),
                            preferred_element_type=F32)
                osink[cur, :] = o
                msink[cur, :] = m[:BAND] * hmask[0] + m[BAND:] * hmask[1]
                lsink[cur, :] = l[:BAND] * hmask[0] + l[BAND:] * hmask[1]

            _for_blocks(seq, dil, block, FWD_UNROLL)

        run_branch(1, q_ref, k_ref, v_ref, ao, am, al)
        for dil in DILATIONS[1:]:
            length = seq // dil
            _permute_in(q_ref, pq, dil, seq)
            _permute_in(k_ref, pk, dil, seq)
            _permute_in(v_ref, pv, dil, seq)
            run_branch(dil, pq, pk, pv, po, pm, pll)
            for r in range(dil):
                nat = pl.ds(r, length, stride=dil)
                per = pl.ds(r * length, length)
                m0 = am[nat, :]
                mb = pm[per, :]
                mn = jnp.maximum(m0, mb)
                e0 = jnp.exp(m0 - mn)
                eb = jnp.exp(mb - mn)
                ao[nat, :] = ao[nat, :] * e0 + po[per, :] * eb
                al[nat, :] = al[nat, :] * e0 + pll[per, :] * eb
                am[nat, :] = mn

        gmat = _group_matrix(LANES)
        gv = g_ref[...]

        def fin(c, carry):
            rows = pl.ds(pl.multiple_of(c * 256, 256), 256)
            lv = al[rows, :]
            o = ao[rows, :] / lv
            o_ref[rows, :] = o
            lse_ref[rows, :] = am[rows, :] + jnp.log(lv)
            ms = _group_sum(o * o, gmat) * (1.0 / HEAD_DIM)
            cat_ref[rows, :] = (o * lax.rsqrt(ms + EPS) * gv).astype(BF16)
            return carry

        lax.fori_loop(0, seq // 256, fin, 0)

    nq = D_ATTN // LANES
    blk = lambda off: pl.BlockSpec((seq, LANES), lambda b, h: (b, h + off))
    scratch = [pltpu.VMEM((seq, LANES), F32) for _ in range(9)]
    return pl.pallas_call(
        body, name="attn_fwd", grid=(nbatch, nq),
        in_specs=[blk(0), blk(nq), blk(2 * nq), pl.BlockSpec((1, LANES), lambda b, h: (0, h)),
                  pl.BlockSpec((N_HEADS, 2 * BAND), lambda b, h: (0, 0))],
        out_specs=[blk(0), blk(0), blk(0)],
        out_shape=[jax.ShapeDtypeStruct((t, D_ATTN), F32), jax.ShapeDtypeStruct((t, D_ATTN), F32),
                   jax.ShapeDtypeStruct((t, D_MODEL), BF16)],
        scratch_shapes=scratch,
        compiler_params=_params(("parallel", "parallel")),
    )(proj, proj, proj, attn_g, _slopes_table())


def _attn_bwd(proj, o, lse, d_cat, attn_g, nbatch, seq):
    t = nbatch * seq
    scale = HEAD_DIM ** -0.5

    def body(q_ref, k_ref, v_ref, o_ref, lse_ref, dy_ref, g_ref, sl_ref, dproj_ref, dg_ref,
             do_n, dl_n, dq_n, dk_n, dv_n, pq, pk, pv, pdo, plse, pdl, pdq, pdk, pdv, stage, sems):
        hp = pl.program_id(0)
        hmask = _head_masks()
        slope = [sl_ref[pl.ds(2 * hp + hh, 1), :] for hh in range(2)]
        gmat = _group_matrix(LANES)
        gv = g_ref[...]

        def prep(c, dg):
            rows = pl.ds(pl.multiple_of(c * 256, 256), 256)
            ov = o_ref[rows, :]
            dyn = dy_ref[rows, :].astype(F32)
            r = lax.rsqrt(_group_sum(ov * ov, gmat) * (1.0 / HEAD_DIM) + EPS)
            gd = dyn * gv
            oh = ov * r
            do = r * (gd - oh * (_group_sum(gd * oh, gmat) * (1.0 / HEAD_DIM)))
            do_n[rows, :] = do
            dl_n[rows, :] = _group_sum(do * ov, gmat)
            return dg + jnp.sum(dyn * oh, axis=0, keepdims=True)

        dg = lax.fori_loop(0, seq // 256, prep, jnp.zeros((1, LANES), F32))

        @pl.when(pl.program_id(1) == 0)
        def _():
            dg_ref[...] = dg

        @pl.when(pl.program_id(1) > 0)
        def _():
            dg_ref[...] += dg

        def clear(*refs):
            def step(c, carry):
                rows = pl.ds(pl.multiple_of(c * 256, 256), 256)
                for ref in refs:
                    ref[rows, :] = jnp.zeros((256, LANES), F32)
                return carry

            lax.fori_loop(0, seq // 256, step, 0)

        clear(dq_n, dk_n, dv_n)

        def run_branch(dil, qs, ks, vs, dos, lses, dls, dqs, dks, dvs):
            bias = _pair_bias(slope, dil)

            def per_head(xv):
                return jnp.concatenate([xv[:, 0:1], xv[:, HEAD_DIM:HEAD_DIM + 1]], axis=0)

            def block(row0, prow):
                cur = pl.ds(row0, BAND)
                keys = cur if prow is None else pl.ds(prow, 2 * BAND)
                q2 = _stack_heads(qs[cur, :] * scale, hmask)
                do2 = _stack_heads(dos[cur, :], hmask)
                kk, vv = ks[keys, :], vs[keys, :]
                s = lax.dot_general(q2, kk.astype(BF16), NT, preferred_element_type=F32)
                s = s + (bias[:, BAND:] if prow is None else bias)
                p = jnp.exp(s - per_head(lses[cur, :]))
                dp = lax.dot_general(do2, vv.astype(BF16), NT, preferred_element_type=F32)
                ds = (p * (dp - per_head(dls[cur, :]))).astype(BF16)
                dqs[cur, :] += jnp.dot(jnp.concatenate([ds[:BAND], ds[BAND:]], axis=1), _stack_heads(kk, hmask),
                                       preferred_element_type=F32)
                dks[keys, :] += lax.dot_general(ds, q2, TN, preferred_element_type=F32)
                dvs[keys, :] += lax.dot_general(p.astype(BF16), do2, TN, preferred_element_type=F32)

            _for_blocks(seq, dil, block, BWD_UNROLL)

        run_branch(1, q_ref, k_ref, v_ref, do_n, lse_ref, dl_n, dq_n, dk_n, dv_n)
        for dil in DILATIONS[1:]:
            length = seq // dil
            for src, dst in ((q_ref, pq), (k_ref, pk), (v_ref, pv), (do_n, pdo), (lse_ref, plse), (dl_n, pdl)):
                _permute_in(src, dst, dil, seq)
            clear(pdq, pdk, pdv)
            run_branch(dil, pq, pk, pv, pdo, plse, pdl, pdq, pdk, pdv)
            for r in range(dil):
                nat = pl.ds(r, length, stride=dil)
                per = pl.ds(r * length, length)
                dq_n[nat, :] += pdq[per, :]
                dk_n[nat, :] += pdk[per, :]
                dv_n[nat, :] += pdv[per, :]

        def emit(c, carry):
            rows = pl.ds(pl.multiple_of(c * 256, 256), 256)
            stage[0, rows, :] = (dq_n[rows, :] * scale).astype(BF16)
            stage[1, rows, :] = dk_n[rows, :].astype(BF16)
            stage[2, rows, :] = dv_n[rows, :].astype(BF16)
            return carry

        lax.fori_loop(0, seq // 256, emit, 0)
        _store_columns(stage, dproj_ref, sems, pl.program_id(1) * seq, seq, [hp, nq + hp, 2 * nq + hp])

    nq = D_ATTN // LANES
    blk = lambda off: pl.BlockSpec((seq, LANES), lambda h, b: (b, h + off))
    vec = pl.BlockSpec((1, LANES), lambda h, b: (0, h))
    scratch = [pltpu.VMEM((seq, LANES), F32) for _ in range(14)]
    scratch += [pltpu.VMEM((3, seq, LANES), BF16), pltpu.SemaphoreType.DMA((3,))]
    d_proj, dg = pl.pallas_call(
        body, name="attn_bwd", grid=(nq, nbatch),
        in_specs=[blk(0), blk(nq), blk(2 * nq), blk(0), blk(0), blk(0), vec,
                  pl.BlockSpec((N_HEADS, 2 * BAND), lambda h, b: (0, 0))],
        out_specs=[pl.BlockSpec(memory_space=pl.ANY), vec],
        out_shape=[jax.ShapeDtypeStruct((t, D_IN), BF16), jax.ShapeDtypeStruct((1, D_ATTN), F32)],
        scratch_shapes=scratch,
        compiler_params=_params(("arbitrary", "arbitrary")),
    )(proj, proj, proj, o, lse, d_cat, attn_g, _slopes_table())
    return d_proj, dg


HALO = SUBLANES
PACKED_ROWS = 2 * SUBLANES


def _window(ref, c, rows, nchunks, after):
    row0 = pl.multiple_of(c * rows, rows)
    prev0 = pl.multiple_of(jnp.maximum(row0 - PACKED_ROWS, 0), PACKED_ROWS)
    before = ref[pl.ds(prev0, PACKED_ROWS), :].astype(F32)[PACKED_ROWS - HALO:] * (c > 0).astype(F32)
    parts = [before, ref[pl.ds(row0, rows), :].astype(F32)]
    if after:
        next0 = pl.multiple_of(jnp.minimum(row0 + rows, (nchunks - 1) * rows), PACKED_ROWS)
        parts.append(ref[pl.ds(next0, PACKED_ROWS), :].astype(F32)[:HALO] * (c < nchunks - 1).astype(F32))
    return jnp.concatenate(parts, axis=0)


def _behind(z):
    z1 = _shift_rows(z, 1)
    return z1, _shift_rows(z1, 1)


def _ahead(dy):
    d1 = _shift_rows(dy, -1)
    return d1, _shift_rows(d1, -1)


def _conv(z, w):
    z1, z2 = _behind(z)
    return w[0:1] * z2 + w[1:2] * z1 + w[2:3] * z


def _conv_bwd(dy, z, w, cur):
    d1, d2 = _ahead(dy)
    dz = w[2:3] * dy + w[1:2] * d1 + w[0:1] * d2
    return dz, [jnp.sum((d * z)[cur], axis=0, keepdims=True) for d in (d2, d1, dy)]


def _sigmoid(a):
    return 0.5 * jnp.tanh(0.5 * a) + 0.5


MIX_ROWS = 256
GATE_B_BLOCK = 3 * D_ATTN // LANES
GATE_C_BLOCK = GATE_B_BLOCK + D_CONV // LANES
U_BLOCK = GATE_C_BLOCK + D_CONV // LANES


def _convmix_fwd(proj, cat, mcw, conv_g, nbatch, seq):
    nchunks = seq // MIX_ROWS

    def body(gb_ref, gc_ref, u_ref, w_ref, g_ref, cat_in, cat_ref):
        del cat_in
        gmat = _group_matrix(LANES)
        w = w_ref[...]
        gv = g_ref[...]

        def step(c, carry):
            cur = pl.ds(pl.multiple_of(c * MIX_ROWS, MIX_ROWS), MIX_ROWS)
            z = _window(gc_ref, c, MIX_ROWS, nchunks, False) * _window(u_ref, c, MIX_ROWS, nchunks, False)
            y = gb_ref[cur, :] * _conv(z, w)[HALO:]
            ms = _group_sum(y * y, gmat) * (1.0 / HEAD_DIM)
            cat_ref[cur, :] = (y * lax.rsqrt(ms + EPS) * gv).astype(BF16)
            return carry

        lax.fori_loop(0, nchunks, step, 0)

    nc = D_CONV // LANES
    blk = lambda off: pl.BlockSpec((seq, LANES), lambda b, j: (b, j + off))
    return pl.pallas_call(
        body, name="convmix_fwd", grid=(nbatch, nc),
        in_specs=[blk(GATE_B_BLOCK), blk(GATE_C_BLOCK), blk(U_BLOCK),
                  pl.BlockSpec((3, LANES), lambda b, j: (0, j)), pl.BlockSpec((1, LANES), lambda b, j: (0, j)),
                  pl.BlockSpec(memory_space=pl.ANY)],
        out_specs=blk(D_ATTN // LANES),
        out_shape=jax.ShapeDtypeStruct(cat.shape, cat.dtype),
        input_output_aliases={5: 0},
        compiler_params=_params(("parallel", "parallel")),
    )(proj, proj, proj, mcw, conv_g, cat)


def _convmix_bwd(proj, d_cat, d_proj, mcw, conv_g, nbatch, seq):
    nchunks = seq // MIX_ROWS

    def body(gb_ref, gc_ref, u_ref, dy_ref, w_ref, g_ref, dproj_in, dproj_ref, dw_ref, dg_ref, stage, sems):
        del dproj_in
        cb = pl.program_id(0)
        b = pl.program_id(1)
        gmat = _group_matrix(LANES)
        w = w_ref[...]
        gv = g_ref[...]
        cur = slice(HALO, HALO + MIX_ROWS)

        def step(c, carry):
            rows = pl.ds(pl.multiple_of(c * MIX_ROWS, MIX_ROWS), MIX_ROWS)
            gb = _window(gb_ref, c, MIX_ROWS, nchunks, True)
            gc = _window(gc_ref, c, MIX_ROWS, nchunks, True)
            u = _window(u_ref, c, MIX_ROWS, nchunks, True)
            dyn = _window(dy_ref, c, MIX_ROWS, nchunks, True)
            z = gc * u
            conv = _conv(z, w)
            y = gb * conv
            r = lax.rsqrt(_group_sum(y * y, gmat) * (1.0 / HEAD_DIM) + EPS)
            yh = y * r
            gd = dyn * gv
            dy = r * (gd - yh * (_group_sum(gd * yh, gmat) * (1.0 / HEAD_DIM)))
            dz, dws = _conv_bwd(dy * gb, z, w, cur)
            stage[0, rows, :] = (dy * conv)[cur].astype(BF16)
            stage[1, rows, :] = (dz * u)[cur].astype(BF16)
            stage[2, rows, :] = (dz * gc)[cur].astype(BF16)
            dg = jnp.sum((dyn * yh)[cur], axis=0, keepdims=True)
            return tuple(a + d for a, d in zip(carry, dws + [dg]))

        zero = jnp.zeros((1, LANES), F32)
        dw0, dw1, dw2, dg = lax.fori_loop(0, nchunks, step, (zero, zero, zero, zero))

        @pl.when(b == 0)
        def _():
            dw_ref[0:1, :] = dw0
            dw_ref[1:2, :] = dw1
            dw_ref[2:3, :] = dw2
            dg_ref[...] = dg

        @pl.when(b > 0)
        def _():
            dw_ref[0:1, :] += dw0
            dw_ref[1:2, :] += dw1
            dw_ref[2:3, :] += dw2
            dg_ref[...] += dg

        _store_columns(stage, dproj_ref, sems, b * seq, seq, [GATE_B_BLOCK + cb, GATE_C_BLOCK + cb, U_BLOCK + cb])

    nc = D_CONV // LANES
    blk = lambda off: pl.BlockSpec((seq, LANES), lambda j, b: (b, j + off))
    return pl.pallas_call(
        body, name="convmix_bwd", grid=(nc, nbatch),
        in_specs=[blk(GATE_B_BLOCK), blk(GATE_C_BLOCK), blk(U_BLOCK), blk(D_ATTN // LANES),
                  pl.BlockSpec((3, LANES), lambda j, b: (0, j)), pl.BlockSpec((1, LANES), lambda j, b: (0, j)),
                  pl.BlockSpec(memory_space=pl.ANY)],
        out_specs=[pl.BlockSpec(memory_space=pl.ANY), pl.BlockSpec((3, LANES), lambda j, b: (0, j)),
                   pl.BlockSpec((1, LANES), lambda j, b: (0, j))],
        out_shape=[jax.ShapeDtypeStruct(d_proj.shape, d_proj.dtype), jax.ShapeDtypeStruct((3, D_CONV), F32),
                   jax.ShapeDtypeStruct((1, D_CONV), F32)],
        scratch_shapes=[pltpu.VMEM((3, seq, LANES), BF16), pltpu.SemaphoreType.DMA((3,))],
        input_output_aliases={6: 0},
        compiler_params=_params(("arbitrary", "arbitrary")),
    )(proj, proj, proj, d_cat, mcw, conv_g, d_proj)


FFN_ROWS = 256


def _ffn_act_fwd(pre, fcw, nbatch, seq):
    t = nbatch * seq
    nchunks = seq // FFN_ROWS

    def body(pre_ref, w_ref, act_ref):
        wa = w_ref[0]
        wc = w_ref[1]

        def step(c, carry):
            cur = pl.ds(pl.multiple_of(c * FFN_ROWS, FFN_ROWS), FFN_ROWS)
            a = _conv(_window(pre_ref.at[0], c, FFN_ROWS, nchunks, False), wa)[HALO:]
            v = _conv(_window(pre_ref.at[1], c, FFN_ROWS, nchunks, False), wc)[HALO:]
            act_ref[cur, :] = (a * _sigmoid(a) * v).astype(BF16)
            return carry

        lax.fori_loop(0, nchunks, step, 0)

    return pl.pallas_call(
        body, name="ffn_act_fwd", grid=(N_UP_PAIRS, nbatch),
        in_specs=[pl.BlockSpec((2, None, seq, UP_CHUNK), lambda i, b: (0, i, b, 0)),
                  pl.BlockSpec((2, None, 3, UP_CHUNK), lambda i, b: (0, i, 0, 0))],
        out_specs=pl.BlockSpec((None, seq, UP_CHUNK), lambda i, b: (i, b, 0)),
        out_shape=jax.ShapeDtypeStruct((N_UP_PAIRS, t, UP_CHUNK), BF16),
        compiler_params=_params(("parallel", "parallel")),
    )(pre, fcw)


def _ffn_down(pre, fcw, wdown, res, g, seq, name):
    t, d = res.shape
    tiles_per_seq = seq // FFN_ROWS

    def body(main_ref, halo_ref, w_ref, wd_ref, r_ref, *rest):
        if g is None:
            x_ref, act_ref = rest
        else:
            g_ref, x_ref, h_ref, act_ref = rest
        inside = ((pl.program_id(0) % tiles_per_seq) > 0).astype(F32)

        def window(part, p):
            before = halo_ref[part, p].astype(F32)[PACKED_ROWS - HALO:] * inside
            return jnp.concatenate([before, main_ref[part, p].astype(F32)], axis=0)

        total = r_ref[...]
        for p in range(N_UP_PAIRS):
            a = _conv(window(0, p), w_ref[0, p])[HALO:]
            v = _conv(window(1, p), w_ref[1, p])[HALO:]
            act = (a * _sigmoid(a) * v).astype(BF16)
            act_ref[p] = act
            total = total + jnp.dot(act, wd_ref[p], preferred_element_type=F32)
        x_ref[...] = total
        if g is not None:
            h_ref[...] = (total * lax.rsqrt(jnp.mean(total * total, axis=-1, keepdims=True) + EPS) * g_ref[...]).astype(BF16)

    row = pl.BlockSpec((FFN_ROWS, d), lambda i: (i, 0))
    tiles_per_halo = FFN_ROWS // PACKED_ROWS
    in_specs = [
        pl.BlockSpec((2, N_UP_PAIRS, FFN_ROWS, UP_CHUNK), lambda i: (0, 0, i, 0)),
        pl.BlockSpec((2, N_UP_PAIRS, PACKED_ROWS, UP_CHUNK), lambda i: (0, 0, jnp.maximum(i * tiles_per_halo - 1, 0), 0)),
        pl.BlockSpec((2, N_UP_PAIRS, 3, UP_CHUNK), lambda i: (0, 0, 0, 0)),
        pl.BlockSpec((N_UP_PAIRS, UP_CHUNK, d), lambda i: (0, 0, 0)), row]
    out_specs = [row]
    out_shape = [jax.ShapeDtypeStruct((t, d), F32)]
    args = [pre, pre, fcw, wdown, res]
    if g is not None:
        in_specs.append(pl.BlockSpec((1, d), lambda i: (0, 0)))
        out_specs.append(row)
        out_shape.append(jax.ShapeDtypeStruct((t, d), BF16))
        args.append(g)
    out_specs.append(pl.BlockSpec((N_UP_PAIRS, FFN_ROWS, UP_CHUNK), lambda i: (0, i, 0)))
    out_shape.append(jax.ShapeDtypeStruct((N_UP_PAIRS, t, UP_CHUNK), BF16))
    return pl.pallas_call(
        body, name=name, grid=(t // FFN_ROWS,), in_specs=in_specs, out_specs=out_specs, out_shape=out_shape,
        compiler_params=_params(("parallel",)),
    )(*args)


def _ffn_act_bwd(pre, d_act, fcw, nbatch, seq):
    nchunks = seq // FFN_ROWS

    def body(pre_ref, da_ref, w_ref, dpre_ref, dw_ref):
        b = pl.program_id(1)
        wa = w_ref[0]
        wc = w_ref[1]
        cur = slice(HALO, HALO + FFN_ROWS)

        def step(c, carry):
            rows = pl.ds(pl.multiple_of(c * FFN_ROWS, FFN_ROWS), FFN_ROWS)
            pg = _window(pre_ref.at[0], c, FFN_ROWS, nchunks, True)
            pv = _window(pre_ref.at[1], c, FFN_ROWS, nchunks, True)
            dact = _window(da_ref, c, FFN_ROWS, nchunks, True)
            a = _conv(pg, wa)
            v = _conv(pv, wc)
            sg = _sigmoid(a)
            asg = a * sg
            dzg, dwg = _conv_bwd(dact * v * (sg + asg - asg * sg), pg, wa, cur)
            dzv, dwv = _conv_bwd(dact * asg, pv, wc, cur)
            dpre_ref[0, rows, :] = dzg[cur].astype(BF16)
            dpre_ref[1, rows, :] = dzv[cur].astype(BF16)
            return tuple(acc + d for acc, d in zip(carry, dwg + dwv))

        zero = jnp.zeros((1, UP_CHUNK), F32)
        sums = lax.fori_loop(0, nchunks, step, (zero,) * 6)

        @pl.when(b == 0)
        def _():
            for i in range(6):
                dw_ref[i // 3, pl.ds(i % 3, 1), :] = sums[i]

        @pl.when(b > 0)
        def _():
            for i in range(6):
                dw_ref[i // 3, pl.ds(i % 3, 1), :] += sums[i]

    pair = pl.BlockSpec((2, None, seq, UP_CHUNK), lambda i, b: (0, i, b, 0))
    wspec = pl.BlockSpec((2, None, 3, UP_CHUNK), lambda i, b: (0, i, 0, 0))
    return pl.pallas_call(
        body, name="ffn_act_bwd", grid=(N_UP_PAIRS, nbatch),
        in_specs=[pair, pl.BlockSpec((None, seq, UP_CHUNK), lambda i, b: (i, b, 0)), wspec],
        out_specs=[pair, wspec],
        out_shape=[jax.ShapeDtypeStruct(pre.shape, BF16), jax.ShapeDtypeStruct(fcw.shape, F32)],
        compiler_params=_params(("parallel", "arbitrary")),
    )(pre, d_act, fcw)


def _ffn_up_bwd(pre, d_act, fcw, wup, x, g, dres, seq, name):
    t, d = x.shape
    tiles_per_seq = seq // FFN_ROWS
    tiles_per_halo = FFN_ROWS // PACKED_ROWS
    last_halo = t // PACKED_ROWS - 1

    def body(pm_ref, pp_ref, pn_ref, dm_ref, dp_ref, dn_ref, w_ref, wu_ref, x_ref, g_ref, dres_ref,
             dpre_ref, dw_ref, dx_ref, dxb_ref, dg_ref):
        i = pl.program_id(0)
        has_prev = ((i % tiles_per_seq) > 0).astype(F32)
        has_next = ((i % tiles_per_seq) < tiles_per_seq - 1).astype(F32)
        cur = slice(HALO, HALO + FFN_ROWS)

        def window(before, main, after):
            return jnp.concatenate([before.astype(F32)[PACKED_ROWS - HALO:] * has_prev, main.astype(F32),
                                    after.astype(F32)[:HALO] * has_next], axis=0)

        dh = jnp.zeros((FFN_ROWS, d), F32)
        sums = []
        for p in range(N_UP_PAIRS):
            pg = window(pp_ref[0, p], pm_ref[0, p], pn_ref[0, p])
            pv = window(pp_ref[1, p], pm_ref[1, p], pn_ref[1, p])
            dact = window(dp_ref[p], dm_ref[p], dn_ref[p])
            a = _conv(pg, w_ref[0, p])
            v = _conv(pv, w_ref[1, p])
            sg = _sigmoid(a)
            asg = a * sg
            dzg, dwg = _conv_bwd(dact * v * (sg + asg - asg * sg), pg, w_ref[0, p], cur)
            dzv, dwv = _conv_bwd(dact * asg, pv, w_ref[1, p], cur)
            dgate = dzg[cur].astype(BF16)
            dval = dzv[cur].astype(BF16)
            dpre_ref[0, p] = dgate
            dpre_ref[1, p] = dval
            dh = dh + jnp.dot(dgate, wu_ref[p], preferred_element_type=F32)
            dh = dh + jnp.dot(dval, wu_ref[N_UP_PAIRS + p], preferred_element_type=F32)
            sums.append(dwg + dwv)

        xv = x_ref[...]
        r = lax.rsqrt(jnp.mean(xv * xv, axis=-1, keepdims=True) + EPS)
        xh = xv * r
        gd = dh * g_ref[...]
        dx = r * (gd - xh * jnp.mean(gd * xh, axis=-1, keepdims=True)) + dres_ref[...]
        dx_ref[...] = dx
        dxb_ref[...] = dx.astype(BF16)
        part = jnp.sum(dh * xh, axis=0, keepdims=True)

        @pl.when(i == 0)
        def _():
            dg_ref[...] = part
            for p in range(N_UP_PAIRS):
                for k in range(6):
                    dw_ref[k // 3, p, pl.ds(k % 3, 1), :] = sums[p][k]

        @pl.when(i > 0)
        def _():
            dg_ref[...] += part
            for p in range(N_UP_PAIRS):
                for k in range(6):
                    dw_ref[k // 3, p, pl.ds(k % 3, 1), :] += sums[p][k]

    def rows4(n):
        return lambda fn: pl.BlockSpec((2, N_UP_PAIRS, n, UP_CHUNK), lambda i: (0, 0, fn(i), 0))

    def rows3(n):
        return lambda fn: pl.BlockSpec((N_UP_PAIRS, n, UP_CHUNK), lambda i: (0, fn(i), 0))

    prev_tile = lambda i: jnp.maximum(i * tiles_per_halo - 1, 0)
    next_tile = lambda i: jnp.minimum((i + 1) * tiles_per_halo, last_halo)
    row = pl.BlockSpec((FFN_ROWS, d), lambda i: (i, 0))
    vec = pl.BlockSpec((1, d), lambda i: (0, 0))
    wspec = pl.BlockSpec((2, N_UP_PAIRS, 3, UP_CHUNK), lambda i: (0, 0, 0, 0))
    return pl.pallas_call(
        body, name=name, grid=(t // FFN_ROWS,),
        in_specs=[rows4(FFN_ROWS)(lambda i: i), rows4(PACKED_ROWS)(prev_tile), rows4(PACKED_ROWS)(next_tile),
                  rows3(FFN_ROWS)(lambda i: i), rows3(PACKED_ROWS)(prev_tile), rows3(PACKED_ROWS)(next_tile),
                  wspec, pl.BlockSpec((N_DEV, UP_CHUNK, d), lambda i: (0, 0, 0)), row, vec, row],
        out_specs=[rows4(FFN_ROWS)(lambda i: i), wspec, row, row, vec],
        out_shape=[jax.ShapeDtypeStruct(pre.shape, BF16), jax.ShapeDtypeStruct(fcw.shape, F32),
                   jax.ShapeDtypeStruct((t, d), F32), jax.ShapeDtypeStruct((t, d), BF16),
                   jax.ShapeDtypeStruct((1, d), F32)],
        compiler_params=_params(("arbitrary",)),
    )(pre, pre, pre, d_act, d_act, d_act, fcw, wup, x, g, dres)


def _adamw(lands, w, m, v, row_tile, name, after=()):
    nl = len(lands)
    _, nr, ncol = lands[0].shape
    c1 = 1.0 - ADAM_B1 ** ADAM_STEP
    c2 = 1.0 - ADAM_B2 ** ADAM_STEP

    def body(*refs):
        land_refs = refs[:nl]
        w_ref, m_ref, v_ref = refs[nl:nl + 3]
        g_ref, d_ref, mo_ref, vo_ref = refs[nl + 3 + len(after):]
        for l in range(nl):
            @pl.when(pl.program_id(0) == l)
            def _(l=l):
                g = land_refs[l][0].astype(F32)
                for j in range(1, N_DEV):
                    g = g + land_refs[l][j].astype(F32)
                g_ref[...] = g

        g = g_ref[...]
        m2 = ADAM_B1 * m_ref[...] + (1.0 - ADAM_B1) * g
        v2 = ADAM_B2 * v_ref[...] + (1.0 - ADAM_B2) * (g * g)
        mo_ref[...] = m2
        vo_ref[...] = v2
        d_ref[...] = -ADAM_LR * ((m2 / c1) / (jnp.sqrt(v2 / c2) + ADAM_EPS) + ADAM_WD * w_ref[...])

    def land_spec(l):
        return pl.BlockSpec((N_DEV, row_tile, ncol), lambda k, i: (0, jnp.where(k == l, i, 0), 0))

    tile = pl.BlockSpec((None, row_tile, ncol), lambda k, i: (k, i, 0))
    return pl.pallas_call(
        body, name=name, grid=(nl, nr // row_tile),
        in_specs=[land_spec(l) for l in range(nl)] + [tile, tile, tile] + [pl.BlockSpec(memory_space=pl.ANY)] * len(after),
        out_specs=[tile] * 4,
        out_shape=[jax.ShapeDtypeStruct(w.shape, F32)] * 4,
        compiler_params=_params(("arbitrary", "arbitrary")),
    )(*lands, w, m, v, *after)


class _Item:
    def __init__(self, src, chunked, land_cols=False):
        self.src, self.chunked, self.land_cols = src, chunked, land_cols
        if chunked == "cols":
            block = (src.shape[0], src.shape[1] // N_DEV)
        else:
            block = src.shape[1:] if chunked else src.shape
        self.width = block[-1]
        self.land_shape = (block[0], N_DEV * block[1]) if land_cols else (N_DEV,) + block

    def _cols(self, first, count=1):
        return pl.ds(pl.multiple_of(first * self.width, LANES), count * self.width)

    def part(self, src_ref, j):
        if self.chunked == "cols":
            return src_ref.at[:, self._cols(j)]
        return src_ref.at[j] if self.chunked else src_ref

    def slot(self, land_ref, s):
        return land_ref.at[:, self._cols(s)] if self.land_cols else land_ref.at[s]

    def seven(self, land_ref):
        return land_ref.at[:, self._cols(0, N_DEV - 1)] if self.land_cols else land_ref.at[pl.ds(0, N_DEV - 1)]


def _mesh_place():
    x, y, c = lax.axis_index("x"), lax.axis_index("y"), lax.axis_index("c")
    return x, y, c, 4 * x + 2 * y + c


def _flipped(x, y, c, k):
    px = 1 - x if k & 4 else x
    py = 1 - y if k & 2 else y
    pc = 1 - c if k & 1 else c
    return (px, py, pc), 4 * px + 2 * py + pc


PEER_ORDER = (2, 4, 6, 3, 5, 7, 1)


def _exchange(items, name):
    n = len(items)

    def body(*refs):
        srcs, lands = refs[:n], refs[n:2 * n]
        send, recv, local = refs[2 * n:]
        x, y, c, me = _mesh_place()

        def copy(i, k, chunk, slot, dev):
            return pltpu.make_async_remote_copy(
                src_ref=items[i].part(srcs[i], chunk), dst_ref=items[i].slot(lands[i], slot),
                send_sem=send.at[i, k - 1], recv_sem=recv.at[i, k - 1], device_id=dev, device_id_type=MESH)

        own = [pltpu.make_async_copy(items[i].part(srcs[i], me), items[i].slot(lands[i], me), local.at[i])
               for i in range(n)]
        for k in PEER_ORDER:
            dev, idx = _flipped(x, y, c, k)
            for i in range(n):
                copy(i, k, idx, me, dev).start()
        for cp in own:
            cp.start()
        for k in PEER_ORDER:
            dev, idx = _flipped(x, y, c, k)
            for i in range(n):
                copy(i, k, me, idx, dev).wait_recv()
        for k in PEER_ORDER:
            dev, idx = _flipped(x, y, c, k)
            for i in range(n):
                copy(i, k, idx, me, dev).wait_send()
        for cp in own:
            cp.wait()

    hbm = pl.BlockSpec(memory_space=pl.ANY)
    return pl.pallas_call(
        body, name=name,
        in_specs=[hbm] * n, out_specs=[hbm] * n,
        out_shape=[jax.ShapeDtypeStruct(it.land_shape, it.src.dtype) for it in items],
        scratch_shapes=[pltpu.SemaphoreType.DMA((n, N_DEV - 1)), pltpu.SemaphoreType.DMA((n, N_DEV - 1)),
                        pltpu.SemaphoreType.DMA((n,))],
        compiler_params=pltpu.CompilerParams(has_side_effects=True),
    )(*[it.src for it in items])


def _sequencer_exchange(items, name, collective_id):
    n = len(items)

    def body(*refs):
        srcs, lands = refs[:n], refs[n:2 * n]
        send, recv, local = refs[2 * n:]
        x, y, c, me = _mesh_place()
        barrier = pltpu.get_barrier_semaphore()
        for k in PEER_ORDER:
            pl.semaphore_signal(barrier, inc=1, device_id=_flipped(x, y, c, k)[0], device_id_type=MESH)
        pl.semaphore_wait(barrier, N_DEV - 1)

        def copy(i, k, chunk, slot, dev):
            return pltpu.make_async_remote_copy(
                src_ref=items[i].part(srcs[i], chunk), dst_ref=items[i].slot(lands[i], slot),
                send_sem=send.at[i, k - 1], recv_sem=recv.at[i, k - 1], device_id=dev, device_id_type=MESH)

        own = [pltpu.make_async_copy(items[i].part(srcs[i], me), items[i].slot(lands[i], me), local.at[i])
               for i in range(n)]
        for cp in own:
            cp.start()
        for k in PEER_ORDER:
            dev, idx = _flipped(x, y, c, k)
            for i in range(n):
                copy(i, k, idx, me, dev).start()
        for k in PEER_ORDER:
            dev, idx = _flipped(x, y, c, k)
            for i in range(n):
                copy(i, k, me, idx, dev).wait_recv()
        for k in PEER_ORDER:
            dev, idx = _flipped(x, y, c, k)
            for i in range(n):
                copy(i, k, idx, me, dev).wait_send()
        for cp in own:
            cp.wait()

    return pl.kernel(
        body, name=name,
        out_type=[jax.ShapeDtypeStruct(it.land_shape, it.src.dtype) for it in items],
        mesh=plsc.ScalarSubcoreMesh(axis_name="sequencer", num_cores=1),
        scratch_types=[pltpu.SemaphoreType.DMA((n, N_DEV - 1)), pltpu.SemaphoreType.DMA((n, N_DEV - 1)),
                       pltpu.SemaphoreType.DMA((n,))],
        compiler_params=pltpu.CompilerParams(collective_id=collective_id),
    )(*[it.src for it in items])


HBM_SPEC = pl.BlockSpec(memory_space=pltpu.HBM)
SEM_SPEC = pl.BlockSpec(memory_space=pltpu.SEMAPHORE)
DATAFLOW = pltpu.SideEffectType.DATAFLOW_SIDE_EFFECTING


def _exchange_start(items, name, after=()):
    n = len(items)
    na = len(after)

    def body(*refs):
        srcs, land_ins = refs[:n], refs[n:2 * n]
        outs = refs[2 * n + na:6 * n + na]
        (local,) = refs[6 * n + na:]
        del land_ins
        x, y, c, me = _mesh_place()
        own = [pltpu.make_async_copy(items[i].part(srcs[i], me), items[i].slot(outs[4 * i + 3], me), local.at[i])
               for i in range(n)]
        for cp in own:
            cp.start()
        for cp in own:
            cp.wait()
        for k in PEER_ORDER:
            dev, idx = _flipped(x, y, c, k)
            for i in range(n):
                send, recv, _, land = outs[4 * i:4 * i + 4]
                pltpu.make_async_remote_copy(
                    src_ref=items[i].part(srcs[i], idx), dst_ref=items[i].slot(land, me), send_sem=send, recv_sem=recv,
                    device_id=dev, device_id_type=MESH).start()

    out_shape, out_specs, args, lands = [], [], [], []
    for it in items:
        out_shape += [pltpu.SemaphoreType.DMA(()), pltpu.SemaphoreType.DMA(()),
                      pltpu.HBM(it.src.shape, it.src.dtype), pltpu.HBM(it.land_shape, it.src.dtype)]
        out_specs += [SEM_SPEC, SEM_SPEC, HBM_SPEC, HBM_SPEC]
        args.append(pltpu.with_memory_space_constraint(it.src, pltpu.HBM))
        lands.append(pltpu.with_memory_space_constraint(lax.empty(it.land_shape, it.src.dtype), pltpu.HBM))
    outs = pl.pallas_call(
        body, name=name,
        in_specs=[HBM_SPEC] * (2 * n) + [pl.BlockSpec(memory_space=pl.ANY)] * na,
        out_specs=out_specs, out_shape=out_shape,
        scratch_shapes=[pltpu.SemaphoreType.DMA((n,))],
        input_output_aliases={**{i: 4 * i + 2 for i in range(n)}, **{n + i: 4 * i + 3 for i in range(n)}},
        compiler_params=pltpu.CompilerParams(has_side_effects=DATAFLOW),
    )(*args, *lands, *after)
    return [tuple(outs[4 * i:4 * i + 4]) + (items[i],) for i in range(n)]


def _started(handles):
    return handles[0][2]


def _exchange_wait(handles, after, name):
    n = len(handles)

    def body(*refs):
        x, y, c, _ = _mesh_place()
        for i in range(n):
            src, land, send, recv = refs[4 * i:4 * i + 4]
            del src
            seven = handles[i][4].seven(land)
            cp = pltpu.make_async_remote_copy(src_ref=seven, dst_ref=seven, send_sem=send, recv_sem=recv,
                                              device_id=(x, y, 1 - c), device_id_type=MESH)
            cp.wait_send()
            cp.wait_recv()

    args, in_specs, out_shape = [], [], []
    for send, recv, src, land, _ in handles:
        args += [src, land, send, recv]
        in_specs += [HBM_SPEC, HBM_SPEC, SEM_SPEC, SEM_SPEC]
        out_shape += [pltpu.HBM(src.shape, src.dtype), pltpu.HBM(land.shape, land.dtype)]
    outs = pl.pallas_call(
        body, name=name,
        in_specs=in_specs + [pl.BlockSpec(memory_space=pl.ANY)] * len(after), out_specs=[HBM_SPEC] * (2 * n),
        out_shape=out_shape,
        input_output_aliases={**{4 * i: 2 * i for i in range(n)}, **{4 * i + 1: 2 * i + 1 for i in range(n)}},
        compiler_params=pltpu.CompilerParams(has_side_effects=DATAFLOW),
    )(*args, *after)
    return [outs[2 * i + 1] for i in range(n)]


TM = 1024
TM_ACC = 512
TN_IN = 768


def kernel(x, norm1_g, w_in, mix_conv_w, attn_out_g, conv_out_g, w_out, norm2_g, ffn_up, ffn_conv_w, ffn_down, final_norm_g, loss_target, m_norm1_g, m_w_in, m_mix_conv_w, m_attn_out_g, m_conv_out_g, m_w_out, m_norm2_g, m_ffn_up, m_ffn_conv_w, m_ffn_down, m_final_norm_g, v_norm1_g, v_w_in, v_mix_conv_w, v_attn_out_g, v_conv_out_g, v_w_out, v_norm2_g, v_ffn_up, v_ffn_conv_w, v_ffn_down, v_final_norm_g):
    nbatch, seq, d = x.shape
    t = nbatch * seq
    nt, nta = t // TM, t // TM_ACC
    out_rows = D_MODEL // N_DEV
    down_rows = D_FF // N_DEV
    xf = x.reshape(t, d)
    target = loss_target.reshape(t, d)

    cw_local = jnp.concatenate([ffn_conv_w, mix_conv_w], axis=-1)
    cast = lambda w: _Item(w.astype(BF16), False)
    cast_in = lambda w: _Item(w.astype(BF16), False, land_cols=True)
    cw_all, win0 = _sequencer_exchange([_Item(cw_local, False), cast_in(w_in[0])], "gather_a", 0)
    up_t, m_up_t, v_up_t = (jnp.swapaxes(a, 1, 2) for a in (ffn_up, m_ffn_up, v_ffn_up))
    wout0, wup0 = _sequencer_exchange([cast(w_out[0]), cast(up_t[0])], "gather_b", 1)
    (wdown0,) = _sequencer_exchange([cast(ffn_down[0])], "gather_c", 2)
    win1, wout1 = _sequencer_exchange([cast_in(w_in[1]), cast(w_out[1])], "gather_d", 3)
    wup1, wdown1 = _sequencer_exchange([cast(up_t[1]), cast(ffn_down[1])], "gather_e", 7)
    win, wup = [win0, win1], [wup0, wup1]
    wout = [w.reshape(D_MODEL, D_MODEL) for w in (wout0, wout1)]
    wdown = [w.reshape(N_UP_PAIRS, UP_CHUNK, D_MODEL) for w in (wdown0, wdown1)]
    fcw = [cw_all[:, k, :, :UP_CHUNK].reshape(2, N_UP_PAIRS, 3, UP_CHUNK) for k in range(DEPTH)]
    mcw = [cw_all[:, k, :, UP_CHUNK:].transpose(1, 0, 2).reshape(3, D_CONV) for k in range(DEPTH)]

    full = lambda i, j, k: (0, 0)

    saved = []
    xin = xf
    h1 = _rms_fwd(xin, norm1_g[0][None], "rms1_fwd_0")
    rows_of = lambda width: pl.BlockSpec((TM_ACC, width), lambda i: (i, 0))
    whole = lambda *shape: pl.BlockSpec(shape, lambda i: (0,) * len(shape))
    chunks_of = lambda n: pl.BlockSpec((n, TM_ACC, UP_CHUNK), lambda i: (0, i, 0))
    for l in range(DEPTH):
        proj = _matmul(
            h1, win[l], grid=(nt, D_IN // TN_IN, 1), dims=NN, name=f"proj_{l}",
            a_spec=pl.BlockSpec((TM, D_MODEL), lambda i, j, k: (i, 0)),
            b_spec=pl.BlockSpec((D_MODEL, TN_IN), lambda i, j, k: (0, j)),
            o_spec=pl.BlockSpec((TM, TN_IN), lambda i, j, k: (i, j)), o_shape=(t, D_IN), o_dtype=F32)
        o, lse, cat = _attn_fwd(proj, attn_out_g[l][None], nbatch, seq)
        cat = _convmix_fwd(proj, cat, mcw[l], conv_out_g[l][None], nbatch, seq)
        xmid, h2 = _matmul_norm(cat, wout[l], xin, norm2_g[l][None], dims=NN, name=f"mix_out_{l}",
                                a_spec=rows_of(D_MODEL), b_spec=whole(D_MODEL, D_MODEL))
        pre = _matmul(
            h2, wup[l], grid=(nt, N_DEV, 1), dims=NT, name=f"ffn_up_{l}",
            a_spec=pl.BlockSpec((TM, D_MODEL), lambda i, j, k: (i, 0)),
            b_spec=pl.BlockSpec((None, UP_CHUNK, D_MODEL), lambda i, j, k: (j, 0, 0)),
            o_spec=pl.BlockSpec((None, TM, UP_CHUNK), lambda i, j, k: (j, i, 0)),
            o_shape=(N_DEV, t, UP_CHUNK), o_dtype=BF16).reshape(2, N_UP_PAIRS, t, UP_CHUNK)
        if l + 1 < DEPTH:
            xout, h_next, act = _ffn_down(pre, fcw[l], wdown[l], xmid, norm1_g[l + 1][None], seq, f"ffn_down_{l}")
        else:
            h_next = None
            xout, act = _ffn_down(pre, fcw[l], wdown[l], xmid, None, seq, f"ffn_down_{l}")
        saved.append((xin, h1, proj, o, lse, cat, xmid, h2, pre, act))
        xin, h1 = xout, h_next

    loss_part, dx, dxb, dgf = _loss_head(xin, final_norm_g[None], target, "loss_head")

    dg1, dg2, dga, dgc = [None] * DEPTH, [None] * DEPTH, [None] * DEPTH, [None] * DEPTH
    for l in reversed(range(DEPTH)):
        xin, h1, proj, o, lse, cat, xmid, h2, pre, act = saved[l]
        d_act = _matmul(
            dxb, wdown[l], grid=(nt, N_UP_PAIRS, 1), dims=NT, name=f"d_act_{l}",
            a_spec=pl.BlockSpec((TM, D_MODEL), lambda i, j, k: (i, 0)),
            b_spec=pl.BlockSpec((None, UP_CHUNK, D_MODEL), lambda i, j, k: (j, 0, 0)),
            o_spec=pl.BlockSpec((None, TM, UP_CHUNK), lambda i, j, k: (j, i, 0)),
            o_shape=(N_UP_PAIRS, t, UP_CHUNK), o_dtype=BF16)
        g_down = _matmul(
            act, dxb, grid=(N_UP_PAIRS, 1, 1), dims=TN, name=f"g_down_{l}",
            a_spec=pl.BlockSpec((None, t, UP_CHUNK), lambda i, j, k: (i, 0, 0)),
            b_spec=pl.BlockSpec((t, D_MODEL), full),
            o_spec=pl.BlockSpec((None, UP_CHUNK, D_MODEL), lambda i, j, k: (i, 0, 0)),
            o_shape=(N_UP_PAIRS, UP_CHUNK, D_MODEL), o_dtype=BF16).reshape(N_DEV, down_rows, D_MODEL)
        d_pre, d_fcw, dxm, dxmb, dg2[l] = _ffn_up_bwd(
            pre, d_act, fcw[l], wup[l], xmid, norm2_g[l][None], dx, seq, f"d_h2_{l}")
        d_pre = d_pre.reshape(N_DEV, t, UP_CHUNK)
        g_up = _matmul(
            d_pre, h2, grid=(N_DEV, 1, 1), dims=TN, name=f"g_up_{l}",
            a_spec=pl.BlockSpec((None, t, UP_CHUNK), lambda i, j, k: (i, 0, 0)),
            b_spec=pl.BlockSpec((t, D_MODEL), full),
            o_spec=pl.BlockSpec((None, UP_CHUNK, D_MODEL), lambda i, j, k: (i, 0, 0)),
            o_shape=(N_DEV, UP_CHUNK, D_MODEL), o_dtype=BF16)
        g_out = _matmul(
            cat, dxmb, grid=(1, 1, nt), dims=TN, name=f"g_out_{l}",
            a_spec=pl.BlockSpec((TM, D_MODEL), lambda i, j, k: (k, 0)),
            b_spec=pl.BlockSpec((TM, D_MODEL), lambda i, j, k: (k, 0)),
            o_spec=pl.BlockSpec((D_MODEL, D_MODEL), full),
            o_shape=(D_MODEL, D_MODEL), o_dtype=BF16).reshape(N_DEV, out_rows, D_MODEL)
        if l == 0:
            land_out0, land_up0, land_down0 = _sequencer_exchange(
                [_Item(g_out, True), _Item(g_up, True), _Item(g_down, True)], "scatter_0a", 5)
        d_cat = _matmul(
            dxmb, wout[l], grid=(nta, 1, 1), dims=NT, name=f"d_cat_{l}",
            a_spec=pl.BlockSpec((TM_ACC, D_MODEL), lambda i, j, k: (i, 0)),
            b_spec=pl.BlockSpec((D_MODEL, D_MODEL), full),
            o_spec=pl.BlockSpec((TM_ACC, D_MODEL), lambda i, j, k: (i, 0)), o_shape=(t, D_MODEL), o_dtype=BF16)
        d_proj, dga[l] = _attn_bwd(proj, o, lse, d_cat, attn_out_g[l][None], nbatch, seq)
        d_proj, d_mcw, dgc[l] = _convmix_bwd(proj, d_cat, d_proj, mcw[l], conv_out_g[l][None], nbatch, seq)
        g_in = _matmul(
            h1, d_proj, grid=(1, D_IN // TN_IN, 1), dims=TN, name=f"g_in_{l}",
            a_spec=pl.BlockSpec((t, D_MODEL), full),
            b_spec=pl.BlockSpec((t, TN_IN), lambda i, j, k: (0, j)),
            o_spec=pl.BlockSpec((D_MODEL, TN_IN), lambda i, j, k: (0, j)),
            o_shape=(D_MODEL, D_IN), o_dtype=BF16)
        g_cw = jnp.concatenate(
            [d_fcw.reshape(N_DEV, 3, UP_CHUNK), d_mcw.reshape(3, N_DEV, D_CONV // N_DEV).transpose(1, 0, 2)], axis=-1)
        if l == 0:
            land_in0, land_cw0 = _sequencer_exchange([_Item(g_in, "cols"), _Item(g_cw, True)], "scatter_0b", 6)
        else:
            land_in1, land_out1, land_up1, land_down1, land_cw1 = _sequencer_exchange(
                [_Item(g_in, "cols"), _Item(g_out, True), _Item(g_up, True), _Item(g_down, True), _Item(g_cw, True)],
                "scatter_1", 4)
        dx, dxb, dg1[l] = _matmul_norm_bwd(
            d_proj, win[l], xin, norm1_g[l][None], dxm, dims=NT, name=f"d_h1_{l}",
            a_spec=rows_of(D_IN), b_spec=whole(D_MODEL, D_IN))

    def pack_small(n1, a, c, n2, f):
        return jnp.concatenate(
            [n1, n2, f[None], jnp.concatenate([a, c], axis=-1), jnp.zeros((1, D_MODEL), F32)], axis=0)[None]

    small = jnp.concatenate(
        [dg1[0], dg1[1], dg2[0], dg2[1], dgf,
         jnp.concatenate([dga[0], dgc[0]], axis=-1), jnp.concatenate([dga[1], dgc[1]], axis=-1),
         jnp.pad(loss_part, ((0, 0), (0, D_MODEL - LANES)))], axis=0)
    (land_small,) = _exchange([_Item(small, False)], "gather_gain_grads")
    res_small = _adamw(
        [land_small], pack_small(norm1_g, attn_out_g, conv_out_g, norm2_g, final_norm_g),
        pack_small(m_norm1_g, m_attn_out_g, m_conv_out_g, m_norm2_g, m_final_norm_g),
        pack_small(v_norm1_g, v_attn_out_g, v_conv_out_g, v_norm2_g, v_final_norm_g), SUBLANES, "adamw_gains")
    res_out = _adamw([land_out0, land_out1], w_out, m_w_out, v_w_out, out_rows, "adamw_w_out", after=[res_small[0]])
    res_up_t = _adamw([land_up0, land_up1], up_t, m_up_t, v_up_t, UP_CHUNK // 4, "adamw_ffn_up", after=[res_out[0]])
    res_up = [jnp.swapaxes(r, 1, 2) for r in res_up_t]
    res_down = _adamw([land_down0, land_down1], ffn_down, m_ffn_down, v_ffn_down, down_rows, "adamw_ffn_down",
                      after=[res_up_t[0]])
    res_in = _adamw([land_in0, land_in1], w_in, m_w_in, v_w_in, 256, "adamw_w_in", after=[res_down[0]])
    res_cw = _adamw(
        [land_cw0, land_cw1], cw_local, jnp.concatenate([m_ffn_conv_w, m_mix_conv_w], axis=-1),
        jnp.concatenate([v_ffn_conv_w, v_mix_conv_w], axis=-1), 3, "adamw_conv_w", after=[res_in[0]])

    loss = res_small[0][0, SUBLANES - 1, 0]

    def unpack(kind):
        s = res_small[kind][0]
        cwr = res_cw[kind]
        return (s[0:2], res_in[kind], cwr[..., UP_CHUNK:], s[5:7, :D_ATTN], s[5:7, D_ATTN:], res_out[kind],
                s[2:4], res_up[kind], cwr[..., :UP_CHUNK], res_down[kind], s[4])

    return (loss, dx.reshape(nbatch, seq, d), *unpack(0), *unpack(1), *unpack(2), *unpack(3))
```

```python
import math

import jax
import jax.numpy as jnp
from jax import lax
from jax.experimental import pallas as pl
from jax.experimental.pallas import tpu as pltpu
from jax.experimental.pallas import tpu_sc as plsc

F32 = jnp.float32
BF16 = jnp.bfloat16

D_MODEL = 1024
D_ATTN = 512
D_CONV = 512
HEAD_DIM = 64
N_HEADS = 8
D_FF = 2816
DEPTH = 2
D_IN = 3 * D_ATTN + 3 * D_CONV
EPS = 1e-6
DILATIONS = (1, 4, 16)
BAND = 128
N_DEV = 8
IN_CHUNK = D_IN // N_DEV
UP_CHUNK = 2 * D_FF // N_DEV
N_UP_PAIRS = N_DEV // 2
CW_PACK = UP_CHUNK + D_CONV // N_DEV
ADAM_LR = 0.001
ADAM_B1 = 0.9
ADAM_B2 = 0.999
ADAM_EPS = 1e-08
ADAM_WD = 0.01
ADAM_STEP = 10
LANES = 128
SUBLANES = 8
VMEM_LIMIT = 56 * 1024 * 1024

NEG = -1e30
MESH = pl.DeviceIdType.MESH


def _params(sem=None, vmem=VMEM_LIMIT):
    return pltpu.CompilerParams(dimension_semantics=sem, vmem_limit_bytes=vmem)


NN = (((1,), (0,)), ((), ()))
NT = (((1,), (1,)), ((), ()))
TN = (((0,), (0,)), ((), ()))


def _contract(a_ref, b_ref, dims):
    def dot(av, bv):
        return lax.dot_general(av.astype(BF16), bv.astype(BF16), dims, preferred_element_type=F32)

    if len(a_ref.shape) == 2:
        return dot(a_ref[...], b_ref[...])
    part = dot(a_ref[0], b_ref[0])
    for c in range(1, a_ref.shape[0]):
        part = part + dot(a_ref[c], b_ref[c])
    return part


def _matmul(a, b, *, grid, a_spec, b_spec, o_spec, o_shape, o_dtype, dims, name, res=None, res_spec=None, after=()):
    nk = grid[2]
    o_block = tuple(s for s in o_spec.block_shape if s is not None)
    na = len(after)

    def body(*refs):
        refs = refs[:2 + (res is not None)] + refs[2 + (res is not None) + na:]
        if res is None:
            a_ref, b_ref, o_ref, *scr = refs
            r_ref = None
        else:
            a_ref, b_ref, r_ref, o_ref, *scr = refs
        part = _contract(a_ref, b_ref, dims)

        def finish(total):
            if r_ref is not None:
                total = total + r_ref[...]
            o_ref[...] = total.astype(o_dtype)

        if nk == 1:
            finish(part)
        else:
            acc = scr[0]
            k = pl.program_id(2)

            @pl.when(k == 0)
            def _():
                acc[...] = part

            @pl.when(k > 0)
            def _():
                acc[...] += part

            @pl.when(k == nk - 1)
            def _():
                finish(acc[...])

    in_specs = [a_spec, b_spec] + ([res_spec] if res is not None else []) + [pl.BlockSpec(memory_space=pl.ANY)] * na
    args = (a, b) + ((res,) if res is not None else ()) + tuple(after)
    return pl.pallas_call(
        body, name=name, grid=grid, in_specs=in_specs, out_specs=o_spec,
        out_shape=jax.ShapeDtypeStruct(o_shape, o_dtype),
        scratch_shapes=[pltpu.VMEM(o_block, F32)] if nk > 1 else [],
        compiler_params=_params(("parallel", "parallel", "arbitrary")),
    )(*args)


ROW_TILE = 512


def _rms_fwd(x, g, name):
    t, d = x.shape

    def body(x_ref, g_ref, h_ref):
        xv = x_ref[...]
        r = lax.rsqrt(jnp.mean(xv * xv, axis=-1, keepdims=True) + EPS)
        h_ref[...] = (xv * r * g_ref[...]).astype(BF16)

    return pl.pallas_call(
        body, name=name, grid=(t // ROW_TILE,),
        in_specs=[pl.BlockSpec((ROW_TILE, d), lambda i: (i, 0)), pl.BlockSpec((1, d), lambda i: (0, 0))],
        out_specs=pl.BlockSpec((ROW_TILE, d), lambda i: (i, 0)),
        out_shape=jax.ShapeDtypeStruct((t, d), BF16),
        compiler_params=_params(("parallel",)),
    )(x, g)


def _rms_bwd(x, g, dh, dres, name):
    t, d = x.shape

    def body(x_ref, g_ref, dh_ref, dres_ref, dx_ref, dxb_ref, dg_ref):
        xv = x_ref[...]
        r = lax.rsqrt(jnp.mean(xv * xv, axis=-1, keepdims=True) + EPS)
        xh = xv * r
        dhv = dh_ref[...]
        gd = dhv * g_ref[...]
        dx = r * (gd - xh * jnp.mean(gd * xh, axis=-1, keepdims=True)) + dres_ref[...]
        dx_ref[...] = dx
        dxb_ref[...] = dx.astype(BF16)
        part = jnp.sum(dhv * xh, axis=0, keepdims=True)

        @pl.when(pl.program_id(0) == 0)
        def _():
            dg_ref[...] = part

        @pl.when(pl.program_id(0) > 0)
        def _():
            dg_ref[...] += part

    row = pl.BlockSpec((ROW_TILE, d), lambda i: (i, 0))
    vec = pl.BlockSpec((1, d), lambda i: (0, 0))
    return pl.pallas_call(
        body, name=name, grid=(t // ROW_TILE,),
        in_specs=[row, vec, row, row], out_specs=[row, row, vec],
        out_shape=[jax.ShapeDtypeStruct((t, d), F32), jax.ShapeDtypeStruct((t, d), BF16),
                   jax.ShapeDtypeStruct((1, d), F32)],
        compiler_params=_params(("arbitrary",)),
    )(x, g, dh, dres)


def _matmul_norm(a, b, res, g, *, a_spec, b_spec, dims, name):
    t, d = res.shape

    def body(a_ref, b_ref, r_ref, g_ref, x_ref, h_ref):
        xv = _contract(a_ref, b_ref, dims) + r_ref[...]
        x_ref[...] = xv
        h_ref[...] = (xv * lax.rsqrt(jnp.mean(xv * xv, axis=-1, keepdims=True) + EPS) * g_ref[...]).astype(BF16)

    row = pl.BlockSpec((TM_ACC, d), lambda i: (i, 0))
    return pl.pallas_call(
        body, name=name, grid=(t // TM_ACC,),
        in_specs=[a_spec, b_spec, row, pl.BlockSpec((1, d), lambda i: (0, 0))], out_specs=[row, row],
        out_shape=[jax.ShapeDtypeStruct((t, d), F32), jax.ShapeDtypeStruct((t, d), BF16)],
        compiler_params=_params(("parallel",)),
    )(a, b, res, g)


def _matmul_norm_bwd(a, b, x, g, dres, *, a_spec, b_spec, dims, name):
    t, d = x.shape

    def body(a_ref, b_ref, x_ref, g_ref, dres_ref, dx_ref, dxb_ref, dg_ref):
        dhv = _contract(a_ref, b_ref, dims)
        xv = x_ref[...]
        r = lax.rsqrt(jnp.mean(xv * xv, axis=-1, keepdims=True) + EPS)
        xh = xv * r
        gd = dhv * g_ref[...]
        dx = r * (gd - xh * jnp.mean(gd * xh, axis=-1, keepdims=True)) + dres_ref[...]
        dx_ref[...] = dx
        dxb_ref[...] = dx.astype(BF16)
        part = jnp.sum(dhv * xh, axis=0, keepdims=True)

        @pl.when(pl.program_id(0) == 0)
        def _():
            dg_ref[...] = part

        @pl.when(pl.program_id(0) > 0)
        def _():
            dg_ref[...] += part

    row = pl.BlockSpec((TM_ACC, d), lambda i: (i, 0))
    vec = pl.BlockSpec((1, d), lambda i: (0, 0))
    return pl.pallas_call(
        body, name=name, grid=(t // TM_ACC,),
        in_specs=[a_spec, b_spec, row, vec, row], out_specs=[row, row, vec],
        out_shape=[jax.ShapeDtypeStruct((t, d), F32), jax.ShapeDtypeStruct((t, d), BF16),
                   jax.ShapeDtypeStruct((1, d), F32)],
        compiler_params=_params(("arbitrary",)),
    )(a, b, x, g, dres)


def _loss_head(x, g, target, name):
    t, d = x.shape

    def body(x_ref, g_ref, t_ref, loss_ref, dx_ref, dxb_ref, dg_ref):
        xv = x_ref[...]
        r = lax.rsqrt(jnp.mean(xv * xv, axis=-1, keepdims=True) + EPS)
        xh = xv * r
        gv = g_ref[...]
        err = xh * gv - t_ref[...]
        loss = jnp.full((1, LANES), 0.5 / d, F32) * jnp.sum(err * err)
        dy = err * (1.0 / d)
        gd = dy * gv
        dx = r * (gd - xh * jnp.mean(gd * xh, axis=-1, keepdims=True))
        dx_ref[...] = dx
        dxb_ref[...] = dx.astype(BF16)
        part = jnp.sum(dy * xh, axis=0, keepdims=True)

        @pl.when(pl.program_id(0) == 0)
        def _():
            dg_ref[...] = part
            loss_ref[...] = loss

        @pl.when(pl.program_id(0) > 0)
        def _():
            dg_ref[...] += part
            loss_ref[...] += loss

    row = pl.BlockSpec((ROW_TILE, d), lambda i: (i, 0))
    vec = pl.BlockSpec((1, d), lambda i: (0, 0))
    return pl.pallas_call(
        body, name=name, grid=(t // ROW_TILE,),
        in_specs=[row, vec, row],
        out_specs=[pl.BlockSpec((1, LANES), lambda i: (0, 0)), row, row, vec],
        out_shape=[jax.ShapeDtypeStruct((1, LANES), F32), jax.ShapeDtypeStruct((t, d), F32),
                   jax.ShapeDtypeStruct((t, d), BF16), jax.ShapeDtypeStruct((1, d), F32)],
        compiler_params=_params(("arbitrary",)),
    )(x, g, target)


def _group_matrix(n):
    shift = int(math.log2(HEAD_DIM))
    r = lax.broadcasted_iota(jnp.int32, (n, n), 0) >> shift
    c = lax.broadcasted_iota(jnp.int32, (n, n), 1) >> shift
    return (r == c).astype(BF16)


def _group_sum(v, gmat):
    hi = v.astype(BF16)
    lo = (v - hi.astype(F32)).astype(BF16)

    def dot(p):
        return jnp.dot(p, gmat, preferred_element_type=F32)

    return dot(hi) + dot(lo)


def _shift_rows(ext, k):
    return pltpu.roll(ext, k % ext.shape[0], 0)


def _store_columns(stage, out_hbm, sems, row0, nrows, col_blocks):
    rows = pl.ds(pl.multiple_of(row0, SUBLANES * 2), nrows)
    copies = [
        pltpu.make_async_copy(stage.at[i], out_hbm.at[rows, pl.ds(pl.multiple_of(cb * LANES, LANES), LANES)], sems.at[i])
        for i, cb in enumerate(col_blocks)
    ]
    for cp in copies:
        cp.start()
    for cp in copies:
        cp.wait()


def _attn_consts(width):
    i = lax.broadcasted_iota(jnp.int32, (BAND, width), 0)
    j = lax.broadcasted_iota(jnp.int32, (BAND, width), 1)
    dist = (width - BAND) + i - j
    inwin = (dist >= 0) & (dist <= BAND)
    return dist.astype(F32), inwin, j


def _head_masks():
    lane = lax.broadcasted_iota(jnp.int32, (1, LANES), 1)
    return [(lane < HEAD_DIM).astype(F32), (lane >= HEAD_DIM).astype(F32)]


def _pair_bias(slope, dil):
    distf, inwin, _ = _attn_consts(2 * BAND)
    return jnp.concatenate([jnp.where(inwin, distf * (slope[hh] * (-float(dil))), NEG) for hh in range(2)], axis=0)


def _stack_heads(xv, hmask):
    return jnp.concatenate([xv * hmask[0], xv * hmask[1]], axis=0).astype(BF16)


FWD_UNROLL = 8
BWD_UNROLL = 8


def _unroll(trips, most):
    return max(u for u in range(1, most + 1) if trips % u == 0)


def _for_blocks(seq, dil, block, most):
    nb = seq // dil // BAND

    def residue(r, carry):
        base = r * nb
        block(pl.multiple_of(base * BAND, BAND), None)
        if nb > 1:
            def rest(n, c):
                block(pl.multiple_of((base + n) * BAND, BAND), pl.multiple_of((base + n - 1) * BAND, BAND))
                return c

            lax.fori_loop(1, nb, rest, 0, unroll=_unroll(nb - 1, most))
        return carry

    if dil == 1:
        residue(0, 0)
    else:
        lax.fori_loop(0, dil, residue, 0, unroll=_unroll(dil, max(1, most // nb)))


def _permute_in(src_ref, dst_ref, dil, seq):
    length = seq // dil
    for r in range(dil):
        dst_ref[pl.ds(r * length, length), :] = src_ref[pl.ds(r, length, stride=dil), :].astype(dst_ref.dtype)


def _slopes_table():
    slopes = 2.0 ** (-8.0 * jnp.arange(1, N_HEADS + 1, dtype=F32) / N_HEADS)
    return jnp.broadcast_to(slopes[:, None], (N_HEADS, 2 * BAND))


def _attn_fwd(proj, attn_g, nbatch, seq):
    t = nbatch * seq
    scale = HEAD_DIM ** -0.5

    def body(q_ref, k_ref, v_ref, g_ref, sl_ref, o_ref, lse_ref, cat_ref, pq, pk, pv, po, pm, pll, ao, am, al):
        hp = pl.program_id(1)
        hmask = _head_masks()
        slope = [sl_ref[pl.ds(2 * hp + hh, 1), :] for hh in range(2)]

        def run_branch(dil, qs, ks, vs, osink, msink, lsink):
            bias = _pair_bias(slope, dil)

            def block(row0, prow):
                cur = pl.ds(row0, BAND)
                q2 = _stack_heads(qs[cur, :] * scale, hmask)
                if prow is None:
                    kk, vv, bias_b = ks[cur, :], vs[cur, :], bias[:, BAND:]
                else:
                    prev = pl.ds(prow, BAND)
                    kk = jnp.concatenate([ks[prev, :], ks[cur, :]], axis=0)
                    vv = jnp.concatenate([vs[prev, :], vs[cur, :]], axis=0)
                    bias_b = bias
                s = lax.dot_general(q2, kk.astype(BF16), NT, preferred_element_type=F32) + bias_b
                m = jnp.max(s, axis=1, keepdims=True)
                p = jnp.exp(s - m)
                l = jnp.sum(p, axis=1, keepdims=True)
                pb = p.astype(BF16)
                o = jnp.dot(jnp.concatenate([pb[:BAND], pb[BAND:]], axis=1), _stack_heads(vv, hmask),
                            preferred_element_type=F32)
                osink[cur, :] = o
                msink[cur, :] = m[:BAND] * hmask[0] + m[BAND:] * hmask[1]
                lsink[cur, :] = l[:BAND] * hmask[0] + l[BAND:] * hmask[1]

            _for_blocks(seq, dil, block, FWD_UNROLL)

        run_branch(1, q_ref, k_ref, v_ref, ao, am, al)
        for dil in DILATIONS[1:]:
            length = seq // dil
            _permute_in(q_ref, pq, dil, seq)
            _permute_in(k_ref, pk, dil, seq)
            _permute_in(v_ref, pv, dil, seq)
            run_branch(dil, pq, pk, pv, po, pm, pll)
            for r in range(dil):
                nat = pl.ds(r, length, stride=dil)
                per = pl.ds(r * length, length)
                m0 = am[nat, :]
                mb = pm[per, :]
                mn = jnp.maximum(m0, mb)
                e0 = jnp.exp(m0 - mn)
                eb = jnp.exp(mb - mn)
                ao[nat, :] = ao[nat, :] * e0 + po[per, :] * eb
                al[nat, :] = al[nat, :] * e0 + pll[per, :] * eb
                am[nat, :] = mn

        gmat = _group_matrix(LANES)
        gv = g_ref[...]

        def fin(c, carry):
            rows = pl.ds(pl.multiple_of(c * 256, 256), 256)
            lv = al[rows, :]
            o = ao[rows, :] / lv
            o_ref[rows, :] = o
            lse_ref[rows, :] = am[rows, :] + jnp.log(lv)
            ms = _group_sum(o * o, gmat) * (1.0 / HEAD_DIM)
            cat_ref[rows, :] = (o * lax.rsqrt(ms + EPS) * gv).astype(BF16)
            return carry

        lax.fori_loop(0, seq // 256, fin, 0)

    nq = D_ATTN // LANES
    blk = lambda off: pl.BlockSpec((seq, LANES), lambda b, h: (b, h + off))
    scratch = [pltpu.VMEM((seq, LANES), F32) for _ in range(9)]
    return pl.pallas_call(
        body, name="attn_fwd", grid=(nbatch, nq),
        in_specs=[blk(0), blk(nq), blk(2 * nq), pl.BlockSpec((1, LANES), lambda b, h: (0, h)),
                  pl.BlockSpec((N_HEADS, 2 * BAND), lambda b, h: (0, 0))],
        out_specs=[blk(0), blk(0), blk(0)],
        out_shape=[jax.ShapeDtypeStruct((t, D_ATTN), F32), jax.ShapeDtypeStruct((t, D_ATTN), F32),
                   jax.ShapeDtypeStruct((t, D_MODEL), BF16)],
        scratch_shapes=scratch,
        compiler_params=_params(("parallel", "parallel")),
    )(proj, proj, proj, attn_g, _slopes_table())


def _attn_bwd(proj, o, lse, d_cat, attn_g, nbatch, seq):
    t = nbatch * seq
    scale = HEAD_DIM ** -0.5

    def body(q_ref, k_ref, v_ref, o_ref, lse_ref, dy_ref, g_ref, sl_ref, dproj_ref, dg_ref,
             do_n, dl_n, dq_n, dk_n, dv_n, pq, pk, pv, pdo, plse, pdl, pdq, pdk, pdv, stage, sems):
        hp = pl.program_id(0)
        hmask = _head_masks()
        slope = [sl_ref[pl.ds(2 * hp + hh, 1), :] for hh in range(2)]
        gmat = _group_matrix(LANES)
        gv = g_ref[...]

        def prep(c, dg):
            rows = pl.ds(pl.multiple_of(c * 256, 256), 256)
            ov = o_ref[rows, :]
            dyn = dy_ref[rows, :].astype(F32)
            r = lax.rsqrt(_group_sum(ov * ov, gmat) * (1.0 / HEAD_DIM) + EPS)
            gd = dyn * gv
            oh = ov * r
            do = r * (gd - oh * (_group_sum(gd * oh, gmat) * (1.0 / HEAD_DIM)))
            do_n[rows, :] = do
            dl_n[rows, :] = _group_sum(do * ov, gmat)
            return dg + jnp.sum(dyn * oh, axis=0, keepdims=True)

        dg = lax.fori_loop(0, seq // 256, prep, jnp.zeros((1, LANES), F32))

        @pl.when(pl.program_id(1) == 0)
        def _():
            dg_ref[...] = dg

        @pl.when(pl.program_id(1) > 0)
        def _():
            dg_ref[...] += dg

        def clear(*refs):
            def step(c, carry):
                rows = pl.ds(pl.multiple_of(c * 256, 256), 256)
                for ref in refs:
                    ref[rows, :] = jnp.zeros((256, LANES), F32)
                return carry

            lax.fori_loop(0, seq // 256, step, 0)

        clear(dq_n, dk_n, dv_n)

        def run_branch(dil, qs, ks, vs, dos, lses, dls, dqs, dks, dvs):
            bias = _pair_bias(slope, dil)

            def per_head(xv):
                return jnp.concatenate([xv[:, 0:1], xv[:, HEAD_DIM:HEAD_DIM + 1]], axis=0)

            def block(row0, prow):
                cur = pl.ds(row0, BAND)
                keys = cur if prow is None else pl.ds(prow, 2 * BAND)
                q2 = _stack_heads(qs[cur, :] * scale, hmask)
                do2 = _stack_heads(dos[cur, :], hmask)
                kk, vv = ks[keys, :], vs[keys, :]
                s = lax.dot_general(q2, kk.astype(BF16), NT, preferred_element_type=F32)
                s = s + (bias[:, BAND:] if prow is None else bias)
                p = jnp.exp(s - per_head(lses[cur, :]))
                dp = lax.dot_general(do2, vv.astype(BF16), NT, preferred_element_type=F32)
                ds = (p * (dp - per_head(dls[cur, :]))).astype(BF16)
                dqs[cur, :] += jnp.dot(jnp.concatenate([ds[:BAND], ds[BAND:]], axis=1), _stack_heads(kk, hmask),
                                       preferred_element_type=F32)
                dks[keys, :] += lax.dot_general(ds, q2, TN, preferred_element_type=F32)
                dvs[keys, :] += lax.dot_general(p.astype(BF16), do2, TN, preferred_element_type=F32)

            _for_blocks(seq, dil, block, BWD_UNROLL)

        run_branch(1, q_ref, k_ref, v_ref, do_n, lse_ref, dl_n, dq_n, dk_n, dv_n)
        for dil in DILATIONS[1:]:
            length = seq // dil
            for src, dst in ((q_ref, pq), (k_ref, pk), (v_ref, pv), (do_n, pdo), (lse_ref, plse), (dl_n, pdl)):
                _permute_in(src, dst, dil, seq)
            clear(pdq, pdk, pdv)
            run_branch(dil, pq, pk, pv, pdo, plse, pdl, pdq, pdk, pdv)
            for r in range(dil):
                nat = pl.ds(r, length, stride=dil)
                per = pl.ds(r * length, length)
                dq_n[nat, :] += pdq[per, :]
                dk_n[nat, :] += pdk[per, :]
                dv_n[nat, :] += pdv[per, :]

        def emit(c, carry):
            rows = pl.ds(pl.multiple_of(c * 256, 256), 256)
            stage[0, rows, :] = (dq_n[rows, :] * scale).astype(BF16)
            stage[1, rows, :] = dk_n[rows, :].astype(BF16)
            stage[2, rows, :] = dv_n[rows, :].astype(BF16)
            return carry

        lax.fori_loop(0, seq // 256, emit, 0)
        _store_columns(stage, dproj_ref, sems, pl.program_id(1) * seq, seq, [hp, nq + hp, 2 * nq + hp])

    nq = D_ATTN // LANES
    blk = lambda off: pl.BlockSpec((seq, LANES), lambda h, b: (b, h + off))
    vec = pl.BlockSpec((1, LANES), lambda h, b: (0, h))
    scratch = [pltpu.VMEM((seq, LANES), F32) for _ in range(14)]
    scratch += [pltpu.VMEM((3, seq, LANES), BF16), pltpu.SemaphoreType.DMA((3,))]
    d_proj, dg = pl.pallas_call(
        body, name="attn_bwd", grid=(nq, nbatch),
        in_specs=[blk(0), blk(nq), blk(2 * nq), blk(0), blk(0), blk(0), vec,
                  pl.BlockSpec((N_HEADS, 2 * BAND), lambda h, b: (0, 0))],
        out_specs=[pl.BlockSpec(memory_space=pl.ANY), vec],
        out_shape=[jax.ShapeDtypeStruct((t, D_IN), BF16), jax.ShapeDtypeStruct((1, D_ATTN), F32)],
        scratch_shapes=scratch,
        compiler_params=_params(("arbitrary", "arbitrary")),
    )(proj, proj, proj, o, lse, d_cat, attn_g, _slopes_table())
    return d_proj, dg


HALO = SUBLANES
PACKED_ROWS = 2 * SUBLANES


def _window(ref, c, rows, nchunks, after):
    row0 = pl.multiple_of(c * rows, rows)
    prev0 = pl.multiple_of(jnp.maximum(row0 - PACKED_ROWS, 0), PACKED_ROWS)
    before = ref[pl.ds(prev0, PACKED_ROWS), :].astype(F32)[PACKED_ROWS - HALO:] * (c > 0).astype(F32)
    parts = [before, ref[pl.ds(row0, rows), :].astype(F32)]
    if after:
        next0 = pl.multiple_of(jnp.minimum(row0 + rows, (nchunks - 1) * rows), PACKED_ROWS)
        parts.append(ref[pl.ds(next0, PACKED_ROWS), :].astype(F32)[:HALO] * (c < nchunks - 1).astype(F32))
    return jnp.concatenate(parts, axis=0)


def _behind(z):
    z1 = _shift_rows(z, 1)
    return z1, _shift_rows(z1, 1)


def _ahead(dy):
    d1 = _shift_rows(dy, -1)
    return d1, _shift_rows(d1, -1)


def _conv(z, w):
    z1, z2 = _behind(z)
    return w[0:1] * z2 + w[1:2] * z1 + w[2:3] * z


def _conv_bwd(dy, z, w, cur):
    d1, d2 = _ahead(dy)
    dz = w[2:3] * dy + w[1:2] * d1 + w[0:1] * d2
    return dz, [jnp.sum((d * z)[cur], axis=0, keepdims=True) for d in (d2, d1, dy)]


def _sigmoid(a):
    return 0.5 * jnp.tanh(0.5 * a) + 0.5


MIX_ROWS = 256
GATE_B_BLOCK = 3 * D_ATTN // LANES
GATE_C_BLOCK = GATE_B_BLOCK + D_CONV // LANES
U_BLOCK = GATE_C_BLOCK + D_CONV // LANES


def _convmix_fwd(proj, cat, mcw, conv_g, nbatch, seq):
    nchunks = seq // MIX_ROWS

    def body(gb_ref, gc_ref, u_ref, w_ref, g_ref, cat_in, cat_ref):
        del cat_in
        gmat = _group_matrix(LANES)
        w = w_ref[...]
        gv = g_ref[...]

        def step(c, carry):
            cur = pl.ds(pl.multiple_of(c * MIX_ROWS, MIX_ROWS), MIX_ROWS)
            z = _window(gc_ref, c, MIX_ROWS, nchunks, False) * _window(u_ref, c, MIX_ROWS, nchunks, False)
            y = gb_ref[cur, :] * _conv(z, w)[HALO:]
            ms = _group_sum(y * y, gmat) * (1.0 / HEAD_DIM)
            cat_ref[cur, :] = (y * lax.rsqrt(ms + EPS) * gv).astype(BF16)
            return carry

        lax.fori_loop(0, nchunks, step, 0)

    nc = D_CONV // LANES
    blk = lambda off: pl.BlockSpec((seq, LANES), lambda b, j: (b, j + off))
    return pl.pallas_call(
        body, name="convmix_fwd", grid=(nbatch, nc),
        in_specs=[blk(GATE_B_BLOCK), blk(GATE_C_BLOCK), blk(U_BLOCK),
                  pl.BlockSpec((3, LANES), lambda b, j: (0, j)), pl.BlockSpec((1, LANES), lambda b, j: (0, j)),
                  pl.BlockSpec(memory_space=pl.ANY)],
        out_specs=blk(D_ATTN // LANES),
        out_shape=jax.ShapeDtypeStruct(cat.shape, cat.dtype),
        input_output_aliases={5: 0},
        compiler_params=_params(("parallel", "parallel")),
    )(proj, proj, proj, mcw, conv_g, cat)


def _convmix_bwd(proj, d_cat, d_proj, mcw, conv_g, nbatch, seq):
    nchunks = seq // MIX_ROWS

    def body(gb_ref, gc_ref, u_ref, dy_ref, w_ref, g_ref, dproj_in, dproj_ref, dw_ref, dg_ref, stage, sems):
        del dproj_in
        cb = pl.program_id(0)
        b = pl.program_id(1)
        gmat = _group_matrix(LANES)
        w = w_ref[...]
        gv = g_ref[...]
        cur = slice(HALO, HALO + MIX_ROWS)

        def step(c, carry):
            rows = pl.ds(pl.multiple_of(c * MIX_ROWS, MIX_ROWS), MIX_ROWS)
            gb = _window(gb_ref, c, MIX_ROWS, nchunks, True)
            gc = _window(gc_ref, c, MIX_ROWS, nchunks, True)
            u = _window(u_ref, c, MIX_ROWS, nchunks, True)
            dyn = _window(dy_ref, c, MIX_ROWS, nchunks, True)
            z = gc * u
            conv = _conv(z, w)
            y = gb * conv
            r = lax.rsqrt(_group_sum(y * y, gmat) * (1.0 / HEAD_DIM) + EPS)
            yh = y * r
            gd = dyn * gv
            dy = r * (gd - yh * (_group_sum(gd * yh, gmat) * (1.0 / HEAD_DIM)))
            dz, dws = _conv_bwd(dy * gb, z, w, cur)
            stage[0, rows, :] = (dy * conv)[cur].astype(BF16)
            stage[1, rows, :] = (dz * u)[cur].astype(BF16)
            stage[2, rows, :] = (dz * gc)[cur].astype(BF16)
            dg = jnp.sum((dyn * yh)[cur], axis=0, keepdims=True)
            return tuple(a + d for a, d in zip(carry, dws + [dg]))

        zero = jnp.zeros((1, LANES), F32)
        dw0, dw1, dw2, dg = lax.fori_loop(0, nchunks, step, (zero, zero, zero, zero))

        @pl.when(b == 0)
        def _():
            dw_ref[0:1, :] = dw0
            dw_ref[1:2, :] = dw1
            dw_ref[2:3, :] = dw2
            dg_ref[...] = dg

        @pl.when(b > 0)
        def _():
            dw_ref[0:1, :] += dw0
            dw_ref[1:2, :] += dw1
            dw_ref[2:3, :] += dw2
            dg_ref[...] += dg

        _store_columns(stage, dproj_ref, sems, b * seq, seq, [GATE_B_BLOCK + cb, GATE_C_BLOCK + cb, U_BLOCK + cb])

    nc = D_CONV // LANES
    blk = lambda off: pl.BlockSpec((seq, LANES), lambda j, b: (b, j + off))
    return pl.pallas_call(
        body, name="convmix_bwd", grid=(nc, nbatch),
        in_specs=[blk(GATE_B_BLOCK), blk(GATE_C_BLOCK), blk(U_BLOCK), blk(D_ATTN // LANES),
                  pl.BlockSpec((3, LANES), lambda j, b: (0, j)), pl.BlockSpec((1, LANES), lambda j, b: (0, j)),
                  pl.BlockSpec(memory_space=pl.ANY)],
        out_specs=[pl.BlockSpec(memory_space=pl.ANY), pl.BlockSpec((3, LANES), lambda j, b: (0, j)),
                   pl.BlockSpec((1, LANES), lambda j, b: (0, j))],
        out_shape=[jax.ShapeDtypeStruct(d_proj.shape, d_proj.dtype), jax.ShapeDtypeStruct((3, D_CONV), F32),
                   jax.ShapeDtypeStruct((1, D_CONV), F32)],
        scratch_shapes=[pltpu.VMEM((3, seq, LANES), BF16), pltpu.SemaphoreType.DMA((3,))],
        input_output_aliases={6: 0},
        compiler_params=_params(("arbitrary", "arbitrary")),
    )(proj, proj, proj, d_cat, mcw, conv_g, d_proj)


FFN_ROWS = 256


def _ffn_act_fwd(pre, fcw, nbatch, seq):
    t = nbatch * seq
    nchunks = seq // FFN_ROWS

    def body(pre_ref, w_ref, act_ref):
        wa = w_ref[0]
        wc = w_ref[1]

        def step(c, carry):
            cur = pl.ds(pl.multiple_of(c * FFN_ROWS, FFN_ROWS), FFN_ROWS)
            a = _conv(_window(pre_ref.at[0], c, FFN_ROWS, nchunks, False), wa)[HALO:]
            v = _conv(_window(pre_ref.at[1], c, FFN_ROWS, nchunks, False), wc)[HALO:]
            act_ref[cur, :] = (a * _sigmoid(a) * v).astype(BF16)
            return carry

        lax.fori_loop(0, nchunks, step, 0)

    return pl.pallas_call(
        body, name="ffn_act_fwd", grid=(N_UP_PAIRS, nbatch),
        in_specs=[pl.BlockSpec((2, None, seq, UP_CHUNK), lambda i, b: (0, i, b, 0)),
                  pl.BlockSpec((2, None, 3, UP_CHUNK), lambda i, b: (0, i, 0, 0))],
        out_specs=pl.BlockSpec((None, seq, UP_CHUNK), lambda i, b: (i, b, 0)),
        out_shape=jax.ShapeDtypeStruct((N_UP_PAIRS, t, UP_CHUNK), BF16),
        compiler_params=_params(("parallel", "parallel")),
    )(pre, fcw)


def _ffn_down(pre, fcw, wdown, res, g, seq, name):
    t, d = res.shape
    tiles_per_seq = seq // FFN_ROWS

    def body(main_ref, halo_ref, w_ref, wd_ref, r_ref, *rest):
        if g is None:
            x_ref, act_ref = rest
        else:
            g_ref, x_ref, h_ref, act_ref = rest
        inside = ((pl.program_id(0) % tiles_per_seq) > 0).astype(F32)

        def window(part, p):
            before = halo_ref[part, p].astype(F32)[PACKED_ROWS - HALO:] * inside
            return jnp.concatenate([before, main_ref[part, p].astype(F32)], axis=0)

        total = r_ref[...]
        for p in range(N_UP_PAIRS):
            a = _conv(window(0, p), w_ref[0, p])[HALO:]
            v = _conv(window(1, p), w_ref[1, p])[HALO:]
            act = (a * _sigmoid(a) * v).astype(BF16)
            act_ref[p] = act
            total = total + jnp.dot(act, wd_ref[p], preferred_element_type=F32)
        x_ref[...] = total
        if g is not None:
            h_ref[...] = (total * lax.rsqrt(jnp.mean(total * total, axis=-1, keepdims=True) + EPS) * g_ref[...]).astype(BF16)

    row = pl.BlockSpec((FFN_ROWS, d), lambda i: (i, 0))
    tiles_per_halo = FFN_ROWS // PACKED_ROWS
    in_specs = [
        pl.BlockSpec((2, N_UP_PAIRS, FFN_ROWS, UP_CHUNK), lambda i: (0, 0, i, 0)),
        pl.BlockSpec((2, N_UP_PAIRS, PACKED_ROWS, UP_CHUNK), lambda i: (0, 0, jnp.maximum(i * tiles_per_halo - 1, 0), 0)),
        pl.BlockSpec((2, N_UP_PAIRS, 3, UP_CHUNK), lambda i: (0, 0, 0, 0)),
        pl.BlockSpec((N_UP_PAIRS, UP_CHUNK, d), lambda i: (0, 0, 0)), row]
    out_specs = [row]
    out_shape = [jax.ShapeDtypeStruct((t, d), F32)]
    args = [pre, pre, fcw, wdown, res]
    if g is not None:
        in_specs.append(pl.BlockSpec((1, d), lambda i: (0, 0)))
        out_specs.append(row)
        out_shape.append(jax.ShapeDtypeStruct((t, d), BF16))
        args.append(g)
    out_specs.append(pl.BlockSpec((N_UP_PAIRS, FFN_ROWS, UP_CHUNK), lambda i: (0, i, 0)))
    out_shape.append(jax.ShapeDtypeStruct((N_UP_PAIRS, t, UP_CHUNK), BF16))
    return pl.pallas_call(
        body, name=name, grid=(t // FFN_ROWS,), in_specs=in_specs, out_specs=out_specs, out_shape=out_shape,
        compiler_params=_params(("parallel",)),
    )(*args)


def _ffn_act_bwd(pre, d_act, fcw, nbatch, seq):
    nchunks = seq // FFN_ROWS

    def body(pre_ref, da_ref, w_ref, dpre_ref, dw_ref):
        b = pl.program_id(1)
        wa = w_ref[0]
        wc = w_ref[1]
        cur = slice(HALO, HALO + FFN_ROWS)

        def step(c, carry):
            rows = pl.ds(pl.multiple_of(c * FFN_ROWS, FFN_ROWS), FFN_ROWS)
            pg = _window(pre_ref.at[0], c, FFN_ROWS, nchunks, True)
            pv = _window(pre_ref.at[1], c, FFN_ROWS, nchunks, True)
            dact = _window(da_ref, c, FFN_ROWS, nchunks, True)
            a = _conv(pg, wa)
            v = _conv(pv, wc)
            sg = _sigmoid(a)
            asg = a * sg
            dzg, dwg = _conv_bwd(dact * v * (sg + asg - asg * sg), pg, wa, cur)
            dzv, dwv = _conv_bwd(dact * asg, pv, wc, cur)
            dpre_ref[0, rows, :] = dzg[cur].astype(BF16)
            dpre_ref[1, rows, :] = dzv[cur].astype(BF16)
            return tuple(acc + d for acc, d in zip(carry, dwg + dwv))

        zero = jnp.zeros((1, UP_CHUNK), F32)
        sums = lax.fori_loop(0, nchunks, step, (zero,) * 6)

        @pl.when(b == 0)
        def _():
            for i in range(6):
                dw_ref[i // 3, pl.ds(i % 3, 1), :] = sums[i]

        @pl.when(b > 0)
        def _():
            for i in range(6):
                dw_ref[i // 3, pl.ds(i % 3, 1), :] += sums[i]

    pair = pl.BlockSpec((2, None, seq, UP_CHUNK), lambda i, b: (0, i, b, 0))
    wspec = pl.BlockSpec((2, None, 3, UP_CHUNK), lambda i, b: (0, i, 0, 0))
    return pl.pallas_call(
        body, name="ffn_act_bwd", grid=(N_UP_PAIRS, nbatch),
        in_specs=[pair, pl.BlockSpec((None, seq, UP_CHUNK), lambda i, b: (i, b, 0)), wspec],
        out_specs=[pair, wspec],
        out_shape=[jax.ShapeDtypeStruct(pre.shape, BF16), jax.ShapeDtypeStruct(fcw.shape, F32)],
        compiler_params=_params(("parallel", "arbitrary")),
    )(pre, d_act, fcw)


def _ffn_up_bwd(pre, dy, fcw, wdown, wup, x, g, dres, seq, name):
    t, d = x.shape
    tiles_per_seq = seq // FFN_ROWS
    tiles_per_halo = FFN_ROWS // PACKED_ROWS
    last_halo = t // PACKED_ROWS - 1

    def body(pm_ref, pp_ref, pn_ref, dm_ref, dp_ref, dn_ref, w_ref, wd_ref, wu_ref, x_ref, g_ref, dres_ref,
             dpre_ref, dw_ref, dx_ref, dxb_ref, dg_ref):
        i = pl.program_id(0)
        has_prev = ((i % tiles_per_seq) > 0).astype(F32)
        has_next = ((i % tiles_per_seq) < tiles_per_seq - 1).astype(F32)
        cur = slice(HALO, HALO + FFN_ROWS)

        def window(before, main, after):
            return jnp.concatenate([before.astype(F32)[PACKED_ROWS - HALO:] * has_prev, main.astype(F32),
                                    after.astype(F32)[:HALO] * has_next], axis=0)

        dy_rows = jnp.concatenate([dp_ref[...], dm_ref[...], dn_ref[...]], axis=0)
        wrow = lax.broadcasted_iota(jnp.int32, (FFN_ROWS + 2 * HALO, 1), 0)
        edge = jnp.where(wrow < HALO, has_prev, jnp.where(wrow >= HALO + FFN_ROWS, has_next, 1.0))

        dh = jnp.zeros((FFN_ROWS, d), F32)
        sums = []
        for p in range(N_UP_PAIRS):
            pg = window(pp_ref[0, p], pm_ref[0, p], pn_ref[0, p])
            pv = window(pp_ref[1, p], pm_ref[1, p], pn_ref[1, p])
            dact = lax.dot_general(dy_rows, wd_ref[p], NT, preferred_element_type=F32)
            dact = dact[PACKED_ROWS - HALO:PACKED_ROWS + FFN_ROWS + HALO] * edge
            a = _conv(pg, w_ref[0, p])
            v = _conv(pv, w_ref[1, p])
            sg = _sigmoid(a)
            asg = a * sg
            dzg, dwg = _conv_bwd(dact * v * (sg + asg - asg * sg), pg, w_ref[0, p], cur)
            dzv, dwv = _conv_bwd(dact * asg, pv, w_ref[1, p], cur)
            dgate = dzg[cur].astype(BF16)
            dval = dzv[cur].astype(BF16)
            dpre_ref[0, p] = dgate
            dpre_ref[1, p] = dval
            dh = dh + jnp.dot(dgate, wu_ref[p], preferred_element_type=F32)
            dh = dh + jnp.dot(dval, wu_ref[N_UP_PAIRS + p], preferred_element_type=F32)
            sums.append(dwg + dwv)

        xv = x_ref[...]
        r = lax.rsqrt(jnp.mean(xv * xv, axis=-1, keepdims=True) + EPS)
        xh = xv * r
        gd = dh * g_ref[...]
        dx = r * (gd - xh * jnp.mean(gd * xh, axis=-1, keepdims=True)) + dres_ref[...]
        dx_ref[...] = dx
        dxb_ref[...] = dx.astype(BF16)
        part = jnp.sum(dh * xh, axis=0, keepdims=True)

        @pl.when(i == 0)
        def _():
            dg_ref[...] = part
            for p in range(N_UP_PAIRS):
                for k in range(6):
                    dw_ref[k // 3, p, pl.ds(k % 3, 1), :] = sums[p][k]

        @pl.when(i > 0)
        def _():
            dg_ref[...] += part
            for p in range(N_UP_PAIRS):
                for k in range(6):
                    dw_ref[k // 3, p, pl.ds(k % 3, 1), :] += sums[p][k]

    def rows4(n):
        return lambda fn: pl.BlockSpec((2, N_UP_PAIRS, n, UP_CHUNK), lambda i: (0, 0, fn(i), 0))

    def rows2(n):
        return lambda fn: pl.BlockSpec((n, d), lambda i: (fn(i), 0))

    prev_tile = lambda i: jnp.maximum(i * tiles_per_halo - 1, 0)
    next_tile = lambda i: jnp.minimum((i + 1) * tiles_per_halo, last_halo)
    row = pl.BlockSpec((FFN_ROWS, d), lambda i: (i, 0))
    vec = pl.BlockSpec((1, d), lambda i: (0, 0))
    wspec = pl.BlockSpec((2, N_UP_PAIRS, 3, UP_CHUNK), lambda i: (0, 0, 0, 0))
    return pl.pallas_call(
        body, name=name, grid=(t // FFN_ROWS,),
        in_specs=[rows4(FFN_ROWS)(lambda i: i), rows4(PACKED_ROWS)(prev_tile), rows4(PACKED_ROWS)(next_tile),
                  rows2(FFN_ROWS)(lambda i: i), rows2(PACKED_ROWS)(prev_tile), rows2(PACKED_ROWS)(next_tile),
                  wspec, pl.BlockSpec((N_UP_PAIRS, UP_CHUNK, d), lambda i: (0, 0, 0)),
                  pl.BlockSpec((N_DEV, UP_CHUNK, d), lambda i: (0, 0, 0)), row, vec, row],
        out_specs=[rows4(FFN_ROWS)(lambda i: i), wspec, row, row, vec],
        out_shape=[jax.ShapeDtypeStruct(pre.shape, BF16), jax.ShapeDtypeStruct(fcw.shape, F32),
                   jax.ShapeDtypeStruct((t, d), F32), jax.ShapeDtypeStruct((t, d), BF16),
                   jax.ShapeDtypeStruct((1, d), F32)],
        compiler_params=_params(("arbitrary",)),
    )(pre, pre, pre, dy, dy, dy, fcw, wdown, wup, x, g, dres)


def _adamw(lands, w, m, v, row_tile, name, after=()):
    nl = len(lands)
    _, nr, ncol = lands[0].shape
    c1 = 1.0 - ADAM_B1 ** ADAM_STEP
    c2 = 1.0 - ADAM_B2 ** ADAM_STEP

    def body(*refs):
        land_refs = refs[:nl]
        w_ref, m_ref, v_ref = refs[nl:nl + 3]
        g_ref, d_ref, mo_ref, vo_ref = refs[nl + 3 + len(after):]
        for l in range(nl):
            @pl.when(pl.program_id(0) == l)
            def _(l=l):
                g = land_refs[l][0].astype(F32)
                for j in range(1, N_DEV):
                    g = g + land_refs[l][j].astype(F32)
                g_ref[...] = g

        g = g_ref[...]
        m2 = ADAM_B1 * m_ref[...] + (1.0 - ADAM_B1) * g
        v2 = ADAM_B2 * v_ref[...] + (1.0 - ADAM_B2) * (g * g)
        mo_ref[...] = m2
        vo_ref[...] = v2
        d_ref[...] = -ADAM_LR * ((m2 / c1) / (jnp.sqrt(v2 / c2) + ADAM_EPS) + ADAM_WD * w_ref[...])

    def land_spec(l):
        return pl.BlockSpec((N_DEV, row_tile, ncol), lambda k, i: (0, jnp.where(k == l, i, 0), 0))

    tile = pl.BlockSpec((None, row_tile, ncol), lambda k, i: (k, i, 0))
    return pl.pallas_call(
        body, name=name, grid=(nl, nr // row_tile),
        in_specs=[land_spec(l) for l in range(nl)] + [tile, tile, tile] + [pl.BlockSpec(memory_space=pl.ANY)] * len(after),
        out_specs=[tile] * 4,
        out_shape=[jax.ShapeDtypeStruct(w.shape, F32)] * 4,
        compiler_params=_params(("arbitrary", "arbitrary")),
    )(*lands, w, m, v, *after)


class _Item:
    def __init__(self, src, chunked, land_cols=False):
        self.src, self.chunked, self.land_cols = src, chunked, land_cols
        if chunked == "cols":
            block = (src.shape[0], src.shape[1] // N_DEV)
        else:
            block = src.shape[1:] if chunked else src.shape
        self.width = block[-1]
        self.land_shape = (block[0], N_DEV * block[1]) if land_cols else (N_DEV,) + block

    def _cols(self, first, count=1):
        return pl.ds(pl.multiple_of(first * self.width, LANES), count * self.width)

    def part(self, src_ref, j):
        if self.chunked == "cols":
            return src_ref.at[:, self._cols(j)]
        return src_ref.at[j] if self.chunked else src_ref

    def slot(self, land_ref, s):
        return land_ref.at[:, self._cols(s)] if self.land_cols else land_ref.at[s]

    def seven(self, land_ref):
        return land_ref.at[:, self._cols(0, N_DEV - 1)] if self.land_cols else land_ref.at[pl.ds(0, N_DEV - 1)]


def _mesh_place():
    x, y, c = lax.axis_index("x"), lax.axis_index("y"), lax.axis_index("c")
    return x, y, c, 4 * x + 2 * y + c


def _flipped(x, y, c, k):
    px = 1 - x if k & 4 else x
    py = 1 - y if k & 2 else y
    pc = 1 - c if k & 1 else c
    return (px, py, pc), 4 * px + 2 * py + pc


PEER_ORDER = (2, 4, 6, 3, 5, 7, 1)


def _exchange(items, name):
    n = len(items)

    def body(*refs):
        srcs, lands = refs[:n], refs[n:2 * n]
        send, recv, local = refs[2 * n:]
        x, y, c, me = _mesh_place()

        def copy(i, k, chunk, slot, dev):
            return pltpu.make_async_remote_copy(
                src_ref=items[i].part(srcs[i], chunk), dst_ref=items[i].slot(lands[i], slot),
                send_sem=send.at[i, k - 1], recv_sem=recv.at[i, k - 1], device_id=dev, device_id_type=MESH)

        own = [pltpu.make_async_copy(items[i].part(srcs[i], me), items[i].slot(lands[i], me), local.at[i])
               for i in range(n)]
        for k in PEER_ORDER:
            dev, idx = _flipped(x, y, c, k)
            for i in range(n):
                copy(i, k, idx, me, dev).start()
        for cp in own:
            cp.start()
        for k in PEER_ORDER:
            dev, idx = _flipped(x, y, c, k)
            for i in range(n):
                copy(i, k, me, idx, dev).wait_recv()
        for k in PEER_ORDER:
            dev, idx = _flipped(x, y, c, k)
            for i in range(n):
                copy(i, k, idx, me, dev).wait_send()
        for cp in own:
            cp.wait()

    hbm = pl.BlockSpec(memory_space=pl.ANY)
    return pl.pallas_call(
        body, name=name,
        in_specs=[hbm] * n, out_specs=[hbm] * n,
        out_shape=[jax.ShapeDtypeStruct(it.land_shape, it.src.dtype) for it in items],
        scratch_shapes=[pltpu.SemaphoreType.DMA((n, N_DEV - 1)), pltpu.SemaphoreType.DMA((n, N_DEV - 1)),
                        pltpu.SemaphoreType.DMA((n,))],
        compiler_params=pltpu.CompilerParams(has_side_effects=True),
    )(*[it.src for it in items])


def _sequencer_exchange(items, name, collective_id):
    n = len(items)

    def body(*refs):
        srcs, lands = refs[:n], refs[n:2 * n]
        send, recv, local = refs[2 * n:]
        x, y, c, me = _mesh_place()
        barrier = pltpu.get_barrier_semaphore()
        for k in PEER_ORDER:
            pl.semaphore_signal(barrier, inc=1, device_id=_flipped(x, y, c, k)[0], device_id_type=MESH)
        pl.semaphore_wait(barrier, N_DEV - 1)

        def copy(i, k, chunk, slot, dev):
            return pltpu.make_async_remote_copy(
                src_ref=items[i].part(srcs[i], chunk), dst_ref=items[i].slot(lands[i], slot),
                send_sem=send.at[i, k - 1], recv_sem=recv.at[i, k - 1], device_id=dev, device_id_type=MESH)

        own = [pltpu.make_async_copy(items[i].part(srcs[i], me), items[i].slot(lands[i], me), local.at[i])
               for i in range(n)]
        for cp in own:
            cp.start()
        for k in PEER_ORDER:
            dev, idx = _flipped(x, y, c, k)
            for i in range(n):
                copy(i, k, idx, me, dev).start()
        for k in PEER_ORDER:
            dev, idx = _flipped(x, y, c, k)
            for i in range(n):
                copy(i, k, me, idx, dev).wait_recv()
        for k in PEER_ORDER:
            dev, idx = _flipped(x, y, c, k)
            for i in range(n):
                copy(i, k, idx, me, dev).wait_send()
        for cp in own:
            cp.wait()

    return pl.kernel(
        body, name=name,
        out_type=[jax.ShapeDtypeStruct(it.land_shape, it.src.dtype) for it in items],
        mesh=plsc.ScalarSubcoreMesh(axis_name="sequencer", num_cores=1),
        scratch_types=[pltpu.SemaphoreType.DMA((n, N_DEV - 1)), pltpu.SemaphoreType.DMA((n, N_DEV - 1)),
                       pltpu.SemaphoreType.DMA((n,))],
        compiler_params=pltpu.CompilerParams(collective_id=collective_id),
    )(*[it.src for it in items])


HBM_SPEC = pl.BlockSpec(memory_space=pltpu.HBM)
SEM_SPEC = pl.BlockSpec(memory_space=pltpu.SEMAPHORE)
DATAFLOW = pltpu.SideEffectType.DATAFLOW_SIDE_EFFECTING


def _exchange_start(items, name, after=()):
    n = len(items)
    na = len(after)

    def body(*refs):
        srcs, land_ins = refs[:n], refs[n:2 * n]
        outs = refs[2 * n + na:6 * n + na]
        (local,) = refs[6 * n + na:]
        del land_ins
        x, y, c, me = _mesh_place()
        own = [pltpu.make_async_copy(items[i].part(srcs[i], me), items[i].slot(outs[4 * i + 3], me), local.at[i])
               for i in range(n)]
        for cp in own:
            cp.start()
        for cp in own:
            cp.wait()
        for k in PEER_ORDER:
            dev, idx = _flipped(x, y, c, k)
            for i in range(n):
                send, recv, _, land = outs[4 * i:4 * i + 4]
                pltpu.make_async_remote_copy(
                    src_ref=items[i].part(srcs[i], idx), dst_ref=items[i].slot(land, me), send_sem=send, recv_sem=recv,
                    device_id=dev, device_id_type=MESH).start()

    out_shape, out_specs, args, lands = [], [], [], []
    for it in items:
        out_shape += [pltpu.SemaphoreType.DMA(()), pltpu.SemaphoreType.DMA(()),
                      pltpu.HBM(it.src.shape, it.src.dtype), pltpu.HBM(it.land_shape, it.src.dtype)]
        out_specs += [SEM_SPEC, SEM_SPEC, HBM_SPEC, HBM_SPEC]
        args.append(pltpu.with_memory_space_constraint(it.src, pltpu.HBM))
        lands.append(pltpu.with_memory_space_constraint(lax.empty(it.land_shape, it.src.dtype), pltpu.HBM))
    outs = pl.pallas_call(
        body, name=name,
        in_specs=[HBM_SPEC] * (2 * n) + [pl.BlockSpec(memory_space=pl.ANY)] * na,
        out_specs=out_specs, out_shape=out_shape,
        scratch_shapes=[pltpu.SemaphoreType.DMA((n,))],
        input_output_aliases={**{i: 4 * i + 2 for i in range(n)}, **{n + i: 4 * i + 3 for i in range(n)}},
        compiler_params=pltpu.CompilerParams(has_side_effects=DATAFLOW),
    )(*args, *lands, *after)
    return [tuple(outs[4 * i:4 * i + 4]) + (items[i],) for i in range(n)]


def _started(handles):
    return handles[0][2]


def _exchange_wait(handles, after, name):
    n = len(handles)

    def body(*refs):
        x, y, c, _ = _mesh_place()
        for i in range(n):
            src, land, send, recv = refs[4 * i:4 * i + 4]
            del src
            seven = handles[i][4].seven(land)
            cp = pltpu.make_async_remote_copy(src_ref=seven, dst_ref=seven, send_sem=send, recv_sem=recv,
                                              device_id=(x, y, 1 - c), device_id_type=MESH)
            cp.wait_send()
            cp.wait_recv()

    args, in_specs, out_shape = [], [], []
    for send, recv, src, land, _ in handles:
        args += [src, land, send, recv]
        in_specs += [HBM_SPEC, HBM_SPEC, SEM_SPEC, SEM_SPEC]
        out_shape += [pltpu.HBM(src.shape, src.dtype), pltpu.HBM(land.shape, land.dtype)]
    outs = pl.pallas_call(
        body, name=name,
        in_specs=in_specs + [pl.BlockSpec(memory_space=pl.ANY)] * len(after), out_specs=[HBM_SPEC] * (2 * n),
        out_shape=out_shape,
        input_output_aliases={**{4 * i: 2 * i for i in range(n)}, **{4 * i + 1: 2 * i + 1 for i in range(n)}},
        compiler_params=pltpu.CompilerParams(has_side_effects=DATAFLOW),
    )(*args, *after)
    return [outs[2 * i + 1] for i in range(n)]


TM = 1024
TM_ACC = 512
TN_IN = 768


def kernel(x, norm1_g, w_in, mix_conv_w, attn_out_g, conv_out_g, w_out, norm2_g, ffn_up, ffn_conv_w, ffn_down, final_norm_g, loss_target, m_norm1_g, m_w_in, m_mix_conv_w, m_attn_out_g, m_conv_out_g, m_w_out, m_norm2_g, m_ffn_up, m_ffn_conv_w, m_ffn_down, m_final_norm_g, v_norm1_g, v_w_in, v_mix_conv_w, v_attn_out_g, v_conv_out_g, v_w_out, v_norm2_g, v_ffn_up, v_ffn_conv_w, v_ffn_down, v_final_norm_g):
    nbatch, seq, d = x.shape
    t = nbatch * seq
    nt, nta = t // TM, t // TM_ACC
    out_rows = D_MODEL // N_DEV
    down_rows = D_FF // N_DEV
    xf = x.reshape(t, d)
    target = loss_target.reshape(t, d)

    cw_local = jnp.concatenate([ffn_conv_w, mix_conv_w], axis=-1)
    cast = lambda w: _Item(w.astype(BF16), False)
    cast_in = lambda w: _Item(w.astype(BF16), False, land_cols=True)
    cw_all, win0 = _sequencer_exchange([_Item(cw_local, False), cast_in(w_in[0])], "gather_a", 0)
    up_t, m_up_t, v_up_t = (jnp.swapaxes(a, 1, 2) for a in (ffn_up, m_ffn_up, v_ffn_up))
    wout0, wup0 = _sequencer_exchange([cast(w_out[0]), cast(up_t[0])], "gather_b", 1)
    (wdown0,) = _sequencer_exchange([cast(ffn_down[0])], "gather_c", 2)
    win1, wout1 = _sequencer_exchange([cast_in(w_in[1]), cast(w_out[1])], "gather_d", 3)
    wup1, wdown1 = _sequencer_exchange([cast(up_t[1]), cast(ffn_down[1])], "gather_e", 7)
    win, wup = [win0, win1], [wup0, wup1]
    wout = [w.reshape(D_MODEL, D_MODEL) for w in (wout0, wout1)]
    wdown = [w.reshape(N_UP_PAIRS, UP_CHUNK, D_MODEL) for w in (wdown0, wdown1)]
    fcw = [cw_all[:, k, :, :UP_CHUNK].reshape(2, N_UP_PAIRS, 3, UP_CHUNK) for k in range(DEPTH)]
    mcw = [cw_all[:, k, :, UP_CHUNK:].transpose(1, 0, 2).reshape(3, D_CONV) for k in range(DEPTH)]

    full = lambda i, j, k: (0, 0)

    saved = []
    xin = xf
    h1 = _rms_fwd(xin, norm1_g[0][None], "rms1_fwd_0")
    rows_of = lambda width: pl.BlockSpec((TM_ACC, width), lambda i: (i, 0))
    whole = lambda *shape: pl.BlockSpec(shape, lambda i: (0,) * len(shape))
    chunks_of = lambda n: pl.BlockSpec((n, TM_ACC, UP_CHUNK), lambda i: (0, i, 0))
    for l in range(DEPTH):
        proj = _matmul(
            h1, win[l], grid=(nt, D_IN // TN_IN, 1), dims=NN, name=f"proj_{l}",
            a_spec=pl.BlockSpec((TM, D_MODEL), lambda i, j, k: (i, 0)),
            b_spec=pl.BlockSpec((D_MODEL, TN_IN), lambda i, j, k: (0, j)),
            o_spec=pl.BlockSpec((TM, TN_IN), lambda i, j, k: (i, j)), o_shape=(t, D_IN), o_dtype=F32)
        o, lse, cat = _attn_fwd(proj, attn_out_g[l][None], nbatch, seq)
        cat = _convmix_fwd(proj, cat, mcw[l], conv_out_g[l][None], nbatch, seq)
        xmid, h2 = _matmul_norm(cat, wout[l], xin, norm2_g[l][None], dims=NN, name=f"mix_out_{l}",
                                a_spec=rows_of(D_MODEL), b_spec=whole(D_MODEL, D_MODEL))
        pre = _matmul(
            h2, wup[l], grid=(nt, N_DEV, 1), dims=NT, name=f"ffn_up_{l}",
            a_spec=pl.BlockSpec((TM, D_MODEL), lambda i, j, k: (i, 0)),
            b_spec=pl.BlockSpec((None, UP_CHUNK, D_MODEL), lambda i, j, k: (j, 0, 0)),
            o_spec=pl.BlockSpec((None, TM, UP_CHUNK), lambda i, j, k: (j, i, 0)),
            o_shape=(N_DEV, t, UP_CHUNK), o_dtype=BF16).reshape(2, N_UP_PAIRS, t, UP_CHUNK)
        if l + 1 < DEPTH:
            xout, h_next, act = _ffn_down(pre, fcw[l], wdown[l], xmid, norm1_g[l + 1][None], seq, f"ffn_down_{l}")
        else:
            h_next = None
            xout, act = _ffn_down(pre, fcw[l], wdown[l], xmid, None, seq, f"ffn_down_{l}")
        saved.append((xin, h1, proj, o, lse, cat, xmid, h2, pre, act))
        xin, h1 = xout, h_next

    loss_part, dx, dxb, dgf = _loss_head(xin, final_norm_g[None], target, "loss_head")

    dg1, dg2, dga, dgc = [None] * DEPTH, [None] * DEPTH, [None] * DEPTH, [None] * DEPTH
    for l in reversed(range(DEPTH)):
        xin, h1, proj, o, lse, cat, xmid, h2, pre, act = saved[l]
        g_down = _matmul(
            act, dxb, grid=(N_UP_PAIRS, 1, 1), dims=TN, name=f"g_down_{l}",
            a_spec=pl.BlockSpec((None, t, UP_CHUNK), lambda i, j, k: (i, 0, 0)),
            b_spec=pl.BlockSpec((t, D_MODEL), full),
            o_spec=pl.BlockSpec((None, UP_CHUNK, D_MODEL), lambda i, j, k: (i, 0, 0)),
            o_shape=(N_UP_PAIRS, UP_CHUNK, D_MODEL), o_dtype=BF16).reshape(N_DEV, down_rows, D_MODEL)
        d_pre, d_fcw, dxm, dxmb, dg2[l] = _ffn_up_bwd(
            pre, dxb, fcw[l], wdown[l], wup[l], xmid, norm2_g[l][None], dx, seq, f"ffn_bwd_{l}")
        d_pre = d_pre.reshape(N_DEV, t, UP_CHUNK)
        g_up = _matmul(
            d_pre, h2, grid=(N_DEV, 1, 1), dims=TN, name=f"g_up_{l}",
            a_spec=pl.BlockSpec((None, t, UP_CHUNK), lambda i, j, k: (i, 0, 0)),
            b_spec=pl.BlockSpec((t, D_MODEL), full),
            o_spec=pl.BlockSpec((None, UP_CHUNK, D_MODEL), lambda i, j, k: (i, 0, 0)),
            o_shape=(N_DEV, UP_CHUNK, D_MODEL), o_dtype=BF16)
        g_out = _matmul(
            cat, dxmb, grid=(1, 1, nt), dims=TN, name=f"g_out_{l}",
            a_spec=pl.BlockSpec((TM, D_MODEL), lambda i, j, k: (k, 0)),
            b_spec=pl.BlockSpec((TM, D_MODEL), lambda i, j, k: (k, 0)),
            o_spec=pl.BlockSpec((D_MODEL, D_MODEL), full),
            o_shape=(D_MODEL, D_MODEL), o_dtype=BF16).reshape(N_DEV, out_rows, D_MODEL)
        if l == 0:
            land_out0, land_up0, land_down0 = _sequencer_exchange(
                [_Item(g_out, True), _Item(g_up, True), _Item(g_down, True)], "scatter_0a", 5)
        d_cat = _matmul(
            dxmb, wout[l], grid=(nta, 1, 1), dims=NT, name=f"d_cat_{l}",
            a_spec=pl.BlockSpec((TM_ACC, D_MODEL), lambda i, j, k: (i, 0)),
            b_spec=pl.BlockSpec((D_MODEL, D_MODEL), full),
            o_spec=pl.BlockSpec((TM_ACC, D_MODEL), lambda i, j, k: (i, 0)), o_shape=(t, D_MODEL), o_dtype=BF16)
        d_proj, dga[l] = _attn_bwd(proj, o, lse, d_cat, attn_out_g[l][None], nbatch, seq)
        d_proj, d_mcw, dgc[l] = _convmix_bwd(proj, d_cat, d_proj, mcw[l], conv_out_g[l][None], nbatch, seq)
        g_in = _matmul(
            h1, d_proj, grid=(1, D_IN // TN_IN, 1), dims=TN, name=f"g_in_{l}",
            a_spec=pl.BlockSpec((t, D_MODEL), full),
            b_spec=pl.BlockSpec((t, TN_IN), lambda i, j, k: (0, j)),
            o_spec=pl.BlockSpec((D_MODEL, TN_IN), lambda i, j, k: (0, j)),
            o_shape=(D_MODEL, D_IN), o_dtype=BF16)
        g_cw = jnp.concatenate(
            [d_fcw.reshape(N_DEV, 3, UP_CHUNK), d_mcw.reshape(3, N_DEV, D_CONV // N_DEV).transpose(1, 0, 2)], axis=-1)
        if l == 0:
            land_in0, land_cw0 = _sequencer_exchange([_Item(g_in, "cols"), _Item(g_cw, True)], "scatter_0b", 6)
        else:
            land_in1, land_out1, land_up1, land_down1, land_cw1 = _sequencer_exchange(
                [_Item(g_in, "cols"), _Item(g_out, True), _Item(g_up, True), _Item(g_down, True), _Item(g_cw, True)],
                "scatter_1", 4)
        dx, dxb, dg1[l] = _matmul_norm_bwd(
            d_proj, win[l], xin, norm1_g[l][None], dxm, dims=NT, name=f"d_h1_{l}",
            a_spec=rows_of(D_IN), b_spec=whole(D_MODEL, D_IN))

    def pack_small(n1, a, c, n2, f):
        return jnp.concatenate(
            [n1, n2, f[None], jnp.concatenate([a, c], axis=-1), jnp.zeros((1, D_MODEL), F32)], axis=0)[None]

    small = jnp.concatenate(
        [dg1[0], dg1[1], dg2[0], dg2[1], dgf,
         jnp.concatenate([dga[0], dgc[0]], axis=-1), jnp.concatenate([dga[1], dgc[1]], axis=-1),
         jnp.pad(loss_part, ((0, 0), (0, D_MODEL - LANES)))], axis=0)
    (land_small,) = _exchange([_Item(small, False)], "gather_gain_grads")
    res_small = _adamw(
        [land_small], pack_small(norm1_g, attn_out_g, conv_out_g, norm2_g, final_norm_g),
        pack_small(m_norm1_g, m_attn_out_g, m_conv_out_g, m_norm2_g, m_final_norm_g),
        pack_small(v_norm1_g, v_attn_out_g, v_conv_out_g, v_norm2_g, v_final_norm_g), SUBLANES, "adamw_gains")
    res_out = _adamw([land_out0, land_out1], w_out, m_w_out, v_w_out, out_rows, "adamw_w_out", after=[res_small[0]])
    res_up_t = _adamw([land_up0, land_up1], up_t, m_up_t, v_up_t, UP_CHUNK // 4, "adamw_ffn_up", after=[res_out[0]])
    res_up = [jnp.swapaxes(r, 1, 2) for r in res_up_t]
    res_down = _adamw([land_down0, land_down1], ffn_down, m_ffn_down, v_ffn_down, down_rows, "adamw_ffn_down",
                      after=[res_up_t[0]])
    res_in = _adamw([land_in0, land_in1], w_in, m_w_in, v_w_in, 256, "adamw_w_in", after=[res_down[0]])
    res_cw = _adamw(
        [land_cw0, land_cw1], cw_local, jnp.concatenate([m_ffn_conv_w, m_mix_conv_w], axis=-1),
        jnp.concatenate([v_ffn_conv_w, v_mix_conv_w], axis=-1), 3, "adamw_conv_w", after=[res_in[0]])

    loss = res_small[0][0, SUBLANES - 1, 0]

    def unpack(kind):
        s = res_small[kind][0]
        cwr = res_cw[kind]
        return (s[0:2], res_in[kind], cwr[..., UP_CHUNK:], s[5:7, :D_ATTN], s[5:7, D_ATTN:], res_out[kind],
                s[2:4], res_up[kind], cwr[..., :UP_CHUNK], res_down[kind], s[4])

    return (loss, dx.reshape(nbatch, seq, d), *unpack(0), *unpack(1), *unpack(2), *unpack(3))
```

```python
import math

import jax
import jax.numpy as jnp
from jax import lax
from jax.experimental import pallas as pl
from jax.experimental.pallas import tpu as pltpu
from jax.experimental.pallas import tpu_sc as plsc

F32 = jnp.float32
BF16 = jnp.bfloat16

D_MODEL = 1024
D_ATTN = 512
D_CONV = 512
HEAD_DIM = 64
N_HEADS = 8
D_FF = 2816
DEPTH = 2
D_IN = 3 * D_ATTN + 3 * D_CONV
EPS = 1e-6
DILATIONS = (1, 4, 16)
BAND = 128
N_DEV = 8
IN_CHUNK = D_IN // N_DEV
UP_CHUNK = 2 * D_FF // N_DEV
N_UP_PAIRS = N_DEV // 2
CW_PACK = UP_CHUNK + D_CONV // N_DEV
ADAM_LR = 0.001
ADAM_B1 = 0.9
ADAM_B2 = 0.999
ADAM_EPS = 1e-08
ADAM_WD = 0.01
ADAM_STEP = 10
LANES = 128
SUBLANES = 8
VMEM_LIMIT = 56 * 1024 * 1024

NEG = -1e30
MESH = pl.DeviceIdType.MESH


def _params(sem=None, vmem=VMEM_LIMIT):
    return pltpu.CompilerParams(dimension_semantics=sem, vmem_limit_bytes=vmem)


NN = (((1,), (0,)), ((), ()))
NT = (((1,), (1,)), ((), ()))
TN = (((0,), (0,)), ((), ()))


def _contract(a_ref, b_ref, dims):
    def dot(av, bv):
        return lax.dot_general(av.astype(BF16), bv.astype(BF16), dims, preferred_element_type=F32)

    if len(a_ref.shape) == 2:
        return dot(a_ref[...], b_ref[...])
    part = dot(a_ref[0], b_ref[0])
    for c in range(1, a_ref.shape[0]):
        part = part + dot(a_ref[c], b_ref[c])
    return part


def _matmul(a, b, *, grid, a_spec, b_spec, o_spec, o_shape, o_dtype, dims, name, res=None, res_spec=None, after=()):
    nk = grid[2]
    o_block = tuple(s for s in o_spec.block_shape if s is not None)
    na = len(after)

    def body(*refs):
        refs = refs[:2 + (res is not None)] + refs[2 + (res is not None) + na:]
        if res is None:
            a_ref, b_ref, o_ref, *scr = refs
            r_ref = None
        else:
            a_ref, b_ref, r_ref, o_ref, *scr = refs
        part = _contract(a_ref, b_ref, dims)

        def finish(total):
            if r_ref is not None:
                total = total + r_ref[...]
            o_ref[...] = total.astype(o_dtype)

        if nk == 1:
            finish(part)
        else:
            acc = scr[0]
            k = pl.program_id(2)

            @pl.when(k == 0)
            def _():
                acc[...] = part

            @pl.when(k > 0)
            def _():
                acc[...] += part

            @pl.when(k == nk - 1)
            def _():
                finish(acc[...])

    in_specs = [a_spec, b_spec] + ([res_spec] if res is not None else []) + [pl.BlockSpec(memory_space=pl.ANY)] * na
    args = (a, b) + ((res,) if res is not None else ()) + tuple(after)
    return pl.pallas_call(
        body, name=name, grid=grid, in_specs=in_specs, out_specs=o_spec,
        out_shape=jax.ShapeDtypeStruct(o_shape, o_dtype),
        scratch_shapes=[pltpu.VMEM(o_block, F32)] if nk > 1 else [],
        compiler_params=_params(("parallel", "parallel", "arbitrary")),
    )(*args)


ROW_TILE = 512


def _rms_fwd(x, g, name):
    t, d = x.shape

    def body(x_ref, g_ref, h_ref):
        xv = x_ref[...]
        r = lax.rsqrt(jnp.mean(xv * xv, axis=-1, keepdims=True) + EPS)
        h_ref[...] = (xv * r * g_ref[...]).astype(BF16)

    return pl.pallas_call(
        body, name=name, grid=(t // ROW_TILE,),
        in_specs=[pl.BlockSpec((ROW_TILE, d), lambda i: (i, 0)), pl.BlockSpec((1, d), lambda i: (0, 0))],
        out_specs=pl.BlockSpec((ROW_TILE, d), lambda i: (i, 0)),
        out_shape=jax.ShapeDtypeStruct((t, d), BF16),
        compiler_params=_params(("parallel",)),
    )(x, g)


def _rms_bwd(x, g, dh, dres, name):
    t, d = x.shape

    def body(x_ref, g_ref, dh_ref, dres_ref, dx_ref, dxb_ref, dg_ref):
        xv = x_ref[...]
        r = lax.rsqrt(jnp.mean(xv * xv, axis=-1, keepdims=True) + EPS)
        xh = xv * r
        dhv = dh_ref[...]
        gd = dhv * g_ref[...]
        dx = r * (gd - xh * jnp.mean(gd * xh, axis=-1, keepdims=True)) + dres_ref[...]
        dx_ref[...] = dx
        dxb_ref[...] = dx.astype(BF16)
        part = jnp.sum(dhv * xh, axis=0, keepdims=True)

        @pl.when(pl.program_id(0) == 0)
        def _():
            dg_ref[...] = part

        @pl.when(pl.program_id(0) > 0)
        def _():
            dg_ref[...] += part

    row = pl.BlockSpec((ROW_TILE, d), lambda i: (i, 0))
    vec = pl.BlockSpec((1, d), lambda i: (0, 0))
    return pl.pallas_call(
        body, name=name, grid=(t // ROW_TILE,),
        in_specs=[row, vec, row, row], out_specs=[row, row, vec],
        out_shape=[jax.ShapeDtypeStruct((t, d), F32), jax.ShapeDtypeStruct((t, d), BF16),
                   jax.ShapeDtypeStruct((1, d), F32)],
        compiler_params=_params(("arbitrary",)),
    )(x, g, dh, dres)


def _matmul_norm(a, b, res, g, *, a_spec, b_spec, dims, name):
    t, d = res.shape

    def body(a_ref, b_ref, r_ref, g_ref, x_ref, h_ref):
        xv = _contract(a_ref, b_ref, dims) + r_ref[...]
        x_ref[...] = xv
        h_ref[...] = (xv * lax.rsqrt(jnp.mean(xv * xv, axis=-1, keepdims=True) + EPS) * g_ref[...]).astype(BF16)

    row = pl.BlockSpec((TM_ACC, d), lambda i: (i, 0))
    return pl.pallas_call(
        body, name=name, grid=(t // TM_ACC,),
        in_specs=[a_spec, b_spec, row, pl.BlockSpec((1, d), lambda i: (0, 0))], out_specs=[row, row],
        out_shape=[jax.ShapeDtypeStruct((t, d), F32), jax.ShapeDtypeStruct((t, d), BF16)],
        compiler_params=_params(("parallel",)),
    )(a, b, res, g)


def _matmul_norm_bwd(a, b, x, g, dres, *, a_spec, b_spec, dims, name):
    t, d = x.shape

    def body(a_ref, b_ref, x_ref, g_ref, dres_ref, dx_ref, dxb_ref, dg_ref):
        dhv = _contract(a_ref, b_ref, dims)
        xv = x_ref[...]
        r = lax.rsqrt(jnp.mean(xv * xv, axis=-1, keepdims=True) + EPS)
        xh = xv * r
        gd = dhv * g_ref[...]
        dx = r * (gd - xh * jnp.mean(gd * xh, axis=-1, keepdims=True)) + dres_ref[...]
        dx_ref[...] = dx
        dxb_ref[...] = dx.astype(BF16)
        part = jnp.sum(dhv * xh, axis=0, keepdims=True)

        @pl.when(pl.program_id(0) == 0)
        def _():
            dg_ref[...] = part

        @pl.when(pl.program_id(0) > 0)
        def _():
            dg_ref[...] += part

    row = pl.BlockSpec((TM_ACC, d), lambda i: (i, 0))
    vec = pl.BlockSpec((1, d), lambda i: (0, 0))
    return pl.pallas_call(
        body, name=name, grid=(t // TM_ACC,),
        in_specs=[a_spec, b_spec, row, vec, row], out_specs=[row, row, vec],
        out_shape=[jax.ShapeDtypeStruct((t, d), F32), jax.ShapeDtypeStruct((t, d), BF16),
                   jax.ShapeDtypeStruct((1, d), F32)],
        compiler_params=_params(("arbitrary",)),
    )(a, b, x, g, dres)


def _loss_head(x, g, target, name):
    t, d = x.shape

    def body(x_ref, g_ref, t_ref, loss_ref, dx_ref, dxb_ref, dg_ref):
        xv = x_ref[...]
        r = lax.rsqrt(jnp.mean(xv * xv, axis=-1, keepdims=True) + EPS)
        xh = xv * r
        gv = g_ref[...]
        err = xh * gv - t_ref[...]
        loss = jnp.full((1, LANES), 0.5 / d, F32) * jnp.sum(err * err)
        dy = err * (1.0 / d)
        gd = dy * gv
        dx = r * (gd - xh * jnp.mean(gd * xh, axis=-1, keepdims=True))
        dx_ref[...] = dx
        dxb_ref[...] = dx.astype(BF16)
        part = jnp.sum(dy * xh, axis=0, keepdims=True)

        @pl.when(pl.program_id(0) == 0)
        def _():
            dg_ref[...] = part
            loss_ref[...] = loss

        @pl.when(pl.program_id(0) > 0)
        def _():
            dg_ref[...] += part
            loss_ref[...] += loss

    row = pl.BlockSpec((ROW_TILE, d), lambda i: (i, 0))
    vec = pl.BlockSpec((1, d), lambda i: (0, 0))
    return pl.pallas_call(
        body, name=name, grid=(t // ROW_TILE,),
        in_specs=[row, vec, row],
        out_specs=[pl.BlockSpec((1, LANES), lambda i: (0, 0)), row, row, vec],
        out_shape=[jax.ShapeDtypeStruct((1, LANES), F32), jax.ShapeDtypeStruct((t, d), F32),
                   jax.ShapeDtypeStruct((t, d), BF16), jax.ShapeDtypeStruct((1, d), F32)],
        compiler_params=_params(("arbitrary",)),
    )(x, g, target)


def _group_matrix(n):
    shift = int(math.log2(HEAD_DIM))
    r = lax.broadcasted_iota(jnp.int32, (n, n), 0) >> shift
    c = lax.broadcasted_iota(jnp.int32, (n, n), 1) >> shift
    return (r == c).astype(BF16)


def _group_sum(v, gmat):
    hi = v.astype(BF16)
    lo = (v - hi.astype(F32)).astype(BF16)

    def dot(p):
        return jnp.dot(p, gmat, preferred_element_type=F32)

    return dot(hi) + dot(lo)


def _shift_rows(ext, k):
    return pltpu.roll(ext, k % ext.shape[0], 0)


def _store_columns(stage, out_hbm, sems, row0, nrows, col_blocks):
    rows = pl.ds(pl.multiple_of(row0, SUBLANES * 2), nrows)
    copies = [
        pltpu.make_async_copy(stage.at[i], out_hbm.at[rows, pl.ds(pl.multiple_of(cb * LANES, LANES), LANES)], sems.at[i])
        for i, cb in enumerate(col_blocks)
    ]
    for cp in copies:
        cp.start()
    for cp in copies:
        cp.wait()


def _attn_consts(width):
    i = lax.broadcasted_iota(jnp.int32, (BAND, width), 0)
    j = lax.broadcasted_iota(jnp.int32, (BAND, width), 1)
    dist = (width - BAND) + i - j
    inwin = (dist >= 0) & (dist <= BAND)
    return dist.astype(F32), inwin, j


def _head_masks():
    lane = lax.broadcasted_iota(jnp.int32, (1, LANES), 1)
    return [(lane < HEAD_DIM).astype(F32), (lane >= HEAD_DIM).astype(F32)]


def _pair_bias(slope, dil):
    distf, inwin, _ = _attn_consts(2 * BAND)
    return jnp.concatenate([jnp.where(inwin, distf * (slope[hh] * (-float(dil))), NEG) for hh in range(2)], axis=0)


def _stack_heads(xv, hmask):
    return jnp.concatenate([xv * hmask[0], xv * hmask[1]], axis=0).astype(BF16)


FWD_UNROLL = 8
BWD_UNROLL = 8


def _unroll(trips, most):
    return max(u for u in range(1, most + 1) if trips % u == 0)


def _for_blocks(seq, dil, block, most):
    nb = seq // dil // BAND

    def residue(r, carry):
        base = r * nb
        block(pl.multiple_of(base * BAND, BAND), None)
        if nb > 1:
            def rest(n, c):
                block(pl.multiple_of((base + n) * BAND, BAND), pl.multiple_of((base + n - 1) * BAND, BAND))
                return c

            lax.fori_loop(1, nb, rest, 0, unroll=_unroll(nb - 1, most))
        return carry

    if dil == 1:
        residue(0, 0)
    else:
        lax.fori_loop(0, dil, residue, 0, unroll=_unroll(dil, max(1, most // nb)))


def _permute_in(src_ref, dst_ref, dil, seq):
    length = seq // dil
    for r in range(dil):
        dst_ref[pl.ds(r * length, length), :] = src_ref[pl.ds(r, length, stride=dil), :].astype(dst_ref.dtype)


def _slopes_table():
    slopes = 2.0 ** (-8.0 * jnp.arange(1, N_HEADS + 1, dtype=F32) / N_HEADS)
    return jnp.broadcast_to(slopes[:, None], (N_HEADS, 2 * BAND))


def _attn_fwd(proj, attn_g, nbatch, seq):
    t = nbatch * seq
    scale = HEAD_DIM ** -0.5

    def body(q_ref, k_ref, v_ref, g_ref, sl_ref, o_ref, lse_ref, cat_ref, pq, pk, pv, po, pm, pll, ao, am, al):
        hp = pl.program_id(1)
        hmask = _head_masks()
        slope = [sl_ref[pl.ds(2 * hp + hh, 1), :] for hh in range(2)]

        def run_branch(dil, qs, ks, vs, osink, msink, lsink):
            bias = _pair_bias(slope, dil)

            def block(row0, prow):
                cur = pl.ds(row0, BAND)
                q2 = _stack_heads(qs[cur, :] * scale, hmask)
                if prow is None:
                    kk, vv, bias_b = ks[cur, :], vs[cur, :], bias[:, BAND:]
                else:
                    prev = pl.ds(prow, BAND)
                    kk = jnp.concatenate([ks[prev, :], ks[cur, :]], axis=0)
                    vv = jnp.concatenate([vs[prev, :], vs[cur, :]], axis=0)
                    bias_b = bias
                s = lax.dot_general(q2, kk.astype(BF16), NT, preferred_element_type=F32) + bias_b
                m = jnp.max(s, axis=1, keepdims=True)
                p = jnp.exp(s - m)
                l = jnp.sum(p, axis=1, keepdims=True)
                pb = p.astype(BF16)
                o = jnp.dot(jnp.concatenate([pb[:BAND], pb[BAND:]], axis=1), _stack_heads(vv, hmask),
                            preferred_element_type=F32)
                osink[cur, :] = o
                msink[cur, :] = m[:BAND] * hmask[0] + m[BAND:] * hmask[1]
                lsink[cur, :] = l[:BAND] * hmask[0] + l[BAND:] * hmask[1]

            _for_blocks(seq, dil, block, FWD_UNROLL)

        run_branch(1, q_ref, k_ref, v_ref, ao, am, al)
        for dil in DILATIONS[1:]:
            length = seq // dil
            _permute_in(q_ref, pq, dil, seq)
            _permute_in(k_ref, pk, dil, seq)
            _permute_in(v_ref, pv, dil, seq)
            run_branch(dil, pq, pk, pv, po, pm, pll)
            for r in range(dil):
                nat = pl.ds(r, length, stride=dil)
                per = pl.ds(r * length, length)
                m0 = am[nat, :]
                mb = pm[per, :]
                mn = jnp.maximum(m0, mb)
                e0 = jnp.exp(m0 - mn)
                eb = jnp.exp(mb - mn)
                ao[nat, :] = ao[nat, :] * e0 + po[per, :] * eb
                al[nat, :] = al[nat, :] * e0 + pll[per, :] * eb
                am[nat, :] = mn

        gmat = _group_matrix(LANES)
        gv = g_ref[...]

        def fin(c, carry):
            rows = pl.ds(pl.multiple_of(c * 256, 256), 256)
            lv = al[rows, :]
            o = ao[rows, :] / lv
            o_ref[rows, :] = o
            lse_ref[rows, :] = am[rows, :] + jnp.log(lv)
            ms = _group_sum(o * o, gmat) * (1.0 / HEAD_DIM)
            cat_ref[rows, :] = (o * lax.rsqrt(ms + EPS) * gv).astype(BF16)
            return carry

        lax.fori_loop(0, seq // 256, fin, 0)

    nq = D_ATTN // LANES
    blk = lambda off: pl.BlockSpec((seq, LANES), lambda b, h: (b, h + off))
    scratch = [pltpu.VMEM((seq, LANES), F32) for _ in range(9)]
    return pl.pallas_call(
        body, name="attn_fwd", grid=(nbatch, nq),
        in_specs=[blk(0), blk(nq), blk(2 * nq), pl.BlockSpec((1, LANES), lambda b, h: (0, h)),
                  pl.BlockSpec((N_HEADS, 2 * BAND), lambda b, h: (0, 0))],
        out_specs=[blk(0), blk(0), blk(0)],
        out_shape=[jax.ShapeDtypeStruct((t, D_ATTN), F32), jax.ShapeDtypeStruct((t, D_ATTN), F32),
                   jax.ShapeDtypeStruct((t, D_MODEL), BF16)],
        scratch_shapes=scratch,
        compiler_params=_params(("parallel", "parallel")),
    )(proj, proj, proj, attn_g, _slopes_table())


def _attn_bwd(proj, o, lse, d_cat, attn_g, nbatch, seq):
    t = nbatch * seq
    scale = HEAD_DIM ** -0.5

    def body(q_ref, k_ref, v_ref, o_ref, lse_ref, dy_ref, g_ref, sl_ref, dproj_ref, dg_ref,
             do_n, dl_n, dq_n, dk_n, dv_n, pq, pk, pv, pdo, plse, pdl, pdq, pdk, pdv, stage, sems):
        hp = pl.program_id(0)
        hmask = _head_masks()
        slope = [sl_ref[pl.ds(2 * hp + hh, 1), :] for hh in range(2)]
        gmat = _group_matrix(LANES)
        gv = g_ref[...]

        def prep(c, dg):
            rows = pl.ds(pl.multiple_of(c * 256, 256), 256)
            ov = o_ref[rows, :]
            dyn = dy_ref[rows, :].astype(F32)
            r = lax.rsqrt(_group_sum(ov * ov, gmat) * (1.0 / HEAD_DIM) + EPS)
            gd = dyn * gv
            oh = ov * r
            do = r * (gd - oh * (_group_sum(gd * oh, gmat) * (1.0 / HEAD_DIM)))
            do_n[rows, :] = do
            dl_n[rows, :] = _group_sum(do * ov, gmat)
            return dg + jnp.sum(dyn * oh, axis=0, keepdims=True)

        dg = lax.fori_loop(0, seq // 256, prep, jnp.zeros((1, LANES), F32))

        @pl.when(pl.program_id(1) == 0)
        def _():
            dg_ref[...] = dg

        @pl.when(pl.program_id(1) > 0)
        def _():
            dg_ref[...] += dg

        def clear(*refs):
            def step(c, carry):
                rows = pl.ds(pl.multiple_of(c * 256, 256), 256)
                for ref in refs:
                    ref[rows, :] = jnp.zeros((256, LANES), F32)
                return carry

            lax.fori_loop(0, seq // 256, step, 0)

        clear(dq_n, dk_n, dv_n)

        def run_branch(dil, qs, ks, vs, dos, lses, dls, dqs, dks, dvs):
            bias = _pair_bias(slope, dil)

            def per_head(xv):
                return jnp.concatenate([xv[:, 0:1], xv[:, HEAD_DIM:HEAD_DIM + 1]], axis=0)

            def block(row0, prow):
                cur = pl.ds(row0, BAND)
                keys = cur if prow is None else pl.ds(prow, 2 * BAND)
                q2 = _stack_heads(qs[cur, :] * scale, hmask)
                do2 = _stack_heads(dos[cur, :], hmask)
                kk, vv = ks[keys, :], vs[keys, :]
                s = lax.dot_general(q2, kk.astype(BF16), NT, preferred_element_type=F32)
                s = s + (bias[:, BAND:] if prow is None else bias)
                p = jnp.exp(s - per_head(lses[cur, :]))
                dp = lax.dot_general(do2, vv.astype(BF16), NT, preferred_element_type=F32)
                ds = (p * (dp - per_head(dls[cur, :]))).astype(BF16)
                dqs[cur, :] += jnp.dot(jnp.concatenate([ds[:BAND], ds[BAND:]], axis=1), _stack_heads(kk, hmask),
                                       preferred_element_type=F32)
                dks[keys, :] += lax.dot_general(ds, q2, TN, preferred_element_type=F32)
                dvs[keys, :] += lax.dot_general(p.astype(BF16), do2, TN, preferred_element_type=F32)

            _for_blocks(seq, dil, block, BWD_UNROLL)

        run_branch(1, q_ref, k_ref, v_ref, do_n, lse_ref, dl_n, dq_n, dk_n, dv_n)
        for dil in DILATIONS[1:]:
            length = seq // dil
            for src, dst in ((q_ref, pq), (k_ref, pk), (v_ref, pv), (do_n, pdo), (lse_ref, plse), (dl_n, pdl)):
                _permute_in(src, dst, dil, seq)
            clear(pdq, pdk, pdv)
            run_branch(dil, pq, pk, pv, pdo, plse, pdl, pdq, pdk, pdv)
            for r in range(dil):
                nat = pl.ds(r, length, stride=dil)
                per = pl.ds(r * length, length)
                dq_n[nat, :] += pdq[per, :]
                dk_n[nat, :] += pdk[per, :]
                dv_n[nat, :] += pdv[per, :]

        def emit(c, carry):
            rows = pl.ds(pl.multiple_of(c * 256, 256), 256)
            stage[0, rows, :] = (dq_n[rows, :] * scale).astype(BF16)
            stage[1, rows, :] = dk_n[rows, :].astype(BF16)
            stage[2, rows, :] = dv_n[rows, :].astype(BF16)
            return carry

        lax.fori_loop(0, seq // 256, emit, 0)
        _store_columns(stage, dproj_ref, sems, pl.program_id(1) * seq, seq, [hp, nq + hp, 2 * nq + hp])

    nq = D_ATTN // LANES
    blk = lambda off: pl.BlockSpec((seq, LANES), lambda h, b: (b, h + off))
    vec = pl.BlockSpec((1, LANES), lambda h, b: (0, h))
    scratch = [pltpu.VMEM((seq, LANES), F32) for _ in range(14)]
    scratch += [pltpu.VMEM((3, seq, LANES), BF16), pltpu.SemaphoreType.DMA((3,))]
    d_proj, dg = pl.pallas_call(
        body, name="attn_bwd", grid=(nq, nbatch),
        in_specs=[blk(0), blk(nq), blk(2 * nq), blk(0), blk(0), blk(0), vec,
                  pl.BlockSpec((N_HEADS, 2 * BAND), lambda h, b: (0, 0))],
        out_specs=[pl.BlockSpec(memory_space=pl.ANY), vec],
        out_shape=[jax.ShapeDtypeStruct((t, D_IN), BF16), jax.ShapeDtypeStruct((1, D_ATTN), F32)],
        scratch_shapes=scratch,
        compiler_params=_params(("arbitrary", "arbitrary")),
    )(proj, proj, proj, o, lse, d_cat, attn_g, _slopes_table())
    return d_proj, dg


HALO = SUBLANES
PACKED_ROWS = 2 * SUBLANES


def _window(ref, c, rows, nchunks, after):
    row0 = pl.multiple_of(c * rows, rows)
    prev0 = pl.multiple_of(jnp.maximum(row0 - PACKED_ROWS, 0), PACKED_ROWS)
    before = ref[pl.ds(prev0, PACKED_ROWS), :].astype(F32)[PACKED_ROWS - HALO:] * (c > 0).astype(F32)
    parts = [before, ref[pl.ds(row0, rows), :].astype(F32)]
    if after:
        next0 = pl.multiple_of(jnp.minimum(row0 + rows, (nchunks - 1) * rows), PACKED_ROWS)
        parts.append(ref[pl.ds(next0, PACKED_ROWS), :].astype(F32)[:HALO] * (c < nchunks - 1).astype(F32))
    return jnp.concatenate(parts, axis=0)


def _behind(z):
    z1 = _shift_rows(z, 1)
    return z1, _shift_rows(z1, 1)


def _ahead(dy):
    d1 = _shift_rows(dy, -1)
    return d1, _shift_rows(d1, -1)


def _conv(z, w):
    z1, z2 = _behind(z)
    return w[0:1] * z2 + w[1:2] * z1 + w[2:3] * z


def _conv_bwd(dy, z, w, cur):
    d1, d2 = _ahead(dy)
    dz = w[2:3] * dy + w[1:2] * d1 + w[0:1] * d2
    return dz, [jnp.sum((d * z)[cur], axis=0, keepdims=True) for d in (d2, d1, dy)]


def _sigmoid(a):
    return 0.5 * jnp.tanh(0.5 * a) + 0.5


MIX_ROWS = 256
GATE_B_BLOCK = 3 * D_ATTN // LANES
GATE_C_BLOCK = GATE_B_BLOCK + D_CONV // LANES
U_BLOCK = GATE_C_BLOCK + D_CONV // LANES


def _convmix_fwd(proj, cat, mcw, conv_g, nbatch, seq):
    nchunks = seq // MIX_ROWS

    def body(gb_ref, gc_ref, u_ref, w_ref, g_ref, cat_in, cat_ref):
        del cat_in
        gmat = _group_matrix(LANES)
        w = w_ref[...]
        gv = g_ref[...]

        def step(c, carry):
            cur = pl.ds(pl.multiple_of(c * MIX_ROWS, MIX_ROWS), MIX_ROWS)
            z = _window(gc_ref, c, MIX_ROWS, nchunks, False) * _window(u_ref, c, MIX_ROWS, nchunks, False)
            y = gb_ref[cur, :] * _conv(z, w)[HALO:]
            ms = _group_sum(y * y, gmat) * (1.0 / HEAD_DIM)
            cat_ref[cur, :] = (y * lax.rsqrt(ms + EPS) * gv).astype(BF16)
            return carry

        lax.fori_loop(0, nchunks, step, 0)

    nc = D_CONV // LANES
    blk = lambda off: pl.BlockSpec((seq, LANES), lambda b, j: (b, j + off))
    return pl.pallas_call(
        body, name="convmix_fwd", grid=(nbatch, nc),
        in_specs=[blk(GATE_B_BLOCK), blk(GATE_C_BLOCK), blk(U_BLOCK),
                  pl.BlockSpec((3, LANES), lambda b, j: (0, j)), pl.BlockSpec((1, LANES), lambda b, j: (0, j)),
                  pl.BlockSpec(memory_space=pl.ANY)],
        out_specs=blk(D_ATTN // LANES),
        out_shape=jax.ShapeDtypeStruct(cat.shape, cat.dtype),
        input_output_aliases={5: 0},
        compiler_params=_params(("parallel", "parallel")),
    )(proj, proj, proj, mcw, conv_g, cat)


def _convmix_bwd(proj, d_cat, d_proj, mcw, conv_g, nbatch, seq):
    nchunks = seq // MIX_ROWS

    def body(gb_ref, gc_ref, u_ref, dy_ref, w_ref, g_ref, dproj_in, dproj_ref, dw_ref, dg_ref, stage, sems):
        del dproj_in
        cb = pl.program_id(0)
        b = pl.program_id(1)
        gmat = _group_matrix(LANES)
        w = w_ref[...]
        gv = g_ref[...]
        cur = slice(HALO, HALO + MIX_ROWS)

        def step(c, carry):
            rows = pl.ds(pl.multiple_of(c * MIX_ROWS, MIX_ROWS), MIX_ROWS)
            gb = _window(gb_ref, c, MIX_ROWS, nchunks, True)
            gc = _window(gc_ref, c, MIX_ROWS, nchunks, True)
            u = _window(u_ref, c, MIX_ROWS, nchunks, True)
            dyn = _window(dy_ref, c, MIX_ROWS, nchunks, True)
            z = gc * u
            conv = _conv(z, w)
            y = gb * conv
            r = lax.rsqrt(_group_sum(y * y, gmat) * (1.0 / HEAD_DIM) + EPS)
            yh = y * r
            gd = dyn * gv
            dy = r * (gd - yh * (_group_sum(gd * yh, gmat) * (1.0 / HEAD_DIM)))
            dz, dws = _conv_bwd(dy * gb, z, w, cur)
            stage[0, rows, :] = (dy * conv)[cur].astype(BF16)
            stage[1, rows, :] = (dz * u)[cur].astype(BF16)
            stage[2, rows, :] = (dz * gc)[cur].astype(BF16)
            dg = jnp.sum((dyn * yh)[cur], axis=0, keepdims=True)
            return tuple(a + d for a, d in zip(carry, dws + [dg]))

        zero = jnp.zeros((1, LANES), F32)
        dw0, dw1, dw2, dg = lax.fori_loop(0, nchunks, step, (zero, zero, zero, zero))

        @pl.when(b == 0)
        def _():
            dw_ref[0:1, :] = dw0
            dw_ref[1:2, :] = dw1
            dw_ref[2:3, :] = dw2
            dg_ref[...] = dg

        @pl.when(b > 0)
        def _():
            dw_ref[0:1, :] += dw0
            dw_ref[1:2, :] += dw1
            dw_ref[2:3, :] += dw2
            dg_ref[...] += dg

        _store_columns(stage, dproj_ref, sems, b * seq, seq, [GATE_B_BLOCK + cb, GATE_C_BLOCK + cb, U_BLOCK + cb])

    nc = D_CONV // LANES
    blk = lambda off: pl.BlockSpec((seq, LANES), lambda j, b: (b, j + off))
    return pl.pallas_call(
        body, name="convmix_bwd", grid=(nc, nbatch),
        in_specs=[blk(GATE_B_BLOCK), blk(GATE_C_BLOCK), blk(U_BLOCK), blk(D_ATTN // LANES),
                  pl.BlockSpec((3, LANES), lambda j, b: (0, j)), pl.BlockSpec((1, LANES), lambda j, b: (0, j)),
                  pl.BlockSpec(memory_space=pl.ANY)],
        out_specs=[pl.BlockSpec(memory_space=pl.ANY), pl.BlockSpec((3, LANES), lambda j, b: (0, j)),
                   pl.BlockSpec((1, LANES), lambda j, b: (0, j))],
        out_shape=[jax.ShapeDtypeStruct(d_proj.shape, d_proj.dtype), jax.ShapeDtypeStruct((3, D_CONV), F32),
                   jax.ShapeDtypeStruct((1, D_CONV), F32)],
        scratch_shapes=[pltpu.VMEM((3, seq, LANES), BF16), pltpu.SemaphoreType.DMA((3,))],
        input_output_aliases={6: 0},
        compiler_params=_params(("arbitrary", "arbitrary")),
    )(proj, proj, proj, d_cat, mcw, conv_g, d_proj)


FFN_ROWS = 256


def _ffn_act_fwd(pre, fcw, nbatch, seq):
    t = nbatch * seq
    nchunks = seq // FFN_ROWS

    def body(pre_ref, w_ref, act_ref):
        wa = w_ref[0]
        wc = w_ref[1]

        def step(c, carry):
            cur = pl.ds(pl.multiple_of(c * FFN_ROWS, FFN_ROWS), FFN_ROWS)
            a = _conv(_window(pre_ref.at[0], c, FFN_ROWS, nchunks, False), wa)[HALO:]
            v = _conv(_window(pre_ref.at[1], c, FFN_ROWS, nchunks, False), wc)[HALO:]
            act_ref[cur, :] = (a * _sigmoid(a) * v).astype(BF16)
            return carry

        lax.fori_loop(0, nchunks, step, 0)

    return pl.pallas_call(
        body, name="ffn_act_fwd", grid=(N_UP_PAIRS, nbatch),
        in_specs=[pl.BlockSpec((2, None, seq, UP_CHUNK), lambda i, b: (0, i, b, 0)),
                  pl.BlockSpec((2, None, 3, UP_CHUNK), lambda i, b: (0, i, 0, 0))],
        out_specs=pl.BlockSpec((None, seq, UP_CHUNK), lambda i, b: (i, b, 0)),
        out_shape=jax.ShapeDtypeStruct((N_UP_PAIRS, t, UP_CHUNK), BF16),
        compiler_params=_params(("parallel", "parallel")),
    )(pre, fcw)


def _ffn_down(pre, fcw, wdown, res, g, seq, name):
    t, d = res.shape
    tiles_per_seq = seq // FFN_ROWS

    def body(main_ref, halo_ref, w_ref, wd_ref, r_ref, *rest):
        if g is None:
            x_ref, act_ref = rest
        else:
            g_ref, x_ref, h_ref, act_ref = rest
        inside = ((pl.program_id(0) % tiles_per_seq) > 0).astype(F32)

        def window(part, p):
            before = halo_ref[part, p].astype(F32)[PACKED_ROWS - HALO:] * inside
            return jnp.concatenate([before, main_ref[part, p].astype(F32)], axis=0)

        total = r_ref[...]
        for p in range(N_UP_PAIRS):
            a = _conv(window(0, p), w_ref[0, p])[HALO:]
            v = _conv(window(1, p), w_ref[1, p])[HALO:]
            act = (a * _sigmoid(a) * v).astype(BF16)
            act_ref[p] = act
            total = total + jnp.dot(act, wd_ref[p], preferred_element_type=F32)
        x_ref[...] = total
        if g is not None:
            h_ref[...] = (total * lax.rsqrt(jnp.mean(total * total, axis=-1, keepdims=True) + EPS) * g_ref[...]).astype(BF16)

    row = pl.BlockSpec((FFN_ROWS, d), lambda i: (i, 0))
    tiles_per_halo = FFN_ROWS // PACKED_ROWS
    in_specs = [
        pl.BlockSpec((2, N_UP_PAIRS, FFN_ROWS, UP_CHUNK), lambda i: (0, 0, i, 0)),
        pl.BlockSpec((2, N_UP_PAIRS, PACKED_ROWS, UP_CHUNK), lambda i: (0, 0, jnp.maximum(i * tiles_per_halo - 1, 0), 0)),
        pl.BlockSpec((2, N_UP_PAIRS, 3, UP_CHUNK), lambda i: (0, 0, 0, 0)),
        pl.BlockSpec((N_UP_PAIRS, UP_CHUNK, d), lambda i: (0, 0, 0)), row]
    out_specs = [row]
    out_shape = [jax.ShapeDtypeStruct((t, d), F32)]
    args = [pre, pre, fcw, wdown, res]
    if g is not None:
        in_specs.append(pl.BlockSpec((1, d), lambda i: (0, 0)))
        out_specs.append(row)
        out_shape.append(jax.ShapeDtypeStruct((t, d), BF16))
        args.append(g)
    out_specs.append(pl.BlockSpec((N_UP_PAIRS, FFN_ROWS, UP_CHUNK), lambda i: (0, i, 0)))
    out_shape.append(jax.ShapeDtypeStruct((N_UP_PAIRS, t, UP_CHUNK), BF16))
    return pl.pallas_call(
        body, name=name, grid=(t // FFN_ROWS,), in_specs=in_specs, out_specs=out_specs, out_shape=out_shape,
        compiler_params=_params(("parallel",)),
    )(*args)


def _ffn_fwd(h, wup, fcw, wdown, res, g, seq, name):
    t, d = res.shape
    tiles_per_seq = seq // FFN_ROWS

    def body(hm_ref, hp_ref, wu_ref, w_ref, wd_ref, r_ref, *rest):
        if g is None:
            x_ref, act_ref, pre_ref = rest
        else:
            g_ref, x_ref, h_ref, act_ref, pre_ref = rest
        inside = ((pl.program_id(0) % tiles_per_seq) > 0).astype(F32)
        wrow = lax.broadcasted_iota(jnp.int32, (FFN_ROWS + HALO, 1), 0)
        edge = jnp.where(wrow < HALO, inside, 1.0)
        rows = jnp.concatenate([hp_ref[...], hm_ref[...]], axis=0)

        def up(j, part, p):
            full = lax.dot_general(rows, wu_ref[j], NT, preferred_element_type=F32).astype(BF16)
            pre_ref[part, p] = full[PACKED_ROWS:]
            return full.astype(F32)[PACKED_ROWS - HALO:] * edge

        total = r_ref[...]
        for p in range(N_UP_PAIRS):
            a = _conv(up(p, 0, p), w_ref[0, p])[HALO:]
            v = _conv(up(N_UP_PAIRS + p, 1, p), w_ref[1, p])[HALO:]
            act = (a * _sigmoid(a) * v).astype(BF16)
            act_ref[p] = act
            total = total + jnp.dot(act, wd_ref[p], preferred_element_type=F32)
        x_ref[...] = total
        if g is not None:
            h_ref[...] = (total * lax.rsqrt(jnp.mean(total * total, axis=-1, keepdims=True) + EPS) * g_ref[...]).astype(BF16)

    row = pl.BlockSpec((FFN_ROWS, d), lambda i: (i, 0))
    tiles_per_halo = FFN_ROWS // PACKED_ROWS
    in_specs = [
        row, pl.BlockSpec((PACKED_ROWS, d), lambda i: (jnp.maximum(i * tiles_per_halo - 1, 0), 0)),
        pl.BlockSpec((N_DEV, UP_CHUNK, d), lambda i: (0, 0, 0)),
        pl.BlockSpec((2, N_UP_PAIRS, 3, UP_CHUNK), lambda i: (0, 0, 0, 0)),
        pl.BlockSpec((N_UP_PAIRS, UP_CHUNK, d), lambda i: (0, 0, 0)), row]
    out_specs = [row]
    out_shape = [jax.ShapeDtypeStruct((t, d), F32)]
    args = [h, h, wup, fcw, wdown, res]
    if g is not None:
        in_specs.append(pl.BlockSpec((1, d), lambda i: (0, 0)))
        out_specs.append(row)
        out_shape.append(jax.ShapeDtypeStruct((t, d), BF16))
        args.append(g)
    out_specs += [pl.BlockSpec((N_UP_PAIRS, FFN_ROWS, UP_CHUNK), lambda i: (0, i, 0)),
                  pl.BlockSpec((2, N_UP_PAIRS, FFN_ROWS, UP_CHUNK), lambda i: (0, 0, i, 0))]
    out_shape += [jax.ShapeDtypeStruct((N_UP_PAIRS, t, UP_CHUNK), BF16),
                  jax.ShapeDtypeStruct((2, N_UP_PAIRS, t, UP_CHUNK), BF16)]
    return pl.pallas_call(
        body, name=name, grid=(t // FFN_ROWS,), in_specs=in_specs, out_specs=out_specs, out_shape=out_shape,
        compiler_params=_params(("parallel",)),
    )(*args)


def _ffn_act_bwd(pre, d_act, fcw, nbatch, seq):
    nchunks = seq // FFN_ROWS

    def body(pre_ref, da_ref, w_ref, dpre_ref, dw_ref):
        b = pl.program_id(1)
        wa = w_ref[0]
        wc = w_ref[1]
        cur = slice(HALO, HALO + FFN_ROWS)

        def step(c, carry):
            rows = pl.ds(pl.multiple_of(c * FFN_ROWS, FFN_ROWS), FFN_ROWS)
            pg = _window(pre_ref.at[0], c, FFN_ROWS, nchunks, True)
            pv = _window(pre_ref.at[1], c, FFN_ROWS, nchunks, True)
            dact = _window(da_ref, c, FFN_ROWS, nchunks, True)
            a = _conv(pg, wa)
            v = _conv(pv, wc)
            sg = _sigmoid(a)
            asg = a * sg
            dzg, dwg = _conv_bwd(dact * v * (sg + asg - asg * sg), pg, wa, cur)
            dzv, dwv = _conv_bwd(dact * asg, pv, wc, cur)
            dpre_ref[0, rows, :] = dzg[cur].astype(BF16)
            dpre_ref[1, rows, :] = dzv[cur].astype(BF16)
            return tuple(acc + d for acc, d in zip(carry, dwg + dwv))

        zero = jnp.zeros((1, UP_CHUNK), F32)
        sums = lax.fori_loop(0, nchunks, step, (zero,) * 6)

        @pl.when(b == 0)
        def _():
            for i in range(6):
                dw_ref[i // 3, pl.ds(i % 3, 1), :] = sums[i]

        @pl.when(b > 0)
        def _():
            for i in range(6):
                dw_ref[i // 3, pl.ds(i % 3, 1), :] += sums[i]

    pair = pl.BlockSpec((2, None, seq, UP_CHUNK), lambda i, b: (0, i, b, 0))
    wspec = pl.BlockSpec((2, None, 3, UP_CHUNK), lambda i, b: (0, i, 0, 0))
    return pl.pallas_call(
        body, name="ffn_act_bwd", grid=(N_UP_PAIRS, nbatch),
        in_specs=[pair, pl.BlockSpec((None, seq, UP_CHUNK), lambda i, b: (i, b, 0)), wspec],
        out_specs=[pair, wspec],
        out_shape=[jax.ShapeDtypeStruct(pre.shape, BF16), jax.ShapeDtypeStruct(fcw.shape, F32)],
        compiler_params=_params(("parallel", "arbitrary")),
    )(pre, d_act, fcw)


def _ffn_up_bwd(pre, dy, fcw, wdown, wup, x, g, dres, seq, name):
    t, d = x.shape
    tiles_per_seq = seq // FFN_ROWS
    tiles_per_halo = FFN_ROWS // PACKED_ROWS
    last_halo = t // PACKED_ROWS - 1

    def body(pm_ref, pp_ref, pn_ref, dm_ref, dp_ref, dn_ref, w_ref, wd_ref, wu_ref, x_ref, g_ref, dres_ref,
             dpre_ref, dw_ref, dx_ref, dxb_ref, dg_ref):
        i = pl.program_id(0)
        has_prev = ((i % tiles_per_seq) > 0).astype(F32)
        has_next = ((i % tiles_per_seq) < tiles_per_seq - 1).astype(F32)
        cur = slice(HALO, HALO + FFN_ROWS)

        def window(before, main, after):
            return jnp.concatenate([before.astype(F32)[PACKED_ROWS - HALO:] * has_prev, main.astype(F32),
                                    after.astype(F32)[:HALO] * has_next], axis=0)

        dy_rows = jnp.concatenate([dp_ref[...], dm_ref[...], dn_ref[...]], axis=0)
        wrow = lax.broadcasted_iota(jnp.int32, (FFN_ROWS + 2 * HALO, 1), 0)
        edge = jnp.where(wrow < HALO, has_prev, jnp.where(wrow >= HALO + FFN_ROWS, has_next, 1.0))

        dh = jnp.zeros((FFN_ROWS, d), F32)
        sums = []
        for p in range(N_UP_PAIRS):
            pg = window(pp_ref[0, p], pm_ref[0, p], pn_ref[0, p])
            pv = window(pp_ref[1, p], pm_ref[1, p], pn_ref[1, p])
            dact = lax.dot_general(dy_rows, wd_ref[p], NT, preferred_element_type=F32)
            dact = dact[PACKED_ROWS - HALO:PACKED_ROWS + FFN_ROWS + HALO] * edge
            a = _conv(pg, w_ref[0, p])
            v = _conv(pv, w_ref[1, p])
            sg = _sigmoid(a)
            asg = a * sg
            dzg, dwg = _conv_bwd(dact * v * (sg + asg - asg * sg), pg, w_ref[0, p], cur)
            dzv, dwv = _conv_bwd(dact * asg, pv, w_ref[1, p], cur)
            dgate = dzg[cur].astype(BF16)
            dval = dzv[cur].astype(BF16)
            dpre_ref[0, p] = dgate
            dpre_ref[1, p] = dval
            dh = dh + jnp.dot(dgate, wu_ref[p], preferred_element_type=F32)
            dh = dh + jnp.dot(dval, wu_ref[N_UP_PAIRS + p], preferred_element_type=F32)
            sums.append(dwg + dwv)

        xv = x_ref[...]
        r = lax.rsqrt(jnp.mean(xv * xv, axis=-1, keepdims=True) + EPS)
        xh = xv * r
        gd = dh * g_ref[...]
        dx = r * (gd - xh * jnp.mean(gd * xh, axis=-1, keepdims=True)) + dres_ref[...]
        dx_ref[...] = dx
        dxb_ref[...] = dx.astype(BF16)
        part = jnp.sum(dh * xh, axis=0, keepdims=True)

        @pl.when(i == 0)
        def _():
            dg_ref[...] = part
            for p in range(N_UP_PAIRS):
                for k in range(6):
                    dw_ref[k // 3, p, pl.ds(k % 3, 1), :] = sums[p][k]

        @pl.when(i > 0)
        def _():
            dg_ref[...] += part
            for p in range(N_UP_PAIRS):
                for k in range(6):
                    dw_ref[k // 3, p, pl.ds(k % 3, 1), :] += sums[p][k]

    def rows4(n):
        return lambda fn: pl.BlockSpec((2, N_UP_PAIRS, n, UP_CHUNK), lambda i: (0, 0, fn(i), 0))

    def rows2(n):
        return lambda fn: pl.BlockSpec((n, d), lambda i: (fn(i), 0))

    prev_tile = lambda i: jnp.maximum(i * tiles_per_halo - 1, 0)
    next_tile = lambda i: jnp.minimum((i + 1) * tiles_per_halo, last_halo)
    row = pl.BlockSpec((FFN_ROWS, d), lambda i: (i, 0))
    vec = pl.BlockSpec((1, d), lambda i: (0, 0))
    wspec = pl.BlockSpec((2, N_UP_PAIRS, 3, UP_CHUNK), lambda i: (0, 0, 0, 0))
    return pl.pallas_call(
        body, name=name, grid=(t // FFN_ROWS,),
        in_specs=[rows4(FFN_ROWS)(lambda i: i), rows4(PACKED_ROWS)(prev_tile), rows4(PACKED_ROWS)(next_tile),
                  rows2(FFN_ROWS)(lambda i: i), rows2(PACKED_ROWS)(prev_tile), rows2(PACKED_ROWS)(next_tile),
                  wspec, pl.BlockSpec((N_UP_PAIRS, UP_CHUNK, d), lambda i: (0, 0, 0)),
                  pl.BlockSpec((N_DEV, UP_CHUNK, d), lambda i: (0, 0, 0)), row, vec, row],
        out_specs=[rows4(FFN_ROWS)(lambda i: i), wspec, row, row, vec],
        out_shape=[jax.ShapeDtypeStruct(pre.shape, BF16), jax.ShapeDtypeStruct(fcw.shape, F32),
                   jax.ShapeDtypeStruct((t, d), F32), jax.ShapeDtypeStruct((t, d), BF16),
                   jax.ShapeDtypeStruct((1, d), F32)],
        compiler_params=_params(("arbitrary",)),
    )(pre, pre, pre, dy, dy, dy, fcw, wdown, wup, x, g, dres)


def _adamw(lands, w, m, v, row_tile, name, after=()):
    nl = len(lands)
    _, nr, ncol = lands[0].shape
    c1 = 1.0 - ADAM_B1 ** ADAM_STEP
    c2 = 1.0 - ADAM_B2 ** ADAM_STEP

    def body(*refs):
        land_refs = refs[:nl]
        w_ref, m_ref, v_ref = refs[nl:nl + 3]
        g_ref, d_ref, mo_ref, vo_ref = refs[nl + 3 + len(after):]
        for l in range(nl):
            @pl.when(pl.program_id(0) == l)
            def _(l=l):
                g = land_refs[l][0].astype(F32)
                for j in range(1, N_DEV):
                    g = g + land_refs[l][j].astype(F32)
                g_ref[...] = g

        g = g_ref[...]
        m2 = ADAM_B1 * m_ref[...] + (1.0 - ADAM_B1) * g
        v2 = ADAM_B2 * v_ref[...] + (1.0 - ADAM_B2) * (g * g)
        mo_ref[...] = m2
        vo_ref[...] = v2
        d_ref[...] = -ADAM_LR * ((m2 / c1) / (jnp.sqrt(v2 / c2) + ADAM_EPS) + ADAM_WD * w_ref[...])

    def land_spec(l):
        return pl.BlockSpec((N_DEV, row_tile, ncol), lambda k, i: (0, jnp.where(k == l, i, 0), 0))

    tile = pl.BlockSpec((None, row_tile, ncol), lambda k, i: (k, i, 0))
    return pl.pallas_call(
        body, name=name, grid=(nl, nr // row_tile),
        in_specs=[land_spec(l) for l in range(nl)] + [tile, tile, tile] + [pl.BlockSpec(memory_space=pl.ANY)] * len(after),
        out_specs=[tile] * 4,
        out_shape=[jax.ShapeDtypeStruct(w.shape, F32)] * 4,
        compiler_params=_params(("arbitrary", "arbitrary")),
    )(*lands, w, m, v, *after)


class _Item:
    def __init__(self, src, chunked, land_cols=False):
        self.src, self.chunked, self.land_cols = src, chunked, land_cols
        if chunked == "cols":
            block = (src.shape[0], src.shape[1] // N_DEV)
        else:
            block = src.shape[1:] if chunked else src.shape
        self.width = block[-1]
        self.land_shape = (block[0], N_DEV * block[1]) if land_cols else (N_DEV,) + block

    def _cols(self, first, count=1):
        return pl.ds(pl.multiple_of(first * self.width, LANES), count * self.width)

    def part(self, src_ref, j):
        if self.chunked == "cols":
            return src_ref.at[:, self._cols(j)]
        return src_ref.at[j] if self.chunked else src_ref

    def slot(self, land_ref, s):
        return land_ref.at[:, self._cols(s)] if self.land_cols else land_ref.at[s]

    def seven(self, land_ref):
        return land_ref.at[:, self._cols(0, N_DEV - 1)] if self.land_cols else land_ref.at[pl.ds(0, N_DEV - 1)]


def _mesh_place():
    x, y, c = lax.axis_index("x"), lax.axis_index("y"), lax.axis_index("c")
    return x, y, c, 4 * x + 2 * y + c


def _flipped(x, y, c, k):
    px = 1 - x if k & 4 else x
    py = 1 - y if k & 2 else y
    pc = 1 - c if k & 1 else c
    return (px, py, pc), 4 * px + 2 * py + pc


PEER_ORDER = (2, 4, 6, 3, 5, 7, 1)


def _exchange(items, name):
    n = len(items)

    def body(*refs):
        srcs, lands = refs[:n], refs[n:2 * n]
        send, recv, local = refs[2 * n:]
        x, y, c, me = _mesh_place()

        def copy(i, k, chunk, slot, dev):
            return pltpu.make_async_remote_copy(
                src_ref=items[i].part(srcs[i], chunk), dst_ref=items[i].slot(lands[i], slot),
                send_sem=send.at[i, k - 1], recv_sem=recv.at[i, k - 1], device_id=dev, device_id_type=MESH)

        own = [pltpu.make_async_copy(items[i].part(srcs[i], me), items[i].slot(lands[i], me), local.at[i])
               for i in range(n)]
        for k in PEER_ORDER:
            dev, idx = _flipped(x, y, c, k)
            for i in range(n):
                copy(i, k, idx, me, dev).start()
        for cp in own:
            cp.start()
        for k in PEER_ORDER:
            dev, idx = _flipped(x, y, c, k)
            for i in range(n):
                copy(i, k, me, idx, dev).wait_recv()
        for k in PEER_ORDER:
            dev, idx = _flipped(x, y, c, k)
            for i in range(n):
                copy(i, k, idx, me, dev).wait_send()
        for cp in own:
            cp.wait()

    hbm = pl.BlockSpec(memory_space=pl.ANY)
    return pl.pallas_call(
        body, name=name,
        in_specs=[hbm] * n, out_specs=[hbm] * n,
        out_shape=[jax.ShapeDtypeStruct(it.land_shape, it.src.dtype) for it in items],
        scratch_shapes=[pltpu.SemaphoreType.DMA((n, N_DEV - 1)), pltpu.SemaphoreType.DMA((n, N_DEV - 1)),
                        pltpu.SemaphoreType.DMA((n,))],
        compiler_params=pltpu.CompilerParams(has_side_effects=True),
    )(*[it.src for it in items])


def _sequencer_exchange(items, name, collective_id):
    n = len(items)

    def body(*refs):
        srcs, lands = refs[:n], refs[n:2 * n]
        send, recv, local = refs[2 * n:]
        x, y, c, me = _mesh_place()
        barrier = pltpu.get_barrier_semaphore()
        for k in PEER_ORDER:
            pl.semaphore_signal(barrier, inc=1, device_id=_flipped(x, y, c, k)[0], device_id_type=MESH)
        pl.semaphore_wait(barrier, N_DEV - 1)

        def copy(i, k, chunk, slot, dev):
            return pltpu.make_async_remote_copy(
                src_ref=items[i].part(srcs[i], chunk), dst_ref=items[i].slot(lands[i], slot),
                send_sem=send.at[i, k - 1], recv_sem=recv.at[i, k - 1], device_id=dev, device_id_type=MESH)

        own = [pltpu.make_async_copy(items[i].part(srcs[i], me), items[i].slot(lands[i], me), local.at[i])
               for i in range(n)]
        for cp in own:
            cp.start()
        for k in PEER_ORDER:
            dev, idx = _flipped(x, y, c, k)
            for i in range(n):
                copy(i, k, idx, me, dev).start()
        for k in PEER_ORDER:
            dev, idx = _flipped(x, y, c, k)
            for i in range(n):
                copy(i, k, me, idx, dev).wait_recv()
        for k in PEER_ORDER:
            dev, idx = _flipped(x, y, c, k)
            for i in range(n):
                copy(i, k, idx, me, dev).wait_send()
        for cp in own:
            cp.wait()

    return pl.kernel(
        body, name=name,
        out_type=[jax.ShapeDtypeStruct(it.land_shape, it.src.dtype) for it in items],
        mesh=plsc.ScalarSubcoreMesh(axis_name="sequencer", num_cores=1),
        scratch_types=[pltpu.SemaphoreType.DMA((n, N_DEV - 1)), pltpu.SemaphoreType.DMA((n, N_DEV - 1)),
                       pltpu.SemaphoreType.DMA((n,))],
        compiler_params=pltpu.CompilerParams(collective_id=collective_id),
    )(*[it.src for it in items])


HBM_SPEC = pl.BlockSpec(memory_space=pltpu.HBM)
SEM_SPEC = pl.BlockSpec(memory_space=pltpu.SEMAPHORE)
DATAFLOW = pltpu.SideEffectType.DATAFLOW_SIDE_EFFECTING


def _exchange_start(items, name, after=()):
    n = len(items)
    na = len(after)

    def body(*refs):
        srcs, land_ins = refs[:n], refs[n:2 * n]
        outs = refs[2 * n + na:6 * n + na]
        (local,) = refs[6 * n + na:]
        del land_ins
        x, y, c, me = _mesh_place()
        own = [pltpu.make_async_copy(items[i].part(srcs[i], me), items[i].slot(outs[4 * i + 3], me), local.at[i])
               for i in range(n)]
        for cp in own:
            cp.start()
        for cp in own:
            cp.wait()
        for k in PEER_ORDER:
            dev, idx = _flipped(x, y, c, k)
            for i in range(n):
                send, recv, _, land = outs[4 * i:4 * i + 4]
                pltpu.make_async_remote_copy(
                    src_ref=items[i].part(srcs[i], idx), dst_ref=items[i].slot(land, me), send_sem=send, recv_sem=recv,
                    device_id=dev, device_id_type=MESH).start()

    out_shape, out_specs, args, lands = [], [], [], []
    for it in items:
        out_shape += [pltpu.SemaphoreType.DMA(()), pltpu.SemaphoreType.DMA(()),
                      pltpu.HBM(it.src.shape, it.src.dtype), pltpu.HBM(it.land_shape, it.src.dtype)]
        out_specs += [SEM_SPEC, SEM_SPEC, HBM_SPEC, HBM_SPEC]
        args.append(pltpu.with_memory_space_constraint(it.src, pltpu.HBM))
        lands.append(pltpu.with_memory_space_constraint(lax.empty(it.land_shape, it.src.dtype), pltpu.HBM))
    outs = pl.pallas_call(
        body, name=name,
        in_specs=[HBM_SPEC] * (2 * n) + [pl.BlockSpec(memory_space=pl.ANY)] * na,
        out_specs=out_specs, out_shape=out_shape,
        scratch_shapes=[pltpu.SemaphoreType.DMA((n,))],
        input_output_aliases={**{i: 4 * i + 2 for i in range(n)}, **{n + i: 4 * i + 3 for i in range(n)}},
        compiler_params=pltpu.CompilerParams(has_side_effects=DATAFLOW),
    )(*args, *lands, *after)
    return [tuple(outs[4 * i:4 * i + 4]) + (items[i],) for i in range(n)]


def _started(handles):
    return handles[0][2]


def _exchange_wait(handles, after, name):
    n = len(handles)

    def body(*refs):
        x, y, c, _ = _mesh_place()
        for i in range(n):
            src, land, send, recv = refs[4 * i:4 * i + 4]
            del src
            seven = handles[i][4].seven(land)
            cp = pltpu.make_async_remote_copy(src_ref=seven, dst_ref=seven, send_sem=send, recv_sem=recv,
                                              device_id=(x, y, 1 - c), device_id_type=MESH)
            cp.wait_send()
            cp.wait_recv()

    args, in_specs, out_shape = [], [], []
    for send, recv, src, land, _ in handles:
        args += [src, land, send, recv]
        in_specs += [HBM_SPEC, HBM_SPEC, SEM_SPEC, SEM_SPEC]
        out_shape += [pltpu.HBM(src.shape, src.dtype), pltpu.HBM(land.shape, land.dtype)]
    outs = pl.pallas_call(
        body, name=name,
        in_specs=in_specs + [pl.BlockSpec(memory_space=pl.ANY)] * len(after), out_specs=[HBM_SPEC] * (2 * n),
        out_shape=out_shape,
        input_output_aliases={**{4 * i: 2 * i for i in range(n)}, **{4 * i + 1: 2 * i + 1 for i in range(n)}},
        compiler_params=pltpu.CompilerParams(has_side_effects=DATAFLOW),
    )(*args, *after)
    return [outs[2 * i + 1] for i in range(n)]


TM = 1024
TM_ACC = 512
TN_IN = 768


def kernel(x, norm1_g, w_in, mix_conv_w, attn_out_g, conv_out_g, w_out, norm2_g, ffn_up, ffn_conv_w, ffn_down, final_norm_g, loss_target, m_norm1_g, m_w_in, m_mix_conv_w, m_attn_out_g, m_conv_out_g, m_w_out, m_norm2_g, m_ffn_up, m_ffn_conv_w, m_ffn_down, m_final_norm_g, v_norm1_g, v_w_in, v_mix_conv_w, v_attn_out_g, v_conv_out_g, v_w_out, v_norm2_g, v_ffn_up, v_ffn_conv_w, v_ffn_down, v_final_norm_g):
    nbatch, seq, d = x.shape
    t = nbatch * seq
    nt, nta = t // TM, t // TM_ACC
    out_rows = D_MODEL // N_DEV
    down_rows = D_FF // N_DEV
    xf = x.reshape(t, d)
    target = loss_target.reshape(t, d)

    cw_local = jnp.concatenate([ffn_conv_w, mix_conv_w], axis=-1)
    cast = lambda w: _Item(w.astype(BF16), False)
    cast_in = lambda w: _Item(w.astype(BF16), False, land_cols=True)
    cw_all, win0 = _sequencer_exchange([_Item(cw_local, False), cast_in(w_in[0])], "gather_a", 0)
    up_t, m_up_t, v_up_t = (jnp.swapaxes(a, 1, 2) for a in (ffn_up, m_ffn_up, v_ffn_up))
    wout0, wup0 = _sequencer_exchange([cast(w_out[0]), cast(up_t[0])], "gather_b", 1)
    (wdown0,) = _sequencer_exchange([cast(ffn_down[0])], "gather_c", 2)
    win1, wout1 = _sequencer_exchange([cast_in(w_in[1]), cast(w_out[1])], "gather_d", 3)
    wup1, wdown1 = _sequencer_exchange([cast(up_t[1]), cast(ffn_down[1])], "gather_e", 7)
    win, wup = [win0, win1], [wup0, wup1]
    wout = [w.reshape(D_MODEL, D_MODEL) for w in (wout0, wout1)]
    wdown = [w.reshape(N_UP_PAIRS, UP_CHUNK, D_MODEL) for w in (wdown0, wdown1)]
    fcw = [cw_all[:, k, :, :UP_CHUNK].reshape(2, N_UP_PAIRS, 3, UP_CHUNK) for k in range(DEPTH)]
    mcw = [cw_all[:, k, :, UP_CHUNK:].transpose(1, 0, 2).reshape(3, D_CONV) for k in range(DEPTH)]

    full = lambda i, j, k: (0, 0)

    saved = []
    xin = xf
    h1 = _rms_fwd(xin, norm1_g[0][None], "rms1_fwd_0")
    rows_of = lambda width: pl.BlockSpec((TM_ACC, width), lambda i: (i, 0))
    whole = lambda *shape: pl.BlockSpec(shape, lambda i: (0,) * len(shape))
    chunks_of = lambda n: pl.BlockSpec((n, TM_ACC, UP_CHUNK), lambda i: (0, i, 0))
    for l in range(DEPTH):
        proj = _matmul(
            h1, win[l], grid=(nt, D_IN // TN_IN, 1), dims=NN, name=f"proj_{l}",
            a_spec=pl.BlockSpec((TM, D_MODEL), lambda i, j, k: (i, 0)),
            b_spec=pl.BlockSpec((D_MODEL, TN_IN), lambda i, j, k: (0, j)),
            o_spec=pl.BlockSpec((TM, TN_IN), lambda i, j, k: (i, j)), o_shape=(t, D_IN), o_dtype=F32)
        o, lse, cat = _attn_fwd(proj, attn_out_g[l][None], nbatch, seq)
        cat = _convmix_fwd(proj, cat, mcw[l], conv_out_g[l][None], nbatch, seq)
        xmid, h2 = _matmul_norm(cat, wout[l], xin, norm2_g[l][None], dims=NN, name=f"mix_out_{l}",
                                a_spec=rows_of(D_MODEL), b_spec=whole(D_MODEL, D_MODEL))
        if l + 1 < DEPTH:
            xout, h_next, act, pre = _ffn_fwd(
                h2, wup[l], fcw[l], wdown[l], xmid, norm1_g[l + 1][None], seq, f"ffn_fwd_{l}")
        else:
            h_next = None
            xout, act, pre = _ffn_fwd(h2, wup[l], fcw[l], wdown[l], xmid, None, seq, f"ffn_fwd_{l}")
        saved.append((xin, h1, proj, o, lse, cat, xmid, h2, pre, act))
        xin, h1 = xout, h_next

    loss_part, dx, dxb, dgf = _loss_head(xin, final_norm_g[None], target, "loss_head")

    dg1, dg2, dga, dgc = [None] * DEPTH, [None] * DEPTH, [None] * DEPTH, [None] * DEPTH
    for l in reversed(range(DEPTH)):
        xin, h1, proj, o, lse, cat, xmid, h2, pre, act = saved[l]
        g_down = _matmul(
            act, dxb, grid=(N_UP_PAIRS, 1, 1), dims=TN, name=f"g_down_{l}",
            a_spec=pl.BlockSpec((None, t, UP_CHUNK), lambda i, j, k: (i, 0, 0)),
            b_spec=pl.BlockSpec((t, D_MODEL), full),
            o_spec=pl.BlockSpec((None, UP_CHUNK, D_MODEL), lambda i, j, k: (i, 0, 0)),
            o_shape=(N_UP_PAIRS, UP_CHUNK, D_MODEL), o_dtype=BF16).reshape(N_DEV, down_rows, D_MODEL)
        d_pre, d_fcw, dxm, dxmb, dg2[l] = _ffn_up_bwd(
            pre, dxb, fcw[l], wdown[l], wup[l], xmid, norm2_g[l][None], dx, seq, f"ffn_bwd_{l}")
        d_pre = d_pre.reshape(N_DEV, t, UP_CHUNK)
        g_up = _matmul(
            d_pre, h2, grid=(N_DEV, 1, 1), dims=TN, name=f"g_up_{l}",
            a_spec=pl.BlockSpec((None, t, UP_CHUNK), lambda i, j, k: (i, 0, 0)),
            b_spec=pl.BlockSpec((t, D_MODEL), full),
            o_spec=pl.BlockSpec((None, UP_CHUNK, D_MODEL), lambda i, j, k: (i, 0, 0)),
            o_shape=(N_DEV, UP_CHUNK, D_MODEL), o_dtype=BF16)
        g_out = _matmul(
            cat, dxmb, grid=(1, 1, nt), dims=TN, name=f"g_out_{l}",
            a_spec=pl.BlockSpec((TM, D_MODEL), lambda i, j, k: (k, 0)),
            b_spec=pl.BlockSpec((TM, D_MODEL), lambda i, j, k: (k, 0)),
            o_spec=pl.BlockSpec((D_MODEL, D_MODEL), full),
            o_shape=(D_MODEL, D_MODEL), o_dtype=BF16).reshape(N_DEV, out_rows, D_MODEL)
        if l == 0:
            land_out0, land_up0, land_down0 = _sequencer_exchange(
                [_Item(g_out, True), _Item(g_up, True), _Item(g_down, True)], "scatter_0a", 5)
        d_cat = _matmul(
            dxmb, wout[l], grid=(nta, 1, 1), dims=NT, name=f"d_cat_{l}",
            a_spec=pl.BlockSpec((TM_ACC, D_MODEL), lambda i, j, k: (i, 0)),
            b_spec=pl.BlockSpec((D_MODEL, D_MODEL), full),
            o_spec=pl.BlockSpec((TM_ACC, D_MODEL), lambda i, j, k: (i, 0)), o_shape=(t, D_MODEL), o_dtype=BF16)
        d_proj, dga[l] = _attn_bwd(proj, o, lse, d_cat, attn_out_g[l][None], nbatch, seq)
        d_proj, d_mcw, dgc[l] = _convmix_bwd(proj, d_cat, d_proj, mcw[l], conv_out_g[l][None], nbatch, seq)
        g_in = _matmul(
            h1, d_proj, grid=(1, D_IN // TN_IN, 1), dims=TN, name=f"g_in_{l}",
            a_spec=pl.BlockSpec((t, D_MODEL), full),
            b_spec=pl.BlockSpec((t, TN_IN), lambda i, j, k: (0, j)),
            o_spec=pl.BlockSpec((D_MODEL, TN_IN), lambda i, j, k: (0, j)),
            o_shape=(D_MODEL, D_IN), o_dtype=BF16)
        g_cw = jnp.concatenate(
            [d_fcw.reshape(N_DEV, 3, UP_CHUNK), d_mcw.reshape(3, N_DEV, D_CONV // N_DEV).transpose(1, 0, 2)], axis=-1)
        if l == 0:
            land_in0, land_cw0 = _sequencer_exchange([_Item(g_in, "cols"), _Item(g_cw, True)], "scatter_0b", 6)
        else:
            land_in1, land_out1, land_up1, land_down1, land_cw1 = _sequencer_exchange(
                [_Item(g_in, "cols"), _Item(g_out, True), _Item(g_up, True), _Item(g_down, True), _Item(g_cw, True)],
                "scatter_1", 4)
        dx, dxb, dg1[l] = _matmul_norm_bwd(
            d_proj, win[l], xin, norm1_g[l][None], dxm, dims=NT, name=f"d_h1_{l}",
            a_spec=rows_of(D_IN), b_spec=whole(D_MODEL, D_IN))

    def pack_small(n1, a, c, n2, f):
        return jnp.concatenate(
            [n1, n2, f[None], jnp.concatenate([a, c], axis=-1), jnp.zeros((1, D_MODEL), F32)], axis=0)[None]

    small = jnp.concatenate(
        [dg1[0], dg1[1], dg2[0], dg2[1], dgf,
         jnp.concatenate([dga[0], dgc[0]], axis=-1), jnp.concatenate([dga[1], dgc[1]], axis=-1),
         jnp.pad(loss_part, ((0, 0), (0, D_MODEL - LANES)))], axis=0)
    (land_small,) = _exchange([_Item(small, False)], "gather_gain_grads")
    res_small = _adamw(
        [land_small], pack_small(norm1_g, attn_out_g, conv_out_g, norm2_g, final_norm_g),
        pack_small(m_norm1_g, m_attn_out_g, m_conv_out_g, m_norm2_g, m_final_norm_g),
        pack_small(v_norm1_g, v_attn_out_g, v_conv_out_g, v_norm2_g, v_final_norm_g), SUBLANES, "adamw_gains")
    res_out = _adamw([land_out0, land_out1], w_out, m_w_out, v_w_out, out_rows, "adamw_w_out", after=[res_small[0]])
    res_up_t = _adamw([land_up0, land_up1], up_t, m_up_t, v_up_t, UP_CHUNK // 4, "adamw_ffn_up", after=[res_out[0]])
    res_up = [jnp.swapaxes(r, 1, 2) for r in res_up_t]
    res_down = _adamw([land_down0, land_down1], ffn_down, m_ffn_down, v_ffn_down, down_rows, "adamw_ffn_down",
                      after=[res_up_t[0]])
    res_in = _adamw([land_in0, land_in1], w_in, m_w_in, v_w_in, 256, "adamw_w_in", after=[res_down[0]])
    res_cw = _adamw(
        [land_cw0, land_cw1], cw_local, jnp.concatenate([m_ffn_conv_w, m_mix_conv_w], axis=-1),
        jnp.concatenate([v_ffn_conv_w, v_mix_conv_w], axis=-1), 3, "adamw_conv_w", after=[res_in[0]])

    loss = res_small[0][0, SUBLANES - 1, 0]

    def unpack(kind):
        s = res_small[kind][0]
        cwr = res_cw[kind]
        return (s[0:2], res_in[kind], cwr[..., UP_CHUNK:], s[5:7, :D_ATTN], s[5:7, D_ATTN:], res_out[kind],
                s[2:4], res_up[kind], cwr[..., :UP_CHUNK], res_down[kind], s[4])

    return (loss, dx.reshape(nbatch, seq, d), *unpack(0), *unpack(1), *unpack(2), *unpack(3))
```

```python
import math

import jax
import jax.numpy as jnp
from jax import lax
from jax.experimental import pallas as pl
from jax.experimental.pallas import tpu as pltpu
from jax.experimental.pallas import tpu_sc as plsc

F32 = jnp.float32
BF16 = jnp.bfloat16

D_MODEL = 1024
D_ATTN = 512
D_CONV = 512
HEAD_DIM = 64
N_HEADS = 8
D_FF = 2816
DEPTH = 2
D_IN = 3 * D_ATTN + 3 * D_CONV
EPS = 1e-6
DILATIONS = (1, 4, 16)
BAND = 128
N_DEV = 8
IN_CHUNK = D_IN // N_DEV
UP_CHUNK = 2 * D_FF // N_DEV
N_UP_PAIRS = N_DEV // 2
CW_PACK = UP_CHUNK + D_CONV // N_DEV
ADAM_LR = 0.001
ADAM_B1 = 0.9
ADAM_B2 = 0.999
ADAM_EPS = 1e-08
ADAM_WD = 0.01
ADAM_STEP = 10
LANES = 128
SUBLANES = 8
VMEM_LIMIT = 56 * 1024 * 1024

NEG = -1e30
MESH = pl.DeviceIdType.MESH


def _params(sem=None, vmem=VMEM_LIMIT):
    return pltpu.CompilerParams(dimension_semantics=sem, vmem_limit_bytes=vmem)


NN = (((1,), (0,)), ((), ()))
NT = (((1,), (1,)), ((), ()))
TN = (((0,), (0,)), ((), ()))


def _contract(a_ref, b_ref, dims):
    def dot(av, bv):
        return lax.dot_general(av.astype(BF16), bv.astype(BF16), dims, preferred_element_type=F32)

    if len(a_ref.shape) == 2:
        return dot(a_ref[...], b_ref[...])
    part = dot(a_ref[0], b_ref[0])
    for c in range(1, a_ref.shape[0]):
        part = part + dot(a_ref[c], b_ref[c])
    return part


def _matmul(a, b, *, grid, a_spec, b_spec, o_spec, o_shape, o_dtype, dims, name, res=None, res_spec=None, after=()):
    nk = grid[2]
    o_block = tuple(s for s in o_spec.block_shape if s is not None)
    na = len(after)

    def body(*refs):
        refs = refs[:2 + (res is not None)] + refs[2 + (res is not None) + na:]
        if res is None:
            a_ref, b_ref, o_ref, *scr = refs
            r_ref = None
        else:
            a_ref, b_ref, r_ref, o_ref, *scr = refs
        part = _contract(a_ref, b_ref, dims)

        def finish(total):
            if r_ref is not None:
                total = total + r_ref[...]
            o_ref[...] = total.astype(o_dtype)

        if nk == 1:
            finish(part)
        else:
            acc = scr[0]
            k = pl.program_id(2)

            @pl.when(k == 0)
            def _():
                acc[...] = part

            @pl.when(k > 0)
            def _():
                acc[...] += part

            @pl.when(k == nk - 1)
            def _():
                finish(acc[...])

    in_specs = [a_spec, b_spec] + ([res_spec] if res is not None else []) + [pl.BlockSpec(memory_space=pl.ANY)] * na
    args = (a, b) + ((res,) if res is not None else ()) + tuple(after)
    return pl.pallas_call(
        body, name=name, grid=grid, in_specs=in_specs, out_specs=o_spec,
        out_shape=jax.ShapeDtypeStruct(o_shape, o_dtype),
        scratch_shapes=[pltpu.VMEM(o_block, F32)] if nk > 1 else [],
        compiler_params=_params(("parallel", "parallel", "arbitrary")),
    )(*args)


ROW_TILE = 512


def _rms_fwd(x, g, name):
    t, d = x.shape

    def body(x_ref, g_ref, h_ref):
        xv = x_ref[...]
        r = lax.rsqrt(jnp.mean(xv * xv, axis=-1, keepdims=True) + EPS)
        h_ref[...] = (xv * r * g_ref[...]).astype(BF16)

    return pl.pallas_call(
        body, name=name, grid=(t // ROW_TILE,),
        in_specs=[pl.BlockSpec((ROW_TILE, d), lambda i: (i, 0)), pl.BlockSpec((1, d), lambda i: (0, 0))],
        out_specs=pl.BlockSpec((ROW_TILE, d), lambda i: (i, 0)),
        out_shape=jax.ShapeDtypeStruct((t, d), BF16),
        compiler_params=_params(("parallel",)),
    )(x, g)


def _rms_bwd(x, g, dh, dres, name):
    t, d = x.shape

    def body(x_ref, g_ref, dh_ref, dres_ref, dx_ref, dxb_ref, dg_ref):
        xv = x_ref[...]
        r = lax.rsqrt(jnp.mean(xv * xv, axis=-1, keepdims=True) + EPS)
        xh = xv * r
        dhv = dh_ref[...]
        gd = dhv * g_ref[...]
        dx = r * (gd - xh * jnp.mean(gd * xh, axis=-1, keepdims=True)) + dres_ref[...]
        dx_ref[...] = dx
        dxb_ref[...] = dx.astype(BF16)
        part = jnp.sum(dhv * xh, axis=0, keepdims=True)

        @pl.when(pl.program_id(0) == 0)
        def _():
            dg_ref[...] = part

        @pl.when(pl.program_id(0) > 0)
        def _():
            dg_ref[...] += part

    row = pl.BlockSpec((ROW_TILE, d), lambda i: (i, 0))
    vec = pl.BlockSpec((1, d), lambda i: (0, 0))
    return pl.pallas_call(
        body, name=name, grid=(t // ROW_TILE,),
        in_specs=[row, vec, row, row], out_specs=[row, row, vec],
        out_shape=[jax.ShapeDtypeStruct((t, d), F32), jax.ShapeDtypeStruct((t, d), BF16),
                   jax.ShapeDtypeStruct((1, d), F32)],
        compiler_params=_params(("arbitrary",)),
    )(x, g, dh, dres)


def _matmul_norm(a, b, res, g, *, a_spec, b_spec, dims, name):
    t, d = res.shape

    def body(a_ref, b_ref, r_ref, g_ref, x_ref, h_ref):
        xv = _contract(a_ref, b_ref, dims) + r_ref[...]
        x_ref[...] = xv
        h_ref[...] = (xv * lax.rsqrt(jnp.mean(xv * xv, axis=-1, keepdims=True) + EPS) * g_ref[...]).astype(BF16)

    row = pl.BlockSpec((TM_ACC, d), lambda i: (i, 0))
    return pl.pallas_call(
        body, name=name, grid=(t // TM_ACC,),
        in_specs=[a_spec, b_spec, row, pl.BlockSpec((1, d), lambda i: (0, 0))], out_specs=[row, row],
        out_shape=[jax.ShapeDtypeStruct((t, d), F32), jax.ShapeDtypeStruct((t, d), BF16)],
        compiler_params=_params(("parallel",)),
    )(a, b, res, g)


def _matmul_norm_bwd(a, b, x, g, dres, *, a_spec, b_spec, dims, name):
    t, d = x.shape

    def body(a_ref, b_ref, x_ref, g_ref, dres_ref, dx_ref, dxb_ref, dg_ref):
        dhv = _contract(a_ref, b_ref, dims)
        xv = x_ref[...]
        r = lax.rsqrt(jnp.mean(xv * xv, axis=-1, keepdims=True) + EPS)
        xh = xv * r
        gd = dhv * g_ref[...]
        dx = r * (gd - xh * jnp.mean(gd * xh, axis=-1, keepdims=True)) + dres_ref[...]
        dx_ref[...] = dx
        dxb_ref[...] = dx.astype(BF16)
        part = jnp.sum(dhv * xh, axis=0, keepdims=True)

        @pl.when(pl.program_id(0) == 0)
        def _():
            dg_ref[...] = part

        @pl.when(pl.program_id(0) > 0)
        def _():
            dg_ref[...] += part

    row = pl.BlockSpec((TM_ACC, d), lambda i: (i, 0))
    vec = pl.BlockSpec((1, d), lambda i: (0, 0))
    return pl.pallas_call(
        body, name=name, grid=(t // TM_ACC,),
        in_specs=[a_spec, b_spec, row, vec, row], out_specs=[row, row, vec],
        out_shape=[jax.ShapeDtypeStruct((t, d), F32), jax.ShapeDtypeStruct((t, d), BF16),
                   jax.ShapeDtypeStruct((1, d), F32)],
        compiler_params=_params(("arbitrary",)),
    )(a, b, x, g, dres)


def _loss_head(x, g, target, name):
    t, d = x.shape

    def body(x_ref, g_ref, t_ref, loss_ref, dx_ref, dxb_ref, dg_ref):
        xv = x_ref[...]
        r = lax.rsqrt(jnp.mean(xv * xv, axis=-1, keepdims=True) + EPS)
        xh = xv * r
        gv = g_ref[...]
        err = xh * gv - t_ref[...]
        loss = jnp.full((1, LANES), 0.5 / d, F32) * jnp.sum(err * err)
        dy = err * (1.0 / d)
        gd = dy * gv
        dx = r * (gd - xh * jnp.mean(gd * xh, axis=-1, keepdims=True))
        dx_ref[...] = dx
        dxb_ref[...] = dx.astype(BF16)
        part = jnp.sum(dy * xh, axis=0, keepdims=True)

        @pl.when(pl.program_id(0) == 0)
        def _():
            dg_ref[...] = part
            loss_ref[...] = loss

        @pl.when(pl.program_id(0) > 0)
        def _():
            dg_ref[...] += part
            loss_ref[...] += loss

    row = pl.BlockSpec((ROW_TILE, d), lambda i: (i, 0))
    vec = pl.BlockSpec((1, d), lambda i: (0, 0))
    return pl.pallas_call(
        body, name=name, grid=(t // ROW_TILE,),
        in_specs=[row, vec, row],
        out_specs=[pl.BlockSpec((1, LANES), lambda i: (0, 0)), row, row, vec],
        out_shape=[jax.ShapeDtypeStruct((1, LANES), F32), jax.ShapeDtypeStruct((t, d), F32),
                   jax.ShapeDtypeStruct((t, d), BF16), jax.ShapeDtypeStruct((1, d), F32)],
        compiler_params=_params(("arbitrary",)),
    )(x, g, target)


def _group_matrix(n):
    shift = int(math.log2(HEAD_DIM))
    r = lax.broadcasted_iota(jnp.int32, (n, n), 0) >> shift
    c = lax.broadcasted_iota(jnp.int32, (n, n), 1) >> shift
    return (r == c).astype(BF16)


def _group_sum(v, gmat):
    hi = v.astype(BF16)
    lo = (v - hi.astype(F32)).astype(BF16)

    def dot(p):
        return jnp.dot(p, gmat, preferred_element_type=F32)

    return dot(hi) + dot(lo)


def _shift_rows(ext, k):
    return pltpu.roll(ext, k % ext.shape[0], 0)


def _store_columns(stage, out_hbm, sems, row0, nrows, col_blocks):
    rows = pl.ds(pl.multiple_of(row0, SUBLANES * 2), nrows)
    copies = [
        pltpu.make_async_copy(stage.at[i], out_hbm.at[rows, pl.ds(pl.multiple_of(cb * LANES, LANES), LANES)], sems.at[i])
        for i, cb in enumerate(col_blocks)
    ]
    for cp in copies:
        cp.start()
    for cp in copies:
        cp.wait()


def _attn_consts(width):
    i = lax.broadcasted_iota(jnp.int32, (BAND, width), 0)
    j = lax.broadcasted_iota(jnp.int32, (BAND, width), 1)
    dist = (width - BAND) + i - j
    inwin = (dist >= 0) & (dist <= BAND)
    return dist.astype(F32), inwin, j


def _head_masks():
    lane = lax.broadcasted_iota(jnp.int32, (1, LANES), 1)
    return [(lane < HEAD_DIM).astype(F32), (lane >= HEAD_DIM).astype(F32)]


def _pair_bias(slope, dil):
    distf, inwin, _ = _attn_consts(2 * BAND)
    return jnp.concatenate([jnp.where(inwin, distf * (slope[hh] * (-float(dil))), NEG) for hh in range(2)], axis=0)


def _stack_heads(xv, hmask):
    return jnp.concatenate([xv * hmask[0], xv * hmask[1]], axis=0).astype(BF16)


FWD_UNROLL = 8
BWD_UNROLL = 8


def _unroll(trips, most):
    return max(u for u in range(1, most + 1) if trips % u == 0)


def _for_blocks(seq, dil, block, most):
    nb = seq // dil // BAND

    def residue(r, carry):
        base = r * nb
        block(pl.multiple_of(base * BAND, BAND), None)
        if nb > 1:
            def rest(n, c):
                block(pl.multiple_of((base + n) * BAND, BAND), pl.multiple_of((base + n - 1) * BAND, BAND))
                return c

            lax.fori_loop(1, nb, rest, 0, unroll=_unroll(nb - 1, most))
        return carry

    if dil == 1:
        residue(0, 0)
    else:
        lax.fori_loop(0, dil, residue, 0, unroll=_unroll(dil, max(1, most // nb)))


def _permute_in(src_ref, dst_ref, dil, seq):
    length = seq // dil
    for r in range(dil):
        dst_ref[pl.ds(r * length, length), :] = src_ref[pl.ds(r, length, stride=dil), :].astype(dst_ref.dtype)


def _slopes_table():
    slopes = 2.0 ** (-8.0 * jnp.arange(1, N_HEADS + 1, dtype=F32) / N_HEADS)
    return jnp.broadcast_to(slopes[:, None], (N_HEADS, 2 * BAND))


def _attn_fwd(proj, attn_g, nbatch, seq):
    t = nbatch * seq
    scale = HEAD_DIM ** -0.5

    def body(q_ref, k_ref, v_ref, g_ref, sl_ref, o_ref, lse_ref, cat_ref, pq, pk, pv, po, pm, pll, ao, am, al):
        hp = pl.program_id(1)
        hmask = _head_masks()
        slope = [sl_ref[pl.ds(2 * hp + hh, 1), :] for hh in range(2)]

        def run_branch(dil, qs, ks, vs, osink, msink, lsink):
            bias = _pair_bias(slope, dil)

            def block(row0, prow):
                cur = pl.ds(row0, BAND)
                q2 = _stack_heads(qs[cur, :] * scale, hmask)
                if prow is None:
                    kk, vv, bias_b = ks[cur, :], vs[cur, :], bias[:, BAND:]
                else:
                    prev = pl.ds(prow, BAND)
                    kk = jnp.concatenate([ks[prev, :], ks[cur, :]], axis=0)
                    vv = jnp.concatenate([vs[prev, :], vs[cur, :]], axis=0)
                    bias_b = bias
                s = lax.dot_general(q2, kk.astype(BF16), NT, preferred_element_type=F32) + bias_b
                m = jnp.max(s, axis=1, keepdims=True)
                p = jnp.exp(s - m)
                l = jnp.sum(p, axis=1, keepdims=True)
                pb = p.astype(BF16)
                o = jnp.dot(jnp.concatenate([pb[:BAND], pb[BAND:]], axis=1), _stack_heads(vv, hmask),
                            preferred_element_type=F32)
                osink[cur, :] = o
                msink[cur, :] = m[:BAND] * hmask[0] + m[BAND:] * hmask[1]
                lsink[cur, :] = l[:BAND] * hmask[0] + l[BAND:] * hmask[1]

            _for_blocks(seq, dil, block, FWD_UNROLL)

        run_branch(1, q_ref, k_ref, v_ref, ao, am, al)
        for dil in DILATIONS[1:]:
            length = seq // dil
            _permute_in(q_ref, pq, dil, seq)
            _permute_in(k_ref, pk, dil, seq)
            _permute_in(v_ref, pv, dil, seq)
            run_branch(dil, pq, pk, pv, po, pm, pll)
            for r in range(dil):
                nat = pl.ds(r, length, stride=dil)
                per = pl.ds(r * length, length)
                m0 = am[nat, :]
                mb = pm[per, :]
                mn = jnp.maximum(m0, mb)
                e0 = jnp.exp(m0 - mn)
                eb = jnp.exp(mb - mn)
                ao[nat, :] = ao[nat, :] * e0 + po[per, :] * eb
                al[nat, :] = al[nat, :] * e0 + pll[per, :] * eb
                am[nat, :] = mn

        gmat = _group_matrix(LANES)
        gv = g_ref[...]

        def fin(c, carry):
            rows = pl.ds(pl.multiple_of(c * 256, 256), 256)
            lv = al[rows, :]
            o = ao[rows, :] / lv
            o_ref[rows, :] = o
            lse_ref[rows, :] = am[rows, :] + jnp.log(lv)
            ms = _group_sum(o * o, gmat) * (1.0 / HEAD_DIM)
            cat_ref[rows, :] = (o * lax.rsqrt(ms + EPS) * gv).astype(BF16)
            return carry

        lax.fori_loop(0, seq // 256, fin, 0)

    nq = D_ATTN // LANES
    blk = lambda off: pl.BlockSpec((seq, LANES), lambda b, h: (b, h + off))
    scratch = [pltpu.VMEM((seq, LANES), F32) for _ in range(9)]
    return pl.pallas_call(
        body, name="attn_fwd", grid=(nbatch, nq),
        in_specs=[blk(0), blk(nq), blk(2 * nq), pl.BlockSpec((1, LANES), lambda b, h: (0, h)),
                  pl.BlockSpec((N_HEADS, 2 * BAND), lambda b, h: (0, 0))],
        out_specs=[blk(0), blk(0), blk(0)],
        out_shape=[jax.ShapeDtypeStruct((t, D_ATTN), F32), jax.ShapeDtypeStruct((t, D_ATTN), F32),
                   jax.ShapeDtypeStruct((t, D_MODEL), BF16)],
        scratch_shapes=scratch,
        compiler_params=_params(("parallel", "parallel")),
    )(proj, proj, proj, attn_g, _slopes_table())


def _attn_bwd(proj, o, lse, d_cat, attn_g, nbatch, seq):
    t = nbatch * seq
    scale = HEAD_DIM ** -0.5

    def body(q_ref, k_ref, v_ref, o_ref, lse_ref, dy_ref, g_ref, sl_ref, dproj_ref, dg_ref,
             do_n, dl_n, dq_n, dk_n, dv_n, pq, pk, pv, pdo, plse, pdl, pdq, pdk, pdv, stage, sems):
        hp = pl.program_id(0)
        hmask = _head_masks()
        slope = [sl_ref[pl.ds(2 * hp + hh, 1), :] for hh in range(2)]
        gmat = _group_matrix(LANES)
        gv = g_ref[...]

        def prep(c, dg):
            rows = pl.ds(pl.multiple_of(c * 256, 256), 256)
            ov = o_ref[rows, :]
            dyn = dy_ref[rows, :].astype(F32)
            r = lax.rsqrt(_group_sum(ov * ov, gmat) * (1.0 / HEAD_DIM) + EPS)
            gd = dyn * gv
            oh = ov * r
            do = r * (gd - oh * (_group_sum(gd * oh, gmat) * (1.0 / HEAD_DIM)))
            do_n[rows, :] = do
            dl_n[rows, :] = _group_sum(do * ov, gmat)
            return dg + jnp.sum(dyn * oh, axis=0, keepdims=True)

        dg = lax.fori_loop(0, seq // 256, prep, jnp.zeros((1, LANES), F32))

        @pl.when(pl.program_id(1) == 0)
        def _():
            dg_ref[...] = dg

        @pl.when(pl.program_id(1) > 0)
        def _():
            dg_ref[...] += dg

        def clear(*refs):
            def step(c, carry):
                rows = pl.ds(pl.multiple_of(c * 256, 256), 256)
                for ref in refs:
                    ref[rows, :] = jnp.zeros((256, LANES), F32)
                return carry

            lax.fori_loop(0, seq // 256, step, 0)

        clear(dq_n, dk_n, dv_n)

        def run_branch(dil, qs, ks, vs, dos, lses, dls, dqs, dks, dvs):
            bias = _pair_bias(slope, dil)

            def per_head(xv):
                return jnp.concatenate([xv[:, 0:1], xv[:, HEAD_DIM:HEAD_DIM + 1]], axis=0)

            def block(row0, prow):
                cur = pl.ds(row0, BAND)
                keys = cur if prow is None else pl.ds(prow, 2 * BAND)
                q2 = _stack_heads(qs[cur, :] * scale, hmask)
                do2 = _stack_heads(dos[cur, :], hmask)
                kk, vv = ks[keys, :], vs[keys, :]
                s = lax.dot_general(q2, kk.astype(BF16), NT, preferred_element_type=F32)
                s = s + (bias[:, BAND:] if prow is None else bias)
                p = jnp.exp(s - per_head(lses[cur, :]))
                dp = lax.dot_general(do2, vv.astype(BF16), NT, preferred_element_type=F32)
                ds = (p * (dp - per_head(dls[cur, :]))).astype(BF16)
                dqs[cur, :] += jnp.dot(jnp.concatenate([ds[:BAND], ds[BAND:]], axis=1), _stack_heads(kk, hmask),
                                       preferred_element_type=F32)
                dks[keys, :] += lax.dot_general(ds, q2, TN, preferred_element_type=F32)
                dvs[keys, :] += lax.dot_general(p.astype(BF16), do2, TN, preferred_element_type=F32)

            _for_blocks(seq, dil, block, BWD_UNROLL)

        run_branch(1, q_ref, k_ref, v_ref, do_n, lse_ref, dl_n, dq_n, dk_n, dv_n)
        for dil in DILATIONS[1:]:
            length = seq // dil
            for src, dst in ((q_ref, pq), (k_ref, pk), (v_ref, pv), (do_n, pdo), (lse_ref, plse), (dl_n, pdl)):
                _permute_in(src, dst, dil, seq)
            clear(pdq, pdk, pdv)
            run_branch(dil, pq, pk, pv, pdo, plse, pdl, pdq, pdk, pdv)
            for r in range(dil):
                nat = pl.ds(r, length, stride=dil)
                per = pl.ds(r * length, length)
                dq_n[nat, :] += pdq[per, :]
                dk_n[nat, :] += pdk[per, :]
                dv_n[nat, :] += pdv[per, :]

        def emit(c, carry):
            rows = pl.ds(pl.multiple_of(c * 256, 256), 256)
            stage[0, rows, :] = (dq_n[rows, :] * scale).astype(BF16)
            stage[1, rows, :] = dk_n[rows, :].astype(BF16)
            stage[2, rows, :] = dv_n[rows, :].astype(BF16)
            return carry

        lax.fori_loop(0, seq // 256, emit, 0)
        _store_columns(stage, dproj_ref, sems, pl.program_id(1) * seq, seq, [hp, nq + hp, 2 * nq + hp])

    nq = D_ATTN // LANES
    blk = lambda off: pl.BlockSpec((seq, LANES), lambda h, b: (b, h + off))
    vec = pl.BlockSpec((1, LANES), lambda h, b: (0, h))
    scratch = [pltpu.VMEM((seq, LANES), F32) for _ in range(14)]
    scratch += [pltpu.VMEM((3, seq, LANES), BF16), pltpu.SemaphoreType.DMA((3,))]
    d_proj, dg = pl.pallas_call(
        body, name="attn_bwd", grid=(nq, nbatch),
        in_specs=[blk(0), blk(nq), blk(2 * nq), blk(0), blk(0), blk(0), vec,
                  pl.BlockSpec((N_HEADS, 2 * BAND), lambda h, b: (0, 0))],
        out_specs=[pl.BlockSpec(memory_space=pl.ANY), vec],
        out_shape=[jax.ShapeDtypeStruct((t, D_IN), BF16), jax.ShapeDtypeStruct((1, D_ATTN), F32)],
        scratch_shapes=scratch,
        compiler_params=_params(("arbitrary", "arbitrary")),
    )(proj, proj, proj, o, lse, d_cat, attn_g, _slopes_table())
    return d_proj, dg


HALO = SUBLANES
PACKED_ROWS = 2 * SUBLANES


def _window(ref, c, rows, nchunks, after):
    row0 = pl.multiple_of(c * rows, rows)
    prev0 = pl.multiple_of(jnp.maximum(row0 - PACKED_ROWS, 0), PACKED_ROWS)
    before = ref[pl.ds(prev0, PACKED_ROWS), :].astype(F32)[PACKED_ROWS - HALO:] * (c > 0).astype(F32)
    parts = [before, ref[pl.ds(row0, rows), :].astype(F32)]
    if after:
        next0 = pl.multiple_of(jnp.minimum(row0 + rows, (nchunks - 1) * rows), PACKED_ROWS)
        parts.append(ref[pl.ds(next0, PACKED_ROWS), :].astype(F32)[:HALO] * (c < nchunks - 1).astype(F32))
    return jnp.concatenate(parts, axis=0)


def _behind(z):
    z1 = _shift_rows(z, 1)
    return z1, _shift_rows(z1, 1)


def _ahead(dy):
    d1 = _shift_rows(dy, -1)
    return d1, _shift_rows(d1, -1)


def _conv(z, w):
    z1, z2 = _behind(z)
    return w[0:1] * z2 + w[1:2] * z1 + w[2:3] * z


def _conv_bwd(dy, z, w, cur):
    d1, d2 = _ahead(dy)
    dz = w[2:3] * dy + w[1:2] * d1 + w[0:1] * d2
    return dz, [jnp.sum((d * z)[cur], axis=0, keepdims=True) for d in (d2, d1, dy)]


def _sigmoid(a):
    return 0.5 * jnp.tanh(0.5 * a) + 0.5


MIX_ROWS = 256
GATE_B_BLOCK = 3 * D_ATTN // LANES
GATE_C_BLOCK = GATE_B_BLOCK + D_CONV // LANES
U_BLOCK = GATE_C_BLOCK + D_CONV // LANES


def _convmix_fwd(proj, cat, mcw, conv_g, nbatch, seq):
    nchunks = seq // MIX_ROWS

    def body(gb_ref, gc_ref, u_ref, w_ref, g_ref, cat_in, cat_ref):
        del cat_in
        gmat = _group_matrix(LANES)
        w = w_ref[...]
        gv = g_ref[...]

        def step(c, carry):
            cur = pl.ds(pl.multiple_of(c * MIX_ROWS, MIX_ROWS), MIX_ROWS)
            z = _window(gc_ref, c, MIX_ROWS, nchunks, False) * _window(u_ref, c, MIX_ROWS, nchunks, False)
            y = gb_ref[cur, :] * _conv(z, w)[HALO:]
            ms = _group_sum(y * y, gmat) * (1.0 / HEAD_DIM)
            cat_ref[cur, :] = (y * lax.rsqrt(ms + EPS) * gv).astype(BF16)
            return carry

        lax.fori_loop(0, nchunks, step, 0)

    nc = D_CONV // LANES
    blk = lambda off: pl.BlockSpec((seq, LANES), lambda b, j: (b, j + off))
    return pl.pallas_call(
        body, name="convmix_fwd", grid=(nbatch, nc),
        in_specs=[blk(GATE_B_BLOCK), blk(GATE_C_BLOCK), blk(U_BLOCK),
                  pl.BlockSpec((3, LANES), lambda b, j: (0, j)), pl.BlockSpec((1, LANES), lambda b, j: (0, j)),
                  pl.BlockSpec(memory_space=pl.ANY)],
        out_specs=blk(D_ATTN // LANES),
        out_shape=jax.ShapeDtypeStruct(cat.shape, cat.dtype),
        input_output_aliases={5: 0},
        compiler_params=_params(("parallel", "parallel")),
    )(proj, proj, proj, mcw, conv_g, cat)


def _convmix_bwd(proj, d_cat, d_proj, mcw, conv_g, nbatch, seq):
    nchunks = seq // MIX_ROWS

    def body(gb_ref, gc_ref, u_ref, dy_ref, w_ref, g_ref, dproj_in, dproj_ref, dw_ref, dg_ref, stage, sems):
        del dproj_in
        cb = pl.program_id(0)
        b = pl.program_id(1)
        gmat = _group_matrix(LANES)
        w = w_ref[...]
        gv = g_ref[...]
        cur = slice(HALO, HALO + MIX_ROWS)

        def step(c, carry):
            rows = pl.ds(pl.multiple_of(c * MIX_ROWS, MIX_ROWS), MIX_ROWS)
            gb = _window(gb_ref, c, MIX_ROWS, nchunks, True)
            gc = _window(gc_ref, c, MIX_ROWS, nchunks, True)
            u = _window(u_ref, c, MIX_ROWS, nchunks, True)
            dyn = _window(dy_ref, c, MIX_ROWS, nchunks, True)
            z = gc * u
            conv = _conv(z, w)
            y = gb * conv
            r = lax.rsqrt(_group_sum(y * y, gmat) * (1.0 / HEAD_DIM) + EPS)
            yh = y * r
            gd = dyn * gv
            dy = r * (gd - yh * (_group_sum(gd * yh, gmat) * (1.0 / HEAD_DIM)))
            dz, dws = _conv_bwd(dy * gb, z, w, cur)
            stage[0, rows, :] = (dy * conv)[cur].astype(BF16)
            stage[1, rows, :] = (dz * u)[cur].astype(BF16)
            stage[2, rows, :] = (dz * gc)[cur].astype(BF16)
            dg = jnp.sum((dyn * yh)[cur], axis=0, keepdims=True)
            return tuple(a + d for a, d in zip(carry, dws + [dg]))

        zero = jnp.zeros((1, LANES), F32)
        dw0, dw1, dw2, dg = lax.fori_loop(0, nchunks, step, (zero, zero, zero, zero))

        @pl.when(b == 0)
        def _():
            dw_ref[0:1, :] = dw0
            dw_ref[1:2, :] = dw1
            dw_ref[2:3, :] = dw2
            dg_ref[...] = dg

        @pl.when(b > 0)
        def _():
            dw_ref[0:1, :] += dw0
            dw_ref[1:2, :] += dw1
            dw_ref[2:3, :] += dw2
            dg_ref[...] += dg

        _store_columns(stage, dproj_ref, sems, b * seq, seq, [GATE_B_BLOCK + cb, GATE_C_BLOCK + cb, U_BLOCK + cb])

    nc = D_CONV // LANES
    blk = lambda off: pl.BlockSpec((seq, LANES), lambda j, b: (b, j + off))
    return pl.pallas_call(
        body, name="convmix_bwd", grid=(nc, nbatch),
        in_specs=[blk(GATE_B_BLOCK), blk(GATE_C_BLOCK), blk(U_BLOCK), blk(D_ATTN // LANES),
                  pl.BlockSpec((3, LANES), lambda j, b: (0, j)), pl.BlockSpec((1, LANES), lambda j, b: (0, j)),
                  pl.BlockSpec(memory_space=pl.ANY)],
        out_specs=[pl.BlockSpec(memory_space=pl.ANY), pl.BlockSpec((3, LANES), lambda j, b: (0, j)),
                   pl.BlockSpec((1, LANES), lambda j, b: (0, j))],
        out_shape=[jax.ShapeDtypeStruct(d_proj.shape, d_proj.dtype), jax.ShapeDtypeStruct((3, D_CONV), F32),
                   jax.ShapeDtypeStruct((1, D_CONV), F32)],
        scratch_shapes=[pltpu.VMEM((3, seq, LANES), BF16), pltpu.SemaphoreType.DMA((3,))],
        input_output_aliases={6: 0},
        compiler_params=_params(("arbitrary", "arbitrary")),
    )(proj, proj, proj, d_cat, mcw, conv_g, d_proj)


FFN_ROWS = 256


def _ffn_act_fwd(pre, fcw, nbatch, seq):
    t = nbatch * seq
    nchunks = seq // FFN_ROWS

    def body(pre_ref, w_ref, act_ref):
        wa = w_ref[0]
        wc = w_ref[1]

        def step(c, carry):
            cur = pl.ds(pl.multiple_of(c * FFN_ROWS, FFN_ROWS), FFN_ROWS)
            a = _conv(_window(pre_ref.at[0], c, FFN_ROWS, nchunks, False), wa)[HALO:]
            v = _conv(_window(pre_ref.at[1], c, FFN_ROWS, nchunks, False), wc)[HALO:]
            act_ref[cur, :] = (a * _sigmoid(a) * v).astype(BF16)
            return carry

        lax.fori_loop(0, nchunks, step, 0)

    return pl.pallas_call(
        body, name="ffn_act_fwd", grid=(N_UP_PAIRS, nbatch),
        in_specs=[pl.BlockSpec((2, None, seq, UP_CHUNK), lambda i, b: (0, i, b, 0)),
                  pl.BlockSpec((2, None, 3, UP_CHUNK), lambda i, b: (0, i, 0, 0))],
        out_specs=pl.BlockSpec((None, seq, UP_CHUNK), lambda i, b: (i, b, 0)),
        out_shape=jax.ShapeDtypeStruct((N_UP_PAIRS, t, UP_CHUNK), BF16),
        compiler_params=_params(("parallel", "parallel")),
    )(pre, fcw)


def _ffn_down(pre, fcw, wdown, res, g, seq, name):
    t, d = res.shape
    tiles_per_seq = seq // FFN_ROWS

    def body(main_ref, halo_ref, w_ref, wd_ref, r_ref, *rest):
        if g is None:
            x_ref, act_ref = rest
        else:
            g_ref, x_ref, h_ref, act_ref = rest
        inside = ((pl.program_id(0) % tiles_per_seq) > 0).astype(F32)

        def window(part, p):
            before = halo_ref[part, p].astype(F32)[PACKED_ROWS - HALO:] * inside
            return jnp.concatenate([before, main_ref[part, p].astype(F32)], axis=0)

        total = r_ref[...]
        for p in range(N_UP_PAIRS):
            a = _conv(window(0, p), w_ref[0, p])[HALO:]
            v = _conv(window(1, p), w_ref[1, p])[HALO:]
            act = (a * _sigmoid(a) * v).astype(BF16)
            act_ref[p] = act
            total = total + jnp.dot(act, wd_ref[p], preferred_element_type=F32)
        x_ref[...] = total
        if g is not None:
            h_ref[...] = (total * lax.rsqrt(jnp.mean(total * total, axis=-1, keepdims=True) + EPS) * g_ref[...]).astype(BF16)

    row = pl.BlockSpec((FFN_ROWS, d), lambda i: (i, 0))
    tiles_per_halo = FFN_ROWS // PACKED_ROWS
    in_specs = [
        pl.BlockSpec((2, N_UP_PAIRS, FFN_ROWS, UP_CHUNK), lambda i: (0, 0, i, 0)),
        pl.BlockSpec((2, N_UP_PAIRS, PACKED_ROWS, UP_CHUNK), lambda i: (0, 0, jnp.maximum(i * tiles_per_halo - 1, 0), 0)),
        pl.BlockSpec((2, N_UP_PAIRS, 3, UP_CHUNK), lambda i: (0, 0, 0, 0)),
        pl.BlockSpec((N_UP_PAIRS, UP_CHUNK, d), lambda i: (0, 0, 0)), row]
    out_specs = [row]
    out_shape = [jax.ShapeDtypeStruct((t, d), F32)]
    args = [pre, pre, fcw, wdown, res]
    if g is not None:
        in_specs.append(pl.BlockSpec((1, d), lambda i: (0, 0)))
        out_specs.append(row)
        out_shape.append(jax.ShapeDtypeStruct((t, d), BF16))
        args.append(g)
    out_specs.append(pl.BlockSpec((N_UP_PAIRS, FFN_ROWS, UP_CHUNK), lambda i: (0, i, 0)))
    out_shape.append(jax.ShapeDtypeStruct((N_UP_PAIRS, t, UP_CHUNK), BF16))
    return pl.pallas_call(
        body, name=name, grid=(t // FFN_ROWS,), in_specs=in_specs, out_specs=out_specs, out_shape=out_shape,
        compiler_params=_params(("parallel",)),
    )(*args)


def _ffn_fwd(h, wup, fcw, wdown, res, g, seq, name):
    t, d = res.shape
    tiles_per_seq = seq // FFN_ROWS

    def body(hm_ref, hp_ref, wu_ref, w_ref, wd_ref, r_ref, *rest):
        if g is None:
            x_ref, act_ref, pre_ref = rest
        else:
            g_ref, x_ref, h_ref, act_ref, pre_ref = rest
        inside = ((pl.program_id(0) % tiles_per_seq) > 0).astype(F32)
        wrow = lax.broadcasted_iota(jnp.int32, (FFN_ROWS + HALO, 1), 0)
        edge = jnp.where(wrow < HALO, inside, 1.0)
        rows = jnp.concatenate([hp_ref[...], hm_ref[...]], axis=0)

        def up(j, part, p):
            full = lax.dot_general(rows, wu_ref[j], NT, preferred_element_type=F32).astype(BF16)
            pre_ref[part, p] = full[PACKED_ROWS:]
            return full.astype(F32)[PACKED_ROWS - HALO:] * edge

        total = r_ref[...]
        for p in range(N_UP_PAIRS):
            a = _conv(up(p, 0, p), w_ref[0, p])[HALO:]
            v = _conv(up(N_UP_PAIRS + p, 1, p), w_ref[1, p])[HALO:]
            act = (a * _sigmoid(a) * v).astype(BF16)
            act_ref[p] = act
            total = total + jnp.dot(act, wd_ref[p], preferred_element_type=F32)
        x_ref[...] = total
        if g is not None:
            h_ref[...] = (total * lax.rsqrt(jnp.mean(total * total, axis=-1, keepdims=True) + EPS) * g_ref[...]).astype(BF16)

    row = pl.BlockSpec((FFN_ROWS, d), lambda i: (i, 0))
    tiles_per_halo = FFN_ROWS // PACKED_ROWS
    in_specs = [
        row, pl.BlockSpec((PACKED_ROWS, d), lambda i: (jnp.maximum(i * tiles_per_halo - 1, 0), 0)),
        pl.BlockSpec((N_DEV, UP_CHUNK, d), lambda i: (0, 0, 0)),
        pl.BlockSpec((2, N_UP_PAIRS, 3, UP_CHUNK), lambda i: (0, 0, 0, 0)),
        pl.BlockSpec((N_UP_PAIRS, UP_CHUNK, d), lambda i: (0, 0, 0)), row]
    out_specs = [row]
    out_shape = [jax.ShapeDtypeStruct((t, d), F32)]
    args = [h, h, wup, fcw, wdown, res]
    if g is not None:
        in_specs.append(pl.BlockSpec((1, d), lambda i: (0, 0)))
        out_specs.append(row)
        out_shape.append(jax.ShapeDtypeStruct((t, d), BF16))
        args.append(g)
    out_specs += [pl.BlockSpec((N_UP_PAIRS, FFN_ROWS, UP_CHUNK), lambda i: (0, i, 0)),
                  pl.BlockSpec((2, N_UP_PAIRS, FFN_ROWS, UP_CHUNK), lambda i: (0, 0, i, 0))]
    out_shape += [jax.ShapeDtypeStruct((N_UP_PAIRS, t, UP_CHUNK), BF16),
                  jax.ShapeDtypeStruct((2, N_UP_PAIRS, t, UP_CHUNK), BF16)]
    return pl.pallas_call(
        body, name=name, grid=(t // FFN_ROWS,), in_specs=in_specs, out_specs=out_specs, out_shape=out_shape,
        compiler_params=_params(("parallel",)),
    )(*args)


def _ffn_act_bwd(pre, d_act, fcw, nbatch, seq):
    nchunks = seq // FFN_ROWS

    def body(pre_ref, da_ref, w_ref, dpre_ref, dw_ref):
        b = pl.program_id(1)
        wa = w_ref[0]
        wc = w_ref[1]
        cur = slice(HALO, HALO + FFN_ROWS)

        def step(c, carry):
            rows = pl.ds(pl.multiple_of(c * FFN_ROWS, FFN_ROWS), FFN_ROWS)
            pg = _window(pre_ref.at[0], c, FFN_ROWS, nchunks, True)
            pv = _window(pre_ref.at[1], c, FFN_ROWS, nchunks, True)
            dact = _window(da_ref, c, FFN_ROWS, nchunks, True)
            a = _conv(pg, wa)
            v = _conv(pv, wc)
            sg = _sigmoid(a)
            asg = a * sg
            dzg, dwg = _conv_bwd(dact * v * (sg + asg - asg * sg), pg, wa, cur)
            dzv, dwv = _conv_bwd(dact * asg, pv, wc, cur)
            dpre_ref[0, rows, :] = dzg[cur].astype(BF16)
            dpre_ref[1, rows, :] = dzv[cur].astype(BF16)
            return tuple(acc + d for acc, d in zip(carry, dwg + dwv))

        zero = jnp.zeros((1, UP_CHUNK), F32)
        sums = lax.fori_loop(0, nchunks, step, (zero,) * 6)

        @pl.when(b == 0)
        def _():
            for i in range(6):
                dw_ref[i // 3, pl.ds(i % 3, 1), :] = sums[i]

        @pl.when(b > 0)
        def _():
            for i in range(6):
                dw_ref[i // 3, pl.ds(i % 3, 1), :] += sums[i]

    pair = pl.BlockSpec((2, None, seq, UP_CHUNK), lambda i, b: (0, i, b, 0))
    wspec = pl.BlockSpec((2, None, 3, UP_CHUNK), lambda i, b: (0, i, 0, 0))
    return pl.pallas_call(
        body, name="ffn_act_bwd", grid=(N_UP_PAIRS, nbatch),
        in_specs=[pair, pl.BlockSpec((None, seq, UP_CHUNK), lambda i, b: (i, b, 0)), wspec],
        out_specs=[pair, wspec],
        out_shape=[jax.ShapeDtypeStruct(pre.shape, BF16), jax.ShapeDtypeStruct(fcw.shape, F32)],
        compiler_params=_params(("parallel", "arbitrary")),
    )(pre, d_act, fcw)


def _ffn_up_bwd(pre, dy, fcw, wdown, wup, x, g, dres, seq, name):
    t, d = x.shape
    tiles_per_seq = seq // FFN_ROWS
    tiles_per_halo = FFN_ROWS // PACKED_ROWS
    last_halo = t // PACKED_ROWS - 1

    def body(pm_ref, pp_ref, pn_ref, dm_ref, dp_ref, dn_ref, w_ref, wd_ref, wu_ref, x_ref, g_ref, dres_ref,
             dpre_ref, dw_ref, dx_ref, dxb_ref, dg_ref):
        i = pl.program_id(0)
        has_prev = ((i % tiles_per_seq) > 0).astype(F32)
        has_next = ((i % tiles_per_seq) < tiles_per_seq - 1).astype(F32)
        cur = slice(HALO, HALO + FFN_ROWS)

        def window(before, main, after):
            return jnp.concatenate([before.astype(F32)[PACKED_ROWS - HALO:] * has_prev, main.astype(F32),
                                    after.astype(F32)[:HALO] * has_next], axis=0)

        dy_rows = jnp.concatenate([dp_ref[...], dm_ref[...], dn_ref[...]], axis=0)
        wrow = lax.broadcasted_iota(jnp.int32, (FFN_ROWS + 2 * HALO, 1), 0)
        edge = jnp.where(wrow < HALO, has_prev, jnp.where(wrow >= HALO + FFN_ROWS, has_next, 1.0))

        dh = jnp.zeros((FFN_ROWS, d), F32)
        sums = []
        for p in range(N_UP_PAIRS):
            pg = window(pp_ref[0, p], pm_ref[0, p], pn_ref[0, p])
            pv = window(pp_ref[1, p], pm_ref[1, p], pn_ref[1, p])
            dact = lax.dot_general(dy_rows, wd_ref[p], NT, preferred_element_type=F32)
            dact = dact[PACKED_ROWS - HALO:PACKED_ROWS + FFN_ROWS + HALO] * edge
            a = _conv(pg, w_ref[0, p])
            v = _conv(pv, w_ref[1, p])
            sg = _sigmoid(a)
            asg = a * sg
            dzg, dwg = _conv_bwd(dact * v * (sg + asg - asg * sg), pg, w_ref[0, p], cur)
            dzv, dwv = _conv_bwd(dact * asg, pv, w_ref[1, p], cur)
            dgate = dzg[cur].astype(BF16)
            dval = dzv[cur].astype(BF16)
            dpre_ref[0, p] = dgate
            dpre_ref[1, p] = dval
            dh = dh + jnp.dot(dgate, wu_ref[p], preferred_element_type=F32)
            dh = dh + jnp.dot(dval, wu_ref[N_UP_PAIRS + p], preferred_element_type=F32)
            sums.append(dwg + dwv)

        xv = x_ref[...]
        r = lax.rsqrt(jnp.mean(xv * xv, axis=-1, keepdims=True) + EPS)
        xh = xv * r
        gd = dh * g_ref[...]
        dx = r * (gd - xh * jnp.mean(gd * xh, axis=-1, keepdims=True)) + dres_ref[...]
        dx_ref[...] = dx
        dxb_ref[...] = dx.astype(BF16)
        part = jnp.sum(dh * xh, axis=0, keepdims=True)

        @pl.when(i == 0)
        def _():
            dg_ref[...] = part
            for p in range(N_UP_PAIRS):
                for k in range(6):
                    dw_ref[k // 3, p, pl.ds(k % 3, 1), :] = sums[p][k]

        @pl.when(i > 0)
        def _():
            dg_ref[...] += part
            for p in range(N_UP_PAIRS):
                for k in range(6):
                    dw_ref[k // 3, p, pl.ds(k % 3, 1), :] += sums[p][k]

    def rows4(n):
        return lambda fn: pl.BlockSpec((2, N_UP_PAIRS, n, UP_CHUNK), lambda i: (0, 0, fn(i), 0))

    def rows2(n):
        return lambda fn: pl.BlockSpec((n, d), lambda i: (fn(i), 0))

    prev_tile = lambda i: jnp.maximum(i * tiles_per_halo - 1, 0)
    next_tile = lambda i: jnp.minimum((i + 1) * tiles_per_halo, last_halo)
    row = pl.BlockSpec((FFN_ROWS, d), lambda i: (i, 0))
    vec = pl.BlockSpec((1, d), lambda i: (0, 0))
    wspec = pl.BlockSpec((2, N_UP_PAIRS, 3, UP_CHUNK), lambda i: (0, 0, 0, 0))
    return pl.pallas_call(
        body, name=name, grid=(t // FFN_ROWS,),
        in_specs=[rows4(FFN_ROWS)(lambda i: i), rows4(PACKED_ROWS)(prev_tile), rows4(PACKED_ROWS)(next_tile),
                  rows2(FFN_ROWS)(lambda i: i), rows2(PACKED_ROWS)(prev_tile), rows2(PACKED_ROWS)(next_tile),
                  wspec, pl.BlockSpec((N_UP_PAIRS, UP_CHUNK, d), lambda i: (0, 0, 0)),
                  pl.BlockSpec((N_DEV, UP_CHUNK, d), lambda i: (0, 0, 0)), row, vec, row],
        out_specs=[rows4(FFN_ROWS)(lambda i: i), wspec, row, row, vec],
        out_shape=[jax.ShapeDtypeStruct(pre.shape, BF16), jax.ShapeDtypeStruct(fcw.shape, F32),
                   jax.ShapeDtypeStruct((t, d), F32), jax.ShapeDtypeStruct((t, d), BF16),
                   jax.ShapeDtypeStruct((1, d), F32)],
        compiler_params=_params(("arbitrary",)),
    )(pre, pre, pre, dy, dy, dy, fcw, wdown, wup, x, g, dres)


def _adamw(lands, w, m, v, row_tile, name, after=()):
    nl = len(lands)
    _, nr, ncol = lands[0].shape
    c1 = 1.0 - ADAM_B1 ** ADAM_STEP
    c2 = 1.0 - ADAM_B2 ** ADAM_STEP

    def body(*refs):
        land_refs = refs[:nl]
        w_ref, m_ref, v_ref = refs[nl:nl + 3]
        g_ref, d_ref, mo_ref, vo_ref = refs[nl + 3 + len(after):]
        for l in range(nl):
            @pl.when(pl.program_id(0) == l)
            def _(l=l):
                g = land_refs[l][0].astype(F32)
                for j in range(1, N_DEV):
                    g = g + land_refs[l][j].astype(F32)
                g_ref[...] = g

        g = g_ref[...]
        m2 = ADAM_B1 * m_ref[...] + (1.0 - ADAM_B1) * g
        v2 = ADAM_B2 * v_ref[...] + (1.0 - ADAM_B2) * (g * g)
        mo_ref[...] = m2
        vo_ref[...] = v2
        d_ref[...] = -ADAM_LR * ((m2 / c1) / (jnp.sqrt(v2 / c2) + ADAM_EPS) + ADAM_WD * w_ref[...])

    def land_spec(l):
        return pl.BlockSpec((N_DEV, row_tile, ncol), lambda k, i: (0, jnp.where(k == l, i, 0), 0))

    tile = pl.BlockSpec((None, row_tile, ncol), lambda k, i: (k, i, 0))
    return pl.pallas_call(
        body, name=name, grid=(nl, nr // row_tile),
        in_specs=[land_spec(l) for l in range(nl)] + [tile, tile, tile] + [pl.BlockSpec(memory_space=pl.ANY)] * len(after),
        out_specs=[tile] * 4,
        out_shape=[jax.ShapeDtypeStruct(w.shape, F32)] * 4,
        compiler_params=_params(("arbitrary", "arbitrary")),
    )(*lands, w, m, v, *after)


class _Item:
    def __init__(self, src, chunked, land_cols=False):
        self.src, self.chunked, self.land_cols = src, chunked, land_cols
        if chunked == "cols":
            block = (src.shape[0], src.shape[1] // N_DEV)
        else:
            block = src.shape[1:] if chunked else src.shape
        self.width = block[-1]
        self.land_shape = (block[0], N_DEV * block[1]) if land_cols else (N_DEV,) + block

    def _cols(self, first, count=1):
        return pl.ds(pl.multiple_of(first * self.width, LANES), count * self.width)

    def part(self, src_ref, j):
        if self.chunked == "cols":
            return src_ref.at[:, self._cols(j)]
        return src_ref.at[j] if self.chunked else src_ref

    def slot(self, land_ref, s):
        return land_ref.at[:, self._cols(s)] if self.land_cols else land_ref.at[s]

    def seven(self, land_ref):
        return land_ref.at[:, self._cols(0, N_DEV - 1)] if self.land_cols else land_ref.at[pl.ds(0, N_DEV - 1)]


def _mesh_place():
    x, y, c = lax.axis_index("x"), lax.axis_index("y"), lax.axis_index("c")
    return x, y, c, 4 * x + 2 * y + c


def _flipped(x, y, c, k):
    px = 1 - x if k & 4 else x
    py = 1 - y if k & 2 else y
    pc = 1 - c if k & 1 else c
    return (px, py, pc), 4 * px + 2 * py + pc


PEER_ORDER = (2, 4, 6, 3, 5, 7, 1)


def _exchange(items, name):
    n = len(items)

    def body(*refs):
        srcs, lands = refs[:n], refs[n:2 * n]
        send, recv, local = refs[2 * n:]
        x, y, c, me = _mesh_place()

        def copy(i, k, chunk, slot, dev):
            return pltpu.make_async_remote_copy(
                src_ref=items[i].part(srcs[i], chunk), dst_ref=items[i].slot(lands[i], slot),
                send_sem=send.at[i, k - 1], recv_sem=recv.at[i, k - 1], device_id=dev, device_id_type=MESH)

        own = [pltpu.make_async_copy(items[i].part(srcs[i], me), items[i].slot(lands[i], me), local.at[i])
               for i in range(n)]
        for k in PEER_ORDER:
            dev, idx = _flipped(x, y, c, k)
            for i in range(n):
                copy(i, k, idx, me, dev).start()
        for cp in own:
            cp.start()
        for k in PEER_ORDER:
            dev, idx = _flipped(x, y, c, k)
            for i in range(n):
                copy(i, k, me, idx, dev).wait_recv()
        for k in PEER_ORDER:
            dev, idx = _flipped(x, y, c, k)
            for i in range(n):
                copy(i, k, idx, me, dev).wait_send()
        for cp in own:
            cp.wait()

    hbm = pl.BlockSpec(memory_space=pl.ANY)
    return pl.pallas_call(
        body, name=name,
        in_specs=[hbm] * n, out_specs=[hbm] * n,
        out_shape=[jax.ShapeDtypeStruct(it.land_shape, it.src.dtype) for it in items],
        scratch_shapes=[pltpu.SemaphoreType.DMA((n, N_DEV - 1)), pltpu.SemaphoreType.DMA((n, N_DEV - 1)),
                        pltpu.SemaphoreType.DMA((n,))],
        compiler_params=pltpu.CompilerParams(has_side_effects=True),
    )(*[it.src for it in items])


SAME_CORE = (2, 4, 6)


def _sequencer_gather(items, name, collective_id):
    n = len(items)

    def body(*refs):
        srcs, lands = refs[:n], refs[n:2 * n]
        send, recv, local = refs[2 * n:]
        x, y, c, me = _mesh_place()
        sibling, _ = _flipped(x, y, c, 1)
        barrier = pltpu.get_barrier_semaphore()
        for k in SAME_CORE + (1,):
            pl.semaphore_signal(barrier, inc=1, device_id=_flipped(x, y, c, k)[0], device_id_type=MESH)
        pl.semaphore_wait(barrier, len(SAME_CORE) + 1)

        def copy(i, q, src, slot, dev):
            return pltpu.make_async_remote_copy(
                src_ref=src, dst_ref=items[i].slot(lands[i], slot),
                send_sem=send.at[i, q - 1], recv_sem=recv.at[i, q - 1], device_id=dev, device_id_type=MESH)

        own = [pltpu.make_async_copy(srcs[i], items[i].slot(lands[i], me), local.at[i]) for i in range(n)]
        for cp in own:
            cp.start()
        for k in SAME_CORE + (1,):
            for i in range(n):
                copy(i, k, srcs[i], me, _flipped(x, y, c, k)[0]).start()
        for k in SAME_CORE:
            dev, idx = _flipped(x, y, c, k)
            for i in range(n):
                copy(i, k, srcs[i], idx, dev).wait_recv()
            for i in range(n):
                copy(i, k + 1, items[i].slot(lands[i], idx), idx, sibling).start()
        for k in (1,) + tuple(k + 1 for k in SAME_CORE):
            _, idx = _flipped(x, y, c, k)
            for i in range(n):
                copy(i, k, srcs[i], idx, sibling).wait_recv()
        for k in range(1, N_DEV):
            for i in range(n):
                copy(i, k, srcs[i], me, sibling).wait_send()
        for cp in own:
            cp.wait()

    return pl.kernel(
        body, name=name,
        out_type=[jax.ShapeDtypeStruct(it.land_shape, it.src.dtype) for it in items],
        mesh=plsc.ScalarSubcoreMesh(axis_name="sequencer", num_cores=1),
        scratch_types=[pltpu.SemaphoreType.DMA((n, N_DEV - 1)), pltpu.SemaphoreType.DMA((n, N_DEV - 1)),
                       pltpu.SemaphoreType.DMA((n,))],
        compiler_params=pltpu.CompilerParams(collective_id=collective_id),
    )(*[it.src for it in items])


def _sequencer_exchange(items, name, collective_id):
    n = len(items)

    def body(*refs):
        srcs, lands = refs[:n], refs[n:2 * n]
        send, recv, local = refs[2 * n:]
        x, y, c, me = _mesh_place()
        barrier = pltpu.get_barrier_semaphore()
        for k in PEER_ORDER:
            pl.semaphore_signal(barrier, inc=1, device_id=_flipped(x, y, c, k)[0], device_id_type=MESH)
        pl.semaphore_wait(barrier, N_DEV - 1)

        def copy(i, k, chunk, slot, dev):
            return pltpu.make_async_remote_copy(
                src_ref=items[i].part(srcs[i], chunk), dst_ref=items[i].slot(lands[i], slot),
                send_sem=send.at[i, k - 1], recv_sem=recv.at[i, k - 1], device_id=dev, device_id_type=MESH)

        own = [pltpu.make_async_copy(items[i].part(srcs[i], me), items[i].slot(lands[i], me), local.at[i])
               for i in range(n)]
        for cp in own:
            cp.start()
        for k in PEER_ORDER:
            dev, idx = _flipped(x, y, c, k)
            for i in range(n):
                copy(i, k, idx, me, dev).start()
        for k in PEER_ORDER:
            dev, idx = _flipped(x, y, c, k)
            for i in range(n):
                copy(i, k, me, idx, dev).wait_recv()
        for k in PEER_ORDER:
            dev, idx = _flipped(x, y, c, k)
            for i in range(n):
                copy(i, k, idx, me, dev).wait_send()
        for cp in own:
            cp.wait()

    return pl.kernel(
        body, name=name,
        out_type=[jax.ShapeDtypeStruct(it.land_shape, it.src.dtype) for it in items],
        mesh=plsc.ScalarSubcoreMesh(axis_name="sequencer", num_cores=1),
        scratch_types=[pltpu.SemaphoreType.DMA((n, N_DEV - 1)), pltpu.SemaphoreType.DMA((n, N_DEV - 1)),
                       pltpu.SemaphoreType.DMA((n,))],
        compiler_params=pltpu.CompilerParams(collective_id=collective_id),
    )(*[it.src for it in items])


HBM_SPEC = pl.BlockSpec(memory_space=pltpu.HBM)
SEM_SPEC = pl.BlockSpec(memory_space=pltpu.SEMAPHORE)
DATAFLOW = pltpu.SideEffectType.DATAFLOW_SIDE_EFFECTING


def _exchange_start(items, name, after=()):
    n = len(items)
    na = len(after)

    def body(*refs):
        srcs, land_ins = refs[:n], refs[n:2 * n]
        outs = refs[2 * n + na:6 * n + na]
        (local,) = refs[6 * n + na:]
        del land_ins
        x, y, c, me = _mesh_place()
        own = [pltpu.make_async_copy(items[i].part(srcs[i], me), items[i].slot(outs[4 * i + 3], me), local.at[i])
               for i in range(n)]
        for cp in own:
            cp.start()
        for cp in own:
            cp.wait()
        for k in PEER_ORDER:
            dev, idx = _flipped(x, y, c, k)
            for i in range(n):
                send, recv, _, land = outs[4 * i:4 * i + 4]
                pltpu.make_async_remote_copy(
                    src_ref=items[i].part(srcs[i], idx), dst_ref=items[i].slot(land, me), send_sem=send, recv_sem=recv,
                    device_id=dev, device_id_type=MESH).start()

    out_shape, out_specs, args, lands = [], [], [], []
    for it in items:
        out_shape += [pltpu.SemaphoreType.DMA(()), pltpu.SemaphoreType.DMA(()),
                      pltpu.HBM(it.src.shape, it.src.dtype), pltpu.HBM(it.land_shape, it.src.dtype)]
        out_specs += [SEM_SPEC, SEM_SPEC, HBM_SPEC, HBM_SPEC]
        args.append(pltpu.with_memory_space_constraint(it.src, pltpu.HBM))
        lands.append(pltpu.with_memory_space_constraint(lax.empty(it.land_shape, it.src.dtype), pltpu.HBM))
    outs = pl.pallas_call(
        body, name=name,
        in_specs=[HBM_SPEC] * (2 * n) + [pl.BlockSpec(memory_space=pl.ANY)] * na,
        out_specs=out_specs, out_shape=out_shape,
        scratch_shapes=[pltpu.SemaphoreType.DMA((n,))],
        input_output_aliases={**{i: 4 * i + 2 for i in range(n)}, **{n + i: 4 * i + 3 for i in range(n)}},
        compiler_params=pltpu.CompilerParams(has_side_effects=DATAFLOW),
    )(*args, *lands, *after)
    return [tuple(outs[4 * i:4 * i + 4]) + (items[i],) for i in range(n)]


def _started(handles):
    return handles[0][2]


def _exchange_wait(handles, after, name):
    n = len(handles)

    def body(*refs):
        x, y, c, _ = _mesh_place()
        for i in range(n):
            src, land, send, recv = refs[4 * i:4 * i + 4]
            del src
            seven = handles[i][4].seven(land)
            cp = pltpu.make_async_remote_copy(src_ref=seven, dst_ref=seven, send_sem=send, recv_sem=recv,
                                              device_id=(x, y, 1 - c), device_id_type=MESH)
            cp.wait_send()
            cp.wait_recv()

    args, in_specs, out_shape = [], [], []
    for send, recv, src, land, _ in handles:
        args += [src, land, send, recv]
        in_specs += [HBM_SPEC, HBM_SPEC, SEM_SPEC, SEM_SPEC]
        out_shape += [pltpu.HBM(src.shape, src.dtype), pltpu.HBM(land.shape, land.dtype)]
    outs = pl.pallas_call(
        body, name=name,
        in_specs=in_specs + [pl.BlockSpec(memory_space=pl.ANY)] * len(after), out_specs=[HBM_SPEC] * (2 * n),
        out_shape=out_shape,
        input_output_aliases={**{4 * i: 2 * i for i in range(n)}, **{4 * i + 1: 2 * i + 1 for i in range(n)}},
        compiler_params=pltpu.CompilerParams(has_side_effects=DATAFLOW),
    )(*args, *after)
    return [outs[2 * i + 1] for i in range(n)]


TM = 1024
TM_ACC = 512
TN_IN = 768


def kernel(x, norm1_g, w_in, mix_conv_w, attn_out_g, conv_out_g, w_out, norm2_g, ffn_up, ffn_conv_w, ffn_down, final_norm_g, loss_target, m_norm1_g, m_w_in, m_mix_conv_w, m_attn_out_g, m_conv_out_g, m_w_out, m_norm2_g, m_ffn_up, m_ffn_conv_w, m_ffn_down, m_final_norm_g, v_norm1_g, v_w_in, v_mix_conv_w, v_attn_out_g, v_conv_out_g, v_w_out, v_norm2_g, v_ffn_up, v_ffn_conv_w, v_ffn_down, v_final_norm_g):
    nbatch, seq, d = x.shape
    t = nbatch * seq
    nt, nta = t // TM, t // TM_ACC
    out_rows = D_MODEL // N_DEV
    down_rows = D_FF // N_DEV
    xf = x.reshape(t, d)
    target = loss_target.reshape(t, d)

    cw_local = jnp.concatenate([ffn_conv_w, mix_conv_w], axis=-1)
    cast = lambda w: _Item(w.astype(BF16), False)
    cast_in = lambda w: _Item(w.astype(BF16), False, land_cols=True)
    cw_all, win0 = _sequencer_gather([_Item(cw_local, False), cast_in(w_in[0])], "gather_a", 0)
    up_t, m_up_t, v_up_t = (jnp.swapaxes(a, 1, 2) for a in (ffn_up, m_ffn_up, v_ffn_up))
    wout0, wup0 = _sequencer_gather([cast(w_out[0]), cast(up_t[0])], "gather_b", 1)
    (wdown0,) = _sequencer_gather([cast(ffn_down[0])], "gather_c", 2)
    win1, wout1 = _sequencer_gather([cast_in(w_in[1]), cast(w_out[1])], "gather_d", 3)
    wup1, wdown1 = _sequencer_gather([cast(up_t[1]), cast(ffn_down[1])], "gather_e", 7)
    win, wup = [win0, win1], [wup0, wup1]
    wout = [w.reshape(D_MODEL, D_MODEL) for w in (wout0, wout1)]
    wdown = [w.reshape(N_UP_PAIRS, UP_CHUNK, D_MODEL) for w in (wdown0, wdown1)]
    fcw = [cw_all[:, k, :, :UP_CHUNK].reshape(2, N_UP_PAIRS, 3, UP_CHUNK) for k in range(DEPTH)]
    mcw = [cw_all[:, k, :, UP_CHUNK:].transpose(1, 0, 2).reshape(3, D_CONV) for k in range(DEPTH)]

    full = lambda i, j, k: (0, 0)

    saved = []
    xin = xf
    h1 = _rms_fwd(xin, norm1_g[0][None], "rms1_fwd_0")
    rows_of = lambda width: pl.BlockSpec((TM_ACC, width), lambda i: (i, 0))
    whole = lambda *shape: pl.BlockSpec(shape, lambda i: (0,) * len(shape))
    chunks_of = lambda n: pl.BlockSpec((n, TM_ACC, UP_CHUNK), lambda i: (0, i, 0))
    for l in range(DEPTH):
        proj = _matmul(
            h1, win[l], grid=(nt, D_IN // TN_IN, 1), dims=NN, name=f"proj_{l}",
            a_spec=pl.BlockSpec((TM, D_MODEL), lambda i, j, k: (i, 0)),
            b_spec=pl.BlockSpec((D_MODEL, TN_IN), lambda i, j, k: (0, j)),
            o_spec=pl.BlockSpec((TM, TN_IN), lambda i, j, k: (i, j)), o_shape=(t, D_IN), o_dtype=F32)
        o, lse, cat = _attn_fwd(proj, attn_out_g[l][None], nbatch, seq)
        cat = _convmix_fwd(proj, cat, mcw[l], conv_out_g[l][None], nbatch, seq)
        xmid, h2 = _matmul_norm(cat, wout[l], xin, norm2_g[l][None], dims=NN, name=f"mix_out_{l}",
                                a_spec=rows_of(D_MODEL), b_spec=whole(D_MODEL, D_MODEL))
        if l + 1 < DEPTH:
            xout, h_next, act, pre = _ffn_fwd(
                h2, wup[l], fcw[l], wdown[l], xmid, norm1_g[l + 1][None], seq, f"ffn_fwd_{l}")
        else:
            h_next = None
            xout, act, pre = _ffn_fwd(h2, wup[l], fcw[l], wdown[l], xmid, None, seq, f"ffn_fwd_{l}")
        saved.append((xin, h1, proj, o, lse, cat, xmid, h2, pre, act))
        xin, h1 = xout, h_next

    loss_part, dx, dxb, dgf = _loss_head(xin, final_norm_g[None], target, "loss_head")

    dg1, dg2, dga, dgc = [None] * DEPTH, [None] * DEPTH, [None] * DEPTH, [None] * DEPTH
    for l in reversed(range(DEPTH)):
        xin, h1, proj, o, lse, cat, xmid, h2, pre, act = saved[l]
        g_down = _matmul(
            act, dxb, grid=(N_UP_PAIRS, 1, 1), dims=TN, name=f"g_down_{l}",
            a_spec=pl.BlockSpec((None, t, UP_CHUNK), lambda i, j, k: (i, 0, 0)),
            b_spec=pl.BlockSpec((t, D_MODEL), full),
            o_spec=pl.BlockSpec((None, UP_CHUNK, D_MODEL), lambda i, j, k: (i, 0, 0)),
            o_shape=(N_UP_PAIRS, UP_CHUNK, D_MODEL), o_dtype=BF16).reshape(N_DEV, down_rows, D_MODEL)
        d_pre, d_fcw, dxm, dxmb, dg2[l] = _ffn_up_bwd(
            pre, dxb, fcw[l], wdown[l], wup[l], xmid, norm2_g[l][None], dx, seq, f"ffn_bwd_{l}")
        d_pre = d_pre.reshape(N_DEV, t, UP_CHUNK)
        g_up = _matmul(
            d_pre, h2, grid=(N_DEV, 1, 1), dims=TN, name=f"g_up_{l}",
            a_spec=pl.BlockSpec((None, t, UP_CHUNK), lambda i, j, k: (i, 0, 0)),
            b_spec=pl.BlockSpec((t, D_MODEL), full),
            o_spec=pl.BlockSpec((None, UP_CHUNK, D_MODEL), lambda i, j, k: (i, 0, 0)),
            o_shape=(N_DEV, UP_CHUNK, D_MODEL), o_dtype=BF16)
        g_out = _matmul(
            cat, dxmb, grid=(1, 1, nt), dims=TN, name=f"g_out_{l}",
            a_spec=pl.BlockSpec((TM, D_MODEL), lambda i, j, k: (k, 0)),
            b_spec=pl.BlockSpec((TM, D_MODEL), lambda i, j, k: (k, 0)),
            o_spec=pl.BlockSpec((D_MODEL, D_MODEL), full),
            o_shape=(D_MODEL, D_MODEL), o_dtype=BF16).reshape(N_DEV, out_rows, D_MODEL)
        if l == 0:
            land_out0, land_up0, land_down0 = _sequencer_exchange(
                [_Item(g_out, True), _Item(g_up, True), _Item(g_down, True)], "scatter_0a", 5)
        d_cat = _matmul(
            dxmb, wout[l], grid=(nta, 1, 1), dims=NT, name=f"d_cat_{l}",
            a_spec=pl.BlockSpec((TM_ACC, D_MODEL), lambda i, j, k: (i, 0)),
            b_spec=pl.BlockSpec((D_MODEL, D_MODEL), full),
            o_spec=pl.BlockSpec((TM_ACC, D_MODEL), lambda i, j, k: (i, 0)), o_shape=(t, D_MODEL), o_dtype=BF16)
        d_proj, dga[l] = _attn_bwd(proj, o, lse, d_cat, attn_out_g[l][None], nbatch, seq)
        d_proj, d_mcw, dgc[l] = _convmix_bwd(proj, d_cat, d_proj, mcw[l], conv_out_g[l][None], nbatch, seq)
        g_in = _matmul(
            h1, d_proj, grid=(1, D_IN // TN_IN, 1), dims=TN, name=f"g_in_{l}",
            a_spec=pl.BlockSpec((t, D_MODEL), full),
            b_spec=pl.BlockSpec((t, TN_IN), lambda i, j, k: (0, j)),
            o_spec=pl.BlockSpec((D_MODEL, TN_IN), lambda i, j, k: (0, j)),
            o_shape=(D_MODEL, D_IN), o_dtype=BF16)
        g_cw = jnp.concatenate(
            [d_fcw.reshape(N_DEV, 3, UP_CHUNK), d_mcw.reshape(3, N_DEV, D_CONV // N_DEV).transpose(1, 0, 2)], axis=-1)
        if l == 0:
            land_in0, land_cw0 = _sequencer_exchange([_Item(g_in, "cols"), _Item(g_cw, True)], "scatter_0b", 6)
        else:
            land_in1, land_out1, land_up1, land_down1, land_cw1 = _sequencer_exchange(
                [_Item(g_in, "cols"), _Item(g_out, True), _Item(g_up, True), _Item(g_down, True), _Item(g_cw, True)],
                "scatter_1", 4)
        dx, dxb, dg1[l] = _matmul_norm_bwd(
            d_proj, win[l], xin, norm1_g[l][None], dxm, dims=NT, name=f"d_h1_{l}",
            a_spec=rows_of(D_IN), b_spec=whole(D_MODEL, D_IN))

    def pack_small(n1, a, c, n2, f):
        return jnp.concatenate(
            [n1, n2, f[None], jnp.concatenate([a, c], axis=-1), jnp.zeros((1, D_MODEL), F32)], axis=0)[None]

    small = jnp.concatenate(
        [dg1[0], dg1[1], dg2[0], dg2[1], dgf,
         jnp.concatenate([dga[0], dgc[0]], axis=-1), jnp.concatenate([dga[1], dgc[1]], axis=-1),
         jnp.pad(loss_part, ((0, 0), (0, D_MODEL - LANES)))], axis=0)
    (land_small,) = _exchange([_Item(small, False)], "gather_gain_grads")
    res_small = _adamw(
        [land_small], pack_small(norm1_g, attn_out_g, conv_out_g, norm2_g, final_norm_g),
        pack_small(m_norm1_g, m_attn_out_g, m_conv_out_g, m_norm2_g, m_final_norm_g),
        pack_small(v_norm1_g, v_attn_out_g, v_conv_out_g, v_norm2_g, v_final_norm_g), SUBLANES, "adamw_gains")
    res_out = _adamw([land_out0, land_out1], w_out, m_w_out, v_w_out, out_rows, "adamw_w_out", after=[res_small[0]])
    res_up_t = _adamw([land_up0, land_up1], up_t, m_up_t, v_up_t, UP_CHUNK // 4, "adamw_ffn_up", after=[res_out[0]])
    res_up = [jnp.swapaxes(r, 1, 2) for r in res_up_t]
    res_down = _adamw([land_down0, land_down1], ffn_down, m_ffn_down, v_ffn_down, down_rows, "adamw_ffn_down",
                      after=[res_up_t[0]])
    res_in = _adamw([land_in0, land_in1], w_in, m_w_in, v_w_in, 256, "adamw_w_in", after=[res_down[0]])
    res_cw = _adamw(
        [land_cw0, land_cw1], cw_local, jnp.concatenate([m_ffn_conv_w, m_mix_conv_w], axis=-1),
        jnp.concatenate([v_ffn_conv_w, v_mix_conv_w], axis=-1), 3, "adamw_conv_w", after=[res_in[0]])

    loss = res_small[0][0, SUBLANES - 1, 0]

    def unpack(kind):
        s = res_small[kind][0]
        cwr = res_cw[kind]
        return (s[0:2], res_in[kind], cwr[..., UP_CHUNK:], s[5:7, :D_ATTN], s[5:7, D_ATTN:], res_out[kind],
                s[2:4], res_up[kind], cwr[..., :UP_CHUNK], res_down[kind], s[4])

    return (loss, dx.reshape(nbatch, seq, d), *unpack(0), *unpack(1), *unpack(2), *unpack(3))
```

```python
import math

import jax
import jax.numpy as jnp
from jax import lax
from jax.experimental import pallas as pl
from jax.experimental.pallas import tpu as pltpu
from jax.experimental.pallas import tpu_sc as plsc

F32 = jnp.float32
BF16 = jnp.bfloat16

D_MODEL = 1024
D_ATTN = 512
D_CONV = 512
HEAD_DIM = 64
N_HEADS = 8
D_FF = 2816
DEPTH = 2
D_IN = 3 * D_ATTN + 3 * D_CONV
EPS = 1e-6
DILATIONS = (1, 4, 16)
BAND = 128
N_DEV = 8
UP_CHUNK = 2 * D_FF // N_DEV
N_UP_PAIRS = N_DEV // 2
CW_PACK = UP_CHUNK + D_CONV // N_DEV
ADAM_LR = 0.001
ADAM_B1 = 0.9
ADAM_B2 = 0.999
ADAM_EPS = 1e-08
ADAM_WD = 0.01
ADAM_STEP = 10
LANES = 128
SUBLANES = 8
VMEM_LIMIT = 56 * 1024 * 1024

NEG = -1e30
MESH = pl.DeviceIdType.MESH


def _params(sem=None, vmem=VMEM_LIMIT):
    return pltpu.CompilerParams(dimension_semantics=sem, vmem_limit_bytes=vmem)


NN = (((1,), (0,)), ((), ()))
NT = (((1,), (1,)), ((), ()))
TN = (((0,), (0,)), ((), ()))
TN_PIECE = 1024


def _contract(a_ref, b_ref, dims):
    def dot(av, bv):
        return lax.dot_general(av.astype(BF16), bv.astype(BF16), dims, preferred_element_type=F32)

    if len(a_ref.shape) == 2:
        if dims == TN and a_ref.shape[0] > TN_PIECE:
            part = None
            for r0 in range(0, a_ref.shape[0], TN_PIECE):
                piece = dot(a_ref[pl.ds(r0, TN_PIECE), :], b_ref[pl.ds(r0, TN_PIECE), :])
                part = piece if part is None else part + piece
            return part
        return dot(a_ref[...], b_ref[...])
    part = dot(a_ref[0], b_ref[0])
    for c in range(1, a_ref.shape[0]):
        part = part + dot(a_ref[c], b_ref[c])
    return part


def _matmul(a, b, *, grid, a_spec, b_spec, o_spec, o_shape, o_dtype, dims, name, res=None, res_spec=None, after=()):
    nk = grid[2]
    o_block = tuple(s for s in o_spec.block_shape if s is not None)
    na = len(after)

    def body(*refs):
        refs = refs[:2 + (res is not None)] + refs[2 + (res is not None) + na:]
        if res is None:
            a_ref, b_ref, o_ref, *scr = refs
            r_ref = None
        else:
            a_ref, b_ref, r_ref, o_ref, *scr = refs
        part = _contract(a_ref, b_ref, dims)

        def finish(total):
            if r_ref is not None:
                total = total + r_ref[...]
            o_ref[...] = total.astype(o_dtype)

        if nk == 1:
            finish(part)
        else:
            acc = scr[0]
            k = pl.program_id(2)

            @pl.when(k == 0)
            def _():
                acc[...] = part

            @pl.when(k > 0)
            def _():
                acc[...] += part

            @pl.when(k == nk - 1)
            def _():
                finish(acc[...])

    in_specs = [a_spec, b_spec] + ([res_spec] if res is not None else []) + [pl.BlockSpec(memory_space=pl.ANY)] * na
    args = (a, b) + ((res,) if res is not None else ()) + tuple(after)
    return pl.pallas_call(
        body, name=name, grid=grid, in_specs=in_specs, out_specs=o_spec,
        out_shape=jax.ShapeDtypeStruct(o_shape, o_dtype),
        scratch_shapes=[pltpu.VMEM(o_block, F32)] if nk > 1 else [],
        compiler_params=_params(("parallel", "parallel", "arbitrary")),
    )(*args)


ROW_TILE = 512


def _rms_fwd(x, g, name):
    t, d = x.shape

    def body(x_ref, g_ref, h_ref):
        xv = x_ref[...]
        r = lax.rsqrt(jnp.mean(xv * xv, axis=-1, keepdims=True) + EPS)
        h_ref[...] = (xv * r * g_ref[...]).astype(BF16)

    return pl.pallas_call(
        body, name=name, grid=(t // ROW_TILE,),
        in_specs=[pl.BlockSpec((ROW_TILE, d), lambda i: (i, 0)), pl.BlockSpec((1, d), lambda i: (0, 0))],
        out_specs=pl.BlockSpec((ROW_TILE, d), lambda i: (i, 0)),
        out_shape=jax.ShapeDtypeStruct((t, d), BF16),
        compiler_params=_params(("parallel",)),
    )(x, g)


def _matmul_norm(a, b, res, g, *, a_spec, b_spec, dims, name):
    t, d = res.shape

    def body(a_ref, b_ref, r_ref, g_ref, x_ref, h_ref):
        xv = _contract(a_ref, b_ref, dims) + r_ref[...]
        x_ref[...] = xv
        h_ref[...] = (xv * lax.rsqrt(jnp.mean(xv * xv, axis=-1, keepdims=True) + EPS) * g_ref[...]).astype(BF16)

    row = pl.BlockSpec((TM_ACC, d), lambda i: (i, 0))
    return pl.pallas_call(
        body, name=name, grid=(t // TM_ACC,),
        in_specs=[a_spec, b_spec, row, pl.BlockSpec((1, d), lambda i: (0, 0))], out_specs=[row, row],
        out_shape=[jax.ShapeDtypeStruct((t, d), F32), jax.ShapeDtypeStruct((t, d), BF16)],
        compiler_params=_params(("parallel",)),
    )(a, b, res, g)


def _matmul_norm_bwd(a, b, x, g, dres, *, a_spec, b_spec, dims, name):
    t, d = x.shape

    def body(a_ref, b_ref, x_ref, g_ref, dres_ref, dx_ref, dxb_ref, dg_ref):
        dhv = _contract(a_ref, b_ref, dims)
        xv = x_ref[...]
        r = lax.rsqrt(jnp.mean(xv * xv, axis=-1, keepdims=True) + EPS)
        xh = xv * r
        gd = dhv * g_ref[...]
        dx = r * (gd - xh * jnp.mean(gd * xh, axis=-1, keepdims=True)) + dres_ref[...]
        dx_ref[...] = dx
        dxb_ref[...] = dx.astype(BF16)
        part = jnp.sum(dhv * xh, axis=0, keepdims=True)

        @pl.when(pl.program_id(0) == 0)
        def _():
            dg_ref[...] = part

        @pl.when(pl.program_id(0) > 0)
        def _():
            dg_ref[...] += part

    row = pl.BlockSpec((TM_ACC, d), lambda i: (i, 0))
    vec = pl.BlockSpec((1, d), lambda i: (0, 0))
    return pl.pallas_call(
        body, name=name, grid=(t // TM_ACC,),
        in_specs=[a_spec, b_spec, row, vec, row], out_specs=[row, row, vec],
        out_shape=[jax.ShapeDtypeStruct((t, d), F32), jax.ShapeDtypeStruct((t, d), BF16),
                   jax.ShapeDtypeStruct((1, d), F32)],
        compiler_params=_params(("arbitrary",)),
    )(a, b, x, g, dres)


def _loss_head(x, g, target, name):
    t, d = x.shape

    def body(x_ref, g_ref, t_ref, loss_ref, dx_ref, dxb_ref, dg_ref):
        xv = x_ref[...]
        r = lax.rsqrt(jnp.mean(xv * xv, axis=-1, keepdims=True) + EPS)
        xh = xv * r
        gv = g_ref[...]
        err = xh * gv - t_ref[...]
        loss = jnp.full((1, LANES), 0.5 / d, F32) * jnp.sum(err * err)
        dy = err * (1.0 / d)
        gd = dy * gv
        dx = r * (gd - xh * jnp.mean(gd * xh, axis=-1, keepdims=True))
        dx_ref[...] = dx
        dxb_ref[...] = dx.astype(BF16)
        part = jnp.sum(dy * xh, axis=0, keepdims=True)

        @pl.when(pl.program_id(0) == 0)
        def _():
            dg_ref[...] = part
            loss_ref[...] = loss

        @pl.when(pl.program_id(0) > 0)
        def _():
            dg_ref[...] += part
            loss_ref[...] += loss

    row = pl.BlockSpec((ROW_TILE, d), lambda i: (i, 0))
    vec = pl.BlockSpec((1, d), lambda i: (0, 0))
    return pl.pallas_call(
        body, name=name, grid=(t // ROW_TILE,),
        in_specs=[row, vec, row],
        out_specs=[pl.BlockSpec((1, LANES), lambda i: (0, 0)), row, row, vec],
        out_shape=[jax.ShapeDtypeStruct((1, LANES), F32), jax.ShapeDtypeStruct((t, d), F32),
                   jax.ShapeDtypeStruct((t, d), BF16), jax.ShapeDtypeStruct((1, d), F32)],
        compiler_params=_params(("arbitrary",)),
    )(x, g, target)


def _group_matrix(n):
    shift = int(math.log2(HEAD_DIM))
    r = lax.broadcasted_iota(jnp.int32, (n, n), 0) >> shift
    c = lax.broadcasted_iota(jnp.int32, (n, n), 1) >> shift
    return (r == c).astype(BF16)


def _group_sum(v, gmat):
    hi = v.astype(BF16)
    lo = (v - hi.astype(F32)).astype(BF16)

    def dot(p):
        return jnp.dot(p, gmat, preferred_element_type=F32)

    return dot(hi) + dot(lo)


def _shift_rows(ext, k):
    return pltpu.roll(ext, k % ext.shape[0], 0)


def _store_columns(stage, out_hbm, sems, row0, nrows, col_blocks):
    rows = pl.ds(pl.multiple_of(row0, SUBLANES * 2), nrows)
    copies = [
        pltpu.make_async_copy(stage.at[i], out_hbm.at[rows, pl.ds(pl.multiple_of(cb * LANES, LANES), LANES)], sems.at[i])
        for i, cb in enumerate(col_blocks)
    ]
    for cp in copies:
        cp.start()
    for cp in copies:
        cp.wait()


def _attn_consts(width):
    i = lax.broadcasted_iota(jnp.int32, (BAND, width), 0)
    j = lax.broadcasted_iota(jnp.int32, (BAND, width), 1)
    dist = (width - BAND) + i - j
    inwin = (dist >= 0) & (dist <= BAND)
    return dist.astype(F32), inwin, j


def _head_masks():
    lane = lax.broadcasted_iota(jnp.int32, (1, LANES), 1)
    return [(lane < HEAD_DIM).astype(F32), (lane >= HEAD_DIM).astype(F32)]


def _pair_bias(slope, dil):
    distf, inwin, _ = _attn_consts(2 * BAND)
    return jnp.concatenate([jnp.where(inwin, distf * (slope[hh] * (-float(dil))), NEG) for hh in range(2)], axis=0)


def _stack_heads(xv, hmask):
    return jnp.concatenate([xv * hmask[0], xv * hmask[1]], axis=0).astype(BF16)


FWD_UNROLL = 8
BWD_UNROLL = 8


def _unroll(trips, most):
    return max(u for u in range(1, most + 1) if trips % u == 0)


def _for_blocks(seq, dil, block, most):
    nb = seq // dil // BAND

    def residue(r, carry):
        base = r * nb
        block(pl.multiple_of(base * BAND, BAND), None)
        if nb > 1:
            def rest(n, c):
                block(pl.multiple_of((base + n) * BAND, BAND), pl.multiple_of((base + n - 1) * BAND, BAND))
                return c

            lax.fori_loop(1, nb, rest, 0, unroll=_unroll(nb - 1, most))
        return carry

    if dil == 1:
        residue(0, 0)
    else:
        lax.fori_loop(0, dil, residue, 0, unroll=_unroll(dil, max(1, most // nb)))


def _permute_in(src_ref, dst_ref, dil, seq):
    length = seq // dil
    for r in range(dil):
        dst_ref[pl.ds(r * length, length), :] = src_ref[pl.ds(r, length, stride=dil), :].astype(dst_ref.dtype)


def _slopes_table():
    slopes = 2.0 ** (-8.0 * jnp.arange(1, N_HEADS + 1, dtype=F32) / N_HEADS)
    return jnp.broadcast_to(slopes[:, None], (N_HEADS, 2 * BAND))


def _attn_fwd(proj, attn_g, nbatch, seq):
    t = nbatch * seq
    scale = HEAD_DIM ** -0.5

    def body(q_ref, k_ref, v_ref, g_ref, sl_ref, o_ref, lse_ref, cat_ref, pq, pk, pv, po, pm, pll, ao, am, al):
        hp = pl.program_id(1)
        hmask = _head_masks()
        slope = [sl_ref[pl.ds(2 * hp + hh, 1), :] for hh in range(2)]

        def run_branch(dil, qs, ks, vs, osink, msink, lsink):
            bias = _pair_bias(slope, dil)

            def block(row0, prow):
                cur = pl.ds(row0, BAND)
                q2 = _stack_heads(qs[cur, :] * scale, hmask)
                if prow is None:
                    kk, vv, bias_b = ks[cur, :], vs[cur, :], bias[:, BAND:]
                else:
                    prev = pl.ds(prow, BAND)
                    kk = jnp.concatenate([ks[prev, :], ks[cur, :]], axis=0)
                    vv = jnp.concatenate([vs[prev, :], vs[cur, :]], axis=0)
                    bias_b = bias
                s = lax.dot_general(q2, kk.astype(BF16), NT, preferred_element_type=F32) + bias_b
                m = jnp.max(s, axis=1, keepdims=True)
                p = jnp.exp(s - m)
                l = jnp.sum(p, axis=1, keepdims=True)
                pb = p.astype(BF16)
                o = jnp.dot(jnp.concatenate([pb[:BAND], pb[BAND:]], axis=1), _stack_heads(vv, hmask),
                            preferred_element_type=F32)
                osink[cur, :] = o
                msink[cur, :] = m[:BAND] * hmask[0] + m[BAND:] * hmask[1]
                lsink[cur, :] = l[:BAND] * hmask[0] + l[BAND:] * hmask[1]

            _for_blocks(seq, dil, block, FWD_UNROLL)

        run_branch(1, q_ref, k_ref, v_ref, ao, am, al)
        for dil in DILATIONS[1:]:
            length = seq // dil
            _permute_in(q_ref, pq, dil, seq)
            _permute_in(k_ref, pk, dil, seq)
            _permute_in(v_ref, pv, dil, seq)
            run_branch(dil, pq, pk, pv, po, pm, pll)
            for r in range(dil):
                nat = pl.ds(r, length, stride=dil)
                per = pl.ds(r * length, length)
                m0 = am[nat, :]
                mb = pm[per, :]
                mn = jnp.maximum(m0, mb)
                e0 = jnp.exp(m0 - mn)
                eb = jnp.exp(mb - mn)
                ao[nat, :] = ao[nat, :] * e0 + po[per, :] * eb
                al[nat, :] = al[nat, :] * e0 + pll[per, :] * eb
                am[nat, :] = mn

        gmat = _group_matrix(LANES)
        gv = g_ref[...]

        def fin(c, carry):
            rows = pl.ds(pl.multiple_of(c * 256, 256), 256)
            lv = al[rows, :]
            o = ao[rows, :] / lv
            o_ref[rows, :] = o
            lse_ref[rows, :] = am[rows, :] + jnp.log(lv)
            ms = _group_sum(o * o, gmat) * (1.0 / HEAD_DIM)
            cat_ref[rows, :] = (o * lax.rsqrt(ms + EPS) * gv).astype(BF16)
            return carry

        lax.fori_loop(0, seq // 256, fin, 0)

    nq = D_ATTN // LANES
    blk = lambda off: pl.BlockSpec((seq, LANES), lambda b, h: (b, h + off))
    scratch = [pltpu.VMEM((seq, LANES), F32) for _ in range(9)]
    return pl.pallas_call(
        body, name="attn_fwd", grid=(nbatch, nq),
        in_specs=[blk(0), blk(nq), blk(2 * nq), pl.BlockSpec((1, LANES), lambda b, h: (0, h)),
                  pl.BlockSpec((N_HEADS, 2 * BAND), lambda b, h: (0, 0))],
        out_specs=[blk(0), blk(0), blk(0)],
        out_shape=[jax.ShapeDtypeStruct((t, D_ATTN), F32), jax.ShapeDtypeStruct((t, D_ATTN), F32),
                   jax.ShapeDtypeStruct((t, D_MODEL), BF16)],
        scratch_shapes=scratch,
        compiler_params=_params(("parallel", "parallel")),
    )(proj, proj, proj, attn_g, _slopes_table())


def _attn_bwd(proj, o, lse, d_cat, attn_g, nbatch, seq):
    t = nbatch * seq
    scale = HEAD_DIM ** -0.5

    def body(q_ref, k_ref, v_ref, o_ref, lse_ref, dy_ref, g_ref, sl_ref, dproj_ref, dg_ref,
             do_n, dl_n, dq_n, dk_n, dv_n, pq, pk, pv, pdo, plse, pdl, pdq, pdk, pdv, stage, sems):
        hp = pl.program_id(0)
        hmask = _head_masks()
        slope = [sl_ref[pl.ds(2 * hp + hh, 1), :] for hh in range(2)]
        gmat = _group_matrix(LANES)
        gv = g_ref[...]

        def prep(c, dg):
            rows = pl.ds(pl.multiple_of(c * 256, 256), 256)
            ov = o_ref[rows, :]
            dyn = dy_ref[rows, :].astype(F32)
            r = lax.rsqrt(_group_sum(ov * ov, gmat) * (1.0 / HEAD_DIM) + EPS)
            gd = dyn * gv
            oh = ov * r
            do = r * (gd - oh * (_group_sum(gd * oh, gmat) * (1.0 / HEAD_DIM)))
            do_n[rows, :] = do
            dl_n[rows, :] = _group_sum(do * ov, gmat)
            return dg + jnp.sum(dyn * oh, axis=0, keepdims=True)

        dg = lax.fori_loop(0, seq // 256, prep, jnp.zeros((1, LANES), F32))

        @pl.when(pl.program_id(1) == 0)
        def _():
            dg_ref[...] = dg

        @pl.when(pl.program_id(1) > 0)
        def _():
            dg_ref[...] += dg

        def clear(*refs):
            def step(c, carry):
                rows = pl.ds(pl.multiple_of(c * 256, 256), 256)
                for ref in refs:
                    ref[rows, :] = jnp.zeros((256, LANES), F32)
                return carry

            lax.fori_loop(0, seq // 256, step, 0)

        clear(dq_n, dk_n, dv_n)

        def run_branch(dil, qs, ks, vs, dos, lses, dls, dqs, dks, dvs):
            bias = _pair_bias(slope, dil)

            def per_head(xv):
                return jnp.concatenate([xv[:, 0:1], xv[:, HEAD_DIM:HEAD_DIM + 1]], axis=0)

            def block(row0, prow):
                cur = pl.ds(row0, BAND)
                keys = cur if prow is None else pl.ds(prow, 2 * BAND)
                q2 = _stack_heads(qs[cur, :] * scale, hmask)
                do2 = _stack_heads(dos[cur, :], hmask)
                kk, vv = ks[keys, :], vs[keys, :]
                s = lax.dot_general(q2, kk.astype(BF16), NT, preferred_element_type=F32)
                s = s + (bias[:, BAND:] if prow is None else bias)
                p = jnp.exp(s - per_head(lses[cur, :]))
                dp = lax.dot_general(do2, vv.astype(BF16), NT, preferred_element_type=F32)
                ds = (p * (dp - per_head(dls[cur, :]))).astype(BF16)
                dqs[cur, :] += jnp.dot(jnp.concatenate([ds[:BAND], ds[BAND:]], axis=1), _stack_heads(kk, hmask),
                                       preferred_element_type=F32)
                dks[keys, :] += lax.dot_general(ds, q2, TN, preferred_element_type=F32)
                dvs[keys, :] += lax.dot_general(p.astype(BF16), do2, TN, preferred_element_type=F32)

            _for_blocks(seq, dil, block, BWD_UNROLL)

        run_branch(1, q_ref, k_ref, v_ref, do_n, lse_ref, dl_n, dq_n, dk_n, dv_n)
        for dil in DILATIONS[1:]:
            length = seq // dil
            for src, dst in ((q_ref, pq), (k_ref, pk), (v_ref, pv), (do_n, pdo), (lse_ref, plse), (dl_n, pdl)):
                _permute_in(src, dst, dil, seq)
            clear(pdq, pdk, pdv)
            run_branch(dil, pq, pk, pv, pdo, plse, pdl, pdq, pdk, pdv)
            for r in range(dil):
                nat = pl.ds(r, length, stride=dil)
                per = pl.ds(r * length, length)
                dq_n[nat, :] += pdq[per, :]
                dk_n[nat, :] += pdk[per, :]
                dv_n[nat, :] += pdv[per, :]

        def emit(c, carry):
            rows = pl.ds(pl.multiple_of(c * 256, 256), 256)
            stage[0, rows, :] = (dq_n[rows, :] * scale).astype(BF16)
            stage[1, rows, :] = dk_n[rows, :].astype(BF16)
            stage[2, rows, :] = dv_n[rows, :].astype(BF16)
            return carry

        lax.fori_loop(0, seq // 256, emit, 0)
        _store_columns(stage, dproj_ref, sems, pl.program_id(1) * seq, seq, [hp, nq + hp, 2 * nq + hp])

    nq = D_ATTN // LANES
    blk = lambda off: pl.BlockSpec((seq, LANES), lambda h, b: (b, h + off))
    vec = pl.BlockSpec((1, LANES), lambda h, b: (0, h))
    scratch = [pltpu.VMEM((seq, LANES), F32) for _ in range(14)]
    scratch += [pltpu.VMEM((3, seq, LANES), BF16), pltpu.SemaphoreType.DMA((3,))]
    d_proj, dg = pl.pallas_call(
        body, name="attn_bwd", grid=(nq, nbatch),
        in_specs=[blk(0), blk(nq), blk(2 * nq), blk(0), blk(0), blk(0), vec,
                  pl.BlockSpec((N_HEADS, 2 * BAND), lambda h, b: (0, 0))],
        out_specs=[pl.BlockSpec(memory_space=pl.ANY), vec],
        out_shape=[jax.ShapeDtypeStruct((t, D_IN), BF16), jax.ShapeDtypeStruct((1, D_ATTN), F32)],
        scratch_shapes=scratch,
        compiler_params=_params(("arbitrary", "arbitrary")),
    )(proj, proj, proj, o, lse, d_cat, attn_g, _slopes_table())
    return d_proj, dg


HALO = SUBLANES
PACKED_ROWS = 2 * SUBLANES


def _window(ref, c, rows, nchunks, after):
    row0 = pl.multiple_of(c * rows, rows)
    prev0 = pl.multiple_of(jnp.maximum(row0 - PACKED_ROWS, 0), PACKED_ROWS)
    before = ref[pl.ds(prev0, PACKED_ROWS), :].astype(F32)[PACKED_ROWS - HALO:] * (c > 0).astype(F32)
    parts = [before, ref[pl.ds(row0, rows), :].astype(F32)]
    if after:
        next0 = pl.multiple_of(jnp.minimum(row0 + rows, (nchunks - 1) * rows), PACKED_ROWS)
        parts.append(ref[pl.ds(next0, PACKED_ROWS), :].astype(F32)[:HALO] * (c < nchunks - 1).astype(F32))
    return jnp.concatenate(parts, axis=0)


def _behind(z):
    z1 = _shift_rows(z, 1)
    return z1, _shift_rows(z1, 1)


def _ahead(dy):
    d1 = _shift_rows(dy, -1)
    return d1, _shift_rows(d1, -1)


def _conv(z, w):
    z1, z2 = _behind(z)
    return w[0:1] * z2 + w[1:2] * z1 + w[2:3] * z


def _conv_bwd(dy, z, w, cur):
    d1, d2 = _ahead(dy)
    dz = w[2:3] * dy + w[1:2] * d1 + w[0:1] * d2
    return dz, [jnp.sum((d * z)[cur], axis=0, keepdims=True) for d in (d2, d1, dy)]


def _sigmoid(a):
    return 0.5 * jnp.tanh(0.5 * a) + 0.5


MIX_ROWS = 256
GATE_B_BLOCK = 3 * D_ATTN // LANES
GATE_C_BLOCK = GATE_B_BLOCK + D_CONV // LANES
U_BLOCK = GATE_C_BLOCK + D_CONV // LANES


def _convmix_fwd(proj, cat, mcw, conv_g, nbatch, seq):
    nchunks = seq // MIX_ROWS

    def body(gb_ref, gc_ref, u_ref, w_ref, g_ref, cat_in, cat_ref):
        del cat_in
        gmat = _group_matrix(LANES)
        w = w_ref[...]
        gv = g_ref[...]

        def step(c, carry):
            cur = pl.ds(pl.multiple_of(c * MIX_ROWS, MIX_ROWS), MIX_ROWS)
            z = _window(gc_ref, c, MIX_ROWS, nchunks, False) * _window(u_ref, c, MIX_ROWS, nchunks, False)
            y = gb_ref[cur, :] * _conv(z, w)[HALO:]
            ms = _group_sum(y * y, gmat) * (1.0 / HEAD_DIM)
            cat_ref[cur, :] = (y * lax.rsqrt(ms + EPS) * gv).astype(BF16)
            return carry

        lax.fori_loop(0, nchunks, step, 0)

    nc = D_CONV // LANES
    blk = lambda off: pl.BlockSpec((seq, LANES), lambda b, j: (b, j + off))
    return pl.pallas_call(
        body, name="convmix_fwd", grid=(nbatch, nc),
        in_specs=[blk(GATE_B_BLOCK), blk(GATE_C_BLOCK), blk(U_BLOCK),
                  pl.BlockSpec((3, LANES), lambda b, j: (0, j)), pl.BlockSpec((1, LANES), lambda b, j: (0, j)),
                  pl.BlockSpec(memory_space=pl.ANY)],
        out_specs=blk(D_ATTN // LANES),
        out_shape=jax.ShapeDtypeStruct(cat.shape, cat.dtype),
        input_output_aliases={5: 0},
        compiler_params=_params(("parallel", "parallel")),
    )(proj, proj, proj, mcw, conv_g, cat)


def _convmix_bwd(proj, d_cat, d_proj, mcw, conv_g, nbatch, seq):
    nchunks = seq // MIX_ROWS

    def body(gb_ref, gc_ref, u_ref, dy_ref, w_ref, g_ref, dproj_in, dproj_ref, dw_ref, dg_ref, stage, sems):
        del dproj_in
        cb = pl.program_id(0)
        b = pl.program_id(1)
        gmat = _group_matrix(LANES)
        w = w_ref[...]
        gv = g_ref[...]
        cur = slice(HALO, HALO + MIX_ROWS)

        def step(c, carry):
            rows = pl.ds(pl.multiple_of(c * MIX_ROWS, MIX_ROWS), MIX_ROWS)
            gb = _window(gb_ref, c, MIX_ROWS, nchunks, True)
            gc = _window(gc_ref, c, MIX_ROWS, nchunks, True)
            u = _window(u_ref, c, MIX_ROWS, nchunks, True)
            dyn = _window(dy_ref, c, MIX_ROWS, nchunks, True)
            z = gc * u
            conv = _conv(z, w)
            y = gb * conv
            r = lax.rsqrt(_group_sum(y * y, gmat) * (1.0 / HEAD_DIM) + EPS)
            yh = y * r
            gd = dyn * gv
            dy = r * (gd - yh * (_group_sum(gd * yh, gmat) * (1.0 / HEAD_DIM)))
            dz, dws = _conv_bwd(dy * gb, z, w, cur)
            stage[0, rows, :] = (dy * conv)[cur].astype(BF16)
            stage[1, rows, :] = (dz * u)[cur].astype(BF16)
            stage[2, rows, :] = (dz * gc)[cur].astype(BF16)
            dg = jnp.sum((dyn * yh)[cur], axis=0, keepdims=True)
            return tuple(a + d for a, d in zip(carry, dws + [dg]))

        zero = jnp.zeros((1, LANES), F32)
        dw0, dw1, dw2, dg = lax.fori_loop(0, nchunks, step, (zero, zero, zero, zero))

        @pl.when(b == 0)
        def _():
            dw_ref[0:1, :] = dw0
            dw_ref[1:2, :] = dw1
            dw_ref[2:3, :] = dw2
            dg_ref[...] = dg

        @pl.when(b > 0)
        def _():
            dw_ref[0:1, :] += dw0
            dw_ref[1:2, :] += dw1
            dw_ref[2:3, :] += dw2
            dg_ref[...] += dg

        _store_columns(stage, dproj_ref, sems, b * seq, seq, [GATE_B_BLOCK + cb, GATE_C_BLOCK + cb, U_BLOCK + cb])

    nc = D_CONV // LANES
    blk = lambda off: pl.BlockSpec((seq, LANES), lambda j, b: (b, j + off))
    return pl.pallas_call(
        body, name="convmix_bwd", grid=(nc, nbatch),
        in_specs=[blk(GATE_B_BLOCK), blk(GATE_C_BLOCK), blk(U_BLOCK), blk(D_ATTN // LANES),
                  pl.BlockSpec((3, LANES), lambda j, b: (0, j)), pl.BlockSpec((1, LANES), lambda j, b: (0, j)),
                  pl.BlockSpec(memory_space=pl.ANY)],
        out_specs=[pl.BlockSpec(memory_space=pl.ANY), pl.BlockSpec((3, LANES), lambda j, b: (0, j)),
                   pl.BlockSpec((1, LANES), lambda j, b: (0, j))],
        out_shape=[jax.ShapeDtypeStruct(d_proj.shape, d_proj.dtype), jax.ShapeDtypeStruct((3, D_CONV), F32),
                   jax.ShapeDtypeStruct((1, D_CONV), F32)],
        scratch_shapes=[pltpu.VMEM((3, seq, LANES), BF16), pltpu.SemaphoreType.DMA((3,))],
        input_output_aliases={6: 0},
        compiler_params=_params(("arbitrary", "arbitrary")),
    )(proj, proj, proj, d_cat, mcw, conv_g, d_proj)


FFN_ROWS = 256


def _ffn_fwd(h, wup, fcw, wdown, res, g, seq, name):
    t, d = res.shape
    tiles_per_seq = seq // FFN_ROWS

    def body(hm_ref, hp_ref, wu_ref, w_ref, wd_ref, r_ref, *rest):
        if g is None:
            x_ref, act_ref, pre_ref = rest
        else:
            g_ref, x_ref, h_ref, act_ref, pre_ref = rest
        inside = ((pl.program_id(0) % tiles_per_seq) > 0).astype(F32)
        wrow = lax.broadcasted_iota(jnp.int32, (FFN_ROWS + HALO, 1), 0)
        edge = jnp.where(wrow < HALO, inside, 1.0)
        rows = jnp.concatenate([hp_ref[...], hm_ref[...]], axis=0)

        def up(j, part, p):
            full = lax.dot_general(rows, wu_ref[j], NT, preferred_element_type=F32).astype(BF16)
            pre_ref[part, p] = full[PACKED_ROWS:]
            return full.astype(F32)[PACKED_ROWS - HALO:] * edge

        total = r_ref[...]
        for p in range(N_UP_PAIRS):
            a = _conv(up(p, 0, p), w_ref[0, p])[HALO:]
            v = _conv(up(N_UP_PAIRS + p, 1, p), w_ref[1, p])[HALO:]
            act = (a * _sigmoid(a) * v).astype(BF16)
            act_ref[p] = act
            total = total + jnp.dot(act, wd_ref[p], preferred_element_type=F32)
        x_ref[...] = total
        if g is not None:
            h_ref[...] = (total * lax.rsqrt(jnp.mean(total * total, axis=-1, keepdims=True) + EPS) * g_ref[...]).astype(BF16)

    row = pl.BlockSpec((FFN_ROWS, d), lambda i: (i, 0))
    tiles_per_halo = FFN_ROWS // PACKED_ROWS
    in_specs = [
        row, pl.BlockSpec((PACKED_ROWS, d), lambda i: (jnp.maximum(i * tiles_per_halo - 1, 0), 0)),
        pl.BlockSpec((N_DEV, UP_CHUNK, d), lambda i: (0, 0, 0)),
        pl.BlockSpec((2, N_UP_PAIRS, 3, UP_CHUNK), lambda i: (0, 0, 0, 0)),
        pl.BlockSpec((N_UP_PAIRS, UP_CHUNK, d), lambda i: (0, 0, 0)), row]
    out_specs = [row]
    out_shape = [jax.ShapeDtypeStruct((t, d), F32)]
    args = [h, h, wup, fcw, wdown, res]
    if g is not None:
        in_specs.append(pl.BlockSpec((1, d), lambda i: (0, 0)))
        out_specs.append(row)
        out_shape.append(jax.ShapeDtypeStruct((t, d), BF16))
        args.append(g)
    out_specs += [pl.BlockSpec((N_UP_PAIRS, FFN_ROWS, UP_CHUNK), lambda i: (0, i, 0)),
                  pl.BlockSpec((2, N_UP_PAIRS, FFN_ROWS, UP_CHUNK), lambda i: (0, 0, i, 0))]
    out_shape += [jax.ShapeDtypeStruct((N_UP_PAIRS, t, UP_CHUNK), BF16),
                  jax.ShapeDtypeStruct((2, N_UP_PAIRS, t, UP_CHUNK), BF16)]
    return pl.pallas_call(
        body, name=name, grid=(t // FFN_ROWS,), in_specs=in_specs, out_specs=out_specs, out_shape=out_shape,
        compiler_params=_params(("parallel",)),
    )(*args)


def _ffn_up_bwd(pre, dy, fcw, wdown, wup, x, g, dres, seq, name):
    t, d = x.shape
    tiles_per_seq = seq // FFN_ROWS
    tiles_per_halo = FFN_ROWS // PACKED_ROWS
    last_halo = t // PACKED_ROWS - 1

    def body(pm_ref, pp_ref, pn_ref, dm_ref, dp_ref, dn_ref, w_ref, wd_ref, wu_ref, x_ref, g_ref, dres_ref,
             dpre_ref, dw_ref, dx_ref, dxb_ref, dg_ref):
        i = pl.program_id(0)
        has_prev = ((i % tiles_per_seq) > 0).astype(F32)
        has_next = ((i % tiles_per_seq) < tiles_per_seq - 1).astype(F32)
        cur = slice(HALO, HALO + FFN_ROWS)

        def window(before, main, after):
            return jnp.concatenate([before.astype(F32)[PACKED_ROWS - HALO:] * has_prev, main.astype(F32),
                                    after.astype(F32)[:HALO] * has_next], axis=0)

        dy_rows = jnp.concatenate([dp_ref[...], dm_ref[...], dn_ref[...]], axis=0)
        wrow = lax.broadcasted_iota(jnp.int32, (FFN_ROWS + 2 * HALO, 1), 0)
        edge = jnp.where(wrow < HALO, has_prev, jnp.where(wrow >= HALO + FFN_ROWS, has_next, 1.0))

        dh = jnp.zeros((FFN_ROWS, d), F32)
        sums = []
        for p in range(N_UP_PAIRS):
            pg = window(pp_ref[0, p], pm_ref[0, p], pn_ref[0, p])
            pv = window(pp_ref[1, p], pm_ref[1, p], pn_ref[1, p])
            dact = lax.dot_general(dy_rows, wd_ref[p], NT, preferred_element_type=F32)
            dact = dact[PACKED_ROWS - HALO:PACKED_ROWS + FFN_ROWS + HALO] * edge
            a = _conv(pg, w_ref[0, p])
            v = _conv(pv, w_ref[1, p])
            sg = _sigmoid(a)
            asg = a * sg
            dzg, dwg = _conv_bwd(dact * v * (sg + asg - asg * sg), pg, w_ref[0, p], cur)
            dzv, dwv = _conv_bwd(dact * asg, pv, w_ref[1, p], cur)
            dgate = dzg[cur].astype(BF16)
            dval = dzv[cur].astype(BF16)
            dpre_ref[0, p] = dgate
            dpre_ref[1, p] = dval
            dh = dh + jnp.dot(dgate, wu_ref[p], preferred_element_type=F32)
            dh = dh + jnp.dot(dval, wu_ref[N_UP_PAIRS + p], preferred_element_type=F32)
            sums.append(dwg + dwv)

        xv = x_ref[...]
        r = lax.rsqrt(jnp.mean(xv * xv, axis=-1, keepdims=True) + EPS)
        xh = xv * r
        gd = dh * g_ref[...]
        dx = r * (gd - xh * jnp.mean(gd * xh, axis=-1, keepdims=True)) + dres_ref[...]
        dx_ref[...] = dx
        dxb_ref[...] = dx.astype(BF16)
        part = jnp.sum(dh * xh, axis=0, keepdims=True)

        @pl.when(i == 0)
        def _():
            dg_ref[...] = part
            for p in range(N_UP_PAIRS):
                for k in range(6):
                    dw_ref[k // 3, p, pl.ds(k % 3, 1), :] = sums[p][k]

        @pl.when(i > 0)
        def _():
            dg_ref[...] += part
            for p in range(N_UP_PAIRS):
                for k in range(6):
                    dw_ref[k // 3, p, pl.ds(k % 3, 1), :] += sums[p][k]

    def rows4(n):
        return lambda fn: pl.BlockSpec((2, N_UP_PAIRS, n, UP_CHUNK), lambda i: (0, 0, fn(i), 0))

    def rows2(n):
        return lambda fn: pl.BlockSpec((n, d), lambda i: (fn(i), 0))

    prev_tile = lambda i: jnp.maximum(i * tiles_per_halo - 1, 0)
    next_tile = lambda i: jnp.minimum((i + 1) * tiles_per_halo, last_halo)
    row = pl.BlockSpec((FFN_ROWS, d), lambda i: (i, 0))
    vec = pl.BlockSpec((1, d), lambda i: (0, 0))
    wspec = pl.BlockSpec((2, N_UP_PAIRS, 3, UP_CHUNK), lambda i: (0, 0, 0, 0))
    return pl.pallas_call(
        body, name=name, grid=(t // FFN_ROWS,),
        in_specs=[rows4(FFN_ROWS)(lambda i: i), rows4(PACKED_ROWS)(prev_tile), rows4(PACKED_ROWS)(next_tile),
                  rows2(FFN_ROWS)(lambda i: i), rows2(PACKED_ROWS)(prev_tile), rows2(PACKED_ROWS)(next_tile),
                  wspec, pl.BlockSpec((N_UP_PAIRS, UP_CHUNK, d), lambda i: (0, 0, 0)),
                  pl.BlockSpec((N_DEV, UP_CHUNK, d), lambda i: (0, 0, 0)), row, vec, row],
        out_specs=[rows4(FFN_ROWS)(lambda i: i), wspec, row, row, vec],
        out_shape=[jax.ShapeDtypeStruct(pre.shape, BF16), jax.ShapeDtypeStruct(fcw.shape, F32),
                   jax.ShapeDtypeStruct((t, d), F32), jax.ShapeDtypeStruct((t, d), BF16),
                   jax.ShapeDtypeStruct((1, d), F32)],
        compiler_params=_params(("arbitrary",)),
    )(pre, pre, pre, dy, dy, dy, fcw, wdown, wup, x, g, dres)


def _adamw(lands, w, m, v, row_tile, name, after=()):
    nl = len(lands)
    _, nr, ncol = lands[0].shape
    c1 = 1.0 - ADAM_B1 ** ADAM_STEP
    c2 = 1.0 - ADAM_B2 ** ADAM_STEP

    def body(*refs):
        land_refs = refs[:nl]
        w_ref, m_ref, v_ref = refs[nl:nl + 3]
        g_ref, d_ref, mo_ref, vo_ref = refs[nl + 3 + len(after):]
        for l in range(nl):
            @pl.when(pl.program_id(0) == l)
            def _(l=l):
                g = land_refs[l][0].astype(F32)
                for j in range(1, N_DEV):
                    g = g + land_refs[l][j].astype(F32)
                g_ref[...] = g

        g = g_ref[...]
        m2 = ADAM_B1 * m_ref[...] + (1.0 - ADAM_B1) * g
        v2 = ADAM_B2 * v_ref[...] + (1.0 - ADAM_B2) * (g * g)
        mo_ref[...] = m2
        vo_ref[...] = v2
        d_ref[...] = -ADAM_LR * ((m2 / c1) / (jnp.sqrt(v2 / c2) + ADAM_EPS) + ADAM_WD * w_ref[...])

    def land_spec(l):
        return pl.BlockSpec((N_DEV, row_tile, ncol), lambda k, i: (0, jnp.where(k == l, i, 0), 0))

    tile = pl.BlockSpec((None, row_tile, ncol), lambda k, i: (k, i, 0))
    return pl.pallas_call(
        body, name=name, grid=(nl, nr // row_tile),
        in_specs=[land_spec(l) for l in range(nl)] + [tile, tile, tile] + [pl.BlockSpec(memory_space=pl.ANY)] * len(after),
        out_specs=[tile] * 4,
        out_shape=[jax.ShapeDtypeStruct(w.shape, F32)] * 4,
        compiler_params=_params(("arbitrary", "arbitrary")),
    )(*lands, w, m, v, *after)


class _Item:
    def __init__(self, src, chunked, land_cols=False):
        self.src, self.chunked, self.land_cols = src, chunked, land_cols
        if chunked == "cols":
            block = (src.shape[0], src.shape[1] // N_DEV)
        else:
            block = src.shape[1:] if chunked else src.shape
        self.width = block[-1]
        self.land_shape = (block[0], N_DEV * block[1]) if land_cols else (N_DEV,) + block

    def _cols(self, first, count=1):
        return pl.ds(pl.multiple_of(first * self.width, LANES), count * self.width)

    def part(self, src_ref, j):
        if self.chunked == "cols":
            return src_ref.at[:, self._cols(j)]
        return src_ref.at[j] if self.chunked else src_ref

    def slot(self, land_ref, s):
        return land_ref.at[:, self._cols(s)] if self.land_cols else land_ref.at[s]


def _mesh_place():
    x, y, c = lax.axis_index("x"), lax.axis_index("y"), lax.axis_index("c")
    return x, y, c, 4 * x + 2 * y + c


def _flipped(x, y, c, k):
    px = 1 - x if k & 4 else x
    py = 1 - y if k & 2 else y
    pc = 1 - c if k & 1 else c
    return (px, py, pc), 4 * px + 2 * py + pc


PEER_ORDER = (2, 4, 6, 3, 5, 7, 1)


def _exchange(items, name):
    n = len(items)

    def body(*refs):
        srcs, lands = refs[:n], refs[n:2 * n]
        send, recv, local = refs[2 * n:]
        x, y, c, me = _mesh_place()

        def copy(i, k, chunk, slot, dev):
            return pltpu.make_async_remote_copy(
                src_ref=items[i].part(srcs[i], chunk), dst_ref=items[i].slot(lands[i], slot),
                send_sem=send.at[i, k - 1], recv_sem=recv.at[i, k - 1], device_id=dev, device_id_type=MESH)

        own = [pltpu.make_async_copy(items[i].part(srcs[i], me), items[i].slot(lands[i], me), local.at[i])
               for i in range(n)]
        for k in PEER_ORDER:
            dev, idx = _flipped(x, y, c, k)
            for i in range(n):
                copy(i, k, idx, me, dev).start()
        for cp in own:
            cp.start()
        for k in PEER_ORDER:
            dev, idx = _flipped(x, y, c, k)
            for i in range(n):
                copy(i, k, me, idx, dev).wait_recv()
        for k in PEER_ORDER:
            dev, idx = _flipped(x, y, c, k)
            for i in range(n):
                copy(i, k, idx, me, dev).wait_send()
        for cp in own:
            cp.wait()

    hbm = pl.BlockSpec(memory_space=pl.ANY)
    return pl.pallas_call(
        body, name=name,
        in_specs=[hbm] * n, out_specs=[hbm] * n,
        out_shape=[jax.ShapeDtypeStruct(it.land_shape, it.src.dtype) for it in items],
        scratch_shapes=[pltpu.SemaphoreType.DMA((n, N_DEV - 1)), pltpu.SemaphoreType.DMA((n, N_DEV - 1)),
                        pltpu.SemaphoreType.DMA((n,))],
        compiler_params=pltpu.CompilerParams(has_side_effects=True),
    )(*[it.src for it in items])


SAME_CORE = (2, 4, 6)


def _sequencer_gather(items, name, collective_id):
    n = len(items)

    def body(*refs):
        srcs, lands = refs[:n], refs[n:2 * n]
        send, recv, local = refs[2 * n:]
        x, y, c, me = _mesh_place()
        sibling, _ = _flipped(x, y, c, 1)
        barrier = pltpu.get_barrier_semaphore()
        for k in SAME_CORE + (1,):
            pl.semaphore_signal(barrier, inc=1, device_id=_flipped(x, y, c, k)[0], device_id_type=MESH)
        pl.semaphore_wait(barrier, len(SAME_CORE) + 1)

        def copy(i, q, src, slot, dev):
            return pltpu.make_async_remote_copy(
                src_ref=src, dst_ref=items[i].slot(lands[i], slot),
                send_sem=send.at[i, q - 1], recv_sem=recv.at[i, q - 1], device_id=dev, device_id_type=MESH)

        own = [pltpu.make_async_copy(srcs[i], items[i].slot(lands[i], me), local.at[i]) for i in range(n)]
        for cp in own:
            cp.start()
        for k in SAME_CORE + (1,):
            for i in range(n):
                copy(i, k, srcs[i], me, _flipped(x, y, c, k)[0]).start()
        for k in SAME_CORE:
            dev, idx = _flipped(x, y, c, k)
            for i in range(n):
                copy(i, k, srcs[i], idx, dev).wait_recv()
            for i in range(n):
                copy(i, k + 1, items[i].slot(lands[i], idx), idx, sibling).start()
        for k in (1,) + tuple(k + 1 for k in SAME_CORE):
            _, idx = _flipped(x, y, c, k)
            for i in range(n):
                copy(i, k, srcs[i], idx, sibling).wait_recv()
        for k in range(1, N_DEV):
            for i in range(n):
                copy(i, k, srcs[i], me, sibling).wait_send()
        for cp in own:
            cp.wait()

    return pl.kernel(
        body, name=name,
        out_type=[jax.ShapeDtypeStruct(it.land_shape, it.src.dtype) for it in items],
        mesh=plsc.ScalarSubcoreMesh(axis_name="sequencer", num_cores=1),
        scratch_types=[pltpu.SemaphoreType.DMA((n, N_DEV - 1)), pltpu.SemaphoreType.DMA((n, N_DEV - 1)),
                       pltpu.SemaphoreType.DMA((n,))],
        compiler_params=pltpu.CompilerParams(collective_id=collective_id),
    )(*[it.src for it in items])


def _sequencer_exchange(items, name, collective_id):
    n = len(items)

    def body(*refs):
        srcs, lands = refs[:n], refs[n:2 * n]
        send, recv, local = refs[2 * n:]
        x, y, c, me = _mesh_place()
        barrier = pltpu.get_barrier_semaphore()
        for k in PEER_ORDER:
            pl.semaphore_signal(barrier, inc=1, device_id=_flipped(x, y, c, k)[0], device_id_type=MESH)
        pl.semaphore_wait(barrier, N_DEV - 1)

        def copy(i, k, chunk, slot, dev):
            return pltpu.make_async_remote_copy(
                src_ref=items[i].part(srcs[i], chunk), dst_ref=items[i].slot(lands[i], slot),
                send_sem=send.at[i, k - 1], recv_sem=recv.at[i, k - 1], device_id=dev, device_id_type=MESH)

        own = [pltpu.make_async_copy(items[i].part(srcs[i], me), items[i].slot(lands[i], me), local.at[i])
               for i in range(n)]
        for cp in own:
            cp.start()
        for k in PEER_ORDER:
            dev, idx = _flipped(x, y, c, k)
            for i in range(n):
                copy(i, k, idx, me, dev).start()
        for k in PEER_ORDER:
            dev, idx = _flipped(x, y, c, k)
            for i in range(n):
                copy(i, k, me, idx, dev).wait_recv()
        for k in PEER_ORDER:
            dev, idx = _flipped(x, y, c, k)
            for i in range(n):
                copy(i, k, idx, me, dev).wait_send()
        for cp in own:
            cp.wait()

    return pl.kernel(
        body, name=name,
        out_type=[jax.ShapeDtypeStruct(it.land_shape, it.src.dtype) for it in items],
        mesh=plsc.ScalarSubcoreMesh(axis_name="sequencer", num_cores=1),
        scratch_types=[pltpu.SemaphoreType.DMA((n, N_DEV - 1)), pltpu.SemaphoreType.DMA((n, N_DEV - 1)),
                       pltpu.SemaphoreType.DMA((n,))],
        compiler_params=pltpu.CompilerParams(collective_id=collective_id),
    )(*[it.src for it in items])


TM = 1024
TM_ACC = 512
TN_IN = 768


def kernel(x, norm1_g, w_in, mix_conv_w, attn_out_g, conv_out_g, w_out, norm2_g, ffn_up, ffn_conv_w, ffn_down, final_norm_g, loss_target, m_norm1_g, m_w_in, m_mix_conv_w, m_attn_out_g, m_conv_out_g, m_w_out, m_norm2_g, m_ffn_up, m_ffn_conv_w, m_ffn_down, m_final_norm_g, v_norm1_g, v_w_in, v_mix_conv_w, v_attn_out_g, v_conv_out_g, v_w_out, v_norm2_g, v_ffn_up, v_ffn_conv_w, v_ffn_down, v_final_norm_g):
    nbatch, seq, d = x.shape
    t = nbatch * seq
    nt, nta = t // TM, t // TM_ACC
    out_rows = D_MODEL // N_DEV
    down_rows = D_FF // N_DEV
    xf = x.reshape(t, d)
    target = loss_target.reshape(t, d)

    cw_local = jnp.concatenate([ffn_conv_w, mix_conv_w], axis=-1)
    cast = lambda w: _Item(w.astype(BF16), False)
    cast_in = lambda w: _Item(w.astype(BF16), False, land_cols=True)
    cw_all, win0 = _sequencer_gather([_Item(cw_local, False), cast_in(w_in[0])], "gather_a", 0)
    up_t, m_up_t, v_up_t = (jnp.swapaxes(a, 1, 2) for a in (ffn_up, m_ffn_up, v_ffn_up))
    wout0, wup0 = _sequencer_gather([cast(w_out[0]), cast(up_t[0])], "gather_b", 1)
    (wdown0,) = _sequencer_gather([cast(ffn_down[0])], "gather_c", 2)
    win1, wout1 = _sequencer_gather([cast_in(w_in[1]), cast(w_out[1])], "gather_d", 3)
    wup1, wdown1 = _sequencer_gather([cast(up_t[1]), cast(ffn_down[1])], "gather_e", 7)
    win, wup = [win0, win1], [wup0, wup1]
    wout = [w.reshape(D_MODEL, D_MODEL) for w in (wout0, wout1)]
    wdown = [w.reshape(N_UP_PAIRS, UP_CHUNK, D_MODEL) for w in (wdown0, wdown1)]
    fcw = [cw_all[:, k, :, :UP_CHUNK].reshape(2, N_UP_PAIRS, 3, UP_CHUNK) for k in range(DEPTH)]
    mcw = [cw_all[:, k, :, UP_CHUNK:].transpose(1, 0, 2).reshape(3, D_CONV) for k in range(DEPTH)]

    full = lambda i, j, k: (0, 0)

    saved = []
    xin = xf
    h1 = _rms_fwd(xin, norm1_g[0][None], "rms1_fwd_0")
    rows_of = lambda width: pl.BlockSpec((TM_ACC, width), lambda i: (i, 0))
    whole = lambda *shape: pl.BlockSpec(shape, lambda i: (0,) * len(shape))
    for l in range(DEPTH):
        proj = _matmul(
            h1, win[l], grid=(nt, D_IN // TN_IN, 1), dims=NN, name=f"proj_{l}",
            a_spec=pl.BlockSpec((TM, D_MODEL), lambda i, j, k: (i, 0)),
            b_spec=pl.BlockSpec((D_MODEL, TN_IN), lambda i, j, k: (0, j)),
            o_spec=pl.BlockSpec((TM, TN_IN), lambda i, j, k: (i, j)), o_shape=(t, D_IN), o_dtype=F32)
        o, lse, cat = _attn_fwd(proj, attn_out_g[l][None], nbatch, seq)
        cat = _convmix_fwd(proj, cat, mcw[l], conv_out_g[l][None], nbatch, seq)
        xmid, h2 = _matmul_norm(cat, wout[l], xin, norm2_g[l][None], dims=NN, name=f"mix_out_{l}",
                                a_spec=rows_of(D_MODEL), b_spec=whole(D_MODEL, D_MODEL))
        if l + 1 < DEPTH:
            xout, h_next, act, pre = _ffn_fwd(
                h2, wup[l], fcw[l], wdown[l], xmid, norm1_g[l + 1][None], seq, f"ffn_fwd_{l}")
        else:
            h_next = None
            xout, act, pre = _ffn_fwd(h2, wup[l], fcw[l], wdown[l], xmid, None, seq, f"ffn_fwd_{l}")
        saved.append((xin, h1, proj, o, lse, cat, xmid, h2, pre, act))
        xin, h1 = xout, h_next

    loss_part, dx, dxb, dgf = _loss_head(xin, final_norm_g[None], target, "loss_head")

    dg1, dg2, dga, dgc = [None] * DEPTH, [None] * DEPTH, [None] * DEPTH, [None] * DEPTH
    for l in reversed(range(DEPTH)):
        xin, h1, proj, o, lse, cat, xmid, h2, pre, act = saved[l]
        g_down = _matmul(
            act, dxb, grid=(N_UP_PAIRS, 1, 1), dims=TN, name=f"g_down_{l}",
            a_spec=pl.BlockSpec((None, t, UP_CHUNK), lambda i, j, k: (i, 0, 0)),
            b_spec=pl.BlockSpec((t, D_MODEL), full),
            o_spec=pl.BlockSpec((None, UP_CHUNK, D_MODEL), lambda i, j, k: (i, 0, 0)),
            o_shape=(N_UP_PAIRS, UP_CHUNK, D_MODEL), o_dtype=BF16).reshape(N_DEV, down_rows, D_MODEL)
        d_pre, d_fcw, dxm, dxmb, dg2[l] = _ffn_up_bwd(
            pre, dxb, fcw[l], wdown[l], wup[l], xmid, norm2_g[l][None], dx, seq, f"ffn_bwd_{l}")
        d_pre = d_pre.reshape(N_DEV, t, UP_CHUNK)
        g_up = _matmul(
            d_pre, h2, grid=(N_DEV, 1, 1), dims=TN, name=f"g_up_{l}",
            a_spec=pl.BlockSpec((None, t, UP_CHUNK), lambda i, j, k: (i, 0, 0)),
            b_spec=pl.BlockSpec((t, D_MODEL), full),
            o_spec=pl.BlockSpec((None, UP_CHUNK, D_MODEL), lambda i, j, k: (i, 0, 0)),
            o_shape=(N_DEV, UP_CHUNK, D_MODEL), o_dtype=BF16)
        g_out = _matmul(
            cat, dxmb, grid=(1, 1, nt), dims=TN, name=f"g_out_{l}",
            a_spec=pl.BlockSpec((TM, D_MODEL), lambda i, j, k: (k, 0)),
            b_spec=pl.BlockSpec((TM, D_MODEL), lambda i, j, k: (k, 0)),
            o_spec=pl.BlockSpec((D_MODEL, D_MODEL), full),
            o_shape=(D_MODEL, D_MODEL), o_dtype=BF16).reshape(N_DEV, out_rows, D_MODEL)
        if l == 0:
            land_out0, land_up0, land_down0 = _sequencer_exchange(
                [_Item(g_out, True), _Item(g_up, True), _Item(g_down, True)], "scatter_0a", 5)
        d_cat = _matmul(
            dxmb, wout[l], grid=(nta, 1, 1), dims=NT, name=f"d_cat_{l}",
            a_spec=pl.BlockSpec((TM_ACC, D_MODEL), lambda i, j, k: (i, 0)),
            b_spec=pl.BlockSpec((D_MODEL, D_MODEL), full),
            o_spec=pl.BlockSpec((TM_ACC, D_MODEL), lambda i, j, k: (i, 0)), o_shape=(t, D_MODEL), o_dtype=BF16)
        d_proj, dga[l] = _attn_bwd(proj, o, lse, d_cat, attn_out_g[l][None], nbatch, seq)
        d_proj, d_mcw, dgc[l] = _convmix_bwd(proj, d_cat, d_proj, mcw[l], conv_out_g[l][None], nbatch, seq)
        g_in = _matmul(
            h1, d_proj, grid=(1, D_IN // TN_IN, 1), dims=TN, name=f"g_in_{l}",
            a_spec=pl.BlockSpec((t, D_MODEL), full),
            b_spec=pl.BlockSpec((t, TN_IN), lambda i, j, k: (0, j)),
            o_spec=pl.BlockSpec((D_MODEL, TN_IN), lambda i, j, k: (0, j)),
            o_shape=(D_MODEL, D_IN), o_dtype=BF16)
        g_cw = jnp.concatenate(
            [d_fcw.reshape(N_DEV, 3, UP_CHUNK), d_mcw.reshape(3, N_DEV, D_CONV // N_DEV).transpose(1, 0, 2)], axis=-1)
        if l == 0:
            land_in0, land_cw0 = _sequencer_exchange([_Item(g_in, "cols"), _Item(g_cw, True)], "scatter_0b", 6)
        else:
            land_in1, land_out1, land_up1, land_down1, land_cw1 = _sequencer_exchange(
                [_Item(g_in, "cols"), _Item(g_out, True), _Item(g_up, True), _Item(g_down, True), _Item(g_cw, True)],
                "scatter_1", 4)
        dx, dxb, dg1[l] = _matmul_norm_bwd(
            d_proj, win[l], xin, norm1_g[l][None], dxm, dims=NT, name=f"d_h1_{l}",
            a_spec=rows_of(D_IN), b_spec=whole(D_MODEL, D_IN))

    def pack_small(n1, a, c, n2, f):
        return jnp.concatenate(
            [n1, n2, f[None], jnp.concatenate([a, c], axis=-1), jnp.zeros((1, D_MODEL), F32)], axis=0)[None]

    small = jnp.concatenate(
        [dg1[0], dg1[1], dg2[0], dg2[1], dgf,
         jnp.concatenate([dga[0], dgc[0]], axis=-1), jnp.concatenate([dga[1], dgc[1]], axis=-1),
         jnp.pad(loss_part, ((0, 0), (0, D_MODEL - LANES)))], axis=0)
    (land_small,) = _exchange([_Item(small, False)], "gather_gain_grads")
    res_small = _adamw(
        [land_small], pack_small(norm1_g, attn_out_g, conv_out_g, norm2_g, final_norm_g),
        pack_small(m_norm1_g, m_attn_out_g, m_conv_out_g, m_norm2_g, m_final_norm_g),
        pack_small(v_norm1_g, v_attn_out_g, v_conv_out_g, v_norm2_g, v_final_norm_g), SUBLANES, "adamw_gains")
    res_out = _adamw([land_out0, land_out1], w_out, m_w_out, v_w_out, out_rows, "adamw_w_out", after=[res_small[0]])
    res_up_t = _adamw([land_up0, land_up1], up_t, m_up_t, v_up_t, UP_CHUNK // 4, "adamw_ffn_up", after=[res_out[0]])
    res_up = [jnp.swapaxes(r, 1, 2) for r in res_up_t]
    res_down = _adamw([land_down0, land_down1], ffn_down, m_ffn_down, v_ffn_down, down_rows, "adamw_ffn_down",
                      after=[res_up_t[0]])
    res_in = _adamw([land_in0, land_in1], w_in, m_w_in, v_w_in, 256, "adamw_w_in", after=[res_down[0]])
    res_cw = _adamw(
        [land_cw0, land_cw1], cw_local, jnp.concatenate([m_ffn_conv_w, m_mix_conv_w], axis=-1),
        jnp.concatenate([v_ffn_conv_w, v_mix_conv_w], axis=-1), 3, "adamw_conv_w", after=[res_in[0]])

    loss = res_small[0][0, SUBLANES - 1, 0]

    def unpack(kind):
        s = res_small[kind][0]
        cwr = res_cw[kind]
        return (s[0:2], res_in[kind], cwr[..., UP_CHUNK:], s[5:7, :D_ATTN], s[5:7, D_ATTN:], res_out[kind],
                s[2:4], res_up[kind], cwr[..., :UP_CHUNK], res_down[kind], s[4])

    return (loss, dx.reshape(nbatch, seq, d), *unpack(0), *unpack(1), *unpack(2), *unpack(3))
```

```python
import math

import jax
import jax.numpy as jnp
from jax import lax
from jax.experimental import pallas as pl
from jax.experimental.pallas import tpu as pltpu
from jax.experimental.pallas import tpu_sc as plsc

F32 = jnp.float32
BF16 = jnp.bfloat16

D_MODEL = 1024
D_ATTN = 512
D_CONV = 512
HEAD_DIM = 64
N_HEADS = 8
D_FF = 2816
DEPTH = 2
D_IN = 3 * D_ATTN + 3 * D_CONV
EPS = 1e-6
DILATIONS = (1, 4, 16)
BAND = 128
N_DEV = 8
UP_CHUNK = 2 * D_FF // N_DEV
N_UP_PAIRS = N_DEV // 2
CW_PACK = UP_CHUNK + D_CONV // N_DEV
ADAM_LR = 0.001
ADAM_B1 = 0.9
ADAM_B2 = 0.999
ADAM_EPS = 1e-08
ADAM_WD = 0.01
ADAM_STEP = 10
LANES = 128
SUBLANES = 8
VMEM_LIMIT = 56 * 1024 * 1024

NEG = -1e30
MESH = pl.DeviceIdType.MESH


def _params(sem=None, vmem=VMEM_LIMIT):
    return pltpu.CompilerParams(dimension_semantics=sem, vmem_limit_bytes=vmem)


NN = (((1,), (0,)), ((), ()))
NT = (((1,), (1,)), ((), ()))
TN = (((0,), (0,)), ((), ()))
TN_PIECE = 1024


def _contract(a_ref, b_ref, dims):
    def dot(av, bv):
        return lax.dot_general(av.astype(BF16), bv.astype(BF16), dims, preferred_element_type=F32)

    if len(a_ref.shape) == 2:
        if dims == TN and a_ref.shape[0] > TN_PIECE:
            part = None
            for r0 in range(0, a_ref.shape[0], TN_PIECE):
                piece = dot(a_ref[pl.ds(r0, TN_PIECE), :], b_ref[pl.ds(r0, TN_PIECE), :])
                part = piece if part is None else part + piece
            return part
        return dot(a_ref[...], b_ref[...])
    part = dot(a_ref[0], b_ref[0])
    for c in range(1, a_ref.shape[0]):
        part = part + dot(a_ref[c], b_ref[c])
    return part


def _matmul(a, b, *, grid, a_spec, b_spec, o_spec, o_shape, o_dtype, dims, name, res=None, res_spec=None, after=()):
    nk = grid[2]
    o_block = tuple(s for s in o_spec.block_shape if s is not None)
    na = len(after)

    def body(*refs):
        refs = refs[:2 + (res is not None)] + refs[2 + (res is not None) + na:]
        if res is None:
            a_ref, b_ref, o_ref, *scr = refs
            r_ref = None
        else:
            a_ref, b_ref, r_ref, o_ref, *scr = refs
        part = _contract(a_ref, b_ref, dims)

        def finish(total):
            if r_ref is not None:
                total = total + r_ref[...]
            o_ref[...] = total.astype(o_dtype)

        if nk == 1:
            finish(part)
        else:
            acc = scr[0]
            k = pl.program_id(2)

            @pl.when(k == 0)
            def _():
                acc[...] = part

            @pl.when(k > 0)
            def _():
                acc[...] += part

            @pl.when(k == nk - 1)
            def _():
                finish(acc[...])

    in_specs = [a_spec, b_spec] + ([res_spec] if res is not None else []) + [pl.BlockSpec(memory_space=pl.ANY)] * na
    args = (a, b) + ((res,) if res is not None else ()) + tuple(after)
    return pl.pallas_call(
        body, name=name, grid=grid, in_specs=in_specs, out_specs=o_spec,
        out_shape=jax.ShapeDtypeStruct(o_shape, o_dtype),
        scratch_shapes=[pltpu.VMEM(o_block, F32)] if nk > 1 else [],
        compiler_params=_params(("parallel", "parallel", "arbitrary")),
    )(*args)


ROW_TILE = 512


def _rms_fwd(x, g, name):
    t, d = x.shape

    def body(x_ref, g_ref, h_ref):
        xv = x_ref[...]
        r = lax.rsqrt(jnp.mean(xv * xv, axis=-1, keepdims=True) + EPS)
        h_ref[...] = (xv * r * g_ref[...]).astype(BF16)

    return pl.pallas_call(
        body, name=name, grid=(t // ROW_TILE,),
        in_specs=[pl.BlockSpec((ROW_TILE, d), lambda i: (i, 0)), pl.BlockSpec((1, d), lambda i: (0, 0))],
        out_specs=pl.BlockSpec((ROW_TILE, d), lambda i: (i, 0)),
        out_shape=jax.ShapeDtypeStruct((t, d), BF16),
        compiler_params=_params(("parallel",)),
    )(x, g)


def _matmul_norm(a, b, res, g, *, a_spec, b_spec, dims, name):
    t, d = res.shape

    def body(a_ref, b_ref, r_ref, g_ref, x_ref, h_ref):
        xv = _contract(a_ref, b_ref, dims) + r_ref[...]
        x_ref[...] = xv
        h_ref[...] = (xv * lax.rsqrt(jnp.mean(xv * xv, axis=-1, keepdims=True) + EPS) * g_ref[...]).astype(BF16)

    row = pl.BlockSpec((TM_ACC, d), lambda i: (i, 0))
    return pl.pallas_call(
        body, name=name, grid=(t // TM_ACC,),
        in_specs=[a_spec, b_spec, row, pl.BlockSpec((1, d), lambda i: (0, 0))], out_specs=[row, row],
        out_shape=[jax.ShapeDtypeStruct((t, d), F32), jax.ShapeDtypeStruct((t, d), BF16)],
        compiler_params=_params(("parallel",)),
    )(a, b, res, g)


def _matmul_norm_bwd(a, b, x, g, dres, *, a_spec, b_spec, dims, name):
    t, d = x.shape

    def body(a_ref, b_ref, x_ref, g_ref, dres_ref, dx_ref, dxb_ref, dg_ref):
        dhv = _contract(a_ref, b_ref, dims)
        xv = x_ref[...]
        r = lax.rsqrt(jnp.mean(xv * xv, axis=-1, keepdims=True) + EPS)
        xh = xv * r
        gd = dhv * g_ref[...]
        dx = r * (gd - xh * jnp.mean(gd * xh, axis=-1, keepdims=True)) + dres_ref[...]
        dx_ref[...] = dx
        dxb_ref[...] = dx.astype(BF16)
        part = jnp.sum(dhv * xh, axis=0, keepdims=True)

        @pl.when(pl.program_id(0) == 0)
        def _():
            dg_ref[...] = part

        @pl.when(pl.program_id(0) > 0)
        def _():
            dg_ref[...] += part

    row = pl.BlockSpec((TM_ACC, d), lambda i: (i, 0))
    vec = pl.BlockSpec((1, d), lambda i: (0, 0))
    return pl.pallas_call(
        body, name=name, grid=(t // TM_ACC,),
        in_specs=[a_spec, b_spec, row, vec, row], out_specs=[row, row, vec],
        out_shape=[jax.ShapeDtypeStruct((t, d), F32), jax.ShapeDtypeStruct((t, d), BF16),
                   jax.ShapeDtypeStruct((1, d), F32)],
        compiler_params=_params(("arbitrary",)),
    )(a, b, x, g, dres)


def _loss_head(x, g, target, name):
    t, d = x.shape

    def body(x_ref, g_ref, t_ref, loss_ref, dx_ref, dxb_ref, dg_ref):
        xv = x_ref[...]
        r = lax.rsqrt(jnp.mean(xv * xv, axis=-1, keepdims=True) + EPS)
        xh = xv * r
        gv = g_ref[...]
        err = xh * gv - t_ref[...]
        loss = jnp.full((1, LANES), 0.5 / d, F32) * jnp.sum(err * err)
        dy = err * (1.0 / d)
        gd = dy * gv
        dx = r * (gd - xh * jnp.mean(gd * xh, axis=-1, keepdims=True))
        dx_ref[...] = dx
        dxb_ref[...] = dx.astype(BF16)
        part = jnp.sum(dy * xh, axis=0, keepdims=True)

        @pl.when(pl.program_id(0) == 0)
        def _():
            dg_ref[...] = part
            loss_ref[...] = loss

        @pl.when(pl.program_id(0) > 0)
        def _():
            dg_ref[...] += part
            loss_ref[...] += loss

    row = pl.BlockSpec((ROW_TILE, d), lambda i: (i, 0))
    vec = pl.BlockSpec((1, d), lambda i: (0, 0))
    return pl.pallas_call(
        body, name=name, grid=(t // ROW_TILE,),
        in_specs=[row, vec, row],
        out_specs=[pl.BlockSpec((1, LANES), lambda i: (0, 0)), row, row, vec],
        out_shape=[jax.ShapeDtypeStruct((1, LANES), F32), jax.ShapeDtypeStruct((t, d), F32),
                   jax.ShapeDtypeStruct((t, d), BF16), jax.ShapeDtypeStruct((1, d), F32)],
        compiler_params=_params(("arbitrary",)),
    )(x, g, target)


def _group_matrix(n):
    shift = int(math.log2(HEAD_DIM))
    r = lax.broadcasted_iota(jnp.int32, (n, n), 0) >> shift
    c = lax.broadcasted_iota(jnp.int32, (n, n), 1) >> shift
    return (r == c).astype(BF16)


def _group_sum(v, gmat):
    hi = v.astype(BF16)
    lo = (v - hi.astype(F32)).astype(BF16)

    def dot(p):
        return jnp.dot(p, gmat, preferred_element_type=F32)

    return dot(hi) + dot(lo)


def _shift_rows(ext, k):
    return pltpu.roll(ext, k % ext.shape[0], 0)


def _store_columns(stage, out_hbm, sems, row0, nrows, col_blocks):
    rows = pl.ds(pl.multiple_of(row0, SUBLANES * 2), nrows)
    copies = [
        pltpu.make_async_copy(stage.at[i], out_hbm.at[rows, pl.ds(pl.multiple_of(cb * LANES, LANES), LANES)], sems.at[i])
        for i, cb in enumerate(col_blocks)
    ]
    for cp in copies:
        cp.start()
    for cp in copies:
        cp.wait()


def _attn_consts(width):
    i = lax.broadcasted_iota(jnp.int32, (BAND, width), 0)
    j = lax.broadcasted_iota(jnp.int32, (BAND, width), 1)
    dist = (width - BAND) + i - j
    inwin = (dist >= 0) & (dist <= BAND)
    return dist.astype(F32), inwin, j


def _head_masks():
    lane = lax.broadcasted_iota(jnp.int32, (1, LANES), 1)
    return [(lane < HEAD_DIM).astype(F32), (lane >= HEAD_DIM).astype(F32)]


def _pair_bias(slope, dil):
    distf, inwin, _ = _attn_consts(2 * BAND)
    return jnp.concatenate([jnp.where(inwin, distf * (slope[hh] * (-float(dil))), NEG) for hh in range(2)], axis=0)


def _stack_heads(xv, hmask):
    return jnp.concatenate([xv * hmask[0], xv * hmask[1]], axis=0).astype(BF16)


FWD_UNROLL = 16
BWD_UNROLL = 16


def _unroll(trips, most):
    return max(u for u in range(1, most + 1) if trips % u == 0)


def _for_blocks(seq, dil, block, most):
    nb = seq // dil // BAND

    def residue(r, carry):
        base = r * nb
        block(pl.multiple_of(base * BAND, BAND), None)
        if nb > 1:
            def rest(n, c):
                block(pl.multiple_of((base + n) * BAND, BAND), pl.multiple_of((base + n - 1) * BAND, BAND))
                return c

            lax.fori_loop(1, nb, rest, 0, unroll=_unroll(nb - 1, most))
        return carry

    if dil == 1:
        residue(0, 0)
    else:
        lax.fori_loop(0, dil, residue, 0, unroll=_unroll(dil, max(1, most // nb)))


def _permute_in(src_ref, dst_ref, dil, seq):
    length = seq // dil
    for r in range(dil):
        dst_ref[pl.ds(r * length, length), :] = src_ref[pl.ds(r, length, stride=dil), :].astype(dst_ref.dtype)


def _slopes_table():
    slopes = 2.0 ** (-8.0 * jnp.arange(1, N_HEADS + 1, dtype=F32) / N_HEADS)
    return jnp.broadcast_to(slopes[:, None], (N_HEADS, 2 * BAND))


def _attn_fwd(proj, attn_g, nbatch, seq):
    t = nbatch * seq
    scale = HEAD_DIM ** -0.5

    def body(q_ref, k_ref, v_ref, g_ref, sl_ref, o_ref, lse_ref, cat_ref, pq, pk, pv, po, pm, pll, ao, am, al):
        hp = pl.program_id(1)
        hmask = _head_masks()
        slope = [sl_ref[pl.ds(2 * hp + hh, 1), :] for hh in range(2)]

        def run_branch(dil, qs, ks, vs, osink, msink, lsink):
            bias = _pair_bias(slope, dil)

            def block(row0, prow):
                cur = pl.ds(row0, BAND)
                q2 = _stack_heads(qs[cur, :] * scale, hmask)
                if prow is None:
                    kk, vv, bias_b = ks[cur, :], vs[cur, :], bias[:, BAND:]
                else:
                    prev = pl.ds(prow, BAND)
                    kk = jnp.concatenate([ks[prev, :], ks[cur, :]], axis=0)
                    vv = jnp.concatenate([vs[prev, :], vs[cur, :]], axis=0)
                    bias_b = bias
                s = lax.dot_general(q2, kk.astype(BF16), NT, preferred_element_type=F32) + bias_b
                m = jnp.max(s, axis=1, keepdims=True)
                p = jnp.exp(s - m)
                l = jnp.sum(p, axis=1, keepdims=True)
                pb = p.astype(BF16)
                o = jnp.dot(jnp.concatenate([pb[:BAND], pb[BAND:]], axis=1), _stack_heads(vv, hmask),
                            preferred_element_type=F32)
                osink[cur, :] = o
                msink[cur, :] = m[:BAND] * hmask[0] + m[BAND:] * hmask[1]
                lsink[cur, :] = l[:BAND] * hmask[0] + l[BAND:] * hmask[1]

            _for_blocks(seq, dil, block, FWD_UNROLL)

        run_branch(1, q_ref, k_ref, v_ref, ao, am, al)
        for dil in DILATIONS[1:]:
            length = seq // dil
            _permute_in(q_ref, pq, dil, seq)
            _permute_in(k_ref, pk, dil, seq)
            _permute_in(v_ref, pv, dil, seq)
            run_branch(dil, pq, pk, pv, po, pm, pll)
            for r in range(dil):
                nat = pl.ds(r, length, stride=dil)
                per = pl.ds(r * length, length)
                m0 = am[nat, :]
                mb = pm[per, :]
                mn = jnp.maximum(m0, mb)
                e0 = jnp.exp(m0 - mn)
                eb = jnp.exp(mb - mn)
                ao[nat, :] = ao[nat, :] * e0 + po[per, :] * eb
                al[nat, :] = al[nat, :] * e0 + pll[per, :] * eb
                am[nat, :] = mn

        gmat = _group_matrix(LANES)
        gv = g_ref[...]

        def fin(c, carry):
            rows = pl.ds(pl.multiple_of(c * 256, 256), 256)
            lv = al[rows, :]
            o = ao[rows, :] / lv
            o_ref[rows, :] = o
            lse_ref[rows, :] = am[rows, :] + jnp.log(lv)
            ms = _group_sum(o * o, gmat) * (1.0 / HEAD_DIM)
            cat_ref[rows, :] = (o * lax.rsqrt(ms + EPS) * gv).astype(BF16)
            return carry

        lax.fori_loop(0, seq // 256, fin, 0)

    nq = D_ATTN // LANES
    blk = lambda off: pl.BlockSpec((seq, LANES), lambda b, h: (b, h + off))
    scratch = [pltpu.VMEM((seq, LANES), F32) for _ in range(9)]
    return pl.pallas_call(
        body, name="attn_fwd", grid=(nbatch, nq),
        in_specs=[blk(0), blk(nq), blk(2 * nq), pl.BlockSpec((1, LANES), lambda b, h: (0, h)),
                  pl.BlockSpec((N_HEADS, 2 * BAND), lambda b, h: (0, 0))],
        out_specs=[blk(0), blk(0), blk(0)],
        out_shape=[jax.ShapeDtypeStruct((t, D_ATTN), F32), jax.ShapeDtypeStruct((t, D_ATTN), F32),
                   jax.ShapeDtypeStruct((t, D_MODEL), BF16)],
        scratch_shapes=scratch,
        compiler_params=_params(("parallel", "parallel")),
    )(proj, proj, proj, attn_g, _slopes_table())


def _attn_bwd(proj, o, lse, d_cat, attn_g, nbatch, seq):
    t = nbatch * seq
    scale = HEAD_DIM ** -0.5

    def body(q_ref, k_ref, v_ref, o_ref, lse_ref, dy_ref, g_ref, sl_ref, dproj_ref, dg_ref,
             do_n, dl_n, dq_n, dk_n, dv_n, pq, pk, pv, pdo, plse, pdl, pdq, pdk, pdv, stage, sems):
        hp = pl.program_id(0)
        hmask = _head_masks()
        slope = [sl_ref[pl.ds(2 * hp + hh, 1), :] for hh in range(2)]
        gmat = _group_matrix(LANES)
        gv = g_ref[...]

        def prep(c, dg):
            rows = pl.ds(pl.multiple_of(c * 256, 256), 256)
            ov = o_ref[rows, :]
            dyn = dy_ref[rows, :].astype(F32)
            r = lax.rsqrt(_group_sum(ov * ov, gmat) * (1.0 / HEAD_DIM) + EPS)
            gd = dyn * gv
            oh = ov * r
            do = r * (gd - oh * (_group_sum(gd * oh, gmat) * (1.0 / HEAD_DIM)))
            do_n[rows, :] = do
            dl_n[rows, :] = _group_sum(do * ov, gmat)
            return dg + jnp.sum(dyn * oh, axis=0, keepdims=True)

        dg = lax.fori_loop(0, seq // 256, prep, jnp.zeros((1, LANES), F32))

        @pl.when(pl.program_id(1) == 0)
        def _():
            dg_ref[...] = dg

        @pl.when(pl.program_id(1) > 0)
        def _():
            dg_ref[...] += dg

        def clear(*refs):
            def step(c, carry):
                rows = pl.ds(pl.multiple_of(c * 256, 256), 256)
                for ref in refs:
                    ref[rows, :] = jnp.zeros((256, LANES), F32)
                return carry

            lax.fori_loop(0, seq // 256, step, 0)

        clear(dq_n, dk_n, dv_n)

        def run_branch(dil, qs, ks, vs, dos, lses, dls, dqs, dks, dvs):
            bias = _pair_bias(slope, dil)

            def per_head(xv):
                return jnp.concatenate([xv[:, 0:1], xv[:, HEAD_DIM:HEAD_DIM + 1]], axis=0)

            def block(row0, prow):
                cur = pl.ds(row0, BAND)
                keys = cur if prow is None else pl.ds(prow, 2 * BAND)
                q2 = _stack_heads(qs[cur, :] * scale, hmask)
                do2 = _stack_heads(dos[cur, :], hmask)
                kk, vv = ks[keys, :], vs[keys, :]
                s = lax.dot_general(q2, kk.astype(BF16), NT, preferred_element_type=F32)
                s = s + (bias[:, BAND:] if prow is None else bias)
                p = jnp.exp(s - per_head(lses[cur, :]))
                dp = lax.dot_general(do2, vv.astype(BF16), NT, preferred_element_type=F32)
                ds = (p * (dp - per_head(dls[cur, :]))).astype(BF16)
                dqs[cur, :] += jnp.dot(jnp.concatenate([ds[:BAND], ds[BAND:]], axis=1), _stack_heads(kk, hmask),
                                       preferred_element_type=F32)
                dks[keys, :] += lax.dot_general(ds, q2, TN, preferred_element_type=F32)
                dvs[keys, :] += lax.dot_general(p.astype(BF16), do2, TN, preferred_element_type=F32)

            _for_blocks(seq, dil, block, BWD_UNROLL)

        run_branch(1, q_ref, k_ref, v_ref, do_n, lse_ref, dl_n, dq_n, dk_n, dv_n)
        for dil in DILATIONS[1:]:
            length = seq // dil
            for src, dst in ((q_ref, pq), (k_ref, pk), (v_ref, pv), (do_n, pdo), (lse_ref, plse), (dl_n, pdl)):
                _permute_in(src, dst, dil, seq)
            clear(pdq, pdk, pdv)
            run_branch(dil, pq, pk, pv, pdo, plse, pdl, pdq, pdk, pdv)
            for r in range(dil):
                nat = pl.ds(r, length, stride=dil)
                per = pl.ds(r * length, length)
                dq_n[nat, :] += pdq[per, :]
                dk_n[nat, :] += pdk[per, :]
                dv_n[nat, :] += pdv[per, :]

        def emit(c, carry):
            rows = pl.ds(pl.multiple_of(c * 256, 256), 256)
            stage[0, rows, :] = (dq_n[rows, :] * scale).astype(BF16)
            stage[1, rows, :] = dk_n[rows, :].astype(BF16)
            stage[2, rows, :] = dv_n[rows, :].astype(BF16)
            return carry

        lax.fori_loop(0, seq // 256, emit, 0)
        _store_columns(stage, dproj_ref, sems, pl.program_id(1) * seq, seq, [hp, nq + hp, 2 * nq + hp])

    nq = D_ATTN // LANES
    blk = lambda off: pl.BlockSpec((seq, LANES), lambda h, b: (b, h + off))
    vec = pl.BlockSpec((1, LANES), lambda h, b: (0, h))
    scratch = [pltpu.VMEM((seq, LANES), F32) for _ in range(14)]
    scratch += [pltpu.VMEM((3, seq, LANES), BF16), pltpu.SemaphoreType.DMA((3,))]
    d_proj, dg = pl.pallas_call(
        body, name="attn_bwd", grid=(nq, nbatch),
        in_specs=[blk(0), blk(nq), blk(2 * nq), blk(0), blk(0), blk(0), vec,
                  pl.BlockSpec((N_HEADS, 2 * BAND), lambda h, b: (0, 0))],
        out_specs=[pl.BlockSpec(memory_space=pl.ANY), vec],
        out_shape=[jax.ShapeDtypeStruct((t, D_IN), BF16), jax.ShapeDtypeStruct((1, D_ATTN), F32)],
        scratch_shapes=scratch,
        compiler_params=_params(("arbitrary", "arbitrary")),
    )(proj, proj, proj, o, lse, d_cat, attn_g, _slopes_table())
    return d_proj, dg


HALO = SUBLANES
PACKED_ROWS = 2 * SUBLANES


def _window(ref, c, rows, nchunks, after):
    row0 = pl.multiple_of(c * rows, rows)
    prev0 = pl.multiple_of(jnp.maximum(row0 - PACKED_ROWS, 0), PACKED_ROWS)
    before = ref[pl.ds(prev0, PACKED_ROWS), :].astype(F32)[PACKED_ROWS - HALO:] * (c > 0).astype(F32)
    parts = [before, ref[pl.ds(row0, rows), :].astype(F32)]
    if after:
        next0 = pl.multiple_of(jnp.minimum(row0 + rows, (nchunks - 1) * rows), PACKED_ROWS)
        parts.append(ref[pl.ds(next0, PACKED_ROWS), :].astype(F32)[:HALO] * (c < nchunks - 1).astype(F32))
    return jnp.concatenate(parts, axis=0)


def _behind(z):
    z1 = _shift_rows(z, 1)
    return z1, _shift_rows(z1, 1)


def _ahead(dy):
    d1 = _shift_rows(dy, -1)
    return d1, _shift_rows(d1, -1)


def _conv(z, w):
    z1, z2 = _behind(z)
    return w[0:1] * z2 + w[1:2] * z1 + w[2:3] * z


def _conv_bwd(dy, z, w, cur):
    d1, d2 = _ahead(dy)
    dz = w[2:3] * dy + w[1:2] * d1 + w[0:1] * d2
    return dz, [jnp.sum((d * z)[cur], axis=0, keepdims=True) for d in (d2, d1, dy)]


def _sigmoid(a):
    return 0.5 * jnp.tanh(0.5 * a) + 0.5


MIX_ROWS = 256
GATE_B_BLOCK = 3 * D_ATTN // LANES
GATE_C_BLOCK = GATE_B_BLOCK + D_CONV // LANES
U_BLOCK = GATE_C_BLOCK + D_CONV // LANES


def _convmix_fwd(proj, cat, mcw, conv_g, nbatch, seq):
    nchunks = seq // MIX_ROWS

    def body(gb_ref, gc_ref, u_ref, w_ref, g_ref, cat_in, cat_ref):
        del cat_in
        gmat = _group_matrix(LANES)
        w = w_ref[...]
        gv = g_ref[...]

        def step(c, carry):
            cur = pl.ds(pl.multiple_of(c * MIX_ROWS, MIX_ROWS), MIX_ROWS)
            z = _window(gc_ref, c, MIX_ROWS, nchunks, False) * _window(u_ref, c, MIX_ROWS, nchunks, False)
            y = gb_ref[cur, :] * _conv(z, w)[HALO:]
            ms = _group_sum(y * y, gmat) * (1.0 / HEAD_DIM)
            cat_ref[cur, :] = (y * lax.rsqrt(ms + EPS) * gv).astype(BF16)
            return carry

        lax.fori_loop(0, nchunks, step, 0)

    nc = D_CONV // LANES
    blk = lambda off: pl.BlockSpec((seq, LANES), lambda b, j: (b, j + off))
    return pl.pallas_call(
        body, name="convmix_fwd", grid=(nbatch, nc),
        in_specs=[blk(GATE_B_BLOCK), blk(GATE_C_BLOCK), blk(U_BLOCK),
                  pl.BlockSpec((3, LANES), lambda b, j: (0, j)), pl.BlockSpec((1, LANES), lambda b, j: (0, j)),
                  pl.BlockSpec(memory_space=pl.ANY)],
        out_specs=blk(D_ATTN // LANES),
        out_shape=jax.ShapeDtypeStruct(cat.shape, cat.dtype),
        input_output_aliases={5: 0},
        compiler_params=_params(("parallel", "parallel")),
    )(proj, proj, proj, mcw, conv_g, cat)


def _convmix_bwd(proj, d_cat, d_proj, mcw, conv_g, nbatch, seq):
    nchunks = seq // MIX_ROWS

    def body(gb_ref, gc_ref, u_ref, dy_ref, w_ref, g_ref, dproj_in, dproj_ref, dw_ref, dg_ref, stage, sems):
        del dproj_in
        cb = pl.program_id(0)
        b = pl.program_id(1)
        gmat = _group_matrix(LANES)
        w = w_ref[...]
        gv = g_ref[...]
        cur = slice(HALO, HALO + MIX_ROWS)

        def step(c, carry):
            rows = pl.ds(pl.multiple_of(c * MIX_ROWS, MIX_ROWS), MIX_ROWS)
            gb = _window(gb_ref, c, MIX_ROWS, nchunks, True)
            gc = _window(gc_ref, c, MIX_ROWS, nchunks, True)
            u = _window(u_ref, c, MIX_ROWS, nchunks, True)
            dyn = _window(dy_ref, c, MIX_ROWS, nchunks, True)
            z = gc * u
            conv = _conv(z, w)
            y = gb * conv
            r = lax.rsqrt(_group_sum(y * y, gmat) * (1.0 / HEAD_DIM) + EPS)
            yh = y * r
            gd = dyn * gv
            dy = r * (gd - yh * (_group_sum(gd * yh, gmat) * (1.0 / HEAD_DIM)))
            dz, dws = _conv_bwd(dy * gb, z, w, cur)
            stage[0, rows, :] = (dy * conv)[cur].astype(BF16)
            stage[1, rows, :] = (dz * u)[cur].astype(BF16)
            stage[2, rows, :] = (dz * gc)[cur].astype(BF16)
            dg = jnp.sum((dyn * yh)[cur], axis=0, keepdims=True)
            return tuple(a + d for a, d in zip(carry, dws + [dg]))

        zero = jnp.zeros((1, LANES), F32)
        dw0, dw1, dw2, dg = lax.fori_loop(0, nchunks, step, (zero, zero, zero, zero))

        @pl.when(b == 0)
        def _():
            dw_ref[0:1, :] = dw0
            dw_ref[1:2, :] = dw1
            dw_ref[2:3, :] = dw2
            dg_ref[...] = dg

        @pl.when(b > 0)
        def _():
            dw_ref[0:1, :] += dw0
            dw_ref[1:2, :] += dw1
            dw_ref[2:3, :] += dw2
            dg_ref[...] += dg

        _store_columns(stage, dproj_ref, sems, b * seq, seq, [GATE_B_BLOCK + cb, GATE_C_BLOCK + cb, U_BLOCK + cb])

    nc = D_CONV // LANES
    blk = lambda off: pl.BlockSpec((seq, LANES), lambda j, b: (b, j + off))
    return pl.pallas_call(
        body, name="convmix_bwd", grid=(nc, nbatch),
        in_specs=[blk(GATE_B_BLOCK), blk(GATE_C_BLOCK), blk(U_BLOCK), blk(D_ATTN // LANES),
                  pl.BlockSpec((3, LANES), lambda j, b: (0, j)), pl.BlockSpec((1, LANES), lambda j, b: (0, j)),
                  pl.BlockSpec(memory_space=pl.ANY)],
        out_specs=[pl.BlockSpec(memory_space=pl.ANY), pl.BlockSpec((3, LANES), lambda j, b: (0, j)),
                   pl.BlockSpec((1, LANES), lambda j, b: (0, j))],
        out_shape=[jax.ShapeDtypeStruct(d_proj.shape, d_proj.dtype), jax.ShapeDtypeStruct((3, D_CONV), F32),
                   jax.ShapeDtypeStruct((1, D_CONV), F32)],
        scratch_shapes=[pltpu.VMEM((3, seq, LANES), BF16), pltpu.SemaphoreType.DMA((3,))],
        input_output_aliases={6: 0},
        compiler_params=_params(("arbitrary", "arbitrary")),
    )(proj, proj, proj, d_cat, mcw, conv_g, d_proj)


FFN_ROWS = 256


def _ffn_fwd(h, wup, fcw, wdown, res, g, seq, name):
    t, d = res.shape
    tiles_per_seq = seq // FFN_ROWS

    def body(hm_ref, hp_ref, wu_ref, w_ref, wd_ref, r_ref, *rest):
        if g is None:
            x_ref, act_ref, pre_ref = rest
        else:
            g_ref, x_ref, h_ref, act_ref, pre_ref = rest
        inside = ((pl.program_id(0) % tiles_per_seq) > 0).astype(F32)
        wrow = lax.broadcasted_iota(jnp.int32, (FFN_ROWS + HALO, 1), 0)
        edge = jnp.where(wrow < HALO, inside, 1.0)
        rows = jnp.concatenate([hp_ref[...], hm_ref[...]], axis=0)

        def up(j, part, p):
            full = lax.dot_general(rows, wu_ref[j], NT, preferred_element_type=F32).astype(BF16)
            pre_ref[part, p] = full[PACKED_ROWS:]
            return full.astype(F32)[PACKED_ROWS - HALO:] * edge

        total = r_ref[...]
        for p in range(N_UP_PAIRS):
            a = _conv(up(p, 0, p), w_ref[0, p])[HALO:]
            v = _conv(up(N_UP_PAIRS + p, 1, p), w_ref[1, p])[HALO:]
            act = (a * _sigmoid(a) * v).astype(BF16)
            act_ref[p] = act
            total = total + jnp.dot(act, wd_ref[p], preferred_element_type=F32)
        x_ref[...] = total
        if g is not None:
            h_ref[...] = (total * lax.rsqrt(jnp.mean(total * total, axis=-1, keepdims=True) + EPS) * g_ref[...]).astype(BF16)

    row = pl.BlockSpec((FFN_ROWS, d), lambda i: (i, 0))
    tiles_per_halo = FFN_ROWS // PACKED_ROWS
    in_specs = [
        row, pl.BlockSpec((PACKED_ROWS, d), lambda i: (jnp.maximum(i * tiles_per_halo - 1, 0), 0)),
        pl.BlockSpec((N_DEV, UP_CHUNK, d), lambda i: (0, 0, 0)),
        pl.BlockSpec((2, N_UP_PAIRS, 3, UP_CHUNK), lambda i: (0, 0, 0, 0)),
        pl.BlockSpec((N_UP_PAIRS, UP_CHUNK, d), lambda i: (0, 0, 0)), row]
    out_specs = [row]
    out_shape = [jax.ShapeDtypeStruct((t, d), F32)]
    args = [h, h, wup, fcw, wdown, res]
    if g is not None:
        in_specs.append(pl.BlockSpec((1, d), lambda i: (0, 0)))
        out_specs.append(row)
        out_shape.append(jax.ShapeDtypeStruct((t, d), BF16))
        args.append(g)
    out_specs += [pl.BlockSpec((N_UP_PAIRS, FFN_ROWS, UP_CHUNK), lambda i: (0, i, 0)),
                  pl.BlockSpec((2, N_UP_PAIRS, FFN_ROWS, UP_CHUNK), lambda i: (0, 0, i, 0))]
    out_shape += [jax.ShapeDtypeStruct((N_UP_PAIRS, t, UP_CHUNK), BF16),
                  jax.ShapeDtypeStruct((2, N_UP_PAIRS, t, UP_CHUNK), BF16)]
    return pl.pallas_call(
        body, name=name, grid=(t // FFN_ROWS,), in_specs=in_specs, out_specs=out_specs, out_shape=out_shape,
        compiler_params=_params(("parallel",)),
    )(*args)


def _ffn_up_bwd(pre, dy, fcw, wdown, wup, x, g, dres, seq, name):
    t, d = x.shape
    tiles_per_seq = seq // FFN_ROWS
    tiles_per_halo = FFN_ROWS // PACKED_ROWS
    last_halo = t // PACKED_ROWS - 1

    def body(pm_ref, pp_ref, pn_ref, dm_ref, dp_ref, dn_ref, w_ref, wd_ref, wu_ref, x_ref, g_ref, dres_ref,
             dpre_ref, dw_ref, dx_ref, dxb_ref, dg_ref):
        i = pl.program_id(0)
        has_prev = ((i % tiles_per_seq) > 0).astype(F32)
        has_next = ((i % tiles_per_seq) < tiles_per_seq - 1).astype(F32)
        cur = slice(HALO, HALO + FFN_ROWS)

        def window(before, main, after):
            return jnp.concatenate([before.astype(F32)[PACKED_ROWS - HALO:] * has_prev, main.astype(F32),
                                    after.astype(F32)[:HALO] * has_next], axis=0)

        dy_rows = jnp.concatenate([dp_ref[...], dm_ref[...], dn_ref[...]], axis=0)
        wrow = lax.broadcasted_iota(jnp.int32, (FFN_ROWS + 2 * HALO, 1), 0)
        edge = jnp.where(wrow < HALO, has_prev, jnp.where(wrow >= HALO + FFN_ROWS, has_next, 1.0))

        dh = jnp.zeros((FFN_ROWS, d), F32)
        sums = []
        for p in range(N_UP_PAIRS):
            pg = window(pp_ref[0, p], pm_ref[0, p], pn_ref[0, p])
            pv = window(pp_ref[1, p], pm_ref[1, p], pn_ref[1, p])
            dact = lax.dot_general(dy_rows, wd_ref[p], NT, preferred_element_type=F32)
            dact = dact[PACKED_ROWS - HALO:PACKED_ROWS + FFN_ROWS + HALO] * edge
            a = _conv(pg, w_ref[0, p])
            v = _conv(pv, w_ref[1, p])
            sg = _sigmoid(a)
            asg = a * sg
            dzg, dwg = _conv_bwd(dact * v * (sg + asg - asg * sg), pg, w_ref[0, p], cur)
            dzv, dwv = _conv_bwd(dact * asg, pv, w_ref[1, p], cur)
            dgate = dzg[cur].astype(BF16)
            dval = dzv[cur].astype(BF16)
            dpre_ref[0, p] = dgate
            dpre_ref[1, p] = dval
            dh = dh + jnp.dot(dgate, wu_ref[p], preferred_element_type=F32)
            dh = dh + jnp.dot(dval, wu_ref[N_UP_PAIRS + p], preferred_element_type=F32)
            sums.append(dwg + dwv)

        xv = x_ref[...]
        r = lax.rsqrt(jnp.mean(xv * xv, axis=-1, keepdims=True) + EPS)
        xh = xv * r
        gd = dh * g_ref[...]
        dx = r * (gd - xh * jnp.mean(gd * xh, axis=-1, keepdims=True)) + dres_ref[...]
        dx_ref[...] = dx
        dxb_ref[...] = dx.astype(BF16)
        part = jnp.sum(dh * xh, axis=0, keepdims=True)

        @pl.when(i == 0)
        def _():
            dg_ref[...] = part
            for p in range(N_UP_PAIRS):
                for k in range(6):
                    dw_ref[k // 3, p, pl.ds(k % 3, 1), :] = sums[p][k]

        @pl.when(i > 0)
        def _():
            dg_ref[...] += part
            for p in range(N_UP_PAIRS):
                for k in range(6):
                    dw_ref[k // 3, p, pl.ds(k % 3, 1), :] += sums[p][k]

    def rows4(n):
        return lambda fn: pl.BlockSpec((2, N_UP_PAIRS, n, UP_CHUNK), lambda i: (0, 0, fn(i), 0))

    def rows2(n):
        return lambda fn: pl.BlockSpec((n, d), lambda i: (fn(i), 0))

    prev_tile = lambda i: jnp.maximum(i * tiles_per_halo - 1, 0)
    next_tile = lambda i: jnp.minimum((i + 1) * tiles_per_halo, last_halo)
    row = pl.BlockSpec((FFN_ROWS, d), lambda i: (i, 0))
    vec = pl.BlockSpec((1, d), lambda i: (0, 0))
    wspec = pl.BlockSpec((2, N_UP_PAIRS, 3, UP_CHUNK), lambda i: (0, 0, 0, 0))
    return pl.pallas_call(
        body, name=name, grid=(t // FFN_ROWS,),
        in_specs=[rows4(FFN_ROWS)(lambda i: i), rows4(PACKED_ROWS)(prev_tile), rows4(PACKED_ROWS)(next_tile),
                  rows2(FFN_ROWS)(lambda i: i), rows2(PACKED_ROWS)(prev_tile), rows2(PACKED_ROWS)(next_tile),
                  wspec, pl.BlockSpec((N_UP_PAIRS, UP_CHUNK, d), lambda i: (0, 0, 0)),
                  pl.BlockSpec((N_DEV, UP_CHUNK, d), lambda i: (0, 0, 0)), row, vec, row],
        out_specs=[rows4(FFN_ROWS)(lambda i: i), wspec, row, row, vec],
        out_shape=[jax.ShapeDtypeStruct(pre.shape, BF16), jax.ShapeDtypeStruct(fcw.shape, F32),
                   jax.ShapeDtypeStruct((t, d), F32), jax.ShapeDtypeStruct((t, d), BF16),
                   jax.ShapeDtypeStruct((1, d), F32)],
        compiler_params=_params(("arbitrary",)),
    )(pre, pre, pre, dy, dy, dy, fcw, wdown, wup, x, g, dres)


def _adamw(lands, w, m, v, row_tile, name, after=()):
    nl = len(lands)
    _, nr, ncol = lands[0].shape
    c1 = 1.0 - ADAM_B1 ** ADAM_STEP
    c2 = 1.0 - ADAM_B2 ** ADAM_STEP

    def body(*refs):
        land_refs = refs[:nl]
        w_ref, m_ref, v_ref = refs[nl:nl + 3]
        g_ref, d_ref, mo_ref, vo_ref = refs[nl + 3 + len(after):]
        for l in range(nl):
            @pl.when(pl.program_id(0) == l)
            def _(l=l):
                g = land_refs[l][0].astype(F32)
                for j in range(1, N_DEV):
                    g = g + land_refs[l][j].astype(F32)
                g_ref[...] = g

        g = g_ref[...]
        m2 = ADAM_B1 * m_ref[...] + (1.0 - ADAM_B1) * g
        v2 = ADAM_B2 * v_ref[...] + (1.0 - ADAM_B2) * (g * g)
        mo_ref[...] = m2
        vo_ref[...] = v2
        d_ref[...] = -ADAM_LR * ((m2 / c1) / (jnp.sqrt(v2 / c2) + ADAM_EPS) + ADAM_WD * w_ref[...])

    def land_spec(l):
        return pl.BlockSpec((N_DEV, row_tile, ncol), lambda k, i: (0, jnp.where(k == l, i, 0), 0))

    tile = pl.BlockSpec((None, row_tile, ncol), lambda k, i: (k, i, 0))
    return pl.pallas_call(
        body, name=name, grid=(nl, nr // row_tile),
        in_specs=[land_spec(l) for l in range(nl)] + [tile, tile, tile] + [pl.BlockSpec(memory_space=pl.ANY)] * len(after),
        out_specs=[tile] * 4,
        out_shape=[jax.ShapeDtypeStruct(w.shape, F32)] * 4,
        compiler_params=_params(("arbitrary", "arbitrary")),
    )(*lands, w, m, v, *after)


class _Item:
    def __init__(self, src, chunked, land_cols=False):
        self.src, self.chunked, self.land_cols = src, chunked, land_cols
        if chunked == "cols":
            block = (src.shape[0], src.shape[1] // N_DEV)
        else:
            block = src.shape[1:] if chunked else src.shape
        self.width = block[-1]
        self.land_shape = (block[0], N_DEV * block[1]) if land_cols else (N_DEV,) + block

    def _cols(self, first, count=1):
        return pl.ds(pl.multiple_of(first * self.width, LANES), count * self.width)

    def part(self, src_ref, j):
        if self.chunked == "cols":
            return src_ref.at[:, self._cols(j)]
        return src_ref.at[j] if self.chunked else src_ref

    def slot(self, land_ref, s):
        return land_ref.at[:, self._cols(s)] if self.land_cols else land_ref.at[s]


def _mesh_place():
    x, y, c = lax.axis_index("x"), lax.axis_index("y"), lax.axis_index("c")
    return x, y, c, 4 * x + 2 * y + c


def _flipped(x, y, c, k):
    px = 1 - x if k & 4 else x
    py = 1 - y if k & 2 else y
    pc = 1 - c if k & 1 else c
    return (px, py, pc), 4 * px + 2 * py + pc


PEER_ORDER = (2, 4, 6, 3, 5, 7, 1)


def _exchange(items, name):
    n = len(items)

    def body(*refs):
        srcs, lands = refs[:n], refs[n:2 * n]
        send, recv, local = refs[2 * n:]
        x, y, c, me = _mesh_place()

        def copy(i, k, chunk, slot, dev):
            return pltpu.make_async_remote_copy(
                src_ref=items[i].part(srcs[i], chunk), dst_ref=items[i].slot(lands[i], slot),
                send_sem=send.at[i, k - 1], recv_sem=recv.at[i, k - 1], device_id=dev, device_id_type=MESH)

        own = [pltpu.make_async_copy(items[i].part(srcs[i], me), items[i].slot(lands[i], me), local.at[i])
               for i in range(n)]
        for k in PEER_ORDER:
            dev, idx = _flipped(x, y, c, k)
            for i in range(n):
                copy(i, k, idx, me, dev).start()
        for cp in own:
            cp.start()
        for k in PEER_ORDER:
            dev, idx = _flipped(x, y, c, k)
            for i in range(n):
                copy(i, k, me, idx, dev).wait_recv()
        for k in PEER_ORDER:
            dev, idx = _flipped(x, y, c, k)
            for i in range(n):
                copy(i, k, idx, me, dev).wait_send()
        for cp in own:
            cp.wait()

    hbm = pl.BlockSpec(memory_space=pl.ANY)
    return pl.pallas_call(
        body, name=name,
        in_specs=[hbm] * n, out_specs=[hbm] * n,
        out_shape=[jax.ShapeDtypeStruct(it.land_shape, it.src.dtype) for it in items],
        scratch_shapes=[pltpu.SemaphoreType.DMA((n, N_DEV - 1)), pltpu.SemaphoreType.DMA((n, N_DEV - 1)),
                        pltpu.SemaphoreType.DMA((n,))],
        compiler_params=pltpu.CompilerParams(has_side_effects=True),
    )(*[it.src for it in items])


SAME_CORE = (2, 4, 6)


def _sequencer_gather(items, name, collective_id):
    n = len(items)

    def body(*refs):
        srcs, lands = refs[:n], refs[n:2 * n]
        send, recv, local = refs[2 * n:]
        x, y, c, me = _mesh_place()
        sibling, _ = _flipped(x, y, c, 1)
        barrier = pltpu.get_barrier_semaphore()
        for k in SAME_CORE + (1,):
            pl.semaphore_signal(barrier, inc=1, device_id=_flipped(x, y, c, k)[0], device_id_type=MESH)
        pl.semaphore_wait(barrier, len(SAME_CORE) + 1)

        def copy(i, q, src, slot, dev):
            return pltpu.make_async_remote_copy(
                src_ref=src, dst_ref=items[i].slot(lands[i], slot),
                send_sem=send.at[i, q - 1], recv_sem=recv.at[i, q - 1], device_id=dev, device_id_type=MESH)

        own = [pltpu.make_async_copy(srcs[i], items[i].slot(lands[i], me), local.at[i]) for i in range(n)]
        for cp in own:
            cp.start()
        for k in SAME_CORE + (1,):
            for i in range(n):
                copy(i, k, srcs[i], me, _flipped(x, y, c, k)[0]).start()
        for k in SAME_CORE:
            dev, idx = _flipped(x, y, c, k)
            for i in range(n):
                copy(i, k, srcs[i], idx, dev).wait_recv()
            for i in range(n):
                copy(i, k + 1, items[i].slot(lands[i], idx), idx, sibling).start()
        for k in (1,) + tuple(k + 1 for k in SAME_CORE):
            _, idx = _flipped(x, y, c, k)
            for i in range(n):
                copy(i, k, srcs[i], idx, sibling).wait_recv()
        for k in range(1, N_DEV):
            for i in range(n):
                copy(i, k, srcs[i], me, sibling).wait_send()
        for cp in own:
            cp.wait()

    return pl.kernel(
        body, name=name,
        out_type=[jax.ShapeDtypeStruct(it.land_shape, it.src.dtype) for it in items],
        mesh=plsc.ScalarSubcoreMesh(axis_name="sequencer", num_cores=1),
        scratch_types=[pltpu.SemaphoreType.DMA((n, N_DEV - 1)), pltpu.SemaphoreType.DMA((n, N_DEV - 1)),
                       pltpu.SemaphoreType.DMA((n,))],
        compiler_params=pltpu.CompilerParams(collective_id=collective_id),
    )(*[it.src for it in items])


def _sequencer_exchange(items, name, collective_id):
    n = len(items)

    def body(*refs):
        srcs, lands = refs[:n], refs[n:2 * n]
        send, recv, local = refs[2 * n:]
        x, y, c, me = _mesh_place()
        barrier = pltpu.get_barrier_semaphore()
        for k in PEER_ORDER:
            pl.semaphore_signal(barrier, inc=1, device_id=_flipped(x, y, c, k)[0], device_id_type=MESH)
        pl.semaphore_wait(barrier, N_DEV - 1)

        def copy(i, k, chunk, slot, dev):
            return pltpu.make_async_remote_copy(
                src_ref=items[i].part(srcs[i], chunk), dst_ref=items[i].slot(lands[i], slot),
                send_sem=send.at[i, k - 1], recv_sem=recv.at[i, k - 1], device_id=dev, device_id_type=MESH)

        own = [pltpu.make_async_copy(items[i].part(srcs[i], me), items[i].slot(lands[i], me), local.at[i])
               for i in range(n)]
        for cp in own:
            cp.start()
        for k in PEER_ORDER:
            dev, idx = _flipped(x, y, c, k)
            for i in range(n):
                copy(i, k, idx, me, dev).start()
        for k in PEER_ORDER:
            dev, idx = _flipped(x, y, c, k)
            for i in range(n):
                copy(i, k, me, idx, dev).wait_recv()
        for k in PEER_ORDER:
            dev, idx = _flipped(x, y, c, k)
            for i in range(n):
                copy(i, k, idx, me, dev).wait_send()
        for cp in own:
            cp.wait()

    return pl.kernel(
        body, name=name,
        out_type=[jax.ShapeDtypeStruct(it.land_shape, it.src.dtype) for it in items],
        mesh=plsc.ScalarSubcoreMesh(axis_name="sequencer", num_cores=1),
        scratch_types=[pltpu.SemaphoreType.DMA((n, N_DEV - 1)), pltpu.SemaphoreType.DMA((n, N_DEV - 1)),
                       pltpu.SemaphoreType.DMA((n,))],
        compiler_params=pltpu.CompilerParams(collective_id=collective_id),
    )(*[it.src for it in items])


TM = 1024
TM_ACC = 512
TN_IN = 768


def kernel(x, norm1_g, w_in, mix_conv_w, attn_out_g, conv_out_g, w_out, norm2_g, ffn_up, ffn_conv_w, ffn_down, final_norm_g, loss_target, m_norm1_g, m_w_in, m_mix_conv_w, m_attn_out_g, m_conv_out_g, m_w_out, m_norm2_g, m_ffn_up, m_ffn_conv_w, m_ffn_down, m_final_norm_g, v_norm1_g, v_w_in, v_mix_conv_w, v_attn_out_g, v_conv_out_g, v_w_out, v_norm2_g, v_ffn_up, v_ffn_conv_w, v_ffn_down, v_final_norm_g):
    nbatch, seq, d = x.shape
    t = nbatch * seq
    nt, nta = t // TM, t // TM_ACC
    out_rows = D_MODEL // N_DEV
    down_rows = D_FF // N_DEV
    xf = x.reshape(t, d)
    target = loss_target.reshape(t, d)

    cw_local = jnp.concatenate([ffn_conv_w, mix_conv_w], axis=-1)
    cast = lambda w: _Item(w.astype(BF16), False)
    cast_in = lambda w: _Item(w.astype(BF16), False, land_cols=True)
    cw_all, win0 = _sequencer_gather([_Item(cw_local, False), cast_in(w_in[0])], "gather_a", 0)
    up_t, m_up_t, v_up_t = (jnp.swapaxes(a, 1, 2) for a in (ffn_up, m_ffn_up, v_ffn_up))
    wout0, wup0 = _sequencer_gather([cast(w_out[0]), cast(up_t[0])], "gather_b", 1)
    (wdown0,) = _sequencer_gather([cast(ffn_down[0])], "gather_c", 2)
    win1, wout1 = _sequencer_gather([cast_in(w_in[1]), cast(w_out[1])], "gather_d", 3)
    wup1, wdown1 = _sequencer_gather([cast(up_t[1]), cast(ffn_down[1])], "gather_e", 7)
    win, wup = [win0, win1], [wup0, wup1]
    wout = [w.reshape(D_MODEL, D_MODEL) for w in (wout0, wout1)]
    wdown = [w.reshape(N_UP_PAIRS, UP_CHUNK, D_MODEL) for w in (wdown0, wdown1)]
    fcw = [cw_all[:, k, :, :UP_CHUNK].reshape(2, N_UP_PAIRS, 3, UP_CHUNK) for k in range(DEPTH)]
    mcw = [cw_all[:, k, :, UP_CHUNK:].transpose(1, 0, 2).reshape(3, D_CONV) for k in range(DEPTH)]

    full = lambda i, j, k: (0, 0)

    saved = []
    xin = xf
    h1 = _rms_fwd(xin, norm1_g[0][None], "rms1_fwd_0")
    rows_of = lambda width: pl.BlockSpec((TM_ACC, width), lambda i: (i, 0))
    whole = lambda *shape: pl.BlockSpec(shape, lambda i: (0,) * len(shape))
    for l in range(DEPTH):
        proj = _matmul(
            h1, win[l], grid=(nt, D_IN // TN_IN, 1), dims=NN, name=f"proj_{l}",
            a_spec=pl.BlockSpec((TM, D_MODEL), lambda i, j, k: (i, 0)),
            b_spec=pl.BlockSpec((D_MODEL, TN_IN), lambda i, j, k: (0, j)),
            o_spec=pl.BlockSpec((TM, TN_IN), lambda i, j, k: (i, j)), o_shape=(t, D_IN), o_dtype=F32)
        o, lse, cat = _attn_fwd(proj, attn_out_g[l][None], nbatch, seq)
        cat = _convmix_fwd(proj, cat, mcw[l], conv_out_g[l][None], nbatch, seq)
        xmid, h2 = _matmul_norm(cat, wout[l], xin, norm2_g[l][None], dims=NN, name=f"mix_out_{l}",
                                a_spec=rows_of(D_MODEL), b_spec=whole(D_MODEL, D_MODEL))
        if l + 1 < DEPTH:
            xout, h_next, act, pre = _ffn_fwd(
                h2, wup[l], fcw[l], wdown[l], xmid, norm1_g[l + 1][None], seq, f"ffn_fwd_{l}")
        else:
            h_next = None
            xout, act, pre = _ffn_fwd(h2, wup[l], fcw[l], wdown[l], xmid, None, seq, f"ffn_fwd_{l}")
        saved.append((xin, h1, proj, o, lse, cat, xmid, h2, pre, act))
        xin, h1 = xout, h_next

    loss_part, dx, dxb, dgf = _loss_head(xin, final_norm_g[None], target, "loss_head")

    dg1, dg2, dga, dgc = [None] * DEPTH, [None] * DEPTH, [None] * DEPTH, [None] * DEPTH
    for l in reversed(range(DEPTH)):
        xin, h1, proj, o, lse, cat, xmid, h2, pre, act = saved[l]
        g_down = _matmul(
            act, dxb, grid=(N_UP_PAIRS, 1, 1), dims=TN, name=f"g_down_{l}",
            a_spec=pl.BlockSpec((None, t, UP_CHUNK), lambda i, j, k: (i, 0, 0)),
            b_spec=pl.BlockSpec((t, D_MODEL), full),
            o_spec=pl.BlockSpec((None, UP_CHUNK, D_MODEL), lambda i, j, k: (i, 0, 0)),
            o_shape=(N_UP_PAIRS, UP_CHUNK, D_MODEL), o_dtype=BF16).reshape(N_DEV, down_rows, D_MODEL)
        d_pre, d_fcw, dxm, dxmb, dg2[l] = _ffn_up_bwd(
            pre, dxb, fcw[l], wdown[l], wup[l], xmid, norm2_g[l][None], dx, seq, f"ffn_bwd_{l}")
        d_pre = d_pre.reshape(N_DEV, t, UP_CHUNK)
        g_up = _matmul(
            d_pre, h2, grid=(N_DEV, 1, 1), dims=TN, name=f"g_up_{l}",
            a_spec=pl.BlockSpec((None, t, UP_CHUNK), lambda i, j, k: (i, 0, 0)),
            b_spec=pl.BlockSpec((t, D_MODEL), full),
            o_spec=pl.BlockSpec((None, UP_CHUNK, D_MODEL), lambda i, j, k: (i, 0, 0)),
            o_shape=(N_DEV, UP_CHUNK, D_MODEL), o_dtype=BF16)
        g_out = _matmul(
            cat, dxmb, grid=(1, 1, nt), dims=TN, name=f"g_out_{l}",
            a_spec=pl.BlockSpec((TM, D_MODEL), lambda i, j, k: (k, 0)),
            b_spec=pl.BlockSpec((TM, D_MODEL), lambda i, j, k: (k, 0)),
            o_spec=pl.BlockSpec((D_MODEL, D_MODEL), full),
            o_shape=(D_MODEL, D_MODEL), o_dtype=BF16).reshape(N_DEV, out_rows, D_MODEL)
        if l == 0:
            land_out0, land_up0, land_down0 = _sequencer_exchange(
                [_Item(g_out, True), _Item(g_up, True), _Item(g_down, True)], "scatter_0a", 5)
        d_cat = _matmul(
            dxmb, wout[l], grid=(nta, 1, 1), dims=NT, name=f"d_cat_{l}",
            a_spec=pl.BlockSpec((TM_ACC, D_MODEL), lambda i, j, k: (i, 0)),
            b_spec=pl.BlockSpec((D_MODEL, D_MODEL), full),
            o_spec=pl.BlockSpec((TM_ACC, D_MODEL), lambda i, j, k: (i, 0)), o_shape=(t, D_MODEL), o_dtype=BF16)
        d_proj, dga[l] = _attn_bwd(proj, o, lse, d_cat, attn_out_g[l][None], nbatch, seq)
        d_proj, d_mcw, dgc[l] = _convmix_bwd(proj, d_cat, d_proj, mcw[l], conv_out_g[l][None], nbatch, seq)
        g_in = _matmul(
            h1, d_proj, grid=(1, D_IN // TN_IN, 1), dims=TN, name=f"g_in_{l}",
            a_spec=pl.BlockSpec((t, D_MODEL), full),
            b_spec=pl.BlockSpec((t, TN_IN), lambda i, j, k: (0, j)),
            o_spec=pl.BlockSpec((D_MODEL, TN_IN), lambda i, j, k: (0, j)),
            o_shape=(D_MODEL, D_IN), o_dtype=BF16)
        g_cw = jnp.concatenate(
            [d_fcw.reshape(N_DEV, 3, UP_CHUNK), d_mcw.reshape(3, N_DEV, D_CONV // N_DEV).transpose(1, 0, 2)], axis=-1)
        if l == 0:
            land_in0, land_cw0 = _sequencer_exchange([_Item(g_in, "cols"), _Item(g_cw, True)], "scatter_0b", 6)
        else:
            land_in1, land_out1, land_up1, land_down1, land_cw1 = _sequencer_exchange(
                [_Item(g_in, "cols"), _Item(g_out, True), _Item(g_up, True), _Item(g_down, True), _Item(g_cw, True)],
                "scatter_1", 4)
        dx, dxb, dg1[l] = _matmul_norm_bwd(
            d_proj, win[l], xin, norm1_g[l][None], dxm, dims=NT, name=f"d_h1_{l}",
            a_spec=rows_of(D_IN), b_spec=whole(D_MODEL, D_IN))

    def pack_small(n1, a, c, n2, f):
        return jnp.concatenate(
            [n1, n2, f[None], jnp.concatenate([a, c], axis=-1), jnp.zeros((1, D_MODEL), F32)], axis=0)[None]

    small = jnp.concatenate(
        [dg1[0], dg1[1], dg2[0], dg2[1], dgf,
         jnp.concatenate([dga[0], dgc[0]], axis=-1), jnp.concatenate([dga[1], dgc[1]], axis=-1),
         jnp.pad(loss_part, ((0, 0), (0, D_MODEL - LANES)))], axis=0)
    (land_small,) = _exchange([_Item(small, False)], "gather_gain_grads")
    res_small = _adamw(
        [land_small], pack_small(norm1_g, attn_out_g, conv_out_g, norm2_g, final_norm_g),
        pack_small(m_norm1_g, m_attn_out_g, m_conv_out_g, m_norm2_g, m_final_norm_g),
        pack_small(v_norm1_g, v_attn_out_g, v_conv_out_g, v_norm2_g, v_final_norm_g), SUBLANES, "adamw_gains")
    res_out = _adamw([land_out0, land_out1], w_out, m_w_out, v_w_out, out_rows, "adamw_w_out", after=[res_small[0]])
    res_up_t = _adamw([land_up0, land_up1], up_t, m_up_t, v_up_t, UP_CHUNK // 4, "adamw_ffn_up", after=[res_out[0]])
    res_up = [jnp.swapaxes(r, 1, 2) for r in res_up_t]
    res_down = _adamw([land_down0, land_down1], ffn_down, m_ffn_down, v_ffn_down, down_rows, "adamw_ffn_down",
                      after=[res_up_t[0]])
    res_in = _adamw([land_in0, land_in1], w_in, m_w_in, v_w_in, 256, "adamw_w_in", after=[res_down[0]])
    res_cw = _adamw(
        [land_cw0, land_cw1], cw_local, jnp.concatenate([m_ffn_conv_w, m_mix_conv_w], axis=-1),
        jnp.concatenate([v_ffn_conv_w, v_mix_conv_w], axis=-1), 3, "adamw_conv_w", after=[res_in[0]])

    loss = res_small[0][0, SUBLANES - 1, 0]

    def unpack(kind):
        s = res_small[kind][0]
        cwr = res_cw[kind]
        return (s[0:2], res_in[kind], cwr[..., UP_CHUNK:], s[5:7, :D_ATTN], s[5:7, D_ATTN:], res_out[kind],
                s[2:4], res_up[kind], cwr[..., :UP_CHUNK], res_down[kind], s[4])

    return (loss, dx.reshape(nbatch, seq, d), *unpack(0), *unpack(1), *unpack(2), *unpack(3))
```

```python
import math

import jax
import jax.numpy as jnp
from jax import lax
from jax.experimental import pallas as pl
from jax.experimental.pallas import tpu as pltpu
from jax.experimental.pallas import tpu_sc as plsc

F32 = jnp.float32
BF16 = jnp.bfloat16

D_MODEL = 1024
D_ATTN = 512
D_CONV = 512
HEAD_DIM = 64
N_HEADS = 8
D_FF = 2816
DEPTH = 2
D_IN = 3 * D_ATTN + 3 * D_CONV
EPS = 1e-6
DILATIONS = (1, 4, 16)
BAND = 128
N_DEV = 8
UP_CHUNK = 2 * D_FF // N_DEV
N_UP_PAIRS = N_DEV // 2
CW_PACK = UP_CHUNK + D_CONV // N_DEV
ADAM_LR = 0.001
ADAM_B1 = 0.9
ADAM_B2 = 0.999
ADAM_EPS = 1e-08
ADAM_WD = 0.01
ADAM_STEP = 10
LANES = 128
SUBLANES = 8
VMEM_LIMIT = 56 * 1024 * 1024

NEG = -1e30
MESH = pl.DeviceIdType.MESH


def _params(sem=None, vmem=VMEM_LIMIT):
    return pltpu.CompilerParams(dimension_semantics=sem, vmem_limit_bytes=vmem)


NN = (((1,), (0,)), ((), ()))
NT = (((1,), (1,)), ((), ()))
TN = (((0,), (0,)), ((), ()))
TN_PIECE = 1024


def _contract(a_ref, b_ref, dims):
    def dot(av, bv):
        return lax.dot_general(av.astype(BF16), bv.astype(BF16), dims, preferred_element_type=F32)

    if len(a_ref.shape) == 2:
        if dims == TN and a_ref.shape[0] > TN_PIECE:
            part = None
            for r0 in range(0, a_ref.shape[0], TN_PIECE):
                piece = dot(a_ref[pl.ds(r0, TN_PIECE), :], b_ref[pl.ds(r0, TN_PIECE), :])
                part = piece if part is None else part + piece
            return part
        return dot(a_ref[...], b_ref[...])
    part = dot(a_ref[0], b_ref[0])
    for c in range(1, a_ref.shape[0]):
        part = part + dot(a_ref[c], b_ref[c])
    return part


def _matmul(a, b, *, grid, a_spec, b_spec, o_spec, o_shape, o_dtype, dims, name, res=None, res_spec=None, after=()):
    nk = grid[2]
    o_block = tuple(s for s in o_spec.block_shape if s is not None)
    na = len(after)

    def body(*refs):
        refs = refs[:2 + (res is not None)] + refs[2 + (res is not None) + na:]
        if res is None:
            a_ref, b_ref, o_ref, *scr = refs
            r_ref = None
        else:
            a_ref, b_ref, r_ref, o_ref, *scr = refs
        part = _contract(a_ref, b_ref, dims)

        def finish(total):
            if r_ref is not None:
                total = total + r_ref[...]
            o_ref[...] = total.astype(o_dtype)

        if nk == 1:
            finish(part)
        else:
            acc = scr[0]
            k = pl.program_id(2)

            @pl.when(k == 0)
            def _():
                acc[...] = part

            @pl.when(k > 0)
            def _():
                acc[...] += part

            @pl.when(k == nk - 1)
            def _():
                finish(acc[...])

    in_specs = [a_spec, b_spec] + ([res_spec] if res is not None else []) + [pl.BlockSpec(memory_space=pl.ANY)] * na
    args = (a, b) + ((res,) if res is not None else ()) + tuple(after)
    return pl.pallas_call(
        body, name=name, grid=grid, in_specs=in_specs, out_specs=o_spec,
        out_shape=jax.ShapeDtypeStruct(o_shape, o_dtype),
        scratch_shapes=[pltpu.VMEM(o_block, F32)] if nk > 1 else [],
        compiler_params=_params(("parallel", "parallel", "arbitrary")),
    )(*args)


ROW_TILE = 512


def _rms_fwd(x, g, name):
    t, d = x.shape

    def body(x_ref, g_ref, h_ref):
        xv = x_ref[...]
        r = lax.rsqrt(jnp.mean(xv * xv, axis=-1, keepdims=True) + EPS)
        h_ref[...] = (xv * r * g_ref[...]).astype(BF16)

    return pl.pallas_call(
        body, name=name, grid=(t // ROW_TILE,),
        in_specs=[pl.BlockSpec((ROW_TILE, d), lambda i: (i, 0)), pl.BlockSpec((1, d), lambda i: (0, 0))],
        out_specs=pl.BlockSpec((ROW_TILE, d), lambda i: (i, 0)),
        out_shape=jax.ShapeDtypeStruct((t, d), BF16),
        compiler_params=_params(("parallel",)),
    )(x, g)


def _matmul_norm(a, b, res, g, *, a_spec, b_spec, dims, name):
    t, d = res.shape

    def body(a_ref, b_ref, r_ref, g_ref, x_ref, h_ref):
        xv = _contract(a_ref, b_ref, dims) + r_ref[...]
        x_ref[...] = xv
        h_ref[...] = (xv * lax.rsqrt(jnp.mean(xv * xv, axis=-1, keepdims=True) + EPS) * g_ref[...]).astype(BF16)

    row = pl.BlockSpec((TM_ACC, d), lambda i: (i, 0))
    return pl.pallas_call(
        body, name=name, grid=(t // TM_ACC,),
        in_specs=[a_spec, b_spec, row, pl.BlockSpec((1, d), lambda i: (0, 0))], out_specs=[row, row],
        out_shape=[jax.ShapeDtypeStruct((t, d), F32), jax.ShapeDtypeStruct((t, d), BF16)],
        compiler_params=_params(("parallel",)),
    )(a, b, res, g)


def _matmul_norm_bwd(a, b, x, g, dres, *, a_spec, b_spec, dims, name):
    t, d = x.shape

    def body(a_ref, b_ref, x_ref, g_ref, dres_ref, dx_ref, dxb_ref, dg_ref):
        dhv = _contract(a_ref, b_ref, dims)
        xv = x_ref[...]
        r = lax.rsqrt(jnp.mean(xv * xv, axis=-1, keepdims=True) + EPS)
        xh = xv * r
        gd = dhv * g_ref[...]
        dx = r * (gd - xh * jnp.mean(gd * xh, axis=-1, keepdims=True)) + dres_ref[...]
        dx_ref[...] = dx
        dxb_ref[...] = dx.astype(BF16)
        part = jnp.sum(dhv * xh, axis=0, keepdims=True)

        @pl.when(pl.program_id(0) == 0)
        def _():
            dg_ref[...] = part

        @pl.when(pl.program_id(0) > 0)
        def _():
            dg_ref[...] += part

    row = pl.BlockSpec((TM_ACC, d), lambda i: (i, 0))
    vec = pl.BlockSpec((1, d), lambda i: (0, 0))
    return pl.pallas_call(
        body, name=name, grid=(t // TM_ACC,),
        in_specs=[a_spec, b_spec, row, vec, row], out_specs=[row, row, vec],
        out_shape=[jax.ShapeDtypeStruct((t, d), F32), jax.ShapeDtypeStruct((t, d), BF16),
                   jax.ShapeDtypeStruct((1, d), F32)],
        compiler_params=_params(("arbitrary",)),
    )(a, b, x, g, dres)


def _loss_head(x, g, target, name):
    t, d = x.shape

    def body(x_ref, g_ref, t_ref, loss_ref, dx_ref, dxb_ref, dg_ref):
        xv = x_ref[...]
        r = lax.rsqrt(jnp.mean(xv * xv, axis=-1, keepdims=True) + EPS)
        xh = xv * r
        gv = g_ref[...]
        err = xh * gv - t_ref[...]
        loss = jnp.full((1, LANES), 0.5 / d, F32) * jnp.sum(err * err)
        dy = err * (1.0 / d)
        gd = dy * gv
        dx = r * (gd - xh * jnp.mean(gd * xh, axis=-1, keepdims=True))
        dx_ref[...] = dx
        dxb_ref[...] = dx.astype(BF16)
        part = jnp.sum(dy * xh, axis=0, keepdims=True)

        @pl.when(pl.program_id(0) == 0)
        def _():
            dg_ref[...] = part
            loss_ref[...] = loss

        @pl.when(pl.program_id(0) > 0)
        def _():
            dg_ref[...] += part
            loss_ref[...] += loss

    row = pl.BlockSpec((ROW_TILE, d), lambda i: (i, 0))
    vec = pl.BlockSpec((1, d), lambda i: (0, 0))
    return pl.pallas_call(
        body, name=name, grid=(t // ROW_TILE,),
        in_specs=[row, vec, row],
        out_specs=[pl.BlockSpec((1, LANES), lambda i: (0, 0)), row, row, vec],
        out_shape=[jax.ShapeDtypeStruct((1, LANES), F32), jax.ShapeDtypeStruct((t, d), F32),
                   jax.ShapeDtypeStruct((t, d), BF16), jax.ShapeDtypeStruct((1, d), F32)],
        compiler_params=_params(("arbitrary",)),
    )(x, g, target)


def _group_matrix(n):
    shift = int(math.log2(HEAD_DIM))
    r = lax.broadcasted_iota(jnp.int32, (n, n), 0) >> shift
    c = lax.broadcasted_iota(jnp.int32, (n, n), 1) >> shift
    return (r == c).astype(BF16)


def _group_sum(v, gmat):
    hi = v.astype(BF16)
    lo = (v - hi.astype(F32)).astype(BF16)

    def dot(p):
        return jnp.dot(p, gmat, preferred_element_type=F32)

    return dot(hi) + dot(lo)


def _shift_rows(ext, k):
    return pltpu.roll(ext, k % ext.shape[0], 0)


def _store_columns(stage, out_hbm, sems, row0, nrows, col_blocks):
    rows = pl.ds(pl.multiple_of(row0, SUBLANES * 2), nrows)
    copies = [
        pltpu.make_async_copy(stage.at[i], out_hbm.at[rows, pl.ds(pl.multiple_of(cb * LANES, LANES), LANES)], sems.at[i])
        for i, cb in enumerate(col_blocks)
    ]
    for cp in copies:
        cp.start()
    for cp in copies:
        cp.wait()


def _attn_consts(width):
    i = lax.broadcasted_iota(jnp.int32, (BAND, width), 0)
    j = lax.broadcasted_iota(jnp.int32, (BAND, width), 1)
    dist = (width - BAND) + i - j
    inwin = (dist >= 0) & (dist <= BAND)
    return dist.astype(F32), inwin, j


def _head_masks():
    lane = lax.broadcasted_iota(jnp.int32, (1, LANES), 1)
    return [(lane < HEAD_DIM).astype(F32), (lane >= HEAD_DIM).astype(F32)]


def _pair_bias(slope, dil):
    distf, inwin, _ = _attn_consts(2 * BAND)
    return jnp.concatenate([jnp.where(inwin, distf * (slope[hh] * (-float(dil))), NEG) for hh in range(2)], axis=0)


def _stack_heads(xv, hmask):
    return jnp.concatenate([xv * hmask[0], xv * hmask[1]], axis=0).astype(BF16)


FWD_UNROLL = 16
BWD_UNROLL = 16


def _unroll(trips, most):
    return max(u for u in range(1, most + 1) if trips % u == 0)


def _for_blocks(seq, dil, block, most):
    nb = seq // dil // BAND

    def residue(r, carry):
        base = r * nb
        block(pl.multiple_of(base * BAND, BAND), None)
        if nb > 1:
            def rest(n, c):
                block(pl.multiple_of((base + n) * BAND, BAND), pl.multiple_of((base + n - 1) * BAND, BAND))
                return c

            lax.fori_loop(1, nb, rest, 0, unroll=_unroll(nb - 1, most))
        return carry

    if dil == 1:
        residue(0, 0)
    else:
        lax.fori_loop(0, dil, residue, 0, unroll=_unroll(dil, max(1, most // nb)))


def _permute_in(src_ref, dst_ref, dil, seq):
    length = seq // dil
    for r in range(dil):
        dst_ref[pl.ds(r * length, length), :] = src_ref[pl.ds(r, length, stride=dil), :].astype(dst_ref.dtype)


def _slopes_table():
    slopes = 2.0 ** (-8.0 * jnp.arange(1, N_HEADS + 1, dtype=F32) / N_HEADS)
    return jnp.broadcast_to(slopes[:, None], (N_HEADS, 2 * BAND))


def _attn_fwd(proj, attn_g, nbatch, seq):
    t = nbatch * seq
    scale = HEAD_DIM ** -0.5

    def body(q_ref, k_ref, v_ref, g_ref, sl_ref, o_ref, lse_ref, cat_ref, pq, pk, pv, po, pm, pll, ao, am, al):
        hp = pl.program_id(1)
        hmask = _head_masks()
        slope = [sl_ref[pl.ds(2 * hp + hh, 1), :] for hh in range(2)]

        def run_branch(dil, qs, ks, vs, osink, msink, lsink):
            bias = _pair_bias(slope, dil)

            def block(row0, prow):
                cur = pl.ds(row0, BAND)
                q2 = _stack_heads(qs[cur, :] * scale, hmask)
                if prow is None:
                    kk, vv, bias_b = ks[cur, :], vs[cur, :], bias[:, BAND:]
                else:
                    prev = pl.ds(prow, BAND)
                    kk = jnp.concatenate([ks[prev, :], ks[cur, :]], axis=0)
                    vv = jnp.concatenate([vs[prev, :], vs[cur, :]], axis=0)
                    bias_b = bias
                s = lax.dot_general(q2, kk.astype(BF16), NT, preferred_element_type=F32) + bias_b
                m = jnp.max(s, axis=1, keepdims=True)
                p = jnp.exp(s - m)
                l = jnp.sum(p, axis=1, keepdims=True)
                pb = p.astype(BF16)
                o = jnp.dot(jnp.concatenate([pb[:BAND], pb[BAND:]], axis=1), _stack_heads(vv, hmask),
                            preferred_element_type=F32)
                osink[cur, :] = o
                msink[cur, :] = m[:BAND] * hmask[0] + m[BAND:] * hmask[1]
                lsink[cur, :] = l[:BAND] * hmask[0] + l[BAND:] * hmask[1]

            _for_blocks(seq, dil, block, FWD_UNROLL)

        run_branch(1, q_ref, k_ref, v_ref, ao, am, al)
        for dil in DILATIONS[1:]:
            length = seq // dil
            _permute_in(q_ref, pq, dil, seq)
            _permute_in(k_ref, pk, dil, seq)
            _permute_in(v_ref, pv, dil, seq)
            run_branch(dil, pq, pk, pv, po, pm, pll)
            for r in range(dil):
                nat = pl.ds(r, length, stride=dil)
                per = pl.ds(r * length, length)
                m0 = am[nat, :]
                mb = pm[per, :]
                mn = jnp.maximum(m0, mb)
                e0 = jnp.exp(m0 - mn)
                eb = jnp.exp(mb - mn)
                ao[nat, :] = ao[nat, :] * e0 + po[per, :] * eb
                al[nat, :] = al[nat, :] * e0 + pll[per, :] * eb
                am[nat, :] = mn

        gmat = _group_matrix(LANES)
        gv = g_ref[...]

        def fin(c, carry):
            rows = pl.ds(pl.multiple_of(c * 256, 256), 256)
            lv = al[rows, :]
            o = ao[rows, :] / lv
            o_ref[rows, :] = o
            lse_ref[rows, :] = am[rows, :] + jnp.log(lv)
            ms = _group_sum(o * o, gmat) * (1.0 / HEAD_DIM)
            cat_ref[rows, :] = (o * lax.rsqrt(ms + EPS) * gv).astype(BF16)
            return carry

        lax.fori_loop(0, seq // 256, fin, 0, unroll=True)

    nq = D_ATTN // LANES
    blk = lambda off: pl.BlockSpec((seq, LANES), lambda b, h: (b, h + off))
    scratch = [pltpu.VMEM((seq, LANES), F32) for _ in range(9)]
    return pl.pallas_call(
        body, name="attn_fwd", grid=(nbatch, nq),
        in_specs=[blk(0), blk(nq), blk(2 * nq), pl.BlockSpec((1, LANES), lambda b, h: (0, h)),
                  pl.BlockSpec((N_HEADS, 2 * BAND), lambda b, h: (0, 0))],
        out_specs=[blk(0), blk(0), blk(0)],
        out_shape=[jax.ShapeDtypeStruct((t, D_ATTN), F32), jax.ShapeDtypeStruct((t, D_ATTN), F32),
                   jax.ShapeDtypeStruct((t, D_MODEL), BF16)],
        scratch_shapes=scratch,
        compiler_params=_params(("parallel", "parallel")),
    )(proj, proj, proj, attn_g, _slopes_table())


def _attn_bwd(proj, o, lse, d_cat, attn_g, nbatch, seq):
    t = nbatch * seq
    scale = HEAD_DIM ** -0.5

    def body(q_ref, k_ref, v_ref, o_ref, lse_ref, dy_ref, g_ref, sl_ref, dproj_ref, dg_ref,
             do_n, dl_n, dq_n, dk_n, dv_n, pq, pk, pv, pdo, plse, pdl, pdq, pdk, pdv, stage, sems):
        hp = pl.program_id(0)
        hmask = _head_masks()
        slope = [sl_ref[pl.ds(2 * hp + hh, 1), :] for hh in range(2)]
        gmat = _group_matrix(LANES)
        gv = g_ref[...]

        def prep(c, dg):
            rows = pl.ds(pl.multiple_of(c * 256, 256), 256)
            ov = o_ref[rows, :]
            dyn = dy_ref[rows, :].astype(F32)
            r = lax.rsqrt(_group_sum(ov * ov, gmat) * (1.0 / HEAD_DIM) + EPS)
            gd = dyn * gv
            oh = ov * r
            do = r * (gd - oh * (_group_sum(gd * oh, gmat) * (1.0 / HEAD_DIM)))
            do_n[rows, :] = do
            dl_n[rows, :] = _group_sum(do * ov, gmat)
            return dg + jnp.sum(dyn * oh, axis=0, keepdims=True)

        dg = lax.fori_loop(0, seq // 256, prep, jnp.zeros((1, LANES), F32), unroll=True)

        @pl.when(pl.program_id(1) == 0)
        def _():
            dg_ref[...] = dg

        @pl.when(pl.program_id(1) > 0)
        def _():
            dg_ref[...] += dg

        def clear(*refs):
            def step(c, carry):
                rows = pl.ds(pl.multiple_of(c * 256, 256), 256)
                for ref in refs:
                    ref[rows, :] = jnp.zeros((256, LANES), F32)
                return carry

            lax.fori_loop(0, seq // 256, step, 0)

        clear(dq_n, dk_n, dv_n)

        def run_branch(dil, qs, ks, vs, dos, lses, dls, dqs, dks, dvs):
            bias = _pair_bias(slope, dil)

            def per_head(xv):
                return jnp.concatenate([xv[:, 0:1], xv[:, HEAD_DIM:HEAD_DIM + 1]], axis=0)

            def block(row0, prow):
                cur = pl.ds(row0, BAND)
                keys = cur if prow is None else pl.ds(prow, 2 * BAND)
                q2 = _stack_heads(qs[cur, :] * scale, hmask)
                do2 = _stack_heads(dos[cur, :], hmask)
                kk, vv = ks[keys, :], vs[keys, :]
                s = lax.dot_general(q2, kk.astype(BF16), NT, preferred_element_type=F32)
                s = s + (bias[:, BAND:] if prow is None else bias)
                p = jnp.exp(s - per_head(lses[cur, :]))
                dp = lax.dot_general(do2, vv.astype(BF16), NT, preferred_element_type=F32)
                ds = (p * (dp - per_head(dls[cur, :]))).astype(BF16)
                dqs[cur, :] += jnp.dot(jnp.concatenate([ds[:BAND], ds[BAND:]], axis=1), _stack_heads(kk, hmask),
                                       preferred_element_type=F32)
                dks[keys, :] += lax.dot_general(ds, q2, TN, preferred_element_type=F32)
                dvs[keys, :] += lax.dot_general(p.astype(BF16), do2, TN, preferred_element_type=F32)

            _for_blocks(seq, dil, block, BWD_UNROLL)

        run_branch(1, q_ref, k_ref, v_ref, do_n, lse_ref, dl_n, dq_n, dk_n, dv_n)
        for dil in DILATIONS[1:]:
            length = seq // dil
            for src, dst in ((q_ref, pq), (k_ref, pk), (v_ref, pv), (do_n, pdo), (lse_ref, plse), (dl_n, pdl)):
                _permute_in(src, dst, dil, seq)
            clear(pdq, pdk, pdv)
            run_branch(dil, pq, pk, pv, pdo, plse, pdl, pdq, pdk, pdv)
            for r in range(dil):
                nat = pl.ds(r, length, stride=dil)
                per = pl.ds(r * length, length)
                dq_n[nat, :] += pdq[per, :]
                dk_n[nat, :] += pdk[per, :]
                dv_n[nat, :] += pdv[per, :]

        def emit(c, carry):
            rows = pl.ds(pl.multiple_of(c * 256, 256), 256)
            stage[0, rows, :] = (dq_n[rows, :] * scale).astype(BF16)
            stage[1, rows, :] = dk_n[rows, :].astype(BF16)
            stage[2, rows, :] = dv_n[rows, :].astype(BF16)
            return carry

        lax.fori_loop(0, seq // 256, emit, 0)
        _store_columns(stage, dproj_ref, sems, pl.program_id(1) * seq, seq, [hp, nq + hp, 2 * nq + hp])

    nq = D_ATTN // LANES
    blk = lambda off: pl.BlockSpec((seq, LANES), lambda h, b: (b, h + off))
    vec = pl.BlockSpec((1, LANES), lambda h, b: (0, h))
    scratch = [pltpu.VMEM((seq, LANES), F32) for _ in range(14)]
    scratch += [pltpu.VMEM((3, seq, LANES), BF16), pltpu.SemaphoreType.DMA((3,))]
    d_proj, dg = pl.pallas_call(
        body, name="attn_bwd", grid=(nq, nbatch),
        in_specs=[blk(0), blk(nq), blk(2 * nq), blk(0), blk(0), blk(0), vec,
                  pl.BlockSpec((N_HEADS, 2 * BAND), lambda h, b: (0, 0))],
        out_specs=[pl.BlockSpec(memory_space=pl.ANY), vec],
        out_shape=[jax.ShapeDtypeStruct((t, D_IN), BF16), jax.ShapeDtypeStruct((1, D_ATTN), F32)],
        scratch_shapes=scratch,
        compiler_params=_params(("arbitrary", "arbitrary")),
    )(proj, proj, proj, o, lse, d_cat, attn_g, _slopes_table())
    return d_proj, dg


HALO = SUBLANES
PACKED_ROWS = 2 * SUBLANES


def _window(ref, c, rows, nchunks, after):
    row0 = pl.multiple_of(c * rows, rows)
    prev0 = pl.multiple_of(jnp.maximum(row0 - PACKED_ROWS, 0), PACKED_ROWS)
    before = ref[pl.ds(prev0, PACKED_ROWS), :].astype(F32)[PACKED_ROWS - HALO:] * (c > 0).astype(F32)
    parts = [before, ref[pl.ds(row0, rows), :].astype(F32)]
    if after:
        next0 = pl.multiple_of(jnp.minimum(row0 + rows, (nchunks - 1) * rows), PACKED_ROWS)
        parts.append(ref[pl.ds(next0, PACKED_ROWS), :].astype(F32)[:HALO] * (c < nchunks - 1).astype(F32))
    return jnp.concatenate(parts, axis=0)


def _behind(z):
    z1 = _shift_rows(z, 1)
    return z1, _shift_rows(z1, 1)


def _ahead(dy):
    d1 = _shift_rows(dy, -1)
    return d1, _shift_rows(d1, -1)


def _conv(z, w):
    z1, z2 = _behind(z)
    return w[0:1] * z2 + w[1:2] * z1 + w[2:3] * z


def _conv_bwd(dy, z, w, cur):
    d1, d2 = _ahead(dy)
    dz = w[2:3] * dy + w[1:2] * d1 + w[0:1] * d2
    return dz, [jnp.sum((d * z)[cur], axis=0, keepdims=True) for d in (d2, d1, dy)]


def _sigmoid(a):
    return 0.5 * jnp.tanh(0.5 * a) + 0.5


MIX_ROWS = 256
GATE_B_BLOCK = 3 * D_ATTN // LANES
GATE_C_BLOCK = GATE_B_BLOCK + D_CONV // LANES
U_BLOCK = GATE_C_BLOCK + D_CONV // LANES


def _convmix_fwd(proj, cat, mcw, conv_g, nbatch, seq):
    nchunks = seq // MIX_ROWS

    def body(gb_ref, gc_ref, u_ref, w_ref, g_ref, cat_in, cat_ref):
        del cat_in
        gmat = _group_matrix(LANES)
        w = w_ref[...]
        gv = g_ref[...]

        def step(c, carry):
            cur = pl.ds(pl.multiple_of(c * MIX_ROWS, MIX_ROWS), MIX_ROWS)
            z = _window(gc_ref, c, MIX_ROWS, nchunks, False) * _window(u_ref, c, MIX_ROWS, nchunks, False)
            y = gb_ref[cur, :] * _conv(z, w)[HALO:]
            ms = _group_sum(y * y, gmat) * (1.0 / HEAD_DIM)
            cat_ref[cur, :] = (y * lax.rsqrt(ms + EPS) * gv).astype(BF16)
            return carry

        lax.fori_loop(0, nchunks, step, 0, unroll=True)

    nc = D_CONV // LANES
    blk = lambda off: pl.BlockSpec((seq, LANES), lambda b, j: (b, j + off))
    return pl.pallas_call(
        body, name="convmix_fwd", grid=(nbatch, nc),
        in_specs=[blk(GATE_B_BLOCK), blk(GATE_C_BLOCK), blk(U_BLOCK),
                  pl.BlockSpec((3, LANES), lambda b, j: (0, j)), pl.BlockSpec((1, LANES), lambda b, j: (0, j)),
                  pl.BlockSpec(memory_space=pl.ANY)],
        out_specs=blk(D_ATTN // LANES),
        out_shape=jax.ShapeDtypeStruct(cat.shape, cat.dtype),
        input_output_aliases={5: 0},
        compiler_params=_params(("parallel", "parallel")),
    )(proj, proj, proj, mcw, conv_g, cat)


def _convmix_bwd(proj, d_cat, d_proj, mcw, conv_g, nbatch, seq):
    nchunks = seq // MIX_ROWS

    def body(gb_ref, gc_ref, u_ref, dy_ref, w_ref, g_ref, dproj_in, dproj_ref, dw_ref, dg_ref, stage, sems):
        del dproj_in
        cb = pl.program_id(0)
        b = pl.program_id(1)
        gmat = _group_matrix(LANES)
        w = w_ref[...]
        gv = g_ref[...]
        cur = slice(HALO, HALO + MIX_ROWS)

        def step(c, carry):
            rows = pl.ds(pl.multiple_of(c * MIX_ROWS, MIX_ROWS), MIX_ROWS)
            gb = _window(gb_ref, c, MIX_ROWS, nchunks, True)
            gc = _window(gc_ref, c, MIX_ROWS, nchunks, True)
            u = _window(u_ref, c, MIX_ROWS, nchunks, True)
            dyn = _window(dy_ref, c, MIX_ROWS, nchunks, True)
            z = gc * u
            conv = _conv(z, w)
            y = gb * conv
            r = lax.rsqrt(_group_sum(y * y, gmat) * (1.0 / HEAD_DIM) + EPS)
            yh = y * r
            gd = dyn * gv
            dy = r * (gd - yh * (_group_sum(gd * yh, gmat) * (1.0 / HEAD_DIM)))
            dz, dws = _conv_bwd(dy * gb, z, w, cur)
            stage[0, rows, :] = (dy * conv)[cur].astype(BF16)
            stage[1, rows, :] = (dz * u)[cur].astype(BF16)
            stage[2, rows, :] = (dz * gc)[cur].astype(BF16)
            dg = jnp.sum((dyn * yh)[cur], axis=0, keepdims=True)
            return tuple(a + d for a, d in zip(carry, dws + [dg]))

        zero = jnp.zeros((1, LANES), F32)
        dw0, dw1, dw2, dg = lax.fori_loop(0, nchunks, step, (zero, zero, zero, zero), unroll=True)

        @pl.when(b == 0)
        def _():
            dw_ref[0:1, :] = dw0
            dw_ref[1:2, :] = dw1
            dw_ref[2:3, :] = dw2
            dg_ref[...] = dg

        @pl.when(b > 0)
        def _():
            dw_ref[0:1, :] += dw0
            dw_ref[1:2, :] += dw1
            dw_ref[2:3, :] += dw2
            dg_ref[...] += dg

        _store_columns(stage, dproj_ref, sems, b * seq, seq, [GATE_B_BLOCK + cb, GATE_C_BLOCK + cb, U_BLOCK + cb])

    nc = D_CONV // LANES
    blk = lambda off: pl.BlockSpec((seq, LANES), lambda j, b: (b, j + off))
    return pl.pallas_call(
        body, name="convmix_bwd", grid=(nc, nbatch),
        in_specs=[blk(GATE_B_BLOCK), blk(GATE_C_BLOCK), blk(U_BLOCK), blk(D_ATTN // LANES),
                  pl.BlockSpec((3, LANES), lambda j, b: (0, j)), pl.BlockSpec((1, LANES), lambda j, b: (0, j)),
                  pl.BlockSpec(memory_space=pl.ANY)],
        out_specs=[pl.BlockSpec(memory_space=pl.ANY), pl.BlockSpec((3, LANES), lambda j, b: (0, j)),
                   pl.BlockSpec((1, LANES), lambda j, b: (0, j))],
        out_shape=[jax.ShapeDtypeStruct(d_proj.shape, d_proj.dtype), jax.ShapeDtypeStruct((3, D_CONV), F32),
                   jax.ShapeDtypeStruct((1, D_CONV), F32)],
        scratch_shapes=[pltpu.VMEM((3, seq, LANES), BF16), pltpu.SemaphoreType.DMA((3,))],
        input_output_aliases={6: 0},
        compiler_params=_params(("arbitrary", "arbitrary")),
    )(proj, proj, proj, d_cat, mcw, conv_g, d_proj)


FFN_ROWS = 256


def _ffn_fwd(h, wup, fcw, wdown, res, g, seq, name):
    t, d = res.shape
    tiles_per_seq = seq // FFN_ROWS

    def body(hm_ref, hp_ref, wu_ref, w_ref, wd_ref, r_ref, *rest):
        if g is None:
            x_ref, act_ref, pre_ref = rest
        else:
            g_ref, x_ref, h_ref, act_ref, pre_ref = rest
        inside = ((pl.program_id(0) % tiles_per_seq) > 0).astype(F32)
        wrow = lax.broadcasted_iota(jnp.int32, (FFN_ROWS + HALO, 1), 0)
        edge = jnp.where(wrow < HALO, inside, 1.0)
        rows = jnp.concatenate([hp_ref[...], hm_ref[...]], axis=0)

        def up(j, part, p):
            full = lax.dot_general(rows, wu_ref[j], NT, preferred_element_type=F32).astype(BF16)
            pre_ref[part, p] = full[PACKED_ROWS:]
            return full.astype(F32)[PACKED_ROWS - HALO:] * edge

        total = r_ref[...]
        for p in range(N_UP_PAIRS):
            a = _conv(up(p, 0, p), w_ref[0, p])[HALO:]
            v = _conv(up(N_UP_PAIRS + p, 1, p), w_ref[1, p])[HALO:]
            act = (a * _sigmoid(a) * v).astype(BF16)
            act_ref[p] = act
            total = total + jnp.dot(act, wd_ref[p], preferred_element_type=F32)
        x_ref[...] = total
        if g is not None:
            h_ref[...] = (total * lax.rsqrt(jnp.mean(total * total, axis=-1, keepdims=True) + EPS) * g_ref[...]).astype(BF16)

    row = pl.BlockSpec((FFN_ROWS, d), lambda i: (i, 0))
    tiles_per_halo = FFN_ROWS // PACKED_ROWS
    in_specs = [
        row, pl.BlockSpec((PACKED_ROWS, d), lambda i: (jnp.maximum(i * tiles_per_halo - 1, 0), 0)),
        pl.BlockSpec((N_DEV, UP_CHUNK, d), lambda i: (0, 0, 0)),
        pl.BlockSpec((2, N_UP_PAIRS, 3, UP_CHUNK), lambda i: (0, 0, 0, 0)),
        pl.BlockSpec((N_UP_PAIRS, UP_CHUNK, d), lambda i: (0, 0, 0)), row]
    out_specs = [row]
    out_shape = [jax.ShapeDtypeStruct((t, d), F32)]
    args = [h, h, wup, fcw, wdown, res]
    if g is not None:
        in_specs.append(pl.BlockSpec((1, d), lambda i: (0, 0)))
        out_specs.append(row)
        out_shape.append(jax.ShapeDtypeStruct((t, d), BF16))
        args.append(g)
    out_specs += [pl.BlockSpec((N_UP_PAIRS, FFN_ROWS, UP_CHUNK), lambda i: (0, i, 0)),
                  pl.BlockSpec((2, N_UP_PAIRS, FFN_ROWS, UP_CHUNK), lambda i: (0, 0, i, 0))]
    out_shape += [jax.ShapeDtypeStruct((N_UP_PAIRS, t, UP_CHUNK), BF16),
                  jax.ShapeDtypeStruct((2, N_UP_PAIRS, t, UP_CHUNK), BF16)]
    return pl.pallas_call(
        body, name=name, grid=(t // FFN_ROWS,), in_specs=in_specs, out_specs=out_specs, out_shape=out_shape,
        compiler_params=_params(("parallel",)),
    )(*args)


def _ffn_up_bwd(pre, dy, fcw, wdown, wup, x, g, dres, seq, name):
    t, d = x.shape
    tiles_per_seq = seq // FFN_ROWS
    tiles_per_halo = FFN_ROWS // PACKED_ROWS
    last_halo = t // PACKED_ROWS - 1

    def body(pm_ref, pp_ref, pn_ref, dm_ref, dp_ref, dn_ref, w_ref, wd_ref, wu_ref, x_ref, g_ref, dres_ref,
             dpre_ref, dw_ref, dx_ref, dxb_ref, dg_ref):
        i = pl.program_id(0)
        has_prev = ((i % tiles_per_seq) > 0).astype(F32)
        has_next = ((i % tiles_per_seq) < tiles_per_seq - 1).astype(F32)
        cur = slice(HALO, HALO + FFN_ROWS)

        def window(before, main, after):
            return jnp.concatenate([before.astype(F32)[PACKED_ROWS - HALO:] * has_prev, main.astype(F32),
                                    after.astype(F32)[:HALO] * has_next], axis=0)

        dy_rows = jnp.concatenate([dp_ref[...], dm_ref[...], dn_ref[...]], axis=0)
        wrow = lax.broadcasted_iota(jnp.int32, (FFN_ROWS + 2 * HALO, 1), 0)
        edge = jnp.where(wrow < HALO, has_prev, jnp.where(wrow >= HALO + FFN_ROWS, has_next, 1.0))

        dh = jnp.zeros((FFN_ROWS, d), F32)
        sums = []
        for p in range(N_UP_PAIRS):
            pg = window(pp_ref[0, p], pm_ref[0, p], pn_ref[0, p])
            pv = window(pp_ref[1, p], pm_ref[1, p], pn_ref[1, p])
            dact = lax.dot_general(dy_rows, wd_ref[p], NT, preferred_element_type=F32)
            dact = dact[PACKED_ROWS - HALO:PACKED_ROWS + FFN_ROWS + HALO] * edge
            a = _conv(pg, w_ref[0, p])
            v = _conv(pv, w_ref[1, p])
            sg = _sigmoid(a)
            asg = a * sg
            dzg, dwg = _conv_bwd(dact * v * (sg + asg - asg * sg), pg, w_ref[0, p], cur)
            dzv, dwv = _conv_bwd(dact * asg, pv, w_ref[1, p], cur)
            dgate = dzg[cur].astype(BF16)
            dval = dzv[cur].astype(BF16)
            dpre_ref[0, p] = dgate
            dpre_ref[1, p] = dval
            dh = dh + jnp.dot(dgate, wu_ref[p], preferred_element_type=F32)
            dh = dh + jnp.dot(dval, wu_ref[N_UP_PAIRS + p], preferred_element_type=F32)
            sums.append(dwg + dwv)

        xv = x_ref[...]
        r = lax.rsqrt(jnp.mean(xv * xv, axis=-1, keepdims=True) + EPS)
        xh = xv * r
        gd = dh * g_ref[...]
        dx = r * (gd - xh * jnp.mean(gd * xh, axis=-1, keepdims=True)) + dres_ref[...]
        dx_ref[...] = dx
        dxb_ref[...] = dx.astype(BF16)
        part = jnp.sum(dh * xh, axis=0, keepdims=True)

        @pl.when(i == 0)
        def _():
            dg_ref[...] = part
            for p in range(N_UP_PAIRS):
                for k in range(6):
                    dw_ref[k // 3, p, pl.ds(k % 3, 1), :] = sums[p][k]

        @pl.when(i > 0)
        def _():
            dg_ref[...] += part
            for p in range(N_UP_PAIRS):
                for k in range(6):
                    dw_ref[k // 3, p, pl.ds(k % 3, 1), :] += sums[p][k]

    def rows4(n):
        return lambda fn: pl.BlockSpec((2, N_UP_PAIRS, n, UP_CHUNK), lambda i: (0, 0, fn(i), 0))

    def rows2(n):
        return lambda fn: pl.BlockSpec((n, d), lambda i: (fn(i), 0))

    prev_tile = lambda i: jnp.maximum(i * tiles_per_halo - 1, 0)
    next_tile = lambda i: jnp.minimum((i + 1) * tiles_per_halo, last_halo)
    row = pl.BlockSpec((FFN_ROWS, d), lambda i: (i, 0))
    vec = pl.BlockSpec((1, d), lambda i: (0, 0))
    wspec = pl.BlockSpec((2, N_UP_PAIRS, 3, UP_CHUNK), lambda i: (0, 0, 0, 0))
    return pl.pallas_call(
        body, name=name, grid=(t // FFN_ROWS,),
        in_specs=[rows4(FFN_ROWS)(lambda i: i), rows4(PACKED_ROWS)(prev_tile), rows4(PACKED_ROWS)(next_tile),
                  rows2(FFN_ROWS)(lambda i: i), rows2(PACKED_ROWS)(prev_tile), rows2(PACKED_ROWS)(next_tile),
                  wspec, pl.BlockSpec((N_UP_PAIRS, UP_CHUNK, d), lambda i: (0, 0, 0)),
                  pl.BlockSpec((N_DEV, UP_CHUNK, d), lambda i: (0, 0, 0)), row, vec, row],
        out_specs=[rows4(FFN_ROWS)(lambda i: i), wspec, row, row, vec],
        out_shape=[jax.ShapeDtypeStruct(pre.shape, BF16), jax.ShapeDtypeStruct(fcw.shape, F32),
                   jax.ShapeDtypeStruct((t, d), F32), jax.ShapeDtypeStruct((t, d), BF16),
                   jax.ShapeDtypeStruct((1, d), F32)],
        compiler_params=_params(("arbitrary",)),
    )(pre, pre, pre, dy, dy, dy, fcw, wdown, wup, x, g, dres)


def _adamw(lands, w, m, v, row_tile, name, after=()):
    nl = len(lands)
    _, nr, ncol = lands[0].shape
    c1 = 1.0 - ADAM_B1 ** ADAM_STEP
    c2 = 1.0 - ADAM_B2 ** ADAM_STEP

    def body(*refs):
        land_refs = refs[:nl]
        w_ref, m_ref, v_ref = refs[nl:nl + 3]
        g_ref, d_ref, mo_ref, vo_ref = refs[nl + 3 + len(after):]
        for l in range(nl):
            @pl.when(pl.program_id(0) == l)
            def _(l=l):
                g = land_refs[l][0].astype(F32)
                for j in range(1, N_DEV):
                    g = g + land_refs[l][j].astype(F32)
                g_ref[...] = g

        g = g_ref[...]
        m2 = ADAM_B1 * m_ref[...] + (1.0 - ADAM_B1) * g
        v2 = ADAM_B2 * v_ref[...] + (1.0 - ADAM_B2) * (g * g)
        mo_ref[...] = m2
        vo_ref[...] = v2
        d_ref[...] = -ADAM_LR * ((m2 / c1) / (jnp.sqrt(v2 / c2) + ADAM_EPS) + ADAM_WD * w_ref[...])

    def land_spec(l):
        return pl.BlockSpec((N_DEV, row_tile, ncol), lambda k, i: (0, jnp.where(k == l, i, 0), 0))

    tile = pl.BlockSpec((None, row_tile, ncol), lambda k, i: (k, i, 0))
    return pl.pallas_call(
        body, name=name, grid=(nl, nr // row_tile),
        in_specs=[land_spec(l) for l in range(nl)] + [tile, tile, tile] + [pl.BlockSpec(memory_space=pl.ANY)] * len(after),
        out_specs=[tile] * 4,
        out_shape=[jax.ShapeDtypeStruct(w.shape, F32)] * 4,
        compiler_params=_params(("arbitrary", "arbitrary")),
    )(*lands, w, m, v, *after)


class _Item:
    def __init__(self, src, chunked, land_cols=False):
        self.src, self.chunked, self.land_cols = src, chunked, land_cols
        if chunked == "cols":
            block = (src.shape[0], src.shape[1] // N_DEV)
        else:
            block = src.shape[1:] if chunked else src.shape
        self.width = block[-1]
        self.land_shape = (block[0], N_DEV * block[1]) if land_cols else (N_DEV,) + block

    def _cols(self, first, count=1):
        return pl.ds(pl.multiple_of(first * self.width, LANES), count * self.width)

    def part(self, src_ref, j):
        if self.chunked == "cols":
            return src_ref.at[:, self._cols(j)]
        return src_ref.at[j] if self.chunked else src_ref

    def slot(self, land_ref, s):
        return land_ref.at[:, self._cols(s)] if self.land_cols else land_ref.at[s]


def _mesh_place():
    x, y, c = lax.axis_index("x"), lax.axis_index("y"), lax.axis_index("c")
    return x, y, c, 4 * x + 2 * y + c


def _flipped(x, y, c, k):
    px = 1 - x if k & 4 else x
    py = 1 - y if k & 2 else y
    pc = 1 - c if k & 1 else c
    return (px, py, pc), 4 * px + 2 * py + pc


PEER_ORDER = (2, 4, 6, 3, 5, 7, 1)


def _exchange(items, name):
    n = len(items)

    def body(*refs):
        srcs, lands = refs[:n], refs[n:2 * n]
        send, recv, local = refs[2 * n:]
        x, y, c, me = _mesh_place()

        def copy(i, k, chunk, slot, dev):
            return pltpu.make_async_remote_copy(
                src_ref=items[i].part(srcs[i], chunk), dst_ref=items[i].slot(lands[i], slot),
                send_sem=send.at[i, k - 1], recv_sem=recv.at[i, k - 1], device_id=dev, device_id_type=MESH)

        own = [pltpu.make_async_copy(items[i].part(srcs[i], me), items[i].slot(lands[i], me), local.at[i])
               for i in range(n)]
        for k in PEER_ORDER:
            dev, idx = _flipped(x, y, c, k)
            for i in range(n):
                copy(i, k, idx, me, dev).start()
        for cp in own:
            cp.start()
        for k in PEER_ORDER:
            dev, idx = _flipped(x, y, c, k)
            for i in range(n):
                copy(i, k, me, idx, dev).wait_recv()
        for k in PEER_ORDER:
            dev, idx = _flipped(x, y, c, k)
            for i in range(n):
                copy(i, k, idx, me, dev).wait_send()
        for cp in own:
            cp.wait()

    hbm = pl.BlockSpec(memory_space=pl.ANY)
    return pl.pallas_call(
        body, name=name,
        in_specs=[hbm] * n, out_specs=[hbm] * n,
        out_shape=[jax.ShapeDtypeStruct(it.land_shape, it.src.dtype) for it in items],
        scratch_shapes=[pltpu.SemaphoreType.DMA((n, N_DEV - 1)), pltpu.SemaphoreType.DMA((n, N_DEV - 1)),
                        pltpu.SemaphoreType.DMA((n,))],
        compiler_params=pltpu.CompilerParams(has_side_effects=True),
    )(*[it.src for it in items])


SAME_CORE = (2, 4, 6)


def _sequencer_gather(items, name, collective_id):
    n = len(items)

    def body(*refs):
        srcs, lands = refs[:n], refs[n:2 * n]
        send, recv, local = refs[2 * n:]
        x, y, c, me = _mesh_place()
        sibling, _ = _flipped(x, y, c, 1)
        barrier = pltpu.get_barrier_semaphore()
        for k in SAME_CORE + (1,):
            pl.semaphore_signal(barrier, inc=1, device_id=_flipped(x, y, c, k)[0], device_id_type=MESH)
        pl.semaphore_wait(barrier, len(SAME_CORE) + 1)

        def copy(i, q, src, slot, dev):
            return pltpu.make_async_remote_copy(
                src_ref=src, dst_ref=items[i].slot(lands[i], slot),
                send_sem=send.at[i, q - 1], recv_sem=recv.at[i, q - 1], device_id=dev, device_id_type=MESH)

        own = [pltpu.make_async_copy(srcs[i], items[i].slot(lands[i], me), local.at[i]) for i in range(n)]
        for cp in own:
            cp.start()
        for k in SAME_CORE + (1,):
            for i in range(n):
                copy(i, k, srcs[i], me, _flipped(x, y, c, k)[0]).start()
        for k in SAME_CORE:
            dev, idx = _flipped(x, y, c, k)
            for i in range(n):
                copy(i, k, srcs[i], idx, dev).wait_recv()
            for i in range(n):
                copy(i, k + 1, items[i].slot(lands[i], idx), idx, sibling).start()
        for k in (1,) + tuple(k + 1 for k in SAME_CORE):
            _, idx = _flipped(x, y, c, k)
            for i in range(n):
                copy(i, k, srcs[i], idx, sibling).wait_recv()
        for k in range(1, N_DEV):
            for i in range(n):
                copy(i, k, srcs[i], me, sibling).wait_send()
        for cp in own:
            cp.wait()

    return pl.kernel(
        body, name=name,
        out_type=[jax.ShapeDtypeStruct(it.land_shape, it.src.dtype) for it in items],
        mesh=plsc.ScalarSubcoreMesh(axis_name="sequencer", num_cores=1),
        scratch_types=[pltpu.SemaphoreType.DMA((n, N_DEV - 1)), pltpu.SemaphoreType.DMA((n, N_DEV - 1)),
                       pltpu.SemaphoreType.DMA((n,))],
        compiler_params=pltpu.CompilerParams(collective_id=collective_id),
    )(*[it.src for it in items])


def _sequencer_exchange(items, name, collective_id):
    n = len(items)

    def body(*refs):
        srcs, lands = refs[:n], refs[n:2 * n]
        send, recv, local = refs[2 * n:]
        x, y, c, me = _mesh_place()
        barrier = pltpu.get_barrier_semaphore()
        for k in PEER_ORDER:
            pl.semaphore_signal(barrier, inc=1, device_id=_flipped(x, y, c, k)[0], device_id_type=MESH)
        pl.semaphore_wait(barrier, N_DEV - 1)

        def copy(i, k, chunk, slot, dev):
            return pltpu.make_async_remote_copy(
                src_ref=items[i].part(srcs[i], chunk), dst_ref=items[i].slot(lands[i], slot),
                send_sem=send.at[i, k - 1], recv_sem=recv.at[i, k - 1], device_id=dev, device_id_type=MESH)

        own = [pltpu.make_async_copy(items[i].part(srcs[i], me), items[i].slot(lands[i], me), local.at[i])
               for i in range(n)]
        for cp in own:
            cp.start()
        for k in PEER_ORDER:
            dev, idx = _flipped(x, y, c, k)
            for i in range(n):
                copy(i, k, idx, me, dev).start()
        for k in PEER_ORDER:
            dev, idx = _flipped(x, y, c, k)
            for i in range(n):
                copy(i, k, me, idx, dev).wait_recv()
        for k in PEER_ORDER:
            dev, idx = _flipped(x, y, c, k)
            for i in range(n):
                copy(i, k, idx, me, dev).wait_send()
        for cp in own:
            cp.wait()

    return pl.kernel(
        body, name=name,
        out_type=[jax.ShapeDtypeStruct(it.land_shape, it.src.dtype) for it in items],
        mesh=plsc.ScalarSubcoreMesh(axis_name="sequencer", num_cores=1),
        scratch_types=[pltpu.SemaphoreType.DMA((n, N_DEV - 1)), pltpu.SemaphoreType.DMA((n, N_DEV - 1)),
                       pltpu.SemaphoreType.DMA((n,))],
        compiler_params=pltpu.CompilerParams(collective_id=collective_id),
    )(*[it.src for it in items])


TM = 1024
TM_ACC = 512
TN_IN = 768


def kernel(x, norm1_g, w_in, mix_conv_w, attn_out_g, conv_out_g, w_out, norm2_g, ffn_up, ffn_conv_w, ffn_down, final_norm_g, loss_target, m_norm1_g, m_w_in, m_mix_conv_w, m_attn_out_g, m_conv_out_g, m_w_out, m_norm2_g, m_ffn_up, m_ffn_conv_w, m_ffn_down, m_final_norm_g, v_norm1_g, v_w_in, v_mix_conv_w, v_attn_out_g, v_conv_out_g, v_w_out, v_norm2_g, v_ffn_up, v_ffn_conv_w, v_ffn_down, v_final_norm_g):
    nbatch, seq, d = x.shape
    t = nbatch * seq
    nt, nta = t // TM, t // TM_ACC
    out_rows = D_MODEL // N_DEV
    down_rows = D_FF // N_DEV
    xf = x.reshape(t, d)
    target = loss_target.reshape(t, d)

    cw_local = jnp.concatenate([ffn_conv_w, mix_conv_w], axis=-1)
    cast = lambda w: _Item(w.astype(BF16), False)
    cast_in = lambda w: _Item(w.astype(BF16), False, land_cols=True)
    cw_all, win0 = _sequencer_gather([_Item(cw_local, False), cast_in(w_in[0])], "gather_a", 0)
    up_t, m_up_t, v_up_t = (jnp.swapaxes(a, 1, 2) for a in (ffn_up, m_ffn_up, v_ffn_up))
    wout0, wup0 = _sequencer_gather([cast(w_out[0]), cast(up_t[0])], "gather_b", 1)
    (wdown0,) = _sequencer_gather([cast(ffn_down[0])], "gather_c", 2)
    win1, wout1 = _sequencer_gather([cast_in(w_in[1]), cast(w_out[1])], "gather_d", 3)
    wup1, wdown1 = _sequencer_gather([cast(up_t[1]), cast(ffn_down[1])], "gather_e", 7)
    win, wup = [win0, win1], [wup0, wup1]
    wout = [w.reshape(D_MODEL, D_MODEL) for w in (wout0, wout1)]
    wdown = [w.reshape(N_UP_PAIRS, UP_CHUNK, D_MODEL) for w in (wdown0, wdown1)]
    fcw = [cw_all[:, k, :, :UP_CHUNK].reshape(2, N_UP_PAIRS, 3, UP_CHUNK) for k in range(DEPTH)]
    mcw = [cw_all[:, k, :, UP_CHUNK:].transpose(1, 0, 2).reshape(3, D_CONV) for k in range(DEPTH)]

    full = lambda i, j, k: (0, 0)

    saved = []
    xin = xf
    h1 = _rms_fwd(xin, norm1_g[0][None], "rms1_fwd_0")
    rows_of = lambda width: pl.BlockSpec((TM_ACC, width), lambda i: (i, 0))
    whole = lambda *shape: pl.BlockSpec(shape, lambda i: (0,) * len(shape))
    for l in range(DEPTH):
        proj = _matmul(
            h1, win[l], grid=(nt, D_IN // TN_IN, 1), dims=NN, name=f"proj_{l}",
            a_spec=pl.BlockSpec((TM, D_MODEL), lambda i, j, k: (i, 0)),
            b_spec=pl.BlockSpec((D_MODEL, TN_IN), lambda i, j, k: (0, j)),
            o_spec=pl.BlockSpec((TM, TN_IN), lambda i, j, k: (i, j)), o_shape=(t, D_IN), o_dtype=F32)
        o, lse, cat = _attn_fwd(proj, attn_out_g[l][None], nbatch, seq)
        cat = _convmix_fwd(proj, cat, mcw[l], conv_out_g[l][None], nbatch, seq)
        xmid, h2 = _matmul_norm(cat, wout[l], xin, norm2_g[l][None], dims=NN, name=f"mix_out_{l}",
                                a_spec=rows_of(D_MODEL), b_spec=whole(D_MODEL, D_MODEL))
        if l + 1 < DEPTH:
            xout, h_next, act, pre = _ffn_fwd(
                h2, wup[l], fcw[l], wdown[l], xmid, norm1_g[l + 1][None], seq, f"ffn_fwd_{l}")
        else:
            h_next = None
            xout, act, pre = _ffn_fwd(h2, wup[l], fcw[l], wdown[l], xmid, None, seq, f"ffn_fwd_{l}")
        saved.append((xin, h1, proj, o, lse, cat, xmid, h2, pre, act))
        xin, h1 = xout, h_next

    loss_part, dx, dxb, dgf = _loss_head(xin, final_norm_g[None], target, "loss_head")

    dg1, dg2, dga, dgc = [None] * DEPTH, [None] * DEPTH, [None] * DEPTH, [None] * DEPTH
    for l in reversed(range(DEPTH)):
        xin, h1, proj, o, lse, cat, xmid, h2, pre, act = saved[l]
        g_down = _matmul(
            act, dxb, grid=(N_UP_PAIRS, 1, 1), dims=TN, name=f"g_down_{l}",
            a_spec=pl.BlockSpec((None, t, UP_CHUNK), lambda i, j, k: (i, 0, 0)),
            b_spec=pl.BlockSpec((t, D_MODEL), full),
            o_spec=pl.BlockSpec((None, UP_CHUNK, D_MODEL), lambda i, j, k: (i, 0, 0)),
            o_shape=(N_UP_PAIRS, UP_CHUNK, D_MODEL), o_dtype=BF16).reshape(N_DEV, down_rows, D_MODEL)
        d_pre, d_fcw, dxm, dxmb, dg2[l] = _ffn_up_bwd(
            pre, dxb, fcw[l], wdown[l], wup[l], xmid, norm2_g[l][None], dx, seq, f"ffn_bwd_{l}")
        d_pre = d_pre.reshape(N_DEV, t, UP_CHUNK)
        g_up = _matmul(
            d_pre, h2, grid=(N_DEV, 1, 1), dims=TN, name=f"g_up_{l}",
            a_spec=pl.BlockSpec((None, t, UP_CHUNK), lambda i, j, k: (i, 0, 0)),
            b_spec=pl.BlockSpec((t, D_MODEL), full),
            o_spec=pl.BlockSpec((None, UP_CHUNK, D_MODEL), lambda i, j, k: (i, 0, 0)),
            o_shape=(N_DEV, UP_CHUNK, D_MODEL), o_dtype=BF16)
        g_out = _matmul(
            cat, dxmb, grid=(1, 1, nt), dims=TN, name=f"g_out_{l}",
            a_spec=pl.BlockSpec((TM, D_MODEL), lambda i, j, k: (k, 0)),
            b_spec=pl.BlockSpec((TM, D_MODEL), lambda i, j, k: (k, 0)),
            o_spec=pl.BlockSpec((D_MODEL, D_MODEL), full),
            o_shape=(D_MODEL, D_MODEL), o_dtype=BF16).reshape(N_DEV, out_rows, D_MODEL)
        if l == 0:
            land_out0, land_up0, land_down0 = _sequencer_exchange(
                [_Item(g_out, True), _Item(g_up, True), _Item(g_down, True)], "scatter_0a", 5)
        d_cat = _matmul(
            dxmb, wout[l], grid=(nta, 1, 1), dims=NT, name=f"d_cat_{l}",
            a_spec=pl.BlockSpec((TM_ACC, D_MODEL), lambda i, j, k: (i, 0)),
            b_spec=pl.BlockSpec((D_MODEL, D_MODEL), full),
            o_spec=pl.BlockSpec((TM_ACC, D_MODEL), lambda i, j, k: (i, 0)), o_shape=(t, D_MODEL), o_dtype=BF16)
        d_proj, dga[l] = _attn_bwd(proj, o, lse, d_cat, attn_out_g[l][None], nbatch, seq)
        d_proj, d_mcw, dgc[l] = _convmix_bwd(proj, d_cat, d_proj, mcw[l], conv_out_g[l][None], nbatch, seq)
        g_in = _matmul(
            h1, d_proj, grid=(1, D_IN // TN_IN, 1), dims=TN, name=f"g_in_{l}",
            a_spec=pl.BlockSpec((t, D_MODEL), full),
            b_spec=pl.BlockSpec((t, TN_IN), lambda i, j, k: (0, j)),
            o_spec=pl.BlockSpec((D_MODEL, TN_IN), lambda i, j, k: (0, j)),
            o_shape=(D_MODEL, D_IN), o_dtype=BF16)
        g_cw = jnp.concatenate(
            [d_fcw.reshape(N_DEV, 3, UP_CHUNK), d_mcw.reshape(3, N_DEV, D_CONV // N_DEV).transpose(1, 0, 2)], axis=-1)
        if l == 0:
            land_in0, land_cw0 = _sequencer_exchange([_Item(g_in, "cols"), _Item(g_cw, True)], "scatter_0b", 6)
        else:
            land_in1, land_out1, land_up1, land_down1, land_cw1 = _sequencer_exchange(
                [_Item(g_in, "cols"), _Item(g_out, True), _Item(g_up, True), _Item(g_down, True), _Item(g_cw, True)],
                "scatter_1", 4)
        dx, dxb, dg1[l] = _matmul_norm_bwd(
            d_proj, win[l], xin, norm1_g[l][None], dxm, dims=NT, name=f"d_h1_{l}",
            a_spec=rows_of(D_IN), b_spec=whole(D_MODEL, D_IN))

    def pack_small(n1, a, c, n2, f):
        return jnp.concatenate(
            [n1, n2, f[None], jnp.concatenate([a, c], axis=-1), jnp.zeros((1, D_MODEL), F32)], axis=0)[None]

    small = jnp.concatenate(
        [dg1[0], dg1[1], dg2[0], dg2[1], dgf,
         jnp.concatenate([dga[0], dgc[0]], axis=-1), jnp.concatenate([dga[1], dgc[1]], axis=-1),
         jnp.pad(loss_part, ((0, 0), (0, D_MODEL - LANES)))], axis=0)
    (land_small,) = _exchange([_Item(small, False)], "gather_gain_grads")
    res_small = _adamw(
        [land_small], pack_small(norm1_g, attn_out_g, conv_out_g, norm2_g, final_norm_g),
        pack_small(m_norm1_g, m_attn_out_g, m_conv_out_g, m_norm2_g, m_final_norm_g),
        pack_small(v_norm1_g, v_attn_out_g, v_conv_out_g, v_norm2_g, v_final_norm_g), SUBLANES, "adamw_gains")
    res_out = _adamw([land_out0, land_out1], w_out, m_w_out, v_w_out, out_rows, "adamw_w_out", after=[res_small[0]])
    res_up_t = _adamw([land_up0, land_up1], up_t, m_up_t, v_up_t, UP_CHUNK // 4, "adamw_ffn_up", after=[res_out[0]])
    res_up = [jnp.swapaxes(r, 1, 2) for r in res_up_t]
    res_down = _adamw([land_down0, land_down1], ffn_down, m_ffn_down, v_ffn_down, down_rows, "adamw_ffn_down",
                      after=[res_up_t[0]])
    res_in = _adamw([land_in0, land_in1], w_in, m_w_in, v_w_in, 256, "adamw_w_in", after=[res_down[0]])
    res_cw = _adamw(
        [land_cw0, land_cw1], cw_local, jnp.concatenate([m_ffn_conv_w, m_mix_conv_w], axis=-1),
        jnp.concatenate([v_ffn_conv_w, v_mix_conv_w], axis=-1), 3, "adamw_conv_w", after=[res_in[0]])

    loss = res_small[0][0, SUBLANES - 1, 0]

    def unpack(kind):
        s = res_small[kind][0]
        cwr = res_cw[kind]
        return (s[0:2], res_in[kind], cwr[..., UP_CHUNK:], s[5:7, :D_ATTN], s[5:7, D_ATTN:], res_out[kind],
                s[2:4], res_up[kind], cwr[..., :UP_CHUNK], res_down[kind], s[4])

    return (loss, dx.reshape(nbatch, seq, d), *unpack(0), *unpack(1), *unpack(2), *unpack(3))
```

```python
import math

import jax
import jax.numpy as jnp
from jax import lax
from jax.experimental import pallas as pl
from jax.experimental.pallas import tpu as pltpu
from jax.experimental.pallas import tpu_sc as plsc

F32 = jnp.float32
BF16 = jnp.bfloat16

D_MODEL = 1024
D_ATTN = 512
D_CONV = 512
HEAD_DIM = 64
N_HEADS = 8
D_FF = 2816
DEPTH = 2
D_IN = 3 * D_ATTN + 3 * D_CONV
EPS = 1e-6
DILATIONS = (1, 4, 16)
BAND = 128
N_DEV = 8
UP_CHUNK = 2 * D_FF // N_DEV
N_UP_PAIRS = N_DEV // 2
CW_PACK = UP_CHUNK + D_CONV // N_DEV
ADAM_LR = 0.001
ADAM_B1 = 0.9
ADAM_B2 = 0.999
ADAM_EPS = 1e-08
ADAM_WD = 0.01
ADAM_STEP = 10
LANES = 128
SUBLANES = 8
VMEM_LIMIT = 56 * 1024 * 1024

NEG = -1e30
MESH = pl.DeviceIdType.MESH


def _params(sem=None, vmem=VMEM_LIMIT):
    return pltpu.CompilerParams(dimension_semantics=sem, vmem_limit_bytes=vmem)


NN = (((1,), (0,)), ((), ()))
NT = (((1,), (1,)), ((), ()))
TN = (((0,), (0,)), ((), ()))
TN_PIECE = 1024


def _contract(a_ref, b_ref, dims):
    def dot(av, bv):
        return lax.dot_general(av.astype(BF16), bv.astype(BF16), dims, preferred_element_type=F32)

    if len(a_ref.shape) == 2:
        if dims == TN and a_ref.shape[0] > TN_PIECE:
            part = None
            for r0 in range(0, a_ref.shape[0], TN_PIECE):
                piece = dot(a_ref[pl.ds(r0, TN_PIECE), :], b_ref[pl.ds(r0, TN_PIECE), :])
                part = piece if part is None else part + piece
            return part
        return dot(a_ref[...], b_ref[...])
    part = dot(a_ref[0], b_ref[0])
    for c in range(1, a_ref.shape[0]):
        part = part + dot(a_ref[c], b_ref[c])
    return part


def _matmul(a, b, *, grid, a_spec, b_spec, o_spec, o_shape, o_dtype, dims, name, res=None, res_spec=None, after=()):
    nk = grid[2]
    o_block = tuple(s for s in o_spec.block_shape if s is not None)
    na = len(after)

    def body(*refs):
        refs = refs[:2 + (res is not None)] + refs[2 + (res is not None) + na:]
        if res is None:
            a_ref, b_ref, o_ref, *scr = refs
            r_ref = None
        else:
            a_ref, b_ref, r_ref, o_ref, *scr = refs
        part = _contract(a_ref, b_ref, dims)

        def finish(total):
            if r_ref is not None:
                total = total + r_ref[...]
            o_ref[...] = total.astype(o_dtype)

        if nk == 1:
            finish(part)
        else:
            acc = scr[0]
            k = pl.program_id(2)

            @pl.when(k == 0)
            def _():
                acc[...] = part

            @pl.when(k > 0)
            def _():
                acc[...] += part

            @pl.when(k == nk - 1)
            def _():
                finish(acc[...])

    in_specs = [a_spec, b_spec] + ([res_spec] if res is not None else []) + [pl.BlockSpec(memory_space=pl.ANY)] * na
    args = (a, b) + ((res,) if res is not None else ()) + tuple(after)
    return pl.pallas_call(
        body, name=name, grid=grid, in_specs=in_specs, out_specs=o_spec,
        out_shape=jax.ShapeDtypeStruct(o_shape, o_dtype),
        scratch_shapes=[pltpu.VMEM(o_block, F32)] if nk > 1 else [],
        compiler_params=_params(("parallel", "parallel", "arbitrary")),
    )(*args)


ROW_TILE = 512


def _rms_fwd(x, g, name):
    t, d = x.shape

    def body(x_ref, g_ref, h_ref):
        xv = x_ref[...]
        r = lax.rsqrt(jnp.mean(xv * xv, axis=-1, keepdims=True) + EPS)
        h_ref[...] = (xv * r * g_ref[...]).astype(BF16)

    return pl.pallas_call(
        body, name=name, grid=(t // ROW_TILE,),
        in_specs=[pl.BlockSpec((ROW_TILE, d), lambda i: (i, 0)), pl.BlockSpec((1, d), lambda i: (0, 0))],
        out_specs=pl.BlockSpec((ROW_TILE, d), lambda i: (i, 0)),
        out_shape=jax.ShapeDtypeStruct((t, d), BF16),
        compiler_params=_params(("parallel",)),
    )(x, g)


def _matmul_norm(a, b, res, g, *, a_spec, b_spec, dims, name):
    t, d = res.shape

    def body(a_ref, b_ref, r_ref, g_ref, x_ref, h_ref):
        xv = _contract(a_ref, b_ref, dims) + r_ref[...]
        x_ref[...] = xv
        h_ref[...] = (xv * lax.rsqrt(jnp.mean(xv * xv, axis=-1, keepdims=True) + EPS) * g_ref[...]).astype(BF16)

    row = pl.BlockSpec((TM_ACC, d), lambda i: (i, 0))
    return pl.pallas_call(
        body, name=name, grid=(t // TM_ACC,),
        in_specs=[a_spec, b_spec, row, pl.BlockSpec((1, d), lambda i: (0, 0))], out_specs=[row, row],
        out_shape=[jax.ShapeDtypeStruct((t, d), F32), jax.ShapeDtypeStruct((t, d), BF16)],
        compiler_params=_params(("parallel",)),
    )(a, b, res, g)


def _matmul_norm_bwd(a, b, x, g, dres, *, a_spec, b_spec, dims, name):
    t, d = x.shape

    def body(a_ref, b_ref, x_ref, g_ref, dres_ref, dx_ref, dxb_ref, dg_ref):
        dhv = _contract(a_ref, b_ref, dims)
        xv = x_ref[...]
        r = lax.rsqrt(jnp.mean(xv * xv, axis=-1, keepdims=True) + EPS)
        xh = xv * r
        gd = dhv * g_ref[...]
        dx = r * (gd - xh * jnp.mean(gd * xh, axis=-1, keepdims=True)) + dres_ref[...]
        dx_ref[...] = dx
        dxb_ref[...] = dx.astype(BF16)
        part = jnp.sum(dhv * xh, axis=0, keepdims=True)

        @pl.when(pl.program_id(0) == 0)
        def _():
            dg_ref[...] = part

        @pl.when(pl.program_id(0) > 0)
        def _():
            dg_ref[...] += part

    row = pl.BlockSpec((TM_ACC, d), lambda i: (i, 0))
    vec = pl.BlockSpec((1, d), lambda i: (0, 0))
    return pl.pallas_call(
        body, name=name, grid=(t // TM_ACC,),
        in_specs=[a_spec, b_spec, row, vec, row], out_specs=[row, row, vec],
        out_shape=[jax.ShapeDtypeStruct((t, d), F32), jax.ShapeDtypeStruct((t, d), BF16),
                   jax.ShapeDtypeStruct((1, d), F32)],
        compiler_params=_params(("arbitrary",)),
    )(a, b, x, g, dres)


def _loss_head(x, g, target, name):
    t, d = x.shape

    def body(x_ref, g_ref, t_ref, loss_ref, dx_ref, dxb_ref, dg_ref):
        xv = x_ref[...]
        r = lax.rsqrt(jnp.mean(xv * xv, axis=-1, keepdims=True) + EPS)
        xh = xv * r
        gv = g_ref[...]
        err = xh * gv - t_ref[...]
        loss = jnp.full((1, LANES), 0.5 / d, F32) * jnp.sum(err * err)
        dy = err * (1.0 / d)
        gd = dy * gv
        dx = r * (gd - xh * jnp.mean(gd * xh, axis=-1, keepdims=True))
        dx_ref[...] = dx
        dxb_ref[...] = dx.astype(BF16)
        part = jnp.sum(dy * xh, axis=0, keepdims=True)

        @pl.when(pl.program_id(0) == 0)
        def _():
            dg_ref[...] = part
            loss_ref[...] = loss

        @pl.when(pl.program_id(0) > 0)
        def _():
            dg_ref[...] += part
            loss_ref[...] += loss

    row = pl.BlockSpec((ROW_TILE, d), lambda i: (i, 0))
    vec = pl.BlockSpec((1, d), lambda i: (0, 0))
    return pl.pallas_call(
        body, name=name, grid=(t // ROW_TILE,),
        in_specs=[row, vec, row],
        out_specs=[pl.BlockSpec((1, LANES), lambda i: (0, 0)), row, row, vec],
        out_shape=[jax.ShapeDtypeStruct((1, LANES), F32), jax.ShapeDtypeStruct((t, d), F32),
                   jax.ShapeDtypeStruct((t, d), BF16), jax.ShapeDtypeStruct((1, d), F32)],
        compiler_params=_params(("arbitrary",)),
    )(x, g, target)


def _group_matrix(n):
    shift = int(math.log2(HEAD_DIM))
    r = lax.broadcasted_iota(jnp.int32, (n, n), 0) >> shift
    c = lax.broadcasted_iota(jnp.int32, (n, n), 1) >> shift
    return (r == c).astype(BF16)


def _group_sum(v, gmat):
    hi = v.astype(BF16)
    lo = (v - hi.astype(F32)).astype(BF16)

    def dot(p):
        return jnp.dot(p, gmat, preferred_element_type=F32)

    return dot(hi) + dot(lo)


def _shift_rows(ext, k):
    return pltpu.roll(ext, k % ext.shape[0], 0)


def _store_columns(stage, out_hbm, sems, row0, nrows, col_blocks):
    rows = pl.ds(pl.multiple_of(row0, SUBLANES * 2), nrows)
    copies = [
        pltpu.make_async_copy(stage.at[i], out_hbm.at[rows, pl.ds(pl.multiple_of(cb * LANES, LANES), LANES)], sems.at[i])
        for i, cb in enumerate(col_blocks)
    ]
    for cp in copies:
        cp.start()
    for cp in copies:
        cp.wait()


def _attn_consts(width):
    i = lax.broadcasted_iota(jnp.int32, (BAND, width), 0)
    j = lax.broadcasted_iota(jnp.int32, (BAND, width), 1)
    dist = (width - BAND) + i - j
    inwin = (dist >= 0) & (dist <= BAND)
    return dist.astype(F32), inwin, j


def _head_masks():
    lane = lax.broadcasted_iota(jnp.int32, (1, LANES), 1)
    return [(lane < HEAD_DIM).astype(F32), (lane >= HEAD_DIM).astype(F32)]


def _pair_bias(slope, dil):
    distf, inwin, _ = _attn_consts(2 * BAND)
    return jnp.concatenate([jnp.where(inwin, distf * (slope[hh] * (-float(dil))), NEG) for hh in range(2)], axis=0)


def _stack_heads(xv, hmask):
    return jnp.concatenate([xv * hmask[0], xv * hmask[1]], axis=0).astype(BF16)


FWD_UNROLL = 16
BWD_UNROLL = 16


def _unroll(trips, most):
    return max(u for u in range(1, most + 1) if trips % u == 0)


def _for_blocks(seq, dil, block, most):
    nb = seq // dil // BAND

    def residue(r, carry):
        base = r * nb
        block(pl.multiple_of(base * BAND, BAND), None)
        if nb > 1:
            def rest(n, c):
                block(pl.multiple_of((base + n) * BAND, BAND), pl.multiple_of((base + n - 1) * BAND, BAND))
                return c

            lax.fori_loop(1, nb, rest, 0, unroll=_unroll(nb - 1, most))
        return carry

    if dil == 1:
        residue(0, 0)
    else:
        lax.fori_loop(0, dil, residue, 0, unroll=_unroll(dil, max(1, most // nb)))


def _permute_in(src_ref, dst_ref, dil, seq):
    length = seq // dil
    for r in range(dil):
        dst_ref[pl.ds(r * length, length), :] = src_ref[pl.ds(r, length, stride=dil), :].astype(dst_ref.dtype)


def _slopes_table():
    slopes = 2.0 ** (-8.0 * jnp.arange(1, N_HEADS + 1, dtype=F32) / N_HEADS)
    return jnp.broadcast_to(slopes[:, None], (N_HEADS, 2 * BAND))


def _attn_fwd(proj, attn_g, nbatch, seq):
    t = nbatch * seq
    scale = HEAD_DIM ** -0.5

    def body(q_ref, k_ref, v_ref, g_ref, sl_ref, o_ref, lse_ref, cat_ref, pq, pk, pv, po, pm, pll, ao, am, al):
        hp = pl.program_id(1)
        hmask = _head_masks()
        slope = [sl_ref[pl.ds(2 * hp + hh, 1), :] for hh in range(2)]

        def run_branch(dil, qs, ks, vs, osink, msink, lsink):
            bias = _pair_bias(slope, dil)

            def block(row0, prow):
                cur = pl.ds(row0, BAND)
                q2 = _stack_heads(qs[cur, :] * scale, hmask)
                if prow is None:
                    kk, vv, bias_b = ks[cur, :], vs[cur, :], bias[:, BAND:]
                else:
                    prev = pl.ds(prow, BAND)
                    kk = jnp.concatenate([ks[prev, :], ks[cur, :]], axis=0)
                    vv = jnp.concatenate([vs[prev, :], vs[cur, :]], axis=0)
                    bias_b = bias
                s = lax.dot_general(q2, kk.astype(BF16), NT, preferred_element_type=F32) + bias_b
                m = jnp.max(s, axis=1, keepdims=True)
                p = jnp.exp(s - m)
                l = jnp.sum(p, axis=1, keepdims=True)
                pb = p.astype(BF16)
                o = jnp.dot(jnp.concatenate([pb[:BAND], pb[BAND:]], axis=1), _stack_heads(vv, hmask),
                            preferred_element_type=F32)
                osink[cur, :] = o
                msink[cur, :] = m[:BAND] * hmask[0] + m[BAND:] * hmask[1]
                lsink[cur, :] = l[:BAND] * hmask[0] + l[BAND:] * hmask[1]

            _for_blocks(seq, dil, block, FWD_UNROLL)

        run_branch(1, q_ref, k_ref, v_ref, ao, am, al)
        for dil in DILATIONS[1:]:
            length = seq // dil
            _permute_in(q_ref, pq, dil, seq)
            _permute_in(k_ref, pk, dil, seq)
            _permute_in(v_ref, pv, dil, seq)
            run_branch(dil, pq, pk, pv, po, pm, pll)
            for r in range(dil):
                nat = pl.ds(r, length, stride=dil)
                per = pl.ds(r * length, length)
                m0 = am[nat, :]
                mb = pm[per, :]
                mn = jnp.maximum(m0, mb)
                e0 = jnp.exp(m0 - mn)
                eb = jnp.exp(mb - mn)
                ao[nat, :] = ao[nat, :] * e0 + po[per, :] * eb
                al[nat, :] = al[nat, :] * e0 + pll[per, :] * eb
                am[nat, :] = mn

        gmat = _group_matrix(LANES)
        gv = g_ref[...]

        def fin(c, carry):
            rows = pl.ds(pl.multiple_of(c * 256, 256), 256)
            lv = al[rows, :]
            o = ao[rows, :] / lv
            o_ref[rows, :] = o
            lse_ref[rows, :] = am[rows, :] + jnp.log(lv)
            ms = _group_sum(o * o, gmat) * (1.0 / HEAD_DIM)
            cat_ref[rows, :] = (o * lax.rsqrt(ms + EPS) * gv).astype(BF16)
            return carry

        lax.fori_loop(0, seq // 256, fin, 0, unroll=True)

    nq = D_ATTN // LANES
    blk = lambda off: pl.BlockSpec((seq, LANES), lambda b, h: (b, h + off))
    scratch = [pltpu.VMEM((seq, LANES), F32) for _ in range(9)]
    return pl.pallas_call(
        body, name="attn_fwd", grid=(nbatch, nq),
        in_specs=[blk(0), blk(nq), blk(2 * nq), pl.BlockSpec((1, LANES), lambda b, h: (0, h)),
                  pl.BlockSpec((N_HEADS, 2 * BAND), lambda b, h: (0, 0))],
        out_specs=[blk(0), blk(0), blk(0)],
        out_shape=[jax.ShapeDtypeStruct((t, D_ATTN), F32), jax.ShapeDtypeStruct((t, D_ATTN), F32),
                   jax.ShapeDtypeStruct((t, D_MODEL), BF16)],
        scratch_shapes=scratch,
        compiler_params=_params(("parallel", "parallel")),
    )(proj, proj, proj, attn_g, _slopes_table())


def _attn_bwd(proj, o, lse, d_cat, attn_g, nbatch, seq):
    t = nbatch * seq
    scale = HEAD_DIM ** -0.5

    def body(q_ref, k_ref, v_ref, o_ref, lse_ref, dy_ref, g_ref, sl_ref, dproj_ref, dg_ref,
             do_n, dl_n, dq_n, dk_n, dv_n, pq, pk, pv, pdo, plse, pdl, pdq, pdk, pdv, stage, sems):
        hp = pl.program_id(0)
        hmask = _head_masks()
        slope = [sl_ref[pl.ds(2 * hp + hh, 1), :] for hh in range(2)]
        gmat = _group_matrix(LANES)
        gv = g_ref[...]

        def prep(c, dg):
            rows = pl.ds(pl.multiple_of(c * 256, 256), 256)
            ov = o_ref[rows, :]
            dyn = dy_ref[rows, :].astype(F32)
            r = lax.rsqrt(_group_sum(ov * ov, gmat) * (1.0 / HEAD_DIM) + EPS)
            gd = dyn * gv
            oh = ov * r
            do = r * (gd - oh * (_group_sum(gd * oh, gmat) * (1.0 / HEAD_DIM)))
            do_n[rows, :] = do
            dl_n[rows, :] = _group_sum(do * ov, gmat)
            return dg + jnp.sum(dyn * oh, axis=0, keepdims=True)

        dg = lax.fori_loop(0, seq // 256, prep, jnp.zeros((1, LANES), F32), unroll=True)

        @pl.when(pl.program_id(1) == 0)
        def _():
            dg_ref[...] = dg

        @pl.when(pl.program_id(1) > 0)
        def _():
            dg_ref[...] += dg

        def clear(*refs):
            def step(c, carry):
                rows = pl.ds(pl.multiple_of(c * 256, 256), 256)
                for ref in refs:
                    ref[rows, :] = jnp.zeros((256, LANES), F32)
                return carry

            lax.fori_loop(0, seq // 256, step, 0)

        clear(dq_n, dk_n, dv_n)

        def run_branch(dil, qs, ks, vs, dos, lses, dls, dqs, dks, dvs):
            bias = _pair_bias(slope, dil)

            def per_head(xv):
                return jnp.concatenate([xv[:, 0:1], xv[:, HEAD_DIM:HEAD_DIM + 1]], axis=0)

            def block(row0, prow):
                cur = pl.ds(row0, BAND)
                keys = cur if prow is None else pl.ds(prow, 2 * BAND)
                q2 = _stack_heads(qs[cur, :] * scale, hmask)
                do2 = _stack_heads(dos[cur, :], hmask)
                kk, vv = ks[keys, :], vs[keys, :]
                s = lax.dot_general(q2, kk.astype(BF16), NT, preferred_element_type=F32)
                s = s + (bias[:, BAND:] if prow is None else bias)
                p = jnp.exp(s - per_head(lses[cur, :]))
                dp = lax.dot_general(do2, vv.astype(BF16), NT, preferred_element_type=F32)
                ds = (p * (dp - per_head(dls[cur, :]))).astype(BF16)
                dqs[cur, :] += jnp.dot(jnp.concatenate([ds[:BAND], ds[BAND:]], axis=1), _stack_heads(kk, hmask),
                                       preferred_element_type=F32)
                dks[keys, :] += lax.dot_general(ds, q2, TN, preferred_element_type=F32)
                dvs[keys, :] += lax.dot_general(p.astype(BF16), do2, TN, preferred_element_type=F32)

            _for_blocks(seq, dil, block, BWD_UNROLL)

        run_branch(1, q_ref, k_ref, v_ref, do_n, lse_ref, dl_n, dq_n, dk_n, dv_n)
        for dil in DILATIONS[1:]:
            length = seq // dil
            for src, dst in ((q_ref, pq), (k_ref, pk), (v_ref, pv), (do_n, pdo), (lse_ref, plse), (dl_n, pdl)):
                _permute_in(src, dst, dil, seq)
            clear(pdq, pdk, pdv)
            run_branch(dil, pq, pk, pv, pdo, plse, pdl, pdq, pdk, pdv)
            for r in range(dil):
                nat = pl.ds(r, length, stride=dil)
                per = pl.ds(r * length, length)
                dq_n[nat, :] += pdq[per, :]
                dk_n[nat, :] += pdk[per, :]
                dv_n[nat, :] += pdv[per, :]

        def emit(c, carry):
            rows = pl.ds(pl.multiple_of(c * 256, 256), 256)
            stage[0, rows, :] = (dq_n[rows, :] * scale).astype(BF16)
            stage[1, rows, :] = dk_n[rows, :].astype(BF16)
            stage[2, rows, :] = dv_n[rows, :].astype(BF16)
            return carry

        lax.fori_loop(0, seq // 256, emit, 0)
        _store_columns(stage, dproj_ref, sems, pl.program_id(1) * seq, seq, [hp, nq + hp, 2 * nq + hp])

    nq = D_ATTN // LANES
    blk = lambda off: pl.BlockSpec((seq, LANES), lambda h, b: (b, h + off))
    vec = pl.BlockSpec((1, LANES), lambda h, b: (0, h))
    scratch = [pltpu.VMEM((seq, LANES), F32) for _ in range(14)]
    scratch += [pltpu.VMEM((3, seq, LANES), BF16), pltpu.SemaphoreType.DMA((3,))]
    d_proj, dg = pl.pallas_call(
        body, name="attn_bwd", grid=(nq, nbatch),
        in_specs=[blk(0), blk(nq), blk(2 * nq), blk(0), blk(0), blk(0), vec,
                  pl.BlockSpec((N_HEADS, 2 * BAND), lambda h, b: (0, 0))],
        out_specs=[pl.BlockSpec(memory_space=pl.ANY), vec],
        out_shape=[jax.ShapeDtypeStruct((t, D_IN), BF16), jax.ShapeDtypeStruct((1, D_ATTN), F32)],
        scratch_shapes=scratch,
        compiler_params=_params(("arbitrary", "arbitrary")),
    )(proj, proj, proj, o, lse, d_cat, attn_g, _slopes_table())
    return d_proj, dg


HALO = SUBLANES
PACKED_ROWS = 2 * SUBLANES


def _window(ref, c, rows, nchunks, after):
    row0 = pl.multiple_of(c * rows, rows)
    prev0 = pl.multiple_of(jnp.maximum(row0 - PACKED_ROWS, 0), PACKED_ROWS)
    before = ref[pl.ds(prev0, PACKED_ROWS), :].astype(F32)[PACKED_ROWS - HALO:] * (c > 0).astype(F32)
    parts = [before, ref[pl.ds(row0, rows), :].astype(F32)]
    if after:
        next0 = pl.multiple_of(jnp.minimum(row0 + rows, (nchunks - 1) * rows), PACKED_ROWS)
        parts.append(ref[pl.ds(next0, PACKED_ROWS), :].astype(F32)[:HALO] * (c < nchunks - 1).astype(F32))
    return jnp.concatenate(parts, axis=0)


def _behind(z):
    z1 = _shift_rows(z, 1)
    return z1, _shift_rows(z1, 1)


def _ahead(dy):
    d1 = _shift_rows(dy, -1)
    return d1, _shift_rows(d1, -1)


def _conv(z, w):
    z1, z2 = _behind(z)
    return w[0:1] * z2 + w[1:2] * z1 + w[2:3] * z


def _conv_bwd(dy, z, w, cur):
    d1, d2 = _ahead(dy)
    dz = w[2:3] * dy + w[1:2] * d1 + w[0:1] * d2
    return dz, [jnp.sum((d * z)[cur], axis=0, keepdims=True) for d in (d2, d1, dy)]


def _sigmoid(a):
    return 0.5 * jnp.tanh(0.5 * a) + 0.5


MIX_ROWS = 256
GATE_B_BLOCK = 3 * D_ATTN // LANES
GATE_C_BLOCK = GATE_B_BLOCK + D_CONV // LANES
U_BLOCK = GATE_C_BLOCK + D_CONV // LANES


def _convmix_fwd(proj, cat, mcw, conv_g, nbatch, seq):
    nchunks = seq // MIX_ROWS

    def body(gb_ref, gc_ref, u_ref, w_ref, g_ref, cat_in, cat_ref):
        del cat_in
        gmat = _group_matrix(LANES)
        w = w_ref[...]
        gv = g_ref[...]

        def step(c, carry):
            cur = pl.ds(pl.multiple_of(c * MIX_ROWS, MIX_ROWS), MIX_ROWS)
            z = _window(gc_ref, c, MIX_ROWS, nchunks, False) * _window(u_ref, c, MIX_ROWS, nchunks, False)
            y = gb_ref[cur, :] * _conv(z, w)[HALO:]
            ms = _group_sum(y * y, gmat) * (1.0 / HEAD_DIM)
            cat_ref[cur, :] = (y * lax.rsqrt(ms + EPS) * gv).astype(BF16)
            return carry

        lax.fori_loop(0, nchunks, step, 0, unroll=True)

    nc = D_CONV // LANES
    blk = lambda off: pl.BlockSpec((seq, LANES), lambda b, j: (b, j + off))
    return pl.pallas_call(
        body, name="convmix_fwd", grid=(nbatch, nc),
        in_specs=[blk(GATE_B_BLOCK), blk(GATE_C_BLOCK), blk(U_BLOCK),
                  pl.BlockSpec((3, LANES), lambda b, j: (0, j)), pl.BlockSpec((1, LANES), lambda b, j: (0, j)),
                  pl.BlockSpec(memory_space=pl.ANY)],
        out_specs=blk(D_ATTN // LANES),
        out_shape=jax.ShapeDtypeStruct(cat.shape, cat.dtype),
        input_output_aliases={5: 0},
        compiler_params=_params(("parallel", "parallel")),
    )(proj, proj, proj, mcw, conv_g, cat)


def _convmix_bwd(proj, d_cat, d_proj, mcw, conv_g, nbatch, seq):
    nchunks = seq // MIX_ROWS

    def body(gb_ref, gc_ref, u_ref, dy_ref, w_ref, g_ref, dproj_in, dproj_ref, dw_ref, dg_ref, stage, sems):
        del dproj_in
        cb = pl.program_id(0)
        b = pl.program_id(1)
        gmat = _group_matrix(LANES)
        w = w_ref[...]
        gv = g_ref[...]
        cur = slice(HALO, HALO + MIX_ROWS)

        def step(c, carry):
            rows = pl.ds(pl.multiple_of(c * MIX_ROWS, MIX_ROWS), MIX_ROWS)
            gb = _window(gb_ref, c, MIX_ROWS, nchunks, True)
            gc = _window(gc_ref, c, MIX_ROWS, nchunks, True)
            u = _window(u_ref, c, MIX_ROWS, nchunks, True)
            dyn = _window(dy_ref, c, MIX_ROWS, nchunks, True)
            z = gc * u
            conv = _conv(z, w)
            y = gb * conv
            r = lax.rsqrt(_group_sum(y * y, gmat) * (1.0 / HEAD_DIM) + EPS)
            yh = y * r
            gd = dyn * gv
            dy = r * (gd - yh * (_group_sum(gd * yh, gmat) * (1.0 / HEAD_DIM)))
            dz, dws = _conv_bwd(dy * gb, z, w, cur)
            stage[0, rows, :] = (dy * conv)[cur].astype(BF16)
            stage[1, rows, :] = (dz * u)[cur].astype(BF16)
            stage[2, rows, :] = (dz * gc)[cur].astype(BF16)
            dg = jnp.sum((dyn * yh)[cur], axis=0, keepdims=True)
            return tuple(a + d for a, d in zip(carry, dws + [dg]))

        zero = jnp.zeros((1, LANES), F32)
        dw0, dw1, dw2, dg = lax.fori_loop(0, nchunks, step, (zero, zero, zero, zero), unroll=True)

        @pl.when(b == 0)
        def _():
            dw_ref[0:1, :] = dw0
            dw_ref[1:2, :] = dw1
            dw_ref[2:3, :] = dw2
            dg_ref[...] = dg

        @pl.when(b > 0)
        def _():
            dw_ref[0:1, :] += dw0
            dw_ref[1:2, :] += dw1
            dw_ref[2:3, :] += dw2
            dg_ref[...] += dg

        _store_columns(stage, dproj_ref, sems, b * seq, seq, [GATE_B_BLOCK + cb, GATE_C_BLOCK + cb, U_BLOCK + cb])

    nc = D_CONV // LANES
    blk = lambda off: pl.BlockSpec((seq, LANES), lambda j, b: (b, j + off))
    return pl.pallas_call(
        body, name="convmix_bwd", grid=(nc, nbatch),
        in_specs=[blk(GATE_B_BLOCK), blk(GATE_C_BLOCK), blk(U_BLOCK), blk(D_ATTN // LANES),
                  pl.BlockSpec((3, LANES), lambda j, b: (0, j)), pl.BlockSpec((1, LANES), lambda j, b: (0, j)),
                  pl.BlockSpec(memory_space=pl.ANY)],
        out_specs=[pl.BlockSpec(memory_space=pl.ANY), pl.BlockSpec((3, LANES), lambda j, b: (0, j)),
                   pl.BlockSpec((1, LANES), lambda j, b: (0, j))],
        out_shape=[jax.ShapeDtypeStruct(d_proj.shape, d_proj.dtype), jax.ShapeDtypeStruct((3, D_CONV), F32),
                   jax.ShapeDtypeStruct((1, D_CONV), F32)],
        scratch_shapes=[pltpu.VMEM((3, seq, LANES), BF16), pltpu.SemaphoreType.DMA((3,))],
        input_output_aliases={6: 0},
        compiler_params=_params(("arbitrary", "arbitrary")),
    )(proj, proj, proj, d_cat, mcw, conv_g, d_proj)


FFN_ROWS = 256
FFN_FWD_ROWS = 256


def _ffn_fwd(h, wup, fcw, wdown, res, g, seq, name):
    t, d = res.shape
    tiles_per_seq = seq // FFN_FWD_ROWS

    def body(hm_ref, hp_ref, wu_ref, w_ref, wd_ref, r_ref, *rest):
        if g is None:
            x_ref, act_ref, pre_ref = rest
        else:
            g_ref, x_ref, h_ref, act_ref, pre_ref = rest
        inside = ((pl.program_id(0) % tiles_per_seq) > 0).astype(BF16)
        rows = jnp.concatenate([hp_ref[...] * inside, hm_ref[...]], axis=0)

        def up(j, part, p):
            full = lax.dot_general(rows, wu_ref[j], NT, preferred_element_type=F32).astype(BF16)
            pre_ref[part, p] = full[PACKED_ROWS:]
            return full.astype(F32)[PACKED_ROWS - HALO:]

        total = r_ref[...]
        for p in range(N_UP_PAIRS):
            a = _conv(up(p, 0, p), w_ref[0, p])[HALO:]
            v = _conv(up(N_UP_PAIRS + p, 1, p), w_ref[1, p])[HALO:]
            act = (a * _sigmoid(a) * v).astype(BF16)
            act_ref[p] = act
            total = total + jnp.dot(act, wd_ref[p], preferred_element_type=F32)
        x_ref[...] = total
        if g is not None:
            h_ref[...] = (total * lax.rsqrt(jnp.mean(total * total, axis=-1, keepdims=True) + EPS) * g_ref[...]).astype(BF16)

    row = pl.BlockSpec((FFN_FWD_ROWS, d), lambda i: (i, 0))
    tiles_per_halo = FFN_FWD_ROWS // PACKED_ROWS
    in_specs = [
        row, pl.BlockSpec((PACKED_ROWS, d), lambda i: (jnp.maximum(i * tiles_per_halo - 1, 0), 0)),
        pl.BlockSpec((N_DEV, UP_CHUNK, d), lambda i: (0, 0, 0)),
        pl.BlockSpec((2, N_UP_PAIRS, 3, UP_CHUNK), lambda i: (0, 0, 0, 0)),
        pl.BlockSpec((N_UP_PAIRS, UP_CHUNK, d), lambda i: (0, 0, 0)), row]
    out_specs = [row]
    out_shape = [jax.ShapeDtypeStruct((t, d), F32)]
    args = [h, h, wup, fcw, wdown, res]
    if g is not None:
        in_specs.append(pl.BlockSpec((1, d), lambda i: (0, 0)))
        out_specs.append(row)
        out_shape.append(jax.ShapeDtypeStruct((t, d), BF16))
        args.append(g)
    out_specs += [pl.BlockSpec((N_UP_PAIRS, FFN_FWD_ROWS, UP_CHUNK), lambda i: (0, i, 0)),
                  pl.BlockSpec((2, N_UP_PAIRS, FFN_FWD_ROWS, UP_CHUNK), lambda i: (0, 0, i, 0))]
    out_shape += [jax.ShapeDtypeStruct((N_UP_PAIRS, t, UP_CHUNK), BF16),
                  jax.ShapeDtypeStruct((2, N_UP_PAIRS, t, UP_CHUNK), BF16)]
    return pl.pallas_call(
        body, name=name, grid=(t // FFN_FWD_ROWS,), in_specs=in_specs, out_specs=out_specs, out_shape=out_shape,
        compiler_params=_params(("parallel",)),
    )(*args)


def _ffn_up_bwd(pre, dy, fcw, wdown, wup, x, g, dres, seq, name):
    t, d = x.shape
    tiles_per_seq = seq // FFN_ROWS
    tiles_per_halo = FFN_ROWS // PACKED_ROWS
    last_halo = t // PACKED_ROWS - 1

    def body(pm_ref, pp_ref, pn_ref, dm_ref, dp_ref, dn_ref, w_ref, wd_ref, wu_ref, x_ref, g_ref, dres_ref,
             dpre_ref, dw_ref, dx_ref, dxb_ref, dg_ref):
        i = pl.program_id(0)
        has_prev = ((i % tiles_per_seq) > 0).astype(F32)
        has_next = ((i % tiles_per_seq) < tiles_per_seq - 1).astype(F32)
        cur = slice(HALO, HALO + FFN_ROWS)

        def window(before, main, after):
            return jnp.concatenate([before.astype(F32)[PACKED_ROWS - HALO:] * has_prev, main.astype(F32),
                                    after.astype(F32)[:HALO] * has_next], axis=0)

        dy_rows = jnp.concatenate([dp_ref[...] * has_prev.astype(BF16), dm_ref[...],
                                   dn_ref[...] * has_next.astype(BF16)], axis=0)

        dh = jnp.zeros((FFN_ROWS, d), F32)
        sums = []
        for p in range(N_UP_PAIRS):
            pg = window(pp_ref[0, p], pm_ref[0, p], pn_ref[0, p])
            pv = window(pp_ref[1, p], pm_ref[1, p], pn_ref[1, p])
            dact = lax.dot_general(dy_rows, wd_ref[p], NT, preferred_element_type=F32)
            dact = dact[PACKED_ROWS - HALO:PACKED_ROWS + FFN_ROWS + HALO]
            a = _conv(pg, w_ref[0, p])
            v = _conv(pv, w_ref[1, p])
            sg = _sigmoid(a)
            asg = a * sg
            dzg, dwg = _conv_bwd(dact * v * (sg + asg - asg * sg), pg, w_ref[0, p], cur)
            dzv, dwv = _conv_bwd(dact * asg, pv, w_ref[1, p], cur)
            dgate = dzg[cur].astype(BF16)
            dval = dzv[cur].astype(BF16)
            dpre_ref[0, p] = dgate
            dpre_ref[1, p] = dval
            dh = dh + jnp.dot(dgate, wu_ref[p], preferred_element_type=F32)
            dh = dh + jnp.dot(dval, wu_ref[N_UP_PAIRS + p], preferred_element_type=F32)
            sums.append(dwg + dwv)

        xv = x_ref[...]
        r = lax.rsqrt(jnp.mean(xv * xv, axis=-1, keepdims=True) + EPS)
        xh = xv * r
        gd = dh * g_ref[...]
        dx = r * (gd - xh * jnp.mean(gd * xh, axis=-1, keepdims=True)) + dres_ref[...]
        dx_ref[...] = dx
        dxb_ref[...] = dx.astype(BF16)
        part = jnp.sum(dh * xh, axis=0, keepdims=True)

        @pl.when(i == 0)
        def _():
            dg_ref[...] = part
            for p in range(N_UP_PAIRS):
                for k in range(6):
                    dw_ref[k // 3, p, pl.ds(k % 3, 1), :] = sums[p][k]

        @pl.when(i > 0)
        def _():
            dg_ref[...] += part
            for p in range(N_UP_PAIRS):
                for k in range(6):
                    dw_ref[k // 3, p, pl.ds(k % 3, 1), :] += sums[p][k]

    def rows4(n):
        return lambda fn: pl.BlockSpec((2, N_UP_PAIRS, n, UP_CHUNK), lambda i: (0, 0, fn(i), 0))

    def rows2(n):
        return lambda fn: pl.BlockSpec((n, d), lambda i: (fn(i), 0))

    prev_tile = lambda i: jnp.maximum(i * tiles_per_halo - 1, 0)
    next_tile = lambda i: jnp.minimum((i + 1) * tiles_per_halo, last_halo)
    row = pl.BlockSpec((FFN_ROWS, d), lambda i: (i, 0))
    vec = pl.BlockSpec((1, d), lambda i: (0, 0))
    wspec = pl.BlockSpec((2, N_UP_PAIRS, 3, UP_CHUNK), lambda i: (0, 0, 0, 0))
    return pl.pallas_call(
        body, name=name, grid=(t // FFN_ROWS,),
        in_specs=[rows4(FFN_ROWS)(lambda i: i), rows4(PACKED_ROWS)(prev_tile), rows4(PACKED_ROWS)(next_tile),
                  rows2(FFN_ROWS)(lambda i: i), rows2(PACKED_ROWS)(prev_tile), rows2(PACKED_ROWS)(next_tile),
                  wspec, pl.BlockSpec((N_UP_PAIRS, UP_CHUNK, d), lambda i: (0, 0, 0)),
                  pl.BlockSpec((N_DEV, UP_CHUNK, d), lambda i: (0, 0, 0)), row, vec, row],
        out_specs=[rows4(FFN_ROWS)(lambda i: i), wspec, row, row, vec],
        out_shape=[jax.ShapeDtypeStruct(pre.shape, BF16), jax.ShapeDtypeStruct(fcw.shape, F32),
                   jax.ShapeDtypeStruct((t, d), F32), jax.ShapeDtypeStruct((t, d), BF16),
                   jax.ShapeDtypeStruct((1, d), F32)],
        compiler_params=_params(("arbitrary",)),
    )(pre, pre, pre, dy, dy, dy, fcw, wdown, wup, x, g, dres)


def _adamw(lands, w, m, v, row_tile, name, after=()):
    nl = len(lands)
    _, nr, ncol = lands[0].shape
    c1 = 1.0 - ADAM_B1 ** ADAM_STEP
    c2 = 1.0 - ADAM_B2 ** ADAM_STEP

    def body(*refs):
        land_refs = refs[:nl]
        w_ref, m_ref, v_ref = refs[nl:nl + 3]
        g_ref, d_ref, mo_ref, vo_ref = refs[nl + 3 + len(after):]
        for l in range(nl):
            @pl.when(pl.program_id(0) == l)
            def _(l=l):
                g = land_refs[l][0].astype(F32)
                for j in range(1, N_DEV):
                    g = g + land_refs[l][j].astype(F32)
                g_ref[...] = g

        g = g_ref[...]
        m2 = ADAM_B1 * m_ref[...] + (1.0 - ADAM_B1) * g
        v2 = ADAM_B2 * v_ref[...] + (1.0 - ADAM_B2) * (g * g)
        mo_ref[...] = m2
        vo_ref[...] = v2
        d_ref[...] = -ADAM_LR * ((m2 / c1) / (jnp.sqrt(v2 / c2) + ADAM_EPS) + ADAM_WD * w_ref[...])

    def land_spec(l):
        return pl.BlockSpec((N_DEV, row_tile, ncol), lambda k, i: (0, jnp.where(k == l, i, 0), 0))

    tile = pl.BlockSpec((None, row_tile, ncol), lambda k, i: (k, i, 0))
    return pl.pallas_call(
        body, name=name, grid=(nl, nr // row_tile),
        in_specs=[land_spec(l) for l in range(nl)] + [tile, tile, tile] + [pl.BlockSpec(memory_space=pl.ANY)] * len(after),
        out_specs=[tile] * 4,
        out_shape=[jax.ShapeDtypeStruct(w.shape, F32)] * 4,
        compiler_params=_params(("arbitrary", "arbitrary")),
    )(*lands, w, m, v, *after)


class _Item:
    def __init__(self, src, chunked, land_cols=False):
        self.src, self.chunked, self.land_cols = src, chunked, land_cols
        if chunked == "cols":
            block = (src.shape[0], src.shape[1] // N_DEV)
        else:
            block = src.shape[1:] if chunked else src.shape
        self.width = block[-1]
        self.land_shape = (block[0], N_DEV * block[1]) if land_cols else (N_DEV,) + block

    def _cols(self, first, count=1):
        return pl.ds(pl.multiple_of(first * self.width, LANES), count * self.width)

    def part(self, src_ref, j):
        if self.chunked == "cols":
            return src_ref.at[:, self._cols(j)]
        return src_ref.at[j] if self.chunked else src_ref

    def slot(self, land_ref, s):
        return land_ref.at[:, self._cols(s)] if self.land_cols else land_ref.at[s]


def _mesh_place():
    x, y, c = lax.axis_index("x"), lax.axis_index("y"), lax.axis_index("c")
    return x, y, c, 4 * x + 2 * y + c


def _flipped(x, y, c, k):
    px = 1 - x if k & 4 else x
    py = 1 - y if k & 2 else y
    pc = 1 - c if k & 1 else c
    return (px, py, pc), 4 * px + 2 * py + pc


PEER_ORDER = (2, 4, 6, 3, 5, 7, 1)


def _exchange(items, name):
    n = len(items)

    def body(*refs):
        srcs, lands = refs[:n], refs[n:2 * n]
        send, recv, local = refs[2 * n:]
        x, y, c, me = _mesh_place()

        def copy(i, k, chunk, slot, dev):
            return pltpu.make_async_remote_copy(
                src_ref=items[i].part(srcs[i], chunk), dst_ref=items[i].slot(lands[i], slot),
                send_sem=send.at[i, k - 1], recv_sem=recv.at[i, k - 1], device_id=dev, device_id_type=MESH)

        own = [pltpu.make_async_copy(items[i].part(srcs[i], me), items[i].slot(lands[i], me), local.at[i])
               for i in range(n)]
        for k in PEER_ORDER:
            dev, idx = _flipped(x, y, c, k)
            for i in range(n):
                copy(i, k, idx, me, dev).start()
        for cp in own:
            cp.start()
        for k in PEER_ORDER:
            dev, idx = _flipped(x, y, c, k)
            for i in range(n):
                copy(i, k, me, idx, dev).wait_recv()
        for k in PEER_ORDER:
            dev, idx = _flipped(x, y, c, k)
            for i in range(n):
                copy(i, k, idx, me, dev).wait_send()
        for cp in own:
            cp.wait()

    hbm = pl.BlockSpec(memory_space=pl.ANY)
    return pl.pallas_call(
        body, name=name,
        in_specs=[hbm] * n, out_specs=[hbm] * n,
        out_shape=[jax.ShapeDtypeStruct(it.land_shape, it.src.dtype) for it in items],
        scratch_shapes=[pltpu.SemaphoreType.DMA((n, N_DEV - 1)), pltpu.SemaphoreType.DMA((n, N_DEV - 1)),
                        pltpu.SemaphoreType.DMA((n,))],
        compiler_params=pltpu.CompilerParams(has_side_effects=True),
    )(*[it.src for it in items])


SAME_CORE = (2, 4, 6)


def _sequencer_gather(items, name, collective_id):
    n = len(items)

    def body(*refs):
        srcs, lands = refs[:n], refs[n:2 * n]
        send, recv, local = refs[2 * n:]
        x, y, c, me = _mesh_place()
        sibling, _ = _flipped(x, y, c, 1)
        barrier = pltpu.get_barrier_semaphore()
        for k in SAME_CORE + (1,):
            pl.semaphore_signal(barrier, inc=1, device_id=_flipped(x, y, c, k)[0], device_id_type=MESH)
        pl.semaphore_wait(barrier, len(SAME_CORE) + 1)

        def copy(i, q, src, slot, dev):
            return pltpu.make_async_remote_copy(
                src_ref=src, dst_ref=items[i].slot(lands[i], slot),
                send_sem=send.at[i, q - 1], recv_sem=recv.at[i, q - 1], device_id=dev, device_id_type=MESH)

        own = [pltpu.make_async_copy(srcs[i], items[i].slot(lands[i], me), local.at[i]) for i in range(n)]
        for cp in own:
            cp.start()
        for k in SAME_CORE + (1,):
            for i in range(n):
                copy(i, k, srcs[i], me, _flipped(x, y, c, k)[0]).start()
        for k in SAME_CORE:
            dev, idx = _flipped(x, y, c, k)
            for i in range(n):
                copy(i, k, srcs[i], idx, dev).wait_recv()
            for i in range(n):
                copy(i, k + 1, items[i].slot(lands[i], idx), idx, sibling).start()
        for k in (1,) + tuple(k + 1 for k in SAME_CORE):
            _, idx = _flipped(x, y, c, k)
            for i in range(n):
                copy(i, k, srcs[i], idx, sibling).wait_recv()
        for k in range(1, N_DEV):
            for i in range(n):
                copy(i, k, srcs[i], me, sibling).wait_send()
        for cp in own:
            cp.wait()

    return pl.kernel(
        body, name=name,
        out_type=[jax.ShapeDtypeStruct(it.land_shape, it.src.dtype) for it in items],
        mesh=plsc.ScalarSubcoreMesh(axis_name="sequencer", num_cores=1),
        scratch_types=[pltpu.SemaphoreType.DMA((n, N_DEV - 1)), pltpu.SemaphoreType.DMA((n, N_DEV - 1)),
                       pltpu.SemaphoreType.DMA((n,))],
        compiler_params=pltpu.CompilerParams(collective_id=collective_id),
    )(*[it.src for it in items])


def _sequencer_exchange(items, name, collective_id):
    n = len(items)

    def body(*refs):
        srcs, lands = refs[:n], refs[n:2 * n]
        send, recv, local = refs[2 * n:]
        x, y, c, me = _mesh_place()
        barrier = pltpu.get_barrier_semaphore()
        for k in PEER_ORDER:
            pl.semaphore_signal(barrier, inc=1, device_id=_flipped(x, y, c, k)[0], device_id_type=MESH)
        pl.semaphore_wait(barrier, N_DEV - 1)

        def copy(i, k, chunk, slot, dev):
            return pltpu.make_async_remote_copy(
                src_ref=items[i].part(srcs[i], chunk), dst_ref=items[i].slot(lands[i], slot),
                send_sem=send.at[i, k - 1], recv_sem=recv.at[i, k - 1], device_id=dev, device_id_type=MESH)

        own = [pltpu.make_async_copy(items[i].part(srcs[i], me), items[i].slot(lands[i], me), local.at[i])
               for i in range(n)]
        for cp in own:
            cp.start()
        for k in PEER_ORDER:
            dev, idx = _flipped(x, y, c, k)
            for i in range(n):
                copy(i, k, idx, me, dev).start()
        for k in PEER_ORDER:
            dev, idx = _flipped(x, y, c, k)
            for i in range(n):
                copy(i, k, me, idx, dev).wait_recv()
        for k in PEER_ORDER:
            dev, idx = _flipped(x, y, c, k)
            for i in range(n):
                copy(i, k, idx, me, dev).wait_send()
        for cp in own:
            cp.wait()

    return pl.kernel(
        body, name=name,
        out_type=[jax.ShapeDtypeStruct(it.land_shape, it.src.dtype) for it in items],
        mesh=plsc.ScalarSubcoreMesh(axis_name="sequencer", num_cores=1),
        scratch_types=[pltpu.SemaphoreType.DMA((n, N_DEV - 1)), pltpu.SemaphoreType.DMA((n, N_DEV - 1)),
                       pltpu.SemaphoreType.DMA((n,))],
        compiler_params=pltpu.CompilerParams(collective_id=collective_id),
    )(*[it.src for it in items])


TM = 1024
TM_ACC = 512
TN_IN = 768


def kernel(x, norm1_g, w_in, mix_conv_w, attn_out_g, conv_out_g, w_out, norm2_g, ffn_up, ffn_conv_w, ffn_down, final_norm_g, loss_target, m_norm1_g, m_w_in, m_mix_conv_w, m_attn_out_g, m_conv_out_g, m_w_out, m_norm2_g, m_ffn_up, m_ffn_conv_w, m_ffn_down, m_final_norm_g, v_norm1_g, v_w_in, v_mix_conv_w, v_attn_out_g, v_conv_out_g, v_w_out, v_norm2_g, v_ffn_up, v_ffn_conv_w, v_ffn_down, v_final_norm_g):
    nbatch, seq, d = x.shape
    t = nbatch * seq
    nt, nta = t // TM, t // TM_ACC
    out_rows = D_MODEL // N_DEV
    down_rows = D_FF // N_DEV
    xf = x.reshape(t, d)
    target = loss_target.reshape(t, d)

    cw_local = jnp.concatenate([ffn_conv_w, mix_conv_w], axis=-1)
    cast = lambda w: _Item(w.astype(BF16), False)
    cast_in = lambda w: _Item(w.astype(BF16), False, land_cols=True)
    cw_all, win0 = _sequencer_gather([_Item(cw_local, False), cast_in(w_in[0])], "gather_a", 0)
    up_t, m_up_t, v_up_t = (jnp.swapaxes(a, 1, 2) for a in (ffn_up, m_ffn_up, v_ffn_up))
    wout0, wup0 = _sequencer_gather([cast(w_out[0]), cast(up_t[0])], "gather_b", 1)
    (wdown0,) = _sequencer_gather([cast(ffn_down[0])], "gather_c", 2)
    win1, wout1 = _sequencer_gather([cast_in(w_in[1]), cast(w_out[1])], "gather_d", 3)
    wup1, wdown1 = _sequencer_gather([cast(up_t[1]), cast(ffn_down[1])], "gather_e", 7)
    win, wup = [win0, win1], [wup0, wup1]
    wout = [w.reshape(D_MODEL, D_MODEL) for w in (wout0, wout1)]
    wdown = [w.reshape(N_UP_PAIRS, UP_CHUNK, D_MODEL) for w in (wdown0, wdown1)]
    fcw = [cw_all[:, k, :, :UP_CHUNK].reshape(2, N_UP_PAIRS, 3, UP_CHUNK) for k in range(DEPTH)]
    mcw = [cw_all[:, k, :, UP_CHUNK:].transpose(1, 0, 2).reshape(3, D_CONV) for k in range(DEPTH)]

    full = lambda i, j, k: (0, 0)

    saved = []
    xin = xf
    h1 = _rms_fwd(xin, norm1_g[0][None], "rms1_fwd_0")
    rows_of = lambda width: pl.BlockSpec((TM_ACC, width), lambda i: (i, 0))
    whole = lambda *shape: pl.BlockSpec(shape, lambda i: (0,) * len(shape))
    for l in range(DEPTH):
        proj = _matmul(
            h1, win[l], grid=(nt, D_IN // TN_IN, 1), dims=NN, name=f"proj_{l}",
            a_spec=pl.BlockSpec((TM, D_MODEL), lambda i, j, k: (i, 0)),
            b_spec=pl.BlockSpec((D_MODEL, TN_IN), lambda i, j, k: (0, j)),
            o_spec=pl.BlockSpec((TM, TN_IN), lambda i, j, k: (i, j)), o_shape=(t, D_IN), o_dtype=F32)
        o, lse, cat = _attn_fwd(proj, attn_out_g[l][None], nbatch, seq)
        cat = _convmix_fwd(proj, cat, mcw[l], conv_out_g[l][None], nbatch, seq)
        xmid, h2 = _matmul_norm(cat, wout[l], xin, norm2_g[l][None], dims=NN, name=f"mix_out_{l}",
                                a_spec=rows_of(D_MODEL), b_spec=whole(D_MODEL, D_MODEL))
        if l + 1 < DEPTH:
            xout, h_next, act, pre = _ffn_fwd(
                h2, wup[l], fcw[l], wdown[l], xmid, norm1_g[l + 1][None], seq, f"ffn_fwd_{l}")
        else:
            h_next = None
            xout, act, pre = _ffn_fwd(h2, wup[l], fcw[l], wdown[l], xmid, None, seq, f"ffn_fwd_{l}")
        saved.append((xin, h1, proj, o, lse, cat, xmid, h2, pre, act))
        xin, h1 = xout, h_next

    loss_part, dx, dxb, dgf = _loss_head(xin, final_norm_g[None], target, "loss_head")

    dg1, dg2, dga, dgc = [None] * DEPTH, [None] * DEPTH, [None] * DEPTH, [None] * DEPTH
    for l in reversed(range(DEPTH)):
        xin, h1, proj, o, lse, cat, xmid, h2, pre, act = saved[l]
        g_down = _matmul(
            act, dxb, grid=(N_UP_PAIRS, 1, 1), dims=TN, name=f"g_down_{l}",
            a_spec=pl.BlockSpec((None, t, UP_CHUNK), lambda i, j, k: (i, 0, 0)),
            b_spec=pl.BlockSpec((t, D_MODEL), full),
            o_spec=pl.BlockSpec((None, UP_CHUNK, D_MODEL), lambda i, j, k: (i, 0, 0)),
            o_shape=(N_UP_PAIRS, UP_CHUNK, D_MODEL), o_dtype=BF16).reshape(N_DEV, down_rows, D_MODEL)
        d_pre, d_fcw, dxm, dxmb, dg2[l] = _ffn_up_bwd(
            pre, dxb, fcw[l], wdown[l], wup[l], xmid, norm2_g[l][None], dx, seq, f"ffn_bwd_{l}")
        d_pre = d_pre.reshape(N_DEV, t, UP_CHUNK)
        g_up = _matmul(
            d_pre, h2, grid=(N_DEV, 1, 1), dims=TN, name=f"g_up_{l}",
            a_spec=pl.BlockSpec((None, t, UP_CHUNK), lambda i, j, k: (i, 0, 0)),
            b_spec=pl.BlockSpec((t, D_MODEL), full),
            o_spec=pl.BlockSpec((None, UP_CHUNK, D_MODEL), lambda i, j, k: (i, 0, 0)),
            o_shape=(N_DEV, UP_CHUNK, D_MODEL), o_dtype=BF16)
        g_out = _matmul(
            cat, dxmb, grid=(1, 1, nt), dims=TN, name=f"g_out_{l}",
            a_spec=pl.BlockSpec((TM, D_MODEL), lambda i, j, k: (k, 0)),
            b_spec=pl.BlockSpec((TM, D_MODEL), lambda i, j, k: (k, 0)),
            o_spec=pl.BlockSpec((D_MODEL, D_MODEL), full),
            o_shape=(D_MODEL, D_MODEL), o_dtype=BF16).reshape(N_DEV, out_rows, D_MODEL)
        if l == 0:
            land_out0, land_up0, land_down0 = _sequencer_exchange(
                [_Item(g_out, True), _Item(g_up, True), _Item(g_down, True)], "scatter_0a", 5)
        d_cat = _matmul(
            dxmb, wout[l], grid=(nta, 1, 1), dims=NT, name=f"d_cat_{l}",
            a_spec=pl.BlockSpec((TM_ACC, D_MODEL), lambda i, j, k: (i, 0)),
            b_spec=pl.BlockSpec((D_MODEL, D_MODEL), full),
            o_spec=pl.BlockSpec((TM_ACC, D_MODEL), lambda i, j, k: (i, 0)), o_shape=(t, D_MODEL), o_dtype=BF16)
        d_proj, dga[l] = _attn_bwd(proj, o, lse, d_cat, attn_out_g[l][None], nbatch, seq)
        d_proj, d_mcw, dgc[l] = _convmix_bwd(proj, d_cat, d_proj, mcw[l], conv_out_g[l][None], nbatch, seq)
        g_in = _matmul(
            h1, d_proj, grid=(1, D_IN // TN_IN, 1), dims=TN, name=f"g_in_{l}",
            a_spec=pl.BlockSpec((t, D_MODEL), full),
            b_spec=pl.BlockSpec((t, TN_IN), lambda i, j, k: (0, j)),
            o_spec=pl.BlockSpec((D_MODEL, TN_IN), lambda i, j, k: (0, j)),
            o_shape=(D_MODEL, D_IN), o_dtype=BF16)
        g_cw = jnp.concatenate(
            [d_fcw.reshape(N_DEV, 3, UP_CHUNK), d_mcw.reshape(3, N_DEV, D_CONV // N_DEV).transpose(1, 0, 2)], axis=-1)
        if l == 0:
            land_in0, land_cw0 = _sequencer_exchange([_Item(g_in, "cols"), _Item(g_cw, True)], "scatter_0b", 6)
        else:
            land_in1, land_out1, land_up1, land_down1, land_cw1 = _sequencer_exchange(
                [_Item(g_in, "cols"), _Item(g_out, True), _Item(g_up, True), _Item(g_down, True), _Item(g_cw, True)],
                "scatter_1", 4)
        dx, dxb, dg1[l] = _matmul_norm_bwd(
            d_proj, win[l], xin, norm1_g[l][None], dxm, dims=NT, name=f"d_h1_{l}",
            a_spec=rows_of(D_IN), b_spec=whole(D_MODEL, D_IN))

    def pack_small(n1, a, c, n2, f):
        return jnp.concatenate(
            [n1, n2, f[None], jnp.concatenate([a, c], axis=-1), jnp.zeros((1, D_MODEL), F32)], axis=0)[None]

    small = jnp.concatenate(
        [dg1[0], dg1[1], dg2[0], dg2[1], dgf,
         jnp.concatenate([dga[0], dgc[0]], axis=-1), jnp.concatenate([dga[1], dgc[1]], axis=-1),
         jnp.pad(loss_part, ((0, 0), (0, D_MODEL - LANES)))], axis=0)
    (land_small,) = _exchange([_Item(small, False)], "gather_gain_grads")
    res_small = _adamw(
        [land_small], pack_small(norm1_g, attn_out_g, conv_out_g, norm2_g, final_norm_g),
        pack_small(m_norm1_g, m_attn_out_g, m_conv_out_g, m_norm2_g, m_final_norm_g),
        pack_small(v_norm1_g, v_attn_out_g, v_conv_out_g, v_norm2_g, v_final_norm_g), SUBLANES, "adamw_gains")
    res_out = _adamw([land_out0, land_out1], w_out, m_w_out, v_w_out, out_rows, "adamw_w_out", after=[res_small[0]])
    res_up_t = _adamw([land_up0, land_up1], up_t, m_up_t, v_up_t, UP_CHUNK // 4, "adamw_ffn_up", after=[res_out[0]])
    res_up = [jnp.swapaxes(r, 1, 2) for r in res_up_t]
    res_down = _adamw([land_down0, land_down1], ffn_down, m_ffn_down, v_ffn_down, down_rows, "adamw_ffn_down",
                      after=[res_up_t[0]])
    res_in = _adamw([land_in0, land_in1], w_in, m_w_in, v_w_in, 256, "adamw_w_in", after=[res_down[0]])
    res_cw = _adamw(
        [land_cw0, land_cw1], cw_local, jnp.concatenate([m_ffn_conv_w, m_mix_conv_w], axis=-1),
        jnp.concatenate([v_ffn_conv_w, v_mix_conv_w], axis=-1), 3, "adamw_conv_w", after=[res_in[0]])

    loss = res_small[0][0, SUBLANES - 1, 0]

    def unpack(kind):
        s = res_small[kind][0]
        cwr = res_cw[kind]
        return (s[0:2], res_in[kind], cwr[..., UP_CHUNK:], s[5:7, :D_ATTN], s[5:7, D_ATTN:], res_out[kind],
                s[2:4], res_up[kind], cwr[..., :UP_CHUNK], res_down[kind], s[4])

    return (loss, dx.reshape(nbatch, seq, d), *unpack(0), *unpack(1), *unpack(2), *unpack(3))
```

```python
import math

import jax
import jax.numpy as jnp
from jax import lax
from jax.experimental import pallas as pl
from jax.experimental.pallas import tpu as pltpu
from jax.experimental.pallas import tpu_sc as plsc

F32 = jnp.float32
BF16 = jnp.bfloat16

D_MODEL = 1024
D_ATTN = 512
D_CONV = 512
HEAD_DIM = 64
N_HEADS = 8
D_FF = 2816
DEPTH = 2
D_IN = 3 * D_ATTN + 3 * D_CONV
EPS = 1e-6
DILATIONS = (1, 4, 16)
BAND = 128
N_DEV = 8
UP_CHUNK = 2 * D_FF // N_DEV
N_UP_PAIRS = N_DEV // 2
CW_PACK = UP_CHUNK + D_CONV // N_DEV
ADAM_LR = 0.001
ADAM_B1 = 0.9
ADAM_B2 = 0.999
ADAM_EPS = 1e-08
ADAM_WD = 0.01
ADAM_STEP = 10
LANES = 128
SUBLANES = 8
VMEM_LIMIT = 56 * 1024 * 1024

NEG = -1e30
MESH = pl.DeviceIdType.MESH


def _params(sem=None, vmem=VMEM_LIMIT):
    return pltpu.CompilerParams(dimension_semantics=sem, vmem_limit_bytes=vmem)


NN = (((1,), (0,)), ((), ()))
NT = (((1,), (1,)), ((), ()))
TN = (((0,), (0,)), ((), ()))
TN_PIECE = 1024


def _contract(a_ref, b_ref, dims):
    def dot(av, bv):
        return lax.dot_general(av.astype(BF16), bv.astype(BF16), dims, preferred_element_type=F32)

    if len(a_ref.shape) == 2:
        if dims == TN and a_ref.shape[0] > TN_PIECE:
            part = None
            for r0 in range(0, a_ref.shape[0], TN_PIECE):
                piece = dot(a_ref[pl.ds(r0, TN_PIECE), :], b_ref[pl.ds(r0, TN_PIECE), :])
                part = piece if part is None else part + piece
            return part
        return dot(a_ref[...], b_ref[...])
    part = dot(a_ref[0], b_ref[0])
    for c in range(1, a_ref.shape[0]):
        part = part + dot(a_ref[c], b_ref[c])
    return part


def _matmul(a, b, *, grid, a_spec, b_spec, o_spec, o_shape, o_dtype, dims, name, res=None, res_spec=None, after=()):
    nk = grid[2]
    o_block = tuple(s for s in o_spec.block_shape if s is not None)
    na = len(after)

    def body(*refs):
        refs = refs[:2 + (res is not None)] + refs[2 + (res is not None) + na:]
        if res is None:
            a_ref, b_ref, o_ref, *scr = refs
            r_ref = None
        else:
            a_ref, b_ref, r_ref, o_ref, *scr = refs
        part = _contract(a_ref, b_ref, dims)

        def finish(total):
            if r_ref is not None:
                total = total + r_ref[...]
            o_ref[...] = total.astype(o_dtype)

        if nk == 1:
            finish(part)
        else:
            acc = scr[0]
            k = pl.program_id(2)

            @pl.when(k == 0)
            def _():
                acc[...] = part

            @pl.when(k > 0)
            def _():
                acc[...] += part

            @pl.when(k == nk - 1)
            def _():
                finish(acc[...])

    in_specs = [a_spec, b_spec] + ([res_spec] if res is not None else []) + [pl.BlockSpec(memory_space=pl.ANY)] * na
    args = (a, b) + ((res,) if res is not None else ()) + tuple(after)
    return pl.pallas_call(
        body, name=name, grid=grid, in_specs=in_specs, out_specs=o_spec,
        out_shape=jax.ShapeDtypeStruct(o_shape, o_dtype),
        scratch_shapes=[pltpu.VMEM(o_block, F32)] if nk > 1 else [],
        compiler_params=_params(("parallel", "parallel", "arbitrary")),
    )(*args)


ROW_TILE = 512


def _rms_fwd(x, g, name):
    t, d = x.shape

    def body(x_ref, g_ref, h_ref):
        xv = x_ref[...]
        r = lax.rsqrt(jnp.mean(xv * xv, axis=-1, keepdims=True) + EPS)
        h_ref[...] = (xv * r * g_ref[...]).astype(BF16)

    return pl.pallas_call(
        body, name=name, grid=(t // ROW_TILE,),
        in_specs=[pl.BlockSpec((ROW_TILE, d), lambda i: (i, 0)), pl.BlockSpec((1, d), lambda i: (0, 0))],
        out_specs=pl.BlockSpec((ROW_TILE, d), lambda i: (i, 0)),
        out_shape=jax.ShapeDtypeStruct((t, d), BF16),
        compiler_params=_params(("parallel",)),
    )(x, g)


def _matmul_norm(a, b, res, g, *, a_spec, b_spec, dims, name):
    t, d = res.shape

    def body(a_ref, b_ref, r_ref, g_ref, x_ref, h_ref):
        xv = _contract(a_ref, b_ref, dims) + r_ref[...]
        x_ref[...] = xv
        h_ref[...] = (xv * lax.rsqrt(jnp.mean(xv * xv, axis=-1, keepdims=True) + EPS) * g_ref[...]).astype(BF16)

    row = pl.BlockSpec((TM_ACC, d), lambda i: (i, 0))
    return pl.pallas_call(
        body, name=name, grid=(t // TM_ACC,),
        in_specs=[a_spec, b_spec, row, pl.BlockSpec((1, d), lambda i: (0, 0))], out_specs=[row, row],
        out_shape=[jax.ShapeDtypeStruct((t, d), F32), jax.ShapeDtypeStruct((t, d), BF16)],
        compiler_params=_params(("parallel",)),
    )(a, b, res, g)


def _matmul_norm_bwd(a, b, x, g, dres, *, a_spec, b_spec, dims, name):
    t, d = x.shape

    def body(a_ref, b_ref, x_ref, g_ref, dres_ref, dx_ref, dxb_ref, dg_ref):
        dhv = _contract(a_ref, b_ref, dims)
        xv = x_ref[...]
        r = lax.rsqrt(jnp.mean(xv * xv, axis=-1, keepdims=True) + EPS)
        xh = xv * r
        gd = dhv * g_ref[...]
        dx = r * (gd - xh * jnp.mean(gd * xh, axis=-1, keepdims=True)) + dres_ref[...]
        dx_ref[...] = dx
        dxb_ref[...] = dx.astype(BF16)
        part = jnp.sum(dhv * xh, axis=0, keepdims=True)

        @pl.when(pl.program_id(0) == 0)
        def _():
            dg_ref[...] = part

        @pl.when(pl.program_id(0) > 0)
        def _():
            dg_ref[...] += part

    row = pl.BlockSpec((TM_ACC, d), lambda i: (i, 0))
    vec = pl.BlockSpec((1, d), lambda i: (0, 0))
    return pl.pallas_call(
        body, name=name, grid=(t // TM_ACC,),
        in_specs=[a_spec, b_spec, row, vec, row], out_specs=[row, row, vec],
        out_shape=[jax.ShapeDtypeStruct((t, d), F32), jax.ShapeDtypeStruct((t, d), BF16),
                   jax.ShapeDtypeStruct((1, d), F32)],
        compiler_params=_params(("arbitrary",)),
    )(a, b, x, g, dres)


def _loss_head(x, g, target, name):
    t, d = x.shape

    def body(x_ref, g_ref, t_ref, loss_ref, dx_ref, dxb_ref, dg_ref):
        xv = x_ref[...]
        r = lax.rsqrt(jnp.mean(xv * xv, axis=-1, keepdims=True) + EPS)
        xh = xv * r
        gv = g_ref[...]
        err = xh * gv - t_ref[...]
        loss = jnp.full((1, LANES), 0.5 / d, F32) * jnp.sum(err * err)
        dy = err * (1.0 / d)
        gd = dy * gv
        dx = r * (gd - xh * jnp.mean(gd * xh, axis=-1, keepdims=True))
        dx_ref[...] = dx
        dxb_ref[...] = dx.astype(BF16)
        part = jnp.sum(dy * xh, axis=0, keepdims=True)

        @pl.when(pl.program_id(0) == 0)
        def _():
            dg_ref[...] = part
            loss_ref[...] = loss

        @pl.when(pl.program_id(0) > 0)
        def _():
            dg_ref[...] += part
            loss_ref[...] += loss

    row = pl.BlockSpec((ROW_TILE, d), lambda i: (i, 0))
    vec = pl.BlockSpec((1, d), lambda i: (0, 0))
    return pl.pallas_call(
        body, name=name, grid=(t // ROW_TILE,),
        in_specs=[row, vec, row],
        out_specs=[pl.BlockSpec((1, LANES), lambda i: (0, 0)), row, row, vec],
        out_shape=[jax.ShapeDtypeStruct((1, LANES), F32), jax.ShapeDtypeStruct((t, d), F32),
                   jax.ShapeDtypeStruct((t, d), BF16), jax.ShapeDtypeStruct((1, d), F32)],
        compiler_params=_params(("arbitrary",)),
    )(x, g, target)


def _group_matrix(n):
    shift = int(math.log2(HEAD_DIM))
    r = lax.broadcasted_iota(jnp.int32, (n, n), 0) >> shift
    c = lax.broadcasted_iota(jnp.int32, (n, n), 1) >> shift
    return (r == c).astype(BF16)


def _group_sum(v, gmat):
    hi = v.astype(BF16)
    lo = (v - hi.astype(F32)).astype(BF16)

    def dot(p):
        return jnp.dot(p, gmat, preferred_element_type=F32)

    return dot(hi) + dot(lo)


def _shift_rows(ext, k):
    return pltpu.roll(ext, k % ext.shape[0], 0)


def _store_columns(stage, out_hbm, sems, row0, nrows, col_blocks):
    rows = pl.ds(pl.multiple_of(row0, SUBLANES * 2), nrows)
    copies = [
        pltpu.make_async_copy(stage.at[i], out_hbm.at[rows, pl.ds(pl.multiple_of(cb * LANES, LANES), LANES)], sems.at[i])
        for i, cb in enumerate(col_blocks)
    ]
    for cp in copies:
        cp.start()
    for cp in copies:
        cp.wait()


def _attn_consts(width):
    i = lax.broadcasted_iota(jnp.int32, (BAND, width), 0)
    j = lax.broadcasted_iota(jnp.int32, (BAND, width), 1)
    dist = (width - BAND) + i - j
    inwin = (dist >= 0) & (dist <= BAND)
    return dist.astype(F32), inwin, j


def _head_masks():
    lane = lax.broadcasted_iota(jnp.int32, (1, LANES), 1)
    return [(lane < HEAD_DIM).astype(F32), (lane >= HEAD_DIM).astype(F32)]


def _pair_bias(slope, dil):
    distf, inwin, _ = _attn_consts(2 * BAND)
    return jnp.concatenate([jnp.where(inwin, distf * (slope[hh] * (-float(dil))), NEG) for hh in range(2)], axis=0)


def _stack_heads(xv, hmask):
    return jnp.concatenate([xv * hmask[0], xv * hmask[1]], axis=0).astype(BF16)


FWD_UNROLL = 16
BWD_UNROLL = 16


def _unroll(trips, most):
    return max(u for u in range(1, most + 1) if trips % u == 0)


def _for_blocks(seq, dil, block, most):
    nb = seq // dil // BAND

    def residue(r, carry):
        base = r * nb
        block(pl.multiple_of(base * BAND, BAND), None)
        if nb > 1:
            def rest(n, c):
                block(pl.multiple_of((base + n) * BAND, BAND), pl.multiple_of((base + n - 1) * BAND, BAND))
                return c

            lax.fori_loop(1, nb, rest, 0, unroll=_unroll(nb - 1, most))
        return carry

    if dil == 1:
        residue(0, 0)
    else:
        lax.fori_loop(0, dil, residue, 0, unroll=_unroll(dil, max(1, most // nb)))


def _permute_in(src_ref, dst_ref, dil, seq):
    length = seq // dil
    for r in range(dil):
        dst_ref[pl.ds(r * length, length), :] = src_ref[pl.ds(r, length, stride=dil), :].astype(dst_ref.dtype)


def _slopes_table():
    slopes = 2.0 ** (-8.0 * jnp.arange(1, N_HEADS + 1, dtype=F32) / N_HEADS)
    return jnp.broadcast_to(slopes[:, None], (N_HEADS, 2 * BAND))


def _attn_fwd(proj, attn_g, nbatch, seq):
    t = nbatch * seq
    scale = HEAD_DIM ** -0.5

    def body(q_ref, k_ref, v_ref, g_ref, sl_ref, o_ref, lse_ref, cat_ref, pq, pk, pv, po, pm, pll, ao, am, al):
        hp = pl.program_id(1)
        hmask = _head_masks()
        slope = [sl_ref[pl.ds(2 * hp + hh, 1), :] for hh in range(2)]

        def run_branch(dil, qs, ks, vs, osink, msink, lsink):
            bias = _pair_bias(slope, dil)

            def block(row0, prow):
                cur = pl.ds(row0, BAND)
                q2 = _stack_heads(qs[cur, :] * scale, hmask)
                if prow is None:
                    kk, vv, bias_b = ks[cur, :], vs[cur, :], bias[:, BAND:]
                else:
                    prev = pl.ds(prow, BAND)
                    kk = jnp.concatenate([ks[prev, :], ks[cur, :]], axis=0)
                    vv = jnp.concatenate([vs[prev, :], vs[cur, :]], axis=0)
                    bias_b = bias
                s = lax.dot_general(q2, kk.astype(BF16), NT, preferred_element_type=F32) + bias_b
                m = jnp.max(s, axis=1, keepdims=True)
                p = jnp.exp(s - m)
                l = jnp.sum(p, axis=1, keepdims=True)
                pb = p.astype(BF16)
                o = jnp.dot(jnp.concatenate([pb[:BAND], pb[BAND:]], axis=1), _stack_heads(vv, hmask),
                            preferred_element_type=F32)
                osink[cur, :] = o
                msink[cur, :] = m[:BAND] * hmask[0] + m[BAND:] * hmask[1]
                lsink[cur, :] = l[:BAND] * hmask[0] + l[BAND:] * hmask[1]

            _for_blocks(seq, dil, block, FWD_UNROLL)

        run_branch(1, q_ref, k_ref, v_ref, ao, am, al)
        for dil in DILATIONS[1:]:
            length = seq // dil
            _permute_in(q_ref, pq, dil, seq)
            _permute_in(k_ref, pk, dil, seq)
            _permute_in(v_ref, pv, dil, seq)
            run_branch(dil, pq, pk, pv, po, pm, pll)
            for r in range(dil):
                nat = pl.ds(r, length, stride=dil)
                per = pl.ds(r * length, length)
                m0 = am[nat, :]
                mb = pm[per, :]
                mn = jnp.maximum(m0, mb)
                e0 = jnp.exp(m0 - mn)
                eb = jnp.exp(mb - mn)
                ao[nat, :] = ao[nat, :] * e0 + po[per, :] * eb
                al[nat, :] = al[nat, :] * e0 + pll[per, :] * eb
                am[nat, :] = mn

        gmat = _group_matrix(LANES)
        gv = g_ref[...]

        def fin(c, carry):
            rows = pl.ds(pl.multiple_of(c * 256, 256), 256)
            lv = al[rows, :]
            o = ao[rows, :] / lv
            o_ref[rows, :] = o
            lse_ref[rows, :] = am[rows, :] + jnp.log(lv)
            ms = _group_sum(o * o, gmat) * (1.0 / HEAD_DIM)
            cat_ref[rows, :] = (o * lax.rsqrt(ms + EPS) * gv).astype(BF16)
            return carry

        lax.fori_loop(0, seq // 256, fin, 0, unroll=True)

    nq = D_ATTN // LANES
    blk = lambda off: pl.BlockSpec((seq, LANES), lambda b, h: (b, h + off))
    scratch = [pltpu.VMEM((seq, LANES), F32) for _ in range(9)]
    return pl.pallas_call(
        body, name="attn_fwd", grid=(nbatch, nq),
        in_specs=[blk(0), blk(nq), blk(2 * nq), pl.BlockSpec((1, LANES), lambda b, h: (0, h)),
                  pl.BlockSpec((N_HEADS, 2 * BAND), lambda b, h: (0, 0))],
        out_specs=[blk(0), blk(0), blk(0)],
        out_shape=[jax.ShapeDtypeStruct((t, D_ATTN), F32), jax.ShapeDtypeStruct((t, D_ATTN), F32),
                   jax.ShapeDtypeStruct((t, D_MODEL), BF16)],
        scratch_shapes=scratch,
        compiler_params=_params(("parallel", "parallel")),
    )(proj, proj, proj, attn_g, _slopes_table())


def _attn_bwd(proj, o, lse, d_cat, attn_g, nbatch, seq):
    t = nbatch * seq
    scale = HEAD_DIM ** -0.5

    def body(q_ref, k_ref, v_ref, o_ref, lse_ref, dy_ref, g_ref, sl_ref, dproj_ref, dg_ref,
             do_n, dl_n, dq_n, dk_n, dv_n, pq, pk, pv, pdo, plse, pdl, pdq, pdk, pdv, stage, sems):
        hp = pl.program_id(0)
        hmask = _head_masks()
        slope = [sl_ref[pl.ds(2 * hp + hh, 1), :] for hh in range(2)]
        gmat = _group_matrix(LANES)
        gv = g_ref[...]

        def prep(c, dg):
            rows = pl.ds(pl.multiple_of(c * 256, 256), 256)
            ov = o_ref[rows, :]
            dyn = dy_ref[rows, :].astype(F32)
            r = lax.rsqrt(_group_sum(ov * ov, gmat) * (1.0 / HEAD_DIM) + EPS)
            gd = dyn * gv
            oh = ov * r
            do = r * (gd - oh * (_group_sum(gd * oh, gmat) * (1.0 / HEAD_DIM)))
            do_n[rows, :] = do
            dl_n[rows, :] = _group_sum(do * ov, gmat)
            return dg + jnp.sum(dyn * oh, axis=0, keepdims=True)

        dg = lax.fori_loop(0, seq // 256, prep, jnp.zeros((1, LANES), F32), unroll=True)

        @pl.when(pl.program_id(1) == 0)
        def _():
            dg_ref[...] = dg

        @pl.when(pl.program_id(1) > 0)
        def _():
            dg_ref[...] += dg

        def run_branch(dil, qs, ks, vs, dos, lses, dls, dqs, dks, dvs):
            bias = _pair_bias(slope, dil)

            def per_head(xv):
                return jnp.concatenate([xv[:, 0:1], xv[:, HEAD_DIM:HEAD_DIM + 1]], axis=0)

            def block(row0, prow):
                cur = pl.ds(row0, BAND)
                keys = cur if prow is None else pl.ds(prow, 2 * BAND)
                q2 = _stack_heads(qs[cur, :] * scale, hmask)
                do2 = _stack_heads(dos[cur, :], hmask)
                kk, vv = ks[keys, :], vs[keys, :]
                s = lax.dot_general(q2, kk.astype(BF16), NT, preferred_element_type=F32)
                s = s + (bias[:, BAND:] if prow is None else bias)
                p = jnp.exp(s - per_head(lses[cur, :]))
                dp = lax.dot_general(do2, vv.astype(BF16), NT, preferred_element_type=F32)
                ds = (p * (dp - per_head(dls[cur, :]))).astype(BF16)
                dkk = lax.dot_general(ds, q2, TN, preferred_element_type=F32)
                dvv = lax.dot_general(p.astype(BF16), do2, TN, preferred_element_type=F32)
                dqs[cur, :] = jnp.dot(jnp.concatenate([ds[:BAND], ds[BAND:]], axis=1), _stack_heads(kk, hmask),
                                      preferred_element_type=F32)
                if prow is None:
                    dks[cur, :] = dkk
                    dvs[cur, :] = dvv
                else:
                    prev = pl.ds(prow, BAND)
                    dks[prev, :] += dkk[:BAND]
                    dvs[prev, :] += dvv[:BAND]
                    dks[cur, :] = dkk[BAND:]
                    dvs[cur, :] = dvv[BAND:]

            _for_blocks(seq, dil, block, BWD_UNROLL)

        run_branch(1, q_ref, k_ref, v_ref, do_n, lse_ref, dl_n, dq_n, dk_n, dv_n)
        for dil in DILATIONS[1:]:
            length = seq // dil
            for src, dst in ((q_ref, pq), (k_ref, pk), (v_ref, pv), (do_n, pdo), (lse_ref, plse), (dl_n, pdl)):
                _permute_in(src, dst, dil, seq)
            run_branch(dil, pq, pk, pv, pdo, plse, pdl, pdq, pdk, pdv)
            for r in range(dil):
                nat = pl.ds(r, length, stride=dil)
                per = pl.ds(r * length, length)
                dq_n[nat, :] += pdq[per, :]
                dk_n[nat, :] += pdk[per, :]
                dv_n[nat, :] += pdv[per, :]

        def emit(c, carry):
            rows = pl.ds(pl.multiple_of(c * 256, 256), 256)
            stage[0, rows, :] = (dq_n[rows, :] * scale).astype(BF16)
            stage[1, rows, :] = dk_n[rows, :].astype(BF16)
            stage[2, rows, :] = dv_n[rows, :].astype(BF16)
            return carry

        lax.fori_loop(0, seq // 256, emit, 0)
        _store_columns(stage, dproj_ref, sems, pl.program_id(1) * seq, seq, [hp, nq + hp, 2 * nq + hp])

    nq = D_ATTN // LANES
    blk = lambda off: pl.BlockSpec((seq, LANES), lambda h, b: (b, h + off))
    vec = pl.BlockSpec((1, LANES), lambda h, b: (0, h))
    scratch = [pltpu.VMEM((seq, LANES), F32) for _ in range(14)]
    scratch += [pltpu.VMEM((3, seq, LANES), BF16), pltpu.SemaphoreType.DMA((3,))]
    d_proj, dg = pl.pallas_call(
        body, name="attn_bwd", grid=(nq, nbatch),
        in_specs=[blk(0), blk(nq), blk(2 * nq), blk(0), blk(0), blk(0), vec,
                  pl.BlockSpec((N_HEADS, 2 * BAND), lambda h, b: (0, 0))],
        out_specs=[pl.BlockSpec(memory_space=pl.ANY), vec],
        out_shape=[jax.ShapeDtypeStruct((t, D_IN), BF16), jax.ShapeDtypeStruct((1, D_ATTN), F32)],
        scratch_shapes=scratch,
        compiler_params=_params(("arbitrary", "arbitrary")),
    )(proj, proj, proj, o, lse, d_cat, attn_g, _slopes_table())
    return d_proj, dg


HALO = SUBLANES
PACKED_ROWS = 2 * SUBLANES


def _window(ref, c, rows, nchunks, after):
    row0 = pl.multiple_of(c * rows, rows)
    prev0 = pl.multiple_of(jnp.maximum(row0 - PACKED_ROWS, 0), PACKED_ROWS)
    before = ref[pl.ds(prev0, PACKED_ROWS), :].astype(F32)[PACKED_ROWS - HALO:] * (c > 0).astype(F32)
    parts = [before, ref[pl.ds(row0, rows), :].astype(F32)]
    if after:
        next0 = pl.multiple_of(jnp.minimum(row0 + rows, (nchunks - 1) * rows), PACKED_ROWS)
        parts.append(ref[pl.ds(next0, PACKED_ROWS), :].astype(F32)[:HALO] * (c < nchunks - 1).astype(F32))
    return jnp.concatenate(parts, axis=0)


def _behind(z):
    z1 = _shift_rows(z, 1)
    return z1, _shift_rows(z1, 1)


def _ahead(dy):
    d1 = _shift_rows(dy, -1)
    return d1, _shift_rows(d1, -1)


def _conv(z, w):
    z1, z2 = _behind(z)
    return w[0:1] * z2 + w[1:2] * z1 + w[2:3] * z


def _conv_bwd(dy, z, w, cur):
    d1, d2 = _ahead(dy)
    dz = w[2:3] * dy + w[1:2] * d1 + w[0:1] * d2
    return dz, [jnp.sum((d * z)[cur], axis=0, keepdims=True) for d in (d2, d1, dy)]


def _sigmoid(a):
    return 0.5 * jnp.tanh(0.5 * a) + 0.5


MIX_ROWS = 256
GATE_B_BLOCK = 3 * D_ATTN // LANES
GATE_C_BLOCK = GATE_B_BLOCK + D_CONV // LANES
U_BLOCK = GATE_C_BLOCK + D_CONV // LANES


def _convmix_fwd(proj, cat, mcw, conv_g, nbatch, seq):
    nchunks = seq // MIX_ROWS

    def body(gb_ref, gc_ref, u_ref, w_ref, g_ref, cat_in, cat_ref):
        del cat_in
        gmat = _group_matrix(LANES)
        w = w_ref[...]
        gv = g_ref[...]

        def step(c, carry):
            cur = pl.ds(pl.multiple_of(c * MIX_ROWS, MIX_ROWS), MIX_ROWS)
            z = _window(gc_ref, c, MIX_ROWS, nchunks, False) * _window(u_ref, c, MIX_ROWS, nchunks, False)
            y = gb_ref[cur, :] * _conv(z, w)[HALO:]
            ms = _group_sum(y * y, gmat) * (1.0 / HEAD_DIM)
            cat_ref[cur, :] = (y * lax.rsqrt(ms + EPS) * gv).astype(BF16)
            return carry

        lax.fori_loop(0, nchunks, step, 0, unroll=True)

    nc = D_CONV // LANES
    blk = lambda off: pl.BlockSpec((seq, LANES), lambda b, j: (b, j + off))
    return pl.pallas_call(
        body, name="convmix_fwd", grid=(nbatch, nc),
        in_specs=[blk(GATE_B_BLOCK), blk(GATE_C_BLOCK), blk(U_BLOCK),
                  pl.BlockSpec((3, LANES), lambda b, j: (0, j)), pl.BlockSpec((1, LANES), lambda b, j: (0, j)),
                  pl.BlockSpec(memory_space=pl.ANY)],
        out_specs=blk(D_ATTN // LANES),
        out_shape=jax.ShapeDtypeStruct(cat.shape, cat.dtype),
        input_output_aliases={5: 0},
        compiler_params=_params(("parallel", "parallel")),
    )(proj, proj, proj, mcw, conv_g, cat)


def _convmix_bwd(proj, d_cat, d_proj, mcw, conv_g, nbatch, seq):
    nchunks = seq // MIX_ROWS

    def body(gb_ref, gc_ref, u_ref, dy_ref, w_ref, g_ref, dproj_in, dproj_ref, dw_ref, dg_ref, stage, sems):
        del dproj_in
        cb = pl.program_id(0)
        b = pl.program_id(1)
        gmat = _group_matrix(LANES)
        w = w_ref[...]
        gv = g_ref[...]
        cur = slice(HALO, HALO + MIX_ROWS)

        def step(c, carry):
            rows = pl.ds(pl.multiple_of(c * MIX_ROWS, MIX_ROWS), MIX_ROWS)
            gb = _window(gb_ref, c, MIX_ROWS, nchunks, True)
            gc = _window(gc_ref, c, MIX_ROWS, nchunks, True)
            u = _window(u_ref, c, MIX_ROWS, nchunks, True)
            dyn = _window(dy_ref, c, MIX_ROWS, nchunks, True)
            z = gc * u
            conv = _conv(z, w)
            y = gb * conv
            r = lax.rsqrt(_group_sum(y * y, gmat) * (1.0 / HEAD_DIM) + EPS)
            yh = y * r
            gd = dyn * gv
            dy = r * (gd - yh * (_group_sum(gd * yh, gmat) * (1.0 / HEAD_DIM)))
            dz, dws = _conv_bwd(dy * gb, z, w, cur)
            stage[0, rows, :] = (dy * conv)[cur].astype(BF16)
            stage[1, rows, :] = (dz * u)[cur].astype(BF16)
            stage[2, rows, :] = (dz * gc)[cur].astype(BF16)
            dg = jnp.sum((dyn * yh)[cur], axis=0, keepdims=True)
            return tuple(a + d for a, d in zip(carry, dws + [dg]))

        zero = jnp.zeros((1, LANES), F32)
        dw0, dw1, dw2, dg = lax.fori_loop(0, nchunks, step, (zero, zero, zero, zero), unroll=True)

        @pl.when(b == 0)
        def _():
            dw_ref[0:1, :] = dw0
            dw_ref[1:2, :] = dw1
            dw_ref[2:3, :] = dw2
            dg_ref[...] = dg

        @pl.when(b > 0)
        def _():
            dw_ref[0:1, :] += dw0
            dw_ref[1:2, :] += dw1
            dw_ref[2:3, :] += dw2
            dg_ref[...] += dg

        _store_columns(stage, dproj_ref, sems, b * seq, seq, [GATE_B_BLOCK + cb, GATE_C_BLOCK + cb, U_BLOCK + cb])

    nc = D_CONV // LANES
    blk = lambda off: pl.BlockSpec((seq, LANES), lambda j, b: (b, j + off))
    return pl.pallas_call(
        body, name="convmix_bwd", grid=(nc, nbatch),
        in_specs=[blk(GATE_B_BLOCK), blk(GATE_C_BLOCK), blk(U_BLOCK), blk(D_ATTN // LANES),
                  pl.BlockSpec((3, LANES), lambda j, b: (0, j)), pl.BlockSpec((1, LANES), lambda j, b: (0, j)),
                  pl.BlockSpec(memory_space=pl.ANY)],
        out_specs=[pl.BlockSpec(memory_space=pl.ANY), pl.BlockSpec((3, LANES), lambda j, b: (0, j)),
                   pl.BlockSpec((1, LANES), lambda j, b: (0, j))],
        out_shape=[jax.ShapeDtypeStruct(d_proj.shape, d_proj.dtype), jax.ShapeDtypeStruct((3, D_CONV), F32),
                   jax.ShapeDtypeStruct((1, D_CONV), F32)],
        scratch_shapes=[pltpu.VMEM((3, seq, LANES), BF16), pltpu.SemaphoreType.DMA((3,))],
        input_output_aliases={6: 0},
        compiler_params=_params(("arbitrary", "arbitrary")),
    )(proj, proj, proj, d_cat, mcw, conv_g, d_proj)


FFN_ROWS = 256
FFN_FWD_ROWS = 256


def _ffn_fwd(h, wup, fcw, wdown, res, g, seq, name):
    t, d = res.shape
    tiles_per_seq = seq // FFN_FWD_ROWS

    def body(hm_ref, hp_ref, wu_ref, w_ref, wd_ref, r_ref, *rest):
        if g is None:
            x_ref, act_ref, pre_ref = rest
        else:
            g_ref, x_ref, h_ref, act_ref, pre_ref = rest
        inside = ((pl.program_id(0) % tiles_per_seq) > 0).astype(F32)
        wrow = lax.broadcasted_iota(jnp.int32, (FFN_FWD_ROWS + HALO, 1), 0)
        edge = jnp.where(wrow < HALO, inside, 1.0)
        rows = jnp.concatenate([hp_ref[...], hm_ref[...]], axis=0)

        def up(j, part, p):
            full = lax.dot_general(rows, wu_ref[j], NT, preferred_element_type=F32).astype(BF16)
            pre_ref[part, p] = full[PACKED_ROWS:]
            return full.astype(F32)[PACKED_ROWS - HALO:] * edge

        total = r_ref[...]
        for p in range(N_UP_PAIRS):
            a = _conv(up(p, 0, p), w_ref[0, p])[HALO:]
            v = _conv(up(N_UP_PAIRS + p, 1, p), w_ref[1, p])[HALO:]
            act = (a * _sigmoid(a) * v).astype(BF16)
            act_ref[p] = act
            total = total + jnp.dot(act, wd_ref[p], preferred_element_type=F32)
        x_ref[...] = total
        if g is not None:
            h_ref[...] = (total * lax.rsqrt(jnp.mean(total * total, axis=-1, keepdims=True) + EPS) * g_ref[...]).astype(BF16)

    row = pl.BlockSpec((FFN_FWD_ROWS, d), lambda i: (i, 0))
    tiles_per_halo = FFN_FWD_ROWS // PACKED_ROWS
    in_specs = [
        row, pl.BlockSpec((PACKED_ROWS, d), lambda i: (jnp.maximum(i * tiles_per_halo - 1, 0), 0)),
        pl.BlockSpec((N_DEV, UP_CHUNK, d), lambda i: (0, 0, 0)),
        pl.BlockSpec((2, N_UP_PAIRS, 3, UP_CHUNK), lambda i: (0, 0, 0, 0)),
        pl.BlockSpec((N_UP_PAIRS, UP_CHUNK, d), lambda i: (0, 0, 0)), row]
    out_specs = [row]
    out_shape = [jax.ShapeDtypeStruct((t, d), F32)]
    args = [h, h, wup, fcw, wdown, res]
    if g is not None:
        in_specs.append(pl.BlockSpec((1, d), lambda i: (0, 0)))
        out_specs.append(row)
        out_shape.append(jax.ShapeDtypeStruct((t, d), BF16))
        args.append(g)
    out_specs += [pl.BlockSpec((N_UP_PAIRS, FFN_FWD_ROWS, UP_CHUNK), lambda i: (0, i, 0)),
                  pl.BlockSpec((2, N_UP_PAIRS, FFN_FWD_ROWS, UP_CHUNK), lambda i: (0, 0, i, 0))]
    out_shape += [jax.ShapeDtypeStruct((N_UP_PAIRS, t, UP_CHUNK), BF16),
                  jax.ShapeDtypeStruct((2, N_UP_PAIRS, t, UP_CHUNK), BF16)]
    return pl.pallas_call(
        body, name=name, grid=(t // FFN_FWD_ROWS,), in_specs=in_specs, out_specs=out_specs, out_shape=out_shape,
        compiler_params=_params(("parallel",)),
    )(*args)


def _ffn_up_bwd(pre, dy, fcw, wdown, wup, x, g, dres, seq, name):
    t, d = x.shape
    tiles_per_seq = seq // FFN_ROWS
    tiles_per_halo = FFN_ROWS // PACKED_ROWS
    last_halo = t // PACKED_ROWS - 1

    def body(pm_ref, pp_ref, pn_ref, dm_ref, dp_ref, dn_ref, w_ref, wd_ref, wu_ref, x_ref, g_ref, dres_ref,
             dpre_ref, dw_ref, dx_ref, dxb_ref, dg_ref):
        i = pl.program_id(0)
        has_prev = ((i % tiles_per_seq) > 0).astype(F32)
        has_next = ((i % tiles_per_seq) < tiles_per_seq - 1).astype(F32)
        cur = slice(HALO, HALO + FFN_ROWS)

        def window(before, main, after):
            return jnp.concatenate([before.astype(F32)[PACKED_ROWS - HALO:] * has_prev, main.astype(F32),
                                    after.astype(F32)[:HALO] * has_next], axis=0)

        dy_rows = jnp.concatenate([dp_ref[...], dm_ref[...], dn_ref[...]], axis=0)
        wrow = lax.broadcasted_iota(jnp.int32, (FFN_ROWS + 2 * HALO, 1), 0)
        edge = jnp.where(wrow < HALO, has_prev, jnp.where(wrow >= HALO + FFN_ROWS, has_next, 1.0))

        dh = jnp.zeros((FFN_ROWS, d), F32)
        sums = []
        for p in range(N_UP_PAIRS):
            pg = window(pp_ref[0, p], pm_ref[0, p], pn_ref[0, p])
            pv = window(pp_ref[1, p], pm_ref[1, p], pn_ref[1, p])
            dact = lax.dot_general(dy_rows, wd_ref[p], NT, preferred_element_type=F32)
            dact = dact[PACKED_ROWS - HALO:PACKED_ROWS + FFN_ROWS + HALO] * edge
            a = _conv(pg, w_ref[0, p])
            v = _conv(pv, w_ref[1, p])
            sg = _sigmoid(a)
            asg = a * sg
            dzg, dwg = _conv_bwd(dact * v * (sg + asg - asg * sg), pg, w_ref[0, p], cur)
            dzv, dwv = _conv_bwd(dact * asg, pv, w_ref[1, p], cur)
            dgate = dzg[cur].astype(BF16)
            dval = dzv[cur].astype(BF16)
            dpre_ref[0, p] = dgate
            dpre_ref[1, p] = dval
            dh = dh + jnp.dot(dgate, wu_ref[p], preferred_element_type=F32)
            dh = dh + jnp.dot(dval, wu_ref[N_UP_PAIRS + p], preferred_element_type=F32)
            sums.append(dwg + dwv)

        xv = x_ref[...]
        r = lax.rsqrt(jnp.mean(xv * xv, axis=-1, keepdims=True) + EPS)
        xh = xv * r
        gd = dh * g_ref[...]
        dx = r * (gd - xh * jnp.mean(gd * xh, axis=-1, keepdims=True)) + dres_ref[...]
        dx_ref[...] = dx
        dxb_ref[...] = dx.astype(BF16)
        part = jnp.sum(dh * xh, axis=0, keepdims=True)

        @pl.when(i == 0)
        def _():
            dg_ref[...] = part
            for p in range(N_UP_PAIRS):
                for k in range(6):
                    dw_ref[k // 3, p, pl.ds(k % 3, 1), :] = sums[p][k]

        @pl.when(i > 0)
        def _():
            dg_ref[...] += part
            for p in range(N_UP_PAIRS):
                for k in range(6):
                    dw_ref[k // 3, p, pl.ds(k % 3, 1), :] += sums[p][k]

    def rows4(n):
        return lambda fn: pl.BlockSpec((2, N_UP_PAIRS, n, UP_CHUNK), lambda i: (0, 0, fn(i), 0))

    def rows2(n):
        return lambda fn: pl.BlockSpec((n, d), lambda i: (fn(i), 0))

    prev_tile = lambda i: jnp.maximum(i * tiles_per_halo - 1, 0)
    next_tile = lambda i: jnp.minimum((i + 1) * tiles_per_halo, last_halo)
    row = pl.BlockSpec((FFN_ROWS, d), lambda i: (i, 0))
    vec = pl.BlockSpec((1, d), lambda i: (0, 0))
    wspec = pl.BlockSpec((2, N_UP_PAIRS, 3, UP_CHUNK), lambda i: (0, 0, 0, 0))
    return pl.pallas_call(
        body, name=name, grid=(t // FFN_ROWS,),
        in_specs=[rows4(FFN_ROWS)(lambda i: i), rows4(PACKED_ROWS)(prev_tile), rows4(PACKED_ROWS)(next_tile),
                  rows2(FFN_ROWS)(lambda i: i), rows2(PACKED_ROWS)(prev_tile), rows2(PACKED_ROWS)(next_tile),
                  wspec, pl.BlockSpec((N_UP_PAIRS, UP_CHUNK, d), lambda i: (0, 0, 0)),
                  pl.BlockSpec((N_DEV, UP_CHUNK, d), lambda i: (0, 0, 0)), row, vec, row],
        out_specs=[rows4(FFN_ROWS)(lambda i: i), wspec, row, row, vec],
        out_shape=[jax.ShapeDtypeStruct(pre.shape, BF16), jax.ShapeDtypeStruct(fcw.shape, F32),
                   jax.ShapeDtypeStruct((t, d), F32), jax.ShapeDtypeStruct((t, d), BF16),
                   jax.ShapeDtypeStruct((1, d), F32)],
        compiler_params=_params(("arbitrary",)),
    )(pre, pre, pre, dy, dy, dy, fcw, wdown, wup, x, g, dres)


def _adamw(lands, w, m, v, row_tile, name, after=()):
    nl = len(lands)
    _, nr, ncol = lands[0].shape
    c1 = 1.0 - ADAM_B1 ** ADAM_STEP
    c2 = 1.0 - ADAM_B2 ** ADAM_STEP

    def body(*refs):
        land_refs = refs[:nl]
        w_ref, m_ref, v_ref = refs[nl:nl + 3]
        g_ref, d_ref, mo_ref, vo_ref = refs[nl + 3 + len(after):]
        for l in range(nl):
            @pl.when(pl.program_id(0) == l)
            def _(l=l):
                g = land_refs[l][0].astype(F32)
                for j in range(1, N_DEV):
                    g = g + land_refs[l][j].astype(F32)
                g_ref[...] = g

        g = g_ref[...]
        m2 = ADAM_B1 * m_ref[...] + (1.0 - ADAM_B1) * g
        v2 = ADAM_B2 * v_ref[...] + (1.0 - ADAM_B2) * (g * g)
        mo_ref[...] = m2
        vo_ref[...] = v2
        d_ref[...] = -ADAM_LR * ((m2 / c1) / (jnp.sqrt(v2 / c2) + ADAM_EPS) + ADAM_WD * w_ref[...])

    def land_spec(l):
        return pl.BlockSpec((N_DEV, row_tile, ncol), lambda k, i: (0, jnp.where(k == l, i, 0), 0))

    tile = pl.BlockSpec((None, row_tile, ncol), lambda k, i: (k, i, 0))
    return pl.pallas_call(
        body, name=name, grid=(nl, nr // row_tile),
        in_specs=[land_spec(l) for l in range(nl)] + [tile, tile, tile] + [pl.BlockSpec(memory_space=pl.ANY)] * len(after),
        out_specs=[tile] * 4,
        out_shape=[jax.ShapeDtypeStruct(w.shape, F32)] * 4,
        compiler_params=_params(("arbitrary", "arbitrary")),
    )(*lands, w, m, v, *after)


class _Item:
    def __init__(self, src, chunked, land_cols=False):
        self.src, self.chunked, self.land_cols = src, chunked, land_cols
        if chunked == "cols":
            block = (src.shape[0], src.shape[1] // N_DEV)
        else:
            block = src.shape[1:] if chunked else src.shape
        self.width = block[-1]
        self.land_shape = (block[0], N_DEV * block[1]) if land_cols else (N_DEV,) + block

    def _cols(self, first, count=1):
        return pl.ds(pl.multiple_of(first * self.width, LANES), count * self.width)

    def part(self, src_ref, j):
        if self.chunked == "cols":
            return src_ref.at[:, self._cols(j)]
        return src_ref.at[j] if self.chunked else src_ref

    def slot(self, land_ref, s):
        return land_ref.at[:, self._cols(s)] if self.land_cols else land_ref.at[s]


def _mesh_place():
    x, y, c = lax.axis_index("x"), lax.axis_index("y"), lax.axis_index("c")
    return x, y, c, 4 * x + 2 * y + c


def _flipped(x, y, c, k):
    px = 1 - x if k & 4 else x
    py = 1 - y if k & 2 else y
    pc = 1 - c if k & 1 else c
    return (px, py, pc), 4 * px + 2 * py + pc


PEER_ORDER = (2, 4, 6, 3, 5, 7, 1)


def _exchange(items, name):
    n = len(items)

    def body(*refs):
        srcs, lands = refs[:n], refs[n:2 * n]
        send, recv, local = refs[2 * n:]
        x, y, c, me = _mesh_place()

        def copy(i, k, chunk, slot, dev):
            return pltpu.make_async_remote_copy(
                src_ref=items[i].part(srcs[i], chunk), dst_ref=items[i].slot(lands[i], slot),
                send_sem=send.at[i, k - 1], recv_sem=recv.at[i, k - 1], device_id=dev, device_id_type=MESH)

        own = [pltpu.make_async_copy(items[i].part(srcs[i], me), items[i].slot(lands[i], me), local.at[i])
               for i in range(n)]
        for k in PEER_ORDER:
            dev, idx = _flipped(x, y, c, k)
            for i in range(n):
                copy(i, k, idx, me, dev).start()
        for cp in own:
            cp.start()
        for k in PEER_ORDER:
            dev, idx = _flipped(x, y, c, k)
            for i in range(n):
                copy(i, k, me, idx, dev).wait_recv()
        for k in PEER_ORDER:
            dev, idx = _flipped(x, y, c, k)
            for i in range(n):
                copy(i, k, idx, me, dev).wait_send()
        for cp in own:
            cp.wait()

    hbm = pl.BlockSpec(memory_space=pl.ANY)
    return pl.pallas_call(
        body, name=name,
        in_specs=[hbm] * n, out_specs=[hbm] * n,
        out_shape=[jax.ShapeDtypeStruct(it.land_shape, it.src.dtype) for it in items],
        scratch_shapes=[pltpu.SemaphoreType.DMA((n, N_DEV - 1)), pltpu.SemaphoreType.DMA((n, N_DEV - 1)),
                        pltpu.SemaphoreType.DMA((n,))],
        compiler_params=pltpu.CompilerParams(has_side_effects=True),
    )(*[it.src for it in items])


SAME_CORE = (2, 4, 6)


def _sequencer_gather(items, name, collective_id):
    n = len(items)

    def body(*refs):
        srcs, lands = refs[:n], refs[n:2 * n]
        send, recv, local = refs[2 * n:]
        x, y, c, me = _mesh_place()
        sibling, _ = _flipped(x, y, c, 1)
        barrier = pltpu.get_barrier_semaphore()
        for k in SAME_CORE + (1,):
            pl.semaphore_signal(barrier, inc=1, device_id=_flipped(x, y, c, k)[0], device_id_type=MESH)
        pl.semaphore_wait(barrier, len(SAME_CORE) + 1)

        def copy(i, q, src, slot, dev):
            return pltpu.make_async_remote_copy(
                src_ref=src, dst_ref=items[i].slot(lands[i], slot),
                send_sem=send.at[i, q - 1], recv_sem=recv.at[i, q - 1], device_id=dev, device_id_type=MESH)

        own = [pltpu.make_async_copy(srcs[i], items[i].slot(lands[i], me), local.at[i]) for i in range(n)]
        for cp in own:
            cp.start()
        for k in SAME_CORE + (1,):
            for i in range(n):
                copy(i, k, srcs[i], me, _flipped(x, y, c, k)[0]).start()
        for k in SAME_CORE:
            dev, idx = _flipped(x, y, c, k)
            for i in range(n):
                copy(i, k, srcs[i], idx, dev).wait_recv()
            for i in range(n):
                copy(i, k + 1, items[i].slot(lands[i], idx), idx, sibling).start()
        for k in (1,) + tuple(k + 1 for k in SAME_CORE):
            _, idx = _flipped(x, y, c, k)
            for i in range(n):
                copy(i, k, srcs[i], idx, sibling).wait_recv()
        for k in range(1, N_DEV):
            for i in range(n):
                copy(i, k, srcs[i], me, sibling).wait_send()
        for cp in own:
            cp.wait()

    return pl.kernel(
        body, name=name,
        out_type=[jax.ShapeDtypeStruct(it.land_shape, it.src.dtype) for it in items],
        mesh=plsc.ScalarSubcoreMesh(axis_name="sequencer", num_cores=1),
        scratch_types=[pltpu.SemaphoreType.DMA((n, N_DEV - 1)), pltpu.SemaphoreType.DMA((n, N_DEV - 1)),
                       pltpu.SemaphoreType.DMA((n,))],
        compiler_params=pltpu.CompilerParams(collective_id=collective_id),
    )(*[it.src for it in items])


def _sequencer_exchange(items, name, collective_id):
    n = len(items)

    def body(*refs):
        srcs, lands = refs[:n], refs[n:2 * n]
        send, recv, local = refs[2 * n:]
        x, y, c, me = _mesh_place()
        barrier = pltpu.get_barrier_semaphore()
        for k in PEER_ORDER:
            pl.semaphore_signal(barrier, inc=1, device_id=_flipped(x, y, c, k)[0], device_id_type=MESH)
        pl.semaphore_wait(barrier, N_DEV - 1)

        def copy(i, k, chunk, slot, dev):
            return pltpu.make_async_remote_copy(
                src_ref=items[i].part(srcs[i], chunk), dst_ref=items[i].slot(lands[i], slot),
                send_sem=send.at[i, k - 1], recv_sem=recv.at[i, k - 1], device_id=dev, device_id_type=MESH)

        own = [pltpu.make_async_copy(items[i].part(srcs[i], me), items[i].slot(lands[i], me), local.at[i])
               for i in range(n)]
        for cp in own:
            cp.start()
        for k in PEER_ORDER:
            dev, idx = _flipped(x, y, c, k)
            for i in range(n):
                copy(i, k, idx, me, dev).start()
        for k in PEER_ORDER:
            dev, idx = _flipped(x, y, c, k)
            for i in range(n):
                copy(i, k, me, idx, dev).wait_recv()
        for k in PEER_ORDER:
            dev, idx = _flipped(x, y, c, k)
            for i in range(n):
                copy(i, k, idx, me, dev).wait_send()
        for cp in own:
            cp.wait()

    return pl.kernel(
        body, name=name,
        out_type=[jax.ShapeDtypeStruct(it.land_shape, it.src.dtype) for it in items],
        mesh=plsc.ScalarSubcoreMesh(axis_name="sequencer", num_cores=1),
        scratch_types=[pltpu.SemaphoreType.DMA((n, N_DEV - 1)), pltpu.SemaphoreType.DMA((n, N_DEV - 1)),
                       pltpu.SemaphoreType.DMA((n,))],
        compiler_params=pltpu.CompilerParams(collective_id=collective_id),
    )(*[it.src for it in items])


TM = 1024
TM_ACC = 512
TN_IN = 768


def kernel(x, norm1_g, w_in, mix_conv_w, attn_out_g, conv_out_g, w_out, norm2_g, ffn_up, ffn_conv_w, ffn_down, final_norm_g, loss_target, m_norm1_g, m_w_in, m_mix_conv_w, m_attn_out_g, m_conv_out_g, m_w_out, m_norm2_g, m_ffn_up, m_ffn_conv_w, m_ffn_down, m_final_norm_g, v_norm1_g, v_w_in, v_mix_conv_w, v_attn_out_g, v_conv_out_g, v_w_out, v_norm2_g, v_ffn_up, v_ffn_conv_w, v_ffn_down, v_final_norm_g):
    nbatch, seq, d = x.shape
    t = nbatch * seq
    nt, nta = t // TM, t // TM_ACC
    out_rows = D_MODEL // N_DEV
    down_rows = D_FF // N_DEV
    xf = x.reshape(t, d)
    target = loss_target.reshape(t, d)

    cw_local = jnp.concatenate([ffn_conv_w, mix_conv_w], axis=-1)
    cast = lambda w: _Item(w.astype(BF16), False)
    cast_in = lambda w: _Item(w.astype(BF16), False, land_cols=True)
    cw_all, win0 = _sequencer_gather([_Item(cw_local, False), cast_in(w_in[0])], "gather_a", 0)
    up_t, m_up_t, v_up_t = (jnp.swapaxes(a, 1, 2) for a in (ffn_up, m_ffn_up, v_ffn_up))
    wout0, wup0 = _sequencer_gather([cast(w_out[0]), cast(up_t[0])], "gather_b", 1)
    (wdown0,) = _sequencer_gather([cast(ffn_down[0])], "gather_c", 2)
    win1, wout1 = _sequencer_gather([cast_in(w_in[1]), cast(w_out[1])], "gather_d", 3)
    wup1, wdown1 = _sequencer_gather([cast(up_t[1]), cast(ffn_down[1])], "gather_e", 7)
    win, wup = [win0, win1], [wup0, wup1]
    wout = [w.reshape(D_MODEL, D_MODEL) for w in (wout0, wout1)]
    wdown = [w.reshape(N_UP_PAIRS, UP_CHUNK, D_MODEL) for w in (wdown0, wdown1)]
    fcw = [cw_all[:, k, :, :UP_CHUNK].reshape(2, N_UP_PAIRS, 3, UP_CHUNK) for k in range(DEPTH)]
    mcw = [cw_all[:, k, :, UP_CHUNK:].transpose(1, 0, 2).reshape(3, D_CONV) for k in range(DEPTH)]

    full = lambda i, j, k: (0, 0)

    saved = []
    xin = xf
    h1 = _rms_fwd(xin, norm1_g[0][None], "rms1_fwd_0")
    rows_of = lambda width: pl.BlockSpec((TM_ACC, width), lambda i: (i, 0))
    whole = lambda *shape: pl.BlockSpec(shape, lambda i: (0,) * len(shape))
    for l in range(DEPTH):
        proj = _matmul(
            h1, win[l], grid=(nt, D_IN // TN_IN, 1), dims=NN, name=f"proj_{l}",
            a_spec=pl.BlockSpec((TM, D_MODEL), lambda i, j, k: (i, 0)),
            b_spec=pl.BlockSpec((D_MODEL, TN_IN), lambda i, j, k: (0, j)),
            o_spec=pl.BlockSpec((TM, TN_IN), lambda i, j, k: (i, j)), o_shape=(t, D_IN), o_dtype=F32)
        o, lse, cat = _attn_fwd(proj, attn_out_g[l][None], nbatch, seq)
        cat = _convmix_fwd(proj, cat, mcw[l], conv_out_g[l][None], nbatch, seq)
        xmid, h2 = _matmul_norm(cat, wout[l], xin, norm2_g[l][None], dims=NN, name=f"mix_out_{l}",
                                a_spec=rows_of(D_MODEL), b_spec=whole(D_MODEL, D_MODEL))
        if l + 1 < DEPTH:
            xout, h_next, act, pre = _ffn_fwd(
                h2, wup[l], fcw[l], wdown[l], xmid, norm1_g[l + 1][None], seq, f"ffn_fwd_{l}")
        else:
            h_next = None
            xout, act, pre = _ffn_fwd(h2, wup[l], fcw[l], wdown[l], xmid, None, seq, f"ffn_fwd_{l}")
        saved.append((xin, h1, proj, o, lse, cat, xmid, h2, pre, act))
        xin, h1 = xout, h_next

    loss_part, dx, dxb, dgf = _loss_head(xin, final_norm_g[None], target, "loss_head")

    dg1, dg2, dga, dgc = [None] * DEPTH, [None] * DEPTH, [None] * DEPTH, [None] * DEPTH
    for l in reversed(range(DEPTH)):
        xin, h1, proj, o, lse, cat, xmid, h2, pre, act = saved[l]
        g_down = _matmul(
            act, dxb, grid=(N_UP_PAIRS, 1, 1), dims=TN, name=f"g_down_{l}",
            a_spec=pl.BlockSpec((None, t, UP_CHUNK), lambda i, j, k: (i, 0, 0)),
            b_spec=pl.BlockSpec((t, D_MODEL), full),
            o_spec=pl.BlockSpec((None, UP_CHUNK, D_MODEL), lambda i, j, k: (i, 0, 0)),
            o_shape=(N_UP_PAIRS, UP_CHUNK, D_MODEL), o_dtype=BF16).reshape(N_DEV, down_rows, D_MODEL)
        d_pre, d_fcw, dxm, dxmb, dg2[l] = _ffn_up_bwd(
            pre, dxb, fcw[l], wdown[l], wup[l], xmid, norm2_g[l][None], dx, seq, f"ffn_bwd_{l}")
        d_pre = d_pre.reshape(N_DEV, t, UP_CHUNK)
        g_up = _matmul(
            d_pre, h2, grid=(N_DEV, 1, 1), dims=TN, name=f"g_up_{l}",
            a_spec=pl.BlockSpec((None, t, UP_CHUNK), lambda i, j, k: (i, 0, 0)),
            b_spec=pl.BlockSpec((t, D_MODEL), full),
            o_spec=pl.BlockSpec((None, UP_CHUNK, D_MODEL), lambda i, j, k: (i, 0, 0)),
            o_shape=(N_DEV, UP_CHUNK, D_MODEL), o_dtype=BF16)
        g_out = _matmul(
            cat, dxmb, grid=(1, 1, nt), dims=TN, name=f"g_out_{l}",
            a_spec=pl.BlockSpec((TM, D_MODEL), lambda i, j, k: (k, 0)),
            b_spec=pl.BlockSpec((TM, D_MODEL), lambda i, j, k: (k, 0)),
            o_spec=pl.BlockSpec((D_MODEL, D_MODEL), full),
            o_shape=(D_MODEL, D_MODEL), o_dtype=BF16).reshape(N_DEV, out_rows, D_MODEL)
        if l == 0:
            land_out0, land_up0, land_down0 = _sequencer_exchange(
                [_Item(g_out, True), _Item(g_up, True), _Item(g_down, True)], "scatter_0a", 5)
        d_cat = _matmul(
            dxmb, wout[l], grid=(nta, 1, 1), dims=NT, name=f"d_cat_{l}",
            a_spec=pl.BlockSpec((TM_ACC, D_MODEL), lambda i, j, k: (i, 0)),
            b_spec=pl.BlockSpec((D_MODEL, D_MODEL), full),
            o_spec=pl.BlockSpec((TM_ACC, D_MODEL), lambda i, j, k: (i, 0)), o_shape=(t, D_MODEL), o_dtype=BF16)
        d_proj, dga[l] = _attn_bwd(proj, o, lse, d_cat, attn_out_g[l][None], nbatch, seq)
        d_proj, d_mcw, dgc[l] = _convmix_bwd(proj, d_cat, d_proj, mcw[l], conv_out_g[l][None], nbatch, seq)
        g_in = _matmul(
            h1, d_proj, grid=(1, D_IN // TN_IN, 1), dims=TN, name=f"g_in_{l}",
            a_spec=pl.BlockSpec((t, D_MODEL), full),
            b_spec=pl.BlockSpec((t, TN_IN), lambda i, j, k: (0, j)),
            o_spec=pl.BlockSpec((D_MODEL, TN_IN), lambda i, j, k: (0, j)),
            o_shape=(D_MODEL, D_IN), o_dtype=BF16)
        g_cw = jnp.concatenate(
            [d_fcw.reshape(N_DEV, 3, UP_CHUNK), d_mcw.reshape(3, N_DEV, D_CONV // N_DEV).transpose(1, 0, 2)], axis=-1)
        if l == 0:
            land_in0, land_cw0 = _sequencer_exchange([_Item(g_in, "cols"), _Item(g_cw, True)], "scatter_0b", 6)
        else:
            land_in1, land_out1, land_up1, land_down1, land_cw1 = _sequencer_exchange(
                [_Item(g_in, "cols"), _Item(g_out, True), _Item(g_up, True), _Item(g_down, True), _Item(g_cw, True)],
                "scatter_1", 4)
        dx, dxb, dg1[l] = _matmul_norm_bwd(
            d_proj, win[l], xin, norm1_g[l][None], dxm, dims=NT, name=f"d_h1_{l}",
            a_spec=rows_of(D_IN), b_spec=whole(D_MODEL, D_IN))

    def pack_small(n1, a, c, n2, f):
        return jnp.concatenate(
            [n1, n2, f[None], jnp.concatenate([a, c], axis=-1), jnp.zeros((1, D_MODEL), F32)], axis=0)[None]

    small = jnp.concatenate(
        [dg1[0], dg1[1], dg2[0], dg2[1], dgf,
         jnp.concatenate([dga[0], dgc[0]], axis=-1), jnp.concatenate([dga[1], dgc[1]], axis=-1),
         jnp.pad(loss_part, ((0, 0), (0, D_MODEL - LANES)))], axis=0)
    (land_small,) = _exchange([_Item(small, False)], "gather_gain_grads")
    res_small = _adamw(
        [land_small], pack_small(norm1_g, attn_out_g, conv_out_g, norm2_g, final_norm_g),
        pack_small(m_norm1_g, m_attn_out_g, m_conv_out_g, m_norm2_g, m_final_norm_g),
        pack_small(v_norm1_g, v_attn_out_g, v_conv_out_g, v_norm2_g, v_final_norm_g), SUBLANES, "adamw_gains")
    res_out = _adamw([land_out0, land_out1], w_out, m_w_out, v_w_out, out_rows, "adamw_w_out", after=[res_small[0]])
    res_up_t = _adamw([land_up0, land_up1], up_t, m_up_t, v_up_t, UP_CHUNK // 4, "adamw_ffn_up", after=[res_out[0]])
    res_up = [jnp.swapaxes(r, 1, 2) for r in res_up_t]
    res_down = _adamw([land_down0, land_down1], ffn_down, m_ffn_down, v_ffn_down, down_rows, "adamw_ffn_down",
                      after=[res_up_t[0]])
    res_in = _adamw([land_in0, land_in1], w_in, m_w_in, v_w_in, 256, "adamw_w_in", after=[res_down[0]])
    res_cw = _adamw(
        [land_cw0, land_cw1], cw_local, jnp.concatenate([m_ffn_conv_w, m_mix_conv_w], axis=-1),
        jnp.concatenate([v_ffn_conv_w, v_mix_conv_w], axis=-1), 3, "adamw_conv_w", after=[res_in[0]])

    loss = res_small[0][0, SUBLANES - 1, 0]

    def unpack(kind):
        s = res_small[kind][0]
        cwr = res_cw[kind]
        return (s[0:2], res_in[kind], cwr[..., UP_CHUNK:], s[5:7, :D_ATTN], s[5:7, D_ATTN:], res_out[kind],
                s[2:4], res_up[kind], cwr[..., :UP_CHUNK], res_down[kind], s[4])

    return (loss, dx.reshape(nbatch, seq, d), *unpack(0), *unpack(1), *unpack(2), *unpack(3))
```

```python
import math

import jax
import jax.numpy as jnp
from jax import lax
from jax.experimental import pallas as pl
from jax.experimental.pallas import tpu as pltpu
from jax.experimental.pallas import tpu_sc as plsc

F32 = jnp.float32
BF16 = jnp.bfloat16

D_MODEL = 1024
D_ATTN = 512
D_CONV = 512
HEAD_DIM = 64
N_HEADS = 8
D_FF = 2816
DEPTH = 2
D_IN = 3 * D_ATTN + 3 * D_CONV
EPS = 1e-6
DILATIONS = (1, 4, 16)
BAND = 128
N_DEV = 8
UP_CHUNK = 2 * D_FF // N_DEV
N_UP_PAIRS = N_DEV // 2
CW_PACK = UP_CHUNK + D_CONV // N_DEV
ADAM_LR = 0.001
ADAM_B1 = 0.9
ADAM_B2 = 0.999
ADAM_EPS = 1e-08
ADAM_WD = 0.01
ADAM_STEP = 10
LANES = 128
SUBLANES = 8
VMEM_LIMIT = 56 * 1024 * 1024

NEG = -1e30
MESH = pl.DeviceIdType.MESH


def _params(sem=None, vmem=VMEM_LIMIT):
    return pltpu.CompilerParams(dimension_semantics=sem, vmem_limit_bytes=vmem)


NN = (((1,), (0,)), ((), ()))
NT = (((1,), (1,)), ((), ()))
TN = (((0,), (0,)), ((), ()))
TN_PIECE = 1024


def _contract(a_ref, b_ref, dims):
    def dot(av, bv):
        return lax.dot_general(av.astype(BF16), bv.astype(BF16), dims, preferred_element_type=F32)

    if len(a_ref.shape) == 2:
        if dims == TN and a_ref.shape[0] > TN_PIECE:
            part = None
            for r0 in range(0, a_ref.shape[0], TN_PIECE):
                piece = dot(a_ref[pl.ds(r0, TN_PIECE), :], b_ref[pl.ds(r0, TN_PIECE), :])
                part = piece if part is None else part + piece
            return part
        return dot(a_ref[...], b_ref[...])
    part = dot(a_ref[0], b_ref[0])
    for c in range(1, a_ref.shape[0]):
        part = part + dot(a_ref[c], b_ref[c])
    return part


def _matmul(a, b, *, grid, a_spec, b_spec, o_spec, o_shape, o_dtype, dims, name, res=None, res_spec=None, after=()):
    nk = grid[2]
    o_block = tuple(s for s in o_spec.block_shape if s is not None)
    na = len(after)

    def body(*refs):
        refs = refs[:2 + (res is not None)] + refs[2 + (res is not None) + na:]
        if res is None:
            a_ref, b_ref, o_ref, *scr = refs
            r_ref = None
        else:
            a_ref, b_ref, r_ref, o_ref, *scr = refs
        part = _contract(a_ref, b_ref, dims)

        def finish(total):
            if r_ref is not None:
                total = total + r_ref[...]
            o_ref[...] = total.astype(o_dtype)

        if nk == 1:
            finish(part)
        else:
            acc = scr[0]
            k = pl.program_id(2)

            @pl.when(k == 0)
            def _():
                acc[...] = part

            @pl.when(k > 0)
            def _():
                acc[...] += part

            @pl.when(k == nk - 1)
            def _():
                finish(acc[...])

    in_specs = [a_spec, b_spec] + ([res_spec] if res is not None else []) + [pl.BlockSpec(memory_space=pl.ANY)] * na
    args = (a, b) + ((res,) if res is not None else ()) + tuple(after)
    return pl.pallas_call(
        body, name=name, grid=grid, in_specs=in_specs, out_specs=o_spec,
        out_shape=jax.ShapeDtypeStruct(o_shape, o_dtype),
        scratch_shapes=[pltpu.VMEM(o_block, F32)] if nk > 1 else [],
        compiler_params=_params(("parallel", "parallel", "arbitrary")),
    )(*args)


ROW_TILE = 512


def _rms_fwd(x, g, name):
    t, d = x.shape

    def body(x_ref, g_ref, h_ref):
        xv = x_ref[...]
        r = lax.rsqrt(jnp.mean(xv * xv, axis=-1, keepdims=True) + EPS)
        h_ref[...] = (xv * r * g_ref[...]).astype(BF16)

    return pl.pallas_call(
        body, name=name, grid=(t // ROW_TILE,),
        in_specs=[pl.BlockSpec((ROW_TILE, d), lambda i: (i, 0)), pl.BlockSpec((1, d), lambda i: (0, 0))],
        out_specs=pl.BlockSpec((ROW_TILE, d), lambda i: (i, 0)),
        out_shape=jax.ShapeDtypeStruct((t, d), BF16),
        compiler_params=_params(("parallel",)),
    )(x, g)


def _matmul_norm(a, b, res, g, *, a_spec, b_spec, dims, name):
    t, d = res.shape

    def body(a_ref, b_ref, r_ref, g_ref, x_ref, h_ref):
        xv = _contract(a_ref, b_ref, dims) + r_ref[...]
        x_ref[...] = xv
        h_ref[...] = (xv * lax.rsqrt(jnp.mean(xv * xv, axis=-1, keepdims=True) + EPS) * g_ref[...]).astype(BF16)

    row = pl.BlockSpec((TM_ACC, d), lambda i: (i, 0))
    return pl.pallas_call(
        body, name=name, grid=(t // TM_ACC,),
        in_specs=[a_spec, b_spec, row, pl.BlockSpec((1, d), lambda i: (0, 0))], out_specs=[row, row],
        out_shape=[jax.ShapeDtypeStruct((t, d), F32), jax.ShapeDtypeStruct((t, d), BF16)],
        compiler_params=_params(("parallel",)),
    )(a, b, res, g)


def _matmul_norm_bwd(a, b, x, g, dres, *, a_spec, b_spec, dims, name):
    t, d = x.shape

    def body(a_ref, b_ref, x_ref, g_ref, dres_ref, dx_ref, dxb_ref, dg_ref):
        dhv = _contract(a_ref, b_ref, dims)
        xv = x_ref[...]
        r = lax.rsqrt(jnp.mean(xv * xv, axis=-1, keepdims=True) + EPS)
        xh = xv * r
        gd = dhv * g_ref[...]
        dx = r * (gd - xh * jnp.mean(gd * xh, axis=-1, keepdims=True)) + dres_ref[...]
        dx_ref[...] = dx
        dxb_ref[...] = dx.astype(BF16)
        part = jnp.sum(dhv * xh, axis=0, keepdims=True)

        @pl.when(pl.program_id(0) == 0)
        def _():
            dg_ref[...] = part

        @pl.when(pl.program_id(0) > 0)
        def _():
            dg_ref[...] += part

    row = pl.BlockSpec((TM_ACC, d), lambda i: (i, 0))
    vec = pl.BlockSpec((1, d), lambda i: (0, 0))
    return pl.pallas_call(
        body, name=name, grid=(t // TM_ACC,),
        in_specs=[a_spec, b_spec, row, vec, row], out_specs=[row, row, vec],
        out_shape=[jax.ShapeDtypeStruct((t, d), F32), jax.ShapeDtypeStruct((t, d), BF16),
                   jax.ShapeDtypeStruct((1, d), F32)],
        compiler_params=_params(("arbitrary",)),
    )(a, b, x, g, dres)


def _loss_head(x, g, target, name):
    t, d = x.shape

    def body(x_ref, g_ref, t_ref, loss_ref, dx_ref, dxb_ref, dg_ref):
        xv = x_ref[...]
        r = lax.rsqrt(jnp.mean(xv * xv, axis=-1, keepdims=True) + EPS)
        xh = xv * r
        gv = g_ref[...]
        err = xh * gv - t_ref[...]
        loss = jnp.full((1, LANES), 0.5 / d, F32) * jnp.sum(err * err)
        dy = err * (1.0 / d)
        gd = dy * gv
        dx = r * (gd - xh * jnp.mean(gd * xh, axis=-1, keepdims=True))
        dx_ref[...] = dx
        dxb_ref[...] = dx.astype(BF16)
        part = jnp.sum(dy * xh, axis=0, keepdims=True)

        @pl.when(pl.program_id(0) == 0)
        def _():
            dg_ref[...] = part
            loss_ref[...] = loss

        @pl.when(pl.program_id(0) > 0)
        def _():
            dg_ref[...] += part
            loss_ref[...] += loss

    row = pl.BlockSpec((ROW_TILE, d), lambda i: (i, 0))
    vec = pl.BlockSpec((1, d), lambda i: (0, 0))
    return pl.pallas_call(
        body, name=name, grid=(t // ROW_TILE,),
        in_specs=[row, vec, row],
        out_specs=[pl.BlockSpec((1, LANES), lambda i: (0, 0)), row, row, vec],
        out_shape=[jax.ShapeDtypeStruct((1, LANES), F32), jax.ShapeDtypeStruct((t, d), F32),
                   jax.ShapeDtypeStruct((t, d), BF16), jax.ShapeDtypeStruct((1, d), F32)],
        compiler_params=_params(("arbitrary",)),
    )(x, g, target)


def _group_matrix(n):
    shift = int(math.log2(HEAD_DIM))
    r = lax.broadcasted_iota(jnp.int32, (n, n), 0) >> shift
    c = lax.broadcasted_iota(jnp.int32, (n, n), 1) >> shift
    return (r == c).astype(BF16)


def _group_sum(v, gmat):
    hi = v.astype(BF16)
    lo = (v - hi.astype(F32)).astype(BF16)

    def dot(p):
        return jnp.dot(p, gmat, preferred_element_type=F32)

    return dot(hi) + dot(lo)


def _shift_rows(ext, k):
    return pltpu.roll(ext, k % ext.shape[0], 0)


def _store_columns(stage, out_hbm, sems, row0, nrows, col_blocks):
    rows = pl.ds(pl.multiple_of(row0, SUBLANES * 2), nrows)
    copies = [
        pltpu.make_async_copy(stage.at[i], out_hbm.at[rows, pl.ds(pl.multiple_of(cb * LANES, LANES), LANES)], sems.at[i])
        for i, cb in enumerate(col_blocks)
    ]
    for cp in copies:
        cp.start()
    for cp in copies:
        cp.wait()


def _attn_consts(width):
    i = lax.broadcasted_iota(jnp.int32, (BAND, width), 0)
    j = lax.broadcasted_iota(jnp.int32, (BAND, width), 1)
    dist = (width - BAND) + i - j
    inwin = (dist >= 0) & (dist <= BAND)
    return dist.astype(F32), inwin, j


def _head_masks():
    lane = lax.broadcasted_iota(jnp.int32, (1, LANES), 1)
    return [(lane < HEAD_DIM).astype(F32), (lane >= HEAD_DIM).astype(F32)]


def _pair_bias(slope, dil):
    distf, inwin, _ = _attn_consts(2 * BAND)
    return jnp.concatenate([jnp.where(inwin, distf * (slope[hh] * (-float(dil))), NEG) for hh in range(2)], axis=0)


def _stack_heads(xv, hmask):
    return jnp.concatenate([xv * hmask[0], xv * hmask[1]], axis=0).astype(BF16)


FWD_UNROLL = 16
BWD_UNROLL = 16


def _unroll(trips, most):
    return max(u for u in range(1, most + 1) if trips % u == 0)


def _for_blocks(seq, dil, block, most):
    nb = seq // dil // BAND

    def residue(r, carry):
        base = r * nb
        block(pl.multiple_of(base * BAND, BAND), None)
        if nb > 1:
            def rest(n, c):
                block(pl.multiple_of((base + n) * BAND, BAND), pl.multiple_of((base + n - 1) * BAND, BAND))
                return c

            lax.fori_loop(1, nb, rest, 0, unroll=_unroll(nb - 1, most))
        return carry

    if dil == 1:
        residue(0, 0)
    else:
        lax.fori_loop(0, dil, residue, 0, unroll=_unroll(dil, max(1, most // nb)))


def _permute_in(src_ref, dst_ref, dil, seq):
    length = seq // dil
    for r in range(dil):
        dst_ref[pl.ds(r * length, length), :] = src_ref[pl.ds(r, length, stride=dil), :].astype(dst_ref.dtype)


def _slopes_table():
    slopes = 2.0 ** (-8.0 * jnp.arange(1, N_HEADS + 1, dtype=F32) / N_HEADS)
    return jnp.broadcast_to(slopes[:, None], (N_HEADS, 2 * BAND))


def _attn_fwd(proj, attn_g, nbatch, seq):
    t = nbatch * seq
    scale = HEAD_DIM ** -0.5

    def body(q_ref, k_ref, v_ref, g_ref, sl_ref, o_ref, lse_ref, cat_ref, pq, pk, pv, po, pm, pll, ao, am, al):
        hp = pl.program_id(1)
        hmask = _head_masks()
        slope = [sl_ref[pl.ds(2 * hp + hh, 1), :] for hh in range(2)]

        def run_branch(dil, qs, ks, vs, osink, msink, lsink):
            bias = _pair_bias(slope, dil)

            def block(row0, prow):
                cur = pl.ds(row0, BAND)
                q2 = _stack_heads(qs[cur, :] * scale, hmask)
                if prow is None:
                    kk, vv, bias_b = ks[cur, :], vs[cur, :], bias[:, BAND:]
                else:
                    prev = pl.ds(prow, BAND)
                    kk = jnp.concatenate([ks[prev, :], ks[cur, :]], axis=0)
                    vv = jnp.concatenate([vs[prev, :], vs[cur, :]], axis=0)
                    bias_b = bias
                s = lax.dot_general(q2, kk.astype(BF16), NT, preferred_element_type=F32) + bias_b
                m = jnp.max(s, axis=1, keepdims=True)
                p = jnp.exp(s - m)
                l = jnp.sum(p, axis=1, keepdims=True)
                pb = p.astype(BF16)
                o = jnp.dot(jnp.concatenate([pb[:BAND], pb[BAND:]], axis=1), _stack_heads(vv, hmask),
                            preferred_element_type=F32)
                osink[cur, :] = o
                msink[cur, :] = m[:BAND] * hmask[0] + m[BAND:] * hmask[1]
                lsink[cur, :] = l[:BAND] * hmask[0] + l[BAND:] * hmask[1]

            _for_blocks(seq, dil, block, FWD_UNROLL)

        run_branch(1, q_ref, k_ref, v_ref, ao, am, al)
        for dil in DILATIONS[1:]:
            length = seq // dil
            _permute_in(q_ref, pq, dil, seq)
            _permute_in(k_ref, pk, dil, seq)
            _permute_in(v_ref, pv, dil, seq)
            run_branch(dil, pq, pk, pv, po, pm, pll)
            for r in range(dil):
                nat = pl.ds(r, length, stride=dil)
                per = pl.ds(r * length, length)
                m0 = am[nat, :]
                mb = pm[per, :]
                mn = jnp.maximum(m0, mb)
                e0 = jnp.exp(m0 - mn)
                eb = jnp.exp(mb - mn)
                ao[nat, :] = ao[nat, :] * e0 + po[per, :] * eb
                al[nat, :] = al[nat, :] * e0 + pll[per, :] * eb
                am[nat, :] = mn

        gmat = _group_matrix(LANES)
        gv = g_ref[...]

        def fin(c, carry):
            rows = pl.ds(pl.multiple_of(c * 256, 256), 256)
            lv = al[rows, :]
            o = ao[rows, :] / lv
            o_ref[rows, :] = o
            lse_ref[rows, :] = am[rows, :] + jnp.log(lv)
            ms = _group_sum(o * o, gmat) * (1.0 / HEAD_DIM)
            cat_ref[rows, :] = (o * lax.rsqrt(ms + EPS) * gv).astype(BF16)
            return carry

        lax.fori_loop(0, seq // 256, fin, 0, unroll=True)

    nq = D_ATTN // LANES
    blk = lambda off: pl.BlockSpec((seq, LANES), lambda b, h: (b, h + off))
    scratch = [pltpu.VMEM((seq, LANES), F32) for _ in range(9)]
    return pl.pallas_call(
        body, name="attn_fwd", grid=(nbatch, nq),
        in_specs=[blk(0), blk(nq), blk(2 * nq), pl.BlockSpec((1, LANES), lambda b, h: (0, h)),
                  pl.BlockSpec((N_HEADS, 2 * BAND), lambda b, h: (0, 0))],
        out_specs=[blk(0), blk(0), blk(0)],
        out_shape=[jax.ShapeDtypeStruct((t, D_ATTN), F32), jax.ShapeDtypeStruct((t, D_ATTN), F32),
                   jax.ShapeDtypeStruct((t, D_MODEL), BF16)],
        scratch_shapes=scratch,
        compiler_params=_params(("parallel", "parallel")),
    )(proj, proj, proj, attn_g, _slopes_table())


def _attn_bwd(proj, o, lse, d_cat, attn_g, nbatch, seq):
    t = nbatch * seq
    scale = HEAD_DIM ** -0.5

    def body(q_ref, k_ref, v_ref, o_ref, lse_ref, dy_ref, g_ref, sl_ref, dproj_ref, dg_ref,
             do_n, dl_n, dq_n, dk_n, dv_n, pq, pk, pv, pdo, plse, pdl, pdq, pdk, pdv, stage, sems):
        hp = pl.program_id(0)
        hmask = _head_masks()
        slope = [sl_ref[pl.ds(2 * hp + hh, 1), :] for hh in range(2)]
        gmat = _group_matrix(LANES)
        gv = g_ref[...]

        def prep(c, dg):
            rows = pl.ds(pl.multiple_of(c * 256, 256), 256)
            ov = o_ref[rows, :]
            dyn = dy_ref[rows, :].astype(F32)
            r = lax.rsqrt(_group_sum(ov * ov, gmat) * (1.0 / HEAD_DIM) + EPS)
            gd = dyn * gv
            oh = ov * r
            do = r * (gd - oh * (_group_sum(gd * oh, gmat) * (1.0 / HEAD_DIM)))
            do_n[rows, :] = do
            dl_n[rows, :] = _group_sum(do * ov, gmat)
            return dg + jnp.sum(dyn * oh, axis=0, keepdims=True)

        dg = lax.fori_loop(0, seq // 256, prep, jnp.zeros((1, LANES), F32), unroll=True)

        @pl.when(pl.program_id(1) == 0)
        def _():
            dg_ref[...] = dg

        @pl.when(pl.program_id(1) > 0)
        def _():
            dg_ref[...] += dg

        def clear(*refs):
            def step(c, carry):
                rows = pl.ds(pl.multiple_of(c * 256, 256), 256)
                for ref in refs:
                    ref[rows, :] = jnp.zeros((256, LANES), F32)
                return carry

            lax.fori_loop(0, seq // 256, step, 0)

        clear(dq_n, dk_n, dv_n)

        def run_branch(dil, qs, ks, vs, dos, lses, dls, dqs, dks, dvs):
            bias = _pair_bias(slope, dil)

            def per_head(xv):
                return jnp.concatenate([xv[:, 0:1], xv[:, HEAD_DIM:HEAD_DIM + 1]], axis=0)

            def block(row0, prow):
                cur = pl.ds(row0, BAND)
                keys = cur if prow is None else pl.ds(prow, 2 * BAND)
                q2 = _stack_heads(qs[cur, :] * scale, hmask)
                do2 = _stack_heads(dos[cur, :], hmask)
                kk, vv = ks[keys, :], vs[keys, :]
                s = lax.dot_general(q2, kk.astype(BF16), NT, preferred_element_type=F32)
                s = s + (bias[:, BAND:] if prow is None else bias)
                p = jnp.exp(s - per_head(lses[cur, :]))
                dp = lax.dot_general(do2, vv.astype(BF16), NT, preferred_element_type=F32)
                ds = (p * (dp - per_head(dls[cur, :]))).astype(BF16)
                dqs[cur, :] += jnp.dot(jnp.concatenate([ds[:BAND], ds[BAND:]], axis=1), _stack_heads(kk, hmask),
                                       preferred_element_type=F32)
                dks[keys, :] += lax.dot_general(ds, q2, TN, preferred_element_type=F32)
                dvs[keys, :] += lax.dot_general(p.astype(BF16), do2, TN, preferred_element_type=F32)

            _for_blocks(seq, dil, block, BWD_UNROLL)

        run_branch(1, q_ref, k_ref, v_ref, do_n, lse_ref, dl_n, dq_n, dk_n, dv_n)
        for dil in DILATIONS[1:]:
            length = seq // dil
            for src, dst in ((q_ref, pq), (k_ref, pk), (v_ref, pv), (do_n, pdo), (lse_ref, plse), (dl_n, pdl)):
                _permute_in(src, dst, dil, seq)
            clear(pdq, pdk, pdv)
            run_branch(dil, pq, pk, pv, pdo, plse, pdl, pdq, pdk, pdv)
            for r in range(dil):
                nat = pl.ds(r, length, stride=dil)
                per = pl.ds(r * length, length)
                dq_n[nat, :] += pdq[per, :]
                dk_n[nat, :] += pdk[per, :]
                dv_n[nat, :] += pdv[per, :]

        def emit(c, carry):
            rows = pl.ds(pl.multiple_of(c * 256, 256), 256)
            stage[0, rows, :] = (dq_n[rows, :] * scale).astype(BF16)
            stage[1, rows, :] = dk_n[rows, :].astype(BF16)
            stage[2, rows, :] = dv_n[rows, :].astype(BF16)
            return carry

        lax.fori_loop(0, seq // 256, emit, 0)
        _store_columns(stage, dproj_ref, sems, pl.program_id(1) * seq, seq, [hp, nq + hp, 2 * nq + hp])

    nq = D_ATTN // LANES
    blk = lambda off: pl.BlockSpec((seq, LANES), lambda h, b: (b, h + off))
    vec = pl.BlockSpec((1, LANES), lambda h, b: (0, h))
    scratch = [pltpu.VMEM((seq, LANES), F32) for _ in range(14)]
    scratch += [pltpu.VMEM((3, seq, LANES), BF16), pltpu.SemaphoreType.DMA((3,))]
    d_proj, dg = pl.pallas_call(
        body, name="attn_bwd", grid=(nq, nbatch),
        in_specs=[blk(0), blk(nq), blk(2 * nq), blk(0), blk(0), blk(0), vec,
                  pl.BlockSpec((N_HEADS, 2 * BAND), lambda h, b: (0, 0))],
        out_specs=[pl.BlockSpec(memory_space=pl.ANY), vec],
        out_shape=[jax.ShapeDtypeStruct((t, D_IN), BF16), jax.ShapeDtypeStruct((1, D_ATTN), F32)],
        scratch_shapes=scratch,
        compiler_params=_params(("arbitrary", "arbitrary")),
    )(proj, proj, proj, o, lse, d_cat, attn_g, _slopes_table())
    return d_proj, dg


HALO = SUBLANES
PACKED_ROWS = 2 * SUBLANES


def _window(ref, c, rows, nchunks, after):
    row0 = pl.multiple_of(c * rows, rows)
    prev0 = pl.multiple_of(jnp.maximum(row0 - PACKED_ROWS, 0), PACKED_ROWS)
    before = ref[pl.ds(prev0, PACKED_ROWS), :].astype(F32)[PACKED_ROWS - HALO:] * (c > 0).astype(F32)
    parts = [before, ref[pl.ds(row0, rows), :].astype(F32)]
    if after:
        next0 = pl.multiple_of(jnp.minimum(row0 + rows, (nchunks - 1) * rows), PACKED_ROWS)
        parts.append(ref[pl.ds(next0, PACKED_ROWS), :].astype(F32)[:HALO] * (c < nchunks - 1).astype(F32))
    return jnp.concatenate(parts, axis=0)


def _behind(z):
    z1 = _shift_rows(z, 1)
    return z1, _shift_rows(z1, 1)


def _ahead(dy):
    d1 = _shift_rows(dy, -1)
    return d1, _shift_rows(d1, -1)


def _conv(z, w):
    z1, z2 = _behind(z)
    return w[0:1] * z2 + w[1:2] * z1 + w[2:3] * z


def _conv_bwd(dy, z, w, cur):
    d1, d2 = _ahead(dy)
    dz = w[2:3] * dy + w[1:2] * d1 + w[0:1] * d2
    return dz, [jnp.sum((d * z)[cur], axis=0, keepdims=True) for d in (d2, d1, dy)]


def _sigmoid(a):
    return 0.5 * jnp.tanh(0.5 * a) + 0.5


MIX_ROWS = 256
GATE_B_BLOCK = 3 * D_ATTN // LANES
GATE_C_BLOCK = GATE_B_BLOCK + D_CONV // LANES
U_BLOCK = GATE_C_BLOCK + D_CONV // LANES


def _convmix_fwd(proj, cat, mcw, conv_g, nbatch, seq):
    nchunks = seq // MIX_ROWS

    def body(gb_ref, gc_ref, u_ref, w_ref, g_ref, cat_in, cat_ref):
        del cat_in
        gmat = _group_matrix(LANES)
        w = w_ref[...]
        gv = g_ref[...]

        def step(c, carry):
            cur = pl.ds(pl.multiple_of(c * MIX_ROWS, MIX_ROWS), MIX_ROWS)
            z = _window(gc_ref, c, MIX_ROWS, nchunks, False) * _window(u_ref, c, MIX_ROWS, nchunks, False)
            y = gb_ref[cur, :] * _conv(z, w)[HALO:]
            ms = _group_sum(y * y, gmat) * (1.0 / HEAD_DIM)
            cat_ref[cur, :] = (y * lax.rsqrt(ms + EPS) * gv).astype(BF16)
            return carry

        lax.fori_loop(0, nchunks, step, 0, unroll=True)

    nc = D_CONV // LANES
    blk = lambda off: pl.BlockSpec((seq, LANES), lambda b, j: (b, j + off))
    return pl.pallas_call(
        body, name="convmix_fwd", grid=(nbatch, nc),
        in_specs=[blk(GATE_B_BLOCK), blk(GATE_C_BLOCK), blk(U_BLOCK),
                  pl.BlockSpec((3, LANES), lambda b, j: (0, j)), pl.BlockSpec((1, LANES), lambda b, j: (0, j)),
                  pl.BlockSpec(memory_space=pl.ANY)],
        out_specs=blk(D_ATTN // LANES),
        out_shape=jax.ShapeDtypeStruct(cat.shape, cat.dtype),
        input_output_aliases={5: 0},
        compiler_params=_params(("parallel", "parallel")),
    )(proj, proj, proj, mcw, conv_g, cat)


def _convmix_bwd(proj, d_cat, d_proj, mcw, conv_g, nbatch, seq):
    nchunks = seq // MIX_ROWS

    def body(gb_ref, gc_ref, u_ref, dy_ref, w_ref, g_ref, dproj_in, dproj_ref, dw_ref, dg_ref, stage, sems):
        del dproj_in
        cb = pl.program_id(0)
        b = pl.program_id(1)
        gmat = _group_matrix(LANES)
        w = w_ref[...]
        gv = g_ref[...]
        cur = slice(HALO, HALO + MIX_ROWS)

        def step(c, carry):
            rows = pl.ds(pl.multiple_of(c * MIX_ROWS, MIX_ROWS), MIX_ROWS)
            gb = _window(gb_ref, c, MIX_ROWS, nchunks, True)
            gc = _window(gc_ref, c, MIX_ROWS, nchunks, True)
            u = _window(u_ref, c, MIX_ROWS, nchunks, True)
            dyn = _window(dy_ref, c, MIX_ROWS, nchunks, True)
            z = gc * u
            conv = _conv(z, w)
            y = gb * conv
            r = lax.rsqrt(_group_sum(y * y, gmat) * (1.0 / HEAD_DIM) + EPS)
            yh = y * r
            gd = dyn * gv
            dy = r * (gd - yh * (_group_sum(gd * yh, gmat) * (1.0 / HEAD_DIM)))
            dz, dws = _conv_bwd(dy * gb, z, w, cur)
            stage[0, rows, :] = (dy * conv)[cur].astype(BF16)
            stage[1, rows, :] = (dz * u)[cur].astype(BF16)
            stage[2, rows, :] = (dz * gc)[cur].astype(BF16)
            dg = jnp.sum((dyn * yh)[cur], axis=0, keepdims=True)
            return tuple(a + d for a, d in zip(carry, dws + [dg]))

        zero = jnp.zeros((1, LANES), F32)
        dw0, dw1, dw2, dg = lax.fori_loop(0, nchunks, step, (zero, zero, zero, zero), unroll=True)

        @pl.when(b == 0)
        def _():
            dw_ref[0:1, :] = dw0
            dw_ref[1:2, :] = dw1
            dw_ref[2:3, :] = dw2
            dg_ref[...] = dg

        @pl.when(b > 0)
        def _():
            dw_ref[0:1, :] += dw0
            dw_ref[1:2, :] += dw1
            dw_ref[2:3, :] += dw2
            dg_ref[...] += dg

        _store_columns(stage, dproj_ref, sems, b * seq, seq, [GATE_B_BLOCK + cb, GATE_C_BLOCK + cb, U_BLOCK + cb])

    nc = D_CONV // LANES
    blk = lambda off: pl.BlockSpec((seq, LANES), lambda j, b: (b, j + off))
    return pl.pallas_call(
        body, name="convmix_bwd", grid=(nc, nbatch),
        in_specs=[blk(GATE_B_BLOCK), blk(GATE_C_BLOCK), blk(U_BLOCK), blk(D_ATTN // LANES),
                  pl.BlockSpec((3, LANES), lambda j, b: (0, j)), pl.BlockSpec((1, LANES), lambda j, b: (0, j)),
                  pl.BlockSpec(memory_space=pl.ANY)],
        out_specs=[pl.BlockSpec(memory_space=pl.ANY), pl.BlockSpec((3, LANES), lambda j, b: (0, j)),
                   pl.BlockSpec((1, LANES), lambda j, b: (0, j))],
        out_shape=[jax.ShapeDtypeStruct(d_proj.shape, d_proj.dtype), jax.ShapeDtypeStruct((3, D_CONV), F32),
                   jax.ShapeDtypeStruct((1, D_CONV), F32)],
        scratch_shapes=[pltpu.VMEM((3, seq, LANES), BF16), pltpu.SemaphoreType.DMA((3,))],
        input_output_aliases={6: 0},
        compiler_params=_params(("arbitrary", "arbitrary")),
    )(proj, proj, proj, d_cat, mcw, conv_g, d_proj)


FFN_ROWS = 256
FFN_FWD_ROWS = 256


def _ffn_fwd(h, wup, fcw, wdown, res, g, seq, name):
    t, d = res.shape
    tiles_per_seq = seq // FFN_FWD_ROWS

    def body(hm_ref, hp_ref, wu_ref, w_ref, wd_ref, r_ref, *rest):
        if g is None:
            x_ref, act_ref, pre_ref = rest
        else:
            g_ref, x_ref, h_ref, act_ref, pre_ref = rest
        inside = ((pl.program_id(0) % tiles_per_seq) > 0).astype(F32)
        wrow = lax.broadcasted_iota(jnp.int32, (FFN_FWD_ROWS + HALO, 1), 0)
        edge = jnp.where(wrow < HALO, inside, 1.0)
        rows = jnp.concatenate([hp_ref[...], hm_ref[...]], axis=0)

        def up(j, part, p):
            full = lax.dot_general(rows, wu_ref[j], NT, preferred_element_type=F32).astype(BF16)
            pre_ref[part, p] = full[PACKED_ROWS:]
            return full.astype(F32)[PACKED_ROWS - HALO:] * edge

        total = r_ref[...]
        for p in range(N_UP_PAIRS):
            a = _conv(up(p, 0, p), w_ref[0, p])[HALO:]
            v = _conv(up(N_UP_PAIRS + p, 1, p), w_ref[1, p])[HALO:]
            act = (a * _sigmoid(a) * v).astype(BF16)
            act_ref[p] = act
            total = total + jnp.dot(act, wd_ref[p], preferred_element_type=F32)
        x_ref[...] = total
        if g is not None:
            h_ref[...] = (total * lax.rsqrt(jnp.mean(total * total, axis=-1, keepdims=True) + EPS) * g_ref[...]).astype(BF16)

    row = pl.BlockSpec((FFN_FWD_ROWS, d), lambda i: (i, 0))
    tiles_per_halo = FFN_FWD_ROWS // PACKED_ROWS
    in_specs = [
        row, pl.BlockSpec((PACKED_ROWS, d), lambda i: (jnp.maximum(i * tiles_per_halo - 1, 0), 0)),
        pl.BlockSpec((N_DEV, UP_CHUNK, d), lambda i: (0, 0, 0)),
        pl.BlockSpec((2, N_UP_PAIRS, 3, UP_CHUNK), lambda i: (0, 0, 0, 0)),
        pl.BlockSpec((N_UP_PAIRS, UP_CHUNK, d), lambda i: (0, 0, 0)), row]
    out_specs = [row]
    out_shape = [jax.ShapeDtypeStruct((t, d), F32)]
    args = [h, h, wup, fcw, wdown, res]
    if g is not None:
        in_specs.append(pl.BlockSpec((1, d), lambda i: (0, 0)))
        out_specs.append(row)
        out_shape.append(jax.ShapeDtypeStruct((t, d), BF16))
        args.append(g)
    out_specs += [pl.BlockSpec((N_UP_PAIRS, FFN_FWD_ROWS, UP_CHUNK), lambda i: (0, i, 0)),
                  pl.BlockSpec((2, N_UP_PAIRS, FFN_FWD_ROWS, UP_CHUNK), lambda i: (0, 0, i, 0))]
    out_shape += [jax.ShapeDtypeStruct((N_UP_PAIRS, t, UP_CHUNK), BF16),
                  jax.ShapeDtypeStruct((2, N_UP_PAIRS, t, UP_CHUNK), BF16)]
    return pl.pallas_call(
        body, name=name, grid=(t // FFN_FWD_ROWS,), in_specs=in_specs, out_specs=out_specs, out_shape=out_shape,
        compiler_params=_params(("parallel",)),
    )(*args)


def _ffn_up_bwd(pre, dy, fcw, wdown, wup, x, g, dres, seq, name):
    t, d = x.shape
    tiles_per_seq = seq // FFN_ROWS
    tiles_per_halo = FFN_ROWS // PACKED_ROWS
    last_halo = t // PACKED_ROWS - 1

    def body(pm_ref, pp_ref, pn_ref, dm_ref, dp_ref, dn_ref, w_ref, wd_ref, wu_ref, x_ref, g_ref, dres_ref,
             dpre_ref, dw_ref, dx_ref, dxb_ref, dg_ref):
        i = pl.program_id(0)
        has_prev = ((i % tiles_per_seq) > 0).astype(F32)
        has_next = ((i % tiles_per_seq) < tiles_per_seq - 1).astype(F32)
        cur = slice(HALO, HALO + FFN_ROWS)

        def window(before, main, after):
            return jnp.concatenate([before.astype(F32)[PACKED_ROWS - HALO:] * has_prev, main.astype(F32),
                                    after.astype(F32)[:HALO] * has_next], axis=0)

        dy_rows = jnp.concatenate([dp_ref[...], dm_ref[...], dn_ref[...]], axis=0)
        wrow = lax.broadcasted_iota(jnp.int32, (FFN_ROWS + 2 * HALO, 1), 0)
        edge = jnp.where(wrow < HALO, has_prev, jnp.where(wrow >= HALO + FFN_ROWS, has_next, 1.0))

        dh = jnp.zeros((FFN_ROWS, d), F32)
        sums = []
        for p in range(N_UP_PAIRS):
            pg = window(pp_ref[0, p], pm_ref[0, p], pn_ref[0, p])
            pv = window(pp_ref[1, p], pm_ref[1, p], pn_ref[1, p])
            dact = lax.dot_general(dy_rows, wd_ref[p], NT, preferred_element_type=F32)
            dact = dact[PACKED_ROWS - HALO:PACKED_ROWS + FFN_ROWS + HALO] * edge
            a = _conv(pg, w_ref[0, p])
            v = _conv(pv, w_ref[1, p])
            sg = _sigmoid(a)
            asg = a * sg
            dzg, dwg = _conv_bwd(dact * v * (sg + asg - asg * sg), pg, w_ref[0, p], cur)
            dzv, dwv = _conv_bwd(dact * asg, pv, w_ref[1, p], cur)
            dgate = dzg[cur].astype(BF16)
            dval = dzv[cur].astype(BF16)
            dpre_ref[0, p] = dgate
            dpre_ref[1, p] = dval
            dh = dh + jnp.dot(dgate, wu_ref[p], preferred_element_type=F32)
            dh = dh + jnp.dot(dval, wu_ref[N_UP_PAIRS + p], preferred_element_type=F32)
            sums.append(dwg + dwv)

        xv = x_ref[...]
        r = lax.rsqrt(jnp.mean(xv * xv, axis=-1, keepdims=True) + EPS)
        xh = xv * r
        gd = dh * g_ref[...]
        dx = r * (gd - xh * jnp.mean(gd * xh, axis=-1, keepdims=True)) + dres_ref[...]
        dx_ref[...] = dx
        dxb_ref[...] = dx.astype(BF16)
        part = jnp.sum(dh * xh, axis=0, keepdims=True)

        @pl.when(i == 0)
        def _():
            dg_ref[...] = part
            for p in range(N_UP_PAIRS):
                for k in range(6):
                    dw_ref[k // 3, p, pl.ds(k % 3, 1), :] = sums[p][k]

        @pl.when(i > 0)
        def _():
            dg_ref[...] += part
            for p in range(N_UP_PAIRS):
                for k in range(6):
                    dw_ref[k // 3, p, pl.ds(k % 3, 1), :] += sums[p][k]

    def rows4(n):
        return lambda fn: pl.BlockSpec((2, N_UP_PAIRS, n, UP_CHUNK), lambda i: (0, 0, fn(i), 0))

    def rows2(n):
        return lambda fn: pl.BlockSpec((n, d), lambda i: (fn(i), 0))

    prev_tile = lambda i: jnp.maximum(i * tiles_per_halo - 1, 0)
    next_tile = lambda i: jnp.minimum((i + 1) * tiles_per_halo, last_halo)
    row = pl.BlockSpec((FFN_ROWS, d), lambda i: (i, 0))
    vec = pl.BlockSpec((1, d), lambda i: (0, 0))
    wspec = pl.BlockSpec((2, N_UP_PAIRS, 3, UP_CHUNK), lambda i: (0, 0, 0, 0))
    return pl.pallas_call(
        body, name=name, grid=(t // FFN_ROWS,),
        in_specs=[rows4(FFN_ROWS)(lambda i: i), rows4(PACKED_ROWS)(prev_tile), rows4(PACKED_ROWS)(next_tile),
                  rows2(FFN_ROWS)(lambda i: i), rows2(PACKED_ROWS)(prev_tile), rows2(PACKED_ROWS)(next_tile),
                  wspec, pl.BlockSpec((N_UP_PAIRS, UP_CHUNK, d), lambda i: (0, 0, 0)),
                  pl.BlockSpec((N_DEV, UP_CHUNK, d), lambda i: (0, 0, 0)), row, vec, row],
        out_specs=[rows4(FFN_ROWS)(lambda i: i), wspec, row, row, vec],
        out_shape=[jax.ShapeDtypeStruct(pre.shape, BF16), jax.ShapeDtypeStruct(fcw.shape, F32),
                   jax.ShapeDtypeStruct((t, d), F32), jax.ShapeDtypeStruct((t, d), BF16),
                   jax.ShapeDtypeStruct((1, d), F32)],
        compiler_params=_params(("arbitrary",)),
    )(pre, pre, pre, dy, dy, dy, fcw, wdown, wup, x, g, dres)


def _adamw(lands, w, m, v, row_tile, name, after=()):
    nl = len(lands)
    _, nr, ncol = lands[0].shape
    c1 = 1.0 - ADAM_B1 ** ADAM_STEP
    c2 = 1.0 - ADAM_B2 ** ADAM_STEP

    def body(*refs):
        land_refs = refs[:nl]
        w_ref, m_ref, v_ref = refs[nl:nl + 3]
        g_ref, d_ref, mo_ref, vo_ref = refs[nl + 3 + len(after):]
        for l in range(nl):
            @pl.when(pl.program_id(0) == l)
            def _(l=l):
                g = land_refs[l][0].astype(F32)
                for j in range(1, N_DEV):
                    g = g + land_refs[l][j].astype(F32)
                g_ref[...] = g

        g = g_ref[...]
        m2 = ADAM_B1 * m_ref[...] + (1.0 - ADAM_B1) * g
        v2 = ADAM_B2 * v_ref[...] + (1.0 - ADAM_B2) * (g * g)
        mo_ref[...] = m2
        vo_ref[...] = v2
        d_ref[...] = -ADAM_LR * ((m2 / c1) / (jnp.sqrt(v2 / c2) + ADAM_EPS) + ADAM_WD * w_ref[...])

    def land_spec(l):
        return pl.BlockSpec((N_DEV, row_tile, ncol), lambda k, i: (0, jnp.where(k == l, i, 0), 0))

    tile = pl.BlockSpec((None, row_tile, ncol), lambda k, i: (k, i, 0))
    return pl.pallas_call(
        body, name=name, grid=(nl, nr // row_tile),
        in_specs=[land_spec(l) for l in range(nl)] + [tile, tile, tile] + [pl.BlockSpec(memory_space=pl.ANY)] * len(after),
        out_specs=[tile] * 4,
        out_shape=[jax.ShapeDtypeStruct(w.shape, F32)] * 4,
        compiler_params=_params(("arbitrary", "arbitrary")),
    )(*lands, w, m, v, *after)


class _Item:
    def __init__(self, src, chunked, land_cols=False):
        self.src, self.chunked, self.land_cols = src, chunked, land_cols
        if chunked == "cols":
            block = (src.shape[0], src.shape[1] // N_DEV)
        else:
            block = src.shape[1:] if chunked else src.shape
        self.width = block[-1]
        self.land_shape = (block[0], N_DEV * block[1]) if land_cols else (N_DEV,) + block

    def _cols(self, first, count=1):
        return pl.ds(pl.multiple_of(first * self.width, LANES), count * self.width)

    def part(self, src_ref, j):
        if self.chunked == "cols":
            return src_ref.at[:, self._cols(j)]
        return src_ref.at[j] if self.chunked else src_ref

    def slot(self, land_ref, s):
        return land_ref.at[:, self._cols(s)] if self.land_cols else land_ref.at[s]


def _mesh_place():
    x, y, c = lax.axis_index("x"), lax.axis_index("y"), lax.axis_index("c")
    return x, y, c, 4 * x + 2 * y + c


def _flipped(x, y, c, k):
    px = 1 - x if k & 4 else x
    py = 1 - y if k & 2 else y
    pc = 1 - c if k & 1 else c
    return (px, py, pc), 4 * px + 2 * py + pc


PEER_ORDER = (2, 4, 6, 3, 5, 7, 1)


def _exchange(items, name):
    n = len(items)

    def body(*refs):
        srcs, lands = refs[:n], refs[n:2 * n]
        send, recv, local = refs[2 * n:]
        x, y, c, me = _mesh_place()

        def copy(i, k, chunk, slot, dev):
            return pltpu.make_async_remote_copy(
                src_ref=items[i].part(srcs[i], chunk), dst_ref=items[i].slot(lands[i], slot),
                send_sem=send.at[i, k - 1], recv_sem=recv.at[i, k - 1], device_id=dev, device_id_type=MESH)

        own = [pltpu.make_async_copy(items[i].part(srcs[i], me), items[i].slot(lands[i], me), local.at[i])
               for i in range(n)]
        for k in PEER_ORDER:
            dev, idx = _flipped(x, y, c, k)
            for i in range(n):
                copy(i, k, idx, me, dev).start()
        for cp in own:
            cp.start()
        for k in PEER_ORDER:
            dev, idx = _flipped(x, y, c, k)
            for i in range(n):
                copy(i, k, me, idx, dev).wait_recv()
        for k in PEER_ORDER:
            dev, idx = _flipped(x, y, c, k)
            for i in range(n):
                copy(i, k, idx, me, dev).wait_send()
        for cp in own:
            cp.wait()

    hbm = pl.BlockSpec(memory_space=pl.ANY)
    return pl.pallas_call(
        body, name=name,
        in_specs=[hbm] * n, out_specs=[hbm] * n,
        out_shape=[jax.ShapeDtypeStruct(it.land_shape, it.src.dtype) for it in items],
        scratch_shapes=[pltpu.SemaphoreType.DMA((n, N_DEV - 1)), pltpu.SemaphoreType.DMA((n, N_DEV - 1)),
                        pltpu.SemaphoreType.DMA((n,))],
        compiler_params=pltpu.CompilerParams(has_side_effects=True),
    )(*[it.src for it in items])


SAME_CORE = (2, 4, 6)


def _sequencer_gather(items, name, collective_id):
    n = len(items)

    def body(*refs):
        srcs, lands = refs[:n], refs[n:2 * n]
        send, recv, local = refs[2 * n:]
        x, y, c, me = _mesh_place()
        sibling, _ = _flipped(x, y, c, 1)
        barrier = pltpu.get_barrier_semaphore()
        for k in SAME_CORE + (1,):
            pl.semaphore_signal(barrier, inc=1, device_id=_flipped(x, y, c, k)[0], device_id_type=MESH)
        pl.semaphore_wait(barrier, len(SAME_CORE) + 1)

        def copy(i, q, src, slot, dev):
            return pltpu.make_async_remote_copy(
                src_ref=src, dst_ref=items[i].slot(lands[i], slot),
                send_sem=send.at[i, q - 1], recv_sem=recv.at[i, q - 1], device_id=dev, device_id_type=MESH)

        own = [pltpu.make_async_copy(srcs[i], items[i].slot(lands[i], me), local.at[i]) for i in range(n)]
        for cp in own:
            cp.start()
        for k in SAME_CORE + (1,):
            for i in range(n):
                copy(i, k, srcs[i], me, _flipped(x, y, c, k)[0]).start()
        for k in SAME_CORE:
            dev, idx = _flipped(x, y, c, k)
            for i in range(n):
                copy(i, k, srcs[i], idx, dev).wait_recv()
            for i in range(n):
                copy(i, k + 1, items[i].slot(lands[i], idx), idx, sibling).start()
        for k in (1,) + tuple(k + 1 for k in SAME_CORE):
            _, idx = _flipped(x, y, c, k)
            for i in range(n):
                copy(i, k, srcs[i], idx, sibling).wait_recv()
        for k in range(1, N_DEV):
            for i in range(n):
                copy(i, k, srcs[i], me, sibling).wait_send()
        for cp in own:
            cp.wait()

    return pl.kernel(
        body, name=name,
        out_type=[jax.ShapeDtypeStruct(it.land_shape, it.src.dtype) for it in items],
        mesh=plsc.ScalarSubcoreMesh(axis_name="sequencer", num_cores=1),
        scratch_types=[pltpu.SemaphoreType.DMA((n, N_DEV - 1)), pltpu.SemaphoreType.DMA((n, N_DEV - 1)),
                       pltpu.SemaphoreType.DMA((n,))],
        compiler_params=pltpu.CompilerParams(collective_id=collective_id),
    )(*[it.src for it in items])


def _sequencer_exchange(items, name, collective_id):
    n = len(items)

    def body(*refs):
        srcs, lands = refs[:n], refs[n:2 * n]
        send, recv, local = refs[2 * n:]
        x, y, c, me = _mesh_place()
        barrier = pltpu.get_barrier_semaphore()
        for k in PEER_ORDER:
            pl.semaphore_signal(barrier, inc=1, device_id=_flipped(x, y, c, k)[0], device_id_type=MESH)
        pl.semaphore_wait(barrier, N_DEV - 1)

        def copy(i, k, chunk, slot, dev):
            return pltpu.make_async_remote_copy(
                src_ref=items[i].part(srcs[i], chunk), dst_ref=items[i].slot(lands[i], slot),
                send_sem=send.at[i, k - 1], recv_sem=recv.at[i, k - 1], device_id=dev, device_id_type=MESH)

        own = [pltpu.make_async_copy(items[i].part(srcs[i], me), items[i].slot(lands[i], me), local.at[i])
               for i in range(n)]
        for cp in own:
            cp.start()
        for k in PEER_ORDER:
            dev, idx = _flipped(x, y, c, k)
            for i in range(n):
                copy(i, k, idx, me, dev).start()
        for k in PEER_ORDER:
            dev, idx = _flipped(x, y, c, k)
            for i in range(n):
                copy(i, k, me, idx, dev).wait_recv()
        for k in PEER_ORDER:
            dev, idx = _flipped(x, y, c, k)
            for i in range(n):
                copy(i, k, idx, me, dev).wait_send()
        for cp in own:
            cp.wait()

    return pl.kernel(
        body, name=name,
        out_type=[jax.ShapeDtypeStruct(it.land_shape, it.src.dtype) for it in items],
        mesh=plsc.ScalarSubcoreMesh(axis_name="sequencer", num_cores=1),
        scratch_types=[pltpu.SemaphoreType.DMA((n, N_DEV - 1)), pltpu.SemaphoreType.DMA((n, N_DEV - 1)),
                       pltpu.SemaphoreType.DMA((n,))],
        compiler_params=pltpu.CompilerParams(collective_id=collective_id),
    )(*[it.src for it in items])


TM = 2048
TM_ACC = 512
TN_IN = 768


def kernel(x, norm1_g, w_in, mix_conv_w, attn_out_g, conv_out_g, w_out, norm2_g, ffn_up, ffn_conv_w, ffn_down, final_norm_g, loss_target, m_norm1_g, m_w_in, m_mix_conv_w, m_attn_out_g, m_conv_out_g, m_w_out, m_norm2_g, m_ffn_up, m_ffn_conv_w, m_ffn_down, m_final_norm_g, v_norm1_g, v_w_in, v_mix_conv_w, v_attn_out_g, v_conv_out_g, v_w_out, v_norm2_g, v_ffn_up, v_ffn_conv_w, v_ffn_down, v_final_norm_g):
    nbatch, seq, d = x.shape
    t = nbatch * seq
    nt, nta = t // TM, t // TM_ACC
    out_rows = D_MODEL // N_DEV
    down_rows = D_FF // N_DEV
    xf = x.reshape(t, d)
    target = loss_target.reshape(t, d)

    cw_local = jnp.concatenate([ffn_conv_w, mix_conv_w], axis=-1)
    cast = lambda w: _Item(w.astype(BF16), False)
    cast_in = lambda w: _Item(w.astype(BF16), False, land_cols=True)
    cw_all, win0 = _sequencer_gather([_Item(cw_local, False), cast_in(w_in[0])], "gather_a", 0)
    up_t, m_up_t, v_up_t = (jnp.swapaxes(a, 1, 2) for a in (ffn_up, m_ffn_up, v_ffn_up))
    wout0, wup0 = _sequencer_gather([cast(w_out[0]), cast(up_t[0])], "gather_b", 1)
    (wdown0,) = _sequencer_gather([cast(ffn_down[0])], "gather_c", 2)
    win1, wout1 = _sequencer_gather([cast_in(w_in[1]), cast(w_out[1])], "gather_d", 3)
    wup1, wdown1 = _sequencer_gather([cast(up_t[1]), cast(ffn_down[1])], "gather_e", 7)
    win, wup = [win0, win1], [wup0, wup1]
    wout = [w.reshape(D_MODEL, D_MODEL) for w in (wout0, wout1)]
    wdown = [w.reshape(N_UP_PAIRS, UP_CHUNK, D_MODEL) for w in (wdown0, wdown1)]
    fcw = [cw_all[:, k, :, :UP_CHUNK].reshape(2, N_UP_PAIRS, 3, UP_CHUNK) for k in range(DEPTH)]
    mcw = [cw_all[:, k, :, UP_CHUNK:].transpose(1, 0, 2).reshape(3, D_CONV) for k in range(DEPTH)]

    full = lambda i, j, k: (0, 0)

    saved = []
    xin = xf
    h1 = _rms_fwd(xin, norm1_g[0][None], "rms1_fwd_0")
    rows_of = lambda width: pl.BlockSpec((TM_ACC, width), lambda i: (i, 0))
    whole = lambda *shape: pl.BlockSpec(shape, lambda i: (0,) * len(shape))
    for l in range(DEPTH):
        proj = _matmul(
            h1, win[l], grid=(nt, D_IN // TN_IN, 1), dims=NN, name=f"proj_{l}",
            a_spec=pl.BlockSpec((TM, D_MODEL), lambda i, j, k: (i, 0)),
            b_spec=pl.BlockSpec((D_MODEL, TN_IN), lambda i, j, k: (0, j)),
            o_spec=pl.BlockSpec((TM, TN_IN), lambda i, j, k: (i, j)), o_shape=(t, D_IN), o_dtype=F32)
        o, lse, cat = _attn_fwd(proj, attn_out_g[l][None], nbatch, seq)
        cat = _convmix_fwd(proj, cat, mcw[l], conv_out_g[l][None], nbatch, seq)
        xmid, h2 = _matmul_norm(cat, wout[l], xin, norm2_g[l][None], dims=NN, name=f"mix_out_{l}",
                                a_spec=rows_of(D_MODEL), b_spec=whole(D_MODEL, D_MODEL))
        if l + 1 < DEPTH:
            xout, h_next, act, pre = _ffn_fwd(
                h2, wup[l], fcw[l], wdown[l], xmid, norm1_g[l + 1][None], seq, f"ffn_fwd_{l}")
        else:
            h_next = None
            xout, act, pre = _ffn_fwd(h2, wup[l], fcw[l], wdown[l], xmid, None, seq, f"ffn_fwd_{l}")
        saved.append((xin, h1, proj, o, lse, cat, xmid, h2, pre, act))
        xin, h1 = xout, h_next

    loss_part, dx, dxb, dgf = _loss_head(xin, final_norm_g[None], target, "loss_head")

    dg1, dg2, dga, dgc = [None] * DEPTH, [None] * DEPTH, [None] * DEPTH, [None] * DEPTH
    for l in reversed(range(DEPTH)):
        xin, h1, proj, o, lse, cat, xmid, h2, pre, act = saved[l]
        g_down = _matmul(
            act, dxb, grid=(N_UP_PAIRS, 1, 1), dims=TN, name=f"g_down_{l}",
            a_spec=pl.BlockSpec((None, t, UP_CHUNK), lambda i, j, k: (i, 0, 0)),
            b_spec=pl.BlockSpec((t, D_MODEL), full),
            o_spec=pl.BlockSpec((None, UP_CHUNK, D_MODEL), lambda i, j, k: (i, 0, 0)),
            o_shape=(N_UP_PAIRS, UP_CHUNK, D_MODEL), o_dtype=BF16).reshape(N_DEV, down_rows, D_MODEL)
        d_pre, d_fcw, dxm, dxmb, dg2[l] = _ffn_up_bwd(
            pre, dxb, fcw[l], wdown[l], wup[l], xmid, norm2_g[l][None], dx, seq, f"ffn_bwd_{l}")
        d_pre = d_pre.reshape(N_DEV, t, UP_CHUNK)
        g_up = _matmul(
            d_pre, h2, grid=(N_DEV, 1, 1), dims=TN, name=f"g_up_{l}",
            a_spec=pl.BlockSpec((None, t, UP_CHUNK), lambda i, j, k: (i, 0, 0)),
            b_spec=pl.BlockSpec((t, D_MODEL), full),
            o_spec=pl.BlockSpec((None, UP_CHUNK, D_MODEL), lambda i, j, k: (i, 0, 0)),
            o_shape=(N_DEV, UP_CHUNK, D_MODEL), o_dtype=BF16)
        g_out = _matmul(
            cat, dxmb, grid=(1, 1, nt), dims=TN, name=f"g_out_{l}",
            a_spec=pl.BlockSpec((TM, D_MODEL), lambda i, j, k: (k, 0)),
            b_spec=pl.BlockSpec((TM, D_MODEL), lambda i, j, k: (k, 0)),
            o_spec=pl.BlockSpec((D_MODEL, D_MODEL), full),
            o_shape=(D_MODEL, D_MODEL), o_dtype=BF16).reshape(N_DEV, out_rows, D_MODEL)
        if l == 0:
            land_out0, land_up0, land_down0 = _sequencer_exchange(
                [_Item(g_out, True), _Item(g_up, True), _Item(g_down, True)], "scatter_0a", 5)
        d_cat = _matmul(
            dxmb, wout[l], grid=(nta, 1, 1), dims=NT, name=f"d_cat_{l}",
            a_spec=pl.BlockSpec((TM_ACC, D_MODEL), lambda i, j, k: (i, 0)),
            b_spec=pl.BlockSpec((D_MODEL, D_MODEL), full),
            o_spec=pl.BlockSpec((TM_ACC, D_MODEL), lambda i, j, k: (i, 0)), o_shape=(t, D_MODEL), o_dtype=BF16)
        d_proj, dga[l] = _attn_bwd(proj, o, lse, d_cat, attn_out_g[l][None], nbatch, seq)
        d_proj, d_mcw, dgc[l] = _convmix_bwd(proj, d_cat, d_proj, mcw[l], conv_out_g[l][None], nbatch, seq)
        g_in = _matmul(
            h1, d_proj, grid=(1, D_IN // TN_IN, 1), dims=TN, name=f"g_in_{l}",
            a_spec=pl.BlockSpec((t, D_MODEL), full),
            b_spec=pl.BlockSpec((t, TN_IN), lambda i, j, k: (0, j)),
            o_spec=pl.BlockSpec((D_MODEL, TN_IN), lambda i, j, k: (0, j)),
            o_shape=(D_MODEL, D_IN), o_dtype=BF16)
        g_cw = jnp.concatenate(
            [d_fcw.reshape(N_DEV, 3, UP_CHUNK), d_mcw.reshape(3, N_DEV, D_CONV // N_DEV).transpose(1, 0, 2)], axis=-1)
        if l == 0:
            land_in0, land_cw0 = _sequencer_exchange([_Item(g_in, "cols"), _Item(g_cw, True)], "scatter_0b", 6)
        else:
            land_in1, land_out1, land_up1, land_down1, land_cw1 = _sequencer_exchange(
                [_Item(g_in, "cols"), _Item(g_out, True), _Item(g_up, True), _Item(g_down, True), _Item(g_cw, True)],
                "scatter_1", 4)
        dx, dxb, dg1[l] = _matmul_norm_bwd(
            d_proj, win[l], xin, norm1_g[l][None], dxm, dims=NT, name=f"d_h1_{l}",
            a_spec=rows_of(D_IN), b_spec=whole(D_MODEL, D_IN))

    def pack_small(n1, a, c, n2, f):
        return jnp.concatenate(
            [n1, n2, f[None], jnp.concatenate([a, c], axis=-1), jnp.zeros((1, D_MODEL), F32)], axis=0)[None]

    small = jnp.concatenate(
        [dg1[0], dg1[1], dg2[0], dg2[1], dgf,
         jnp.concatenate([dga[0], dgc[0]], axis=-1), jnp.concatenate([dga[1], dgc[1]], axis=-1),
         jnp.pad(loss_part, ((0, 0), (0, D_MODEL - LANES)))], axis=0)
    (land_small,) = _exchange([_Item(small, False)], "gather_gain_grads")
    res_small = _adamw(
        [land_small], pack_small(norm1_g, attn_out_g, conv_out_g, norm2_g, final_norm_g),
        pack_small(m_norm1_g, m_attn_out_g, m_conv_out_g, m_norm2_g, m_final_norm_g),
        pack_small(v_norm1_g, v_attn_out_g, v_conv_out_g, v_norm2_g, v_final_norm_g), SUBLANES, "adamw_gains")
    res_out = _adamw([land_out0, land_out1], w_out, m_w_out, v_w_out, out_rows, "adamw_w_out", after=[res_small[0]])
    res_up_t = _adamw([land_up0, land_up1], up_t, m_up_t, v_up_t, UP_CHUNK // 4, "adamw_ffn_up", after=[res_out[0]])
    res_up = [jnp.swapaxes(r, 1, 2) for r in res_up_t]
    res_down = _adamw([land_down0, land_down1], ffn_down, m_ffn_down, v_ffn_down, down_rows, "adamw_ffn_down",
                      after=[res_up_t[0]])
    res_in = _adamw([land_in0, land_in1], w_in, m_w_in, v_w_in, 256, "adamw_w_in", after=[res_down[0]])
    res_cw = _adamw(
        [land_cw0, land_cw1], cw_local, jnp.concatenate([m_ffn_conv_w, m_mix_conv_w], axis=-1),
        jnp.concatenate([v_ffn_conv_w, v_mix_conv_w], axis=-1), 3, "adamw_conv_w", after=[res_in[0]])

    loss = res_small[0][0, SUBLANES - 1, 0]

    def unpack(kind):
        s = res_small[kind][0]
        cwr = res_cw[kind]
        return (s[0:2], res_in[kind], cwr[..., UP_CHUNK:], s[5:7, :D_ATTN], s[5:7, D_ATTN:], res_out[kind],
                s[2:4], res_up[kind], cwr[..., :UP_CHUNK], res_down[kind], s[4])

    return (loss, dx.reshape(nbatch, seq, d), *unpack(0), *unpack(1), *unpack(2), *unpack(3))
```

```python
import math

import jax
import jax.numpy as jnp
from jax import lax
from jax.experimental import pallas as pl
from jax.experimental.pallas import tpu as pltpu
from jax.experimental.pallas import tpu_sc as plsc

F32 = jnp.float32
BF16 = jnp.bfloat16

D_MODEL = 1024
D_ATTN = 512
D_CONV = 512
HEAD_DIM = 64
N_HEADS = 8
D_FF = 2816
DEPTH = 2
D_IN = 3 * D_ATTN + 3 * D_CONV
EPS = 1e-6
DILATIONS = (1, 4, 16)
BAND = 128
N_DEV = 8
UP_CHUNK = 2 * D_FF // N_DEV
N_UP_PAIRS = N_DEV // 2
CW_PACK = UP_CHUNK + D_CONV // N_DEV
ADAM_LR = 0.001
ADAM_B1 = 0.9
ADAM_B2 = 0.999
ADAM_EPS = 1e-08
ADAM_WD = 0.01
ADAM_STEP = 10
LANES = 128
SUBLANES = 8
VMEM_LIMIT = 56 * 1024 * 1024

NEG = -1e30
MESH = pl.DeviceIdType.MESH


def _params(sem=None, vmem=VMEM_LIMIT):
    return pltpu.CompilerParams(dimension_semantics=sem, vmem_limit_bytes=vmem)


NN = (((1,), (0,)), ((), ()))
NT = (((1,), (1,)), ((), ()))
TN = (((0,), (0,)), ((), ()))
TN_PIECE = 1024


def _contract(a_ref, b_ref, dims):
    def dot(av, bv):
        return lax.dot_general(av.astype(BF16), bv.astype(BF16), dims, preferred_element_type=F32)

    if len(a_ref.shape) == 2:
        if dims == TN and a_ref.shape[0] > TN_PIECE:
            part = None
            for r0 in range(0, a_ref.shape[0], TN_PIECE):
                piece = dot(a_ref[pl.ds(r0, TN_PIECE), :], b_ref[pl.ds(r0, TN_PIECE), :])
                part = piece if part is None else part + piece
            return part
        return dot(a_ref[...], b_ref[...])
    part = dot(a_ref[0], b_ref[0])
    for c in range(1, a_ref.shape[0]):
        part = part + dot(a_ref[c], b_ref[c])
    return part


def _matmul(a, b, *, grid, a_spec, b_spec, o_spec, o_shape, o_dtype, dims, name, res=None, res_spec=None, after=()):
    nk = grid[2]
    o_block = tuple(s for s in o_spec.block_shape if s is not None)
    na = len(after)

    def body(*refs):
        refs = refs[:2 + (res is not None)] + refs[2 + (res is not None) + na:]
        if res is None:
            a_ref, b_ref, o_ref, *scr = refs
            r_ref = None
        else:
            a_ref, b_ref, r_ref, o_ref, *scr = refs
        part = _contract(a_ref, b_ref, dims)

        def finish(total):
            if r_ref is not None:
                total = total + r_ref[...]
            o_ref[...] = total.astype(o_dtype)

        if nk == 1:
            finish(part)
        else:
            acc = scr[0]
            k = pl.program_id(2)

            @pl.when(k == 0)
            def _():
                acc[...] = part

            @pl.when(k > 0)
            def _():
                acc[...] += part

            @pl.when(k == nk - 1)
            def _():
                finish(acc[...])

    in_specs = [a_spec, b_spec] + ([res_spec] if res is not None else []) + [pl.BlockSpec(memory_space=pl.ANY)] * na
    args = (a, b) + ((res,) if res is not None else ()) + tuple(after)
    return pl.pallas_call(
        body, name=name, grid=grid, in_specs=in_specs, out_specs=o_spec,
        out_shape=jax.ShapeDtypeStruct(o_shape, o_dtype),
        scratch_shapes=[pltpu.VMEM(o_block, F32)] if nk > 1 else [],
        compiler_params=_params(("parallel", "parallel", "arbitrary")),
    )(*args)


ROW_TILE = 512


def _rms_fwd(x, g, name):
    t, d = x.shape

    def body(x_ref, g_ref, h_ref):
        xv = x_ref[...]
        r = lax.rsqrt(jnp.mean(xv * xv, axis=-1, keepdims=True) + EPS)
        h_ref[...] = (xv * r * g_ref[...]).astype(BF16)

    return pl.pallas_call(
        body, name=name, grid=(t // ROW_TILE,),
        in_specs=[pl.BlockSpec((ROW_TILE, d), lambda i: (i, 0)), pl.BlockSpec((1, d), lambda i: (0, 0))],
        out_specs=pl.BlockSpec((ROW_TILE, d), lambda i: (i, 0)),
        out_shape=jax.ShapeDtypeStruct((t, d), BF16),
        compiler_params=_params(("parallel",)),
    )(x, g)


def _matmul_norm(a, b, res, g, *, a_spec, b_spec, dims, name):
    t, d = res.shape

    def body(a_ref, b_ref, r_ref, g_ref, x_ref, h_ref):
        xv = _contract(a_ref, b_ref, dims) + r_ref[...]
        x_ref[...] = xv
        h_ref[...] = (xv * lax.rsqrt(jnp.mean(xv * xv, axis=-1, keepdims=True) + EPS) * g_ref[...]).astype(BF16)

    row = pl.BlockSpec((TM_ACC, d), lambda i: (i, 0))
    return pl.pallas_call(
        body, name=name, grid=(t // TM_ACC,),
        in_specs=[a_spec, b_spec, row, pl.BlockSpec((1, d), lambda i: (0, 0))], out_specs=[row, row],
        out_shape=[jax.ShapeDtypeStruct((t, d), F32), jax.ShapeDtypeStruct((t, d), BF16)],
        compiler_params=_params(("parallel",)),
    )(a, b, res, g)


def _matmul_norm_bwd(a, b, x, g, dres, *, a_spec, b_spec, dims, name):
    t, d = x.shape

    def body(a_ref, b_ref, x_ref, g_ref, dres_ref, dx_ref, dxb_ref, dg_ref):
        dhv = _contract(a_ref, b_ref, dims)
        xv = x_ref[...]
        r = lax.rsqrt(jnp.mean(xv * xv, axis=-1, keepdims=True) + EPS)
        xh = xv * r
        gd = dhv * g_ref[...]
        dx = r * (gd - xh * jnp.mean(gd * xh, axis=-1, keepdims=True)) + dres_ref[...]
        dx_ref[...] = dx
        dxb_ref[...] = dx.astype(BF16)
        part = jnp.sum(dhv * xh, axis=0, keepdims=True)

        @pl.when(pl.program_id(0) == 0)
        def _():
            dg_ref[...] = part

        @pl.when(pl.program_id(0) > 0)
        def _():
            dg_ref[...] += part

    row = pl.BlockSpec((TM_ACC, d), lambda i: (i, 0))
    vec = pl.BlockSpec((1, d), lambda i: (0, 0))
    return pl.pallas_call(
        body, name=name, grid=(t // TM_ACC,),
        in_specs=[a_spec, b_spec, row, vec, row], out_specs=[row, row, vec],
        out_shape=[jax.ShapeDtypeStruct((t, d), F32), jax.ShapeDtypeStruct((t, d), BF16),
                   jax.ShapeDtypeStruct((1, d), F32)],
        compiler_params=_params(("arbitrary",)),
    )(a, b, x, g, dres)


def _loss_head(x, g, target, name):
    t, d = x.shape

    def body(x_ref, g_ref, t_ref, loss_ref, dx_ref, dxb_ref, dg_ref):
        xv = x_ref[...]
        r = lax.rsqrt(jnp.mean(xv * xv, axis=-1, keepdims=True) + EPS)
        xh = xv * r
        gv = g_ref[...]
        err = xh * gv - t_ref[...]
        loss = jnp.full((1, LANES), 0.5 / d, F32) * jnp.sum(err * err)
        dy = err * (1.0 / d)
        gd = dy * gv
        dx = r * (gd - xh * jnp.mean(gd * xh, axis=-1, keepdims=True))
        dx_ref[...] = dx
        dxb_ref[...] = dx.astype(BF16)
        part = jnp.sum(dy * xh, axis=0, keepdims=True)

        @pl.when(pl.program_id(0) == 0)
        def _():
            dg_ref[...] = part
            loss_ref[...] = loss

        @pl.when(pl.program_id(0) > 0)
        def _():
            dg_ref[...] += part
            loss_ref[...] += loss

    row = pl.BlockSpec((ROW_TILE, d), lambda i: (i, 0))
    vec = pl.BlockSpec((1, d), lambda i: (0, 0))
    return pl.pallas_call(
        body, name=name, grid=(t // ROW_TILE,),
        in_specs=[row, vec, row],
        out_specs=[pl.BlockSpec((1, LANES), lambda i: (0, 0)), row, row, vec],
        out_shape=[jax.ShapeDtypeStruct((1, LANES), F32), jax.ShapeDtypeStruct((t, d), F32),
                   jax.ShapeDtypeStruct((t, d), BF16), jax.ShapeDtypeStruct((1, d), F32)],
        compiler_params=_params(("arbitrary",)),
    )(x, g, target)


def _group_matrix(n):
    shift = int(math.log2(HEAD_DIM))
    r = lax.broadcasted_iota(jnp.int32, (n, n), 0) >> shift
    c = lax.broadcasted_iota(jnp.int32, (n, n), 1) >> shift
    return (r == c).astype(BF16)


def _group_sum(v, gmat):
    hi = v.astype(BF16)
    lo = (v - hi.astype(F32)).astype(BF16)

    def dot(p):
        return jnp.dot(p, gmat, preferred_element_type=F32)

    return dot(hi) + dot(lo)


def _shift_rows(ext, k):
    return pltpu.roll(ext, k % ext.shape[0], 0)


def _store_columns(stage, out_hbm, sems, row0, nrows, col_blocks):
    rows = pl.ds(pl.multiple_of(row0, SUBLANES * 2), nrows)
    copies = [
        pltpu.make_async_copy(stage.at[i], out_hbm.at[rows, pl.ds(pl.multiple_of(cb * LANES, LANES), LANES)], sems.at[i])
        for i, cb in enumerate(col_blocks)
    ]
    for cp in copies:
        cp.start()
    for cp in copies:
        cp.wait()


def _attn_consts(width):
    i = lax.broadcasted_iota(jnp.int32, (BAND, width), 0)
    j = lax.broadcasted_iota(jnp.int32, (BAND, width), 1)
    dist = (width - BAND) + i - j
    inwin = (dist >= 0) & (dist <= BAND)
    return dist.astype(F32), inwin, j


def _head_masks():
    lane = lax.broadcasted_iota(jnp.int32, (1, LANES), 1)
    return [(lane < HEAD_DIM).astype(F32), (lane >= HEAD_DIM).astype(F32)]


def _pair_bias(slope, dil):
    distf, inwin, _ = _attn_consts(2 * BAND)
    return jnp.concatenate([jnp.where(inwin, distf * (slope[hh] * (-float(dil))), NEG) for hh in range(2)], axis=0)


def _stack_heads(xv, hmask):
    return jnp.concatenate([xv * hmask[0], xv * hmask[1]], axis=0).astype(BF16)


FWD_UNROLL = 16
BWD_UNROLL = 16


def _unroll(trips, most):
    return max(u for u in range(1, most + 1) if trips % u == 0)


def _for_blocks(seq, dil, block, most):
    nb = seq // dil // BAND

    def residue(r, carry):
        base = r * nb
        block(pl.multiple_of(base * BAND, BAND), None)
        if nb > 1:
            def rest(n, c):
                block(pl.multiple_of((base + n) * BAND, BAND), pl.multiple_of((base + n - 1) * BAND, BAND))
                return c

            lax.fori_loop(1, nb, rest, 0, unroll=_unroll(nb - 1, most))
        return carry

    if dil == 1:
        residue(0, 0)
    else:
        lax.fori_loop(0, dil, residue, 0, unroll=_unroll(dil, max(1, most // nb)))


def _permute_in(src_ref, dst_ref, dil, seq):
    length = seq // dil
    for r in range(dil):
        dst_ref[pl.ds(r * length, length), :] = src_ref[pl.ds(r, length, stride=dil), :].astype(dst_ref.dtype)


def _slopes_table():
    slopes = 2.0 ** (-8.0 * jnp.arange(1, N_HEADS + 1, dtype=F32) / N_HEADS)
    return jnp.broadcast_to(slopes[:, None], (N_HEADS, 2 * BAND))


def _attn_fwd(proj, attn_g, nbatch, seq):
    t = nbatch * seq
    scale = HEAD_DIM ** -0.5

    def body(q_ref, k_ref, v_ref, g_ref, sl_ref, o_ref, lse_ref, cat_ref, pq, pk, pv, po, pm, pll, ao, am, al):
        hp = pl.program_id(1)
        hmask = _head_masks()
        slope = [sl_ref[pl.ds(2 * hp + hh, 1), :] for hh in range(2)]

        def run_branch(dil, qs, ks, vs, osink, msink, lsink):
            bias = _pair_bias(slope, dil)

            def block(row0, prow):
                cur = pl.ds(row0, BAND)
                q2 = _stack_heads(qs[cur, :] * scale, hmask)
                if prow is None:
                    kk, vv, bias_b = ks[cur, :], vs[cur, :], bias[:, BAND:]
                else:
                    prev = pl.ds(prow, BAND)
                    kk = jnp.concatenate([ks[prev, :], ks[cur, :]], axis=0)
                    vv = jnp.concatenate([vs[prev, :], vs[cur, :]], axis=0)
                    bias_b = bias
                s = lax.dot_general(q2, kk.astype(BF16), NT, preferred_element_type=F32) + bias_b
                m = jnp.max(s, axis=1, keepdims=True)
                p = jnp.exp(s - m)
                l = jnp.sum(p, axis=1, keepdims=True)
                pb = p.astype(BF16)
                o = jnp.dot(jnp.concatenate([pb[:BAND], pb[BAND:]], axis=1), _stack_heads(vv, hmask),
                            preferred_element_type=F32)
                osink[cur, :] = o
                msink[cur, :] = m[:BAND] * hmask[0] + m[BAND:] * hmask[1]
                lsink[cur, :] = l[:BAND] * hmask[0] + l[BAND:] * hmask[1]

            _for_blocks(seq, dil, block, FWD_UNROLL)

        run_branch(1, q_ref, k_ref, v_ref, ao, am, al)
        for dil in DILATIONS[1:]:
            length = seq // dil
            _permute_in(q_ref, pq, dil, seq)
            _permute_in(k_ref, pk, dil, seq)
            _permute_in(v_ref, pv, dil, seq)
            run_branch(dil, pq, pk, pv, po, pm, pll)
            for r in range(dil):
                nat = pl.ds(r, length, stride=dil)
                per = pl.ds(r * length, length)
                m0 = am[nat, :]
                mb = pm[per, :]
                mn = jnp.maximum(m0, mb)
                e0 = jnp.exp(m0 - mn)
                eb = jnp.exp(mb - mn)
                ao[nat, :] = ao[nat, :] * e0 + po[per, :] * eb
                al[nat, :] = al[nat, :] * e0 + pll[per, :] * eb
                am[nat, :] = mn

        gmat = _group_matrix(LANES)
        gv = g_ref[...]

        def fin(c, carry):
            rows = pl.ds(pl.multiple_of(c * 256, 256), 256)
            lv = al[rows, :]
            o = ao[rows, :] / lv
            o_ref[rows, :] = o
            lse_ref[rows, :] = am[rows, :] + jnp.log(lv)
            ms = _group_sum(o * o, gmat) * (1.0 / HEAD_DIM)
            cat_ref[rows, :] = (o * lax.rsqrt(ms + EPS) * gv).astype(BF16)
            return carry

        lax.fori_loop(0, seq // 256, fin, 0, unroll=True)

    nq = D_ATTN // LANES
    blk = lambda off: pl.BlockSpec((seq, LANES), lambda b, h: (b, h + off))
    scratch = [pltpu.VMEM((seq, LANES), F32) for _ in range(9)]
    return pl.pallas_call(
        body, name="attn_fwd", grid=(nbatch, nq),
        in_specs=[blk(0), blk(nq), blk(2 * nq), pl.BlockSpec((1, LANES), lambda b, h: (0, h)),
                  pl.BlockSpec((N_HEADS, 2 * BAND), lambda b, h: (0, 0))],
        out_specs=[blk(0), blk(0), blk(0)],
        out_shape=[jax.ShapeDtypeStruct((t, D_ATTN), F32), jax.ShapeDtypeStruct((t, D_ATTN), F32),
                   jax.ShapeDtypeStruct((t, D_MODEL), BF16)],
        scratch_shapes=scratch,
        compiler_params=_params(("parallel", "parallel")),
    )(proj, proj, proj, attn_g, _slopes_table())


def _attn_bwd(proj, o, lse, d_cat, attn_g, nbatch, seq):
    t = nbatch * seq
    scale = HEAD_DIM ** -0.5

    def body(q_ref, k_ref, v_ref, o_ref, lse_ref, dy_ref, g_ref, sl_ref, dproj_ref, dg_ref,
             do_n, dl_n, dq_n, dk_n, dv_n, pq, pk, pv, pdo, plse, pdl, pdq, pdk, pdv, stage, sems):
        hp = pl.program_id(0)
        hmask = _head_masks()
        slope = [sl_ref[pl.ds(2 * hp + hh, 1), :] for hh in range(2)]
        gmat = _group_matrix(LANES)
        gv = g_ref[...]

        def prep(c, dg):
            rows = pl.ds(pl.multiple_of(c * 256, 256), 256)
            ov = o_ref[rows, :]
            dyn = dy_ref[rows, :].astype(F32)
            r = lax.rsqrt(_group_sum(ov * ov, gmat) * (1.0 / HEAD_DIM) + EPS)
            gd = dyn * gv
            oh = ov * r
            do = r * (gd - oh * (_group_sum(gd * oh, gmat) * (1.0 / HEAD_DIM)))
            do_n[rows, :] = do
            dl_n[rows, :] = _group_sum(do * ov, gmat)
            return dg + jnp.sum(dyn * oh, axis=0, keepdims=True)

        dg = lax.fori_loop(0, seq // 256, prep, jnp.zeros((1, LANES), F32), unroll=True)

        @pl.when(pl.program_id(1) == 0)
        def _():
            dg_ref[...] = dg

        @pl.when(pl.program_id(1) > 0)
        def _():
            dg_ref[...] += dg

        def clear(*refs):
            def step(c, carry):
                rows = pl.ds(pl.multiple_of(c * 256, 256), 256)
                for ref in refs:
                    ref[rows, :] = jnp.zeros((256, LANES), F32)
                return carry

            lax.fori_loop(0, seq // 256, step, 0)

        clear(dq_n, dk_n, dv_n)

        def run_branch(dil, qs, ks, vs, dos, lses, dls, dqs, dks, dvs):
            bias = _pair_bias(slope, dil)

            def per_head(xv):
                return jnp.concatenate([xv[:, 0:1], xv[:, HEAD_DIM:HEAD_DIM + 1]], axis=0)

            def block(row0, prow):
                cur = pl.ds(row0, BAND)
                keys = cur if prow is None else pl.ds(prow, 2 * BAND)
                q2 = _stack_heads(qs[cur, :] * scale, hmask)
                do2 = _stack_heads(dos[cur, :], hmask)
                kk, vv = ks[keys, :], vs[keys, :]
                s = lax.dot_general(q2, kk.astype(BF16), NT, preferred_element_type=F32)
                s = s + (bias[:, BAND:] if prow is None else bias)
                p = jnp.exp(s - per_head(lses[cur, :]))
                dp = lax.dot_general(do2, vv.astype(BF16), NT, preferred_element_type=F32)
                ds = (p * (dp - per_head(dls[cur, :]))).astype(BF16)
                dqs[cur, :] += jnp.dot(jnp.concatenate([ds[:BAND], ds[BAND:]], axis=1), _stack_heads(kk, hmask),
                                       preferred_element_type=F32)
                dks[keys, :] += lax.dot_general(ds, q2, TN, preferred_element_type=F32)
                dvs[keys, :] += lax.dot_general(p.astype(BF16), do2, TN, preferred_element_type=F32)

            _for_blocks(seq, dil, block, BWD_UNROLL)

        run_branch(1, q_ref, k_ref, v_ref, do_n, lse_ref, dl_n, dq_n, dk_n, dv_n)
        for dil in DILATIONS[1:]:
            length = seq // dil
            for src, dst in ((q_ref, pq), (k_ref, pk), (v_ref, pv), (do_n, pdo), (lse_ref, plse), (dl_n, pdl)):
                _permute_in(src, dst, dil, seq)
            clear(pdq, pdk, pdv)
            run_branch(dil, pq, pk, pv, pdo, plse, pdl, pdq, pdk, pdv)
            for r in range(dil):
                nat = pl.ds(r, length, stride=dil)
                per = pl.ds(r * length, length)
                dq_n[nat, :] += pdq[per, :]
                dk_n[nat, :] += pdk[per, :]
                dv_n[nat, :] += pdv[per, :]

        def emit(c, carry):
            rows = pl.ds(pl.multiple_of(c * 256, 256), 256)
            stage[0, rows, :] = (dq_n[rows, :] * scale).astype(BF16)
            stage[1, rows, :] = dk_n[rows, :].astype(BF16)
            stage[2, rows, :] = dv_n[rows, :].astype(BF16)
            return carry

        lax.fori_loop(0, seq // 256, emit, 0)
        _store_columns(stage, dproj_ref, sems, pl.program_id(1) * seq, seq, [hp, nq + hp, 2 * nq + hp])

    nq = D_ATTN // LANES
    blk = lambda off: pl.BlockSpec((seq, LANES), lambda h, b: (b, h + off))
    vec = pl.BlockSpec((1, LANES), lambda h, b: (0, h))
    scratch = [pltpu.VMEM((seq, LANES), F32) for _ in range(14)]
    scratch += [pltpu.VMEM((3, seq, LANES), BF16), pltpu.SemaphoreType.DMA((3,))]
    d_proj, dg = pl.pallas_call(
        body, name="attn_bwd", grid=(nq, nbatch),
        in_specs=[blk(0), blk(nq), blk(2 * nq), blk(0), blk(0), blk(0), vec,
                  pl.BlockSpec((N_HEADS, 2 * BAND), lambda h, b: (0, 0))],
        out_specs=[pl.BlockSpec(memory_space=pl.ANY), vec],
        out_shape=[jax.ShapeDtypeStruct((t, D_IN), BF16), jax.ShapeDtypeStruct((1, D_ATTN), F32)],
        scratch_shapes=scratch,
        compiler_params=_params(("arbitrary", "arbitrary")),
    )(proj, proj, proj, o, lse, d_cat, attn_g, _slopes_table())
    return d_proj, dg


HALO = SUBLANES
PACKED_ROWS = 2 * SUBLANES


def _window(ref, c, rows, nchunks, after):
    row0 = pl.multiple_of(c * rows, rows)
    prev0 = pl.multiple_of(jnp.maximum(row0 - PACKED_ROWS, 0), PACKED_ROWS)
    before = ref[pl.ds(prev0, PACKED_ROWS), :].astype(F32)[PACKED_ROWS - HALO:] * (c > 0).astype(F32)
    parts = [before, ref[pl.ds(row0, rows), :].astype(F32)]
    if after:
        next0 = pl.multiple_of(jnp.minimum(row0 + rows, (nchunks - 1) * rows), PACKED_ROWS)
        parts.append(ref[pl.ds(next0, PACKED_ROWS), :].astype(F32)[:HALO] * (c < nchunks - 1).astype(F32))
    return jnp.concatenate(parts, axis=0)


def _behind(z):
    z1 = _shift_rows(z, 1)
    return z1, _shift_rows(z1, 1)


def _ahead(dy):
    d1 = _shift_rows(dy, -1)
    return d1, _shift_rows(d1, -1)


def _conv(z, w):
    z1, z2 = _behind(z)
    return w[0:1] * z2 + w[1:2] * z1 + w[2:3] * z


def _conv_bwd(dy, z, w, cur):
    d1, d2 = _ahead(dy)
    dz = w[2:3] * dy + w[1:2] * d1 + w[0:1] * d2
    return dz, [jnp.sum((d * z)[cur], axis=0, keepdims=True) for d in (d2, d1, dy)]


def _sigmoid(a):
    return 0.5 * jnp.tanh(0.5 * a) + 0.5


MIX_ROWS = 512
GATE_B_BLOCK = 3 * D_ATTN // LANES
GATE_C_BLOCK = GATE_B_BLOCK + D_CONV // LANES
U_BLOCK = GATE_C_BLOCK + D_CONV // LANES


def _convmix_fwd(proj, cat, mcw, conv_g, nbatch, seq):
    nchunks = seq // MIX_ROWS

    def body(gb_ref, gc_ref, u_ref, w_ref, g_ref, cat_in, cat_ref):
        del cat_in
        gmat = _group_matrix(LANES)
        w = w_ref[...]
        gv = g_ref[...]

        def step(c, carry):
            cur = pl.ds(pl.multiple_of(c * MIX_ROWS, MIX_ROWS), MIX_ROWS)
            z = _window(gc_ref, c, MIX_ROWS, nchunks, False) * _window(u_ref, c, MIX_ROWS, nchunks, False)
            y = gb_ref[cur, :] * _conv(z, w)[HALO:]
            ms = _group_sum(y * y, gmat) * (1.0 / HEAD_DIM)
            cat_ref[cur, :] = (y * lax.rsqrt(ms + EPS) * gv).astype(BF16)
            return carry

        lax.fori_loop(0, nchunks, step, 0, unroll=True)

    nc = D_CONV // LANES
    blk = lambda off: pl.BlockSpec((seq, LANES), lambda b, j: (b, j + off))
    return pl.pallas_call(
        body, name="convmix_fwd", grid=(nbatch, nc),
        in_specs=[blk(GATE_B_BLOCK), blk(GATE_C_BLOCK), blk(U_BLOCK),
                  pl.BlockSpec((3, LANES), lambda b, j: (0, j)), pl.BlockSpec((1, LANES), lambda b, j: (0, j)),
                  pl.BlockSpec(memory_space=pl.ANY)],
        out_specs=blk(D_ATTN // LANES),
        out_shape=jax.ShapeDtypeStruct(cat.shape, cat.dtype),
        input_output_aliases={5: 0},
        compiler_params=_params(("parallel", "parallel")),
    )(proj, proj, proj, mcw, conv_g, cat)


def _convmix_bwd(proj, d_cat, d_proj, mcw, conv_g, nbatch, seq):
    nchunks = seq // MIX_ROWS

    def body(gb_ref, gc_ref, u_ref, dy_ref, w_ref, g_ref, dproj_in, dproj_ref, dw_ref, dg_ref, stage, sems):
        del dproj_in
        cb = pl.program_id(0)
        b = pl.program_id(1)
        gmat = _group_matrix(LANES)
        w = w_ref[...]
        gv = g_ref[...]
        cur = slice(HALO, HALO + MIX_ROWS)

        def step(c, carry):
            rows = pl.ds(pl.multiple_of(c * MIX_ROWS, MIX_ROWS), MIX_ROWS)
            gb = _window(gb_ref, c, MIX_ROWS, nchunks, True)
            gc = _window(gc_ref, c, MIX_ROWS, nchunks, True)
            u = _window(u_ref, c, MIX_ROWS, nchunks, True)
            dyn = _window(dy_ref, c, MIX_ROWS, nchunks, True)
            z = gc * u
            conv = _conv(z, w)
            y = gb * conv
            r = lax.rsqrt(_group_sum(y * y, gmat) * (1.0 / HEAD_DIM) + EPS)
            yh = y * r
            gd = dyn * gv
            dy = r * (gd - yh * (_group_sum(gd * yh, gmat) * (1.0 / HEAD_DIM)))
            dz, dws = _conv_bwd(dy * gb, z, w, cur)
            stage[0, rows, :] = (dy * conv)[cur].astype(BF16)
            stage[1, rows, :] = (dz * u)[cur].astype(BF16)
            stage[2, rows, :] = (dz * gc)[cur].astype(BF16)
            dg = jnp.sum((dyn * yh)[cur], axis=0, keepdims=True)
            return tuple(a + d for a, d in zip(carry, dws + [dg]))

        zero = jnp.zeros((1, LANES), F32)
        dw0, dw1, dw2, dg = lax.fori_loop(0, nchunks, step, (zero, zero, zero, zero), unroll=True)

        @pl.when(b == 0)
        def _():
            dw_ref[0:1, :] = dw0
            dw_ref[1:2, :] = dw1
            dw_ref[2:3, :] = dw2
            dg_ref[...] = dg

        @pl.when(b > 0)
        def _():
            dw_ref[0:1, :] += dw0
            dw_ref[1:2, :] += dw1
            dw_ref[2:3, :] += dw2
            dg_ref[...] += dg

        _store_columns(stage, dproj_ref, sems, b * seq, seq, [GATE_B_BLOCK + cb, GATE_C_BLOCK + cb, U_BLOCK + cb])

    nc = D_CONV // LANES
    blk = lambda off: pl.BlockSpec((seq, LANES), lambda j, b: (b, j + off))
    return pl.pallas_call(
        body, name="convmix_bwd", grid=(nc, nbatch),
        in_specs=[blk(GATE_B_BLOCK), blk(GATE_C_BLOCK), blk(U_BLOCK), blk(D_ATTN // LANES),
                  pl.BlockSpec((3, LANES), lambda j, b: (0, j)), pl.BlockSpec((1, LANES), lambda j, b: (0, j)),
                  pl.BlockSpec(memory_space=pl.ANY)],
        out_specs=[pl.BlockSpec(memory_space=pl.ANY), pl.BlockSpec((3, LANES), lambda j, b: (0, j)),
                   pl.BlockSpec((1, LANES), lambda j, b: (0, j))],
        out_shape=[jax.ShapeDtypeStruct(d_proj.shape, d_proj.dtype), jax.ShapeDtypeStruct((3, D_CONV), F32),
                   jax.ShapeDtypeStruct((1, D_CONV), F32)],
        scratch_shapes=[pltpu.VMEM((3, seq, LANES), BF16), pltpu.SemaphoreType.DMA((3,))],
        input_output_aliases={6: 0},
        compiler_params=_params(("arbitrary", "arbitrary")),
    )(proj, proj, proj, d_cat, mcw, conv_g, d_proj)


FFN_ROWS = 256
FFN_FWD_ROWS = 256


def _ffn_fwd(h, wup, fcw, wdown, res, g, seq, name):
    t, d = res.shape
    tiles_per_seq = seq // FFN_FWD_ROWS

    def body(hm_ref, hp_ref, wu_ref, w_ref, wd_ref, r_ref, *rest):
        if g is None:
            x_ref, act_ref, pre_ref = rest
        else:
            g_ref, x_ref, h_ref, act_ref, pre_ref = rest
        inside = ((pl.program_id(0) % tiles_per_seq) > 0).astype(F32)
        wrow = lax.broadcasted_iota(jnp.int32, (FFN_FWD_ROWS + HALO, 1), 0)
        edge = jnp.where(wrow < HALO, inside, 1.0)
        rows = jnp.concatenate([hp_ref[...], hm_ref[...]], axis=0)

        def up(j, part, p):
            full = lax.dot_general(rows, wu_ref[j], NT, preferred_element_type=F32).astype(BF16)
            pre_ref[part, p] = full[PACKED_ROWS:]
            return full.astype(F32)[PACKED_ROWS - HALO:] * edge

        total = r_ref[...]
        for p in range(N_UP_PAIRS):
            a = _conv(up(p, 0, p), w_ref[0, p])[HALO:]
            v = _conv(up(N_UP_PAIRS + p, 1, p), w_ref[1, p])[HALO:]
            act = (a * _sigmoid(a) * v).astype(BF16)
            act_ref[p] = act
            total = total + jnp.dot(act, wd_ref[p], preferred_element_type=F32)
        x_ref[...] = total
        if g is not None:
            h_ref[...] = (total * lax.rsqrt(jnp.mean(total * total, axis=-1, keepdims=True) + EPS) * g_ref[...]).astype(BF16)

    row = pl.BlockSpec((FFN_FWD_ROWS, d), lambda i: (i, 0))
    tiles_per_halo = FFN_FWD_ROWS // PACKED_ROWS
    in_specs = [
        row, pl.BlockSpec((PACKED_ROWS, d), lambda i: (jnp.maximum(i * tiles_per_halo - 1, 0), 0)),
        pl.BlockSpec((N_DEV, UP_CHUNK, d), lambda i: (0, 0, 0)),
        pl.BlockSpec((2, N_UP_PAIRS, 3, UP_CHUNK), lambda i: (0, 0, 0, 0)),
        pl.BlockSpec((N_UP_PAIRS, UP_CHUNK, d), lambda i: (0, 0, 0)), row]
    out_specs = [row]
    out_shape = [jax.ShapeDtypeStruct((t, d), F32)]
    args = [h, h, wup, fcw, wdown, res]
    if g is not None:
        in_specs.append(pl.BlockSpec((1, d), lambda i: (0, 0)))
        out_specs.append(row)
        out_shape.append(jax.ShapeDtypeStruct((t, d), BF16))
        args.append(g)
    out_specs += [pl.BlockSpec((N_UP_PAIRS, FFN_FWD_ROWS, UP_CHUNK), lambda i: (0, i, 0)),
                  pl.BlockSpec((2, N_UP_PAIRS, FFN_FWD_ROWS, UP_CHUNK), lambda i: (0, 0, i, 0))]
    out_shape += [jax.ShapeDtypeStruct((N_UP_PAIRS, t, UP_CHUNK), BF16),
                  jax.ShapeDtypeStruct((2, N_UP_PAIRS, t, UP_CHUNK), BF16)]
    return pl.pallas_call(
        body, name=name, grid=(t // FFN_FWD_ROWS,), in_specs=in_specs, out_specs=out_specs, out_shape=out_shape,
        compiler_params=_params(("parallel",)),
    )(*args)


def _ffn_up_bwd(pre, dy, fcw, wdown, wup, x, g, dres, seq, name):
    t, d = x.shape
    tiles_per_seq = seq // FFN_ROWS
    tiles_per_halo = FFN_ROWS // PACKED_ROWS
    last_halo = t // PACKED_ROWS - 1

    def body(pm_ref, pp_ref, pn_ref, dm_ref, dp_ref, dn_ref, w_ref, wd_ref, wu_ref, x_ref, g_ref, dres_ref,
             dpre_ref, dw_ref, dx_ref, dxb_ref, dg_ref):
        i = pl.program_id(0)
        has_prev = ((i % tiles_per_seq) > 0).astype(F32)
        has_next = ((i % tiles_per_seq) < tiles_per_seq - 1).astype(F32)
        cur = slice(HALO, HALO + FFN_ROWS)

        def window(before, main, after):
            return jnp.concatenate([before.astype(F32)[PACKED_ROWS - HALO:] * has_prev, main.astype(F32),
                                    after.astype(F32)[:HALO] * has_next], axis=0)

        dy_rows = jnp.concatenate([dp_ref[...], dm_ref[...], dn_ref[...]], axis=0)
        wrow = lax.broadcasted_iota(jnp.int32, (FFN_ROWS + 2 * HALO, 1), 0)
        edge = jnp.where(wrow < HALO, has_prev, jnp.where(wrow >= HALO + FFN_ROWS, has_next, 1.0))

        dh = jnp.zeros((FFN_ROWS, d), F32)
        sums = []
        for p in range(N_UP_PAIRS):
            pg = window(pp_ref[0, p], pm_ref[0, p], pn_ref[0, p])
            pv = window(pp_ref[1, p], pm_ref[1, p], pn_ref[1, p])
            dact = lax.dot_general(dy_rows, wd_ref[p], NT, preferred_element_type=F32)
            dact = dact[PACKED_ROWS - HALO:PACKED_ROWS + FFN_ROWS + HALO] * edge
            a = _conv(pg, w_ref[0, p])
            v = _conv(pv, w_ref[1, p])
            sg = _sigmoid(a)
            asg = a * sg
            dzg, dwg = _conv_bwd(dact * v * (sg + asg - asg * sg), pg, w_ref[0, p], cur)
            dzv, dwv = _conv_bwd(dact * asg, pv, w_ref[1, p], cur)
            dgate = dzg[cur].astype(BF16)
            dval = dzv[cur].astype(BF16)
            dpre_ref[0, p] = dgate
            dpre_ref[1, p] = dval
            dh = dh + jnp.dot(dgate, wu_ref[p], preferred_element_type=F32)
            dh = dh + jnp.dot(dval, wu_ref[N_UP_PAIRS + p], preferred_element_type=F32)
            sums.append(dwg + dwv)

        xv = x_ref[...]
        r = lax.rsqrt(jnp.mean(xv * xv, axis=-1, keepdims=True) + EPS)
        xh = xv * r
        gd = dh * g_ref[...]
        dx = r * (gd - xh * jnp.mean(gd * xh, axis=-1, keepdims=True)) + dres_ref[...]
        dx_ref[...] = dx
        dxb_ref[...] = dx.astype(BF16)
        part = jnp.sum(dh * xh, axis=0, keepdims=True)

        @pl.when(i == 0)
        def _():
            dg_ref[...] = part
            for p in range(N_UP_PAIRS):
                for k in range(6):
                    dw_ref[k // 3, p, pl.ds(k % 3, 1), :] = sums[p][k]

        @pl.when(i > 0)
        def _():
            dg_ref[...] += part
            for p in range(N_UP_PAIRS):
                for k in range(6):
                    dw_ref[k // 3, p, pl.ds(k % 3, 1), :] += sums[p][k]

    def rows4(n):
        return lambda fn: pl.BlockSpec((2, N_UP_PAIRS, n, UP_CHUNK), lambda i: (0, 0, fn(i), 0))

    def rows2(n):
        return lambda fn: pl.BlockSpec((n, d), lambda i: (fn(i), 0))

    prev_tile = lambda i: jnp.maximum(i * tiles_per_halo - 1, 0)
    next_tile = lambda i: jnp.minimum((i + 1) * tiles_per_halo, last_halo)
    row = pl.BlockSpec((FFN_ROWS, d), lambda i: (i, 0))
    vec = pl.BlockSpec((1, d), lambda i: (0, 0))
    wspec = pl.BlockSpec((2, N_UP_PAIRS, 3, UP_CHUNK), lambda i: (0, 0, 0, 0))
    return pl.pallas_call(
        body, name=name, grid=(t // FFN_ROWS,),
        in_specs=[rows4(FFN_ROWS)(lambda i: i), rows4(PACKED_ROWS)(prev_tile), rows4(PACKED_ROWS)(next_tile),
                  rows2(FFN_ROWS)(lambda i: i), rows2(PACKED_ROWS)(prev_tile), rows2(PACKED_ROWS)(next_tile),
                  wspec, pl.BlockSpec((N_UP_PAIRS, UP_CHUNK, d), lambda i: (0, 0, 0)),
                  pl.BlockSpec((N_DEV, UP_CHUNK, d), lambda i: (0, 0, 0)), row, vec, row],
        out_specs=[rows4(FFN_ROWS)(lambda i: i), wspec, row, row, vec],
        out_shape=[jax.ShapeDtypeStruct(pre.shape, BF16), jax.ShapeDtypeStruct(fcw.shape, F32),
                   jax.ShapeDtypeStruct((t, d), F32), jax.ShapeDtypeStruct((t, d), BF16),
                   jax.ShapeDtypeStruct((1, d), F32)],
        compiler_params=_params(("arbitrary",)),
    )(pre, pre, pre, dy, dy, dy, fcw, wdown, wup, x, g, dres)


def _adamw(lands, w, m, v, row_tile, name, after=()):
    nl = len(lands)
    _, nr, ncol = lands[0].shape
    c1 = 1.0 - ADAM_B1 ** ADAM_STEP
    c2 = 1.0 - ADAM_B2 ** ADAM_STEP

    def body(*refs):
        land_refs = refs[:nl]
        w_ref, m_ref, v_ref = refs[nl:nl + 3]
        g_ref, d_ref, mo_ref, vo_ref = refs[nl + 3 + len(after):]
        for l in range(nl):
            @pl.when(pl.program_id(0) == l)
            def _(l=l):
                g = land_refs[l][0].astype(F32)
                for j in range(1, N_DEV):
                    g = g + land_refs[l][j].astype(F32)
                g_ref[...] = g

        g = g_ref[...]
        m2 = ADAM_B1 * m_ref[...] + (1.0 - ADAM_B1) * g
        v2 = ADAM_B2 * v_ref[...] + (1.0 - ADAM_B2) * (g * g)
        mo_ref[...] = m2
        vo_ref[...] = v2
        d_ref[...] = -ADAM_LR * ((m2 / c1) / (jnp.sqrt(v2 / c2) + ADAM_EPS) + ADAM_WD * w_ref[...])

    def land_spec(l):
        return pl.BlockSpec((N_DEV, row_tile, ncol), lambda k, i: (0, jnp.where(k == l, i, 0), 0))

    tile = pl.BlockSpec((None, row_tile, ncol), lambda k, i: (k, i, 0))
    return pl.pallas_call(
        body, name=name, grid=(nl, nr // row_tile),
        in_specs=[land_spec(l) for l in range(nl)] + [tile, tile, tile] + [pl.BlockSpec(memory_space=pl.ANY)] * len(after),
        out_specs=[tile] * 4,
        out_shape=[jax.ShapeDtypeStruct(w.shape, F32)] * 4,
        compiler_params=_params(("arbitrary", "arbitrary")),
    )(*lands, w, m, v, *after)


class _Item:
    def __init__(self, src, chunked, land_cols=False):
        self.src, self.chunked, self.land_cols = src, chunked, land_cols
        if chunked == "cols":
            block = (src.shape[0], src.shape[1] // N_DEV)
        else:
            block = src.shape[1:] if chunked else src.shape
        self.width = block[-1]
        self.land_shape = (block[0], N_DEV * block[1]) if land_cols else (N_DEV,) + block

    def _cols(self, first, count=1):
        return pl.ds(pl.multiple_of(first * self.width, LANES), count * self.width)

    def part(self, src_ref, j):
        if self.chunked == "cols":
            return src_ref.at[:, self._cols(j)]
        return src_ref.at[j] if self.chunked else src_ref

    def slot(self, land_ref, s):
        return land_ref.at[:, self._cols(s)] if self.land_cols else land_ref.at[s]


def _mesh_place():
    x, y, c = lax.axis_index("x"), lax.axis_index("y"), lax.axis_index("c")
    return x, y, c, 4 * x + 2 * y + c


def _flipped(x, y, c, k):
    px = 1 - x if k & 4 else x
    py = 1 - y if k & 2 else y
    pc = 1 - c if k & 1 else c
    return (px, py, pc), 4 * px + 2 * py + pc


PEER_ORDER = (2, 4, 6, 3, 5, 7, 1)


def _exchange(items, name):
    n = len(items)

    def body(*refs):
        srcs, lands = refs[:n], refs[n:2 * n]
        send, recv, local = refs[2 * n:]
        x, y, c, me = _mesh_place()

        def copy(i, k, chunk, slot, dev):
            return pltpu.make_async_remote_copy(
                src_ref=items[i].part(srcs[i], chunk), dst_ref=items[i].slot(lands[i], slot),
                send_sem=send.at[i, k - 1], recv_sem=recv.at[i, k - 1], device_id=dev, device_id_type=MESH)

        own = [pltpu.make_async_copy(items[i].part(srcs[i], me), items[i].slot(lands[i], me), local.at[i])
               for i in range(n)]
        for k in PEER_ORDER:
            dev, idx = _flipped(x, y, c, k)
            for i in range(n):
                copy(i, k, idx, me, dev).start()
        for cp in own:
            cp.start()
        for k in PEER_ORDER:
            dev, idx = _flipped(x, y, c, k)
            for i in range(n):
                copy(i, k, me, idx, dev).wait_recv()
        for k in PEER_ORDER:
            dev, idx = _flipped(x, y, c, k)
            for i in range(n):
                copy(i, k, idx, me, dev).wait_send()
        for cp in own:
            cp.wait()

    hbm = pl.BlockSpec(memory_space=pl.ANY)
    return pl.pallas_call(
        body, name=name,
        in_specs=[hbm] * n, out_specs=[hbm] * n,
        out_shape=[jax.ShapeDtypeStruct(it.land_shape, it.src.dtype) for it in items],
        scratch_shapes=[pltpu.SemaphoreType.DMA((n, N_DEV - 1)), pltpu.SemaphoreType.DMA((n, N_DEV - 1)),
                        pltpu.SemaphoreType.DMA((n,))],
        compiler_params=pltpu.CompilerParams(has_side_effects=True),
    )(*[it.src for it in items])


SAME_CORE = (2, 4, 6)


def _sequencer_gather(items, name, collective_id):
    n = len(items)

    def body(*refs):
        srcs, lands = refs[:n], refs[n:2 * n]
        send, recv, local = refs[2 * n:]
        x, y, c, me = _mesh_place()
        sibling, _ = _flipped(x, y, c, 1)
        barrier = pltpu.get_barrier_semaphore()
        for k in SAME_CORE + (1,):
            pl.semaphore_signal(barrier, inc=1, device_id=_flipped(x, y, c, k)[0], device_id_type=MESH)
        pl.semaphore_wait(barrier, len(SAME_CORE) + 1)

        def copy(i, q, src, slot, dev):
            return pltpu.make_async_remote_copy(
                src_ref=src, dst_ref=items[i].slot(lands[i], slot),
                send_sem=send.at[i, q - 1], recv_sem=recv.at[i, q - 1], device_id=dev, device_id_type=MESH)

        own = [pltpu.make_async_copy(srcs[i], items[i].slot(lands[i], me), local.at[i]) for i in range(n)]
        for cp in own:
            cp.start()
        for k in SAME_CORE + (1,):
            for i in range(n):
                copy(i, k, srcs[i], me, _flipped(x, y, c, k)[0]).start()
        for k in SAME_CORE:
            dev, idx = _flipped(x, y, c, k)
            for i in range(n):
                copy(i, k, srcs[i], idx, dev).wait_recv()
            for i in range(n):
                copy(i, k + 1, items[i].slot(lands[i], idx), idx, sibling).start()
        for k in (1,) + tuple(k + 1 for k in SAME_CORE):
            _, idx = _flipped(x, y, c, k)
            for i in range(n):
                copy(i, k, srcs[i], idx, sibling).wait_recv()
        for k in range(1, N_DEV):
            for i in range(n):
                copy(i, k, srcs[i], me, sibling).wait_send()
        for cp in own:
            cp.wait()

    return pl.kernel(
        body, name=name,
        out_type=[jax.ShapeDtypeStruct(it.land_shape, it.src.dtype) for it in items],
        mesh=plsc.ScalarSubcoreMesh(axis_name="sequencer", num_cores=1),
        scratch_types=[pltpu.SemaphoreType.DMA((n, N_DEV - 1)), pltpu.SemaphoreType.DMA((n, N_DEV - 1)),
                       pltpu.SemaphoreType.DMA((n,))],
        compiler_params=pltpu.CompilerParams(collective_id=collective_id),
    )(*[it.src for it in items])


def _sequencer_exchange(items, name, collective_id):
    n = len(items)

    def body(*refs):
        srcs, lands = refs[:n], refs[n:2 * n]
        send, recv, local = refs[2 * n:]
        x, y, c, me = _mesh_place()
        barrier = pltpu.get_barrier_semaphore()
        for k in PEER_ORDER:
            pl.semaphore_signal(barrier, inc=1, device_id=_flipped(x, y, c, k)[0], device_id_type=MESH)
        pl.semaphore_wait(barrier, N_DEV - 1)

        def copy(i, k, chunk, slot, dev):
            return pltpu.make_async_remote_copy(
                src_ref=items[i].part(srcs[i], chunk), dst_ref=items[i].slot(lands[i], slot),
                send_sem=send.at[i, k - 1], recv_sem=recv.at[i, k - 1], device_id=dev, device_id_type=MESH)

        own = [pltpu.make_async_copy(items[i].part(srcs[i], me), items[i].slot(lands[i], me), local.at[i])
               for i in range(n)]
        for cp in own:
            cp.start()
        for k in PEER_ORDER:
            dev, idx = _flipped(x, y, c, k)
            for i in range(n):
                copy(i, k, idx, me, dev).start()
        for k in PEER_ORDER:
            dev, idx = _flipped(x, y, c, k)
            for i in range(n):
                copy(i, k, me, idx, dev).wait_recv()
        for k in PEER_ORDER:
            dev, idx = _flipped(x, y, c, k)
            for i in range(n):
                copy(i, k, idx, me, dev).wait_send()
        for cp in own:
            cp.wait()

    return pl.kernel(
        body, name=name,
        out_type=[jax.ShapeDtypeStruct(it.land_shape, it.src.dtype) for it in items],
        mesh=plsc.ScalarSubcoreMesh(axis_name="sequencer", num_cores=1),
        scratch_types=[pltpu.SemaphoreType.DMA((n, N_DEV - 1)), pltpu.SemaphoreType.DMA((n, N_DEV - 1)),
                       pltpu.SemaphoreType.DMA((n,))],
        compiler_params=pltpu.CompilerParams(collective_id=collective_id),
    )(*[it.src for it in items])


TM = 2048
TM_ACC = 512
TN_IN = 768


def kernel(x, norm1_g, w_in, mix_conv_w, attn_out_g, conv_out_g, w_out, norm2_g, ffn_up, ffn_conv_w, ffn_down, final_norm_g, loss_target, m_norm1_g, m_w_in, m_mix_conv_w, m_attn_out_g, m_conv_out_g, m_w_out, m_norm2_g, m_ffn_up, m_ffn_conv_w, m_ffn_down, m_final_norm_g, v_norm1_g, v_w_in, v_mix_conv_w, v_attn_out_g, v_conv_out_g, v_w_out, v_norm2_g, v_ffn_up, v_ffn_conv_w, v_ffn_down, v_final_norm_g):
    nbatch, seq, d = x.shape
    t = nbatch * seq
    nt, nta = t // TM, t // TM_ACC
    out_rows = D_MODEL // N_DEV
    down_rows = D_FF // N_DEV
    xf = x.reshape(t, d)
    target = loss_target.reshape(t, d)

    cw_local = jnp.concatenate([ffn_conv_w, mix_conv_w], axis=-1)
    cast = lambda w: _Item(w.astype(BF16), False)
    cast_in = lambda w: _Item(w.astype(BF16), False, land_cols=True)
    cw_all, win0 = _sequencer_gather([_Item(cw_local, False), cast_in(w_in[0])], "gather_a", 0)
    up_t, m_up_t, v_up_t = (jnp.swapaxes(a, 1, 2) for a in (ffn_up, m_ffn_up, v_ffn_up))
    wout0, wup0 = _sequencer_gather([cast(w_out[0]), cast(up_t[0])], "gather_b", 1)
    (wdown0,) = _sequencer_gather([cast(ffn_down[0])], "gather_c", 2)
    win1, wout1 = _sequencer_gather([cast_in(w_in[1]), cast(w_out[1])], "gather_d", 3)
    wup1, wdown1 = _sequencer_gather([cast(up_t[1]), cast(ffn_down[1])], "gather_e", 7)
    win, wup = [win0, win1], [wup0, wup1]
    wout = [w.reshape(D_MODEL, D_MODEL) for w in (wout0, wout1)]
    wdown = [w.reshape(N_UP_PAIRS, UP_CHUNK, D_MODEL) for w in (wdown0, wdown1)]
    fcw = [cw_all[:, k, :, :UP_CHUNK].reshape(2, N_UP_PAIRS, 3, UP_CHUNK) for k in range(DEPTH)]
    mcw = [cw_all[:, k, :, UP_CHUNK:].transpose(1, 0, 2).reshape(3, D_CONV) for k in range(DEPTH)]

    full = lambda i, j, k: (0, 0)

    saved = []
    xin = xf
    h1 = _rms_fwd(xin, norm1_g[0][None], "rms1_fwd_0")
    rows_of = lambda width: pl.BlockSpec((TM_ACC, width), lambda i: (i, 0))
    whole = lambda *shape: pl.BlockSpec(shape, lambda i: (0,) * len(shape))
    for l in range(DEPTH):
        proj = _matmul(
            h1, win[l], grid=(nt, D_IN // TN_IN, 1), dims=NN, name=f"proj_{l}",
            a_spec=pl.BlockSpec((TM, D_MODEL), lambda i, j, k: (i, 0)),
            b_spec=pl.BlockSpec((D_MODEL, TN_IN), lambda i, j, k: (0, j)),
            o_spec=pl.BlockSpec((TM, TN_IN), lambda i, j, k: (i, j)), o_shape=(t, D_IN), o_dtype=F32)
        o, lse, cat = _attn_fwd(proj, attn_out_g[l][None], nbatch, seq)
        cat = _convmix_fwd(proj, cat, mcw[l], conv_out_g[l][None], nbatch, seq)
        xmid, h2 = _matmul_norm(cat, wout[l], xin, norm2_g[l][None], dims=NN, name=f"mix_out_{l}",
                                a_spec=rows_of(D_MODEL), b_spec=whole(D_MODEL, D_MODEL))
        if l + 1 < DEPTH:
            xout, h_next, act, pre = _ffn_fwd(
                h2, wup[l], fcw[l], wdown[l], xmid, norm1_g[l + 1][None], seq, f"ffn_fwd_{l}")
        else:
            h_next = None
            xout, act, pre = _ffn_fwd(h2, wup[l], fcw[l], wdown[l], xmid, None, seq, f"ffn_fwd_{l}")
        saved.append((xin, h1, proj, o, lse, cat, xmid, h2, pre, act))
        xin, h1 = xout, h_next

    loss_part, dx, dxb, dgf = _loss_head(xin, final_norm_g[None], target, "loss_head")

    dg1, dg2, dga, dgc = [None] * DEPTH, [None] * DEPTH, [None] * DEPTH, [None] * DEPTH
    for l in reversed(range(DEPTH)):
        xin, h1, proj, o, lse, cat, xmid, h2, pre, act = saved[l]
        g_down = _matmul(
            act, dxb, grid=(N_UP_PAIRS, 1, 1), dims=TN, name=f"g_down_{l}",
            a_spec=pl.BlockSpec((None, t, UP_CHUNK), lambda i, j, k: (i, 0, 0)),
            b_spec=pl.BlockSpec((t, D_MODEL), full),
            o_spec=pl.BlockSpec((None, UP_CHUNK, D_MODEL), lambda i, j, k: (i, 0, 0)),
            o_shape=(N_UP_PAIRS, UP_CHUNK, D_MODEL), o_dtype=BF16).reshape(N_DEV, down_rows, D_MODEL)
        d_pre, d_fcw, dxm, dxmb, dg2[l] = _ffn_up_bwd(
            pre, dxb, fcw[l], wdown[l], wup[l], xmid, norm2_g[l][None], dx, seq, f"ffn_bwd_{l}")
        d_pre = d_pre.reshape(N_DEV, t, UP_CHUNK)
        g_up = _matmul(
            d_pre, h2, grid=(N_DEV, 1, 1), dims=TN, name=f"g_up_{l}",
            a_spec=pl.BlockSpec((None, t, UP_CHUNK), lambda i, j, k: (i, 0, 0)),
            b_spec=pl.BlockSpec((t, D_MODEL), full),
            o_spec=pl.BlockSpec((None, UP_CHUNK, D_MODEL), lambda i, j, k: (i, 0, 0)),
            o_shape=(N_DEV, UP_CHUNK, D_MODEL), o_dtype=BF16)
        g_out = _matmul(
            cat, dxmb, grid=(1, 1, nt), dims=TN, name=f"g_out_{l}",
            a_spec=pl.BlockSpec((TM, D_MODEL), lambda i, j, k: (k, 0)),
            b_spec=pl.BlockSpec((TM, D_MODEL), lambda i, j, k: (k, 0)),
            o_spec=pl.BlockSpec((D_MODEL, D_MODEL), full),
            o_shape=(D_MODEL, D_MODEL), o_dtype=BF16).reshape(N_DEV, out_rows, D_MODEL)
        if l == 0:
            land_out0, land_up0, land_down0 = _sequencer_exchange(
                [_Item(g_out, True), _Item(g_up, True), _Item(g_down, True)], "scatter_0a", 5)
        d_cat = _matmul(
            dxmb, wout[l], grid=(nta, 1, 1), dims=NT, name=f"d_cat_{l}",
            a_spec=pl.BlockSpec((TM_ACC, D_MODEL), lambda i, j, k: (i, 0)),
            b_spec=pl.BlockSpec((D_MODEL, D_MODEL), full),
            o_spec=pl.BlockSpec((TM_ACC, D_MODEL), lambda i, j, k: (i, 0)), o_shape=(t, D_MODEL), o_dtype=BF16)
        d_proj, dga[l] = _attn_bwd(proj, o, lse, d_cat, attn_out_g[l][None], nbatch, seq)
        d_proj, d_mcw, dgc[l] = _convmix_bwd(proj, d_cat, d_proj, mcw[l], conv_out_g[l][None], nbatch, seq)
        g_in = _matmul(
            h1, d_proj, grid=(1, D_IN // TN_IN, 1), dims=TN, name=f"g_in_{l}",
            a_spec=pl.BlockSpec((t, D_MODEL), full),
            b_spec=pl.BlockSpec((t, TN_IN), lambda i, j, k: (0, j)),
            o_spec=pl.BlockSpec((D_MODEL, TN_IN), lambda i, j, k: (0, j)),
            o_shape=(D_MODEL, D_IN), o_dtype=BF16)
        g_cw = jnp.concatenate(
            [d_fcw.reshape(N_DEV, 3, UP_CHUNK), d_mcw.reshape(3, N_DEV, D_CONV // N_DEV).transpose(1, 0, 2)], axis=-1)
        if l == 0:
            land_in0, land_cw0 = _sequencer_exchange([_Item(g_in, "cols"), _Item(g_cw, True)], "scatter_0b", 6)
        else:
            land_in1, land_out1, land_up1, land_down1, land_cw1 = _sequencer_exchange(
                [_Item(g_in, "cols"), _Item(g_out, True), _Item(g_up, True), _Item(g_down, True), _Item(g_cw, True)],
                "scatter_1", 4)
        dx, dxb, dg1[l] = _matmul_norm_bwd(
            d_proj, win[l], xin, norm1_g[l][None], dxm, dims=NT, name=f"d_h1_{l}",
            a_spec=rows_of(D_IN), b_spec=whole(D_MODEL, D_IN))

    def pack_small(n1, a, c, n2, f):
        return jnp.concatenate(
            [n1, n2, f[None], jnp.concatenate([a, c], axis=-1), jnp.zeros((1, D_MODEL), F32)], axis=0)[None]

    small = jnp.concatenate(
        [dg1[0], dg1[1], dg2[0], dg2[1], dgf,
         jnp.concatenate([dga[0], dgc[0]], axis=-1), jnp.concatenate([dga[1], dgc[1]], axis=-1),
         jnp.pad(loss_part, ((0, 0), (0, D_MODEL - LANES)))], axis=0)
    (land_small,) = _exchange([_Item(small, False)], "gather_gain_grads")
    res_small = _adamw(
        [land_small], pack_small(norm1_g, attn_out_g, conv_out_g, norm2_g, final_norm_g),
        pack_small(m_norm1_g, m_attn_out_g, m_conv_out_g, m_norm2_g, m_final_norm_g),
        pack_small(v_norm1_g, v_attn_out_g, v_conv_out_g, v_norm2_g, v_final_norm_g), SUBLANES, "adamw_gains")
    res_out = _adamw([land_out0, land_out1], w_out, m_w_out, v_w_out, out_rows, "adamw_w_out", after=[res_small[0]])
    res_up_t = _adamw([land_up0, land_up1], up_t, m_up_t, v_up_t, UP_CHUNK // 4, "adamw_ffn_up", after=[res_out[0]])
    res_up = [jnp.swapaxes(r, 1, 2) for r in res_up_t]
    res_down = _adamw([land_down0, land_down1], ffn_down, m_ffn_down, v_ffn_down, down_rows, "adamw_ffn_down",
                      after=[res_up_t[0]])
    res_in = _adamw([land_in0, land_in1], w_in, m_w_in, v_w_in, 256, "adamw_w_in", after=[res_down[0]])
    res_cw = _adamw(
        [land_cw0, land_cw1], cw_local, jnp.concatenate([m_ffn_conv_w, m_mix_conv_w], axis=-1),
        jnp.concatenate([v_ffn_conv_w, v_mix_conv_w], axis=-1), 3, "adamw_conv_w", after=[res_in[0]])

    loss = res_small[0][0, SUBLANES - 1, 0]

    def unpack(kind):
        s = res_small[kind][0]
        cwr = res_cw[kind]
        return (s[0:2], res_in[kind], cwr[..., UP_CHUNK:], s[5:7, :D_ATTN], s[5:7, D_ATTN:], res_out[kind],
                s[2:4], res_up[kind], cwr[..., :UP_CHUNK], res_down[kind], s[4])

    return (loss, dx.reshape(nbatch, seq, d), *unpack(0), *unpack(1), *unpack(2), *unpack(3))
```
